```python
import jax, jax.numpy as jnp
from jax import lax
import numpy as np

D_MODEL = 1024
BATCH = 4
SEQ = 4096
DEPTH = 4
DEC_BATCH = 8
DEC_SEQ = 64
PAST_LEN = 2048

CHUNK = 64
WINDOW = 128
WIN_CHUNKS = WINDOW // CHUNK
HEAD_DIM = 64
N_Q_HEADS = 8
N_KV_HEADS = 2
Q_PER_KV = N_Q_HEADS // N_KV_HEADS
ATT_WIDTH = N_Q_HEADS * HEAD_DIM
KV_WIDTH = N_KV_HEADS * HEAD_DIM
ROPE_THETA = 10000.0
GM_GROUPS = 4
GM_CH = 128
GM_WIDTH = GM_GROUPS * GM_CH
GM_CHUNK = 128
D_MIX = ATT_WIDTH + GM_WIDTH
D_IN = ATT_WIDTH + 2 * KV_WIDTH + 2 * GM_WIDTH
N_EXPERTS = 16
N_EXPERT_GROUPS = 4
EXPERTS_PER_GROUP = N_EXPERTS // N_EXPERT_GROUPS
TOP_K = 2
D_EXPERT = 512
ALPHA = (2 * DEPTH) ** 0.25
BETA = (8 * DEPTH) ** -0.25
LN_EPS = 1e-5
NEG_INF = -1e30

kernel_name = "hymba_swa_sink_gmlp_grouped_moe_stream_step"


def layer_norm(x, g, b):
    xf = x.astype(jnp.float32)
    mu = xf.mean(-1, keepdims=True)
    var = jnp.square(xf - mu).mean(-1, keepdims=True)
    y = (xf - mu) * lax.rsqrt(var + LN_EPS) * g.astype(jnp.float32) + b.astype(jnp.float32)
    return y.astype(x.dtype)


def rms_norm(x, g):
    xf = x.astype(jnp.float32)
    y = xf * lax.rsqrt(jnp.square(xf).mean(-1, keepdims=True) + LN_EPS) * g.astype(jnp.float32)
    return y.astype(x.dtype)


def rope(x, pos):
    half = HEAD_DIM // 2
    inv = ROPE_THETA ** (-jnp.arange(half, dtype=jnp.float32) / half)
    ang = pos.astype(jnp.float32)[:, None] * inv[None, :]
    cos = jnp.cos(ang)[:, None, :]
    sin = jnp.sin(ang)[:, None, :]
    x1 = x[..., :half].astype(jnp.float32)
    x2 = x[..., half:].astype(jnp.float32)
    return jnp.concatenate([x1 * cos - x2 * sin, x2 * cos + x1 * sin], -1).astype(x.dtype)


def split_proj(x, w_in):
    h = jnp.einsum('bsd,de->bse', x, w_in)
    o1 = ATT_WIDTH
    o2 = o1 + KV_WIDTH
    o3 = o2 + KV_WIDTH
    o4 = o3 + GM_WIDTH
    B, S = x.shape[:2]
    q = h[..., :o1].reshape(B, S, N_Q_HEADS, HEAD_DIM)
    k = h[..., o1:o2].reshape(B, S, N_KV_HEADS, HEAD_DIM)
    v = h[..., o2:o3].reshape(B, S, N_KV_HEADS, HEAD_DIM)
    return q, k, v, h[..., o3:o4], h[..., o4:]


def sink_attention(q, k, v, sinks, mask):
    s = jnp.einsum('...qhgd,...khd->...hgqk', q, k).astype(jnp.float32) * (HEAD_DIM ** -0.5)
    if mask is not None:
        s = jnp.where(mask, s, NEG_INF)
    sink = jnp.broadcast_to(sinks.astype(jnp.float32).reshape(N_KV_HEADS, Q_PER_KV, 1, 1), s.shape[:-1] + (1,))
    p = jax.nn.softmax(jnp.concatenate([s, sink], -1), axis=-1)[..., :-1]
    return jnp.einsum('...hgqk,...khd->...qhgd', p.astype(v.dtype), v)


def window_attention_prompt(q, k, v, sinks):
    B, S = q.shape[:2]
    nc = S // CHUNK
    qb = q.reshape(B, nc, CHUNK, N_KV_HEADS, Q_PER_KV, HEAD_DIM)
    pad = ((0, 0), (WIN_CHUNKS * CHUNK, 0), (0, 0), (0, 0))
    kp = jnp.pad(k, pad).reshape(B, nc + WIN_CHUNKS, CHUNK, N_KV_HEADS, HEAD_DIM)
    vp = jnp.pad(v, pad).reshape(B, nc + WIN_CHUNKS, CHUNK, N_KV_HEADS, HEAD_DIM)
    kb = jnp.concatenate([kp[:, i:i + nc] for i in range(WIN_CHUNKS + 1)], axis=2)
    vb = jnp.concatenate([vp[:, i:i + nc] for i in range(WIN_CHUNKS + 1)], axis=2)
    blk = jnp.arange(nc)[:, None] + jnp.arange((WIN_CHUNKS + 1) * CHUNK)[None, :] // CHUNK - WIN_CHUNKS
    mask = (blk >= 0)[:, None, None, None, :]
    o = sink_attention(qb, kb, vb, sinks, mask)
    return o.reshape(B, S, ATT_WIDTH)


def window_attention_sample(q, k_new, v_new, k_cache, v_cache, sinks):
    B, T = q.shape[:2]
    qb = q.reshape(B, T, N_KV_HEADS, Q_PER_KV, HEAD_DIM)
    k_all = jnp.concatenate([k_cache, k_new], axis=1)
    v_all = jnp.concatenate([v_cache, v_new], axis=1)
    o = sink_attention(qb, k_all, v_all, sinks, None)
    return o.reshape(B, T, ATT_WIDTH)


def gmlp_prompt(u, gv, ln_g, ln_b, ws, bs):
    B, S = u.shape[:2]
    n = S // GM_CHUNK
    ug = jax.nn.gelu(u, approximate=False).reshape(B, n, GM_CHUNK, GM_GROUPS, GM_CH)
    vn = layer_norm(jax.nn.gelu(gv, approximate=False).reshape(B, n, GM_CHUNK, GM_GROUPS, GM_CH), ln_g, ln_b)
    s = jnp.einsum('gpq,bnqgc->bnpgc', jnp.tril(ws), vn) + bs.T[:, :, None]
    return (ug * s).reshape(B, S, GM_WIDTH)


def gmlp_sample(u, gv, ln_g, ln_b, ws, bs):
    B, T = u.shape[:2]
    ug = jax.nn.gelu(u, approximate=False).reshape(B, T, GM_GROUPS, GM_CH)
    vn = layer_norm(jax.nn.gelu(gv, approximate=False).reshape(B, T, GM_GROUPS, GM_CH), ln_g, ln_b)
    s = jnp.einsum('gpq,bqgc->bpgc', jnp.tril(ws)[:, :T, :T], vn) + bs[:, :T].T[:, :, None]
    return (ug * s).reshape(B, T, GM_WIDTH), vn.reshape(B, T, GM_WIDTH)


def grouped_moe(x, w_router, router_bias, wg, wu, wd):
    shp = x.shape
    xt = x.reshape(-1, D_MODEL)
    scores = jax.nn.sigmoid(jnp.einsum('td,de->te', xt.astype(jnp.float32), w_router.astype(jnp.float32)))
    sel = scores + router_bias.astype(jnp.float32)
    grp_score = lax.top_k(sel.reshape(-1, N_EXPERT_GROUPS, EXPERTS_PER_GROUP), 2)[0].sum(-1)
    best = jnp.argmax(grp_score, axis=-1)
    in_grp = (jnp.arange(N_EXPERTS) // EXPERTS_PER_GROUP)[None, :] == best[:, None]
    _, idx = lax.top_k(jnp.where(in_grp, sel, NEG_INF), TOP_K)
    w = jnp.take_along_axis(scores, idx, axis=-1)
    w = w / w.sum(-1, keepdims=True)
    combine = jnp.einsum('tk,tke->te', w, jax.nn.one_hot(idx, N_EXPERTS, dtype=jnp.float32)).astype(x.dtype)
    y = jnp.zeros_like(xt)
    for e in range(N_EXPERTS):
        h = jax.nn.silu(xt @ wg[e]) * (xt @ wu[e])
        y = y + combine[:, e:e + 1] * (h @ wd[e])
    return y.reshape(shp)


def setup_inputs(seed: int = 0) -> dict:
    key = jax.random.key(seed)
    ks = jax.random.split(key, 24)
    f32 = jnp.float32
    n = lambda k, s: jax.random.normal(k, s, f32)
    cache_len = min(WINDOW, PAST_LEN)
    col_scale = jnp.concatenate([
        jnp.ones((ATT_WIDTH + KV_WIDTH,), f32), jnp.full((KV_WIDTH,), BETA, f32),
        jnp.full((GM_WIDTH,), BETA, f32), jnp.ones((GM_WIDTH,), f32)])
    return {
        "x_prompt": n(ks[0], (BATCH, SEQ, D_MODEL)),
        "x_sample": n(ks[1], (DEC_BATCH, DEC_SEQ, D_MODEL)),
        "cache_k": n(ks[2], (DEPTH, DEC_BATCH, cache_len, N_KV_HEADS, HEAD_DIM)),
        "cache_v": n(ks[3], (DEPTH, DEC_BATCH, cache_len, N_KV_HEADS, HEAD_DIM)),
        "w_in": n(ks[4], (DEPTH, D_MODEL, D_IN)) * (D_MODEL ** -0.5) * col_scale,
        "sinks": 0.5 * n(ks[5], (DEPTH, N_Q_HEADS)),
        "gm_ln_g": 1.0 + 0.02 * n(ks[6], (DEPTH, GM_GROUPS, GM_CH)),
        "gm_ln_b": 0.02 * n(ks[7], (DEPTH, GM_GROUPS, GM_CH)),
        "gm_ws": n(ks[8], (DEPTH, GM_GROUPS, GM_CHUNK, GM_CHUNK)) * (0.5 * GM_CHUNK ** -0.5),
        "gm_bs": 1.0 + 0.1 * n(ks[9], (DEPTH, GM_GROUPS, GM_CHUNK)),
        "out_norm_a": 1.0 + 0.02 * n(ks[10], (DEPTH, ATT_WIDTH)),
        "out_norm_b": 1.0 + 0.02 * n(ks[11], (DEPTH, GM_WIDTH)),
        "w_o": n(ks[12], (DEPTH, D_MIX, D_MODEL)) * (D_MIX ** -0.5) * BETA,
        "ln1_g": 1.0 + 0.02 * n(ks[13], (DEPTH, D_MODEL)),
        "ln1_b": 0.02 * n(ks[14], (DEPTH, D_MODEL)),
        "w_router": n(ks[15], (D_MODEL, N_EXPERTS)) * (D_MODEL ** -0.5),
        "router_bias": 0.01 * n(ks[16], (N_EXPERTS,)),
        "w_gate": n(ks[17], (DEPTH, N_EXPERTS, D_MODEL, D_EXPERT)) * (D_MODEL ** -0.5),
        "w_up": n(ks[18], (DEPTH, N_EXPERTS, D_MODEL, D_EXPERT)) * (D_MODEL ** -0.5) * BETA,
        "w_down": n(ks[19], (DEPTH, N_EXPERTS, D_EXPERT, D_MODEL)) * (D_EXPERT ** -0.5) * BETA,
        "ln2_g": 1.0 + 0.02 * n(ks[20], (DEPTH, D_MODEL)),
        "ln2_b": 0.02 * n(ks[21], (DEPTH, D_MODEL)),
    }


def reference(x_prompt, x_sample, cache_k, cache_v, w_in, sinks, gm_ln_g, gm_ln_b, gm_ws, gm_bs,
              out_norm_a, out_norm_b, w_o, ln1_g, ln1_b, w_router, router_bias,
              w_gate, w_up, w_down, ln2_g, ln2_b):

    def layer_tail(x, l, att_o, gm_o):
        mixed = jnp.concatenate([rms_norm(att_o, out_norm_a[l]), rms_norm(gm_o, out_norm_b[l])], -1)
        x = layer_norm(ALPHA * x + jnp.einsum('bse,ed->bsd', mixed, w_o[l]), ln1_g[l], ln1_b[l])
        y = grouped_moe(x, w_router, router_bias, w_gate[l], w_up[l], w_down[l])
        return layer_norm(ALPHA * x + y, ln2_g[l], ln2_b[l])

    pos_p = jnp.arange(x_prompt.shape[1])
    xp = x_prompt
    kp_new, vp_new = [], []
    for l in range(DEPTH):
        q, k, v, u, gv = split_proj(xp, w_in[l])
        q = rope(q, pos_p)
        k = rope(k, pos_p)
        att_o = window_attention_prompt(q, k, v, sinks[l])
        gm_o = gmlp_prompt(u, gv, gm_ln_g[l], gm_ln_b[l], gm_ws[l], gm_bs[l])
        kp_new.append(k[:, -WINDOW:])
        vp_new.append(v[:, -WINDOW:])
        xp = layer_tail(xp, l, att_o, gm_o)

    pos_s = PAST_LEN + jnp.arange(x_sample.shape[1])
    xs = x_sample
    ks_new, vs_new, gms_new = [], [], []
    for l in range(DEPTH):
        q, k, v, u, gv = split_proj(xs, w_in[l])
        q = rope(q, pos_s)
        k = rope(k, pos_s)
        att_o = window_attention_sample(q, k, v, cache_k[l], cache_v[l], sinks[l])
        gm_o, gm_v = gmlp_sample(u, gv, gm_ln_g[l], gm_ln_b[l], gm_ws[l], gm_bs[l])
        ks_new.append(k)
        vs_new.append(v)
        gms_new.append(gm_v)
        xs = layer_tail(xs, l, att_o, gm_o)

    return (xp, xs, jnp.stack(kp_new), jnp.stack(vp_new), jnp.stack(ks_new), jnp.stack(vs_new), jnp.stack(gms_new))
```

```python
import functools

import jax
import jax.numpy as jnp
from jax import lax
from jax.experimental import pallas as pl
from jax.experimental.pallas import tpu as pltpu

D_MODEL = 1024
CHUNK = 64
WINDOW = 128
HEAD_DIM = 64
N_Q_HEADS = 8
N_KV_HEADS = 2
ATT_WIDTH = N_Q_HEADS * HEAD_DIM
KV_WIDTH = N_KV_HEADS * HEAD_DIM
ROPE_THETA = 10000.0
GM_GROUPS = 4
GM_CH = 128
GM_WIDTH = GM_GROUPS * GM_CH
GM_CHUNK = 128
D_MIX = ATT_WIDTH + GM_WIDTH
D_IN = ATT_WIDTH + 2 * KV_WIDTH + 2 * GM_WIDTH
N_EXPERTS = 16
N_EXPERT_GROUPS = 4
EXPERTS_PER_GROUP = N_EXPERTS // N_EXPERT_GROUPS
TOP_K = 2
D_EXPERT = 512
LN_EPS = 1e-5
NEG_INF = -1e30

LANES = 128
TOK_TILE = 512
EXPERT_TILE = 256
KEYS_PER_CHUNK = WINDOW + CHUNK
VMEM_LIMIT = 56 * 1024 * 1024

F32 = jnp.float32
BF16 = jnp.bfloat16


def _layer_norm(x, g, b):
    mu = jnp.mean(x, axis=-1, keepdims=True)
    d = x - mu
    var = jnp.mean(d * d, axis=-1, keepdims=True)
    return d * lax.rsqrt(var + LN_EPS) * g + b


def _rms_norm(x, g):
    return x * lax.rsqrt(jnp.mean(x * x, axis=-1, keepdims=True) + LN_EPS) * g


def _gelu(x):
    return 0.5 * x * (1.0 + lax.erf(x * (0.5 ** 0.5)))


def _first_index_of_max(rows):
    m = rows[0]
    for r in rows[1:]:
        m = jnp.maximum(m, r)
    idx = jnp.full(m.shape, len(rows), jnp.int32)
    for e in reversed(range(len(rows))):
        idx = jnp.where(rows[e] == m, e, idx)
    return m, idx


def _mixer_kernel(n_prompt_tiles, tiles_per_seq, alpha,
                  sinks_ref, x_ref, w_in_ref, cos_ref, sa_ref, sb_ref, ck_ref, cv_ref,
                  lng_ref, lnb_ref, ws_ref, bst_ref, na_ref, nb_ref, w_o_ref,
                  l1g_ref, l1b_ref, wrt_ref, rb_ref,
                  x1_ref, k_ref, v_ref, vn_ref, ridx_ref, rw_ref,
                  kd_ref, vd_ref, att_ref, gm_ref):
    i = pl.program_id(0)
    is_sample = i >= n_prompt_tiles
    seq_start = jnp.logical_and(jnp.logical_not(is_sample), (i % tiles_per_seq) == 0)
    tm = TOK_TILE

    x = x_ref[...]
    h = jnp.dot(x.astype(BF16), w_in_ref[...], preferred_element_type=F32)

    cos = cos_ref[...]
    sa = sa_ref[...]
    sb = sb_ref[...]

    def rope(blk):
        return blk * cos + pltpu.roll(blk, LANES - HEAD_DIM // 2, 1) * sa + pltpu.roll(blk, HEAD_DIM // 2, 1) * sb

    lane = lax.broadcasted_iota(jnp.int32, (1, LANES), 1)
    lo_half = lane < HEAD_DIM

    o_k = ATT_WIDTH
    o_v = o_k + KV_WIDTH
    o_u = o_v + KV_WIDTH
    o_g = o_u + GM_WIDTH

    k_rot = rope(h[:, o_k:o_v])
    v_new = h[:, o_v:o_u]
    k_ref[...] = k_rot
    v_ref[...] = v_new

    def dup_heads(a):
        sw = pltpu.roll(a, HEAD_DIM, 1)
        return jnp.where(lo_half, a, sw).astype(BF16), jnp.where(lo_half, sw, a).astype(BF16)

    k_d = dup_heads(k_rot)
    v_d = dup_heads(v_new)

    @pl.when(jnp.logical_not(is_sample))
    def _():
        @pl.when(seq_start)
        def _():
            for g in range(N_KV_HEADS):
                kd_ref[g, 0:WINDOW, :] = jnp.zeros((WINDOW, LANES), BF16)
                vd_ref[g, 0:WINDOW, :] = jnp.zeros((WINDOW, LANES), BF16)

        @pl.when(jnp.logical_not(seq_start))
        def _():
            for g in range(N_KV_HEADS):
                kd_ref[g, 0:WINDOW, :] = kd_ref[g, tm:tm + WINDOW, :]
                vd_ref[g, 0:WINDOW, :] = vd_ref[g, tm:tm + WINDOW, :]

        for g in range(N_KV_HEADS):
            kd_ref[g, WINDOW:WINDOW + tm, :] = k_d[g]
            vd_ref[g, WINDOW:WINDOW + tm, :] = v_d[g]

    @pl.when(is_sample)
    def _():
        for b in range(tm // CHUNK):
            ck = dup_heads(ck_ref[b])
            cv = dup_heads(cv_ref[b])
            base = b * KEYS_PER_CHUNK
            for g in range(N_KV_HEADS):
                kd_ref[g, base:base + WINDOW, :] = ck[g]
                vd_ref[g, base:base + WINDOW, :] = cv[g]
                kd_ref[g, base + WINDOW:base + KEYS_PER_CHUNK, :] = k_d[g][b * CHUNK:(b + 1) * CHUNK]
                vd_ref[g, base + WINDOW:base + KEYS_PER_CHUNK, :] = v_d[g][b * CHUNK:(b + 1) * CHUNK]

    scale = HEAD_DIM ** -0.5
    for b in range(ATT_WIDTH // LANES):
        qb = rope(h[:, b * LANES:(b + 1) * LANES]) * scale
        att_ref[:, b * LANES:(b + 1) * LANES] = qb

    key_stride = jnp.where(is_sample, KEYS_PER_CHUNK, CHUNK)
    col = lax.broadcasted_iota(jnp.int32, (1, KEYS_PER_CHUNK), 1)
    row_blk = lax.broadcasted_iota(jnp.int32, (4 * CHUNK, 1), 0) // CHUNK

    def chunk_body(c, carry):
        r0 = pl.multiple_of(c * CHUNK, CHUNK)
        k0 = pl.multiple_of(c * key_stride, CHUNK)
        first_valid = jnp.where(seq_start, (WINDOW // CHUNK - c) * CHUNK, 0)
        valid = col >= first_valid
        outs = []
        for g in range(N_KV_HEADS):
            qs = []
            for bb in range(2):
                qblk = att_ref[pl.ds(r0, CHUNK), (2 * g + bb) * LANES:(2 * g + bb + 1) * LANES]
                qs.append(jnp.where(lo_half, qblk, 0.0).astype(BF16))
                qs.append(jnp.where(lo_half, 0.0, qblk).astype(BF16))
            q_st = jnp.concatenate(qs, axis=0)
            keys = kd_ref[g, pl.ds(k0, KEYS_PER_CHUNK), :]
            vals = vd_ref[g, pl.ds(k0, KEYS_PER_CHUNK), :]
            s = lax.dot_general(q_st, keys, (((1,), (1,)), ((), ())), preferred_element_type=F32)
            s = jnp.where(valid, s, NEG_INF)
            sink = jnp.where(row_blk == 0, sinks_ref[4 * g],
                             jnp.where(row_blk == 1, sinks_ref[4 * g + 1],
                                       jnp.where(row_blk == 2, sinks_ref[4 * g + 2], sinks_ref[4 * g + 3])))
            m = jnp.maximum(jnp.max(s, axis=-1, keepdims=True), sink)
            p = jnp.exp(s - m)
            denom = jnp.sum(p, axis=-1, keepdims=True) + jnp.exp(sink - m)
            p = p * (1.0 / denom)
            o = jnp.dot(p.astype(BF16), vals, preferred_element_type=F32)
            for bb in range(2):
                outs.append(jnp.where(lo_half, o[(2 * bb) * CHUNK:(2 * bb + 1) * CHUNK],
                                      o[(2 * bb + 1) * CHUNK:(2 * bb + 2) * CHUNK]))
        for b in range(ATT_WIDTH // LANES):
            att_ref[pl.ds(r0, CHUNK), b * LANES:(b + 1) * LANES] = outs[b]
        return carry

    lax.fori_loop(0, tm // CHUNK, chunk_body, 0)

    prow = lax.broadcasted_iota(jnp.int32, (GM_CHUNK, GM_CHUNK), 0)
    pcol = lax.broadcasted_iota(jnp.int32, (GM_CHUNK, GM_CHUNK), 1)
    half = GM_CHUNK // 2
    tril = pcol <= prow
    same_blk = (prow < half) == (pcol < half)
    prow1 = lax.broadcasted_iota(jnp.int32, (GM_CHUNK, 1), 0)
    bst = bst_ref[...]
    bst_s = jnp.where(prow1 < half, bst, pltpu.roll(bst, half, 0))
    bias = jnp.where(is_sample, bst_s, bst)
    for g in range(GM_GROUPS):
        wg_ = ws_ref[g]
        w_s = jnp.where(prow < half, wg_, pltpu.roll(pltpu.roll(wg_, half, 0), half, 1))
        w_eff = jnp.where(is_sample, jnp.where(same_blk, w_s, 0.0), wg_)
        w_eff = jnp.where(tril, w_eff, 0.0).astype(BF16)
        cols = slice(g * GM_CH, (g + 1) * GM_CH)
        ug = _gelu(h[:, o_u + g * GM_CH:o_u + (g + 1) * GM_CH])
        vn = _layer_norm(_gelu(h[:, o_g + g * GM_CH:o_g + (g + 1) * GM_CH]), lng_ref[:, cols], lnb_ref[:, cols])

        @pl.when(is_sample)
        def _():
            vn_ref[:, cols] = vn

        vn_b = vn.astype(BF16)
        for n in range(tm // GM_CHUNK):
            rows = slice(n * GM_CHUNK, (n + 1) * GM_CHUNK)
            s = jnp.dot(w_eff, vn_b[rows], preferred_element_type=F32) + bias[:, g:g + 1]
            gm_ref[rows, cols] = ug[rows] * s

    mixed = jnp.concatenate([_rms_norm(att_ref[...], na_ref[...]), _rms_norm(gm_ref[...], nb_ref[...])], axis=-1)
    y = alpha * x + jnp.dot(mixed.astype(BF16), w_o_ref[...], preferred_element_type=F32)
    x1 = _layer_norm(y, l1g_ref[...], l1b_ref[...])
    x1_ref[...] = x1

    logits = lax.dot_general(wrt_ref[...], x1, (((1,), (1,)), ((), ())),
                             preferred_element_type=F32, precision=lax.Precision.HIGHEST)
    scores = jax.nn.sigmoid(logits)
    sel = scores + rb_ref[...]
    sel_rows = [sel[e:e + 1, :] for e in range(N_EXPERTS)]
    sc_rows = [scores[e:e + 1, :] for e in range(N_EXPERTS)]
    grp = []
    for g in range(N_EXPERT_GROUPS):
        r = sel_rows[g * EXPERTS_PER_GROUP:(g + 1) * EXPERTS_PER_GROUP]
        best_pair = None
        for a in range(EXPERTS_PER_GROUP):
            for b in range(a + 1, EXPERTS_PER_GROUP):
                pair = r[a] + r[b]
                best_pair = pair if best_pair is None else jnp.maximum(best_pair, pair)
        grp.append(best_pair)
    _, best = _first_index_of_max(grp)
    masked = [jnp.where(best == (e // EXPERTS_PER_GROUP), sel_rows[e], NEG_INF) for e in range(N_EXPERTS)]
    _, e0 = _first_index_of_max(masked)
    masked2 = [jnp.where(e0 == e, -jnp.inf, masked[e]) for e in range(N_EXPERTS)]
    _, e1 = _first_index_of_max(masked2)
    w0 = jnp.zeros_like(sc_rows[0])
    w1 = jnp.zeros_like(sc_rows[0])
    for e in range(N_EXPERTS):
        w0 = jnp.where(e0 == e, sc_rows[e], w0)
        w1 = jnp.where(e1 == e, sc_rows[e], w1)
    wsum = w0 + w1
    zi = jnp.zeros((6, tm), jnp.int32)
    zf = jnp.zeros((6, tm), F32)
    ridx_ref[...] = jnp.concatenate([e0, e1, zi], axis=0)
    rw_ref[...] = jnp.concatenate([w0 / wsum, w1 / wsum, zf], axis=0)


def _mixer_call(layer_args, x_all, tables, n_prompt_tiles, tiles_per_seq, alpha):
    (sinks, w_in, ck, cv, lng, lnb, ws, bst, na, nb, w_o, l1g, l1b, wrt, rb) = layer_args
    cos_t, sa_t, sb_t = tables
    t_all = x_all.shape[0]
    n_tiles = t_all // TOK_TILE
    n_seq_tiles = tiles_per_seq

    def const(shape):
        nd = len(shape)
        return pl.BlockSpec(shape, lambda i, _nd=nd: (0,) * _nd)

    def tab_map(i):
        return (jnp.where(i < n_prompt_tiles, i % n_seq_tiles, n_seq_tiles), 0)

    row_blk = lambda w: pl.BlockSpec((TOK_TILE, w), lambda i: (i, 0))
    in_specs = [
        pl.BlockSpec(memory_space=pltpu.SMEM),
        row_blk(D_MODEL),
        const((D_MODEL, D_IN)),
        pl.BlockSpec((TOK_TILE, LANES), tab_map),
        pl.BlockSpec((TOK_TILE, LANES), tab_map),
        pl.BlockSpec((TOK_TILE, LANES), tab_map),
        const(ck.shape), const(cv.shape),
        const((1, GM_WIDTH)), const((1, GM_WIDTH)),
        const((GM_GROUPS, GM_CHUNK, GM_CHUNK)), const((GM_CHUNK, GM_GROUPS)),
        const((1, ATT_WIDTH)), const((1, GM_WIDTH)),
        const((D_MIX, D_MODEL)),
        const((1, D_MODEL)), const((1, D_MODEL)),
        const((N_EXPERTS, D_MODEL)), const((N_EXPERTS, 1)),
    ]
    out_shape = [
        jax.ShapeDtypeStruct((t_all, D_MODEL), F32),
        jax.ShapeDtypeStruct((t_all, KV_WIDTH), F32),
        jax.ShapeDtypeStruct((t_all, KV_WIDTH), F32),
        jax.ShapeDtypeStruct((TOK_TILE, GM_WIDTH), F32),
        jax.ShapeDtypeStruct((8, t_all), jnp.int32),
        jax.ShapeDtypeStruct((8, t_all), F32),
    ]
    out_specs = [
        row_blk(D_MODEL), row_blk(KV_WIDTH), row_blk(KV_WIDTH),
        const((TOK_TILE, GM_WIDTH)),
        pl.BlockSpec((8, TOK_TILE), lambda i: (0, i)),
        pl.BlockSpec((8, TOK_TILE), lambda i: (0, i)),
    ]
    kd_rows = (TOK_TILE // CHUNK) * KEYS_PER_CHUNK
    scratch = [
        pltpu.VMEM((N_KV_HEADS, kd_rows, LANES), BF16),
        pltpu.VMEM((N_KV_HEADS, kd_rows, LANES), BF16),
        pltpu.VMEM((TOK_TILE, ATT_WIDTH), F32),
        pltpu.VMEM((TOK_TILE, GM_WIDTH), F32),
    ]
    return pl.pallas_call(
        functools.partial(_mixer_kernel, n_prompt_tiles, tiles_per_seq, alpha),
        grid=(n_tiles,),
        in_specs=in_specs, out_specs=out_specs, out_shape=out_shape,
        scratch_shapes=scratch,
        compiler_params=pltpu.CompilerParams(dimension_semantics=("arbitrary",), vmem_limit_bytes=VMEM_LIMIT),
        name="mixer",
    )(sinks, x_all, w_in, cos_t, sa_t, sb_t, ck, cv, lng, lnb, ws, bst, na, nb, w_o, l1g, l1b, wrt, rb)


def _expert_kernel(te_ref, nt_ref, xs_ref, wg_ref, wu_ref, wd_ref, out_ref, wg_s, wu_s, wd_s):
    i = pl.program_id(0)
    changed = jnp.logical_or(i == 0, te_ref[i] != te_ref[jnp.maximum(i - 1, 0)])

    @pl.when(changed)
    def _():
        wg_s[...] = wg_ref[0].astype(BF16)
        wu_s[...] = wu_ref[0].astype(BF16)
        wd_s[...] = wd_ref[0].astype(BF16)

    @pl.when(i < nt_ref[0])
    def _():
        xs = xs_ref[...]
        g = jnp.dot(xs, wg_s[...], preferred_element_type=F32)
        u = jnp.dot(xs, wu_s[...], preferred_element_type=F32)
        hmid = (g * jax.nn.sigmoid(g)) * u
        out_ref[...] = jnp.dot(hmid.astype(BF16), wd_s[...], preferred_element_type=F32)

    @pl.when(i >= nt_ref[0])
    def _():
        out_ref[...] = jnp.zeros_like(out_ref)


def _expert_call(tile_expert, n_used, xs, wg, wu, wd):
    n_rows = xs.shape[0]
    n_tiles = n_rows // EXPERT_TILE
    grid_spec = pltpu.PrefetchScalarGridSpec(
        num_scalar_prefetch=2,
        grid=(n_tiles,),
        in_specs=[
            pl.BlockSpec((EXPERT_TILE, D_MODEL), lambda i, te, nt: (i, 0)),
            pl.BlockSpec((1, D_MODEL, D_EXPERT), lambda i, te, nt: (te[i], 0, 0)),
            pl.BlockSpec((1, D_MODEL, D_EXPERT), lambda i, te, nt: (te[i], 0, 0)),
            pl.BlockSpec((1, D_EXPERT, D_MODEL), lambda i, te, nt: (te[i], 0, 0)),
        ],
        out_specs=pl.BlockSpec((EXPERT_TILE, D_MODEL), lambda i, te, nt: (i, 0)),
        scratch_shapes=[
            pltpu.VMEM((D_MODEL, D_EXPERT), BF16),
            pltpu.VMEM((D_MODEL, D_EXPERT), BF16),
            pltpu.VMEM((D_EXPERT, D_MODEL), BF16),
        ],
    )
    return pl.pallas_call(
        _expert_kernel,
        grid_spec=grid_spec,
        out_shape=jax.ShapeDtypeStruct((n_rows, D_MODEL), F32),
        compiler_params=pltpu.CompilerParams(dimension_semantics=("arbitrary",), vmem_limit_bytes=VMEM_LIMIT),
        name="experts",
    )(tile_expert, n_used, xs, wg, wu, wd)


def _combine_kernel(alpha, x1_ref, o_ref, w_ref, g_ref, b_ref, out_ref):
    w = w_ref[...]
    y = w[:, 0:1] * o_ref[:, :D_MODEL] + w[:, 1:2] * o_ref[:, D_MODEL:]
    out_ref[...] = _layer_norm(alpha * x1_ref[...] + y, g_ref[...], b_ref[...])


def _combine_call(x1, o_pair, w_col, g, b, alpha):
    t_all = x1.shape[0]
    return pl.pallas_call(
        functools.partial(_combine_kernel, alpha),
        grid=(t_all // TOK_TILE,),
        in_specs=[
            pl.BlockSpec((TOK_TILE, D_MODEL), lambda i: (i, 0)),
            pl.BlockSpec((TOK_TILE, 2 * D_MODEL), lambda i: (i, 0)),
            pl.BlockSpec((TOK_TILE, TOP_K), lambda i: (i, 0)),
            pl.BlockSpec((1, D_MODEL), lambda i: (0, 0)),
            pl.BlockSpec((1, D_MODEL), lambda i: (0, 0)),
        ],
        out_specs=pl.BlockSpec((TOK_TILE, D_MODEL), lambda i: (i, 0)),
        out_shape=jax.ShapeDtypeStruct((t_all, D_MODEL), F32),
        compiler_params=pltpu.CompilerParams(dimension_semantics=("arbitrary",), vmem_limit_bytes=VMEM_LIMIT),
        name="combine",
    )(x1, o_pair, w_col, g, b)


def _rope_tables(seq, dec_seq, past_len):
    half = HEAD_DIM // 2
    inv = ROPE_THETA ** (-jnp.arange(half, dtype=F32) / half)
    pos = jnp.concatenate([jnp.arange(seq), past_len + (jnp.arange(TOK_TILE) % dec_seq)])
    ang = pos.astype(F32)[:, None] * inv[None, :]
    cos = jnp.cos(ang)
    sin = jnp.sin(ang)
    zero = jnp.zeros_like(sin)
    reps = LANES // HEAD_DIM
    cos_t = jnp.tile(jnp.concatenate([cos, cos], -1), (1, reps))
    sa_t = jnp.tile(jnp.concatenate([-sin, zero], -1), (1, reps))
    sb_t = jnp.tile(jnp.concatenate([zero, sin], -1), (1, reps))
    return cos_t, sa_t, sb_t


def _dispatch_plan(ridx, n_tiles):
    t_all = ridx.shape[1]
    flat_e = ridx[:TOP_K].T.reshape(-1)
    onehot = (flat_e[:, None] == jnp.arange(N_EXPERTS)[None, :]).astype(jnp.int32)
    csum = jnp.cumsum(onehot, axis=0)
    rank = jnp.take_along_axis(csum, flat_e[:, None], axis=1)[:, 0] - 1
    counts = csum[-1]
    tiles_e = (counts + EXPERT_TILE - 1) // EXPERT_TILE
    tile_end = jnp.cumsum(tiles_e)
    offs = (tile_end - tiles_e) * EXPERT_TILE
    pos = offs[flat_e] + rank
    n_used = tile_end[-1:]
    tile_expert = jnp.minimum(
        jnp.searchsorted(tile_end, jnp.arange(n_tiles), side="right"), N_EXPERTS - 1).astype(jnp.int32)
    src = jnp.zeros((n_tiles * EXPERT_TILE,), jnp.int32).at[pos].set(jnp.arange(TOP_K * t_all, dtype=jnp.int32) // TOP_K)
    return pos, src, tile_expert, n_used.astype(jnp.int32)


def kernel(x_prompt, x_sample, cache_k, cache_v, w_in, sinks, gm_ln_g, gm_ln_b, gm_ws, gm_bs,
           out_norm_a, out_norm_b, w_o, ln1_g, ln1_b, w_router, router_bias,
           w_gate, w_up, w_down, ln2_g, ln2_b):
    batch, seq, _ = x_prompt.shape
    dec_batch, dec_seq, _ = x_sample.shape
    depth = w_in.shape[0]
    past_len = 2048
    assert dec_batch * dec_seq == TOK_TILE and dec_seq == CHUNK and seq % TOK_TILE == 0
    assert cache_k.shape[2] == WINDOW
    alpha = (2 * depth) ** 0.25
    n_prompt = batch * seq
    t_all = n_prompt + dec_batch * dec_seq
    n_prompt_tiles = n_prompt // TOK_TILE
    tiles_per_seq = seq // TOK_TILE
    n_exp_tiles = (TOP_K * t_all) // EXPERT_TILE + N_EXPERTS

    tables = _rope_tables(seq, dec_seq, past_len)
    x = jnp.concatenate([x_prompt.reshape(n_prompt, D_MODEL), x_sample.reshape(-1, D_MODEL)], axis=0)
    wrt = w_router.T
    rb = router_bias.reshape(N_EXPERTS, 1)

    kp, vp, ks, vs, gms = [], [], [], [], []
    for l in range(depth):
        layer_args = (
            sinks[l], w_in[l].astype(BF16),
            cache_k[l].reshape(dec_batch, WINDOW, KV_WIDTH), cache_v[l].reshape(dec_batch, WINDOW, KV_WIDTH),
            gm_ln_g[l].reshape(1, GM_WIDTH), gm_ln_b[l].reshape(1, GM_WIDTH),
            gm_ws[l], gm_bs[l].T,
            out_norm_a[l].reshape(1, ATT_WIDTH), out_norm_b[l].reshape(1, GM_WIDTH),
            w_o[l].astype(BF16),
            ln1_g[l].reshape(1, D_MODEL), ln1_b[l].reshape(1, D_MODEL), wrt, rb)
        x1, k_all, v_all, vn_s, ridx, rw = _mixer_call(layer_args, x, tables, n_prompt_tiles, tiles_per_seq, alpha)

        pos, src, tile_expert, n_used = _dispatch_plan(ridx, n_exp_tiles)
        xs = jnp.take(x1.astype(BF16), src, axis=0)
        outs = _expert_call(tile_expert, n_used, xs, w_gate[l], w_up[l], w_down[l])
        o_pair = jnp.take(outs, pos, axis=0).reshape(t_all, TOP_K * D_MODEL)
        x = _combine_call(x1, o_pair, rw[:TOP_K].T, ln2_g[l].reshape(1, D_MODEL), ln2_b[l].reshape(1, D_MODEL), alpha)

        k4 = k_all[:n_prompt].reshape(batch, seq, N_KV_HEADS, HEAD_DIM)
        v4 = v_all[:n_prompt].reshape(batch, seq, N_KV_HEADS, HEAD_DIM)
        kp.append(k4[:, -WINDOW:])
        vp.append(v4[:, -WINDOW:])
        ks.append(k_all[n_prompt:].reshape(dec_batch, dec_seq, N_KV_HEADS, HEAD_DIM))
        vs.append(v_all[n_prompt:].reshape(dec_batch, dec_seq, N_KV_HEADS, HEAD_DIM))
        gms.append(vn_s.reshape(dec_batch, dec_seq, GM_WIDTH))

    y_prompt = x[:n_prompt].reshape(batch, seq, D_MODEL)
    y_sample = x[n_prompt:].reshape(dec_batch, dec_seq, D_MODEL)
    return (y_prompt, y_sample, jnp.stack(kp), jnp.stack(vp), jnp.stack(ks), jnp.stack(vs), jnp.stack(gms))
```

```python
import functools

import jax
import jax.numpy as jnp
from jax import lax
from jax.experimental import pallas as pl
from jax.experimental.pallas import tpu as pltpu

D_MODEL = 1024
CHUNK = 64
WINDOW = 128
HEAD_DIM = 64
N_Q_HEADS = 8
N_KV_HEADS = 2
ATT_WIDTH = N_Q_HEADS * HEAD_DIM
KV_WIDTH = N_KV_HEADS * HEAD_DIM
ROPE_THETA = 10000.0
GM_GROUPS = 4
GM_CH = 128
GM_WIDTH = GM_GROUPS * GM_CH
GM_CHUNK = 128
D_MIX = ATT_WIDTH + GM_WIDTH
D_IN = ATT_WIDTH + 2 * KV_WIDTH + 2 * GM_WIDTH
N_EXPERTS = 16
N_EXPERT_GROUPS = 4
EXPERTS_PER_GROUP = N_EXPERTS // N_EXPERT_GROUPS
TOP_K = 2
D_EXPERT = 512
LN_EPS = 1e-5
NEG_INF = -1e30

LANES = 128
TILE_ROWS = D_MODEL // LANES
TOK_TILE = 512
EXPERT_TILE = 256
KEYS_PER_CHUNK = WINDOW + CHUNK
VMEM_LIMIT = 56 * 1024 * 1024

F32 = jnp.float32
BF16 = jnp.bfloat16


def _layer_norm(x, g, b):
    mu = jnp.mean(x, axis=-1, keepdims=True)
    d = x - mu
    var = jnp.mean(d * d, axis=-1, keepdims=True)
    return d * lax.rsqrt(var + LN_EPS) * g + b


def _rms_norm(x, g):
    return x * lax.rsqrt(jnp.mean(x * x, axis=-1, keepdims=True) + LN_EPS) * g


def _gelu(x):
    return 0.5 * x * (1.0 + lax.erf(x * (0.5 ** 0.5)))


def _store_token_tiles(ref, x):
    n = x.shape[0]
    for c in range(TILE_ROWS):
        ref[pl.ds(c, n, stride=TILE_ROWS), :] = x[:, c * LANES:(c + 1) * LANES]


def _load_token_tiles(ref, n, row0=0):
    return jnp.concatenate([ref[pl.ds(row0 + c, n, stride=TILE_ROWS), :] for c in range(TILE_ROWS)], axis=-1)


def _first_index_of_max(rows):
    m = rows[0]
    for r in rows[1:]:
        m = jnp.maximum(m, r)
    idx = jnp.full(m.shape, len(rows), jnp.int32)
    for e in reversed(range(len(rows))):
        idx = jnp.where(rows[e] == m, e, idx)
    return m, idx


def _mixer_kernel(n_prompt_tiles, tiles_per_seq, alpha,
                  sinks_ref, x_ref, w_in_ref, cos_ref, sa_ref, sb_ref, ck_ref, cv_ref,
                  lng_ref, lnb_ref, ws_ref, bst_ref, na_ref, nb_ref, w_o_ref,
                  l1g_ref, l1b_ref, wrt_ref, rb_ref,
                  x1t_ref, k_ref, v_ref, vn_ref, ridx_ref, rw_ref, counts_ref,
                  kd_ref, vd_ref, att_ref, gm_ref, cnt_ref):
    i = pl.program_id(0)
    is_sample = i >= n_prompt_tiles
    seq_start = jnp.logical_and(jnp.logical_not(is_sample), (i % tiles_per_seq) == 0)
    tm = TOK_TILE

    x = x_ref[...]
    h = jnp.dot(x.astype(BF16), w_in_ref[...], preferred_element_type=F32)

    cos = cos_ref[...]
    sa = sa_ref[...]
    sb = sb_ref[...]

    def rope(blk):
        return blk * cos + pltpu.roll(blk, LANES - HEAD_DIM // 2, 1) * sa + pltpu.roll(blk, HEAD_DIM // 2, 1) * sb

    lane = lax.broadcasted_iota(jnp.int32, (1, LANES), 1)
    lo_half = lane < HEAD_DIM

    o_k = ATT_WIDTH
    o_v = o_k + KV_WIDTH
    o_u = o_v + KV_WIDTH
    o_g = o_u + GM_WIDTH

    k_rot = rope(h[:, o_k:o_v])
    v_new = h[:, o_v:o_u]
    k_ref[...] = k_rot
    v_ref[...] = v_new

    def dup_heads(a):
        sw = pltpu.roll(a, HEAD_DIM, 1)
        return jnp.where(lo_half, a, sw).astype(BF16), jnp.where(lo_half, sw, a).astype(BF16)

    k_d = dup_heads(k_rot)
    v_d = dup_heads(v_new)

    @pl.when(jnp.logical_not(is_sample))
    def _():
        @pl.when(seq_start)
        def _():
            for g in range(N_KV_HEADS):
                kd_ref[g, 0:WINDOW, :] = jnp.zeros((WINDOW, LANES), BF16)
                vd_ref[g, 0:WINDOW, :] = jnp.zeros((WINDOW, LANES), BF16)

        @pl.when(jnp.logical_not(seq_start))
        def _():
            for g in range(N_KV_HEADS):
                kd_ref[g, 0:WINDOW, :] = kd_ref[g, tm:tm + WINDOW, :]
                vd_ref[g, 0:WINDOW, :] = vd_ref[g, tm:tm + WINDOW, :]

        for g in range(N_KV_HEADS):
            kd_ref[g, WINDOW:WINDOW + tm, :] = k_d[g]
            vd_ref[g, WINDOW:WINDOW + tm, :] = v_d[g]

    @pl.when(is_sample)
    def _():
        for b in range(tm // CHUNK):
            ck = dup_heads(ck_ref[b])
            cv = dup_heads(cv_ref[b])
            base = b * KEYS_PER_CHUNK
            for g in range(N_KV_HEADS):
                kd_ref[g, base:base + WINDOW, :] = ck[g]
                vd_ref[g, base:base + WINDOW, :] = cv[g]
                kd_ref[g, base + WINDOW:base + KEYS_PER_CHUNK, :] = k_d[g][b * CHUNK:(b + 1) * CHUNK]
                vd_ref[g, base + WINDOW:base + KEYS_PER_CHUNK, :] = v_d[g][b * CHUNK:(b + 1) * CHUNK]

    scale = HEAD_DIM ** -0.5
    for b in range(ATT_WIDTH // LANES):
        qb = rope(h[:, b * LANES:(b + 1) * LANES]) * scale
        att_ref[:, b * LANES:(b + 1) * LANES] = qb

    key_stride = jnp.where(is_sample, KEYS_PER_CHUNK, CHUNK)
    col = lax.broadcasted_iota(jnp.int32, (1, KEYS_PER_CHUNK), 1)
    row_blk = lax.broadcasted_iota(jnp.int32, (4 * CHUNK, 1), 0) // CHUNK

    def chunk_body(c, carry):
        r0 = pl.multiple_of(c * CHUNK, CHUNK)
        k0 = pl.multiple_of(c * key_stride, CHUNK)
        first_valid = jnp.where(seq_start, (WINDOW // CHUNK - c) * CHUNK, 0)
        valid = col >= first_valid
        outs = []
        for g in range(N_KV_HEADS):
            qs = []
            for bb in range(2):
                qblk = att_ref[pl.ds(r0, CHUNK), (2 * g + bb) * LANES:(2 * g + bb + 1) * LANES]
                qs.append(jnp.where(lo_half, qblk, 0.0).astype(BF16))
                qs.append(jnp.where(lo_half, 0.0, qblk).astype(BF16))
            q_st = jnp.concatenate(qs, axis=0)
            keys = kd_ref[g, pl.ds(k0, KEYS_PER_CHUNK), :]
            vals = vd_ref[g, pl.ds(k0, KEYS_PER_CHUNK), :]
            s = lax.dot_general(q_st, keys, (((1,), (1,)), ((), ())), preferred_element_type=F32)
            s = jnp.where(valid, s, NEG_INF)
            sink = jnp.where(row_blk == 0, sinks_ref[4 * g],
                             jnp.where(row_blk == 1, sinks_ref[4 * g + 1],
                                       jnp.where(row_blk == 2, sinks_ref[4 * g + 2], sinks_ref[4 * g + 3])))
            m = jnp.maximum(jnp.max(s, axis=-1, keepdims=True), sink)
            p = jnp.exp(s - m)
            denom = jnp.sum(p, axis=-1, keepdims=True) + jnp.exp(sink - m)
            p = p * (1.0 / denom)
            o = jnp.dot(p.astype(BF16), vals, preferred_element_type=F32)
            for bb in range(2):
                outs.append(jnp.where(lo_half, o[(2 * bb) * CHUNK:(2 * bb + 1) * CHUNK],
                                      o[(2 * bb + 1) * CHUNK:(2 * bb + 2) * CHUNK]))
        for b in range(ATT_WIDTH // LANES):
            att_ref[pl.ds(r0, CHUNK), b * LANES:(b + 1) * LANES] = outs[b]
        return carry

    lax.fori_loop(0, tm // CHUNK, chunk_body, 0)

    prow = lax.broadcasted_iota(jnp.int32, (GM_CHUNK, GM_CHUNK), 0)
    pcol = lax.broadcasted_iota(jnp.int32, (GM_CHUNK, GM_CHUNK), 1)
    half = GM_CHUNK // 2
    tril = pcol <= prow
    same_blk = (prow < half) == (pcol < half)
    prow1 = lax.broadcasted_iota(jnp.int32, (GM_CHUNK, 1), 0)
    bst = bst_ref[...]
    bst_s = jnp.where(prow1 < half, bst, pltpu.roll(bst, half, 0))
    bias = jnp.where(is_sample, bst_s, bst)
    for g in range(GM_GROUPS):
        wg_ = ws_ref[g]
        w_s = jnp.where(prow < half, wg_, pltpu.roll(pltpu.roll(wg_, half, 0), half, 1))
        w_eff = jnp.where(is_sample, jnp.where(same_blk, w_s, 0.0), wg_)
        w_eff = jnp.where(tril, w_eff, 0.0).astype(BF16)
        cols = slice(g * GM_CH, (g + 1) * GM_CH)
        ug = _gelu(h[:, o_u + g * GM_CH:o_u + (g + 1) * GM_CH])
        vn = _layer_norm(_gelu(h[:, o_g + g * GM_CH:o_g + (g + 1) * GM_CH]), lng_ref[:, cols], lnb_ref[:, cols])

        @pl.when(is_sample)
        def _():
            vn_ref[:, cols] = vn

        vn_b = vn.astype(BF16)
        for n in range(tm // GM_CHUNK):
            rows = slice(n * GM_CHUNK, (n + 1) * GM_CHUNK)
            s = jnp.dot(w_eff, vn_b[rows], preferred_element_type=F32) + bias[:, g:g + 1]
            gm_ref[rows, cols] = ug[rows] * s

    mixed = jnp.concatenate([_rms_norm(att_ref[...], na_ref[...]), _rms_norm(gm_ref[...], nb_ref[...])], axis=-1)
    y = alpha * x + jnp.dot(mixed.astype(BF16), w_o_ref[...], preferred_element_type=F32)
    x1 = _layer_norm(y, l1g_ref[...], l1b_ref[...])
    _store_token_tiles(x1t_ref, x1)

    logits = lax.dot_general(wrt_ref[...], x1, (((1,), (1,)), ((), ())),
                             preferred_element_type=F32, precision=lax.Precision.HIGHEST)
    scores = jax.nn.sigmoid(logits)
    sel = scores + rb_ref[...]
    sel_rows = [sel[e:e + 1, :] for e in range(N_EXPERTS)]
    sc_rows = [scores[e:e + 1, :] for e in range(N_EXPERTS)]
    grp = []
    for g in range(N_EXPERT_GROUPS):
        r = sel_rows[g * EXPERTS_PER_GROUP:(g + 1) * EXPERTS_PER_GROUP]
        best_pair = None
        for a in range(EXPERTS_PER_GROUP):
            for b in range(a + 1, EXPERTS_PER_GROUP):
                pair = r[a] + r[b]
                best_pair = pair if best_pair is None else jnp.maximum(best_pair, pair)
        grp.append(best_pair)
    _, best = _first_index_of_max(grp)
    masked = [jnp.where(best == (e // EXPERTS_PER_GROUP), sel_rows[e], NEG_INF) for e in range(N_EXPERTS)]
    _, e0 = _first_index_of_max(masked)
    masked2 = [jnp.where(e0 == e, -jnp.inf, masked[e]) for e in range(N_EXPERTS)]
    _, e1 = _first_index_of_max(masked2)
    w0 = jnp.zeros_like(sc_rows[0])
    w1 = jnp.zeros_like(sc_rows[0])
    for e in range(N_EXPERTS):
        w0 = jnp.where(e0 == e, sc_rows[e], w0)
        w1 = jnp.where(e1 == e, sc_rows[e], w1)
    wsum = w0 + w1
    rw_ref[...] = jnp.concatenate([w0 / wsum, w1 / wsum, jnp.zeros((6, tm), F32)], axis=0)

    @pl.when(i == 0)
    def _():
        cnt_ref[...] = jnp.zeros_like(cnt_ref)

    eid = lax.broadcasted_iota(jnp.int32, (N_EXPERTS, tm), 0)
    oh0 = (eid == e0).astype(F32)
    oh1 = (eid == e1).astype(F32)
    oh = oh0 + oh1
    upper = (lax.broadcasted_iota(jnp.int32, (tm, tm), 0) <= lax.broadcasted_iota(jnp.int32, (tm, tm), 1))
    incl = jnp.dot(oh.astype(BF16), upper.astype(BF16), preferred_element_type=F32)
    before = cnt_ref[:, 0:1] + incl - oh
    r0 = jnp.sum(oh0 * before, axis=0, keepdims=True).astype(jnp.int32)
    r1 = jnp.sum(oh1 * before, axis=0, keepdims=True).astype(jnp.int32)
    cnt = cnt_ref[...] + incl[:, tm - 1:tm]
    cnt_ref[...] = cnt
    counts_ref[...] = cnt.astype(jnp.int32)
    ridx_ref[...] = jnp.concatenate([e0, e1, r0, r1, jnp.zeros((4, tm), jnp.int32)], axis=0)


def _mixer_call(layer_args, x_all, tables, n_prompt_tiles, tiles_per_seq, alpha):
    (sinks, w_in, ck, cv, lng, lnb, ws, bst, na, nb, w_o, l1g, l1b, wrt, rb) = layer_args
    cos_t, sa_t, sb_t = tables
    t_all = x_all.shape[0]
    n_tiles = t_all // TOK_TILE
    n_seq_tiles = tiles_per_seq

    def const(shape):
        nd = len(shape)
        return pl.BlockSpec(shape, lambda i, _nd=nd: (0,) * _nd)

    def tab_map(i):
        return (jnp.where(i < n_prompt_tiles, i % n_seq_tiles, n_seq_tiles), 0)

    row_blk = lambda w: pl.BlockSpec((TOK_TILE, w), lambda i: (i, 0))
    in_specs = [
        pl.BlockSpec(memory_space=pltpu.SMEM),
        row_blk(D_MODEL),
        const((D_MODEL, D_IN)),
        pl.BlockSpec((TOK_TILE, LANES), tab_map),
        pl.BlockSpec((TOK_TILE, LANES), tab_map),
        pl.BlockSpec((TOK_TILE, LANES), tab_map),
        const(ck.shape), const(cv.shape),
        const((1, GM_WIDTH)), const((1, GM_WIDTH)),
        const((GM_GROUPS, GM_CHUNK, GM_CHUNK)), const((GM_CHUNK, GM_GROUPS)),
        const((1, ATT_WIDTH)), const((1, GM_WIDTH)),
        const((D_MIX, D_MODEL)),
        const((1, D_MODEL)), const((1, D_MODEL)),
        const((N_EXPERTS, D_MODEL)), const((N_EXPERTS, 1)),
    ]
    out_shape = [
        jax.ShapeDtypeStruct((t_all * TILE_ROWS, LANES), F32),
        jax.ShapeDtypeStruct((t_all, KV_WIDTH), F32),
        jax.ShapeDtypeStruct((t_all, KV_WIDTH), F32),
        jax.ShapeDtypeStruct((TOK_TILE, GM_WIDTH), F32),
        jax.ShapeDtypeStruct((8, t_all), jnp.int32),
        jax.ShapeDtypeStruct((8, t_all), F32),
        jax.ShapeDtypeStruct((N_EXPERTS, LANES), jnp.int32),
    ]
    out_specs = [
        pl.BlockSpec((TOK_TILE * TILE_ROWS, LANES), lambda i: (i, 0)),
        row_blk(KV_WIDTH), row_blk(KV_WIDTH),
        const((TOK_TILE, GM_WIDTH)),
        pl.BlockSpec((8, TOK_TILE), lambda i: (0, i)),
        pl.BlockSpec((8, TOK_TILE), lambda i: (0, i)),
        const((N_EXPERTS, LANES)),
    ]
    kd_rows = (TOK_TILE // CHUNK) * KEYS_PER_CHUNK
    scratch = [
        pltpu.VMEM((N_KV_HEADS, kd_rows, LANES), BF16),
        pltpu.VMEM((N_KV_HEADS, kd_rows, LANES), BF16),
        pltpu.VMEM((TOK_TILE, ATT_WIDTH), F32),
        pltpu.VMEM((TOK_TILE, GM_WIDTH), F32),
        pltpu.VMEM((N_EXPERTS, LANES), F32),
    ]
    return pl.pallas_call(
        functools.partial(_mixer_kernel, n_prompt_tiles, tiles_per_seq, alpha),
        grid=(n_tiles,),
        in_specs=in_specs, out_specs=out_specs, out_shape=out_shape,
        scratch_shapes=scratch,
        compiler_params=pltpu.CompilerParams(dimension_semantics=("arbitrary",), vmem_limit_bytes=VMEM_LIMIT),
        name="mixer",
    )(sinks, x_all, w_in, cos_t, sa_t, sb_t, ck, cv, lng, lnb, ws, bst, na, nb, w_o, l1g, l1b, wrt, rb)


def _expert_kernel(te_ref, nt_ref, xs_ref, wg_ref, wu_ref, wd_ref, out_ref, wg_s, wu_s, wd_s):
    i = pl.program_id(0)
    changed = jnp.logical_or(i == 0, te_ref[i] != te_ref[jnp.maximum(i - 1, 0)])

    @pl.when(changed)
    def _():
        wg_s[...] = wg_ref[0].astype(BF16)
        wu_s[...] = wu_ref[0].astype(BF16)
        wd_s[...] = wd_ref[0].astype(BF16)

    @pl.when(i < nt_ref[0])
    def _():
        xs = _load_token_tiles(xs_ref, EXPERT_TILE).astype(BF16)
        g = jnp.dot(xs, wg_s[...], preferred_element_type=F32)
        u = jnp.dot(xs, wu_s[...], preferred_element_type=F32)
        hmid = (g * jax.nn.sigmoid(g)) * u
        _store_token_tiles(out_ref, jnp.dot(hmid.astype(BF16), wd_s[...], preferred_element_type=F32))

    @pl.when(i >= nt_ref[0])
    def _():
        out_ref[...] = jnp.zeros_like(out_ref)


def _expert_call(tile_expert, n_used, xs, wg, wu, wd):
    n_tiles = xs.shape[0] // (EXPERT_TILE * TILE_ROWS)
    tile_blk = pl.BlockSpec((EXPERT_TILE * TILE_ROWS, LANES), lambda i, te, nt: (i, 0))
    grid_spec = pltpu.PrefetchScalarGridSpec(
        num_scalar_prefetch=2,
        grid=(n_tiles,),
        in_specs=[
            tile_blk,
            pl.BlockSpec((1, D_MODEL, D_EXPERT), lambda i, te, nt: (te[i], 0, 0)),
            pl.BlockSpec((1, D_MODEL, D_EXPERT), lambda i, te, nt: (te[i], 0, 0)),
            pl.BlockSpec((1, D_EXPERT, D_MODEL), lambda i, te, nt: (te[i], 0, 0)),
        ],
        out_specs=tile_blk,
        scratch_shapes=[
            pltpu.VMEM((D_MODEL, D_EXPERT), BF16),
            pltpu.VMEM((D_MODEL, D_EXPERT), BF16),
            pltpu.VMEM((D_EXPERT, D_MODEL), BF16),
        ],
    )
    return pl.pallas_call(
        _expert_kernel,
        grid_spec=grid_spec,
        out_shape=jax.ShapeDtypeStruct(xs.shape, F32),
        compiler_params=pltpu.CompilerParams(dimension_semantics=("arbitrary",), vmem_limit_bytes=VMEM_LIMIT),
        name="experts",
    )(tile_expert, n_used, xs, wg, wu, wd)


ROWS_PER_STEP = TOK_TILE * TOP_K


def _token_tile(ref, idx):
    return ref.at[pl.ds(pl.multiple_of(idx * TILE_ROWS, TILE_ROWS), TILE_ROWS)]


def _dispatch_kernel(n_tok_steps, t_all, pos_ref, pad_ref, x1t_hbm, xs_hbm, zero_ref, sems):
    j = pl.program_id(0)
    slot = j % 2

    @pl.when(j == 0)
    def _():
        zero_ref[...] = jnp.zeros_like(zero_ref)

    @pl.when(j < n_tok_steps)
    def _():
        def body(tt, c):
            t = j * TOK_TILE + tt
            for k in range(TOP_K):
                pltpu.make_async_copy(_token_tile(x1t_hbm, t), _token_tile(xs_hbm, pos_ref[k * t_all + t]),
                                      sems.at[slot]).start()
            return c
        lax.fori_loop(0, TOK_TILE, body, 0, unroll=8)

    @pl.when(j >= n_tok_steps)
    def _():
        def body(q, c):
            row = pad_ref[(j - n_tok_steps) * ROWS_PER_STEP + q]
            pltpu.make_async_copy(zero_ref, _token_tile(xs_hbm, row), sems.at[slot]).start()
            return c
        lax.fori_loop(0, ROWS_PER_STEP, body, 0, unroll=8)

    def wait_step(s):
        n = ROWS_PER_STEP * TILE_ROWS
        pltpu.make_async_copy(x1t_hbm.at[pl.ds(0, n)], xs_hbm.at[pl.ds(0, n)], sems.at[s]).wait()

    @pl.when(j > 0)
    def _():
        wait_step(1 - slot)

    @pl.when(j == pl.num_programs(0) - 1)
    def _():
        wait_step(slot)


def _dispatch_call(pos_flat, pad_rows, x1t, n_sorted_rows):
    t_all = x1t.shape[0] // TILE_ROWS
    n_tok_steps = t_all // TOK_TILE
    n_steps = n_tok_steps + pad_rows.shape[0] // ROWS_PER_STEP
    grid_spec = pltpu.PrefetchScalarGridSpec(
        num_scalar_prefetch=2,
        grid=(n_steps,),
        in_specs=[pl.BlockSpec(memory_space=pl.ANY)],
        out_specs=pl.BlockSpec(memory_space=pl.ANY),
        scratch_shapes=[pltpu.VMEM((TILE_ROWS, LANES), F32), pltpu.SemaphoreType.DMA((2,))],
    )
    return pl.pallas_call(
        functools.partial(_dispatch_kernel, n_tok_steps, t_all),
        grid_spec=grid_spec,
        out_shape=jax.ShapeDtypeStruct((n_sorted_rows * TILE_ROWS, LANES), F32),
        compiler_params=pltpu.CompilerParams(dimension_semantics=("arbitrary",)),
        name="dispatch",
    )(pos_flat, pad_rows, x1t)


def _combine_kernel(alpha, t_all, pos_ref, x1t_ref, w_ref, g_ref, b_ref, outs_hbm, out_ref, gbuf, sems):
    j = pl.program_id(0)
    slot = j % 2
    k_rows = TOK_TILE * TILE_ROWS

    def issue(step, s):
        def body(tt, c):
            t = step * TOK_TILE + tt
            for k in range(TOP_K):
                pltpu.make_async_copy(_token_tile(outs_hbm, pos_ref[k * t_all + t]),
                                      _token_tile(gbuf.at[s], k * TOK_TILE + tt), sems.at[s]).start()
            return c
        lax.fori_loop(0, TOK_TILE, body, 0, unroll=8)

    @pl.when(j == 0)
    def _():
        issue(0, 0)

    @pl.when(j + 1 < pl.num_programs(0))
    def _():
        issue(j + 1, 1 - slot)

    pltpu.make_async_copy(outs_hbm.at[pl.ds(0, TOP_K * k_rows)], gbuf.at[slot], sems.at[slot]).wait()

    w = w_ref[...]
    y = (w[:, 0:1] * _load_token_tiles(gbuf.at[slot], TOK_TILE)
         + w[:, 1:2] * _load_token_tiles(gbuf.at[slot], TOK_TILE, k_rows))
    x1 = _load_token_tiles(x1t_ref, TOK_TILE)
    out_ref[...] = _layer_norm(alpha * x1 + y, g_ref[...], b_ref[...])


def _combine_call(pos_flat, x1t, w_col, g, b, outs, alpha):
    t_all = x1t.shape[0] // TILE_ROWS
    grid_spec = pltpu.PrefetchScalarGridSpec(
        num_scalar_prefetch=1,
        grid=(t_all // TOK_TILE,),
        in_specs=[
            pl.BlockSpec((TOK_TILE * TILE_ROWS, LANES), lambda i, p: (i, 0)),
            pl.BlockSpec((TOK_TILE, TOP_K), lambda i, p: (i, 0)),
            pl.BlockSpec((1, D_MODEL), lambda i, p: (0, 0)),
            pl.BlockSpec((1, D_MODEL), lambda i, p: (0, 0)),
            pl.BlockSpec(memory_space=pl.ANY),
        ],
        out_specs=pl.BlockSpec((TOK_TILE, D_MODEL), lambda i, p: (i, 0)),
        scratch_shapes=[pltpu.VMEM((2, TOP_K * TOK_TILE * TILE_ROWS, LANES), F32), pltpu.SemaphoreType.DMA((2,))],
    )
    return pl.pallas_call(
        functools.partial(_combine_kernel, alpha, t_all),
        grid_spec=grid_spec,
        out_shape=jax.ShapeDtypeStruct((t_all, D_MODEL), F32),
        compiler_params=pltpu.CompilerParams(dimension_semantics=("arbitrary",), vmem_limit_bytes=VMEM_LIMIT),
        name="combine",
    )(pos_flat, x1t, w_col, g, b, outs)


def _rope_tables(seq, dec_seq, past_len):
    half = HEAD_DIM // 2
    inv = ROPE_THETA ** (-jnp.arange(half, dtype=F32) / half)
    pos = jnp.concatenate([jnp.arange(seq), past_len + (jnp.arange(TOK_TILE) % dec_seq)])
    ang = pos.astype(F32)[:, None] * inv[None, :]
    cos = jnp.cos(ang)
    sin = jnp.sin(ang)
    zero = jnp.zeros_like(sin)
    reps = LANES // HEAD_DIM
    cos_t = jnp.tile(jnp.concatenate([cos, cos], -1), (1, reps))
    sa_t = jnp.tile(jnp.concatenate([-sin, zero], -1), (1, reps))
    sb_t = jnp.tile(jnp.concatenate([zero, sin], -1), (1, reps))
    return cos_t, sa_t, sb_t


def _inclusive_cumsum(v):
    n = v.shape[0]
    tri = jnp.arange(n)[None, :] <= jnp.arange(n)[:, None]
    return jnp.sum(jnp.where(tri, v[None, :], 0), axis=1)


def _dispatch_plan(ridx, counts, n_tiles):
    t_all = ridx.shape[1]
    tiles_e = (counts + EXPERT_TILE - 1) // EXPERT_TILE
    tile_end = _inclusive_cumsum(tiles_e)
    offs = (tile_end - tiles_e) * EXPERT_TILE
    n_used = tile_end[-1:]
    tile_expert = jnp.minimum(
        jnp.sum((jnp.arange(n_tiles)[:, None] >= tile_end[None, :]).astype(jnp.int32), axis=1), N_EXPERTS - 1)
    experts = jnp.arange(N_EXPERTS)
    pos = ridx[TOP_K:2 * TOP_K] + jnp.sum(
        jnp.where(ridx[:TOP_K, :, None] == experts[None, None, :], offs[None, None, :], 0), axis=-1)
    gap_start = jnp.concatenate([offs + counts, n_used * EXPERT_TILE])
    gap_len = jnp.concatenate([tiles_e * EXPERT_TILE - counts, (n_tiles - n_used) * EXPERT_TILE])
    gap_end_q = _inclusive_cumsum(gap_len)
    q = jnp.arange(n_tiles * EXPERT_TILE - TOP_K * t_all)
    gap = jnp.sum((q[:, None] >= gap_end_q[None, :]).astype(jnp.int32), axis=1)
    in_gap = gap[:, None] == jnp.arange(N_EXPERTS + 1)[None, :]
    pad_rows = q + jnp.sum(jnp.where(in_gap, (gap_start - gap_end_q + gap_len)[None, :], 0), axis=1)
    return (pos.reshape(-1).astype(jnp.int32), pad_rows.astype(jnp.int32),
            tile_expert.astype(jnp.int32), n_used.astype(jnp.int32))


def kernel(x_prompt, x_sample, cache_k, cache_v, w_in, sinks, gm_ln_g, gm_ln_b, gm_ws, gm_bs,
           out_norm_a, out_norm_b, w_o, ln1_g, ln1_b, w_router, router_bias,
           w_gate, w_up, w_down, ln2_g, ln2_b):
    batch, seq, _ = x_prompt.shape
    dec_batch, dec_seq, _ = x_sample.shape
    depth = w_in.shape[0]
    past_len = 2048
    assert dec_batch * dec_seq == TOK_TILE and dec_seq == CHUNK and seq % TOK_TILE == 0
    assert cache_k.shape[2] == WINDOW
    alpha = (2 * depth) ** 0.25
    n_prompt = batch * seq
    t_all = n_prompt + dec_batch * dec_seq
    n_prompt_tiles = n_prompt // TOK_TILE
    tiles_per_seq = seq // TOK_TILE
    n_exp_tiles = (TOP_K * t_all) // EXPERT_TILE + N_EXPERTS

    tables = _rope_tables(seq, dec_seq, past_len)
    x = jnp.concatenate([x_prompt.reshape(n_prompt, D_MODEL), x_sample.reshape(-1, D_MODEL)], axis=0)
    wrt = w_router.T
    rb = router_bias.reshape(N_EXPERTS, 1)

    kp, vp, ks, vs, gms = [], [], [], [], []
    for l in range(depth):
        layer_args = (
            sinks[l], w_in[l].astype(BF16),
            cache_k[l].reshape(dec_batch, WINDOW, KV_WIDTH), cache_v[l].reshape(dec_batch, WINDOW, KV_WIDTH),
            gm_ln_g[l].reshape(1, GM_WIDTH), gm_ln_b[l].reshape(1, GM_WIDTH),
            gm_ws[l], gm_bs[l].T,
            out_norm_a[l].reshape(1, ATT_WIDTH), out_norm_b[l].reshape(1, GM_WIDTH),
            w_o[l].astype(BF16),
            ln1_g[l].reshape(1, D_MODEL), ln1_b[l].reshape(1, D_MODEL), wrt, rb)
        x1t, k_all, v_all, vn_s, ridx, rw, counts = _mixer_call(
            layer_args, x, tables, n_prompt_tiles, tiles_per_seq, alpha)

        pos, pad_rows, tile_expert, n_used = _dispatch_plan(ridx, counts[:, 0], n_exp_tiles)
        xs = _dispatch_call(pos, pad_rows, x1t, n_exp_tiles * EXPERT_TILE)
        outs = _expert_call(tile_expert, n_used, xs, w_gate[l], w_up[l], w_down[l])
        x = _combine_call(pos, x1t, rw[:TOP_K].T, ln2_g[l].reshape(1, D_MODEL), ln2_b[l].reshape(1, D_MODEL),
                          outs, alpha)

        k4 = k_all[:n_prompt].reshape(batch, seq, N_KV_HEADS, HEAD_DIM)
        v4 = v_all[:n_prompt].reshape(batch, seq, N_KV_HEADS, HEAD_DIM)
        kp.append(k4[:, -WINDOW:])
        vp.append(v4[:, -WINDOW:])
        ks.append(k_all[n_prompt:].reshape(dec_batch, dec_seq, N_KV_HEADS, HEAD_DIM))
        vs.append(v_all[n_prompt:].reshape(dec_batch, dec_seq, N_KV_HEADS, HEAD_DIM))
        gms.append(vn_s.reshape(dec_batch, dec_seq, GM_WIDTH))

    y_prompt = x[:n_prompt].reshape(batch, seq, D_MODEL)
    y_sample = x[n_prompt:].reshape(dec_batch, dec_seq, D_MODEL)
    return (y_prompt, y_sample, jnp.stack(kp), jnp.stack(vp), jnp.stack(ks), jnp.stack(vs), jnp.stack(gms))
```

```python
import functools

import jax
import jax.numpy as jnp
from jax import lax
from jax.experimental import pallas as pl
from jax.experimental.pallas import tpu as pltpu

D_MODEL = 1024
CHUNK = 64
WINDOW = 128
HEAD_DIM = 64
N_Q_HEADS = 8
N_KV_HEADS = 2
Q_PER_KV = N_Q_HEADS // N_KV_HEADS
PAST_LEN = 2048
ATT_WIDTH = N_Q_HEADS * HEAD_DIM
KV_WIDTH = N_KV_HEADS * HEAD_DIM
ROPE_THETA = 10000.0
GM_GROUPS = 4
GM_CH = 128
GM_WIDTH = GM_GROUPS * GM_CH
GM_CHUNK = 128
D_MIX = ATT_WIDTH + GM_WIDTH
D_IN = ATT_WIDTH + 2 * KV_WIDTH + 2 * GM_WIDTH
N_EXPERTS = 16
N_EXPERT_GROUPS = 4
EXPERTS_PER_GROUP = N_EXPERTS // N_EXPERT_GROUPS
TOP_K = 2
D_EXPERT = 512
LN_EPS = 1e-5
NEG_INF = -1e30

LANES = 128
TILE_ROWS = D_MODEL // LANES
TOK_TILE = 512
EXPERT_TILE = 256
KEYS_PER_CHUNK = WINDOW + CHUNK
VMEM_LIMIT = 56 * 1024 * 1024

F32 = jnp.float32
BF16 = jnp.bfloat16


def _layer_norm(x, g, b):
    mu = jnp.mean(x, axis=-1, keepdims=True)
    d = x - mu
    var = jnp.mean(d * d, axis=-1, keepdims=True)
    return d * lax.rsqrt(var + LN_EPS) * g + b


def _rms_norm(x, g):
    return x * lax.rsqrt(jnp.mean(x * x, axis=-1, keepdims=True) + LN_EPS) * g


def _gelu(x):
    return 0.5 * x * (1.0 + lax.erf(x * (0.5 ** 0.5)))


def _store_token_tiles(ref, x):
    n = x.shape[0]
    for c in range(TILE_ROWS):
        ref[pl.ds(c, n, stride=TILE_ROWS), :] = x[:, c * LANES:(c + 1) * LANES]


def _load_token_tiles(ref, n, row0=0):
    return jnp.concatenate([ref[pl.ds(row0 + c, n, stride=TILE_ROWS), :] for c in range(TILE_ROWS)], axis=-1)


def _first_index_of_max(rows):
    m = rows[0]
    for r in rows[1:]:
        m = jnp.maximum(m, r)
    idx = jnp.full(m.shape, len(rows), jnp.int32)
    for e in reversed(range(len(rows))):
        idx = jnp.where(rows[e] == m, e, idx)
    return m, idx


def _mixer_kernel(n_prompt_tiles, tiles_per_seq, alpha,
                  sinks_ref, x_ref, w_in_ref, cos_ref, sa_ref, sb_ref, ck_ref, cv_ref,
                  lng_ref, lnb_ref, ws_ref, bst_ref, na_ref, nb_ref, w_o_ref,
                  l1g_ref, l1b_ref, wrt_ref, rb_ref,
                  x1t_ref, k_ref, v_ref, vn_ref, ridx_ref, rw_ref, counts_ref,
                  kd_ref, vd_ref, qm_ref, att_ref, gm_ref, cnt_ref):
    i = pl.program_id(0)
    is_sample = i >= n_prompt_tiles
    seq_start = jnp.logical_and(jnp.logical_not(is_sample), (i % tiles_per_seq) == 0)
    tm = TOK_TILE

    x = x_ref[...]
    h = jnp.dot(x.astype(BF16), w_in_ref[...], preferred_element_type=F32)

    cos = cos_ref[...]
    sa = sa_ref[...]
    sb = sb_ref[...]

    def rope(blk):
        return blk * cos + pltpu.roll(blk, LANES - HEAD_DIM // 2, 1) * sa + pltpu.roll(blk, HEAD_DIM // 2, 1) * sb

    lane = lax.broadcasted_iota(jnp.int32, (1, LANES), 1)
    lo_half = lane < HEAD_DIM

    o_k = ATT_WIDTH
    o_v = o_k + KV_WIDTH
    o_u = o_v + KV_WIDTH
    o_g = o_u + GM_WIDTH

    k_rot = rope(h[:, o_k:o_v])
    v_new = h[:, o_v:o_u]
    k_ref[...] = k_rot
    v_ref[...] = v_new

    def dup_heads(a):
        sw = pltpu.roll(a, HEAD_DIM, 1)
        return jnp.where(lo_half, a, sw).astype(BF16), jnp.where(lo_half, sw, a).astype(BF16)

    k_d = dup_heads(k_rot)
    v_d = dup_heads(v_new)

    @pl.when(jnp.logical_not(is_sample))
    def _():
        @pl.when(seq_start)
        def _():
            for g in range(N_KV_HEADS):
                kd_ref[g, 0:WINDOW, :] = jnp.zeros((WINDOW, LANES), BF16)
                vd_ref[g, 0:WINDOW, :] = jnp.zeros((WINDOW, LANES), BF16)

        @pl.when(jnp.logical_not(seq_start))
        def _():
            for g in range(N_KV_HEADS):
                kd_ref[g, 0:WINDOW, :] = kd_ref[g, tm:tm + WINDOW, :]
                vd_ref[g, 0:WINDOW, :] = vd_ref[g, tm:tm + WINDOW, :]

        for g in range(N_KV_HEADS):
            kd_ref[g, WINDOW:WINDOW + tm, :] = k_d[g]
            vd_ref[g, WINDOW:WINDOW + tm, :] = v_d[g]

    @pl.when(is_sample)
    def _():
        for b in range(tm // CHUNK):
            ck = dup_heads(ck_ref[b])
            cv = dup_heads(cv_ref[b])
            base = b * KEYS_PER_CHUNK
            for g in range(N_KV_HEADS):
                kd_ref[g, base:base + WINDOW, :] = ck[g]
                vd_ref[g, base:base + WINDOW, :] = cv[g]
                kd_ref[g, base + WINDOW:base + KEYS_PER_CHUNK, :] = k_d[g][b * CHUNK:(b + 1) * CHUNK]
                vd_ref[g, base + WINDOW:base + KEYS_PER_CHUNK, :] = v_d[g][b * CHUNK:(b + 1) * CHUNK]

    scale = HEAD_DIM ** -0.5
    for b in range(ATT_WIDTH // LANES):
        qb = rope(h[:, b * LANES:(b + 1) * LANES]) * scale
        qm_ref[2 * b] = jnp.where(lo_half, qb, 0.0).astype(BF16)
        qm_ref[2 * b + 1] = jnp.where(lo_half, 0.0, qb).astype(BF16)

    key_stride = jnp.where(is_sample, KEYS_PER_CHUNK, CHUNK)
    col = lax.broadcasted_iota(jnp.int32, (1, KEYS_PER_CHUNK), 1)
    row_blk = lax.broadcasted_iota(jnp.int32, (Q_PER_KV * CHUNK, 1), 0) // CHUNK
    sink_cols = [
        jnp.where(row_blk == 0, sinks_ref[Q_PER_KV * g],
                  jnp.where(row_blk == 1, sinks_ref[Q_PER_KV * g + 1],
                            jnp.where(row_blk == 2, sinks_ref[Q_PER_KV * g + 2], sinks_ref[Q_PER_KV * g + 3])))
        for g in range(N_KV_HEADS)]

    def chunk_body(c, carry):
        r0 = pl.multiple_of(c * CHUNK, CHUNK)
        k0 = pl.multiple_of(c * key_stride, CHUNK)
        first_valid = jnp.where(seq_start, (WINDOW // CHUNK - c) * CHUNK, 0)
        valid = col >= first_valid
        outs = []
        for g in range(N_KV_HEADS):
            q_st = jnp.concatenate([qm_ref[Q_PER_KV * g + r, pl.ds(r0, CHUNK), :] for r in range(Q_PER_KV)],
                                   axis=0)
            keys = kd_ref[g, pl.ds(k0, KEYS_PER_CHUNK), :]
            vals = vd_ref[g, pl.ds(k0, KEYS_PER_CHUNK), :]
            s = lax.dot_general(q_st, keys, (((1,), (1,)), ((), ())), preferred_element_type=F32)
            s = jnp.where(valid, s, NEG_INF)
            sink = sink_cols[g]
            m = jnp.maximum(jnp.max(s, axis=-1, keepdims=True), sink)
            p = jnp.exp(s - m)
            denom = jnp.sum(p, axis=-1, keepdims=True) + jnp.exp(sink - m)
            o = jnp.dot(p.astype(BF16), vals, preferred_element_type=F32) * (1.0 / denom)
            for bb in range(2):
                outs.append(jnp.where(lo_half, o[(2 * bb) * CHUNK:(2 * bb + 1) * CHUNK],
                                      o[(2 * bb + 1) * CHUNK:(2 * bb + 2) * CHUNK]))
        for b in range(ATT_WIDTH // LANES):
            att_ref[pl.ds(r0, CHUNK), b * LANES:(b + 1) * LANES] = outs[b]
        return carry

    lax.fori_loop(0, tm // CHUNK, chunk_body, 0, unroll=2)

    prow = lax.broadcasted_iota(jnp.int32, (GM_CHUNK, GM_CHUNK), 0)
    pcol = lax.broadcasted_iota(jnp.int32, (GM_CHUNK, GM_CHUNK), 1)
    half = GM_CHUNK // 2
    tril = pcol <= prow
    same_blk = (prow < half) == (pcol < half)
    prow1 = lax.broadcasted_iota(jnp.int32, (GM_CHUNK, 1), 0)
    bst = bst_ref[...]
    bst_s = jnp.where(prow1 < half, bst, pltpu.roll(bst, half, 0))
    bias = jnp.where(is_sample, bst_s, bst)
    for g in range(GM_GROUPS):
        wg_ = ws_ref[g]
        w_s = jnp.where(prow < half, wg_, pltpu.roll(pltpu.roll(wg_, half, 0), half, 1))
        w_eff = jnp.where(is_sample, jnp.where(same_blk, w_s, 0.0), wg_)
        w_eff = jnp.where(tril, w_eff, 0.0).astype(BF16)
        cols = slice(g * GM_CH, (g + 1) * GM_CH)
        ug = _gelu(h[:, o_u + g * GM_CH:o_u + (g + 1) * GM_CH])
        vn = _layer_norm(_gelu(h[:, o_g + g * GM_CH:o_g + (g + 1) * GM_CH]), lng_ref[:, cols], lnb_ref[:, cols])

        @pl.when(is_sample)
        def _():
            vn_ref[:, cols] = vn

        vn_b = vn.astype(BF16)
        for n in range(tm // GM_CHUNK):
            rows = slice(n * GM_CHUNK, (n + 1) * GM_CHUNK)
            s = jnp.dot(w_eff, vn_b[rows], preferred_element_type=F32) + bias[:, g:g + 1]
            gm_ref[rows, cols] = ug[rows] * s

    mixed = jnp.concatenate([_rms_norm(att_ref[...], na_ref[...]), _rms_norm(gm_ref[...], nb_ref[...])], axis=-1)
    y = alpha * x + jnp.dot(mixed.astype(BF16), w_o_ref[...], preferred_element_type=F32)
    x1 = _layer_norm(y, l1g_ref[...], l1b_ref[...])
    _store_token_tiles(x1t_ref, x1)

    logits = lax.dot_general(wrt_ref[...], x1, (((1,), (1,)), ((), ())),
                             preferred_element_type=F32, precision=lax.Precision.HIGHEST)
    scores = jax.nn.sigmoid(logits)
    sel = scores + rb_ref[...]
    sel_rows = [sel[e:e + 1, :] for e in range(N_EXPERTS)]
    sc_rows = [scores[e:e + 1, :] for e in range(N_EXPERTS)]
    grp = []
    for g in range(N_EXPERT_GROUPS):
        r = sel_rows[g * EXPERTS_PER_GROUP:(g + 1) * EXPERTS_PER_GROUP]
        best_pair = None
        for a in range(EXPERTS_PER_GROUP):
            for b in range(a + 1, EXPERTS_PER_GROUP):
                pair = r[a] + r[b]
                best_pair = pair if best_pair is None else jnp.maximum(best_pair, pair)
        grp.append(best_pair)
    _, best = _first_index_of_max(grp)
    masked = [jnp.where(best == (e // EXPERTS_PER_GROUP), sel_rows[e], NEG_INF) for e in range(N_EXPERTS)]
    _, e0 = _first_index_of_max(masked)
    masked2 = [jnp.where(e0 == e, -jnp.inf, masked[e]) for e in range(N_EXPERTS)]
    _, e1 = _first_index_of_max(masked2)
    w0 = jnp.zeros_like(sc_rows[0])
    w1 = jnp.zeros_like(sc_rows[0])
    for e in range(N_EXPERTS):
        w0 = jnp.where(e0 == e, sc_rows[e], w0)
        w1 = jnp.where(e1 == e, sc_rows[e], w1)
    wsum = w0 + w1
    rw_ref[...] = jnp.concatenate([w0 / wsum, w1 / wsum, jnp.zeros((6, tm), F32)], axis=0)

    @pl.when(i == 0)
    def _():
        cnt_ref[...] = jnp.zeros_like(cnt_ref)

    eid = lax.broadcasted_iota(jnp.int32, (N_EXPERTS, tm), 0)
    oh0 = (eid == e0).astype(F32)
    oh1 = (eid == e1).astype(F32)
    oh = oh0 + oh1
    upper = (lax.broadcasted_iota(jnp.int32, (tm, tm), 0) <= lax.broadcasted_iota(jnp.int32, (tm, tm), 1))
    incl = jnp.dot(oh.astype(BF16), upper.astype(BF16), preferred_element_type=F32)
    before = cnt_ref[:, 0:1] + incl - oh
    r0 = jnp.sum(oh0 * before, axis=0, keepdims=True).astype(jnp.int32)
    r1 = jnp.sum(oh1 * before, axis=0, keepdims=True).astype(jnp.int32)
    cnt = cnt_ref[...] + incl[:, tm - 1:tm]
    cnt_ref[...] = cnt
    counts_ref[...] = cnt.astype(jnp.int32)
    ridx_ref[...] = jnp.concatenate([e0, e1, r0, r1, jnp.zeros((4, tm), jnp.int32)], axis=0)


def _mixer_call(layer_args, x_all, tables, n_prompt_tiles, tiles_per_seq, alpha):
    (sinks, w_in, ck, cv, lng, lnb, ws, bst, na, nb, w_o, l1g, l1b, wrt, rb) = layer_args
    cos_t, sa_t, sb_t = tables
    t_all = x_all.shape[0]
    n_tiles = t_all // TOK_TILE
    n_seq_tiles = tiles_per_seq

    def const(shape):
        nd = len(shape)
        return pl.BlockSpec(shape, lambda i, _nd=nd: (0,) * _nd)

    def tab_map(i):
        return (jnp.where(i < n_prompt_tiles, i % n_seq_tiles, n_seq_tiles), 0)

    row_blk = lambda w: pl.BlockSpec((TOK_TILE, w), lambda i: (i, 0))
    in_specs = [
        pl.BlockSpec(memory_space=pltpu.SMEM),
        row_blk(D_MODEL),
        const((D_MODEL, D_IN)),
        pl.BlockSpec((TOK_TILE, LANES), tab_map),
        pl.BlockSpec((TOK_TILE, LANES), tab_map),
        pl.BlockSpec((TOK_TILE, LANES), tab_map),
        const(ck.shape), const(cv.shape),
        const((1, GM_WIDTH)), const((1, GM_WIDTH)),
        const((GM_GROUPS, GM_CHUNK, GM_CHUNK)), const((GM_CHUNK, GM_GROUPS)),
        const((1, ATT_WIDTH)), const((1, GM_WIDTH)),
        const((D_MIX, D_MODEL)),
        const((1, D_MODEL)), const((1, D_MODEL)),
        const((N_EXPERTS, D_MODEL)), const((N_EXPERTS, 1)),
    ]
    out_shape = [
        jax.ShapeDtypeStruct((t_all * TILE_ROWS, LANES), F32),
        jax.ShapeDtypeStruct((t_all, KV_WIDTH), F32),
        jax.ShapeDtypeStruct((t_all, KV_WIDTH), F32),
        jax.ShapeDtypeStruct((TOK_TILE, GM_WIDTH), F32),
        jax.ShapeDtypeStruct((8, t_all), jnp.int32),
        jax.ShapeDtypeStruct((8, t_all), F32),
        jax.ShapeDtypeStruct((N_EXPERTS, LANES), jnp.int32),
    ]
    out_specs = [
        pl.BlockSpec((TOK_TILE * TILE_ROWS, LANES), lambda i: (i, 0)),
        row_blk(KV_WIDTH), row_blk(KV_WIDTH),
        const((TOK_TILE, GM_WIDTH)),
        pl.BlockSpec((8, TOK_TILE), lambda i: (0, i)),
        pl.BlockSpec((8, TOK_TILE), lambda i: (0, i)),
        const((N_EXPERTS, LANES)),
    ]
    kd_rows = (TOK_TILE // CHUNK) * KEYS_PER_CHUNK
    scratch = [
        pltpu.VMEM((N_KV_HEADS, kd_rows, LANES), BF16),
        pltpu.VMEM((N_KV_HEADS, kd_rows, LANES), BF16),
        pltpu.VMEM((N_Q_HEADS, TOK_TILE, LANES), BF16),
        pltpu.VMEM((TOK_TILE, ATT_WIDTH), F32),
        pltpu.VMEM((TOK_TILE, GM_WIDTH), F32),
        pltpu.VMEM((N_EXPERTS, LANES), F32),
    ]
    return pl.pallas_call(
        functools.partial(_mixer_kernel, n_prompt_tiles, tiles_per_seq, alpha),
        grid=(n_tiles,),
        in_specs=in_specs, out_specs=out_specs, out_shape=out_shape,
        scratch_shapes=scratch,
        compiler_params=pltpu.CompilerParams(dimension_semantics=("arbitrary",), vmem_limit_bytes=VMEM_LIMIT),
        name="mixer",
    )(sinks, x_all, w_in, cos_t, sa_t, sb_t, ck, cv, lng, lnb, ws, bst, na, nb, w_o, l1g, l1b, wrt, rb)


def _expert_kernel(te_ref, nt_ref, xs_ref, wg_ref, wu_ref, wd_ref, out_ref, wg_s, wu_s, wd_s):
    i = pl.program_id(0)
    changed = jnp.logical_or(i == 0, te_ref[i] != te_ref[jnp.maximum(i - 1, 0)])

    @pl.when(changed)
    def _():
        wg_s[...] = wg_ref[0].astype(BF16)
        wu_s[...] = wu_ref[0].astype(BF16)
        wd_s[...] = wd_ref[0].astype(BF16)

    @pl.when(i < nt_ref[0])
    def _():
        xs = _load_token_tiles(xs_ref, EXPERT_TILE).astype(BF16)
        g = jnp.dot(xs, wg_s[...], preferred_element_type=F32)
        u = jnp.dot(xs, wu_s[...], preferred_element_type=F32)
        hmid = (g * jax.nn.sigmoid(g)) * u
        _store_token_tiles(out_ref, jnp.dot(hmid.astype(BF16), wd_s[...], preferred_element_type=F32))

    @pl.when(i >= nt_ref[0])
    def _():
        out_ref[...] = jnp.zeros_like(out_ref)


def _expert_call(tile_expert, n_used, xs, wg, wu, wd):
    n_tiles = xs.shape[0] // (EXPERT_TILE * TILE_ROWS)
    tile_blk = pl.BlockSpec((EXPERT_TILE * TILE_ROWS, LANES), lambda i, te, nt: (i, 0))
    grid_spec = pltpu.PrefetchScalarGridSpec(
        num_scalar_prefetch=2,
        grid=(n_tiles,),
        in_specs=[
            tile_blk,
            pl.BlockSpec((1, D_MODEL, D_EXPERT), lambda i, te, nt: (te[i], 0, 0)),
            pl.BlockSpec((1, D_MODEL, D_EXPERT), lambda i, te, nt: (te[i], 0, 0)),
            pl.BlockSpec((1, D_EXPERT, D_MODEL), lambda i, te, nt: (te[i], 0, 0)),
        ],
        out_specs=tile_blk,
        scratch_shapes=[
            pltpu.VMEM((D_MODEL, D_EXPERT), BF16),
            pltpu.VMEM((D_MODEL, D_EXPERT), BF16),
            pltpu.VMEM((D_EXPERT, D_MODEL), BF16),
        ],
    )
    return pl.pallas_call(
        _expert_kernel,
        grid_spec=grid_spec,
        out_shape=jax.ShapeDtypeStruct(xs.shape, F32),
        compiler_params=pltpu.CompilerParams(dimension_semantics=("arbitrary",), vmem_limit_bytes=VMEM_LIMIT),
        name="experts",
    )(tile_expert, n_used, xs, wg, wu, wd)


ROWS_PER_STEP = TOK_TILE * TOP_K


def _token_tile(ref, idx):
    return ref.at[pl.ds(pl.multiple_of(idx * TILE_ROWS, TILE_ROWS), TILE_ROWS)]


def _dispatch_kernel(n_tok_steps, t_all, pos_ref, pad_ref, x1t_ref, xs_hbm, zero_ref, sems):
    j = pl.program_id(0)

    @pl.when(j == 0)
    def _():
        zero_ref[...] = jnp.zeros_like(zero_ref)

    @pl.when(j < n_tok_steps)
    def _():
        def body(tt, c):
            t = j * TOK_TILE + tt
            for k in range(TOP_K):
                pltpu.make_async_copy(_token_tile(x1t_ref, tt), _token_tile(xs_hbm, pos_ref[k * t_all + t]),
                                      sems.at[k]).start()
            return c
        lax.fori_loop(0, TOK_TILE, body, 0, unroll=8)

    @pl.when(j >= n_tok_steps)
    def _():
        def body(q, c):
            for k in range(TOP_K):
                row = pad_ref[(j - n_tok_steps) * ROWS_PER_STEP + k * TOK_TILE + q]
                pltpu.make_async_copy(zero_ref, _token_tile(xs_hbm, row), sems.at[k]).start()
            return c
        lax.fori_loop(0, TOK_TILE, body, 0, unroll=8)

    for k in range(TOP_K):
        pltpu.make_async_copy(x1t_ref, xs_hbm.at[pl.ds(0, TOK_TILE * TILE_ROWS)], sems.at[k]).wait()


def _dispatch_call(pos_flat, pad_rows, x1t, n_sorted_rows):
    t_all = x1t.shape[0] // TILE_ROWS
    n_tok_steps = t_all // TOK_TILE
    n_steps = n_tok_steps + pad_rows.shape[0] // ROWS_PER_STEP
    grid_spec = pltpu.PrefetchScalarGridSpec(
        num_scalar_prefetch=2,
        grid=(n_steps,),
        in_specs=[pl.BlockSpec((TOK_TILE * TILE_ROWS, LANES),
                               lambda j, p, q: (jnp.minimum(j, n_tok_steps - 1), 0))],
        out_specs=pl.BlockSpec(memory_space=pl.ANY),
        scratch_shapes=[pltpu.VMEM((TILE_ROWS, LANES), F32), pltpu.SemaphoreType.DMA((TOP_K,))],
    )
    return pl.pallas_call(
        functools.partial(_dispatch_kernel, n_tok_steps, t_all),
        grid_spec=grid_spec,
        out_shape=jax.ShapeDtypeStruct((n_sorted_rows * TILE_ROWS, LANES), F32),
        compiler_params=pltpu.CompilerParams(dimension_semantics=("arbitrary",)),
        name="dispatch",
    )(pos_flat, pad_rows, x1t)


def _combine_kernel(alpha, t_all, pos_ref, x1t_ref, w_ref, g_ref, b_ref, outs_hbm, out_ref, gbuf, sems):
    j = pl.program_id(0)
    slot = j % 2
    k_rows = TOK_TILE * TILE_ROWS

    def issue(step, s):
        def body(tt, c):
            t = step * TOK_TILE + tt
            for k in range(TOP_K):
                pltpu.make_async_copy(_token_tile(outs_hbm, pos_ref[k * t_all + t]),
                                      _token_tile(gbuf.at[s], k * TOK_TILE + tt), sems.at[s]).start()
            return c
        lax.fori_loop(0, TOK_TILE, body, 0, unroll=8)

    @pl.when(j == 0)
    def _():
        issue(0, 0)

    @pl.when(j + 1 < pl.num_programs(0))
    def _():
        issue(j + 1, 1 - slot)

    pltpu.make_async_copy(outs_hbm.at[pl.ds(0, TOP_K * k_rows)], gbuf.at[slot], sems.at[slot]).wait()

    w = w_ref[...]
    y = (w[:, 0:1] * _load_token_tiles(gbuf.at[slot], TOK_TILE)
         + w[:, 1:2] * _load_token_tiles(gbuf.at[slot], TOK_TILE, k_rows))
    x1 = _load_token_tiles(x1t_ref, TOK_TILE)
    out_ref[...] = _layer_norm(alpha * x1 + y, g_ref[...], b_ref[...])


def _combine_call(pos_flat, x1t, w_col, g, b, outs, alpha):
    t_all = x1t.shape[0] // TILE_ROWS
    grid_spec = pltpu.PrefetchScalarGridSpec(
        num_scalar_prefetch=1,
        grid=(t_all // TOK_TILE,),
        in_specs=[
            pl.BlockSpec((TOK_TILE * TILE_ROWS, LANES), lambda i, p: (i, 0)),
            pl.BlockSpec((TOK_TILE, TOP_K), lambda i, p: (i, 0)),
            pl.BlockSpec((1, D_MODEL), lambda i, p: (0, 0)),
            pl.BlockSpec((1, D_MODEL), lambda i, p: (0, 0)),
            pl.BlockSpec(memory_space=pl.ANY),
        ],
        out_specs=pl.BlockSpec((TOK_TILE, D_MODEL), lambda i, p: (i, 0)),
        scratch_shapes=[pltpu.VMEM((2, TOP_K * TOK_TILE * TILE_ROWS, LANES), F32), pltpu.SemaphoreType.DMA((2,))],
    )
    return pl.pallas_call(
        functools.partial(_combine_kernel, alpha, t_all),
        grid_spec=grid_spec,
        out_shape=jax.ShapeDtypeStruct((t_all, D_MODEL), F32),
        compiler_params=pltpu.CompilerParams(dimension_semantics=("arbitrary",), vmem_limit_bytes=VMEM_LIMIT),
        name="combine",
    )(pos_flat, x1t, w_col, g, b, outs)


def _rope_tables(seq, dec_seq, past_len):
    half = HEAD_DIM // 2
    inv = ROPE_THETA ** (-jnp.arange(half, dtype=F32) / half)
    pos = jnp.concatenate([jnp.arange(seq), past_len + (jnp.arange(TOK_TILE) % dec_seq)])
    ang = pos.astype(F32)[:, None] * inv[None, :]
    cos = jnp.cos(ang)
    sin = jnp.sin(ang)
    zero = jnp.zeros_like(sin)
    reps = LANES // HEAD_DIM
    cos_t = jnp.tile(jnp.concatenate([cos, cos], -1), (1, reps))
    sa_t = jnp.tile(jnp.concatenate([-sin, zero], -1), (1, reps))
    sb_t = jnp.tile(jnp.concatenate([zero, sin], -1), (1, reps))
    return cos_t, sa_t, sb_t


def _inclusive_cumsum(v):
    n = v.shape[0]
    tri = jnp.arange(n)[None, :] <= jnp.arange(n)[:, None]
    return jnp.sum(jnp.where(tri, v[None, :], 0), axis=1)


def _dispatch_plan(ridx, counts, n_tiles):
    t_all = ridx.shape[1]
    tiles_e = (counts + EXPERT_TILE - 1) // EXPERT_TILE
    tile_end = _inclusive_cumsum(tiles_e)
    offs = (tile_end - tiles_e) * EXPERT_TILE
    n_used = tile_end[-1:]
    tile_expert = jnp.minimum(
        jnp.sum((jnp.arange(n_tiles)[:, None] >= tile_end[None, :]).astype(jnp.int32), axis=1), N_EXPERTS - 1)
    experts = jnp.arange(N_EXPERTS)
    pos = ridx[TOP_K:2 * TOP_K] + jnp.sum(
        jnp.where(ridx[:TOP_K, :, None] == experts[None, None, :], offs[None, None, :], 0), axis=-1)
    gap_start = jnp.concatenate([offs + counts, n_used * EXPERT_TILE])
    gap_len = jnp.concatenate([tiles_e * EXPERT_TILE - counts, (n_tiles - n_used) * EXPERT_TILE])
    gap_end_q = _inclusive_cumsum(gap_len)
    q = jnp.arange(n_tiles * EXPERT_TILE - TOP_K * t_all)
    gap = jnp.sum((q[:, None] >= gap_end_q[None, :]).astype(jnp.int32), axis=1)
    in_gap = gap[:, None] == jnp.arange(N_EXPERTS + 1)[None, :]
    pad_rows = q + jnp.sum(jnp.where(in_gap, (gap_start - gap_end_q + gap_len)[None, :], 0), axis=1)
    return (pos.reshape(-1).astype(jnp.int32), pad_rows.astype(jnp.int32),
            tile_expert.astype(jnp.int32), n_used.astype(jnp.int32))


def kernel(x_prompt, x_sample, cache_k, cache_v, w_in, sinks, gm_ln_g, gm_ln_b, gm_ws, gm_bs,
           out_norm_a, out_norm_b, w_o, ln1_g, ln1_b, w_router, router_bias,
           w_gate, w_up, w_down, ln2_g, ln2_b):
    batch, seq, _ = x_prompt.shape
    dec_batch, dec_seq, _ = x_sample.shape
    depth = w_in.shape[0]
    past_len = PAST_LEN
    assert dec_batch * dec_seq == TOK_TILE and dec_seq == CHUNK and seq % TOK_TILE == 0
    assert cache_k.shape[2] == WINDOW
    alpha = (2 * depth) ** 0.25
    n_prompt = batch * seq
    t_all = n_prompt + dec_batch * dec_seq
    n_prompt_tiles = n_prompt // TOK_TILE
    tiles_per_seq = seq // TOK_TILE
    n_exp_tiles = (TOP_K * t_all) // EXPERT_TILE + N_EXPERTS

    tables = _rope_tables(seq, dec_seq, past_len)
    x = jnp.concatenate([x_prompt.reshape(n_prompt, D_MODEL), x_sample.reshape(-1, D_MODEL)], axis=0)
    wrt = w_router.T
    rb = router_bias.reshape(N_EXPERTS, 1)

    kp, vp, ks, vs, gms = [], [], [], [], []
    for l in range(depth):
        layer_args = (
            sinks[l], w_in[l].astype(BF16),
            cache_k[l].reshape(dec_batch, WINDOW, KV_WIDTH), cache_v[l].reshape(dec_batch, WINDOW, KV_WIDTH),
            gm_ln_g[l].reshape(1, GM_WIDTH), gm_ln_b[l].reshape(1, GM_WIDTH),
            gm_ws[l], gm_bs[l].T,
            out_norm_a[l].reshape(1, ATT_WIDTH), out_norm_b[l].reshape(1, GM_WIDTH),
            w_o[l].astype(BF16),
            ln1_g[l].reshape(1, D_MODEL), ln1_b[l].reshape(1, D_MODEL), wrt, rb)
        x1t, k_all, v_all, vn_s, ridx, rw, counts = _mixer_call(
            layer_args, x, tables, n_prompt_tiles, tiles_per_seq, alpha)

        pos, pad_rows, tile_expert, n_used = _dispatch_plan(ridx, counts[:, 0], n_exp_tiles)
        xs = _dispatch_call(pos, pad_rows, x1t, n_exp_tiles * EXPERT_TILE)
        outs = _expert_call(tile_expert, n_used, xs, w_gate[l], w_up[l], w_down[l])
        x = _combine_call(pos, x1t, rw[:TOP_K].T, ln2_g[l].reshape(1, D_MODEL), ln2_b[l].reshape(1, D_MODEL),
                          outs, alpha)

        k4 = k_all[:n_prompt].reshape(batch, seq, N_KV_HEADS, HEAD_DIM)
        v4 = v_all[:n_prompt].reshape(batch, seq, N_KV_HEADS, HEAD_DIM)
        kp.append(k4[:, -WINDOW:])
        vp.append(v4[:, -WINDOW:])
        ks.append(k_all[n_prompt:].reshape(dec_batch, dec_seq, N_KV_HEADS, HEAD_DIM))
        vs.append(v_all[n_prompt:].reshape(dec_batch, dec_seq, N_KV_HEADS, HEAD_DIM))
        gms.append(vn_s.reshape(dec_batch, dec_seq, GM_WIDTH))

    y_prompt = x[:n_prompt].reshape(batch, seq, D_MODEL)
    y_sample = x[n_prompt:].reshape(dec_batch, dec_seq, D_MODEL)
    return (y_prompt, y_sample, jnp.stack(kp), jnp.stack(vp), jnp.stack(ks), jnp.stack(vs), jnp.stack(gms))
```

```python
import functools

import jax
import jax.numpy as jnp
from jax import lax
from jax.experimental import pallas as pl
from jax.experimental.pallas import tpu as pltpu

D_MODEL = 1024
CHUNK = 64
WINDOW = 128
HEAD_DIM = 64
N_Q_HEADS = 8
N_KV_HEADS = 2
Q_PER_KV = N_Q_HEADS // N_KV_HEADS
PAST_LEN = 2048
ATT_WIDTH = N_Q_HEADS * HEAD_DIM
KV_WIDTH = N_KV_HEADS * HEAD_DIM
ROPE_THETA = 10000.0
GM_GROUPS = 4
GM_CH = 128
GM_WIDTH = GM_GROUPS * GM_CH
GM_CHUNK = 128
D_MIX = ATT_WIDTH + GM_WIDTH
D_IN = ATT_WIDTH + 2 * KV_WIDTH + 2 * GM_WIDTH
N_EXPERTS = 16
N_EXPERT_GROUPS = 4
EXPERTS_PER_GROUP = N_EXPERTS // N_EXPERT_GROUPS
TOP_K = 2
D_EXPERT = 512
LN_EPS = 1e-5
NEG_INF = -1e30

LANES = 128
TILE_ROWS = D_MODEL // LANES
TOK_TILE = 512
EXPERT_TILE = 512
KEYS_PER_CHUNK = WINDOW + CHUNK
VMEM_LIMIT = 56 * 1024 * 1024

F32 = jnp.float32
BF16 = jnp.bfloat16


def _layer_norm(x, g, b):
    mu = jnp.mean(x, axis=-1, keepdims=True)
    d = x - mu
    var = jnp.mean(d * d, axis=-1, keepdims=True)
    return d * lax.rsqrt(var + LN_EPS) * g + b


def _rms_norm(x, g):
    return x * lax.rsqrt(jnp.mean(x * x, axis=-1, keepdims=True) + LN_EPS) * g


def _gelu(x):
    return 0.5 * x * (1.0 + lax.erf(x * (0.5 ** 0.5)))


def _store_token_tiles(ref, x):
    n = x.shape[0]
    for c in range(TILE_ROWS):
        ref[pl.ds(c, n, stride=TILE_ROWS), :] = x[:, c * LANES:(c + 1) * LANES]


def _load_token_tiles(ref, n, row0=0):
    return jnp.concatenate([ref[pl.ds(row0 + c, n, stride=TILE_ROWS), :] for c in range(TILE_ROWS)], axis=-1)


def _first_index_of_max(rows):
    m = rows[0]
    for r in rows[1:]:
        m = jnp.maximum(m, r)
    idx = jnp.full(m.shape, len(rows), jnp.int32)
    for e in reversed(range(len(rows))):
        idx = jnp.where(rows[e] == m, e, idx)
    return m, idx


def _mixer_kernel(n_prompt_tiles, tiles_per_seq, alpha,
                  sinks_ref, x_ref, w_in_ref, cos_ref, sa_ref, sb_ref, ck_ref, cv_ref,
                  lng_ref, lnb_ref, ws_ref, bst_ref, na_ref, nb_ref, w_o_ref,
                  l1g_ref, l1b_ref, wrt_ref, rb_ref,
                  x1t_ref, k_ref, v_ref, vn_ref, ridx_ref, rw_ref, counts_ref,
                  kd_ref, vd_ref, qm_ref, att_ref, gm_ref, cnt_ref):
    i = pl.program_id(0)
    is_sample = i >= n_prompt_tiles
    seq_start = jnp.logical_and(jnp.logical_not(is_sample), (i % tiles_per_seq) == 0)
    tm = TOK_TILE

    x = x_ref[...]
    h = jnp.dot(x.astype(BF16), w_in_ref[...], preferred_element_type=F32)

    cos = cos_ref[...]
    sa = sa_ref[...]
    sb = sb_ref[...]

    def rope(blk):
        return blk * cos + pltpu.roll(blk, LANES - HEAD_DIM // 2, 1) * sa + pltpu.roll(blk, HEAD_DIM // 2, 1) * sb

    lane = lax.broadcasted_iota(jnp.int32, (1, LANES), 1)
    lo_half = lane < HEAD_DIM

    o_k = ATT_WIDTH
    o_v = o_k + KV_WIDTH
    o_u = o_v + KV_WIDTH
    o_g = o_u + GM_WIDTH

    k_rot = rope(h[:, o_k:o_v])
    v_new = h[:, o_v:o_u]
    k_ref[...] = k_rot
    v_ref[...] = v_new

    def dup_heads(a):
        sw = pltpu.roll(a, HEAD_DIM, 1)
        return jnp.where(lo_half, a, sw).astype(BF16), jnp.where(lo_half, sw, a).astype(BF16)

    k_d = dup_heads(k_rot)
    v_d = dup_heads(v_new)

    @pl.when(jnp.logical_not(is_sample))
    def _():
        @pl.when(seq_start)
        def _():
            for g in range(N_KV_HEADS):
                kd_ref[g, 0:WINDOW, :] = jnp.zeros((WINDOW, LANES), BF16)
                vd_ref[g, 0:WINDOW, :] = jnp.zeros((WINDOW, LANES), BF16)

        @pl.when(jnp.logical_not(seq_start))
        def _():
            for g in range(N_KV_HEADS):
                kd_ref[g, 0:WINDOW, :] = kd_ref[g, tm:tm + WINDOW, :]
                vd_ref[g, 0:WINDOW, :] = vd_ref[g, tm:tm + WINDOW, :]

        for g in range(N_KV_HEADS):
            kd_ref[g, WINDOW:WINDOW + tm, :] = k_d[g]
            vd_ref[g, WINDOW:WINDOW + tm, :] = v_d[g]

    @pl.when(is_sample)
    def _():
        for b in range(tm // CHUNK):
            ck = dup_heads(ck_ref[b])
            cv = dup_heads(cv_ref[b])
            base = b * KEYS_PER_CHUNK
            for g in range(N_KV_HEADS):
                kd_ref[g, base:base + WINDOW, :] = ck[g]
                vd_ref[g, base:base + WINDOW, :] = cv[g]
                kd_ref[g, base + WINDOW:base + KEYS_PER_CHUNK, :] = k_d[g][b * CHUNK:(b + 1) * CHUNK]
                vd_ref[g, base + WINDOW:base + KEYS_PER_CHUNK, :] = v_d[g][b * CHUNK:(b + 1) * CHUNK]

    scale = HEAD_DIM ** -0.5
    for b in range(ATT_WIDTH // LANES):
        qb = rope(h[:, b * LANES:(b + 1) * LANES]) * scale
        qm_ref[2 * b] = jnp.where(lo_half, qb, 0.0).astype(BF16)
        qm_ref[2 * b + 1] = jnp.where(lo_half, 0.0, qb).astype(BF16)

    key_stride = jnp.where(is_sample, KEYS_PER_CHUNK, CHUNK)
    col = lax.broadcasted_iota(jnp.int32, (1, KEYS_PER_CHUNK), 1)
    row_blk = lax.broadcasted_iota(jnp.int32, (Q_PER_KV * CHUNK, 1), 0) // CHUNK
    sink_cols = [
        jnp.where(row_blk == 0, sinks_ref[Q_PER_KV * g],
                  jnp.where(row_blk == 1, sinks_ref[Q_PER_KV * g + 1],
                            jnp.where(row_blk == 2, sinks_ref[Q_PER_KV * g + 2], sinks_ref[Q_PER_KV * g + 3])))
        for g in range(N_KV_HEADS)]

    def chunk_body(c, carry):
        r0 = pl.multiple_of(c * CHUNK, CHUNK)
        k0 = pl.multiple_of(c * key_stride, CHUNK)
        first_valid = jnp.where(seq_start, (WINDOW // CHUNK - c) * CHUNK, 0)
        valid = col >= first_valid
        outs = []
        for g in range(N_KV_HEADS):
            q_st = jnp.concatenate([qm_ref[Q_PER_KV * g + r, pl.ds(r0, CHUNK), :] for r in range(Q_PER_KV)],
                                   axis=0)
            keys = kd_ref[g, pl.ds(k0, KEYS_PER_CHUNK), :]
            vals = vd_ref[g, pl.ds(k0, KEYS_PER_CHUNK), :]
            s = lax.dot_general(q_st, keys, (((1,), (1,)), ((), ())), preferred_element_type=F32)
            s = jnp.where(valid, s, NEG_INF)
            sink = sink_cols[g]
            m = jnp.maximum(jnp.max(s, axis=-1, keepdims=True), sink)
            p = jnp.exp(s - m)
            denom = jnp.sum(p, axis=-1, keepdims=True) + jnp.exp(sink - m)
            o = jnp.dot(p.astype(BF16), vals, preferred_element_type=F32) * (1.0 / denom)
            for bb in range(2):
                outs.append(jnp.where(lo_half, o[(2 * bb) * CHUNK:(2 * bb + 1) * CHUNK],
                                      o[(2 * bb + 1) * CHUNK:(2 * bb + 2) * CHUNK]))
        for b in range(ATT_WIDTH // LANES):
            att_ref[pl.ds(r0, CHUNK), b * LANES:(b + 1) * LANES] = outs[b]
        return carry

    lax.fori_loop(0, tm // CHUNK, chunk_body, 0, unroll=2)

    prow = lax.broadcasted_iota(jnp.int32, (GM_CHUNK, GM_CHUNK), 0)
    pcol = lax.broadcasted_iota(jnp.int32, (GM_CHUNK, GM_CHUNK), 1)
    half = GM_CHUNK // 2
    tril = pcol <= prow
    same_blk = (prow < half) == (pcol < half)
    prow1 = lax.broadcasted_iota(jnp.int32, (GM_CHUNK, 1), 0)
    bst = bst_ref[...]
    bst_s = jnp.where(prow1 < half, bst, pltpu.roll(bst, half, 0))
    bias = jnp.where(is_sample, bst_s, bst)
    for g in range(GM_GROUPS):
        wg_ = ws_ref[g]
        w_s = jnp.where(prow < half, wg_, pltpu.roll(pltpu.roll(wg_, half, 0), half, 1))
        w_eff = jnp.where(is_sample, jnp.where(same_blk, w_s, 0.0), wg_)
        w_eff = jnp.where(tril, w_eff, 0.0).astype(BF16)
        cols = slice(g * GM_CH, (g + 1) * GM_CH)
        ug = _gelu(h[:, o_u + g * GM_CH:o_u + (g + 1) * GM_CH])
        vn = _layer_norm(_gelu(h[:, o_g + g * GM_CH:o_g + (g + 1) * GM_CH]), lng_ref[:, cols], lnb_ref[:, cols])

        @pl.when(is_sample)
        def _():
            vn_ref[:, cols] = vn

        vn_b = vn.astype(BF16)
        for n in range(tm // GM_CHUNK):
            rows = slice(n * GM_CHUNK, (n + 1) * GM_CHUNK)
            s = jnp.dot(w_eff, vn_b[rows], preferred_element_type=F32) + bias[:, g:g + 1]
            gm_ref[rows, cols] = ug[rows] * s

    mixed = jnp.concatenate([_rms_norm(att_ref[...], na_ref[...]), _rms_norm(gm_ref[...], nb_ref[...])], axis=-1)
    y = alpha * x + jnp.dot(mixed.astype(BF16), w_o_ref[...], preferred_element_type=F32)
    x1 = _layer_norm(y, l1g_ref[...], l1b_ref[...])
    _store_token_tiles(x1t_ref, x1)

    logits = lax.dot_general(wrt_ref[...], x1, (((1,), (1,)), ((), ())),
                             preferred_element_type=F32, precision=lax.Precision.HIGHEST)
    scores = jax.nn.sigmoid(logits)
    sel = scores + rb_ref[...]
    sel_rows = [sel[e:e + 1, :] for e in range(N_EXPERTS)]
    sc_rows = [scores[e:e + 1, :] for e in range(N_EXPERTS)]
    grp = []
    for g in range(N_EXPERT_GROUPS):
        r = sel_rows[g * EXPERTS_PER_GROUP:(g + 1) * EXPERTS_PER_GROUP]
        best_pair = None
        for a in range(EXPERTS_PER_GROUP):
            for b in range(a + 1, EXPERTS_PER_GROUP):
                pair = r[a] + r[b]
                best_pair = pair if best_pair is None else jnp.maximum(best_pair, pair)
        grp.append(best_pair)
    _, best = _first_index_of_max(grp)
    masked = [jnp.where(best == (e // EXPERTS_PER_GROUP), sel_rows[e], NEG_INF) for e in range(N_EXPERTS)]
    _, e0 = _first_index_of_max(masked)
    masked2 = [jnp.where(e0 == e, -jnp.inf, masked[e]) for e in range(N_EXPERTS)]
    _, e1 = _first_index_of_max(masked2)
    w0 = jnp.zeros_like(sc_rows[0])
    w1 = jnp.zeros_like(sc_rows[0])
    for e in range(N_EXPERTS):
        w0 = jnp.where(e0 == e, sc_rows[e], w0)
        w1 = jnp.where(e1 == e, sc_rows[e], w1)
    wsum = w0 + w1
    rw_ref[...] = jnp.concatenate([w0 / wsum, w1 / wsum, jnp.zeros((6, tm), F32)], axis=0)

    @pl.when(i == 0)
    def _():
        cnt_ref[...] = jnp.zeros_like(cnt_ref)

    eid = lax.broadcasted_iota(jnp.int32, (N_EXPERTS, tm), 0)
    oh0 = (eid == e0).astype(F32)
    oh1 = (eid == e1).astype(F32)
    oh = oh0 + oh1
    upper = (lax.broadcasted_iota(jnp.int32, (tm, tm), 0) <= lax.broadcasted_iota(jnp.int32, (tm, tm), 1))
    incl = jnp.dot(oh.astype(BF16), upper.astype(BF16), preferred_element_type=F32)
    before = cnt_ref[:, 0:1] + incl - oh
    r0 = jnp.sum(oh0 * before, axis=0, keepdims=True).astype(jnp.int32)
    r1 = jnp.sum(oh1 * before, axis=0, keepdims=True).astype(jnp.int32)
    cnt = cnt_ref[...] + incl[:, tm - 1:tm]
    cnt_ref[...] = cnt
    counts_ref[...] = cnt.astype(jnp.int32)
    ridx_ref[...] = jnp.concatenate([e0, e1, r0, r1, jnp.zeros((4, tm), jnp.int32)], axis=0)


def _mixer_call(layer_args, x_all, tables, n_prompt_tiles, tiles_per_seq, alpha):
    (sinks, w_in, ck, cv, lng, lnb, ws, bst, na, nb, w_o, l1g, l1b, wrt, rb) = layer_args
    cos_t, sa_t, sb_t = tables
    t_all = x_all.shape[0]
    n_tiles = t_all // TOK_TILE
    n_seq_tiles = tiles_per_seq

    def const(shape):
        nd = len(shape)
        return pl.BlockSpec(shape, lambda i, _nd=nd: (0,) * _nd)

    def tab_map(i):
        return (jnp.where(i < n_prompt_tiles, i % n_seq_tiles, n_seq_tiles), 0)

    row_blk = lambda w: pl.BlockSpec((TOK_TILE, w), lambda i: (i, 0))
    in_specs = [
        pl.BlockSpec(memory_space=pltpu.SMEM),
        row_blk(D_MODEL),
        const((D_MODEL, D_IN)),
        pl.BlockSpec((TOK_TILE, LANES), tab_map),
        pl.BlockSpec((TOK_TILE, LANES), tab_map),
        pl.BlockSpec((TOK_TILE, LANES), tab_map),
        const(ck.shape), const(cv.shape),
        const((1, GM_WIDTH)), const((1, GM_WIDTH)),
        const((GM_GROUPS, GM_CHUNK, GM_CHUNK)), const((GM_CHUNK, GM_GROUPS)),
        const((1, ATT_WIDTH)), const((1, GM_WIDTH)),
        const((D_MIX, D_MODEL)),
        const((1, D_MODEL)), const((1, D_MODEL)),
        const((N_EXPERTS, D_MODEL)), const((N_EXPERTS, 1)),
    ]
    out_shape = [
        jax.ShapeDtypeStruct((t_all * TILE_ROWS, LANES), F32),
        jax.ShapeDtypeStruct((t_all, KV_WIDTH), F32),
        jax.ShapeDtypeStruct((t_all, KV_WIDTH), F32),
        jax.ShapeDtypeStruct((TOK_TILE, GM_WIDTH), F32),
        jax.ShapeDtypeStruct((8, t_all), jnp.int32),
        jax.ShapeDtypeStruct((8, t_all), F32),
        jax.ShapeDtypeStruct((N_EXPERTS, LANES), jnp.int32),
    ]
    out_specs = [
        pl.BlockSpec((TOK_TILE * TILE_ROWS, LANES), lambda i: (i, 0)),
        row_blk(KV_WIDTH), row_blk(KV_WIDTH),
        const((TOK_TILE, GM_WIDTH)),
        pl.BlockSpec((8, TOK_TILE), lambda i: (0, i)),
        pl.BlockSpec((8, TOK_TILE), lambda i: (0, i)),
        const((N_EXPERTS, LANES)),
    ]
    kd_rows = (TOK_TILE // CHUNK) * KEYS_PER_CHUNK
    scratch = [
        pltpu.VMEM((N_KV_HEADS, kd_rows, LANES), BF16),
        pltpu.VMEM((N_KV_HEADS, kd_rows, LANES), BF16),
        pltpu.VMEM((N_Q_HEADS, TOK_TILE, LANES), BF16),
        pltpu.VMEM((TOK_TILE, ATT_WIDTH), F32),
        pltpu.VMEM((TOK_TILE, GM_WIDTH), F32),
        pltpu.VMEM((N_EXPERTS, LANES), F32),
    ]
    return pl.pallas_call(
        functools.partial(_mixer_kernel, n_prompt_tiles, tiles_per_seq, alpha),
        grid=(n_tiles,),
        in_specs=in_specs, out_specs=out_specs, out_shape=out_shape,
        scratch_shapes=scratch,
        compiler_params=pltpu.CompilerParams(dimension_semantics=("arbitrary",), vmem_limit_bytes=VMEM_LIMIT),
        name="mixer",
    )(sinks, x_all, w_in, cos_t, sa_t, sb_t, ck, cv, lng, lnb, ws, bst, na, nb, w_o, l1g, l1b, wrt, rb)


def _expert_kernel(te_ref, nt_ref, xs_ref, wg_ref, wu_ref, wd_ref, out_ref, wg_s, wu_s, wd_s):
    i = pl.program_id(0)
    changed = jnp.logical_or(i == 0, te_ref[i] != te_ref[jnp.maximum(i - 1, 0)])

    @pl.when(changed)
    def _():
        wg_s[...] = wg_ref[0].astype(BF16)
        wu_s[...] = wu_ref[0].astype(BF16)
        wd_s[...] = wd_ref[0].astype(BF16)

    @pl.when(i < nt_ref[0])
    def _():
        xs = _load_token_tiles(xs_ref, EXPERT_TILE).astype(BF16)
        g = jnp.dot(xs, wg_s[...], preferred_element_type=F32)
        u = jnp.dot(xs, wu_s[...], preferred_element_type=F32)
        hmid = (g * jax.nn.sigmoid(g)) * u
        _store_token_tiles(out_ref, jnp.dot(hmid.astype(BF16), wd_s[...], preferred_element_type=F32))

    @pl.when(i >= nt_ref[0])
    def _():
        out_ref[...] = jnp.zeros_like(out_ref)


def _expert_call(tile_expert, n_used, xs, wg, wu, wd):
    n_tiles = xs.shape[0] // (EXPERT_TILE * TILE_ROWS)
    tile_blk = pl.BlockSpec((EXPERT_TILE * TILE_ROWS, LANES), lambda i, te, nt: (i, 0))
    grid_spec = pltpu.PrefetchScalarGridSpec(
        num_scalar_prefetch=2,
        grid=(n_tiles,),
        in_specs=[
            tile_blk,
            pl.BlockSpec((1, D_MODEL, D_EXPERT), lambda i, te, nt: (te[i], 0, 0)),
            pl.BlockSpec((1, D_MODEL, D_EXPERT), lambda i, te, nt: (te[i], 0, 0)),
            pl.BlockSpec((1, D_EXPERT, D_MODEL), lambda i, te, nt: (te[i], 0, 0)),
        ],
        out_specs=tile_blk,
        scratch_shapes=[
            pltpu.VMEM((D_MODEL, D_EXPERT), BF16),
            pltpu.VMEM((D_MODEL, D_EXPERT), BF16),
            pltpu.VMEM((D_EXPERT, D_MODEL), BF16),
        ],
    )
    return pl.pallas_call(
        _expert_kernel,
        grid_spec=grid_spec,
        out_shape=jax.ShapeDtypeStruct(xs.shape, F32),
        compiler_params=pltpu.CompilerParams(dimension_semantics=("arbitrary",), vmem_limit_bytes=VMEM_LIMIT),
        name="experts",
    )(tile_expert, n_used, xs, wg, wu, wd)


ROWS_PER_STEP = TOK_TILE * TOP_K


def _token_tile(ref, idx):
    return ref.at[pl.ds(pl.multiple_of(idx * TILE_ROWS, TILE_ROWS), TILE_ROWS)]


def _dispatch_kernel(n_tok_steps, t_all, pos_ref, pad_ref, x1t_ref, xs_hbm, zero_ref, sems):
    j = pl.program_id(0)

    @pl.when(j == 0)
    def _():
        zero_ref[...] = jnp.zeros_like(zero_ref)

    @pl.when(j < n_tok_steps)
    def _():
        def body(tt, c):
            t = j * TOK_TILE + tt
            for k in range(TOP_K):
                pltpu.make_async_copy(_token_tile(x1t_ref, tt), _token_tile(xs_hbm, pos_ref[k * t_all + t]),
                                      sems.at[k]).start()
            return c
        lax.fori_loop(0, TOK_TILE, body, 0, unroll=8)

    @pl.when(j >= n_tok_steps)
    def _():
        def body(q, c):
            for k in range(TOP_K):
                row = pad_ref[(j - n_tok_steps) * ROWS_PER_STEP + k * TOK_TILE + q]
                pltpu.make_async_copy(zero_ref, _token_tile(xs_hbm, row), sems.at[k]).start()
            return c
        lax.fori_loop(0, TOK_TILE, body, 0, unroll=8)

    for k in range(TOP_K):
        pltpu.make_async_copy(x1t_ref, xs_hbm.at[pl.ds(0, TOK_TILE * TILE_ROWS)], sems.at[k]).wait()


def _dispatch_call(pos_flat, pad_rows, x1t, n_sorted_rows):
    t_all = x1t.shape[0] // TILE_ROWS
    n_tok_steps = t_all // TOK_TILE
    n_steps = n_tok_steps + pad_rows.shape[0] // ROWS_PER_STEP
    grid_spec = pltpu.PrefetchScalarGridSpec(
        num_scalar_prefetch=2,
        grid=(n_steps,),
        in_specs=[pl.BlockSpec((TOK_TILE * TILE_ROWS, LANES),
                               lambda j, p, q: (jnp.minimum(j, n_tok_steps - 1), 0))],
        out_specs=pl.BlockSpec(memory_space=pl.ANY),
        scratch_shapes=[pltpu.VMEM((TILE_ROWS, LANES), F32), pltpu.SemaphoreType.DMA((TOP_K,))],
    )
    return pl.pallas_call(
        functools.partial(_dispatch_kernel, n_tok_steps, t_all),
        grid_spec=grid_spec,
        out_shape=jax.ShapeDtypeStruct((n_sorted_rows * TILE_ROWS, LANES), F32),
        compiler_params=pltpu.CompilerParams(dimension_semantics=("arbitrary",)),
        name="dispatch",
    )(pos_flat, pad_rows, x1t)


def _combine_kernel(alpha, t_all, n_main_tiles, pos_ref, x1t_ref, w_ref, g_ref, b_ref, outs_hbm, *rest):
    if n_main_tiles is None:
        out_ref, gbuf, sems = rest
        tail_ref = None
    else:
        out_ref, tail_ref, gbuf, sems = rest
    j = pl.program_id(0)
    slot = j % 2
    k_rows = TOK_TILE * TILE_ROWS

    n = pl.num_programs(0)

    def wait(s):
        pltpu.make_async_copy(outs_hbm.at[pl.ds(0, TOP_K * k_rows)], gbuf.at[s], sems.at[s]).wait()

    @pl.when(j == 0)
    def _():
        def body(tt, c):
            for k in range(TOP_K):
                pltpu.make_async_copy(_token_tile(outs_hbm, pos_ref[k * t_all + tt]),
                                      _token_tile(gbuf.at[0], k * TOK_TILE + tt), sems.at[0]).start()
            return c
        lax.fori_loop(0, TOK_TILE, body, 0, unroll=8)

    wait(slot)

    nxt = jnp.where(j + 1 < n, j + 1, 0)
    for tt in range(TOK_TILE):
        for k in range(TOP_K):
            pltpu.make_async_copy(_token_tile(outs_hbm, pos_ref[k * t_all + nxt * TOK_TILE + tt]),
                                  gbuf.at[1 - slot, pl.ds((k * TOK_TILE + tt) * TILE_ROWS, TILE_ROWS)],
                                  sems.at[1 - slot]).start()

    w = w_ref[...]
    y = (w[:, 0:1] * _load_token_tiles(gbuf.at[slot], TOK_TILE)
         + w[:, 1:2] * _load_token_tiles(gbuf.at[slot], TOK_TILE, k_rows))
    x1 = _load_token_tiles(x1t_ref, TOK_TILE)
    res = _layer_norm(alpha * x1 + y, g_ref[...], b_ref[...])
    if tail_ref is None:
        out_ref[...] = res
    else:
        @pl.when(j < n_main_tiles)
        def _():
            out_ref[...] = res

        @pl.when(j >= n_main_tiles)
        def _():
            tail_ref[...] = res

    @pl.when(j == n - 1)
    def _():
        wait(1 - slot)


def _combine_call(pos_flat, x1t, w_col, g, b, outs, alpha, n_main_tiles=None):
    t_all = x1t.shape[0] // TILE_ROWS
    n_tiles = t_all // TOK_TILE
    if n_main_tiles is None:
        out_specs = pl.BlockSpec((TOK_TILE, D_MODEL), lambda i, p: (i, 0))
        out_shape = jax.ShapeDtypeStruct((t_all, D_MODEL), F32)
    else:
        assert n_tiles == n_main_tiles + 1
        out_specs = [pl.BlockSpec((TOK_TILE, D_MODEL), lambda i, p: (jnp.minimum(i, n_main_tiles - 1), 0)),
                     pl.BlockSpec((TOK_TILE, D_MODEL), lambda i, p: (0, 0))]
        out_shape = [jax.ShapeDtypeStruct((n_main_tiles * TOK_TILE, D_MODEL), F32),
                     jax.ShapeDtypeStruct((TOK_TILE, D_MODEL), F32)]
    grid_spec = pltpu.PrefetchScalarGridSpec(
        num_scalar_prefetch=1,
        grid=(t_all // TOK_TILE,),
        in_specs=[
            pl.BlockSpec((TOK_TILE * TILE_ROWS, LANES), lambda i, p: (i, 0)),
            pl.BlockSpec((TOK_TILE, TOP_K), lambda i, p: (i, 0)),
            pl.BlockSpec((1, D_MODEL), lambda i, p: (0, 0)),
            pl.BlockSpec((1, D_MODEL), lambda i, p: (0, 0)),
            pl.BlockSpec(memory_space=pl.ANY),
        ],
        out_specs=out_specs,
        scratch_shapes=[pltpu.VMEM((2, TOP_K * TOK_TILE * TILE_ROWS, LANES), F32), pltpu.SemaphoreType.DMA((2,))],
    )
    return pl.pallas_call(
        functools.partial(_combine_kernel, alpha, t_all, n_main_tiles),
        grid_spec=grid_spec,
        out_shape=out_shape,
        compiler_params=pltpu.CompilerParams(dimension_semantics=("arbitrary",), vmem_limit_bytes=VMEM_LIMIT),
        name="combine",
    )(pos_flat, x1t, w_col, g, b, outs)


def _rope_tables(seq, dec_seq, past_len):
    half = HEAD_DIM // 2
    inv = ROPE_THETA ** (-jnp.arange(half, dtype=F32) / half)
    pos = jnp.concatenate([jnp.arange(seq), past_len + (jnp.arange(TOK_TILE) % dec_seq)])
    ang = pos.astype(F32)[:, None] * inv[None, :]
    cos = jnp.cos(ang)
    sin = jnp.sin(ang)
    zero = jnp.zeros_like(sin)
    reps = LANES // HEAD_DIM
    cos_t = jnp.tile(jnp.concatenate([cos, cos], -1), (1, reps))
    sa_t = jnp.tile(jnp.concatenate([-sin, zero], -1), (1, reps))
    sb_t = jnp.tile(jnp.concatenate([zero, sin], -1), (1, reps))
    return cos_t, sa_t, sb_t


def _inclusive_cumsum(v):
    n = v.shape[0]
    tri = jnp.arange(n)[None, :] <= jnp.arange(n)[:, None]
    return jnp.sum(jnp.where(tri, v[None, :], 0), axis=1)


def _dispatch_plan(ridx, counts, n_tiles):
    t_all = ridx.shape[1]
    tiles_e = (counts + EXPERT_TILE - 1) // EXPERT_TILE
    tile_end = _inclusive_cumsum(tiles_e)
    offs = (tile_end - tiles_e) * EXPERT_TILE
    n_used = tile_end[-1:]
    tile_expert = jnp.minimum(
        jnp.sum((jnp.arange(n_tiles)[:, None] >= tile_end[None, :]).astype(jnp.int32), axis=1), N_EXPERTS - 1)
    experts = jnp.arange(N_EXPERTS)
    pos = ridx[TOP_K:2 * TOP_K] + jnp.sum(
        jnp.where(ridx[:TOP_K, :, None] == experts[None, None, :], offs[None, None, :], 0), axis=-1)
    gap_start = jnp.concatenate([offs + counts, n_used * EXPERT_TILE])
    gap_len = jnp.concatenate([tiles_e * EXPERT_TILE - counts, (n_tiles - n_used) * EXPERT_TILE])
    gap_end_q = _inclusive_cumsum(gap_len)
    q = jnp.arange(n_tiles * EXPERT_TILE - TOP_K * t_all)
    gap = jnp.sum((q[:, None] >= gap_end_q[None, :]).astype(jnp.int32), axis=1)
    in_gap = gap[:, None] == jnp.arange(N_EXPERTS + 1)[None, :]
    pad_rows = q + jnp.sum(jnp.where(in_gap, (gap_start - gap_end_q + gap_len)[None, :], 0), axis=1)
    return (pos.reshape(-1).astype(jnp.int32), pad_rows.astype(jnp.int32),
            tile_expert.astype(jnp.int32), n_used.astype(jnp.int32))


def kernel(x_prompt, x_sample, cache_k, cache_v, w_in, sinks, gm_ln_g, gm_ln_b, gm_ws, gm_bs,
           out_norm_a, out_norm_b, w_o, ln1_g, ln1_b, w_router, router_bias,
           w_gate, w_up, w_down, ln2_g, ln2_b):
    batch, seq, _ = x_prompt.shape
    dec_batch, dec_seq, _ = x_sample.shape
    depth = w_in.shape[0]
    past_len = PAST_LEN
    assert dec_batch * dec_seq == TOK_TILE and dec_seq == CHUNK and seq % TOK_TILE == 0
    assert cache_k.shape[2] == WINDOW
    alpha = (2 * depth) ** 0.25
    n_prompt = batch * seq
    t_all = n_prompt + dec_batch * dec_seq
    n_prompt_tiles = n_prompt // TOK_TILE
    tiles_per_seq = seq // TOK_TILE
    n_exp_tiles = (TOP_K * t_all) // EXPERT_TILE + N_EXPERTS

    tables = _rope_tables(seq, dec_seq, past_len)
    x = jnp.concatenate([x_prompt.reshape(n_prompt, D_MODEL), x_sample.reshape(-1, D_MODEL)], axis=0)
    wrt = w_router.T
    rb = router_bias.reshape(N_EXPERTS, 1)

    kp, vp, ks, vs, gms = [], [], [], [], []
    for l in range(depth):
        layer_args = (
            sinks[l], w_in[l].astype(BF16),
            cache_k[l].reshape(dec_batch, WINDOW, KV_WIDTH), cache_v[l].reshape(dec_batch, WINDOW, KV_WIDTH),
            gm_ln_g[l].reshape(1, GM_WIDTH), gm_ln_b[l].reshape(1, GM_WIDTH),
            gm_ws[l], gm_bs[l].T,
            out_norm_a[l].reshape(1, ATT_WIDTH), out_norm_b[l].reshape(1, GM_WIDTH),
            w_o[l].astype(BF16),
            ln1_g[l].reshape(1, D_MODEL), ln1_b[l].reshape(1, D_MODEL), wrt, rb)
        x1t, k_all, v_all, vn_s, ridx, rw, counts = _mixer_call(
            layer_args, x, tables, n_prompt_tiles, tiles_per_seq, alpha)

        pos, pad_rows, tile_expert, n_used = _dispatch_plan(ridx, counts[:, 0], n_exp_tiles)
        xs = _dispatch_call(pos, pad_rows, x1t, n_exp_tiles * EXPERT_TILE)
        outs = _expert_call(tile_expert, n_used, xs, w_gate[l], w_up[l], w_down[l])
        x = _combine_call(pos, x1t, rw[:TOP_K].T, ln2_g[l].reshape(1, D_MODEL), ln2_b[l].reshape(1, D_MODEL),
                          outs, alpha, n_main_tiles=n_prompt_tiles if l == depth - 1 else None)

        k4 = k_all[:n_prompt].reshape(batch, seq, N_KV_HEADS, HEAD_DIM)
        v4 = v_all[:n_prompt].reshape(batch, seq, N_KV_HEADS, HEAD_DIM)
        kp.append(k4[:, -WINDOW:])
        vp.append(v4[:, -WINDOW:])
        ks.append(k_all[n_prompt:].reshape(dec_batch, dec_seq, N_KV_HEADS, HEAD_DIM))
        vs.append(v_all[n_prompt:].reshape(dec_batch, dec_seq, N_KV_HEADS, HEAD_DIM))
        gms.append(vn_s.reshape(dec_batch, dec_seq, GM_WIDTH))

    y_prompt = x[0].reshape(batch, seq, D_MODEL)
    y_sample = x[1].reshape(dec_batch, dec_seq, D_MODEL)
    return (y_prompt, y_sample, jnp.stack(kp), jnp.stack(vp), jnp.stack(ks), jnp.stack(vs), jnp.stack(gms))
```

```python
import functools

import jax
import jax.numpy as jnp
from jax import lax
from jax.experimental import pallas as pl
from jax.experimental.pallas import tpu as pltpu

D_MODEL = 1024
CHUNK = 64
WINDOW = 128
HEAD_DIM = 64
N_Q_HEADS = 8
N_KV_HEADS = 2
Q_PER_KV = N_Q_HEADS // N_KV_HEADS
PAST_LEN = 2048
ATT_WIDTH = N_Q_HEADS * HEAD_DIM
KV_WIDTH = N_KV_HEADS * HEAD_DIM
ROPE_THETA = 10000.0
GM_GROUPS = 4
GM_CH = 128
GM_WIDTH = GM_GROUPS * GM_CH
GM_CHUNK = 128
D_MIX = ATT_WIDTH + GM_WIDTH
D_IN = ATT_WIDTH + 2 * KV_WIDTH + 2 * GM_WIDTH
N_EXPERTS = 16
N_EXPERT_GROUPS = 4
EXPERTS_PER_GROUP = N_EXPERTS // N_EXPERT_GROUPS
TOP_K = 2
D_EXPERT = 512
LN_EPS = 1e-5
NEG_INF = -1e30

LANES = 128
TILE_ROWS = D_MODEL // LANES
TOK_TILE = 512
EXPERT_TILE = 512
KEYS_PER_CHUNK = WINDOW + CHUNK
VMEM_LIMIT = 56 * 1024 * 1024

F32 = jnp.float32
BF16 = jnp.bfloat16


def _layer_norm(x, g, b):
    mu = jnp.mean(x, axis=-1, keepdims=True)
    d = x - mu
    var = jnp.mean(d * d, axis=-1, keepdims=True)
    return d * lax.rsqrt(var + LN_EPS) * g + b


def _rms_norm(x, g):
    return x * lax.rsqrt(jnp.mean(x * x, axis=-1, keepdims=True) + LN_EPS) * g


def _gelu(x):
    return 0.5 * x * (1.0 + lax.erf(x * (0.5 ** 0.5)))


def _store_token_tiles(ref, x):
    n = x.shape[0]
    for c in range(TILE_ROWS):
        ref[pl.ds(c, n, stride=TILE_ROWS), :] = x[:, c * LANES:(c + 1) * LANES]


def _load_token_tiles(ref, n, row0=0):
    return jnp.concatenate([ref[pl.ds(row0 + c, n, stride=TILE_ROWS), :] for c in range(TILE_ROWS)], axis=-1)


def _first_index_of_max(rows):
    m = rows[0]
    for r in rows[1:]:
        m = jnp.maximum(m, r)
    idx = jnp.full(m.shape, len(rows), jnp.int32)
    for e in reversed(range(len(rows))):
        idx = jnp.where(rows[e] == m, e, idx)
    return m, idx


def _mixer_kernel(n_prompt_tiles, tiles_per_seq, alpha,
                  sinks_ref, x_ref, w_in_ref, cos_ref, sa_ref, sb_ref, ck_ref, cv_ref,
                  lng_ref, lnb_ref, ws_ref, bst_ref, na_ref, nb_ref, w_o_ref,
                  l1g_ref, l1b_ref, wrt_ref, rb_ref,
                  x1t_ref, k_ref, v_ref, vn_ref, ridx_ref, rw_ref, counts_ref,
                  kd_ref, vd_ref, qm_ref, att_ref, gm_ref, cnt_ref):
    i = pl.program_id(0)
    is_sample = i >= n_prompt_tiles
    seq_start = jnp.logical_and(jnp.logical_not(is_sample), (i % tiles_per_seq) == 0)
    tm = TOK_TILE

    x = x_ref[...]
    h = jnp.dot(x.astype(BF16), w_in_ref[...], preferred_element_type=F32)

    cos = cos_ref[...]
    sa = sa_ref[...]
    sb = sb_ref[...]

    def rope(blk):
        return blk * cos + pltpu.roll(blk, LANES - HEAD_DIM // 2, 1) * sa + pltpu.roll(blk, HEAD_DIM // 2, 1) * sb

    lane = lax.broadcasted_iota(jnp.int32, (1, LANES), 1)
    lo_half = lane < HEAD_DIM

    o_k = ATT_WIDTH
    o_v = o_k + KV_WIDTH
    o_u = o_v + KV_WIDTH
    o_g = o_u + GM_WIDTH

    k_rot = rope(h[:, o_k:o_v])
    v_new = h[:, o_v:o_u]
    k_ref[...] = k_rot
    v_ref[...] = v_new

    def dup_heads(a):
        sw = pltpu.roll(a, HEAD_DIM, 1)
        return jnp.where(lo_half, a, sw).astype(BF16), jnp.where(lo_half, sw, a).astype(BF16)

    k_d = dup_heads(k_rot)
    v_d = dup_heads(v_new)

    @pl.when(jnp.logical_not(is_sample))
    def _():
        @pl.when(seq_start)
        def _():
            for g in range(N_KV_HEADS):
                kd_ref[g, 0:WINDOW, :] = jnp.zeros((WINDOW, LANES), BF16)
                vd_ref[g, 0:WINDOW, :] = jnp.zeros((WINDOW, LANES), BF16)

        @pl.when(jnp.logical_not(seq_start))
        def _():
            for g in range(N_KV_HEADS):
                kd_ref[g, 0:WINDOW, :] = kd_ref[g, tm:tm + WINDOW, :]
                vd_ref[g, 0:WINDOW, :] = vd_ref[g, tm:tm + WINDOW, :]

        for g in range(N_KV_HEADS):
            kd_ref[g, WINDOW:WINDOW + tm, :] = k_d[g]
            vd_ref[g, WINDOW:WINDOW + tm, :] = v_d[g]

    @pl.when(is_sample)
    def _():
        for b in range(tm // CHUNK):
            ck = dup_heads(ck_ref[b])
            cv = dup_heads(cv_ref[b])
            base = b * KEYS_PER_CHUNK
            for g in range(N_KV_HEADS):
                kd_ref[g, base:base + WINDOW, :] = ck[g]
                vd_ref[g, base:base + WINDOW, :] = cv[g]
                kd_ref[g, base + WINDOW:base + KEYS_PER_CHUNK, :] = k_d[g][b * CHUNK:(b + 1) * CHUNK]
                vd_ref[g, base + WINDOW:base + KEYS_PER_CHUNK, :] = v_d[g][b * CHUNK:(b + 1) * CHUNK]

    scale = HEAD_DIM ** -0.5
    for b in range(ATT_WIDTH // LANES):
        qb = rope(h[:, b * LANES:(b + 1) * LANES]) * scale
        qm_ref[2 * b] = jnp.where(lo_half, qb, 0.0).astype(BF16)
        qm_ref[2 * b + 1] = jnp.where(lo_half, 0.0, qb).astype(BF16)

    key_stride = jnp.where(is_sample, KEYS_PER_CHUNK, CHUNK)
    col = lax.broadcasted_iota(jnp.int32, (1, KEYS_PER_CHUNK), 1)
    row_blk = lax.broadcasted_iota(jnp.int32, (Q_PER_KV * CHUNK, 1), 0) // CHUNK
    sink_cols = [
        jnp.where(row_blk == 0, sinks_ref[Q_PER_KV * g],
                  jnp.where(row_blk == 1, sinks_ref[Q_PER_KV * g + 1],
                            jnp.where(row_blk == 2, sinks_ref[Q_PER_KV * g + 2], sinks_ref[Q_PER_KV * g + 3])))
        for g in range(N_KV_HEADS)]

    def chunk_body(c, carry):
        r0 = pl.multiple_of(c * CHUNK, CHUNK)
        k0 = pl.multiple_of(c * key_stride, CHUNK)
        first_valid = jnp.where(seq_start, (WINDOW // CHUNK - c) * CHUNK, 0)
        valid = col >= first_valid
        outs = []
        for g in range(N_KV_HEADS):
            q_st = jnp.concatenate([qm_ref[Q_PER_KV * g + r, pl.ds(r0, CHUNK), :] for r in range(Q_PER_KV)],
                                   axis=0)
            keys = kd_ref[g, pl.ds(k0, KEYS_PER_CHUNK), :]
            vals = vd_ref[g, pl.ds(k0, KEYS_PER_CHUNK), :]
            s = lax.dot_general(q_st, keys, (((1,), (1,)), ((), ())), preferred_element_type=F32)
            s = jnp.where(valid, s, NEG_INF)
            sink = sink_cols[g]
            m = jnp.maximum(jnp.max(s, axis=-1, keepdims=True), sink)
            p = jnp.exp(s - m)
            denom = jnp.sum(p, axis=-1, keepdims=True) + jnp.exp(sink - m)
            o = jnp.dot(p.astype(BF16), vals, preferred_element_type=F32) * (1.0 / denom)
            for bb in range(2):
                outs.append(jnp.where(lo_half, o[(2 * bb) * CHUNK:(2 * bb + 1) * CHUNK],
                                      o[(2 * bb + 1) * CHUNK:(2 * bb + 2) * CHUNK]))
        for b in range(ATT_WIDTH // LANES):
            att_ref[pl.ds(r0, CHUNK), b * LANES:(b + 1) * LANES] = outs[b]
        return carry

    lax.fori_loop(0, tm // CHUNK, chunk_body, 0, unroll=2)

    prow = lax.broadcasted_iota(jnp.int32, (GM_CHUNK, GM_CHUNK), 0)
    pcol = lax.broadcasted_iota(jnp.int32, (GM_CHUNK, GM_CHUNK), 1)
    half = GM_CHUNK // 2
    tril = pcol <= prow
    same_blk = (prow < half) == (pcol < half)
    prow1 = lax.broadcasted_iota(jnp.int32, (GM_CHUNK, 1), 0)
    bst = bst_ref[...]
    bst_s = jnp.where(prow1 < half, bst, pltpu.roll(bst, half, 0))
    bias = jnp.where(is_sample, bst_s, bst)
    for g in range(GM_GROUPS):
        wg_ = ws_ref[g]
        w_s = jnp.where(prow < half, wg_, pltpu.roll(pltpu.roll(wg_, half, 0), half, 1))
        w_eff = jnp.where(is_sample, jnp.where(same_blk, w_s, 0.0), wg_)
        w_eff = jnp.where(tril, w_eff, 0.0).astype(BF16)
        cols = slice(g * GM_CH, (g + 1) * GM_CH)
        ug = _gelu(h[:, o_u + g * GM_CH:o_u + (g + 1) * GM_CH])
        vn = _layer_norm(_gelu(h[:, o_g + g * GM_CH:o_g + (g + 1) * GM_CH]), lng_ref[:, cols], lnb_ref[:, cols])

        @pl.when(is_sample)
        def _():
            vn_ref[:, cols] = vn

        vn_b = vn.astype(BF16)
        for n in range(tm // GM_CHUNK):
            rows = slice(n * GM_CHUNK, (n + 1) * GM_CHUNK)
            s = jnp.dot(w_eff, vn_b[rows], preferred_element_type=F32) + bias[:, g:g + 1]
            gm_ref[rows, cols] = ug[rows] * s

    mixed = jnp.concatenate([_rms_norm(att_ref[...], na_ref[...]), _rms_norm(gm_ref[...], nb_ref[...])], axis=-1)
    y = alpha * x + jnp.dot(mixed.astype(BF16), w_o_ref[...], preferred_element_type=F32)
    x1 = _layer_norm(y, l1g_ref[...], l1b_ref[...])
    _store_token_tiles(x1t_ref, x1)

    logits = lax.dot_general(wrt_ref[...], x1, (((1,), (1,)), ((), ())),
                             preferred_element_type=F32, precision=lax.Precision.HIGHEST)
    scores = jax.nn.sigmoid(logits)
    sel = scores + rb_ref[...]
    sel_rows = [sel[e:e + 1, :] for e in range(N_EXPERTS)]
    sc_rows = [scores[e:e + 1, :] for e in range(N_EXPERTS)]
    grp = []
    for g in range(N_EXPERT_GROUPS):
        r = sel_rows[g * EXPERTS_PER_GROUP:(g + 1) * EXPERTS_PER_GROUP]
        best_pair = None
        for a in range(EXPERTS_PER_GROUP):
            for b in range(a + 1, EXPERTS_PER_GROUP):
                pair = r[a] + r[b]
                best_pair = pair if best_pair is None else jnp.maximum(best_pair, pair)
        grp.append(best_pair)
    _, best = _first_index_of_max(grp)
    masked = [jnp.where(best == (e // EXPERTS_PER_GROUP), sel_rows[e], NEG_INF) for e in range(N_EXPERTS)]
    _, e0 = _first_index_of_max(masked)
    masked2 = [jnp.where(e0 == e, -jnp.inf, masked[e]) for e in range(N_EXPERTS)]
    _, e1 = _first_index_of_max(masked2)
    w0 = jnp.zeros_like(sc_rows[0])
    w1 = jnp.zeros_like(sc_rows[0])
    for e in range(N_EXPERTS):
        w0 = jnp.where(e0 == e, sc_rows[e], w0)
        w1 = jnp.where(e1 == e, sc_rows[e], w1)
    wsum = w0 + w1
    rw_ref[...] = jnp.concatenate([w0 / wsum, w1 / wsum, jnp.zeros((6, tm), F32)], axis=0)

    @pl.when(i == 0)
    def _():
        cnt_ref[...] = jnp.zeros_like(cnt_ref)

    eid = lax.broadcasted_iota(jnp.int32, (N_EXPERTS, tm), 0)
    oh0 = (eid == e0).astype(F32)
    oh1 = (eid == e1).astype(F32)
    oh = oh0 + oh1
    upper = (lax.broadcasted_iota(jnp.int32, (tm, tm), 0) <= lax.broadcasted_iota(jnp.int32, (tm, tm), 1))
    incl = jnp.dot(oh.astype(BF16), upper.astype(BF16), preferred_element_type=F32)
    before = cnt_ref[:, 0:1] + incl - oh
    r0 = jnp.sum(oh0 * before, axis=0, keepdims=True).astype(jnp.int32)
    r1 = jnp.sum(oh1 * before, axis=0, keepdims=True).astype(jnp.int32)
    cnt = cnt_ref[...] + incl[:, tm - 1:tm]
    cnt_ref[...] = cnt
    counts_ref[...] = cnt.astype(jnp.int32)
    ridx_ref[...] = jnp.concatenate([e0, e1, r0, r1, jnp.zeros((4, tm), jnp.int32)], axis=0)


def _mixer_call(layer_args, x_all, tables, n_prompt_tiles, tiles_per_seq, alpha):
    (sinks, w_in, ck, cv, lng, lnb, ws, bst, na, nb, w_o, l1g, l1b, wrt, rb) = layer_args
    cos_t, sa_t, sb_t = tables
    t_all = x_all.shape[0]
    n_tiles = t_all // TOK_TILE
    n_seq_tiles = tiles_per_seq

    def const(shape):
        nd = len(shape)
        return pl.BlockSpec(shape, lambda i, _nd=nd: (0,) * _nd)

    def tab_map(i):
        return (jnp.where(i < n_prompt_tiles, i % n_seq_tiles, n_seq_tiles), 0)

    row_blk = lambda w: pl.BlockSpec((TOK_TILE, w), lambda i: (i, 0))
    in_specs = [
        pl.BlockSpec(memory_space=pltpu.SMEM),
        row_blk(D_MODEL),
        const((D_MODEL, D_IN)),
        pl.BlockSpec((TOK_TILE, LANES), tab_map),
        pl.BlockSpec((TOK_TILE, LANES), tab_map),
        pl.BlockSpec((TOK_TILE, LANES), tab_map),
        const(ck.shape), const(cv.shape),
        const((1, GM_WIDTH)), const((1, GM_WIDTH)),
        const((GM_GROUPS, GM_CHUNK, GM_CHUNK)), const((GM_CHUNK, GM_GROUPS)),
        const((1, ATT_WIDTH)), const((1, GM_WIDTH)),
        const((D_MIX, D_MODEL)),
        const((1, D_MODEL)), const((1, D_MODEL)),
        const((N_EXPERTS, D_MODEL)), const((N_EXPERTS, 1)),
    ]
    out_shape = [
        jax.ShapeDtypeStruct((t_all * TILE_ROWS, LANES), F32),
        jax.ShapeDtypeStruct((t_all, KV_WIDTH), F32),
        jax.ShapeDtypeStruct((t_all, KV_WIDTH), F32),
        jax.ShapeDtypeStruct((TOK_TILE, GM_WIDTH), F32),
        jax.ShapeDtypeStruct((8, t_all), jnp.int32),
        jax.ShapeDtypeStruct((8, t_all), F32),
        jax.ShapeDtypeStruct((N_EXPERTS, LANES), jnp.int32),
    ]
    out_specs = [
        pl.BlockSpec((TOK_TILE * TILE_ROWS, LANES), lambda i: (i, 0)),
        row_blk(KV_WIDTH), row_blk(KV_WIDTH),
        const((TOK_TILE, GM_WIDTH)),
        pl.BlockSpec((8, TOK_TILE), lambda i: (0, i)),
        pl.BlockSpec((8, TOK_TILE), lambda i: (0, i)),
        const((N_EXPERTS, LANES)),
    ]
    kd_rows = (TOK_TILE // CHUNK) * KEYS_PER_CHUNK
    scratch = [
        pltpu.VMEM((N_KV_HEADS, kd_rows, LANES), BF16),
        pltpu.VMEM((N_KV_HEADS, kd_rows, LANES), BF16),
        pltpu.VMEM((N_Q_HEADS, TOK_TILE, LANES), BF16),
        pltpu.VMEM((TOK_TILE, ATT_WIDTH), F32),
        pltpu.VMEM((TOK_TILE, GM_WIDTH), F32),
        pltpu.VMEM((N_EXPERTS, LANES), F32),
    ]
    return pl.pallas_call(
        functools.partial(_mixer_kernel, n_prompt_tiles, tiles_per_seq, alpha),
        grid=(n_tiles,),
        in_specs=in_specs, out_specs=out_specs, out_shape=out_shape,
        scratch_shapes=scratch,
        compiler_params=pltpu.CompilerParams(dimension_semantics=("arbitrary",), vmem_limit_bytes=VMEM_LIMIT),
        name="mixer",
    )(sinks, x_all, w_in, cos_t, sa_t, sb_t, ck, cv, lng, lnb, ws, bst, na, nb, w_o, l1g, l1b, wrt, rb)


def _expert_kernel(te_ref, nt_ref, xs_ref, wg_ref, wu_ref, wd_ref, out_ref, wg_s, wu_s, wd_s):
    i = pl.program_id(0)
    changed = jnp.logical_or(i == 0, te_ref[i] != te_ref[jnp.maximum(i - 1, 0)])

    @pl.when(changed)
    def _():
        wg_s[...] = wg_ref[0].astype(BF16)
        wu_s[...] = wu_ref[0].astype(BF16)
        wd_s[...] = wd_ref[0].astype(BF16)

    @pl.when(i < nt_ref[0])
    def _():
        xs = _load_token_tiles(xs_ref, EXPERT_TILE).astype(BF16)
        g = jnp.dot(xs, wg_s[...], preferred_element_type=F32)
        u = jnp.dot(xs, wu_s[...], preferred_element_type=F32)
        hmid = (g * jax.nn.sigmoid(g)) * u
        _store_token_tiles(out_ref, jnp.dot(hmid.astype(BF16), wd_s[...], preferred_element_type=F32))

    @pl.when(i >= nt_ref[0])
    def _():
        out_ref[...] = jnp.zeros_like(out_ref)


def _expert_call(tile_expert, n_used, xs, wg, wu, wd):
    n_tiles = xs.shape[0] // (EXPERT_TILE * TILE_ROWS)
    tile_blk = pl.BlockSpec((EXPERT_TILE * TILE_ROWS, LANES), lambda i, te, nt: (i, 0))
    grid_spec = pltpu.PrefetchScalarGridSpec(
        num_scalar_prefetch=2,
        grid=(n_tiles,),
        in_specs=[
            tile_blk,
            pl.BlockSpec((1, D_MODEL, D_EXPERT), lambda i, te, nt: (te[i], 0, 0)),
            pl.BlockSpec((1, D_MODEL, D_EXPERT), lambda i, te, nt: (te[i], 0, 0)),
            pl.BlockSpec((1, D_EXPERT, D_MODEL), lambda i, te, nt: (te[i], 0, 0)),
        ],
        out_specs=tile_blk,
        scratch_shapes=[
            pltpu.VMEM((D_MODEL, D_EXPERT), BF16),
            pltpu.VMEM((D_MODEL, D_EXPERT), BF16),
            pltpu.VMEM((D_EXPERT, D_MODEL), BF16),
        ],
    )
    return pl.pallas_call(
        _expert_kernel,
        grid_spec=grid_spec,
        out_shape=jax.ShapeDtypeStruct(xs.shape, F32),
        compiler_params=pltpu.CompilerParams(dimension_semantics=("arbitrary",), vmem_limit_bytes=VMEM_LIMIT),
        name="experts",
    )(tile_expert, n_used, xs, wg, wu, wd)


ROWS_PER_STEP = TOK_TILE * TOP_K


def _token_tile(ref, idx):
    return ref.at[pl.ds(pl.multiple_of(idx * TILE_ROWS, TILE_ROWS), TILE_ROWS)]


def _dispatch_kernel(n_tok_steps, t_all, pos_ref, pad_ref, x1t_ref, xs_hbm, zero_ref, sems):
    j = pl.program_id(0)

    @pl.when(j == 0)
    def _():
        zero_ref[...] = jnp.zeros_like(zero_ref)

    @pl.when(j < n_tok_steps)
    def _():
        def body(tt, c):
            t = j * TOK_TILE + tt
            for k in range(TOP_K):
                pltpu.make_async_copy(_token_tile(x1t_ref, tt), _token_tile(xs_hbm, pos_ref[k * t_all + t]),
                                      sems.at[k]).start()
            return c
        lax.fori_loop(0, TOK_TILE, body, 0, unroll=8)

    @pl.when(j >= n_tok_steps)
    def _():
        def body(q, c):
            for k in range(TOP_K):
                row = pad_ref[(j - n_tok_steps) * ROWS_PER_STEP + k * TOK_TILE + q]
                pltpu.make_async_copy(zero_ref, _token_tile(xs_hbm, row), sems.at[k]).start()
            return c
        lax.fori_loop(0, TOK_TILE, body, 0, unroll=8)

    for k in range(TOP_K):
        pltpu.make_async_copy(x1t_ref, xs_hbm.at[pl.ds(0, TOK_TILE * TILE_ROWS)], sems.at[k]).wait()


def _dispatch_call(pos_flat, pad_rows, x1t, n_sorted_rows):
    t_all = x1t.shape[0] // TILE_ROWS
    n_tok_steps = t_all // TOK_TILE
    n_steps = n_tok_steps + pad_rows.shape[0] // ROWS_PER_STEP
    grid_spec = pltpu.PrefetchScalarGridSpec(
        num_scalar_prefetch=2,
        grid=(n_steps,),
        in_specs=[pl.BlockSpec((TOK_TILE * TILE_ROWS, LANES),
                               lambda j, p, q: (jnp.minimum(j, n_tok_steps - 1), 0))],
        out_specs=pl.BlockSpec(memory_space=pl.ANY),
        scratch_shapes=[pltpu.VMEM((TILE_ROWS, LANES), F32), pltpu.SemaphoreType.DMA((TOP_K,))],
    )
    return pl.pallas_call(
        functools.partial(_dispatch_kernel, n_tok_steps, t_all),
        grid_spec=grid_spec,
        out_shape=jax.ShapeDtypeStruct((n_sorted_rows * TILE_ROWS, LANES), F32),
        compiler_params=pltpu.CompilerParams(dimension_semantics=("arbitrary",)),
        name="dispatch",
    )(pos_flat, pad_rows, x1t)


def _combine_kernel(alpha, t_all, n_main_tiles, pos_ref, x1t_ref, w_ref, g_ref, b_ref, outs_hbm, *rest):
    if n_main_tiles is None:
        out_ref, gbuf, sems = rest
        tail_ref = None
    else:
        out_ref, tail_ref, gbuf, sems = rest
    j = pl.program_id(0)
    slot = j % 2
    k_rows = TOK_TILE * TILE_ROWS

    n = pl.num_programs(0)

    def wait(s):
        pltpu.make_async_copy(outs_hbm.at[pl.ds(0, TOP_K * k_rows)], gbuf.at[s], sems.at[s]).wait()

    def issue(step, s):
        def body(tt, c):
            t = step * TOK_TILE + tt
            for k in range(TOP_K):
                pltpu.make_async_copy(_token_tile(outs_hbm, pos_ref[k * t_all + t]),
                                      _token_tile(gbuf.at[s], k * TOK_TILE + tt), sems.at[s]).start()
            return c
        lax.fori_loop(0, TOK_TILE, body, 0, unroll=8)

    @pl.when(j == 0)
    def _():
        issue(0, 0)

    @pl.when(j + 1 < n)
    def _():
        issue(j + 1, 1 - slot)

    wait(slot)

    w = w_ref[...]
    y = (w[:, 0:1] * _load_token_tiles(gbuf.at[slot], TOK_TILE)
         + w[:, 1:2] * _load_token_tiles(gbuf.at[slot], TOK_TILE, k_rows))
    x1 = _load_token_tiles(x1t_ref, TOK_TILE)
    res = _layer_norm(alpha * x1 + y, g_ref[...], b_ref[...])
    if tail_ref is None:
        out_ref[...] = res
    else:
        @pl.when(j < n_main_tiles)
        def _():
            out_ref[...] = res

        @pl.when(j >= n_main_tiles)
        def _():
            tail_ref[...] = res


def _combine_call(pos_flat, x1t, w_col, g, b, outs, alpha, n_main_tiles=None):
    t_all = x1t.shape[0] // TILE_ROWS
    n_tiles = t_all // TOK_TILE
    if n_main_tiles is None:
        out_specs = pl.BlockSpec((TOK_TILE, D_MODEL), lambda i, p: (i, 0))
        out_shape = jax.ShapeDtypeStruct((t_all, D_MODEL), F32)
    else:
        assert n_tiles == n_main_tiles + 1
        out_specs = [pl.BlockSpec((TOK_TILE, D_MODEL), lambda i, p: (jnp.minimum(i, n_main_tiles - 1), 0)),
                     pl.BlockSpec((TOK_TILE, D_MODEL), lambda i, p: (0, 0))]
        out_shape = [jax.ShapeDtypeStruct((n_main_tiles * TOK_TILE, D_MODEL), F32),
                     jax.ShapeDtypeStruct((TOK_TILE, D_MODEL), F32)]
    grid_spec = pltpu.PrefetchScalarGridSpec(
        num_scalar_prefetch=1,
        grid=(t_all // TOK_TILE,),
        in_specs=[
            pl.BlockSpec((TOK_TILE * TILE_ROWS, LANES), lambda i, p: (i, 0)),
            pl.BlockSpec((TOK_TILE, TOP_K), lambda i, p: (i, 0)),
            pl.BlockSpec((1, D_MODEL), lambda i, p: (0, 0)),
            pl.BlockSpec((1, D_MODEL), lambda i, p: (0, 0)),
            pl.BlockSpec(memory_space=pl.ANY),
        ],
        out_specs=out_specs,
        scratch_shapes=[pltpu.VMEM((2, TOP_K * TOK_TILE * TILE_ROWS, LANES), F32), pltpu.SemaphoreType.DMA((2,))],
    )
    return pl.pallas_call(
        functools.partial(_combine_kernel, alpha, t_all, n_main_tiles),
        grid_spec=grid_spec,
        out_shape=out_shape,
        compiler_params=pltpu.CompilerParams(dimension_semantics=("arbitrary",), vmem_limit_bytes=VMEM_LIMIT),
        name="combine",
    )(pos_flat, x1t, w_col, g, b, outs)


def _rope_tables(seq, dec_seq, past_len):
    half = HEAD_DIM // 2
    inv = ROPE_THETA ** (-jnp.arange(half, dtype=F32) / half)
    pos = jnp.concatenate([jnp.arange(seq), past_len + (jnp.arange(TOK_TILE) % dec_seq)])
    ang = pos.astype(F32)[:, None] * inv[None, :]
    cos = jnp.cos(ang)
    sin = jnp.sin(ang)
    zero = jnp.zeros_like(sin)
    reps = LANES // HEAD_DIM
    cos_t = jnp.tile(jnp.concatenate([cos, cos], -1), (1, reps))
    sa_t = jnp.tile(jnp.concatenate([-sin, zero], -1), (1, reps))
    sb_t = jnp.tile(jnp.concatenate([zero, sin], -1), (1, reps))
    return cos_t, sa_t, sb_t


def _inclusive_cumsum(v):
    n = v.shape[0]
    tri = jnp.arange(n)[None, :] <= jnp.arange(n)[:, None]
    return jnp.sum(jnp.where(tri, v[None, :], 0), axis=1)


def _dispatch_plan(ridx, counts, n_tiles):
    t_all = ridx.shape[1]
    tiles_e = (counts + EXPERT_TILE - 1) // EXPERT_TILE
    tile_end = _inclusive_cumsum(tiles_e)
    offs = (tile_end - tiles_e) * EXPERT_TILE
    n_used = tile_end[-1:]
    tile_expert = jnp.minimum(
        jnp.sum((jnp.arange(n_tiles)[:, None] >= tile_end[None, :]).astype(jnp.int32), axis=1), N_EXPERTS - 1)
    experts = jnp.arange(N_EXPERTS)
    pos = ridx[TOP_K:2 * TOP_K] + jnp.sum(
        jnp.where(ridx[:TOP_K, :, None] == experts[None, None, :], offs[None, None, :], 0), axis=-1)
    gap_start = jnp.concatenate([offs + counts, n_used * EXPERT_TILE])
    gap_len = jnp.concatenate([tiles_e * EXPERT_TILE - counts, (n_tiles - n_used) * EXPERT_TILE])
    gap_end_q = _inclusive_cumsum(gap_len)
    q = jnp.arange(n_tiles * EXPERT_TILE - TOP_K * t_all)
    gap = jnp.sum((q[:, None] >= gap_end_q[None, :]).astype(jnp.int32), axis=1)
    in_gap = gap[:, None] == jnp.arange(N_EXPERTS + 1)[None, :]
    pad_rows = q + jnp.sum(jnp.where(in_gap, (gap_start - gap_end_q + gap_len)[None, :], 0), axis=1)
    return (pos.reshape(-1).astype(jnp.int32), pad_rows.astype(jnp.int32),
            tile_expert.astype(jnp.int32), n_used.astype(jnp.int32))


def kernel(x_prompt, x_sample, cache_k, cache_v, w_in, sinks, gm_ln_g, gm_ln_b, gm_ws, gm_bs,
           out_norm_a, out_norm_b, w_o, ln1_g, ln1_b, w_router, router_bias,
           w_gate, w_up, w_down, ln2_g, ln2_b):
    batch, seq, _ = x_prompt.shape
    dec_batch, dec_seq, _ = x_sample.shape
    depth = w_in.shape[0]
    past_len = PAST_LEN
    assert dec_batch * dec_seq == TOK_TILE and dec_seq == CHUNK and seq % TOK_TILE == 0
    assert cache_k.shape[2] == WINDOW
    alpha = (2 * depth) ** 0.25
    n_prompt = batch * seq
    t_all = n_prompt + dec_batch * dec_seq
    n_prompt_tiles = n_prompt // TOK_TILE
    tiles_per_seq = seq // TOK_TILE
    n_exp_tiles = (TOP_K * t_all) // EXPERT_TILE + N_EXPERTS

    tables = _rope_tables(seq, dec_seq, past_len)
    x = jnp.concatenate([x_prompt.reshape(n_prompt, D_MODEL), x_sample.reshape(-1, D_MODEL)], axis=0)
    wrt = w_router.T
    rb = router_bias.reshape(N_EXPERTS, 1)

    kp, vp, ks, vs, gms = [], [], [], [], []
    for l in range(depth):
        layer_args = (
            sinks[l], w_in[l].astype(BF16),
            cache_k[l].reshape(dec_batch, WINDOW, KV_WIDTH), cache_v[l].reshape(dec_batch, WINDOW, KV_WIDTH),
            gm_ln_g[l].reshape(1, GM_WIDTH), gm_ln_b[l].reshape(1, GM_WIDTH),
            gm_ws[l], gm_bs[l].T,
            out_norm_a[l].reshape(1, ATT_WIDTH), out_norm_b[l].reshape(1, GM_WIDTH),
            w_o[l].astype(BF16),
            ln1_g[l].reshape(1, D_MODEL), ln1_b[l].reshape(1, D_MODEL), wrt, rb)
        x1t, k_all, v_all, vn_s, ridx, rw, counts = _mixer_call(
            layer_args, x, tables, n_prompt_tiles, tiles_per_seq, alpha)

        pos, pad_rows, tile_expert, n_used = _dispatch_plan(ridx, counts[:, 0], n_exp_tiles)
        xs = _dispatch_call(pos, pad_rows, x1t, n_exp_tiles * EXPERT_TILE)
        outs = _expert_call(tile_expert, n_used, xs, w_gate[l], w_up[l], w_down[l])
        x = _combine_call(pos, x1t, rw[:TOP_K].T, ln2_g[l].reshape(1, D_MODEL), ln2_b[l].reshape(1, D_MODEL),
                          outs, alpha, n_main_tiles=n_prompt_tiles if l == depth - 1 else None)

        k_tail = k_all[:n_prompt].reshape(batch, seq, KV_WIDTH)[:, -WINDOW:]
        v_tail = v_all[:n_prompt].reshape(batch, seq, KV_WIDTH)[:, -WINDOW:]
        kp.append(k_tail.reshape(batch, WINDOW, N_KV_HEADS, HEAD_DIM))
        vp.append(v_tail.reshape(batch, WINDOW, N_KV_HEADS, HEAD_DIM))
        ks.append(k_all[n_prompt:].reshape(dec_batch, dec_seq, N_KV_HEADS, HEAD_DIM))
        vs.append(v_all[n_prompt:].reshape(dec_batch, dec_seq, N_KV_HEADS, HEAD_DIM))
        gms.append(vn_s.reshape(dec_batch, dec_seq, GM_WIDTH))

    y_prompt = x[0].reshape(batch, seq, D_MODEL)
    y_sample = x[1].reshape(dec_batch, dec_seq, D_MODEL)
    return (y_prompt, y_sample, jnp.stack(kp), jnp.stack(vp), jnp.stack(ks), jnp.stack(vs), jnp.stack(gms))
```

```python
import functools

import jax
import jax.numpy as jnp
from jax import lax
from jax.experimental import pallas as pl
from jax.experimental.pallas import tpu as pltpu

D_MODEL = 1024
CHUNK = 64
WINDOW = 128
HEAD_DIM = 64
N_Q_HEADS = 8
N_KV_HEADS = 2
Q_PER_KV = N_Q_HEADS // N_KV_HEADS
PAST_LEN = 2048
ATT_WIDTH = N_Q_HEADS * HEAD_DIM
KV_WIDTH = N_KV_HEADS * HEAD_DIM
ROPE_THETA = 10000.0
GM_GROUPS = 4
GM_CH = 128
GM_WIDTH = GM_GROUPS * GM_CH
GM_CHUNK = 128
D_MIX = ATT_WIDTH + GM_WIDTH
D_IN = ATT_WIDTH + 2 * KV_WIDTH + 2 * GM_WIDTH
N_EXPERTS = 16
N_EXPERT_GROUPS = 4
EXPERTS_PER_GROUP = N_EXPERTS // N_EXPERT_GROUPS
TOP_K = 2
D_EXPERT = 512
LN_EPS = 1e-5
NEG_INF = -1e30

LANES = 128
TILE_ROWS = D_MODEL // LANES
TOK_TILE = 512
EXPERT_TILE = 512
KEYS_PER_CHUNK = WINDOW + CHUNK
VMEM_LIMIT = 56 * 1024 * 1024

F32 = jnp.float32
BF16 = jnp.bfloat16


def _layer_norm(x, g, b):
    mu = jnp.mean(x, axis=-1, keepdims=True)
    d = x - mu
    var = jnp.mean(d * d, axis=-1, keepdims=True)
    return d * lax.rsqrt(var + LN_EPS) * g + b


def _rms_norm(x, g):
    return x * lax.rsqrt(jnp.mean(x * x, axis=-1, keepdims=True) + LN_EPS) * g


def _gelu(x):
    return 0.5 * x * (1.0 + lax.erf(x * (0.5 ** 0.5)))


def _store_token_tiles(ref, x):
    n = x.shape[0]
    for c in range(TILE_ROWS):
        ref[pl.ds(c, n, stride=TILE_ROWS), :] = x[:, c * LANES:(c + 1) * LANES]


def _load_token_tiles(ref, n, row0=0):
    return jnp.concatenate([ref[pl.ds(row0 + c, n, stride=TILE_ROWS), :] for c in range(TILE_ROWS)], axis=-1)


def _first_index_of_max(rows):
    m = rows[0]
    for r in rows[1:]:
        m = jnp.maximum(m, r)
    idx = jnp.full(m.shape, len(rows), jnp.int32)
    for e in reversed(range(len(rows))):
        idx = jnp.where(rows[e] == m, e, idx)
    return m, idx


def _mixer_kernel(n_prompt_tiles, tiles_per_seq, alpha,
                  sinks_ref, x_ref, w_in_ref, cos_ref, sa_ref, sb_ref, ck_ref, cv_ref,
                  lng_ref, lnb_ref, ws_ref, bst_ref, na_ref, nb_ref, w_o_ref,
                  l1g_ref, l1b_ref, wrt_ref, rb_ref,
                  x1t_ref, k_ref, v_ref, vn_ref, ridx_ref, rw_ref, counts_ref,
                  kd_ref, vd_ref, qm_ref, att_ref, gm_ref, cnt_ref):
    i = pl.program_id(0)
    is_sample = i >= n_prompt_tiles
    seq_start = jnp.logical_and(jnp.logical_not(is_sample), (i % tiles_per_seq) == 0)
    tm = TOK_TILE

    x = x_ref[...]
    h = jnp.dot(x.astype(BF16), w_in_ref[...], preferred_element_type=F32)

    cos = cos_ref[...]
    sa = sa_ref[...]
    sb = sb_ref[...]

    def rope(blk):
        return blk * cos + pltpu.roll(blk, LANES - HEAD_DIM // 2, 1) * sa + pltpu.roll(blk, HEAD_DIM // 2, 1) * sb

    lane = lax.broadcasted_iota(jnp.int32, (1, LANES), 1)
    lo_half = lane < HEAD_DIM

    o_k = ATT_WIDTH
    o_v = o_k + KV_WIDTH
    o_u = o_v + KV_WIDTH
    o_g = o_u + GM_WIDTH

    k_rot = rope(h[:, o_k:o_v])
    v_new = h[:, o_v:o_u]
    k_ref[...] = k_rot
    v_ref[...] = v_new

    def dup_heads(a):
        sw = pltpu.roll(a, HEAD_DIM, 1)
        return jnp.where(lo_half, a, sw).astype(BF16), jnp.where(lo_half, sw, a).astype(BF16)

    k_d = dup_heads(k_rot)
    v_d = dup_heads(v_new)

    @pl.when(jnp.logical_not(is_sample))
    def _():
        @pl.when(seq_start)
        def _():
            for g in range(N_KV_HEADS):
                kd_ref[g, 0:WINDOW, :] = jnp.zeros((WINDOW, LANES), BF16)
                vd_ref[g, 0:WINDOW, :] = jnp.zeros((WINDOW, LANES), BF16)

        @pl.when(jnp.logical_not(seq_start))
        def _():
            for g in range(N_KV_HEADS):
                kd_ref[g, 0:WINDOW, :] = kd_ref[g, tm:tm + WINDOW, :]
                vd_ref[g, 0:WINDOW, :] = vd_ref[g, tm:tm + WINDOW, :]

        for g in range(N_KV_HEADS):
            kd_ref[g, WINDOW:WINDOW + tm, :] = k_d[g]
            vd_ref[g, WINDOW:WINDOW + tm, :] = v_d[g]

    @pl.when(is_sample)
    def _():
        for b in range(tm // CHUNK):
            ck = dup_heads(ck_ref[b])
            cv = dup_heads(cv_ref[b])
            base = b * KEYS_PER_CHUNK
            for g in range(N_KV_HEADS):
                kd_ref[g, base:base + WINDOW, :] = ck[g]
                vd_ref[g, base:base + WINDOW, :] = cv[g]
                kd_ref[g, base + WINDOW:base + KEYS_PER_CHUNK, :] = k_d[g][b * CHUNK:(b + 1) * CHUNK]
                vd_ref[g, base + WINDOW:base + KEYS_PER_CHUNK, :] = v_d[g][b * CHUNK:(b + 1) * CHUNK]

    scale = HEAD_DIM ** -0.5
    for b in range(ATT_WIDTH // LANES):
        qb = rope(h[:, b * LANES:(b + 1) * LANES]) * scale
        qm_ref[2 * b] = jnp.where(lo_half, qb, 0.0).astype(BF16)
        qm_ref[2 * b + 1] = jnp.where(lo_half, 0.0, qb).astype(BF16)

    key_stride = jnp.where(is_sample, KEYS_PER_CHUNK, CHUNK)
    col = lax.broadcasted_iota(jnp.int32, (1, KEYS_PER_CHUNK), 1)
    row_blk = lax.broadcasted_iota(jnp.int32, (Q_PER_KV * CHUNK, 1), 0) // CHUNK
    sink_cols = [
        jnp.where(row_blk == 0, sinks_ref[Q_PER_KV * g],
                  jnp.where(row_blk == 1, sinks_ref[Q_PER_KV * g + 1],
                            jnp.where(row_blk == 2, sinks_ref[Q_PER_KV * g + 2], sinks_ref[Q_PER_KV * g + 3])))
        for g in range(N_KV_HEADS)]

    def chunk_body(c, carry):
        r0 = pl.multiple_of(c * CHUNK, CHUNK)
        k0 = pl.multiple_of(c * key_stride, CHUNK)
        first_valid = jnp.where(seq_start, (WINDOW // CHUNK - c) * CHUNK, 0)
        valid = col >= first_valid
        outs = []
        for g in range(N_KV_HEADS):
            q_st = jnp.concatenate([qm_ref[Q_PER_KV * g + r, pl.ds(r0, CHUNK), :] for r in range(Q_PER_KV)],
                                   axis=0)
            keys = kd_ref[g, pl.ds(k0, KEYS_PER_CHUNK), :]
            vals = vd_ref[g, pl.ds(k0, KEYS_PER_CHUNK), :]
            s = lax.dot_general(q_st, keys, (((1,), (1,)), ((), ())), preferred_element_type=F32)
            s = jnp.where(valid, s, NEG_INF)
            sink = sink_cols[g]
            m = jnp.maximum(jnp.max(s, axis=-1, keepdims=True), sink)
            p = jnp.exp(s - m)
            denom = jnp.sum(p, axis=-1, keepdims=True) + jnp.exp(sink - m)
            o = jnp.dot(p.astype(BF16), vals, preferred_element_type=F32) * (1.0 / denom)
            for bb in range(2):
                outs.append(jnp.where(lo_half, o[(2 * bb) * CHUNK:(2 * bb + 1) * CHUNK],
                                      o[(2 * bb + 1) * CHUNK:(2 * bb + 2) * CHUNK]))
        for b in range(ATT_WIDTH // LANES):
            att_ref[pl.ds(r0, CHUNK), b * LANES:(b + 1) * LANES] = outs[b]
        return carry

    lax.fori_loop(0, tm // CHUNK, chunk_body, 0, unroll=2)

    prow = lax.broadcasted_iota(jnp.int32, (GM_CHUNK, GM_CHUNK), 0)
    pcol = lax.broadcasted_iota(jnp.int32, (GM_CHUNK, GM_CHUNK), 1)
    half = GM_CHUNK // 2
    tril = pcol <= prow
    same_blk = (prow < half) == (pcol < half)
    prow1 = lax.broadcasted_iota(jnp.int32, (GM_CHUNK, 1), 0)
    bst = bst_ref[...]
    bst_s = jnp.where(prow1 < half, bst, pltpu.roll(bst, half, 0))
    bias = jnp.where(is_sample, bst_s, bst)
    for g in range(GM_GROUPS):
        wg_ = ws_ref[g]
        w_s = jnp.where(prow < half, wg_, pltpu.roll(pltpu.roll(wg_, half, 0), half, 1))
        w_eff = jnp.where(is_sample, jnp.where(same_blk, w_s, 0.0), wg_)
        w_eff = jnp.where(tril, w_eff, 0.0).astype(BF16)
        cols = slice(g * GM_CH, (g + 1) * GM_CH)
        ug = _gelu(h[:, o_u + g * GM_CH:o_u + (g + 1) * GM_CH])
        vn = _layer_norm(_gelu(h[:, o_g + g * GM_CH:o_g + (g + 1) * GM_CH]), lng_ref[:, cols], lnb_ref[:, cols])

        @pl.when(is_sample)
        def _():
            vn_ref[:, cols] = vn

        vn_b = vn.astype(BF16)
        for n in range(tm // GM_CHUNK):
            rows = slice(n * GM_CHUNK, (n + 1) * GM_CHUNK)
            s = jnp.dot(w_eff, vn_b[rows], preferred_element_type=F32) + bias[:, g:g + 1]
            gm_ref[rows, cols] = ug[rows] * s

    mixed = jnp.concatenate([_rms_norm(att_ref[...], na_ref[...]), _rms_norm(gm_ref[...], nb_ref[...])], axis=-1)
    y = alpha * x + jnp.dot(mixed.astype(BF16), w_o_ref[...], preferred_element_type=F32)
    x1 = _layer_norm(y, l1g_ref[...], l1b_ref[...])
    _store_token_tiles(x1t_ref, x1)

    logits = lax.dot_general(wrt_ref[...], x1, (((1,), (1,)), ((), ())),
                             preferred_element_type=F32, precision=lax.Precision.HIGHEST)
    scores = jax.nn.sigmoid(logits)
    sel = scores + rb_ref[...]
    sel_rows = [sel[e:e + 1, :] for e in range(N_EXPERTS)]
    sc_rows = [scores[e:e + 1, :] for e in range(N_EXPERTS)]
    grp = []
    for g in range(N_EXPERT_GROUPS):
        r = sel_rows[g * EXPERTS_PER_GROUP:(g + 1) * EXPERTS_PER_GROUP]
        best_pair = None
        for a in range(EXPERTS_PER_GROUP):
            for b in range(a + 1, EXPERTS_PER_GROUP):
                pair = r[a] + r[b]
                best_pair = pair if best_pair is None else jnp.maximum(best_pair, pair)
        grp.append(best_pair)
    _, best = _first_index_of_max(grp)
    masked = [jnp.where(best == (e // EXPERTS_PER_GROUP), sel_rows[e], NEG_INF) for e in range(N_EXPERTS)]
    _, e0 = _first_index_of_max(masked)
    masked2 = [jnp.where(e0 == e, -jnp.inf, masked[e]) for e in range(N_EXPERTS)]
    _, e1 = _first_index_of_max(masked2)
    w0 = jnp.zeros_like(sc_rows[0])
    w1 = jnp.zeros_like(sc_rows[0])
    for e in range(N_EXPERTS):
        w0 = jnp.where(e0 == e, sc_rows[e], w0)
        w1 = jnp.where(e1 == e, sc_rows[e], w1)
    wsum = w0 + w1
    rw_ref[...] = jnp.concatenate([w0 / wsum, w1 / wsum, jnp.zeros((6, tm), F32)], axis=0)

    @pl.when(i == 0)
    def _():
        cnt_ref[...] = jnp.zeros_like(cnt_ref)

    eid = lax.broadcasted_iota(jnp.int32, (N_EXPERTS, tm), 0)
    oh0 = (eid == e0).astype(F32)
    oh1 = (eid == e1).astype(F32)
    oh = oh0 + oh1
    upper = (lax.broadcasted_iota(jnp.int32, (tm, tm), 0) <= lax.broadcasted_iota(jnp.int32, (tm, tm), 1))
    incl = jnp.dot(oh.astype(BF16), upper.astype(BF16), preferred_element_type=F32)
    before = cnt_ref[:, 0:1] + incl - oh
    r0 = jnp.sum(oh0 * before, axis=0, keepdims=True).astype(jnp.int32)
    r1 = jnp.sum(oh1 * before, axis=0, keepdims=True).astype(jnp.int32)
    cnt = cnt_ref[...] + incl[:, tm - 1:tm]
    cnt_ref[...] = cnt
    counts_ref[...] = cnt.astype(jnp.int32)
    ridx_ref[...] = jnp.concatenate([e0, e1, r0, r1, jnp.zeros((4, tm), jnp.int32)], axis=0)


def _mixer_call(layer_args, x_all, tables, n_prompt_tiles, tiles_per_seq, alpha):
    (sinks, w_in, ck, cv, lng, lnb, ws, bst, na, nb, w_o, l1g, l1b, wrt, rb) = layer_args
    cos_t, sa_t, sb_t = tables
    t_all = x_all.shape[0]
    n_tiles = t_all // TOK_TILE
    n_seq_tiles = tiles_per_seq

    def const(shape):
        nd = len(shape)
        return pl.BlockSpec(shape, lambda i, _nd=nd: (0,) * _nd)

    def tab_map(i):
        return (jnp.where(i < n_prompt_tiles, i % n_seq_tiles, n_seq_tiles), 0)

    row_blk = lambda w: pl.BlockSpec((TOK_TILE, w), lambda i: (i, 0))
    in_specs = [
        pl.BlockSpec(memory_space=pltpu.SMEM),
        row_blk(D_MODEL),
        const((D_MODEL, D_IN)),
        pl.BlockSpec((TOK_TILE, LANES), tab_map),
        pl.BlockSpec((TOK_TILE, LANES), tab_map),
        pl.BlockSpec((TOK_TILE, LANES), tab_map),
        const(ck.shape), const(cv.shape),
        const((1, GM_WIDTH)), const((1, GM_WIDTH)),
        const((GM_GROUPS, GM_CHUNK, GM_CHUNK)), const((GM_CHUNK, GM_GROUPS)),
        const((1, ATT_WIDTH)), const((1, GM_WIDTH)),
        const((D_MIX, D_MODEL)),
        const((1, D_MODEL)), const((1, D_MODEL)),
        const((N_EXPERTS, D_MODEL)), const((N_EXPERTS, 1)),
    ]
    out_shape = [
        jax.ShapeDtypeStruct((t_all * TILE_ROWS, LANES), F32),
        jax.ShapeDtypeStruct((t_all, KV_WIDTH), F32),
        jax.ShapeDtypeStruct((t_all, KV_WIDTH), F32),
        jax.ShapeDtypeStruct((TOK_TILE, GM_WIDTH), F32),
        jax.ShapeDtypeStruct((8, t_all), jnp.int32),
        jax.ShapeDtypeStruct((8, t_all), F32),
        jax.ShapeDtypeStruct((N_EXPERTS, LANES), jnp.int32),
    ]
    out_specs = [
        pl.BlockSpec((TOK_TILE * TILE_ROWS, LANES), lambda i: (i, 0)),
        row_blk(KV_WIDTH), row_blk(KV_WIDTH),
        const((TOK_TILE, GM_WIDTH)),
        pl.BlockSpec((8, TOK_TILE), lambda i: (0, i)),
        pl.BlockSpec((8, TOK_TILE), lambda i: (0, i)),
        const((N_EXPERTS, LANES)),
    ]
    kd_rows = (TOK_TILE // CHUNK) * KEYS_PER_CHUNK
    scratch = [
        pltpu.VMEM((N_KV_HEADS, kd_rows, LANES), BF16),
        pltpu.VMEM((N_KV_HEADS, kd_rows, LANES), BF16),
        pltpu.VMEM((N_Q_HEADS, TOK_TILE, LANES), BF16),
        pltpu.VMEM((TOK_TILE, ATT_WIDTH), F32),
        pltpu.VMEM((TOK_TILE, GM_WIDTH), F32),
        pltpu.VMEM((N_EXPERTS, LANES), F32),
    ]
    return pl.pallas_call(
        functools.partial(_mixer_kernel, n_prompt_tiles, tiles_per_seq, alpha),
        grid=(n_tiles,),
        in_specs=in_specs, out_specs=out_specs, out_shape=out_shape,
        scratch_shapes=scratch,
        compiler_params=pltpu.CompilerParams(dimension_semantics=("arbitrary",), vmem_limit_bytes=VMEM_LIMIT),
        name="mixer",
    )(sinks, x_all, w_in, cos_t, sa_t, sb_t, ck, cv, lng, lnb, ws, bst, na, nb, w_o, l1g, l1b, wrt, rb)


def _expert_kernel(te_ref, nt_ref, xs_ref, wg_ref, wu_ref, wd_ref, out_ref, wg_s, wu_s, wd_s):
    i = pl.program_id(0)
    changed = jnp.logical_or(i == 0, te_ref[i] != te_ref[jnp.maximum(i - 1, 0)])

    @pl.when(changed)
    def _():
        wg_s[...] = wg_ref[0, 0].astype(BF16)
        wu_s[...] = wu_ref[0, 0].astype(BF16)
        wd_s[...] = wd_ref[0, 0].astype(BF16)

    @pl.when(i < nt_ref[0])
    def _():
        xs = _load_token_tiles(xs_ref, EXPERT_TILE).astype(BF16)
        g = jnp.dot(xs, wg_s[...], preferred_element_type=F32)
        u = jnp.dot(xs, wu_s[...], preferred_element_type=F32)
        hmid = (g * jax.nn.sigmoid(g)) * u
        _store_token_tiles(out_ref, jnp.dot(hmid.astype(BF16), wd_s[...], preferred_element_type=F32))

    @pl.when(i >= nt_ref[0])
    def _():
        out_ref[...] = jnp.zeros_like(out_ref)


def _expert_call(tile_expert, n_used, xs, wg, wu, wd, layer):
    n_tiles = xs.shape[0] // (EXPERT_TILE * TILE_ROWS)
    tile_blk = pl.BlockSpec((EXPERT_TILE * TILE_ROWS, LANES), lambda i, te, nt: (i, 0))
    grid_spec = pltpu.PrefetchScalarGridSpec(
        num_scalar_prefetch=2,
        grid=(n_tiles,),
        in_specs=[
            tile_blk,
            pl.BlockSpec((1, 1, D_MODEL, D_EXPERT), lambda i, te, nt: (layer, te[i], 0, 0)),
            pl.BlockSpec((1, 1, D_MODEL, D_EXPERT), lambda i, te, nt: (layer, te[i], 0, 0)),
            pl.BlockSpec((1, 1, D_EXPERT, D_MODEL), lambda i, te, nt: (layer, te[i], 0, 0)),
        ],
        out_specs=tile_blk,
        scratch_shapes=[
            pltpu.VMEM((D_MODEL, D_EXPERT), BF16),
            pltpu.VMEM((D_MODEL, D_EXPERT), BF16),
            pltpu.VMEM((D_EXPERT, D_MODEL), BF16),
        ],
    )
    return pl.pallas_call(
        _expert_kernel,
        grid_spec=grid_spec,
        out_shape=jax.ShapeDtypeStruct(xs.shape, F32),
        compiler_params=pltpu.CompilerParams(dimension_semantics=("arbitrary",), vmem_limit_bytes=VMEM_LIMIT),
        name="experts",
    )(tile_expert, n_used, xs, wg, wu, wd)


ROWS_PER_STEP = TOK_TILE * TOP_K


def _token_tile(ref, idx):
    return ref.at[pl.ds(pl.multiple_of(idx * TILE_ROWS, TILE_ROWS), TILE_ROWS)]


def _dispatch_kernel(n_tok_steps, t_all, pos_ref, pad_ref, x1t_ref, xs_hbm, zero_ref, sems):
    j = pl.program_id(0)

    @pl.when(j == 0)
    def _():
        zero_ref[...] = jnp.zeros_like(zero_ref)

    @pl.when(j < n_tok_steps)
    def _():
        def body(tt, c):
            t = j * TOK_TILE + tt
            for k in range(TOP_K):
                pltpu.make_async_copy(_token_tile(x1t_ref, tt), _token_tile(xs_hbm, pos_ref[k * t_all + t]),
                                      sems.at[k]).start()
            return c
        lax.fori_loop(0, TOK_TILE, body, 0, unroll=8)

    @pl.when(j >= n_tok_steps)
    def _():
        def body(q, c):
            for k in range(TOP_K):
                row = pad_ref[(j - n_tok_steps) * ROWS_PER_STEP + k * TOK_TILE + q]
                pltpu.make_async_copy(zero_ref, _token_tile(xs_hbm, row), sems.at[k]).start()
            return c
        lax.fori_loop(0, TOK_TILE, body, 0, unroll=8)

    for k in range(TOP_K):
        pltpu.make_async_copy(x1t_ref, xs_hbm.at[pl.ds(0, TOK_TILE * TILE_ROWS)], sems.at[k]).wait()


def _dispatch_call(pos_flat, pad_rows, x1t, n_sorted_rows):
    t_all = x1t.shape[0] // TILE_ROWS
    n_tok_steps = t_all // TOK_TILE
    n_steps = n_tok_steps + pad_rows.shape[0] // ROWS_PER_STEP
    grid_spec = pltpu.PrefetchScalarGridSpec(
        num_scalar_prefetch=2,
        grid=(n_steps,),
        in_specs=[pl.BlockSpec((TOK_TILE * TILE_ROWS, LANES),
                               lambda j, p, q: (jnp.minimum(j, n_tok_steps - 1), 0))],
        out_specs=pl.BlockSpec(memory_space=pl.ANY),
        scratch_shapes=[pltpu.VMEM((TILE_ROWS, LANES), F32), pltpu.SemaphoreType.DMA((TOP_K,))],
    )
    return pl.pallas_call(
        functools.partial(_dispatch_kernel, n_tok_steps, t_all),
        grid_spec=grid_spec,
        out_shape=jax.ShapeDtypeStruct((n_sorted_rows * TILE_ROWS, LANES), F32),
        compiler_params=pltpu.CompilerParams(dimension_semantics=("arbitrary",)),
        name="dispatch",
    )(pos_flat, pad_rows, x1t)


def _combine_kernel(alpha, t_all, n_main_tiles, pos_ref, x1t_ref, w_ref, g_ref, b_ref, outs_hbm, *rest):
    if n_main_tiles is None:
        out_ref, gbuf, sems = rest
        tail_ref = None
    else:
        out_ref, tail_ref, gbuf, sems = rest
    j = pl.program_id(0)
    slot = j % 2
    k_rows = TOK_TILE * TILE_ROWS

    n = pl.num_programs(0)

    def wait(s):
        pltpu.make_async_copy(outs_hbm.at[pl.ds(0, TOP_K * k_rows)], gbuf.at[s], sems.at[s]).wait()

    def issue(step, s):
        def body(tt, c):
            t = step * TOK_TILE + tt
            for k in range(TOP_K):
                pltpu.make_async_copy(_token_tile(outs_hbm, pos_ref[k * t_all + t]),
                                      _token_tile(gbuf.at[s], k * TOK_TILE + tt), sems.at[s]).start()
            return c
        lax.fori_loop(0, TOK_TILE, body, 0, unroll=8)

    @pl.when(j == 0)
    def _():
        issue(0, 0)

    @pl.when(j + 1 < n)
    def _():
        issue(j + 1, 1 - slot)

    wait(slot)

    w = w_ref[...]
    y = (w[:, 0:1] * _load_token_tiles(gbuf.at[slot], TOK_TILE)
         + w[:, 1:2] * _load_token_tiles(gbuf.at[slot], TOK_TILE, k_rows))
    x1 = _load_token_tiles(x1t_ref, TOK_TILE)
    res = _layer_norm(alpha * x1 + y, g_ref[...], b_ref[...])
    if tail_ref is None:
        out_ref[...] = res
    else:
        @pl.when(j < n_main_tiles)
        def _():
            out_ref[...] = res

        @pl.when(j >= n_main_tiles)
        def _():
            tail_ref[...] = res


def _combine_call(pos_flat, x1t, w_col, g, b, outs, alpha, n_main_tiles=None):
    t_all = x1t.shape[0] // TILE_ROWS
    n_tiles = t_all // TOK_TILE
    if n_main_tiles is None:
        out_specs = pl.BlockSpec((TOK_TILE, D_MODEL), lambda i, p: (i, 0))
        out_shape = jax.ShapeDtypeStruct((t_all, D_MODEL), F32)
    else:
        assert n_tiles == n_main_tiles + 1
        out_specs = [pl.BlockSpec((TOK_TILE, D_MODEL), lambda i, p: (jnp.minimum(i, n_main_tiles - 1), 0)),
                     pl.BlockSpec((TOK_TILE, D_MODEL), lambda i, p: (0, 0))]
        out_shape = [jax.ShapeDtypeStruct((n_main_tiles * TOK_TILE, D_MODEL), F32),
                     jax.ShapeDtypeStruct((TOK_TILE, D_MODEL), F32)]
    grid_spec = pltpu.PrefetchScalarGridSpec(
        num_scalar_prefetch=1,
        grid=(t_all // TOK_TILE,),
        in_specs=[
            pl.BlockSpec((TOK_TILE * TILE_ROWS, LANES), lambda i, p: (i, 0)),
            pl.BlockSpec((TOK_TILE, TOP_K), lambda i, p: (i, 0)),
            pl.BlockSpec((1, D_MODEL), lambda i, p: (0, 0)),
            pl.BlockSpec((1, D_MODEL), lambda i, p: (0, 0)),
            pl.BlockSpec(memory_space=pl.ANY),
        ],
        out_specs=out_specs,
        scratch_shapes=[pltpu.VMEM((2, TOP_K * TOK_TILE * TILE_ROWS, LANES), F32), pltpu.SemaphoreType.DMA((2,))],
    )
    return pl.pallas_call(
        functools.partial(_combine_kernel, alpha, t_all, n_main_tiles),
        grid_spec=grid_spec,
        out_shape=out_shape,
        compiler_params=pltpu.CompilerParams(dimension_semantics=("arbitrary",), vmem_limit_bytes=VMEM_LIMIT),
        name="combine",
    )(pos_flat, x1t, w_col, g, b, outs)


def _rope_tables(seq, dec_seq, past_len):
    half = HEAD_DIM // 2
    inv = ROPE_THETA ** (-jnp.arange(half, dtype=F32) / half)
    pos = jnp.concatenate([jnp.arange(seq), past_len + (jnp.arange(TOK_TILE) % dec_seq)])
    ang = pos.astype(F32)[:, None] * inv[None, :]
    cos = jnp.cos(ang)
    sin = jnp.sin(ang)
    zero = jnp.zeros_like(sin)
    reps = LANES // HEAD_DIM
    cos_t = jnp.tile(jnp.concatenate([cos, cos], -1), (1, reps))
    sa_t = jnp.tile(jnp.concatenate([-sin, zero], -1), (1, reps))
    sb_t = jnp.tile(jnp.concatenate([zero, sin], -1), (1, reps))
    return cos_t, sa_t, sb_t


def _inclusive_cumsum(v):
    n = v.shape[0]
    tri = jnp.arange(n)[None, :] <= jnp.arange(n)[:, None]
    return jnp.sum(jnp.where(tri, v[None, :], 0), axis=1)


def _dispatch_plan(ridx, counts, n_tiles):
    t_all = ridx.shape[1]
    tiles_e = (counts + EXPERT_TILE - 1) // EXPERT_TILE
    tile_end = _inclusive_cumsum(tiles_e)
    offs = (tile_end - tiles_e) * EXPERT_TILE
    n_used = tile_end[-1:]
    tile_expert = jnp.minimum(
        jnp.sum((jnp.arange(n_tiles)[:, None] >= tile_end[None, :]).astype(jnp.int32), axis=1), N_EXPERTS - 1)
    experts = jnp.arange(N_EXPERTS)
    pos = ridx[TOP_K:2 * TOP_K] + jnp.sum(
        jnp.where(ridx[:TOP_K, :, None] == experts[None, None, :], offs[None, None, :], 0), axis=-1)
    gap_start = jnp.concatenate([offs + counts, n_used * EXPERT_TILE])
    gap_len = jnp.concatenate([tiles_e * EXPERT_TILE - counts, (n_tiles - n_used) * EXPERT_TILE])
    gap_end_q = _inclusive_cumsum(gap_len)
    q = jnp.arange(n_tiles * EXPERT_TILE - TOP_K * t_all)
    gap = jnp.sum((q[:, None] >= gap_end_q[None, :]).astype(jnp.int32), axis=1)
    in_gap = gap[:, None] == jnp.arange(N_EXPERTS + 1)[None, :]
    pad_rows = q + jnp.sum(jnp.where(in_gap, (gap_start - gap_end_q + gap_len)[None, :], 0), axis=1)
    return (pos.reshape(-1).astype(jnp.int32), pad_rows.astype(jnp.int32),
            tile_expert.astype(jnp.int32), n_used.astype(jnp.int32))


def kernel(x_prompt, x_sample, cache_k, cache_v, w_in, sinks, gm_ln_g, gm_ln_b, gm_ws, gm_bs,
           out_norm_a, out_norm_b, w_o, ln1_g, ln1_b, w_router, router_bias,
           w_gate, w_up, w_down, ln2_g, ln2_b):
    batch, seq, _ = x_prompt.shape
    dec_batch, dec_seq, _ = x_sample.shape
    depth = w_in.shape[0]
    past_len = PAST_LEN
    assert dec_batch * dec_seq == TOK_TILE and dec_seq == CHUNK and seq % TOK_TILE == 0
    assert cache_k.shape[2] == WINDOW
    alpha = (2 * depth) ** 0.25
    n_prompt = batch * seq
    t_all = n_prompt + dec_batch * dec_seq
    n_prompt_tiles = n_prompt // TOK_TILE
    tiles_per_seq = seq // TOK_TILE
    n_exp_tiles = (TOP_K * t_all) // EXPERT_TILE + N_EXPERTS

    tables = _rope_tables(seq, dec_seq, past_len)
    x = jnp.concatenate([x_prompt.reshape(n_prompt, D_MODEL), x_sample.reshape(-1, D_MODEL)], axis=0)
    wrt = w_router.T
    rb = router_bias.reshape(N_EXPERTS, 1)

    kp, vp, ks, vs, gms = [], [], [], [], []
    for l in range(depth):
        layer_args = (
            sinks[l], w_in[l].astype(BF16),
            cache_k[l].reshape(dec_batch, WINDOW, KV_WIDTH), cache_v[l].reshape(dec_batch, WINDOW, KV_WIDTH),
            gm_ln_g[l].reshape(1, GM_WIDTH), gm_ln_b[l].reshape(1, GM_WIDTH),
            gm_ws[l], gm_bs[l].T,
            out_norm_a[l].reshape(1, ATT_WIDTH), out_norm_b[l].reshape(1, GM_WIDTH),
            w_o[l].astype(BF16),
            ln1_g[l].reshape(1, D_MODEL), ln1_b[l].reshape(1, D_MODEL), wrt, rb)
        x1t, k_all, v_all, vn_s, ridx, rw, counts = _mixer_call(
            layer_args, x, tables, n_prompt_tiles, tiles_per_seq, alpha)

        pos, pad_rows, tile_expert, n_used = _dispatch_plan(ridx, counts[:, 0], n_exp_tiles)
        xs = _dispatch_call(pos, pad_rows, x1t, n_exp_tiles * EXPERT_TILE)
        outs = _expert_call(tile_expert, n_used, xs, w_gate, w_up, w_down, l)
        x = _combine_call(pos, x1t, rw[:TOP_K].T, ln2_g[l].reshape(1, D_MODEL), ln2_b[l].reshape(1, D_MODEL),
                          outs, alpha, n_main_tiles=n_prompt_tiles if l == depth - 1 else None)

        k_tail = k_all[:n_prompt].reshape(batch, seq, KV_WIDTH)[:, -WINDOW:]
        v_tail = v_all[:n_prompt].reshape(batch, seq, KV_WIDTH)[:, -WINDOW:]
        kp.append(k_tail.reshape(batch, WINDOW, N_KV_HEADS, HEAD_DIM))
        vp.append(v_tail.reshape(batch, WINDOW, N_KV_HEADS, HEAD_DIM))
        ks.append(k_all[n_prompt:].reshape(dec_batch, dec_seq, N_KV_HEADS, HEAD_DIM))
        vs.append(v_all[n_prompt:].reshape(dec_batch, dec_seq, N_KV_HEADS, HEAD_DIM))
        gms.append(vn_s.reshape(dec_batch, dec_seq, GM_WIDTH))

    y_prompt = x[0].reshape(batch, seq, D_MODEL)
    y_sample = x[1].reshape(dec_batch, dec_seq, D_MODEL)
    return (y_prompt, y_sample, jnp.stack(kp), jnp.stack(vp), jnp.stack(ks), jnp.stack(vs), jnp.stack(gms))
```

```python
import functools

import jax
import jax.numpy as jnp
from jax import lax
from jax.experimental import pallas as pl
from jax.experimental.pallas import tpu as pltpu
from jax.experimental.pallas import tpu_sc as plsc

D_MODEL = 1024
CHUNK = 64
WINDOW = 128
HEAD_DIM = 64
N_Q_HEADS = 8
N_KV_HEADS = 2
Q_PER_KV = N_Q_HEADS // N_KV_HEADS
PAST_LEN = 2048
ATT_WIDTH = N_Q_HEADS * HEAD_DIM
KV_WIDTH = N_KV_HEADS * HEAD_DIM
ROPE_THETA = 10000.0
GM_GROUPS = 4
GM_CH = 128
GM_WIDTH = GM_GROUPS * GM_CH
GM_CHUNK = 128
D_MIX = ATT_WIDTH + GM_WIDTH
D_IN = ATT_WIDTH + 2 * KV_WIDTH + 2 * GM_WIDTH
N_EXPERTS = 16
N_EXPERT_GROUPS = 4
EXPERTS_PER_GROUP = N_EXPERTS // N_EXPERT_GROUPS
TOP_K = 2
D_EXPERT = 512
LN_EPS = 1e-5
NEG_INF = -1e30

LANES = 128
TILE_ROWS = D_MODEL // LANES
TOK_TILE = 512
EXPERT_TILE = 512
KEYS_PER_CHUNK = WINDOW + CHUNK
VMEM_LIMIT = 56 * 1024 * 1024

F32 = jnp.float32
BF16 = jnp.bfloat16


def _layer_norm(x, g, b):
    mu = jnp.mean(x, axis=-1, keepdims=True)
    d = x - mu
    var = jnp.mean(d * d, axis=-1, keepdims=True)
    return d * lax.rsqrt(var + LN_EPS) * g + b


def _rms_norm(x, g):
    return x * lax.rsqrt(jnp.mean(x * x, axis=-1, keepdims=True) + LN_EPS) * g


def _gelu(x):
    return 0.5 * x * (1.0 + lax.erf(x * (0.5 ** 0.5)))


def _store_token_tiles(ref, x):
    n = x.shape[0]
    for c in range(TILE_ROWS):
        ref[pl.ds(c, n, stride=TILE_ROWS), :] = x[:, c * LANES:(c + 1) * LANES]


def _load_token_tiles(ref, n, row0=0):
    return jnp.concatenate([ref[pl.ds(row0 + c, n, stride=TILE_ROWS), :] for c in range(TILE_ROWS)], axis=-1)


def _first_index_of_max(rows):
    m = rows[0]
    for r in rows[1:]:
        m = jnp.maximum(m, r)
    idx = jnp.full(m.shape, len(rows), jnp.int32)
    for e in reversed(range(len(rows))):
        idx = jnp.where(rows[e] == m, e, idx)
    return m, idx


def _mixer_kernel(n_prompt_tiles, tiles_per_seq, alpha,
                  sinks_ref, x_ref, w_in_ref, cos_ref, sa_ref, sb_ref, ck_ref, cv_ref,
                  lng_ref, lnb_ref, ws_ref, bst_ref, na_ref, nb_ref, w_o_ref,
                  l1g_ref, l1b_ref, wrt_ref, rb_ref,
                  x1t_ref, k_ref, v_ref, vn_ref, ridx_ref, rw_ref, counts_ref,
                  kd_ref, vd_ref, qm_ref, att_ref, gm_ref, cnt_ref):
    i = pl.program_id(0)
    is_sample = i >= n_prompt_tiles
    seq_start = jnp.logical_and(jnp.logical_not(is_sample), (i % tiles_per_seq) == 0)
    tm = TOK_TILE

    x = x_ref[...]
    h = jnp.dot(x.astype(BF16), w_in_ref[...], preferred_element_type=F32)

    cos = cos_ref[...]
    sa = sa_ref[...]
    sb = sb_ref[...]

    def rope(blk):
        return blk * cos + pltpu.roll(blk, LANES - HEAD_DIM // 2, 1) * sa + pltpu.roll(blk, HEAD_DIM // 2, 1) * sb

    lane = lax.broadcasted_iota(jnp.int32, (1, LANES), 1)
    lo_half = lane < HEAD_DIM

    o_k = ATT_WIDTH
    o_v = o_k + KV_WIDTH
    o_u = o_v + KV_WIDTH
    o_g = o_u + GM_WIDTH

    k_rot = rope(h[:, o_k:o_v])
    v_new = h[:, o_v:o_u]
    k_ref[...] = k_rot
    v_ref[...] = v_new

    def dup_heads(a):
        sw = pltpu.roll(a, HEAD_DIM, 1)
        return jnp.where(lo_half, a, sw).astype(BF16), jnp.where(lo_half, sw, a).astype(BF16)

    k_d = dup_heads(k_rot)
    v_d = dup_heads(v_new)

    @pl.when(jnp.logical_not(is_sample))
    def _():
        @pl.when(seq_start)
        def _():
            for g in range(N_KV_HEADS):
                kd_ref[g, 0:WINDOW, :] = jnp.zeros((WINDOW, LANES), BF16)
                vd_ref[g, 0:WINDOW, :] = jnp.zeros((WINDOW, LANES), BF16)

        @pl.when(jnp.logical_not(seq_start))
        def _():
            for g in range(N_KV_HEADS):
                kd_ref[g, 0:WINDOW, :] = kd_ref[g, tm:tm + WINDOW, :]
                vd_ref[g, 0:WINDOW, :] = vd_ref[g, tm:tm + WINDOW, :]

        for g in range(N_KV_HEADS):
            kd_ref[g, WINDOW:WINDOW + tm, :] = k_d[g]
            vd_ref[g, WINDOW:WINDOW + tm, :] = v_d[g]

    @pl.when(is_sample)
    def _():
        for b in range(tm // CHUNK):
            ck = dup_heads(ck_ref[b])
            cv = dup_heads(cv_ref[b])
            base = b * KEYS_PER_CHUNK
            for g in range(N_KV_HEADS):
                kd_ref[g, base:base + WINDOW, :] = ck[g]
                vd_ref[g, base:base + WINDOW, :] = cv[g]
                kd_ref[g, base + WINDOW:base + KEYS_PER_CHUNK, :] = k_d[g][b * CHUNK:(b + 1) * CHUNK]
                vd_ref[g, base + WINDOW:base + KEYS_PER_CHUNK, :] = v_d[g][b * CHUNK:(b + 1) * CHUNK]

    scale = HEAD_DIM ** -0.5
    for b in range(ATT_WIDTH // LANES):
        qb = rope(h[:, b * LANES:(b + 1) * LANES]) * scale
        qm_ref[2 * b] = jnp.where(lo_half, qb, 0.0).astype(BF16)
        qm_ref[2 * b + 1] = jnp.where(lo_half, 0.0, qb).astype(BF16)

    key_stride = jnp.where(is_sample, KEYS_PER_CHUNK, CHUNK)
    col = lax.broadcasted_iota(jnp.int32, (1, KEYS_PER_CHUNK), 1)
    row_blk = lax.broadcasted_iota(jnp.int32, (Q_PER_KV * CHUNK, 1), 0) // CHUNK
    sink_cols = [
        jnp.where(row_blk == 0, sinks_ref[Q_PER_KV * g],
                  jnp.where(row_blk == 1, sinks_ref[Q_PER_KV * g + 1],
                            jnp.where(row_blk == 2, sinks_ref[Q_PER_KV * g + 2], sinks_ref[Q_PER_KV * g + 3])))
        for g in range(N_KV_HEADS)]

    def chunk_body(c, carry):
        r0 = pl.multiple_of(c * CHUNK, CHUNK)
        k0 = pl.multiple_of(c * key_stride, CHUNK)
        first_valid = jnp.where(seq_start, (WINDOW // CHUNK - c) * CHUNK, 0)
        valid = col >= first_valid
        outs = []
        for g in range(N_KV_HEADS):
            q_st = jnp.concatenate([qm_ref[Q_PER_KV * g + r, pl.ds(r0, CHUNK), :] for r in range(Q_PER_KV)],
                                   axis=0)
            keys = kd_ref[g, pl.ds(k0, KEYS_PER_CHUNK), :]
            vals = vd_ref[g, pl.ds(k0, KEYS_PER_CHUNK), :]
            s = lax.dot_general(q_st, keys, (((1,), (1,)), ((), ())), preferred_element_type=F32)
            s = jnp.where(valid, s, NEG_INF)
            sink = sink_cols[g]
            m = jnp.maximum(jnp.max(s, axis=-1, keepdims=True), sink)
            p = jnp.exp(s - m)
            denom = jnp.sum(p, axis=-1, keepdims=True) + jnp.exp(sink - m)
            o = jnp.dot(p.astype(BF16), vals, preferred_element_type=F32) * (1.0 / denom)
            for bb in range(2):
                outs.append(jnp.where(lo_half, o[(2 * bb) * CHUNK:(2 * bb + 1) * CHUNK],
                                      o[(2 * bb + 1) * CHUNK:(2 * bb + 2) * CHUNK]))
        for b in range(ATT_WIDTH // LANES):
            att_ref[pl.ds(r0, CHUNK), b * LANES:(b + 1) * LANES] = outs[b]
        return carry

    lax.fori_loop(0, tm // CHUNK, chunk_body, 0, unroll=2)

    prow = lax.broadcasted_iota(jnp.int32, (GM_CHUNK, GM_CHUNK), 0)
    pcol = lax.broadcasted_iota(jnp.int32, (GM_CHUNK, GM_CHUNK), 1)
    half = GM_CHUNK // 2
    tril = pcol <= prow
    same_blk = (prow < half) == (pcol < half)
    prow1 = lax.broadcasted_iota(jnp.int32, (GM_CHUNK, 1), 0)
    bst = bst_ref[...]
    bst_s = jnp.where(prow1 < half, bst, pltpu.roll(bst, half, 0))
    bias = jnp.where(is_sample, bst_s, bst)
    for g in range(GM_GROUPS):
        wg_ = ws_ref[g]
        w_s = jnp.where(prow < half, wg_, pltpu.roll(pltpu.roll(wg_, half, 0), half, 1))
        w_eff = jnp.where(is_sample, jnp.where(same_blk, w_s, 0.0), wg_)
        w_eff = jnp.where(tril, w_eff, 0.0).astype(BF16)
        cols = slice(g * GM_CH, (g + 1) * GM_CH)
        ug = _gelu(h[:, o_u + g * GM_CH:o_u + (g + 1) * GM_CH])
        vn = _layer_norm(_gelu(h[:, o_g + g * GM_CH:o_g + (g + 1) * GM_CH]), lng_ref[:, cols], lnb_ref[:, cols])

        @pl.when(is_sample)
        def _():
            vn_ref[:, cols] = vn

        vn_b = vn.astype(BF16)
        for n in range(tm // GM_CHUNK):
            rows = slice(n * GM_CHUNK, (n + 1) * GM_CHUNK)
            s = jnp.dot(w_eff, vn_b[rows], preferred_element_type=F32) + bias[:, g:g + 1]
            gm_ref[rows, cols] = ug[rows] * s

    mixed = jnp.concatenate([_rms_norm(att_ref[...], na_ref[...]), _rms_norm(gm_ref[...], nb_ref[...])], axis=-1)
    y = alpha * x + jnp.dot(mixed.astype(BF16), w_o_ref[...], preferred_element_type=F32)
    x1 = _layer_norm(y, l1g_ref[...], l1b_ref[...])
    _store_token_tiles(x1t_ref, x1)

    logits = lax.dot_general(wrt_ref[...], x1, (((1,), (1,)), ((), ())),
                             preferred_element_type=F32, precision=lax.Precision.HIGHEST)
    scores = jax.nn.sigmoid(logits)
    sel = scores + rb_ref[...]
    sel_rows = [sel[e:e + 1, :] for e in range(N_EXPERTS)]
    sc_rows = [scores[e:e + 1, :] for e in range(N_EXPERTS)]
    grp = []
    for g in range(N_EXPERT_GROUPS):
        r = sel_rows[g * EXPERTS_PER_GROUP:(g + 1) * EXPERTS_PER_GROUP]
        best_pair = None
        for a in range(EXPERTS_PER_GROUP):
            for b in range(a + 1, EXPERTS_PER_GROUP):
                pair = r[a] + r[b]
                best_pair = pair if best_pair is None else jnp.maximum(best_pair, pair)
        grp.append(best_pair)
    _, best = _first_index_of_max(grp)
    masked = [jnp.where(best == (e // EXPERTS_PER_GROUP), sel_rows[e], NEG_INF) for e in range(N_EXPERTS)]
    _, e0 = _first_index_of_max(masked)
    masked2 = [jnp.where(e0 == e, -jnp.inf, masked[e]) for e in range(N_EXPERTS)]
    _, e1 = _first_index_of_max(masked2)
    w0 = jnp.zeros_like(sc_rows[0])
    w1 = jnp.zeros_like(sc_rows[0])
    for e in range(N_EXPERTS):
        w0 = jnp.where(e0 == e, sc_rows[e], w0)
        w1 = jnp.where(e1 == e, sc_rows[e], w1)
    wsum = w0 + w1
    rw_ref[...] = jnp.concatenate([w0 / wsum, w1 / wsum, jnp.zeros((6, tm), F32)], axis=0)

    @pl.when(i == 0)
    def _():
        cnt_ref[...] = jnp.zeros_like(cnt_ref)

    eid = lax.broadcasted_iota(jnp.int32, (N_EXPERTS, tm), 0)
    oh0 = (eid == e0).astype(F32)
    oh1 = (eid == e1).astype(F32)
    oh = oh0 + oh1
    upper = (lax.broadcasted_iota(jnp.int32, (tm, tm), 0) <= lax.broadcasted_iota(jnp.int32, (tm, tm), 1))
    incl = jnp.dot(oh.astype(BF16), upper.astype(BF16), preferred_element_type=F32)
    before = cnt_ref[:, 0:1] + incl - oh
    r0 = jnp.sum(oh0 * before, axis=0, keepdims=True).astype(jnp.int32)
    r1 = jnp.sum(oh1 * before, axis=0, keepdims=True).astype(jnp.int32)
    cnt = cnt_ref[...] + incl[:, tm - 1:tm]
    cnt_ref[...] = cnt
    counts_ref[...] = cnt.astype(jnp.int32)
    ridx_ref[...] = jnp.concatenate([e0, e1, r0, r1, jnp.zeros((4, tm), jnp.int32)], axis=0)


def _mixer_call(layer_args, x_all, tables, n_prompt_tiles, tiles_per_seq, alpha):
    (sinks, w_in, ck, cv, lng, lnb, ws, bst, na, nb, w_o, l1g, l1b, wrt, rb) = layer_args
    cos_t, sa_t, sb_t = tables
    t_all = x_all.shape[0]
    n_tiles = t_all // TOK_TILE
    n_seq_tiles = tiles_per_seq

    def const(shape):
        nd = len(shape)
        return pl.BlockSpec(shape, lambda i, _nd=nd: (0,) * _nd)

    def tab_map(i):
        return (jnp.where(i < n_prompt_tiles, i % n_seq_tiles, n_seq_tiles), 0)

    row_blk = lambda w: pl.BlockSpec((TOK_TILE, w), lambda i: (i, 0))
    in_specs = [
        pl.BlockSpec(memory_space=pltpu.SMEM),
        row_blk(D_MODEL),
        const((D_MODEL, D_IN)),
        pl.BlockSpec((TOK_TILE, LANES), tab_map),
        pl.BlockSpec((TOK_TILE, LANES), tab_map),
        pl.BlockSpec((TOK_TILE, LANES), tab_map),
        const(ck.shape), const(cv.shape),
        const((1, GM_WIDTH)), const((1, GM_WIDTH)),
        const((GM_GROUPS, GM_CHUNK, GM_CHUNK)), const((GM_CHUNK, GM_GROUPS)),
        const((1, ATT_WIDTH)), const((1, GM_WIDTH)),
        const((D_MIX, D_MODEL)),
        const((1, D_MODEL)), const((1, D_MODEL)),
        const((N_EXPERTS, D_MODEL)), const((N_EXPERTS, 1)),
    ]
    out_shape = [
        jax.ShapeDtypeStruct((t_all * TILE_ROWS, LANES), F32),
        jax.ShapeDtypeStruct((t_all, KV_WIDTH), F32),
        jax.ShapeDtypeStruct((t_all, KV_WIDTH), F32),
        jax.ShapeDtypeStruct((TOK_TILE, GM_WIDTH), F32),
        jax.ShapeDtypeStruct((8, t_all), jnp.int32),
        jax.ShapeDtypeStruct((8, t_all), F32),
        jax.ShapeDtypeStruct((N_EXPERTS, LANES), jnp.int32),
    ]
    out_specs = [
        pl.BlockSpec((TOK_TILE * TILE_ROWS, LANES), lambda i: (i, 0)),
        row_blk(KV_WIDTH), row_blk(KV_WIDTH),
        const((TOK_TILE, GM_WIDTH)),
        pl.BlockSpec((8, TOK_TILE), lambda i: (0, i)),
        pl.BlockSpec((8, TOK_TILE), lambda i: (0, i)),
        const((N_EXPERTS, LANES)),
    ]
    kd_rows = (TOK_TILE // CHUNK) * KEYS_PER_CHUNK
    scratch = [
        pltpu.VMEM((N_KV_HEADS, kd_rows, LANES), BF16),
        pltpu.VMEM((N_KV_HEADS, kd_rows, LANES), BF16),
        pltpu.VMEM((N_Q_HEADS, TOK_TILE, LANES), BF16),
        pltpu.VMEM((TOK_TILE, ATT_WIDTH), F32),
        pltpu.VMEM((TOK_TILE, GM_WIDTH), F32),
        pltpu.VMEM((N_EXPERTS, LANES), F32),
    ]
    return pl.pallas_call(
        functools.partial(_mixer_kernel, n_prompt_tiles, tiles_per_seq, alpha),
        grid=(n_tiles,),
        in_specs=in_specs, out_specs=out_specs, out_shape=out_shape,
        scratch_shapes=scratch,
        compiler_params=pltpu.CompilerParams(dimension_semantics=("arbitrary",), vmem_limit_bytes=VMEM_LIMIT),
        name="mixer",
    )(sinks, x_all, w_in, cos_t, sa_t, sb_t, ck, cv, lng, lnb, ws, bst, na, nb, w_o, l1g, l1b, wrt, rb)


def _expert_kernel(te_ref, nt_ref, xs_ref, wg_ref, wu_ref, wd_ref, out_ref, wg_s, wu_s, wd_s):
    i = pl.program_id(0)
    changed = jnp.logical_or(i == 0, te_ref[i] != te_ref[jnp.maximum(i - 1, 0)])

    @pl.when(changed)
    def _():
        wg_s[...] = wg_ref[0, 0].astype(BF16)
        wu_s[...] = wu_ref[0, 0].astype(BF16)
        wd_s[...] = wd_ref[0, 0].astype(BF16)

    @pl.when(i < nt_ref[0])
    def _():
        xs = _load_token_tiles(xs_ref, EXPERT_TILE).astype(BF16)
        g = jnp.dot(xs, wg_s[...], preferred_element_type=F32)
        u = jnp.dot(xs, wu_s[...], preferred_element_type=F32)
        hmid = (g * jax.nn.sigmoid(g)) * u
        _store_token_tiles(out_ref, jnp.dot(hmid.astype(BF16), wd_s[...], preferred_element_type=F32))

    @pl.when(i >= nt_ref[0])
    def _():
        out_ref[...] = jnp.zeros_like(out_ref)


def _expert_call(tile_expert, n_used, xs, wg, wu, wd, layer):
    n_tiles = xs.shape[0] // (EXPERT_TILE * TILE_ROWS)
    tile_blk = pl.BlockSpec((EXPERT_TILE * TILE_ROWS, LANES), lambda i, te, nt: (i, 0))
    grid_spec = pltpu.PrefetchScalarGridSpec(
        num_scalar_prefetch=2,
        grid=(n_tiles,),
        in_specs=[
            tile_blk,
            pl.BlockSpec((1, 1, D_MODEL, D_EXPERT), lambda i, te, nt: (layer, te[i], 0, 0)),
            pl.BlockSpec((1, 1, D_MODEL, D_EXPERT), lambda i, te, nt: (layer, te[i], 0, 0)),
            pl.BlockSpec((1, 1, D_EXPERT, D_MODEL), lambda i, te, nt: (layer, te[i], 0, 0)),
        ],
        out_specs=tile_blk,
        scratch_shapes=[
            pltpu.VMEM((D_MODEL, D_EXPERT), BF16),
            pltpu.VMEM((D_MODEL, D_EXPERT), BF16),
            pltpu.VMEM((D_EXPERT, D_MODEL), BF16),
        ],
    )
    return pl.pallas_call(
        _expert_kernel,
        grid_spec=grid_spec,
        out_shape=jax.ShapeDtypeStruct(xs.shape, F32),
        compiler_params=pltpu.CompilerParams(dimension_semantics=("arbitrary",), vmem_limit_bytes=VMEM_LIMIT),
        name="experts",
    )(tile_expert, n_used, xs, wg, wu, wd)


ROWS_PER_STEP = TOK_TILE * TOP_K


def _token_tile(ref, idx):
    return ref.at[pl.ds(pl.multiple_of(idx * TILE_ROWS, TILE_ROWS), TILE_ROWS)]


def _dispatch_kernel(n_tok_steps, t_all, pos_ref, pad_ref, x1t_ref, xs_hbm, zero_ref, sems):
    j = pl.program_id(0)

    @pl.when(j == 0)
    def _():
        zero_ref[...] = jnp.zeros_like(zero_ref)

    @pl.when(j < n_tok_steps)
    def _():
        def body(tt, c):
            t = j * TOK_TILE + tt
            for k in range(TOP_K):
                pltpu.make_async_copy(_token_tile(x1t_ref, tt), _token_tile(xs_hbm, pos_ref[k * t_all + t]),
                                      sems.at[k]).start()
            return c
        lax.fori_loop(0, TOK_TILE, body, 0, unroll=8)

    @pl.when(j >= n_tok_steps)
    def _():
        def body(q, c):
            for k in range(TOP_K):
                row = pad_ref[(j - n_tok_steps) * ROWS_PER_STEP + k * TOK_TILE + q]
                pltpu.make_async_copy(zero_ref, _token_tile(xs_hbm, row), sems.at[k]).start()
            return c
        lax.fori_loop(0, TOK_TILE, body, 0, unroll=8)

    for k in range(TOP_K):
        pltpu.make_async_copy(x1t_ref, xs_hbm.at[pl.ds(0, TOK_TILE * TILE_ROWS)], sems.at[k]).wait()


def _dispatch_call(pos_flat, pad_rows, x1t, n_sorted_rows):
    t_all = x1t.shape[0] // TILE_ROWS
    n_tok_steps = t_all // TOK_TILE
    n_steps = n_tok_steps + pad_rows.shape[0] // ROWS_PER_STEP
    grid_spec = pltpu.PrefetchScalarGridSpec(
        num_scalar_prefetch=2,
        grid=(n_steps,),
        in_specs=[pl.BlockSpec((TOK_TILE * TILE_ROWS, LANES),
                               lambda j, p, q: (jnp.minimum(j, n_tok_steps - 1), 0))],
        out_specs=pl.BlockSpec(memory_space=pl.ANY),
        scratch_shapes=[pltpu.VMEM((TILE_ROWS, LANES), F32), pltpu.SemaphoreType.DMA((TOP_K,))],
    )
    return pl.pallas_call(
        functools.partial(_dispatch_kernel, n_tok_steps, t_all),
        grid_spec=grid_spec,
        out_shape=jax.ShapeDtypeStruct((n_sorted_rows * TILE_ROWS, LANES), F32),
        compiler_params=pltpu.CompilerParams(dimension_semantics=("arbitrary",)),
        name="dispatch",
    )(pos_flat, pad_rows, x1t)


def _combine_kernel(alpha, t_all, n_main_tiles, pos_ref, x1t_ref, w_ref, g_ref, b_ref, outs_hbm, *rest):
    if n_main_tiles is None:
        out_ref, gbuf, sems = rest
        tail_ref = None
    else:
        out_ref, tail_ref, gbuf, sems = rest
    j = pl.program_id(0)
    slot = j % 2
    k_rows = TOK_TILE * TILE_ROWS

    n = pl.num_programs(0)

    def wait(s):
        pltpu.make_async_copy(outs_hbm.at[pl.ds(0, TOP_K * k_rows)], gbuf.at[s], sems.at[s]).wait()

    def issue(step, s):
        def body(tt, c):
            t = step * TOK_TILE + tt
            for k in range(TOP_K):
                pltpu.make_async_copy(_token_tile(outs_hbm, pos_ref[k * t_all + t]),
                                      _token_tile(gbuf.at[s], k * TOK_TILE + tt), sems.at[s]).start()
            return c
        lax.fori_loop(0, TOK_TILE, body, 0, unroll=8)

    @pl.when(j == 0)
    def _():
        issue(0, 0)

    @pl.when(j + 1 < n)
    def _():
        issue(j + 1, 1 - slot)

    wait(slot)

    w = w_ref[...]
    y = (w[:, 0:1] * _load_token_tiles(gbuf.at[slot], TOK_TILE)
         + w[:, 1:2] * _load_token_tiles(gbuf.at[slot], TOK_TILE, k_rows))
    x1 = _load_token_tiles(x1t_ref, TOK_TILE)
    res = _layer_norm(alpha * x1 + y, g_ref[...], b_ref[...])
    if tail_ref is None:
        out_ref[...] = res
    else:
        @pl.when(j < n_main_tiles)
        def _():
            out_ref[...] = res

        @pl.when(j >= n_main_tiles)
        def _():
            tail_ref[...] = res


def _combine_call(pos_flat, x1t, w_col, g, b, outs, alpha, n_main_tiles=None):
    t_all = x1t.shape[0] // TILE_ROWS
    n_tiles = t_all // TOK_TILE
    if n_main_tiles is None:
        out_specs = pl.BlockSpec((TOK_TILE, D_MODEL), lambda i, p: (i, 0))
        out_shape = jax.ShapeDtypeStruct((t_all, D_MODEL), F32)
    else:
        assert n_tiles == n_main_tiles + 1
        out_specs = [pl.BlockSpec((TOK_TILE, D_MODEL), lambda i, p: (jnp.minimum(i, n_main_tiles - 1), 0)),
                     pl.BlockSpec((TOK_TILE, D_MODEL), lambda i, p: (0, 0))]
        out_shape = [jax.ShapeDtypeStruct((n_main_tiles * TOK_TILE, D_MODEL), F32),
                     jax.ShapeDtypeStruct((TOK_TILE, D_MODEL), F32)]
    grid_spec = pltpu.PrefetchScalarGridSpec(
        num_scalar_prefetch=1,
        grid=(t_all // TOK_TILE,),
        in_specs=[
            pl.BlockSpec((TOK_TILE * TILE_ROWS, LANES), lambda i, p: (i, 0)),
            pl.BlockSpec((TOK_TILE, TOP_K), lambda i, p: (i, 0)),
            pl.BlockSpec((1, D_MODEL), lambda i, p: (0, 0)),
            pl.BlockSpec((1, D_MODEL), lambda i, p: (0, 0)),
            pl.BlockSpec(memory_space=pl.ANY),
        ],
        out_specs=out_specs,
        scratch_shapes=[pltpu.VMEM((2, TOP_K * TOK_TILE * TILE_ROWS, LANES), F32), pltpu.SemaphoreType.DMA((2,))],
    )
    return pl.pallas_call(
        functools.partial(_combine_kernel, alpha, t_all, n_main_tiles),
        grid_spec=grid_spec,
        out_shape=out_shape,
        compiler_params=pltpu.CompilerParams(dimension_semantics=("arbitrary",), vmem_limit_bytes=VMEM_LIMIT),
        name="combine",
    )(pos_flat, x1t, w_col, g, b, outs)


SC_CHUNK = 48
SC_PAD_CHUNK = 32


def _sc_workers():
    info = plsc.get_sparse_core_info()
    return info.num_cores, info.num_cores * info.num_subcores


def _sc_dispatch_call(pos_flat, pad_rows, x1t, n_sorted_rows):
    t_all = x1t.shape[0] // TILE_ROWS
    nc, nw = _sc_workers()
    per_w = t_all // nw
    n_ch = per_w // SC_CHUNK
    n_pad = pad_rows.shape[0]
    pad_per_w = n_pad // nw
    n_pch = pad_per_w // SC_PAD_CHUNK
    assert n_ch * SC_CHUNK * nw == t_all and n_pch * SC_PAD_CHUNK * nw == n_pad
    zeros = jnp.zeros((SC_PAD_CHUNK, TILE_ROWS, LANES), F32)

    @functools.partial(
        pl.kernel, mesh=plsc.VectorSubcoreMesh(core_axis_name="c", subcore_axis_name="s"),
        out_type=jax.ShapeDtypeStruct((n_sorted_rows, TILE_ROWS, LANES), F32),
        scratch_types=[
            pltpu.VMEM((SC_CHUNK,), jnp.int32), pltpu.VMEM((SC_CHUNK,), jnp.int32),
            pltpu.VMEM((SC_CHUNK, TILE_ROWS, LANES), F32),
            pltpu.VMEM((SC_PAD_CHUNK,), jnp.int32),
            pltpu.VMEM((SC_PAD_CHUNK, TILE_ROWS, LANES), F32),
        ],
    )
    def k(pos_hbm, pad_hbm, x_hbm, z_hbm, xs_hbm, i0_v, i1_v, rows_v, ip_v, z_v):
        wid = lax.axis_index("s") * nc + lax.axis_index("c")

        @pl.loop(0, n_ch)
        def _(j):
            b = pl.multiple_of(wid * per_w + j * SC_CHUNK, 8)
            pltpu.sync_copy(pos_hbm.at[pl.ds(b, SC_CHUNK)], i0_v)
            pltpu.sync_copy(pos_hbm.at[pl.ds(t_all + b, SC_CHUNK)], i1_v)
            pltpu.sync_copy(x_hbm.at[pl.ds(b, SC_CHUNK)], rows_v)
            pltpu.sync_copy(rows_v, xs_hbm.at[i0_v])
            pltpu.sync_copy(rows_v, xs_hbm.at[i1_v])

        pltpu.sync_copy(z_hbm, z_v)

        @pl.loop(0, n_pch)
        def _(j):
            b = pl.multiple_of(wid * pad_per_w + j * SC_PAD_CHUNK, 8)
            pltpu.sync_copy(pad_hbm.at[pl.ds(b, SC_PAD_CHUNK)], ip_v)
            pltpu.sync_copy(z_v, xs_hbm.at[ip_v])

    xs = k(pos_flat, pad_rows, x1t.reshape(t_all, TILE_ROWS, LANES), zeros)
    return xs.reshape(n_sorted_rows * TILE_ROWS, LANES)


def _sc_gather_call(pos_flat, outs):
    n_assign = pos_flat.shape[0]
    nc, nw = _sc_workers()
    per_w = n_assign // nw
    n_ch = per_w // SC_CHUNK
    assert n_ch * SC_CHUNK * nw == n_assign

    @functools.partial(
        pl.kernel, mesh=plsc.VectorSubcoreMesh(core_axis_name="c", subcore_axis_name="s"),
        out_type=jax.ShapeDtypeStruct((n_assign, TILE_ROWS, LANES), F32),
        scratch_types=[pltpu.VMEM((SC_CHUNK,), jnp.int32), pltpu.VMEM((SC_CHUNK, TILE_ROWS, LANES), F32)],
    )
    def k(pos_hbm, o_hbm, y_hbm, i_v, rows_v):
        wid = lax.axis_index("s") * nc + lax.axis_index("c")

        @pl.loop(0, n_ch)
        def _(j):
            b = pl.multiple_of(wid * per_w + j * SC_CHUNK, 8)
            pltpu.sync_copy(pos_hbm.at[pl.ds(b, SC_CHUNK)], i_v)
            pltpu.sync_copy(o_hbm.at[i_v], rows_v)
            pltpu.sync_copy(rows_v, y_hbm.at[pl.ds(b, SC_CHUNK)])

    y = k(pos_flat, outs.reshape(-1, TILE_ROWS, LANES))
    return y.reshape(n_assign * TILE_ROWS, LANES)


def _combine_dense_kernel(alpha, n_main_tiles, x1t_ref, y0_ref, y1_ref, w_ref, g_ref, b_ref, out_ref, *tail):
    j = pl.program_id(0)
    w = w_ref[...]
    y = w[:, 0:1] * _load_token_tiles(y0_ref, TOK_TILE) + w[:, 1:2] * _load_token_tiles(y1_ref, TOK_TILE)
    res = _layer_norm(alpha * _load_token_tiles(x1t_ref, TOK_TILE) + y, g_ref[...], b_ref[...])
    if n_main_tiles is None:
        out_ref[...] = res
    else:
        @pl.when(j < n_main_tiles)
        def _():
            out_ref[...] = res

        @pl.when(j >= n_main_tiles)
        def _():
            tail[0][...] = res


def _combine_dense_call(x1t, y_pair, w_col, g, b, alpha, n_main_tiles=None):
    t_all = x1t.shape[0] // TILE_ROWS
    n_tiles = t_all // TOK_TILE
    if n_main_tiles is None:
        out_specs = pl.BlockSpec((TOK_TILE, D_MODEL), lambda i: (i, 0))
        out_shape = jax.ShapeDtypeStruct((t_all, D_MODEL), F32)
    else:
        assert n_tiles == n_main_tiles + 1
        out_specs = [pl.BlockSpec((TOK_TILE, D_MODEL), lambda i: (jnp.minimum(i, n_main_tiles - 1), 0)),
                     pl.BlockSpec((TOK_TILE, D_MODEL), lambda i: (0, 0))]
        out_shape = [jax.ShapeDtypeStruct((n_main_tiles * TOK_TILE, D_MODEL), F32),
                     jax.ShapeDtypeStruct((TOK_TILE, D_MODEL), F32)]
    tile = (TOK_TILE * TILE_ROWS, LANES)
    return pl.pallas_call(
        functools.partial(_combine_dense_kernel, alpha, n_main_tiles),
        grid=(n_tiles,),
        in_specs=[
            pl.BlockSpec(tile, lambda i: (i, 0)),
            pl.BlockSpec(tile, lambda i: (i, 0)),
            pl.BlockSpec(tile, lambda i: (i + n_tiles, 0)),
            pl.BlockSpec((TOK_TILE, TOP_K), lambda i: (i, 0)),
            pl.BlockSpec((1, D_MODEL), lambda i: (0, 0)),
            pl.BlockSpec((1, D_MODEL), lambda i: (0, 0)),
        ],
        out_specs=out_specs,
        out_shape=out_shape,
        compiler_params=pltpu.CompilerParams(dimension_semantics=("arbitrary",), vmem_limit_bytes=VMEM_LIMIT),
        name="combine",
    )(x1t, y_pair, y_pair, w_col, g, b)


def _rope_tables(seq, dec_seq, past_len):
    half = HEAD_DIM // 2
    inv = ROPE_THETA ** (-jnp.arange(half, dtype=F32) / half)
    pos = jnp.concatenate([jnp.arange(seq), past_len + (jnp.arange(TOK_TILE) % dec_seq)])
    ang = pos.astype(F32)[:, None] * inv[None, :]
    cos = jnp.cos(ang)
    sin = jnp.sin(ang)
    zero = jnp.zeros_like(sin)
    reps = LANES // HEAD_DIM
    cos_t = jnp.tile(jnp.concatenate([cos, cos], -1), (1, reps))
    sa_t = jnp.tile(jnp.concatenate([-sin, zero], -1), (1, reps))
    sb_t = jnp.tile(jnp.concatenate([zero, sin], -1), (1, reps))
    return cos_t, sa_t, sb_t


def _inclusive_cumsum(v):
    n = v.shape[0]
    tri = jnp.arange(n)[None, :] <= jnp.arange(n)[:, None]
    return jnp.sum(jnp.where(tri, v[None, :], 0), axis=1)


def _dispatch_plan(ridx, counts, n_tiles):
    t_all = ridx.shape[1]
    tiles_e = (counts + EXPERT_TILE - 1) // EXPERT_TILE
    tile_end = _inclusive_cumsum(tiles_e)
    offs = (tile_end - tiles_e) * EXPERT_TILE
    n_used = tile_end[-1:]
    tile_expert = jnp.minimum(
        jnp.sum((jnp.arange(n_tiles)[:, None] >= tile_end[None, :]).astype(jnp.int32), axis=1), N_EXPERTS - 1)
    experts = jnp.arange(N_EXPERTS)
    pos = ridx[TOP_K:2 * TOP_K] + jnp.sum(
        jnp.where(ridx[:TOP_K, :, None] == experts[None, None, :], offs[None, None, :], 0), axis=-1)
    gap_start = jnp.concatenate([offs + counts, n_used * EXPERT_TILE])
    gap_len = jnp.concatenate([tiles_e * EXPERT_TILE - counts, (n_tiles - n_used) * EXPERT_TILE])
    gap_end_q = _inclusive_cumsum(gap_len)
    q = jnp.arange(n_tiles * EXPERT_TILE - TOP_K * t_all)
    gap = jnp.sum((q[:, None] >= gap_end_q[None, :]).astype(jnp.int32), axis=1)
    in_gap = gap[:, None] == jnp.arange(N_EXPERTS + 1)[None, :]
    pad_rows = q + jnp.sum(jnp.where(in_gap, (gap_start - gap_end_q + gap_len)[None, :], 0), axis=1)
    return (pos.reshape(-1).astype(jnp.int32), pad_rows.astype(jnp.int32),
            tile_expert.astype(jnp.int32), n_used.astype(jnp.int32))


def kernel(x_prompt, x_sample, cache_k, cache_v, w_in, sinks, gm_ln_g, gm_ln_b, gm_ws, gm_bs,
           out_norm_a, out_norm_b, w_o, ln1_g, ln1_b, w_router, router_bias,
           w_gate, w_up, w_down, ln2_g, ln2_b):
    batch, seq, _ = x_prompt.shape
    dec_batch, dec_seq, _ = x_sample.shape
    depth = w_in.shape[0]
    past_len = PAST_LEN
    assert dec_batch * dec_seq == TOK_TILE and dec_seq == CHUNK and seq % TOK_TILE == 0
    assert cache_k.shape[2] == WINDOW
    alpha = (2 * depth) ** 0.25
    n_prompt = batch * seq
    t_all = n_prompt + dec_batch * dec_seq
    n_prompt_tiles = n_prompt // TOK_TILE
    tiles_per_seq = seq // TOK_TILE
    n_exp_tiles = (TOP_K * t_all) // EXPERT_TILE + N_EXPERTS

    tables = _rope_tables(seq, dec_seq, past_len)
    x = jnp.concatenate([x_prompt.reshape(n_prompt, D_MODEL), x_sample.reshape(-1, D_MODEL)], axis=0)
    wrt = w_router.T
    rb = router_bias.reshape(N_EXPERTS, 1)

    kp, vp, ks, vs, gms = [], [], [], [], []
    for l in range(depth):
        layer_args = (
            sinks[l], w_in[l].astype(BF16),
            cache_k[l].reshape(dec_batch, WINDOW, KV_WIDTH), cache_v[l].reshape(dec_batch, WINDOW, KV_WIDTH),
            gm_ln_g[l].reshape(1, GM_WIDTH), gm_ln_b[l].reshape(1, GM_WIDTH),
            gm_ws[l], gm_bs[l].T,
            out_norm_a[l].reshape(1, ATT_WIDTH), out_norm_b[l].reshape(1, GM_WIDTH),
            w_o[l].astype(BF16),
            ln1_g[l].reshape(1, D_MODEL), ln1_b[l].reshape(1, D_MODEL), wrt, rb)
        x1t, k_all, v_all, vn_s, ridx, rw, counts = _mixer_call(
            layer_args, x, tables, n_prompt_tiles, tiles_per_seq, alpha)

        pos, pad_rows, tile_expert, n_used = _dispatch_plan(ridx, counts[:, 0], n_exp_tiles)
        xs = _sc_dispatch_call(pos, pad_rows, x1t, n_exp_tiles * EXPERT_TILE)
        outs = _expert_call(tile_expert, n_used, xs, w_gate, w_up, w_down, l)
        y_pair = _sc_gather_call(pos, outs)
        x = _combine_dense_call(x1t, y_pair, rw[:TOP_K].T, ln2_g[l].reshape(1, D_MODEL),
                                ln2_b[l].reshape(1, D_MODEL), alpha,
                                n_main_tiles=n_prompt_tiles if l == depth - 1 else None)

        k_tail = k_all[:n_prompt].reshape(batch, seq, KV_WIDTH)[:, -WINDOW:]
        v_tail = v_all[:n_prompt].reshape(batch, seq, KV_WIDTH)[:, -WINDOW:]
        kp.append(k_tail.reshape(batch, WINDOW, N_KV_HEADS, HEAD_DIM))
        vp.append(v_tail.reshape(batch, WINDOW, N_KV_HEADS, HEAD_DIM))
        ks.append(k_all[n_prompt:].reshape(dec_batch, dec_seq, N_KV_HEADS, HEAD_DIM))
        vs.append(v_all[n_prompt:].reshape(dec_batch, dec_seq, N_KV_HEADS, HEAD_DIM))
        gms.append(vn_s.reshape(dec_batch, dec_seq, GM_WIDTH))

    y_prompt = x[0].reshape(batch, seq, D_MODEL)
    y_sample = x[1].reshape(dec_batch, dec_seq, D_MODEL)
    return (y_prompt, y_sample, jnp.stack(kp), jnp.stack(vp), jnp.stack(ks), jnp.stack(vs), jnp.stack(gms))
```

```python
import functools

import jax
import jax.numpy as jnp
from jax import lax
from jax.experimental import pallas as pl
from jax.experimental.pallas import tpu as pltpu
from jax.experimental.pallas import tpu_sc as plsc

D_MODEL = 1024
CHUNK = 64
WINDOW = 128
HEAD_DIM = 64
N_Q_HEADS = 8
N_KV_HEADS = 2
Q_PER_KV = N_Q_HEADS // N_KV_HEADS
PAST_LEN = 2048
ATT_WIDTH = N_Q_HEADS * HEAD_DIM
KV_WIDTH = N_KV_HEADS * HEAD_DIM
ROPE_THETA = 10000.0
GM_GROUPS = 4
GM_CH = 128
GM_WIDTH = GM_GROUPS * GM_CH
GM_CHUNK = 128
D_MIX = ATT_WIDTH + GM_WIDTH
D_IN = ATT_WIDTH + 2 * KV_WIDTH + 2 * GM_WIDTH
N_EXPERTS = 16
N_EXPERT_GROUPS = 4
EXPERTS_PER_GROUP = N_EXPERTS // N_EXPERT_GROUPS
TOP_K = 2
D_EXPERT = 512
LN_EPS = 1e-5
NEG_INF = -1e30

LANES = 128
TILE_ROWS = D_MODEL // LANES
TOK_TILE = 512
EXPERT_TILE = 512
KEYS_PER_CHUNK = WINDOW + CHUNK
VMEM_LIMIT = 56 * 1024 * 1024

F32 = jnp.float32
BF16 = jnp.bfloat16


def _layer_norm(x, g, b):
    mu = jnp.mean(x, axis=-1, keepdims=True)
    d = x - mu
    var = jnp.mean(d * d, axis=-1, keepdims=True)
    return d * lax.rsqrt(var + LN_EPS) * g + b


def _rms_norm(x, g):
    return x * lax.rsqrt(jnp.mean(x * x, axis=-1, keepdims=True) + LN_EPS) * g


def _gelu(x):
    return 0.5 * x * (1.0 + lax.erf(x * (0.5 ** 0.5)))


def _store_token_tiles(ref, x):
    n = x.shape[0]
    for c in range(TILE_ROWS):
        ref[pl.ds(c, n, stride=TILE_ROWS), :] = x[:, c * LANES:(c + 1) * LANES]


def _load_token_tiles(ref, n, row0=0):
    return jnp.concatenate([ref[pl.ds(row0 + c, n, stride=TILE_ROWS), :] for c in range(TILE_ROWS)], axis=-1)


def _first_index_of_max(rows):
    m = rows[0]
    for r in rows[1:]:
        m = jnp.maximum(m, r)
    idx = jnp.full(m.shape, len(rows), jnp.int32)
    for e in reversed(range(len(rows))):
        idx = jnp.where(rows[e] == m, e, idx)
    return m, idx


def _mixer_kernel(n_prompt_tiles, tiles_per_seq, alpha,
                  sinks_ref, x_ref, w_in_ref, cos_ref, sa_ref, sb_ref, ck_ref, cv_ref,
                  lng_ref, lnb_ref, ws_ref, bst_ref, na_ref, nb_ref, w_o_ref,
                  l1g_ref, l1b_ref, wrt_ref, rb_ref,
                  x1t_ref, k_ref, v_ref, vn_ref, ridx_ref, rw_ref, counts_ref,
                  kd_ref, vd_ref, qm_ref, att_ref, gm_ref, cnt_ref):
    i = pl.program_id(0)
    is_sample = i >= n_prompt_tiles
    seq_start = jnp.logical_and(jnp.logical_not(is_sample), (i % tiles_per_seq) == 0)
    tm = TOK_TILE

    x = x_ref[...]
    h = jnp.dot(x.astype(BF16), w_in_ref[...], preferred_element_type=F32)

    cos = cos_ref[...]
    sa = sa_ref[...]
    sb = sb_ref[...]

    def rope(blk):
        return blk * cos + pltpu.roll(blk, LANES - HEAD_DIM // 2, 1) * sa + pltpu.roll(blk, HEAD_DIM // 2, 1) * sb

    lane = lax.broadcasted_iota(jnp.int32, (1, LANES), 1)
    lo_half = lane < HEAD_DIM

    o_k = ATT_WIDTH
    o_v = o_k + KV_WIDTH
    o_u = o_v + KV_WIDTH
    o_g = o_u + GM_WIDTH

    k_rot = rope(h[:, o_k:o_v])
    v_new = h[:, o_v:o_u]
    k_ref[...] = k_rot
    v_ref[...] = v_new

    def dup_heads(a):
        sw = pltpu.roll(a, HEAD_DIM, 1)
        return jnp.where(lo_half, a, sw).astype(BF16), jnp.where(lo_half, sw, a).astype(BF16)

    k_d = dup_heads(k_rot)
    v_d = dup_heads(v_new)

    @pl.when(jnp.logical_not(is_sample))
    def _():
        @pl.when(seq_start)
        def _():
            for g in range(N_KV_HEADS):
                kd_ref[g, 0:WINDOW, :] = jnp.zeros((WINDOW, LANES), BF16)
                vd_ref[g, 0:WINDOW, :] = jnp.zeros((WINDOW, LANES), BF16)

        @pl.when(jnp.logical_not(seq_start))
        def _():
            for g in range(N_KV_HEADS):
                kd_ref[g, 0:WINDOW, :] = kd_ref[g, tm:tm + WINDOW, :]
                vd_ref[g, 0:WINDOW, :] = vd_ref[g, tm:tm + WINDOW, :]

        for g in range(N_KV_HEADS):
            kd_ref[g, WINDOW:WINDOW + tm, :] = k_d[g]
            vd_ref[g, WINDOW:WINDOW + tm, :] = v_d[g]

    @pl.when(is_sample)
    def _():
        for b in range(tm // CHUNK):
            ck = dup_heads(ck_ref[b])
            cv = dup_heads(cv_ref[b])
            base = b * KEYS_PER_CHUNK
            for g in range(N_KV_HEADS):
                kd_ref[g, base:base + WINDOW, :] = ck[g]
                vd_ref[g, base:base + WINDOW, :] = cv[g]
                kd_ref[g, base + WINDOW:base + KEYS_PER_CHUNK, :] = k_d[g][b * CHUNK:(b + 1) * CHUNK]
                vd_ref[g, base + WINDOW:base + KEYS_PER_CHUNK, :] = v_d[g][b * CHUNK:(b + 1) * CHUNK]

    scale = HEAD_DIM ** -0.5
    for b in range(ATT_WIDTH // LANES):
        qb = rope(h[:, b * LANES:(b + 1) * LANES]) * scale
        qm_ref[2 * b] = jnp.where(lo_half, qb, 0.0).astype(BF16)
        qm_ref[2 * b + 1] = jnp.where(lo_half, 0.0, qb).astype(BF16)

    key_stride = jnp.where(is_sample, KEYS_PER_CHUNK, CHUNK)
    col = lax.broadcasted_iota(jnp.int32, (1, KEYS_PER_CHUNK), 1)
    row_blk = lax.broadcasted_iota(jnp.int32, (Q_PER_KV * CHUNK, 1), 0) // CHUNK
    sink_cols = [
        jnp.where(row_blk == 0, sinks_ref[Q_PER_KV * g],
                  jnp.where(row_blk == 1, sinks_ref[Q_PER_KV * g + 1],
                            jnp.where(row_blk == 2, sinks_ref[Q_PER_KV * g + 2], sinks_ref[Q_PER_KV * g + 3])))
        for g in range(N_KV_HEADS)]

    def chunk_body(c, carry):
        r0 = pl.multiple_of(c * CHUNK, CHUNK)
        k0 = pl.multiple_of(c * key_stride, CHUNK)
        first_valid = jnp.where(seq_start, (WINDOW // CHUNK - c) * CHUNK, 0)
        valid = col >= first_valid
        outs = []
        for g in range(N_KV_HEADS):
            q_st = jnp.concatenate([qm_ref[Q_PER_KV * g + r, pl.ds(r0, CHUNK), :] for r in range(Q_PER_KV)],
                                   axis=0)
            keys = kd_ref[g, pl.ds(k0, KEYS_PER_CHUNK), :]
            vals = vd_ref[g, pl.ds(k0, KEYS_PER_CHUNK), :]
            s = lax.dot_general(q_st, keys, (((1,), (1,)), ((), ())), preferred_element_type=F32)
            s = jnp.where(valid, s, NEG_INF)
            sink = sink_cols[g]
            m = jnp.maximum(jnp.max(s, axis=-1, keepdims=True), sink)
            p = jnp.exp(s - m)
            denom = jnp.sum(p, axis=-1, keepdims=True) + jnp.exp(sink - m)
            o = jnp.dot(p.astype(BF16), vals, preferred_element_type=F32) * (1.0 / denom)
            for bb in range(2):
                outs.append(jnp.where(lo_half, o[(2 * bb) * CHUNK:(2 * bb + 1) * CHUNK],
                                      o[(2 * bb + 1) * CHUNK:(2 * bb + 2) * CHUNK]))
        for b in range(ATT_WIDTH // LANES):
            att_ref[pl.ds(r0, CHUNK), b * LANES:(b + 1) * LANES] = outs[b]
        return carry

    lax.fori_loop(0, tm // CHUNK, chunk_body, 0, unroll=8)

    prow = lax.broadcasted_iota(jnp.int32, (GM_CHUNK, GM_CHUNK), 0)
    pcol = lax.broadcasted_iota(jnp.int32, (GM_CHUNK, GM_CHUNK), 1)
    half = GM_CHUNK // 2
    tril = pcol <= prow
    same_blk = (prow < half) == (pcol < half)
    prow1 = lax.broadcasted_iota(jnp.int32, (GM_CHUNK, 1), 0)
    bst = bst_ref[...]
    bst_s = jnp.where(prow1 < half, bst, pltpu.roll(bst, half, 0))
    bias = jnp.where(is_sample, bst_s, bst)
    for g in range(GM_GROUPS):
        wg_ = ws_ref[g]
        w_s = jnp.where(prow < half, wg_, pltpu.roll(pltpu.roll(wg_, half, 0), half, 1))
        w_eff = jnp.where(is_sample, jnp.where(same_blk, w_s, 0.0), wg_)
        w_eff = jnp.where(tril, w_eff, 0.0).astype(BF16)
        cols = slice(g * GM_CH, (g + 1) * GM_CH)
        ug = _gelu(h[:, o_u + g * GM_CH:o_u + (g + 1) * GM_CH])
        vn = _layer_norm(_gelu(h[:, o_g + g * GM_CH:o_g + (g + 1) * GM_CH]), lng_ref[:, cols], lnb_ref[:, cols])

        @pl.when(is_sample)
        def _():
            vn_ref[:, cols] = vn

        vn_b = vn.astype(BF16)
        for n in range(tm // GM_CHUNK):
            rows = slice(n * GM_CHUNK, (n + 1) * GM_CHUNK)
            s = jnp.dot(w_eff, vn_b[rows], preferred_element_type=F32) + bias[:, g:g + 1]
            gm_ref[rows, cols] = ug[rows] * s

    mixed = jnp.concatenate([_rms_norm(att_ref[...], na_ref[...]), _rms_norm(gm_ref[...], nb_ref[...])], axis=-1)
    y = alpha * x + jnp.dot(mixed.astype(BF16), w_o_ref[...], preferred_element_type=F32)
    x1 = _layer_norm(y, l1g_ref[...], l1b_ref[...])
    _store_token_tiles(x1t_ref, x1)

    logits = lax.dot_general(wrt_ref[...], x1, (((1,), (1,)), ((), ())),
                             preferred_element_type=F32, precision=lax.Precision.HIGHEST)
    scores = jax.nn.sigmoid(logits)
    sel = scores + rb_ref[...]
    sel_rows = [sel[e:e + 1, :] for e in range(N_EXPERTS)]
    sc_rows = [scores[e:e + 1, :] for e in range(N_EXPERTS)]
    grp = []
    for g in range(N_EXPERT_GROUPS):
        r = sel_rows[g * EXPERTS_PER_GROUP:(g + 1) * EXPERTS_PER_GROUP]
        best_pair = None
        for a in range(EXPERTS_PER_GROUP):
            for b in range(a + 1, EXPERTS_PER_GROUP):
                pair = r[a] + r[b]
                best_pair = pair if best_pair is None else jnp.maximum(best_pair, pair)
        grp.append(best_pair)
    _, best = _first_index_of_max(grp)
    masked = [jnp.where(best == (e // EXPERTS_PER_GROUP), sel_rows[e], NEG_INF) for e in range(N_EXPERTS)]
    _, e0 = _first_index_of_max(masked)
    masked2 = [jnp.where(e0 == e, -jnp.inf, masked[e]) for e in range(N_EXPERTS)]
    _, e1 = _first_index_of_max(masked2)
    w0 = jnp.zeros_like(sc_rows[0])
    w1 = jnp.zeros_like(sc_rows[0])
    for e in range(N_EXPERTS):
        w0 = jnp.where(e0 == e, sc_rows[e], w0)
        w1 = jnp.where(e1 == e, sc_rows[e], w1)
    wsum = w0 + w1
    rw_ref[...] = jnp.concatenate([w0 / wsum, w1 / wsum, jnp.zeros((6, tm), F32)], axis=0)

    @pl.when(i == 0)
    def _():
        cnt_ref[...] = jnp.zeros_like(cnt_ref)

    eid = lax.broadcasted_iota(jnp.int32, (N_EXPERTS, tm), 0)
    oh0 = (eid == e0).astype(F32)
    oh1 = (eid == e1).astype(F32)
    oh = oh0 + oh1
    upper = (lax.broadcasted_iota(jnp.int32, (tm, tm), 0) <= lax.broadcasted_iota(jnp.int32, (tm, tm), 1))
    incl = jnp.dot(oh.astype(BF16), upper.astype(BF16), preferred_element_type=F32)
    before = cnt_ref[:, 0:1] + incl - oh
    r0 = jnp.sum(oh0 * before, axis=0, keepdims=True).astype(jnp.int32)
    r1 = jnp.sum(oh1 * before, axis=0, keepdims=True).astype(jnp.int32)
    cnt = cnt_ref[...] + incl[:, tm - 1:tm]
    cnt_ref[...] = cnt
    counts_ref[...] = cnt.astype(jnp.int32)
    ridx_ref[...] = jnp.concatenate([e0, e1, r0, r1, jnp.zeros((4, tm), jnp.int32)], axis=0)


def _mixer_call(layer_args, x_all, tables, n_prompt_tiles, tiles_per_seq, alpha):
    (sinks, w_in, ck, cv, lng, lnb, ws, bst, na, nb, w_o, l1g, l1b, wrt, rb) = layer_args
    cos_t, sa_t, sb_t = tables
    t_all = x_all.shape[0]
    n_tiles = t_all // TOK_TILE
    n_seq_tiles = tiles_per_seq

    def const(shape):
        nd = len(shape)
        return pl.BlockSpec(shape, lambda i, _nd=nd: (0,) * _nd)

    def tab_map(i):
        return (jnp.where(i < n_prompt_tiles, i % n_seq_tiles, n_seq_tiles), 0)

    row_blk = lambda w: pl.BlockSpec((TOK_TILE, w), lambda i: (i, 0))
    in_specs = [
        pl.BlockSpec(memory_space=pltpu.SMEM),
        row_blk(D_MODEL),
        const((D_MODEL, D_IN)),
        pl.BlockSpec((TOK_TILE, LANES), tab_map),
        pl.BlockSpec((TOK_TILE, LANES), tab_map),
        pl.BlockSpec((TOK_TILE, LANES), tab_map),
        const(ck.shape), const(cv.shape),
        const((1, GM_WIDTH)), const((1, GM_WIDTH)),
        const((GM_GROUPS, GM_CHUNK, GM_CHUNK)), const((GM_CHUNK, GM_GROUPS)),
        const((1, ATT_WIDTH)), const((1, GM_WIDTH)),
        const((D_MIX, D_MODEL)),
        const((1, D_MODEL)), const((1, D_MODEL)),
        const((N_EXPERTS, D_MODEL)), const((N_EXPERTS, 1)),
    ]
    out_shape = [
        jax.ShapeDtypeStruct((t_all * TILE_ROWS, LANES), F32),
        jax.ShapeDtypeStruct((t_all, KV_WIDTH), F32),
        jax.ShapeDtypeStruct((t_all, KV_WIDTH), F32),
        jax.ShapeDtypeStruct((TOK_TILE, GM_WIDTH), F32),
        jax.ShapeDtypeStruct((8, t_all), jnp.int32),
        jax.ShapeDtypeStruct((8, t_all), F32),
        jax.ShapeDtypeStruct((N_EXPERTS, LANES), jnp.int32),
    ]
    out_specs = [
        pl.BlockSpec((TOK_TILE * TILE_ROWS, LANES), lambda i: (i, 0)),
        row_blk(KV_WIDTH), row_blk(KV_WIDTH),
        const((TOK_TILE, GM_WIDTH)),
        pl.BlockSpec((8, TOK_TILE), lambda i: (0, i)),
        pl.BlockSpec((8, TOK_TILE), lambda i: (0, i)),
        const((N_EXPERTS, LANES)),
    ]
    kd_rows = (TOK_TILE // CHUNK) * KEYS_PER_CHUNK
    scratch = [
        pltpu.VMEM((N_KV_HEADS, kd_rows, LANES), BF16),
        pltpu.VMEM((N_KV_HEADS, kd_rows, LANES), BF16),
        pltpu.VMEM((N_Q_HEADS, TOK_TILE, LANES), BF16),
        pltpu.VMEM((TOK_TILE, ATT_WIDTH), F32),
        pltpu.VMEM((TOK_TILE, GM_WIDTH), F32),
        pltpu.VMEM((N_EXPERTS, LANES), F32),
    ]
    return pl.pallas_call(
        functools.partial(_mixer_kernel, n_prompt_tiles, tiles_per_seq, alpha),
        grid=(n_tiles,),
        in_specs=in_specs, out_specs=out_specs, out_shape=out_shape,
        scratch_shapes=scratch,
        compiler_params=pltpu.CompilerParams(dimension_semantics=("arbitrary",), vmem_limit_bytes=VMEM_LIMIT),
        name="mixer",
    )(sinks, x_all, w_in, cos_t, sa_t, sb_t, ck, cv, lng, lnb, ws, bst, na, nb, w_o, l1g, l1b, wrt, rb)


def _expert_kernel(te_ref, nt_ref, xs_ref, wg_ref, wu_ref, wd_ref, out_ref, wg_s, wu_s, wd_s):
    i = pl.program_id(0)
    changed = jnp.logical_or(i == 0, te_ref[i] != te_ref[jnp.maximum(i - 1, 0)])

    @pl.when(changed)
    def _():
        wg_s[...] = wg_ref[0, 0].astype(BF16)
        wu_s[...] = wu_ref[0, 0].astype(BF16)
        wd_s[...] = wd_ref[0, 0].astype(BF16)

    @pl.when(i < nt_ref[0])
    def _():
        xs = _load_token_tiles(xs_ref, EXPERT_TILE).astype(BF16)
        g = jnp.dot(xs, wg_s[...], preferred_element_type=F32)
        u = jnp.dot(xs, wu_s[...], preferred_element_type=F32)
        hmid = (g * jax.nn.sigmoid(g)) * u
        _store_token_tiles(out_ref, jnp.dot(hmid.astype(BF16), wd_s[...], preferred_element_type=F32))

    @pl.when(i >= nt_ref[0])
    def _():
        out_ref[...] = jnp.zeros_like(out_ref)


def _expert_call(tile_expert, n_used, xs, wg, wu, wd, layer):
    n_tiles = xs.shape[0] // (EXPERT_TILE * TILE_ROWS)
    tile_blk = pl.BlockSpec((EXPERT_TILE * TILE_ROWS, LANES), lambda i, te, nt: (i, 0))
    grid_spec = pltpu.PrefetchScalarGridSpec(
        num_scalar_prefetch=2,
        grid=(n_tiles,),
        in_specs=[
            tile_blk,
            pl.BlockSpec((1, 1, D_MODEL, D_EXPERT), lambda i, te, nt: (layer, te[i], 0, 0)),
            pl.BlockSpec((1, 1, D_MODEL, D_EXPERT), lambda i, te, nt: (layer, te[i], 0, 0)),
            pl.BlockSpec((1, 1, D_EXPERT, D_MODEL), lambda i, te, nt: (layer, te[i], 0, 0)),
        ],
        out_specs=tile_blk,
        scratch_shapes=[
            pltpu.VMEM((D_MODEL, D_EXPERT), BF16),
            pltpu.VMEM((D_MODEL, D_EXPERT), BF16),
            pltpu.VMEM((D_EXPERT, D_MODEL), BF16),
        ],
    )
    return pl.pallas_call(
        _expert_kernel,
        grid_spec=grid_spec,
        out_shape=jax.ShapeDtypeStruct(xs.shape, F32),
        compiler_params=pltpu.CompilerParams(dimension_semantics=("arbitrary",), vmem_limit_bytes=VMEM_LIMIT),
        name="experts",
    )(tile_expert, n_used, xs, wg, wu, wd)


ROWS_PER_STEP = TOK_TILE * TOP_K


def _token_tile(ref, idx):
    return ref.at[pl.ds(pl.multiple_of(idx * TILE_ROWS, TILE_ROWS), TILE_ROWS)]


def _dispatch_kernel(n_tok_steps, t_all, pos_ref, pad_ref, x1t_ref, xs_hbm, zero_ref, sems):
    j = pl.program_id(0)

    @pl.when(j == 0)
    def _():
        zero_ref[...] = jnp.zeros_like(zero_ref)

    @pl.when(j < n_tok_steps)
    def _():
        def body(tt, c):
            t = j * TOK_TILE + tt
            for k in range(TOP_K):
                pltpu.make_async_copy(_token_tile(x1t_ref, tt), _token_tile(xs_hbm, pos_ref[k * t_all + t]),
                                      sems.at[k]).start()
            return c
        lax.fori_loop(0, TOK_TILE, body, 0, unroll=8)

    @pl.when(j >= n_tok_steps)
    def _():
        def body(q, c):
            for k in range(TOP_K):
                row = pad_ref[(j - n_tok_steps) * ROWS_PER_STEP + k * TOK_TILE + q]
                pltpu.make_async_copy(zero_ref, _token_tile(xs_hbm, row), sems.at[k]).start()
            return c
        lax.fori_loop(0, TOK_TILE, body, 0, unroll=8)

    for k in range(TOP_K):
        pltpu.make_async_copy(x1t_ref, xs_hbm.at[pl.ds(0, TOK_TILE * TILE_ROWS)], sems.at[k]).wait()


def _dispatch_call(pos_flat, pad_rows, x1t, n_sorted_rows):
    t_all = x1t.shape[0] // TILE_ROWS
    n_tok_steps = t_all // TOK_TILE
    n_steps = n_tok_steps + pad_rows.shape[0] // ROWS_PER_STEP
    grid_spec = pltpu.PrefetchScalarGridSpec(
        num_scalar_prefetch=2,
        grid=(n_steps,),
        in_specs=[pl.BlockSpec((TOK_TILE * TILE_ROWS, LANES),
                               lambda j, p, q: (jnp.minimum(j, n_tok_steps - 1), 0))],
        out_specs=pl.BlockSpec(memory_space=pl.ANY),
        scratch_shapes=[pltpu.VMEM((TILE_ROWS, LANES), F32), pltpu.SemaphoreType.DMA((TOP_K,))],
    )
    return pl.pallas_call(
        functools.partial(_dispatch_kernel, n_tok_steps, t_all),
        grid_spec=grid_spec,
        out_shape=jax.ShapeDtypeStruct((n_sorted_rows * TILE_ROWS, LANES), F32),
        compiler_params=pltpu.CompilerParams(dimension_semantics=("arbitrary",)),
        name="dispatch",
    )(pos_flat, pad_rows, x1t)


def _combine_kernel(alpha, t_all, n_main_tiles, pos_ref, x1t_ref, w_ref, g_ref, b_ref, outs_hbm, *rest):
    if n_main_tiles is None:
        out_ref, gbuf, sems = rest
        tail_ref = None
    else:
        out_ref, tail_ref, gbuf, sems = rest
    j = pl.program_id(0)
    slot = j % 2
    k_rows = TOK_TILE * TILE_ROWS

    n = pl.num_programs(0)

    def wait(s):
        pltpu.make_async_copy(outs_hbm.at[pl.ds(0, TOP_K * k_rows)], gbuf.at[s], sems.at[s]).wait()

    def issue(step, s):
        def body(tt, c):
            t = step * TOK_TILE + tt
            for k in range(TOP_K):
                pltpu.make_async_copy(_token_tile(outs_hbm, pos_ref[k * t_all + t]),
                                      _token_tile(gbuf.at[s], k * TOK_TILE + tt), sems.at[s]).start()
            return c
        lax.fori_loop(0, TOK_TILE, body, 0, unroll=8)

    @pl.when(j == 0)
    def _():
        issue(0, 0)

    @pl.when(j + 1 < n)
    def _():
        issue(j + 1, 1 - slot)

    wait(slot)

    w = w_ref[...]
    y = (w[:, 0:1] * _load_token_tiles(gbuf.at[slot], TOK_TILE)
         + w[:, 1:2] * _load_token_tiles(gbuf.at[slot], TOK_TILE, k_rows))
    x1 = _load_token_tiles(x1t_ref, TOK_TILE)
    res = _layer_norm(alpha * x1 + y, g_ref[...], b_ref[...])
    if tail_ref is None:
        out_ref[...] = res
    else:
        @pl.when(j < n_main_tiles)
        def _():
            out_ref[...] = res

        @pl.when(j >= n_main_tiles)
        def _():
            tail_ref[...] = res


def _combine_call(pos_flat, x1t, w_col, g, b, outs, alpha, n_main_tiles=None):
    t_all = x1t.shape[0] // TILE_ROWS
    n_tiles = t_all // TOK_TILE
    if n_main_tiles is None:
        out_specs = pl.BlockSpec((TOK_TILE, D_MODEL), lambda i, p: (i, 0))
        out_shape = jax.ShapeDtypeStruct((t_all, D_MODEL), F32)
    else:
        assert n_tiles == n_main_tiles + 1
        out_specs = [pl.BlockSpec((TOK_TILE, D_MODEL), lambda i, p: (jnp.minimum(i, n_main_tiles - 1), 0)),
                     pl.BlockSpec((TOK_TILE, D_MODEL), lambda i, p: (0, 0))]
        out_shape = [jax.ShapeDtypeStruct((n_main_tiles * TOK_TILE, D_MODEL), F32),
                     jax.ShapeDtypeStruct((TOK_TILE, D_MODEL), F32)]
    grid_spec = pltpu.PrefetchScalarGridSpec(
        num_scalar_prefetch=1,
        grid=(t_all // TOK_TILE,),
        in_specs=[
            pl.BlockSpec((TOK_TILE * TILE_ROWS, LANES), lambda i, p: (i, 0)),
            pl.BlockSpec((TOK_TILE, TOP_K), lambda i, p: (i, 0)),
            pl.BlockSpec((1, D_MODEL), lambda i, p: (0, 0)),
            pl.BlockSpec((1, D_MODEL), lambda i, p: (0, 0)),
            pl.BlockSpec(memory_space=pl.ANY),
        ],
        out_specs=out_specs,
        scratch_shapes=[pltpu.VMEM((2, TOP_K * TOK_TILE * TILE_ROWS, LANES), F32), pltpu.SemaphoreType.DMA((2,))],
    )
    return pl.pallas_call(
        functools.partial(_combine_kernel, alpha, t_all, n_main_tiles),
        grid_spec=grid_spec,
        out_shape=out_shape,
        compiler_params=pltpu.CompilerParams(dimension_semantics=("arbitrary",), vmem_limit_bytes=VMEM_LIMIT),
        name="combine",
    )(pos_flat, x1t, w_col, g, b, outs)


SC_CHUNK = 48
SC_PAD_CHUNK = 32


def _sc_workers():
    info = plsc.get_sparse_core_info()
    return info.num_cores, info.num_cores * info.num_subcores


def _sc_dispatch_call(pos_flat, pad_rows, x1t, n_sorted_rows):
    t_all = x1t.shape[0] // TILE_ROWS
    nc, nw = _sc_workers()
    per_w = t_all // nw
    n_ch = per_w // SC_CHUNK
    n_pad = pad_rows.shape[0]
    pad_per_w = n_pad // nw
    n_pch = pad_per_w // SC_PAD_CHUNK
    assert n_ch * SC_CHUNK * nw == t_all and n_pch * SC_PAD_CHUNK * nw == n_pad
    zeros = jnp.zeros((SC_PAD_CHUNK, TILE_ROWS, LANES), F32)

    @functools.partial(
        pl.kernel, mesh=plsc.VectorSubcoreMesh(core_axis_name="c", subcore_axis_name="s"),
        out_type=jax.ShapeDtypeStruct((n_sorted_rows, TILE_ROWS, LANES), F32),
        scratch_types=[
            pltpu.VMEM((SC_CHUNK,), jnp.int32), pltpu.VMEM((SC_CHUNK,), jnp.int32),
            pltpu.VMEM((SC_CHUNK, TILE_ROWS, LANES), F32),
            pltpu.VMEM((SC_PAD_CHUNK,), jnp.int32),
            pltpu.VMEM((SC_PAD_CHUNK, TILE_ROWS, LANES), F32),
        ],
    )
    def k(pos_hbm, pad_hbm, x_hbm, z_hbm, xs_hbm, i0_v, i1_v, rows_v, ip_v, z_v):
        wid = lax.axis_index("s") * nc + lax.axis_index("c")

        @pl.loop(0, n_ch)
        def _(j):
            b = pl.multiple_of(wid * per_w + j * SC_CHUNK, 8)
            pltpu.sync_copy(pos_hbm.at[pl.ds(b, SC_CHUNK)], i0_v)
            pltpu.sync_copy(pos_hbm.at[pl.ds(t_all + b, SC_CHUNK)], i1_v)
            pltpu.sync_copy(x_hbm.at[pl.ds(b, SC_CHUNK)], rows_v)
            pltpu.sync_copy(rows_v, xs_hbm.at[i0_v])
            pltpu.sync_copy(rows_v, xs_hbm.at[i1_v])

        pltpu.sync_copy(z_hbm, z_v)

        @pl.loop(0, n_pch)
        def _(j):
            b = pl.multiple_of(wid * pad_per_w + j * SC_PAD_CHUNK, 8)
            pltpu.sync_copy(pad_hbm.at[pl.ds(b, SC_PAD_CHUNK)], ip_v)
            pltpu.sync_copy(z_v, xs_hbm.at[ip_v])

    xs = k(pos_flat, pad_rows, x1t.reshape(t_all, TILE_ROWS, LANES), zeros)
    return xs.reshape(n_sorted_rows * TILE_ROWS, LANES)


def _sc_gather_call(pos_flat, outs):
    n_assign = pos_flat.shape[0]
    nc, nw = _sc_workers()
    per_w = n_assign // nw
    n_ch = per_w // SC_CHUNK
    assert n_ch * SC_CHUNK * nw == n_assign

    @functools.partial(
        pl.kernel, mesh=plsc.VectorSubcoreMesh(core_axis_name="c", subcore_axis_name="s"),
        out_type=jax.ShapeDtypeStruct((n_assign, TILE_ROWS, LANES), F32),
        scratch_types=[pltpu.VMEM((SC_CHUNK,), jnp.int32), pltpu.VMEM((SC_CHUNK, TILE_ROWS, LANES), F32)],
    )
    def k(pos_hbm, o_hbm, y_hbm, i_v, rows_v):
        wid = lax.axis_index("s") * nc + lax.axis_index("c")

        @pl.loop(0, n_ch)
        def _(j):
            b = pl.multiple_of(wid * per_w + j * SC_CHUNK, 8)
            pltpu.sync_copy(pos_hbm.at[pl.ds(b, SC_CHUNK)], i_v)
            pltpu.sync_copy(o_hbm.at[i_v], rows_v)
            pltpu.sync_copy(rows_v, y_hbm.at[pl.ds(b, SC_CHUNK)])

    y = k(pos_flat, outs.reshape(-1, TILE_ROWS, LANES))
    return y.reshape(n_assign * TILE_ROWS, LANES)


def _combine_dense_kernel(alpha, n_main_tiles, x1t_ref, y0_ref, y1_ref, w_ref, g_ref, b_ref, out_ref, *tail):
    j = pl.program_id(0)
    w = w_ref[...]
    y = w[:, 0:1] * _load_token_tiles(y0_ref, TOK_TILE) + w[:, 1:2] * _load_token_tiles(y1_ref, TOK_TILE)
    res = _layer_norm(alpha * _load_token_tiles(x1t_ref, TOK_TILE) + y, g_ref[...], b_ref[...])
    if n_main_tiles is None:
        out_ref[...] = res
    else:
        @pl.when(j < n_main_tiles)
        def _():
            out_ref[...] = res

        @pl.when(j >= n_main_tiles)
        def _():
            tail[0][...] = res


def _combine_dense_call(x1t, y_pair, w_col, g, b, alpha, n_main_tiles=None):
    t_all = x1t.shape[0] // TILE_ROWS
    n_tiles = t_all // TOK_TILE
    if n_main_tiles is None:
        out_specs = pl.BlockSpec((TOK_TILE, D_MODEL), lambda i: (i, 0))
        out_shape = jax.ShapeDtypeStruct((t_all, D_MODEL), F32)
    else:
        assert n_tiles == n_main_tiles + 1
        out_specs = [pl.BlockSpec((TOK_TILE, D_MODEL), lambda i: (jnp.minimum(i, n_main_tiles - 1), 0)),
                     pl.BlockSpec((TOK_TILE, D_MODEL), lambda i: (0, 0))]
        out_shape = [jax.ShapeDtypeStruct((n_main_tiles * TOK_TILE, D_MODEL), F32),
                     jax.ShapeDtypeStruct((TOK_TILE, D_MODEL), F32)]
    tile = (TOK_TILE * TILE_ROWS, LANES)
    return pl.pallas_call(
        functools.partial(_combine_dense_kernel, alpha, n_main_tiles),
        grid=(n_tiles,),
        in_specs=[
            pl.BlockSpec(tile, lambda i: (i, 0)),
            pl.BlockSpec(tile, lambda i: (i, 0)),
            pl.BlockSpec(tile, lambda i: (i + n_tiles, 0)),
            pl.BlockSpec((TOK_TILE, TOP_K), lambda i: (i, 0)),
            pl.BlockSpec((1, D_MODEL), lambda i: (0, 0)),
            pl.BlockSpec((1, D_MODEL), lambda i: (0, 0)),
        ],
        out_specs=out_specs,
        out_shape=out_shape,
        compiler_params=pltpu.CompilerParams(dimension_semantics=("arbitrary",), vmem_limit_bytes=VMEM_LIMIT),
        name="combine",
    )(x1t, y_pair, y_pair, w_col, g, b)


def _rope_tables(seq, dec_seq, past_len):
    half = HEAD_DIM // 2
    inv = ROPE_THETA ** (-jnp.arange(half, dtype=F32) / half)
    pos = jnp.concatenate([jnp.arange(seq), past_len + (jnp.arange(TOK_TILE) % dec_seq)])
    ang = pos.astype(F32)[:, None] * inv[None, :]
    cos = jnp.cos(ang)
    sin = jnp.sin(ang)
    zero = jnp.zeros_like(sin)
    reps = LANES // HEAD_DIM
    cos_t = jnp.tile(jnp.concatenate([cos, cos], -1), (1, reps))
    sa_t = jnp.tile(jnp.concatenate([-sin, zero], -1), (1, reps))
    sb_t = jnp.tile(jnp.concatenate([zero, sin], -1), (1, reps))
    return cos_t, sa_t, sb_t


def _inclusive_cumsum(v):
    n = v.shape[0]
    tri = jnp.arange(n)[None, :] <= jnp.arange(n)[:, None]
    return jnp.sum(jnp.where(tri, v[None, :], 0), axis=1)


def _dispatch_plan(ridx, counts, n_tiles):
    t_all = ridx.shape[1]
    tiles_e = (counts + EXPERT_TILE - 1) // EXPERT_TILE
    tile_end = _inclusive_cumsum(tiles_e)
    offs = (tile_end - tiles_e) * EXPERT_TILE
    n_used = tile_end[-1:]
    tile_expert = jnp.minimum(
        jnp.sum((jnp.arange(n_tiles)[:, None] >= tile_end[None, :]).astype(jnp.int32), axis=1), N_EXPERTS - 1)
    experts = jnp.arange(N_EXPERTS)
    pos = ridx[TOP_K:2 * TOP_K] + jnp.sum(
        jnp.where(ridx[:TOP_K, :, None] == experts[None, None, :], offs[None, None, :], 0), axis=-1)
    gap_start = jnp.concatenate([offs + counts, n_used * EXPERT_TILE])
    gap_len = jnp.concatenate([tiles_e * EXPERT_TILE - counts, (n_tiles - n_used) * EXPERT_TILE])
    gap_end_q = _inclusive_cumsum(gap_len)
    q = jnp.arange(n_tiles * EXPERT_TILE - TOP_K * t_all)
    gap = jnp.sum((q[:, None] >= gap_end_q[None, :]).astype(jnp.int32), axis=1)
    in_gap = gap[:, None] == jnp.arange(N_EXPERTS + 1)[None, :]
    pad_rows = q + jnp.sum(jnp.where(in_gap, (gap_start - gap_end_q + gap_len)[None, :], 0), axis=1)
    return (pos.reshape(-1).astype(jnp.int32), pad_rows.astype(jnp.int32),
            tile_expert.astype(jnp.int32), n_used.astype(jnp.int32))


def kernel(x_prompt, x_sample, cache_k, cache_v, w_in, sinks, gm_ln_g, gm_ln_b, gm_ws, gm_bs,
           out_norm_a, out_norm_b, w_o, ln1_g, ln1_b, w_router, router_bias,
           w_gate, w_up, w_down, ln2_g, ln2_b):
    batch, seq, _ = x_prompt.shape
    dec_batch, dec_seq, _ = x_sample.shape
    depth = w_in.shape[0]
    past_len = PAST_LEN
    assert dec_batch * dec_seq == TOK_TILE and dec_seq == CHUNK and seq % TOK_TILE == 0
    assert cache_k.shape[2] == WINDOW
    alpha = (2 * depth) ** 0.25
    n_prompt = batch * seq
    t_all = n_prompt + dec_batch * dec_seq
    n_prompt_tiles = n_prompt // TOK_TILE
    tiles_per_seq = seq // TOK_TILE
    n_exp_tiles = (TOP_K * t_all) // EXPERT_TILE + N_EXPERTS

    tables = _rope_tables(seq, dec_seq, past_len)
    x = jnp.concatenate([x_prompt.reshape(n_prompt, D_MODEL), x_sample.reshape(-1, D_MODEL)], axis=0)
    wrt = w_router.T
    rb = router_bias.reshape(N_EXPERTS, 1)

    kp, vp, ks, vs, gms = [], [], [], [], []
    for l in range(depth):
        layer_args = (
            sinks[l], w_in[l].astype(BF16),
            cache_k[l].reshape(dec_batch, WINDOW, KV_WIDTH), cache_v[l].reshape(dec_batch, WINDOW, KV_WIDTH),
            gm_ln_g[l].reshape(1, GM_WIDTH), gm_ln_b[l].reshape(1, GM_WIDTH),
            gm_ws[l], gm_bs[l].T,
            out_norm_a[l].reshape(1, ATT_WIDTH), out_norm_b[l].reshape(1, GM_WIDTH),
            w_o[l].astype(BF16),
            ln1_g[l].reshape(1, D_MODEL), ln1_b[l].reshape(1, D_MODEL), wrt, rb)
        x1t, k_all, v_all, vn_s, ridx, rw, counts = _mixer_call(
            layer_args, x, tables, n_prompt_tiles, tiles_per_seq, alpha)

        pos, pad_rows, tile_expert, n_used = _dispatch_plan(ridx, counts[:, 0], n_exp_tiles)
        xs = _sc_dispatch_call(pos, pad_rows, x1t, n_exp_tiles * EXPERT_TILE)
        outs = _expert_call(tile_expert, n_used, xs, w_gate, w_up, w_down, l)
        y_pair = _sc_gather_call(pos, outs)
        x = _combine_dense_call(x1t, y_pair, rw[:TOP_K].T, ln2_g[l].reshape(1, D_MODEL),
                                ln2_b[l].reshape(1, D_MODEL), alpha,
                                n_main_tiles=n_prompt_tiles if l == depth - 1 else None)

        k_tail = k_all[:n_prompt].reshape(batch, seq, KV_WIDTH)[:, -WINDOW:]
        v_tail = v_all[:n_prompt].reshape(batch, seq, KV_WIDTH)[:, -WINDOW:]
        kp.append(k_tail.reshape(batch, WINDOW, N_KV_HEADS, HEAD_DIM))
        vp.append(v_tail.reshape(batch, WINDOW, N_KV_HEADS, HEAD_DIM))
        ks.append(k_all[n_prompt:].reshape(dec_batch, dec_seq, N_KV_HEADS, HEAD_DIM))
        vs.append(v_all[n_prompt:].reshape(dec_batch, dec_seq, N_KV_HEADS, HEAD_DIM))
        gms.append(vn_s.reshape(dec_batch, dec_seq, GM_WIDTH))

    y_prompt = x[0].reshape(batch, seq, D_MODEL)
    y_sample = x[1].reshape(dec_batch, dec_seq, D_MODEL)
    return (y_prompt, y_sample, jnp.stack(kp), jnp.stack(vp), jnp.stack(ks), jnp.stack(vs), jnp.stack(gms))
```

```python
import functools

import jax
import jax.numpy as jnp
from jax import lax
from jax.experimental import pallas as pl
from jax.experimental.pallas import tpu as pltpu
from jax.experimental.pallas import tpu_sc as plsc

D_MODEL = 1024
CHUNK = 64
WINDOW = 128
HEAD_DIM = 64
N_Q_HEADS = 8
N_KV_HEADS = 2
Q_PER_KV = N_Q_HEADS // N_KV_HEADS
PAST_LEN = 2048
ATT_WIDTH = N_Q_HEADS * HEAD_DIM
KV_WIDTH = N_KV_HEADS * HEAD_DIM
ROPE_THETA = 10000.0
GM_GROUPS = 4
GM_CH = 128
GM_WIDTH = GM_GROUPS * GM_CH
GM_CHUNK = 128
D_MIX = ATT_WIDTH + GM_WIDTH
D_IN = ATT_WIDTH + 2 * KV_WIDTH + 2 * GM_WIDTH
N_EXPERTS = 16
N_EXPERT_GROUPS = 4
EXPERTS_PER_GROUP = N_EXPERTS // N_EXPERT_GROUPS
TOP_K = 2
D_EXPERT = 512
LN_EPS = 1e-5
NEG_INF = -1e30

LANES = 128
TILE_ROWS = D_MODEL // LANES
TOK_TILE = 512
EXPERT_TILE = 512
KEYS_PER_CHUNK = WINDOW + CHUNK
VMEM_LIMIT = 56 * 1024 * 1024

F32 = jnp.float32
BF16 = jnp.bfloat16


def _layer_norm(x, g, b):
    mu = jnp.mean(x, axis=-1, keepdims=True)
    d = x - mu
    var = jnp.mean(d * d, axis=-1, keepdims=True)
    return d * lax.rsqrt(var + LN_EPS) * g + b


def _rms_norm(x, g):
    return x * lax.rsqrt(jnp.mean(x * x, axis=-1, keepdims=True) + LN_EPS) * g


def _gelu(x):
    return 0.5 * x * (1.0 + lax.erf(x * (0.5 ** 0.5)))


def _store_token_tiles(ref, x):
    n = x.shape[0]
    for c in range(TILE_ROWS):
        ref[pl.ds(c, n, stride=TILE_ROWS), :] = x[:, c * LANES:(c + 1) * LANES]


def _load_token_tiles(ref, n, row0=0):
    return jnp.concatenate([ref[pl.ds(row0 + c, n, stride=TILE_ROWS), :] for c in range(TILE_ROWS)], axis=-1)


def _first_index_of_max(rows):
    m = rows[0]
    for r in rows[1:]:
        m = jnp.maximum(m, r)
    idx = jnp.full(m.shape, len(rows), jnp.int32)
    for e in reversed(range(len(rows))):
        idx = jnp.where(rows[e] == m, e, idx)
    return m, idx


def _mixer_kernel(n_prompt_tiles, tiles_per_seq, alpha, has_sample,
                  sinks_ref, x_ref, w_in_ref, cos_ref, sa_ref, sb_ref, ck_ref, cv_ref,
                  lng_ref, lnb_ref, ws_ref, bst_ref, na_ref, nb_ref, w_o_ref,
                  l1g_ref, l1b_ref, wrt_ref, rb_ref, x1t_ref, k_ref, v_ref, *rest):
    if has_sample:
        vn_ref, ridx_ref, rw_ref, counts_ref, kd_ref, vd_ref, qm_ref, att_ref, gm_ref, cnt_ref = rest
    else:
        ridx_ref, rw_ref, counts_ref, kd_ref, vd_ref, qm_ref, att_ref, gm_ref, cnt_ref = rest
    i = pl.program_id(0)
    is_sample = i >= n_prompt_tiles
    seq_start = jnp.logical_and(jnp.logical_not(is_sample), (i % tiles_per_seq) == 0)
    tm = TOK_TILE

    x = x_ref[...]
    h = jnp.dot(x.astype(BF16), w_in_ref[...], preferred_element_type=F32)

    cos = cos_ref[...]
    sa = sa_ref[...]
    sb = sb_ref[...]

    def rope(blk):
        return blk * cos + pltpu.roll(blk, LANES - HEAD_DIM // 2, 1) * sa + pltpu.roll(blk, HEAD_DIM // 2, 1) * sb

    lane = lax.broadcasted_iota(jnp.int32, (1, LANES), 1)
    lo_half = lane < HEAD_DIM

    o_k = ATT_WIDTH
    o_v = o_k + KV_WIDTH
    o_u = o_v + KV_WIDTH
    o_g = o_u + GM_WIDTH

    k_rot = rope(h[:, o_k:o_v])
    v_new = h[:, o_v:o_u]
    k_ref[...] = k_rot
    v_ref[...] = v_new

    def dup_heads(a):
        sw = pltpu.roll(a, HEAD_DIM, 1)
        return jnp.where(lo_half, a, sw).astype(BF16), jnp.where(lo_half, sw, a).astype(BF16)

    k_d = dup_heads(k_rot)
    v_d = dup_heads(v_new)

    @pl.when(jnp.logical_not(is_sample))
    def _():
        @pl.when(seq_start)
        def _():
            for g in range(N_KV_HEADS):
                kd_ref[g, 0:WINDOW, :] = jnp.zeros((WINDOW, LANES), BF16)
                vd_ref[g, 0:WINDOW, :] = jnp.zeros((WINDOW, LANES), BF16)

        @pl.when(jnp.logical_not(seq_start))
        def _():
            for g in range(N_KV_HEADS):
                kd_ref[g, 0:WINDOW, :] = kd_ref[g, tm:tm + WINDOW, :]
                vd_ref[g, 0:WINDOW, :] = vd_ref[g, tm:tm + WINDOW, :]

        for g in range(N_KV_HEADS):
            kd_ref[g, WINDOW:WINDOW + tm, :] = k_d[g]
            vd_ref[g, WINDOW:WINDOW + tm, :] = v_d[g]

    @pl.when(is_sample)
    def _():
        for b in range(tm // CHUNK):
            ck = dup_heads(ck_ref[b])
            cv = dup_heads(cv_ref[b])
            base = b * KEYS_PER_CHUNK
            for g in range(N_KV_HEADS):
                kd_ref[g, base:base + WINDOW, :] = ck[g]
                vd_ref[g, base:base + WINDOW, :] = cv[g]
                kd_ref[g, base + WINDOW:base + KEYS_PER_CHUNK, :] = k_d[g][b * CHUNK:(b + 1) * CHUNK]
                vd_ref[g, base + WINDOW:base + KEYS_PER_CHUNK, :] = v_d[g][b * CHUNK:(b + 1) * CHUNK]

    scale = HEAD_DIM ** -0.5
    for b in range(ATT_WIDTH // LANES):
        qb = rope(h[:, b * LANES:(b + 1) * LANES]) * scale
        qm_ref[2 * b] = jnp.where(lo_half, qb, 0.0).astype(BF16)
        qm_ref[2 * b + 1] = jnp.where(lo_half, 0.0, qb).astype(BF16)

    key_stride = jnp.where(is_sample, KEYS_PER_CHUNK, CHUNK)
    col = lax.broadcasted_iota(jnp.int32, (1, KEYS_PER_CHUNK), 1)
    row_blk = lax.broadcasted_iota(jnp.int32, (Q_PER_KV * CHUNK, 1), 0) // CHUNK
    sink_cols = [
        jnp.where(row_blk == 0, sinks_ref[Q_PER_KV * g],
                  jnp.where(row_blk == 1, sinks_ref[Q_PER_KV * g + 1],
                            jnp.where(row_blk == 2, sinks_ref[Q_PER_KV * g + 2], sinks_ref[Q_PER_KV * g + 3])))
        for g in range(N_KV_HEADS)]

    def chunk_body(c, carry):
        r0 = pl.multiple_of(c * CHUNK, CHUNK)
        k0 = pl.multiple_of(c * key_stride, CHUNK)
        first_valid = jnp.where(seq_start, (WINDOW // CHUNK - c) * CHUNK, 0)
        valid = col >= first_valid
        outs = []
        for g in range(N_KV_HEADS):
            q_st = jnp.concatenate([qm_ref[Q_PER_KV * g + r, pl.ds(r0, CHUNK), :] for r in range(Q_PER_KV)],
                                   axis=0)
            keys = kd_ref[g, pl.ds(k0, KEYS_PER_CHUNK), :]
            vals = vd_ref[g, pl.ds(k0, KEYS_PER_CHUNK), :]
            s = lax.dot_general(q_st, keys, (((1,), (1,)), ((), ())), preferred_element_type=F32)
            s = jnp.where(valid, s, NEG_INF)
            sink = sink_cols[g]
            m = jnp.maximum(jnp.max(s, axis=-1, keepdims=True), sink)
            p = jnp.exp(s - m)
            denom = jnp.sum(p, axis=-1, keepdims=True) + jnp.exp(sink - m)
            o = jnp.dot(p.astype(BF16), vals, preferred_element_type=F32) * (1.0 / denom)
            for bb in range(2):
                outs.append(jnp.where(lo_half, o[(2 * bb) * CHUNK:(2 * bb + 1) * CHUNK],
                                      o[(2 * bb + 1) * CHUNK:(2 * bb + 2) * CHUNK]))
        for b in range(ATT_WIDTH // LANES):
            att_ref[pl.ds(r0, CHUNK), b * LANES:(b + 1) * LANES] = outs[b]
        return carry

    lax.fori_loop(0, tm // CHUNK, chunk_body, 0, unroll=8)

    prow = lax.broadcasted_iota(jnp.int32, (GM_CHUNK, GM_CHUNK), 0)
    pcol = lax.broadcasted_iota(jnp.int32, (GM_CHUNK, GM_CHUNK), 1)
    half = GM_CHUNK // 2
    tril = pcol <= prow
    same_blk = (prow < half) == (pcol < half)
    prow1 = lax.broadcasted_iota(jnp.int32, (GM_CHUNK, 1), 0)
    bst = bst_ref[...]
    bst_s = jnp.where(prow1 < half, bst, pltpu.roll(bst, half, 0))
    bias = jnp.where(is_sample, bst_s, bst)
    for g in range(GM_GROUPS):
        wg_ = ws_ref[g]
        w_s = jnp.where(prow < half, wg_, pltpu.roll(pltpu.roll(wg_, half, 0), half, 1))
        w_eff = jnp.where(is_sample, jnp.where(same_blk, w_s, 0.0), wg_)
        w_eff = jnp.where(tril, w_eff, 0.0).astype(BF16)
        cols = slice(g * GM_CH, (g + 1) * GM_CH)
        ug = _gelu(h[:, o_u + g * GM_CH:o_u + (g + 1) * GM_CH])
        vn = _layer_norm(_gelu(h[:, o_g + g * GM_CH:o_g + (g + 1) * GM_CH]), lng_ref[:, cols], lnb_ref[:, cols])

        if has_sample:
            @pl.when(is_sample)
            def _(cols=cols, vn=vn):
                vn_ref[:, cols] = vn

        vn_b = vn.astype(BF16)
        for n in range(tm // GM_CHUNK):
            rows = slice(n * GM_CHUNK, (n + 1) * GM_CHUNK)
            s = jnp.dot(w_eff, vn_b[rows], preferred_element_type=F32) + bias[:, g:g + 1]
            gm_ref[rows, cols] = ug[rows] * s

    mixed = jnp.concatenate([_rms_norm(att_ref[...], na_ref[...]), _rms_norm(gm_ref[...], nb_ref[...])], axis=-1)
    y = alpha * x + jnp.dot(mixed.astype(BF16), w_o_ref[...], preferred_element_type=F32)
    x1 = _layer_norm(y, l1g_ref[...], l1b_ref[...])
    _store_token_tiles(x1t_ref, x1)

    logits = lax.dot_general(wrt_ref[...], x1, (((1,), (1,)), ((), ())),
                             preferred_element_type=F32, precision=lax.Precision.HIGHEST)
    scores = jax.nn.sigmoid(logits)
    sel = scores + rb_ref[...]
    sel_rows = [sel[e:e + 1, :] for e in range(N_EXPERTS)]
    sc_rows = [scores[e:e + 1, :] for e in range(N_EXPERTS)]
    grp = []
    for g in range(N_EXPERT_GROUPS):
        r = sel_rows[g * EXPERTS_PER_GROUP:(g + 1) * EXPERTS_PER_GROUP]
        best_pair = None
        for a in range(EXPERTS_PER_GROUP):
            for b in range(a + 1, EXPERTS_PER_GROUP):
                pair = r[a] + r[b]
                best_pair = pair if best_pair is None else jnp.maximum(best_pair, pair)
        grp.append(best_pair)
    _, best = _first_index_of_max(grp)
    masked = [jnp.where(best == (e // EXPERTS_PER_GROUP), sel_rows[e], NEG_INF) for e in range(N_EXPERTS)]
    _, e0 = _first_index_of_max(masked)
    masked2 = [jnp.where(e0 == e, -jnp.inf, masked[e]) for e in range(N_EXPERTS)]
    _, e1 = _first_index_of_max(masked2)
    w0 = jnp.zeros_like(sc_rows[0])
    w1 = jnp.zeros_like(sc_rows[0])
    for e in range(N_EXPERTS):
        w0 = jnp.where(e0 == e, sc_rows[e], w0)
        w1 = jnp.where(e1 == e, sc_rows[e], w1)
    wsum = w0 + w1
    rw_ref[...] = jnp.concatenate([w0 / wsum, w1 / wsum, jnp.zeros((6, tm), F32)], axis=0)

    @pl.when(i == 0)
    def _():
        cnt_ref[...] = jnp.zeros_like(cnt_ref)

    eid = lax.broadcasted_iota(jnp.int32, (N_EXPERTS, tm), 0)
    oh0 = (eid == e0).astype(F32)
    oh1 = (eid == e1).astype(F32)
    oh = oh0 + oh1
    upper = (lax.broadcasted_iota(jnp.int32, (tm, tm), 0) <= lax.broadcasted_iota(jnp.int32, (tm, tm), 1))
    incl = jnp.dot(oh.astype(BF16), upper.astype(BF16), preferred_element_type=F32)
    before = cnt_ref[:, 0:1] + incl - oh
    r0 = jnp.sum(oh0 * before, axis=0, keepdims=True).astype(jnp.int32)
    r1 = jnp.sum(oh1 * before, axis=0, keepdims=True).astype(jnp.int32)
    cnt = cnt_ref[...] + incl[:, tm - 1:tm]
    cnt_ref[...] = cnt
    counts_ref[...] = cnt.astype(jnp.int32)
    ridx_ref[...] = jnp.concatenate([e0, e1, r0, r1, jnp.zeros((4, tm), jnp.int32)], axis=0)


def _mixer_call(layer_args, x_all, tables, n_prompt_tiles, tiles_per_seq, alpha):
    (sinks, w_in, ck, cv, lng, lnb, ws, bst, na, nb, w_o, l1g, l1b, wrt, rb) = layer_args
    cos_t, sa_t, sb_t = tables
    t_all = x_all.shape[0]
    n_tiles = t_all // TOK_TILE
    n_seq_tiles = tiles_per_seq
    has_sample = n_tiles > n_prompt_tiles

    def const(shape):
        nd = len(shape)
        return pl.BlockSpec(shape, lambda i, _nd=nd: (0,) * _nd)

    def tab_map(i):
        return (jnp.where(i < n_prompt_tiles, i % n_seq_tiles, n_seq_tiles), 0)

    row_blk = lambda w: pl.BlockSpec((TOK_TILE, w), lambda i: (i, 0))
    in_specs = [
        pl.BlockSpec(memory_space=pltpu.SMEM),
        row_blk(D_MODEL),
        const((D_MODEL, D_IN)),
        pl.BlockSpec((TOK_TILE, LANES), tab_map),
        pl.BlockSpec((TOK_TILE, LANES), tab_map),
        pl.BlockSpec((TOK_TILE, LANES), tab_map),
        const(ck.shape), const(cv.shape),
        const((1, GM_WIDTH)), const((1, GM_WIDTH)),
        const((GM_GROUPS, GM_CHUNK, GM_CHUNK)), const((GM_CHUNK, GM_GROUPS)),
        const((1, ATT_WIDTH)), const((1, GM_WIDTH)),
        const((D_MIX, D_MODEL)),
        const((1, D_MODEL)), const((1, D_MODEL)),
        const((N_EXPERTS, D_MODEL)), const((N_EXPERTS, 1)),
    ]
    out_shape = [
        jax.ShapeDtypeStruct((t_all * TILE_ROWS, LANES), F32),
        jax.ShapeDtypeStruct((t_all, KV_WIDTH), F32),
        jax.ShapeDtypeStruct((t_all, KV_WIDTH), F32),
        jax.ShapeDtypeStruct((TOK_TILE, GM_WIDTH), F32),
        jax.ShapeDtypeStruct((8, t_all), jnp.int32),
        jax.ShapeDtypeStruct((8, t_all), F32),
        jax.ShapeDtypeStruct((N_EXPERTS, LANES), jnp.int32),
    ]
    out_specs = [
        pl.BlockSpec((TOK_TILE * TILE_ROWS, LANES), lambda i: (i, 0)),
        row_blk(KV_WIDTH), row_blk(KV_WIDTH),
        const((TOK_TILE, GM_WIDTH)),
        pl.BlockSpec((8, TOK_TILE), lambda i: (0, i)),
        pl.BlockSpec((8, TOK_TILE), lambda i: (0, i)),
        const((N_EXPERTS, LANES)),
    ]
    if not has_sample:
        del out_shape[3], out_specs[3]
    kd_rows = (TOK_TILE // CHUNK) * KEYS_PER_CHUNK
    scratch = [
        pltpu.VMEM((N_KV_HEADS, kd_rows, LANES), BF16),
        pltpu.VMEM((N_KV_HEADS, kd_rows, LANES), BF16),
        pltpu.VMEM((N_Q_HEADS, TOK_TILE, LANES), BF16),
        pltpu.VMEM((TOK_TILE, ATT_WIDTH), F32),
        pltpu.VMEM((TOK_TILE, GM_WIDTH), F32),
        pltpu.VMEM((N_EXPERTS, LANES), F32),
    ]
    return pl.pallas_call(
        functools.partial(_mixer_kernel, n_prompt_tiles, tiles_per_seq, alpha, has_sample),
        grid=(n_tiles,),
        in_specs=in_specs, out_specs=out_specs, out_shape=out_shape,
        scratch_shapes=scratch,
        compiler_params=pltpu.CompilerParams(dimension_semantics=("arbitrary",), vmem_limit_bytes=VMEM_LIMIT),
        name="mixer",
    )(sinks, x_all, w_in, cos_t, sa_t, sb_t, ck, cv, lng, lnb, ws, bst, na, nb, w_o, l1g, l1b, wrt, rb)


def _expert_kernel(te_ref, nt_ref, xs_ref, wg_ref, wu_ref, wd_ref, out_ref, wg_s, wu_s, wd_s):
    i = pl.program_id(0)
    changed = jnp.logical_or(i == 0, te_ref[i] != te_ref[jnp.maximum(i - 1, 0)])

    @pl.when(changed)
    def _():
        wg_s[...] = wg_ref[0, 0].astype(BF16)
        wu_s[...] = wu_ref[0, 0].astype(BF16)
        wd_s[...] = wd_ref[0, 0].astype(BF16)

    @pl.when(i < nt_ref[0])
    def _():
        xs = _load_token_tiles(xs_ref, EXPERT_TILE).astype(BF16)
        g = jnp.dot(xs, wg_s[...], preferred_element_type=F32)
        u = jnp.dot(xs, wu_s[...], preferred_element_type=F32)
        hmid = (g * jax.nn.sigmoid(g)) * u
        _store_token_tiles(out_ref, jnp.dot(hmid.astype(BF16), wd_s[...], preferred_element_type=F32))

    @pl.when(i >= nt_ref[0])
    def _():
        out_ref[...] = jnp.zeros_like(out_ref)


def _expert_call(tile_expert, n_used, xs, wg, wu, wd, layer):
    n_tiles = xs.shape[0] // (EXPERT_TILE * TILE_ROWS)
    tile_blk = pl.BlockSpec((EXPERT_TILE * TILE_ROWS, LANES), lambda i, te, nt: (i, 0))
    grid_spec = pltpu.PrefetchScalarGridSpec(
        num_scalar_prefetch=2,
        grid=(n_tiles,),
        in_specs=[
            tile_blk,
            pl.BlockSpec((1, 1, D_MODEL, D_EXPERT), lambda i, te, nt: (layer, te[i], 0, 0)),
            pl.BlockSpec((1, 1, D_MODEL, D_EXPERT), lambda i, te, nt: (layer, te[i], 0, 0)),
            pl.BlockSpec((1, 1, D_EXPERT, D_MODEL), lambda i, te, nt: (layer, te[i], 0, 0)),
        ],
        out_specs=tile_blk,
        scratch_shapes=[
            pltpu.VMEM((D_MODEL, D_EXPERT), BF16),
            pltpu.VMEM((D_MODEL, D_EXPERT), BF16),
            pltpu.VMEM((D_EXPERT, D_MODEL), BF16),
        ],
    )
    return pl.pallas_call(
        _expert_kernel,
        grid_spec=grid_spec,
        out_shape=jax.ShapeDtypeStruct(xs.shape, F32),
        compiler_params=pltpu.CompilerParams(dimension_semantics=("arbitrary",), vmem_limit_bytes=VMEM_LIMIT),
        name="experts",
    )(tile_expert, n_used, xs, wg, wu, wd)


ROWS_PER_STEP = TOK_TILE * TOP_K


def _token_tile(ref, idx):
    return ref.at[pl.ds(pl.multiple_of(idx * TILE_ROWS, TILE_ROWS), TILE_ROWS)]


def _dispatch_kernel(n_tok_steps, t_all, pos_ref, pad_ref, x1t_ref, xs_hbm, zero_ref, sems):
    j = pl.program_id(0)

    @pl.when(j == 0)
    def _():
        zero_ref[...] = jnp.zeros_like(zero_ref)

    @pl.when(j < n_tok_steps)
    def _():
        def body(tt, c):
            t = j * TOK_TILE + tt
            for k in range(TOP_K):
                pltpu.make_async_copy(_token_tile(x1t_ref, tt), _token_tile(xs_hbm, pos_ref[k * t_all + t]),
                                      sems.at[k]).start()
            return c
        lax.fori_loop(0, TOK_TILE, body, 0, unroll=8)

    @pl.when(j >= n_tok_steps)
    def _():
        def body(q, c):
            for k in range(TOP_K):
                row = pad_ref[(j - n_tok_steps) * ROWS_PER_STEP + k * TOK_TILE + q]
                pltpu.make_async_copy(zero_ref, _token_tile(xs_hbm, row), sems.at[k]).start()
            return c
        lax.fori_loop(0, TOK_TILE, body, 0, unroll=8)

    for k in range(TOP_K):
        pltpu.make_async_copy(x1t_ref, xs_hbm.at[pl.ds(0, TOK_TILE * TILE_ROWS)], sems.at[k]).wait()


def _dispatch_call(pos_flat, pad_rows, x1t, n_sorted_rows):
    t_all = x1t.shape[0] // TILE_ROWS
    n_tok_steps = t_all // TOK_TILE
    n_steps = n_tok_steps + pad_rows.shape[0] // ROWS_PER_STEP
    grid_spec = pltpu.PrefetchScalarGridSpec(
        num_scalar_prefetch=2,
        grid=(n_steps,),
        in_specs=[pl.BlockSpec((TOK_TILE * TILE_ROWS, LANES),
                               lambda j, p, q: (jnp.minimum(j, n_tok_steps - 1), 0))],
        out_specs=pl.BlockSpec(memory_space=pl.ANY),
        scratch_shapes=[pltpu.VMEM((TILE_ROWS, LANES), F32), pltpu.SemaphoreType.DMA((TOP_K,))],
    )
    return pl.pallas_call(
        functools.partial(_dispatch_kernel, n_tok_steps, t_all),
        grid_spec=grid_spec,
        out_shape=jax.ShapeDtypeStruct((n_sorted_rows * TILE_ROWS, LANES), F32),
        compiler_params=pltpu.CompilerParams(dimension_semantics=("arbitrary",)),
        name="dispatch",
    )(pos_flat, pad_rows, x1t)


def _combine_kernel(alpha, t_all, n_main_tiles, pos_ref, x1t_ref, w_ref, g_ref, b_ref, outs_hbm, *rest):
    if n_main_tiles is None:
        out_ref, gbuf, sems = rest
        tail_ref = None
    else:
        out_ref, tail_ref, gbuf, sems = rest
    j = pl.program_id(0)
    slot = j % 2
    k_rows = TOK_TILE * TILE_ROWS

    n = pl.num_programs(0)

    def wait(s):
        pltpu.make_async_copy(outs_hbm.at[pl.ds(0, TOP_K * k_rows)], gbuf.at[s], sems.at[s]).wait()

    def issue(step, s):
        def body(tt, c):
            t = step * TOK_TILE + tt
            for k in range(TOP_K):
                pltpu.make_async_copy(_token_tile(outs_hbm, pos_ref[k * t_all + t]),
                                      _token_tile(gbuf.at[s], k * TOK_TILE + tt), sems.at[s]).start()
            return c
        lax.fori_loop(0, TOK_TILE, body, 0, unroll=8)

    @pl.when(j == 0)
    def _():
        issue(0, 0)

    @pl.when(j + 1 < n)
    def _():
        issue(j + 1, 1 - slot)

    wait(slot)

    w = w_ref[...]
    y = (w[:, 0:1] * _load_token_tiles(gbuf.at[slot], TOK_TILE)
         + w[:, 1:2] * _load_token_tiles(gbuf.at[slot], TOK_TILE, k_rows))
    x1 = _load_token_tiles(x1t_ref, TOK_TILE)
    res = _layer_norm(alpha * x1 + y, g_ref[...], b_ref[...])
    if tail_ref is None:
        out_ref[...] = res
    else:
        @pl.when(j < n_main_tiles)
        def _():
            out_ref[...] = res

        @pl.when(j >= n_main_tiles)
        def _():
            tail_ref[...] = res


def _combine_call(pos_flat, x1t, w_col, g, b, outs, alpha, n_main_tiles=None):
    t_all = x1t.shape[0] // TILE_ROWS
    n_tiles = t_all // TOK_TILE
    if n_main_tiles is None:
        out_specs = pl.BlockSpec((TOK_TILE, D_MODEL), lambda i, p: (i, 0))
        out_shape = jax.ShapeDtypeStruct((t_all, D_MODEL), F32)
    else:
        assert n_tiles == n_main_tiles + 1
        out_specs = [pl.BlockSpec((TOK_TILE, D_MODEL), lambda i, p: (jnp.minimum(i, n_main_tiles - 1), 0)),
                     pl.BlockSpec((TOK_TILE, D_MODEL), lambda i, p: (0, 0))]
        out_shape = [jax.ShapeDtypeStruct((n_main_tiles * TOK_TILE, D_MODEL), F32),
                     jax.ShapeDtypeStruct((TOK_TILE, D_MODEL), F32)]
    grid_spec = pltpu.PrefetchScalarGridSpec(
        num_scalar_prefetch=1,
        grid=(t_all // TOK_TILE,),
        in_specs=[
            pl.BlockSpec((TOK_TILE * TILE_ROWS, LANES), lambda i, p: (i, 0)),
            pl.BlockSpec((TOK_TILE, TOP_K), lambda i, p: (i, 0)),
            pl.BlockSpec((1, D_MODEL), lambda i, p: (0, 0)),
            pl.BlockSpec((1, D_MODEL), lambda i, p: (0, 0)),
            pl.BlockSpec(memory_space=pl.ANY),
        ],
        out_specs=out_specs,
        scratch_shapes=[pltpu.VMEM((2, TOP_K * TOK_TILE * TILE_ROWS, LANES), F32), pltpu.SemaphoreType.DMA((2,))],
    )
    return pl.pallas_call(
        functools.partial(_combine_kernel, alpha, t_all, n_main_tiles),
        grid_spec=grid_spec,
        out_shape=out_shape,
        compiler_params=pltpu.CompilerParams(dimension_semantics=("arbitrary",), vmem_limit_bytes=VMEM_LIMIT),
        name="combine",
    )(pos_flat, x1t, w_col, g, b, outs)


SC_CHUNK = 48
SC_PAD_CHUNK = 32


def _sc_workers():
    info = plsc.get_sparse_core_info()
    return info.num_cores, info.num_cores * info.num_subcores


def _sc_split(per_w):
    n_full = per_w // SC_CHUNK
    tail = per_w - n_full * SC_CHUNK
    assert tail % 8 == 0
    return n_full, tail


def _sc_dispatch_call(pos_flat, pad_rows, x1t, n_sorted_rows):
    t_all = x1t.shape[0] // TILE_ROWS
    nc, nw = _sc_workers()
    per_w = t_all // nw
    n_ch, tail = _sc_split(per_w)
    n_pad = pad_rows.shape[0]
    pad_per_w = n_pad // nw
    n_pch = pad_per_w // SC_PAD_CHUNK
    assert per_w * nw == t_all and n_pch * SC_PAD_CHUNK * nw == n_pad
    zeros = jnp.zeros((SC_PAD_CHUNK, TILE_ROWS, LANES), F32)
    tail_scratch = [] if tail == 0 else [
        pltpu.VMEM((tail,), jnp.int32), pltpu.VMEM((tail,), jnp.int32), pltpu.VMEM((tail, TILE_ROWS, LANES), F32)]

    @functools.partial(
        pl.kernel, mesh=plsc.VectorSubcoreMesh(core_axis_name="c", subcore_axis_name="s"),
        out_type=jax.ShapeDtypeStruct((n_sorted_rows, TILE_ROWS, LANES), F32),
        scratch_types=[
            pltpu.VMEM((SC_CHUNK,), jnp.int32), pltpu.VMEM((SC_CHUNK,), jnp.int32),
            pltpu.VMEM((SC_CHUNK, TILE_ROWS, LANES), F32),
            pltpu.VMEM((SC_PAD_CHUNK,), jnp.int32),
            pltpu.VMEM((SC_PAD_CHUNK, TILE_ROWS, LANES), F32),
        ] + tail_scratch,
    )
    def k(pos_hbm, pad_hbm, x_hbm, z_hbm, xs_hbm, i0_v, i1_v, rows_v, ip_v, z_v, *tail_refs):
        wid = lax.axis_index("s") * nc + lax.axis_index("c")

        def move(b, n, idx0, idx1, rows):
            pltpu.sync_copy(pos_hbm.at[pl.ds(b, n)], idx0)
            pltpu.sync_copy(pos_hbm.at[pl.ds(t_all + b, n)], idx1)
            pltpu.sync_copy(x_hbm.at[pl.ds(b, n)], rows)
            pltpu.sync_copy(rows, xs_hbm.at[idx0])
            pltpu.sync_copy(rows, xs_hbm.at[idx1])

        @pl.loop(0, n_ch)
        def _(j):
            move(pl.multiple_of(wid * per_w + j * SC_CHUNK, 8), SC_CHUNK, i0_v, i1_v, rows_v)

        if tail:
            move(pl.multiple_of(wid * per_w + n_ch * SC_CHUNK, 8), tail, *tail_refs)

        pltpu.sync_copy(z_hbm, z_v)

        @pl.loop(0, n_pch)
        def _(j):
            b = pl.multiple_of(wid * pad_per_w + j * SC_PAD_CHUNK, 8)
            pltpu.sync_copy(pad_hbm.at[pl.ds(b, SC_PAD_CHUNK)], ip_v)
            pltpu.sync_copy(z_v, xs_hbm.at[ip_v])

    xs = k(pos_flat, pad_rows, x1t.reshape(t_all, TILE_ROWS, LANES), zeros)
    return xs.reshape(n_sorted_rows * TILE_ROWS, LANES)


def _sc_gather_call(pos_flat, outs):
    n_assign = pos_flat.shape[0]
    nc, nw = _sc_workers()
    per_w = n_assign // nw
    n_ch, tail = _sc_split(per_w)
    assert per_w * nw == n_assign
    tail_scratch = [] if tail == 0 else [pltpu.VMEM((tail,), jnp.int32), pltpu.VMEM((tail, TILE_ROWS, LANES), F32)]

    @functools.partial(
        pl.kernel, mesh=plsc.VectorSubcoreMesh(core_axis_name="c", subcore_axis_name="s"),
        out_type=jax.ShapeDtypeStruct((n_assign, TILE_ROWS, LANES), F32),
        scratch_types=[pltpu.VMEM((SC_CHUNK,), jnp.int32),
                       pltpu.VMEM((SC_CHUNK, TILE_ROWS, LANES), F32)] + tail_scratch,
    )
    def k(pos_hbm, o_hbm, y_hbm, i_v, rows_v, *tail_refs):
        wid = lax.axis_index("s") * nc + lax.axis_index("c")

        def move(b, n, idx, rows):
            pltpu.sync_copy(pos_hbm.at[pl.ds(b, n)], idx)
            pltpu.sync_copy(o_hbm.at[idx], rows)
            pltpu.sync_copy(rows, y_hbm.at[pl.ds(b, n)])

        @pl.loop(0, n_ch)
        def _(j):
            move(pl.multiple_of(wid * per_w + j * SC_CHUNK, 8), SC_CHUNK, i_v, rows_v)

        if tail:
            move(pl.multiple_of(wid * per_w + n_ch * SC_CHUNK, 8), tail, *tail_refs)

    y = k(pos_flat, outs.reshape(-1, TILE_ROWS, LANES))
    return y.reshape(n_assign * TILE_ROWS, LANES)


def _combine_dense_kernel(alpha, n_main_tiles, x1t_ref, y0_ref, y1_ref, w_ref, g_ref, b_ref, out_ref, *tail):
    j = pl.program_id(0)
    w = w_ref[...]
    y = w[:, 0:1] * _load_token_tiles(y0_ref, TOK_TILE) + w[:, 1:2] * _load_token_tiles(y1_ref, TOK_TILE)
    res = _layer_norm(alpha * _load_token_tiles(x1t_ref, TOK_TILE) + y, g_ref[...], b_ref[...])
    if n_main_tiles is None:
        out_ref[...] = res
    else:
        @pl.when(j < n_main_tiles)
        def _():
            out_ref[...] = res

        @pl.when(j >= n_main_tiles)
        def _():
            tail[0][...] = res


def _combine_dense_call(x1t, y_pair, w_col, g, b, alpha, n_main_tiles=None):
    t_all = x1t.shape[0] // TILE_ROWS
    n_tiles = t_all // TOK_TILE
    if n_main_tiles is None:
        out_specs = pl.BlockSpec((TOK_TILE, D_MODEL), lambda i: (i, 0))
        out_shape = jax.ShapeDtypeStruct((t_all, D_MODEL), F32)
    else:
        assert n_tiles == n_main_tiles + 1
        out_specs = [pl.BlockSpec((TOK_TILE, D_MODEL), lambda i: (jnp.minimum(i, n_main_tiles - 1), 0)),
                     pl.BlockSpec((TOK_TILE, D_MODEL), lambda i: (0, 0))]
        out_shape = [jax.ShapeDtypeStruct((n_main_tiles * TOK_TILE, D_MODEL), F32),
                     jax.ShapeDtypeStruct((TOK_TILE, D_MODEL), F32)]
    tile = (TOK_TILE * TILE_ROWS, LANES)
    return pl.pallas_call(
        functools.partial(_combine_dense_kernel, alpha, n_main_tiles),
        grid=(n_tiles,),
        in_specs=[
            pl.BlockSpec(tile, lambda i: (i, 0)),
            pl.BlockSpec(tile, lambda i: (i, 0)),
            pl.BlockSpec(tile, lambda i: (i + n_tiles, 0)),
            pl.BlockSpec((TOK_TILE, TOP_K), lambda i: (i, 0)),
            pl.BlockSpec((1, D_MODEL), lambda i: (0, 0)),
            pl.BlockSpec((1, D_MODEL), lambda i: (0, 0)),
        ],
        out_specs=out_specs,
        out_shape=out_shape,
        compiler_params=pltpu.CompilerParams(dimension_semantics=("arbitrary",), vmem_limit_bytes=VMEM_LIMIT),
        name="combine",
    )(x1t, y_pair, y_pair, w_col, g, b)


def _rope_tables(seq, dec_seq, past_len):
    half = HEAD_DIM // 2
    inv = ROPE_THETA ** (-jnp.arange(half, dtype=F32) / half)
    pos = jnp.concatenate([jnp.arange(seq), past_len + (jnp.arange(TOK_TILE) % dec_seq)])
    ang = pos.astype(F32)[:, None] * inv[None, :]
    cos = jnp.cos(ang)
    sin = jnp.sin(ang)
    zero = jnp.zeros_like(sin)
    reps = LANES // HEAD_DIM
    cos_t = jnp.tile(jnp.concatenate([cos, cos], -1), (1, reps))
    sa_t = jnp.tile(jnp.concatenate([-sin, zero], -1), (1, reps))
    sb_t = jnp.tile(jnp.concatenate([zero, sin], -1), (1, reps))
    return cos_t, sa_t, sb_t


def _inclusive_cumsum(v):
    n = v.shape[0]
    tri = jnp.arange(n)[None, :] <= jnp.arange(n)[:, None]
    return jnp.sum(jnp.where(tri, v[None, :], 0), axis=1)


def _dispatch_plan(ridx, counts, n_tiles):
    t_all = ridx.shape[1]
    tiles_e = (counts + EXPERT_TILE - 1) // EXPERT_TILE
    tile_end = _inclusive_cumsum(tiles_e)
    offs = (tile_end - tiles_e) * EXPERT_TILE
    n_used = tile_end[-1:]
    tile_expert = jnp.minimum(
        jnp.sum((jnp.arange(n_tiles)[:, None] >= tile_end[None, :]).astype(jnp.int32), axis=1), N_EXPERTS - 1)
    experts = jnp.arange(N_EXPERTS)
    pos = ridx[TOP_K:2 * TOP_K] + jnp.sum(
        jnp.where(ridx[:TOP_K, :, None] == experts[None, None, :], offs[None, None, :], 0), axis=-1)
    gap_start = jnp.concatenate([offs + counts, n_used * EXPERT_TILE])
    gap_len = jnp.concatenate([tiles_e * EXPERT_TILE - counts, (n_tiles - n_used) * EXPERT_TILE])
    gap_end_q = _inclusive_cumsum(gap_len)
    q = jnp.arange(n_tiles * EXPERT_TILE - TOP_K * t_all)
    gap = jnp.sum((q[:, None] >= gap_end_q[None, :]).astype(jnp.int32), axis=1)
    in_gap = gap[:, None] == jnp.arange(N_EXPERTS + 1)[None, :]
    pad_rows = q + jnp.sum(jnp.where(in_gap, (gap_start - gap_end_q + gap_len)[None, :], 0), axis=1)
    return (pos.reshape(-1).astype(jnp.int32), pad_rows.astype(jnp.int32),
            tile_expert.astype(jnp.int32), n_used.astype(jnp.int32))


def kernel(x_prompt, x_sample, cache_k, cache_v, w_in, sinks, gm_ln_g, gm_ln_b, gm_ws, gm_bs,
           out_norm_a, out_norm_b, w_o, ln1_g, ln1_b, w_router, router_bias,
           w_gate, w_up, w_down, ln2_g, ln2_b):
    batch, seq, _ = x_prompt.shape
    dec_batch, dec_seq, _ = x_sample.shape
    depth = w_in.shape[0]
    past_len = PAST_LEN
    assert dec_batch * dec_seq == TOK_TILE and dec_seq == CHUNK and seq % TOK_TILE == 0
    assert cache_k.shape[2] == WINDOW
    alpha = (2 * depth) ** 0.25
    tiles_per_seq = seq // TOK_TILE

    tables = _rope_tables(seq, dec_seq, past_len)
    wrt = w_router.T
    rb = router_bias.reshape(N_EXPERTS, 1)

    half = batch // 2
    streams = [
        dict(x=x_prompt[:half].reshape(half * seq, D_MODEL), n_seq=half, has_sample=False),
        dict(x=jnp.concatenate([x_prompt[half:].reshape((batch - half) * seq, D_MODEL),
                                x_sample.reshape(-1, D_MODEL)], axis=0), n_seq=batch - half, has_sample=True),
    ]
    kp, vp, ks, vs, gms = [], [], [], [], []
    for l in range(depth):
        layer_args = (
            sinks[l], w_in[l].astype(BF16),
            cache_k[l].reshape(dec_batch, WINDOW, KV_WIDTH), cache_v[l].reshape(dec_batch, WINDOW, KV_WIDTH),
            gm_ln_g[l].reshape(1, GM_WIDTH), gm_ln_b[l].reshape(1, GM_WIDTH),
            gm_ws[l], gm_bs[l].T,
            out_norm_a[l].reshape(1, ATT_WIDTH), out_norm_b[l].reshape(1, GM_WIDTH),
            w_o[l].astype(BF16),
            ln1_g[l].reshape(1, D_MODEL), ln1_b[l].reshape(1, D_MODEL), wrt, rb)
        k_tails, v_tails = [], []
        for st in streams:
            n_prompt = st["n_seq"] * seq
            n_prompt_tiles = n_prompt // TOK_TILE
            t_all = st["x"].shape[0]
            n_exp_tiles = (TOP_K * t_all) // EXPERT_TILE + N_EXPERTS
            res = _mixer_call(layer_args, st["x"], tables, n_prompt_tiles, tiles_per_seq, alpha)
            if st["has_sample"]:
                x1t, k_all, v_all, vn_s, ridx, rw, counts = res
            else:
                x1t, k_all, v_all, ridx, rw, counts = res

            pos, pad_rows, tile_expert, n_used = _dispatch_plan(ridx, counts[:, 0], n_exp_tiles)
            xs = _sc_dispatch_call(pos, pad_rows, x1t, n_exp_tiles * EXPERT_TILE)
            outs = _expert_call(tile_expert, n_used, xs, w_gate, w_up, w_down, l)
            y_pair = _sc_gather_call(pos, outs)
            split = n_prompt_tiles if (l == depth - 1 and st["has_sample"]) else None
            st["x"] = _combine_dense_call(x1t, y_pair, rw[:TOP_K].T, ln2_g[l].reshape(1, D_MODEL),
                                          ln2_b[l].reshape(1, D_MODEL), alpha, n_main_tiles=split)

            k_tails.append(k_all[:n_prompt].reshape(st["n_seq"], seq, KV_WIDTH)[:, -WINDOW:])
            v_tails.append(v_all[:n_prompt].reshape(st["n_seq"], seq, KV_WIDTH)[:, -WINDOW:])
            if st["has_sample"]:
                ks.append(k_all[n_prompt:].reshape(dec_batch, dec_seq, N_KV_HEADS, HEAD_DIM))
                vs.append(v_all[n_prompt:].reshape(dec_batch, dec_seq, N_KV_HEADS, HEAD_DIM))
                gms.append(vn_s.reshape(dec_batch, dec_seq, GM_WIDTH))
        kp.append(jnp.concatenate(k_tails, axis=0).reshape(batch, WINDOW, N_KV_HEADS, HEAD_DIM))
        vp.append(jnp.concatenate(v_tails, axis=0).reshape(batch, WINDOW, N_KV_HEADS, HEAD_DIM))

    y_prompt = jnp.concatenate([streams[0]["x"], streams[1]["x"][0]], axis=0).reshape(batch, seq, D_MODEL)
    y_sample = streams[1]["x"][1].reshape(dec_batch, dec_seq, D_MODEL)
    return (y_prompt, y_sample, jnp.stack(kp), jnp.stack(vp), jnp.stack(ks), jnp.stack(vs), jnp.stack(gms))
```

```python
import functools

import jax
import jax.numpy as jnp
from jax import lax
from jax.experimental import pallas as pl
from jax.experimental.pallas import tpu as pltpu
from jax.experimental.pallas import tpu_sc as plsc

D_MODEL = 1024
CHUNK = 64
WINDOW = 128
HEAD_DIM = 64
N_Q_HEADS = 8
N_KV_HEADS = 2
Q_PER_KV = N_Q_HEADS // N_KV_HEADS
PAST_LEN = 2048
ATT_WIDTH = N_Q_HEADS * HEAD_DIM
KV_WIDTH = N_KV_HEADS * HEAD_DIM
ROPE_THETA = 10000.0
GM_GROUPS = 4
GM_CH = 128
GM_WIDTH = GM_GROUPS * GM_CH
GM_CHUNK = 128
D_MIX = ATT_WIDTH + GM_WIDTH
D_IN = ATT_WIDTH + 2 * KV_WIDTH + 2 * GM_WIDTH
N_EXPERTS = 16
N_EXPERT_GROUPS = 4
EXPERTS_PER_GROUP = N_EXPERTS // N_EXPERT_GROUPS
TOP_K = 2
D_EXPERT = 512
LN_EPS = 1e-5
NEG_INF = -1e30

LANES = 128
TILE_ROWS = D_MODEL // LANES
PACK_ROWS = TILE_ROWS // 2
HI_HALF = 0xFFFF0000
TOK_TILE = 512
EXPERT_TILE = 512
KEYS_PER_CHUNK = WINDOW + CHUNK
VMEM_LIMIT = 56 * 1024 * 1024

F32 = jnp.float32
BF16 = jnp.bfloat16


def _layer_norm(x, g, b):
    mu = jnp.mean(x, axis=-1, keepdims=True)
    d = x - mu
    var = jnp.mean(d * d, axis=-1, keepdims=True)
    return d * lax.rsqrt(var + LN_EPS) * g + b


def _rms_norm(x, g):
    return x * lax.rsqrt(jnp.mean(x * x, axis=-1, keepdims=True) + LN_EPS) * g


def _gelu(x):
    return 0.5 * x * (1.0 + lax.erf(x * (0.5 ** 0.5)))


def _store_token_tiles(ref, x):
    n = x.shape[0]
    for c in range(TILE_ROWS):
        ref[pl.ds(c, n, stride=TILE_ROWS), :] = x[:, c * LANES:(c + 1) * LANES]


def _load_token_tiles(ref, n, row0=0):
    return jnp.concatenate([ref[pl.ds(row0 + c, n, stride=TILE_ROWS), :] for c in range(TILE_ROWS)], axis=-1)


def _bf16_bits(x):
    return lax.bitcast_convert_type(x.astype(BF16).astype(F32), jnp.uint32)


def _store_packed_tiles(ref, x):
    n = x.shape[0]
    for s in range(PACK_ROWS):
        lo = _bf16_bits(x[:, (2 * s) * LANES:(2 * s + 1) * LANES]) >> 16
        hi = _bf16_bits(x[:, (2 * s + 1) * LANES:(2 * s + 2) * LANES]) & jnp.uint32(HI_HALF)
        ref[pl.ds(s, n, stride=PACK_ROWS), :] = hi | lo


def _load_packed_tiles(ref, n, dtype):
    cols = []
    for s in range(PACK_ROWS):
        w = ref[pl.ds(s, n, stride=PACK_ROWS), :]
        cols.append(lax.bitcast_convert_type(w << 16, F32).astype(dtype))
        cols.append(lax.bitcast_convert_type(w & jnp.uint32(HI_HALF), F32).astype(dtype))
    return jnp.concatenate(cols, axis=-1)


def _first_index_of_max(rows):
    m = rows[0]
    for r in rows[1:]:
        m = jnp.maximum(m, r)
    idx = jnp.full(m.shape, len(rows), jnp.int32)
    for e in reversed(range(len(rows))):
        idx = jnp.where(rows[e] == m, e, idx)
    return m, idx


def _mixer_kernel(n_prompt_tiles, tiles_per_seq, alpha, has_sample,
                  sinks_ref, x_ref, w_in_ref, cos_ref, sa_ref, sb_ref, ck_ref, cv_ref,
                  lng_ref, lnb_ref, ws_ref, bst_ref, na_ref, nb_ref, w_o_ref,
                  l1g_ref, l1b_ref, wrt_ref, rb_ref, x1t_ref, x1p_ref, k_ref, v_ref, *rest):
    if has_sample:
        vn_ref, ridx_ref, rw_ref, counts_ref, kd_ref, vd_ref, qm_ref, att_ref, gm_ref, cnt_ref = rest
    else:
        ridx_ref, rw_ref, counts_ref, kd_ref, vd_ref, qm_ref, att_ref, gm_ref, cnt_ref = rest
    i = pl.program_id(0)
    is_sample = i >= n_prompt_tiles
    seq_start = jnp.logical_and(jnp.logical_not(is_sample), (i % tiles_per_seq) == 0)
    tm = TOK_TILE

    x = x_ref[...]
    h = jnp.dot(x.astype(BF16), w_in_ref[...], preferred_element_type=F32)

    cos = cos_ref[...]
    sa = sa_ref[...]
    sb = sb_ref[...]

    def rope(blk):
        return blk * cos + pltpu.roll(blk, LANES - HEAD_DIM // 2, 1) * sa + pltpu.roll(blk, HEAD_DIM // 2, 1) * sb

    lane = lax.broadcasted_iota(jnp.int32, (1, LANES), 1)
    lo_half = lane < HEAD_DIM

    o_k = ATT_WIDTH
    o_v = o_k + KV_WIDTH
    o_u = o_v + KV_WIDTH
    o_g = o_u + GM_WIDTH

    k_rot = rope(h[:, o_k:o_v])
    v_new = h[:, o_v:o_u]
    k_ref[...] = k_rot
    v_ref[...] = v_new

    def dup_heads(a):
        sw = pltpu.roll(a, HEAD_DIM, 1)
        return jnp.where(lo_half, a, sw).astype(BF16), jnp.where(lo_half, sw, a).astype(BF16)

    k_d = dup_heads(k_rot)
    v_d = dup_heads(v_new)

    @pl.when(jnp.logical_not(is_sample))
    def _():
        @pl.when(seq_start)
        def _():
            for g in range(N_KV_HEADS):
                kd_ref[g, 0:WINDOW, :] = jnp.zeros((WINDOW, LANES), BF16)
                vd_ref[g, 0:WINDOW, :] = jnp.zeros((WINDOW, LANES), BF16)

        @pl.when(jnp.logical_not(seq_start))
        def _():
            for g in range(N_KV_HEADS):
                kd_ref[g, 0:WINDOW, :] = kd_ref[g, tm:tm + WINDOW, :]
                vd_ref[g, 0:WINDOW, :] = vd_ref[g, tm:tm + WINDOW, :]

        for g in range(N_KV_HEADS):
            kd_ref[g, WINDOW:WINDOW + tm, :] = k_d[g]
            vd_ref[g, WINDOW:WINDOW + tm, :] = v_d[g]

    @pl.when(is_sample)
    def _():
        for b in range(tm // CHUNK):
            ck = dup_heads(ck_ref[b])
            cv = dup_heads(cv_ref[b])
            base = b * KEYS_PER_CHUNK
            for g in range(N_KV_HEADS):
                kd_ref[g, base:base + WINDOW, :] = ck[g]
                vd_ref[g, base:base + WINDOW, :] = cv[g]
                kd_ref[g, base + WINDOW:base + KEYS_PER_CHUNK, :] = k_d[g][b * CHUNK:(b + 1) * CHUNK]
                vd_ref[g, base + WINDOW:base + KEYS_PER_CHUNK, :] = v_d[g][b * CHUNK:(b + 1) * CHUNK]

    scale = HEAD_DIM ** -0.5
    for b in range(ATT_WIDTH // LANES):
        qb = rope(h[:, b * LANES:(b + 1) * LANES]) * scale
        qm_ref[2 * b] = jnp.where(lo_half, qb, 0.0).astype(BF16)
        qm_ref[2 * b + 1] = jnp.where(lo_half, 0.0, qb).astype(BF16)

    key_stride = jnp.where(is_sample, KEYS_PER_CHUNK, CHUNK)
    col = lax.broadcasted_iota(jnp.int32, (1, KEYS_PER_CHUNK), 1)
    row_blk = lax.broadcasted_iota(jnp.int32, (Q_PER_KV * CHUNK, 1), 0) // CHUNK
    sink_cols = [
        jnp.where(row_blk == 0, sinks_ref[Q_PER_KV * g],
                  jnp.where(row_blk == 1, sinks_ref[Q_PER_KV * g + 1],
                            jnp.where(row_blk == 2, sinks_ref[Q_PER_KV * g + 2], sinks_ref[Q_PER_KV * g + 3])))
        for g in range(N_KV_HEADS)]

    def chunk_body(c, carry):
        r0 = pl.multiple_of(c * CHUNK, CHUNK)
        k0 = pl.multiple_of(c * key_stride, CHUNK)
        first_valid = jnp.where(seq_start, (WINDOW // CHUNK - c) * CHUNK, 0)
        valid = col >= first_valid
        outs = []
        for g in range(N_KV_HEADS):
            q_st = jnp.concatenate([qm_ref[Q_PER_KV * g + r, pl.ds(r0, CHUNK), :] for r in range(Q_PER_KV)],
                                   axis=0)
            keys = kd_ref[g, pl.ds(k0, KEYS_PER_CHUNK), :]
            vals = vd_ref[g, pl.ds(k0, KEYS_PER_CHUNK), :]
            s = lax.dot_general(q_st, keys, (((1,), (1,)), ((), ())), preferred_element_type=F32)
            s = jnp.where(valid, s, NEG_INF)
            sink = sink_cols[g]
            m = jnp.maximum(jnp.max(s, axis=-1, keepdims=True), sink)
            p = jnp.exp(s - m)
            denom = jnp.sum(p, axis=-1, keepdims=True) + jnp.exp(sink - m)
            o = jnp.dot(p.astype(BF16), vals, preferred_element_type=F32) * (1.0 / denom)
            for bb in range(2):
                outs.append(jnp.where(lo_half, o[(2 * bb) * CHUNK:(2 * bb + 1) * CHUNK],
                                      o[(2 * bb + 1) * CHUNK:(2 * bb + 2) * CHUNK]))
        for b in range(ATT_WIDTH // LANES):
            att_ref[pl.ds(r0, CHUNK), b * LANES:(b + 1) * LANES] = outs[b]
        return carry

    lax.fori_loop(0, tm // CHUNK, chunk_body, 0, unroll=8)

    prow = lax.broadcasted_iota(jnp.int32, (GM_CHUNK, GM_CHUNK), 0)
    pcol = lax.broadcasted_iota(jnp.int32, (GM_CHUNK, GM_CHUNK), 1)
    half = GM_CHUNK // 2
    tril = pcol <= prow
    same_blk = (prow < half) == (pcol < half)
    prow1 = lax.broadcasted_iota(jnp.int32, (GM_CHUNK, 1), 0)
    bst = bst_ref[...]
    bst_s = jnp.where(prow1 < half, bst, pltpu.roll(bst, half, 0))
    bias = jnp.where(is_sample, bst_s, bst)
    for g in range(GM_GROUPS):
        wg_ = ws_ref[g]
        w_s = jnp.where(prow < half, wg_, pltpu.roll(pltpu.roll(wg_, half, 0), half, 1))
        w_eff = jnp.where(is_sample, jnp.where(same_blk, w_s, 0.0), wg_)
        w_eff = jnp.where(tril, w_eff, 0.0).astype(BF16)
        cols = slice(g * GM_CH, (g + 1) * GM_CH)
        ug = _gelu(h[:, o_u + g * GM_CH:o_u + (g + 1) * GM_CH])
        vn = _layer_norm(_gelu(h[:, o_g + g * GM_CH:o_g + (g + 1) * GM_CH]), lng_ref[:, cols], lnb_ref[:, cols])

        if has_sample:
            @pl.when(is_sample)
            def _(cols=cols, vn=vn):
                vn_ref[:, cols] = vn

        vn_b = vn.astype(BF16)
        for n in range(tm // GM_CHUNK):
            rows = slice(n * GM_CHUNK, (n + 1) * GM_CHUNK)
            s = jnp.dot(w_eff, vn_b[rows], preferred_element_type=F32) + bias[:, g:g + 1]
            gm_ref[rows, cols] = ug[rows] * s

    mixed = jnp.concatenate([_rms_norm(att_ref[...], na_ref[...]), _rms_norm(gm_ref[...], nb_ref[...])], axis=-1)
    y = alpha * x + jnp.dot(mixed.astype(BF16), w_o_ref[...], preferred_element_type=F32)
    x1 = _layer_norm(y, l1g_ref[...], l1b_ref[...])
    _store_token_tiles(x1t_ref, x1)
    _store_packed_tiles(x1p_ref, x1)

    logits = lax.dot_general(wrt_ref[...], x1, (((1,), (1,)), ((), ())),
                             preferred_element_type=F32, precision=lax.Precision.HIGHEST)
    scores = jax.nn.sigmoid(logits)
    sel = scores + rb_ref[...]
    sel_rows = [sel[e:e + 1, :] for e in range(N_EXPERTS)]
    sc_rows = [scores[e:e + 1, :] for e in range(N_EXPERTS)]
    grp = []
    for g in range(N_EXPERT_GROUPS):
        r = sel_rows[g * EXPERTS_PER_GROUP:(g + 1) * EXPERTS_PER_GROUP]
        best_pair = None
        for a in range(EXPERTS_PER_GROUP):
            for b in range(a + 1, EXPERTS_PER_GROUP):
                pair = r[a] + r[b]
                best_pair = pair if best_pair is None else jnp.maximum(best_pair, pair)
        grp.append(best_pair)
    _, best = _first_index_of_max(grp)
    masked = [jnp.where(best == (e // EXPERTS_PER_GROUP), sel_rows[e], NEG_INF) for e in range(N_EXPERTS)]
    _, e0 = _first_index_of_max(masked)
    masked2 = [jnp.where(e0 == e, -jnp.inf, masked[e]) for e in range(N_EXPERTS)]
    _, e1 = _first_index_of_max(masked2)
    w0 = jnp.zeros_like(sc_rows[0])
    w1 = jnp.zeros_like(sc_rows[0])
    for e in range(N_EXPERTS):
        w0 = jnp.where(e0 == e, sc_rows[e], w0)
        w1 = jnp.where(e1 == e, sc_rows[e], w1)
    wsum = w0 + w1
    rw_ref[...] = jnp.concatenate([w0 / wsum, w1 / wsum, jnp.zeros((6, tm), F32)], axis=0)

    @pl.when(i == 0)
    def _():
        cnt_ref[...] = jnp.zeros_like(cnt_ref)

    eid = lax.broadcasted_iota(jnp.int32, (N_EXPERTS, tm), 0)
    oh0 = (eid == e0).astype(F32)
    oh1 = (eid == e1).astype(F32)
    oh = oh0 + oh1
    upper = (lax.broadcasted_iota(jnp.int32, (tm, tm), 0) <= lax.broadcasted_iota(jnp.int32, (tm, tm), 1))
    incl = jnp.dot(oh.astype(BF16), upper.astype(BF16), preferred_element_type=F32)
    before = cnt_ref[:, 0:1] + incl - oh
    r0 = jnp.sum(oh0 * before, axis=0, keepdims=True).astype(jnp.int32)
    r1 = jnp.sum(oh1 * before, axis=0, keepdims=True).astype(jnp.int32)
    cnt = cnt_ref[...] + incl[:, tm - 1:tm]
    cnt_ref[...] = cnt
    counts_ref[...] = cnt.astype(jnp.int32)
    ridx_ref[...] = jnp.concatenate([e0, e1, r0, r1, jnp.zeros((4, tm), jnp.int32)], axis=0)


def _mixer_call(layer_args, x_all, tables, n_prompt_tiles, tiles_per_seq, alpha):
    (sinks, w_in, ck, cv, lng, lnb, ws, bst, na, nb, w_o, l1g, l1b, wrt, rb) = layer_args
    cos_t, sa_t, sb_t = tables
    t_all = x_all.shape[0]
    n_tiles = t_all // TOK_TILE
    n_seq_tiles = tiles_per_seq
    has_sample = n_tiles > n_prompt_tiles

    def const(shape):
        nd = len(shape)
        return pl.BlockSpec(shape, lambda i, _nd=nd: (0,) * _nd)

    def tab_map(i):
        return (jnp.where(i < n_prompt_tiles, i % n_seq_tiles, n_seq_tiles), 0)

    row_blk = lambda w: pl.BlockSpec((TOK_TILE, w), lambda i: (i, 0))
    in_specs = [
        pl.BlockSpec(memory_space=pltpu.SMEM),
        row_blk(D_MODEL),
        const((D_MODEL, D_IN)),
        pl.BlockSpec((TOK_TILE, LANES), tab_map),
        pl.BlockSpec((TOK_TILE, LANES), tab_map),
        pl.BlockSpec((TOK_TILE, LANES), tab_map),
        const(ck.shape), const(cv.shape),
        const((1, GM_WIDTH)), const((1, GM_WIDTH)),
        const((GM_GROUPS, GM_CHUNK, GM_CHUNK)), const((GM_CHUNK, GM_GROUPS)),
        const((1, ATT_WIDTH)), const((1, GM_WIDTH)),
        const((D_MIX, D_MODEL)),
        const((1, D_MODEL)), const((1, D_MODEL)),
        const((N_EXPERTS, D_MODEL)), const((N_EXPERTS, 1)),
    ]
    out_shape = [
        jax.ShapeDtypeStruct((t_all * TILE_ROWS, LANES), F32),
        jax.ShapeDtypeStruct((t_all * PACK_ROWS, LANES), jnp.uint32),
        jax.ShapeDtypeStruct((t_all, KV_WIDTH), F32),
        jax.ShapeDtypeStruct((t_all, KV_WIDTH), F32),
        jax.ShapeDtypeStruct((TOK_TILE, GM_WIDTH), F32),
        jax.ShapeDtypeStruct((8, t_all), jnp.int32),
        jax.ShapeDtypeStruct((8, t_all), F32),
        jax.ShapeDtypeStruct((N_EXPERTS, LANES), jnp.int32),
    ]
    out_specs = [
        pl.BlockSpec((TOK_TILE * TILE_ROWS, LANES), lambda i: (i, 0)),
        pl.BlockSpec((TOK_TILE * PACK_ROWS, LANES), lambda i: (i, 0)),
        row_blk(KV_WIDTH), row_blk(KV_WIDTH),
        const((TOK_TILE, GM_WIDTH)),
        pl.BlockSpec((8, TOK_TILE), lambda i: (0, i)),
        pl.BlockSpec((8, TOK_TILE), lambda i: (0, i)),
        const((N_EXPERTS, LANES)),
    ]
    if not has_sample:
        del out_shape[4], out_specs[4]
    kd_rows = (TOK_TILE // CHUNK) * KEYS_PER_CHUNK
    scratch = [
        pltpu.VMEM((N_KV_HEADS, kd_rows, LANES), BF16),
        pltpu.VMEM((N_KV_HEADS, kd_rows, LANES), BF16),
        pltpu.VMEM((N_Q_HEADS, TOK_TILE, LANES), BF16),
        pltpu.VMEM((TOK_TILE, ATT_WIDTH), F32),
        pltpu.VMEM((TOK_TILE, GM_WIDTH), F32),
        pltpu.VMEM((N_EXPERTS, LANES), F32),
    ]
    return pl.pallas_call(
        functools.partial(_mixer_kernel, n_prompt_tiles, tiles_per_seq, alpha, has_sample),
        grid=(n_tiles,),
        in_specs=in_specs, out_specs=out_specs, out_shape=out_shape,
        scratch_shapes=scratch,
        compiler_params=pltpu.CompilerParams(dimension_semantics=("arbitrary",), vmem_limit_bytes=VMEM_LIMIT),
        name="mixer",
    )(sinks, x_all, w_in, cos_t, sa_t, sb_t, ck, cv, lng, lnb, ws, bst, na, nb, w_o, l1g, l1b, wrt, rb)


def _expert_kernel(te_ref, nt_ref, xs_ref, wg_ref, wu_ref, wd_ref, out_ref, wg_s, wu_s, wd_s):
    i = pl.program_id(0)
    changed = jnp.logical_or(i == 0, te_ref[i] != te_ref[jnp.maximum(i - 1, 0)])

    @pl.when(changed)
    def _():
        wg_s[...] = wg_ref[0, 0].astype(BF16)
        wu_s[...] = wu_ref[0, 0].astype(BF16)
        wd_s[...] = wd_ref[0, 0].astype(BF16)

    @pl.when(i < nt_ref[0])
    def _():
        xs = _load_packed_tiles(xs_ref, EXPERT_TILE, BF16)
        g = jnp.dot(xs, wg_s[...], preferred_element_type=F32)
        u = jnp.dot(xs, wu_s[...], preferred_element_type=F32)
        hmid = (g * jax.nn.sigmoid(g)) * u
        _store_packed_tiles(out_ref, jnp.dot(hmid.astype(BF16), wd_s[...], preferred_element_type=F32))

    @pl.when(i >= nt_ref[0])
    def _():
        out_ref[...] = jnp.zeros_like(out_ref)


def _expert_call(tile_expert, n_used, xs, wg, wu, wd, layer):
    n_tiles = xs.shape[0] // (EXPERT_TILE * PACK_ROWS)
    tile_blk = pl.BlockSpec((EXPERT_TILE * PACK_ROWS, LANES), lambda i, te, nt: (i, 0))
    grid_spec = pltpu.PrefetchScalarGridSpec(
        num_scalar_prefetch=2,
        grid=(n_tiles,),
        in_specs=[
            tile_blk,
            pl.BlockSpec((1, 1, D_MODEL, D_EXPERT), lambda i, te, nt: (layer, te[i], 0, 0)),
            pl.BlockSpec((1, 1, D_MODEL, D_EXPERT), lambda i, te, nt: (layer, te[i], 0, 0)),
            pl.BlockSpec((1, 1, D_EXPERT, D_MODEL), lambda i, te, nt: (layer, te[i], 0, 0)),
        ],
        out_specs=tile_blk,
        scratch_shapes=[
            pltpu.VMEM((D_MODEL, D_EXPERT), BF16),
            pltpu.VMEM((D_MODEL, D_EXPERT), BF16),
            pltpu.VMEM((D_EXPERT, D_MODEL), BF16),
        ],
    )
    return pl.pallas_call(
        _expert_kernel,
        grid_spec=grid_spec,
        out_shape=jax.ShapeDtypeStruct(xs.shape, jnp.uint32),
        compiler_params=pltpu.CompilerParams(dimension_semantics=("arbitrary",), vmem_limit_bytes=VMEM_LIMIT),
        name="experts",
    )(tile_expert, n_used, xs, wg, wu, wd)


ROWS_PER_STEP = TOK_TILE * TOP_K


def _token_tile(ref, idx):
    return ref.at[pl.ds(pl.multiple_of(idx * TILE_ROWS, TILE_ROWS), TILE_ROWS)]


def _dispatch_kernel(n_tok_steps, t_all, pos_ref, pad_ref, x1t_ref, xs_hbm, zero_ref, sems):
    j = pl.program_id(0)

    @pl.when(j == 0)
    def _():
        zero_ref[...] = jnp.zeros_like(zero_ref)

    @pl.when(j < n_tok_steps)
    def _():
        def body(tt, c):
            t = j * TOK_TILE + tt
            for k in range(TOP_K):
                pltpu.make_async_copy(_token_tile(x1t_ref, tt), _token_tile(xs_hbm, pos_ref[k * t_all + t]),
                                      sems.at[k]).start()
            return c
        lax.fori_loop(0, TOK_TILE, body, 0, unroll=8)

    @pl.when(j >= n_tok_steps)
    def _():
        def body(q, c):
            for k in range(TOP_K):
                row = pad_ref[(j - n_tok_steps) * ROWS_PER_STEP + k * TOK_TILE + q]
                pltpu.make_async_copy(zero_ref, _token_tile(xs_hbm, row), sems.at[k]).start()
            return c
        lax.fori_loop(0, TOK_TILE, body, 0, unroll=8)

    for k in range(TOP_K):
        pltpu.make_async_copy(x1t_ref, xs_hbm.at[pl.ds(0, TOK_TILE * TILE_ROWS)], sems.at[k]).wait()


def _dispatch_call(pos_flat, pad_rows, x1t, n_sorted_rows):
    t_all = x1t.shape[0] // TILE_ROWS
    n_tok_steps = t_all // TOK_TILE
    n_steps = n_tok_steps + pad_rows.shape[0] // ROWS_PER_STEP
    grid_spec = pltpu.PrefetchScalarGridSpec(
        num_scalar_prefetch=2,
        grid=(n_steps,),
        in_specs=[pl.BlockSpec((TOK_TILE * TILE_ROWS, LANES),
                               lambda j, p, q: (jnp.minimum(j, n_tok_steps - 1), 0))],
        out_specs=pl.BlockSpec(memory_space=pl.ANY),
        scratch_shapes=[pltpu.VMEM((TILE_ROWS, LANES), F32), pltpu.SemaphoreType.DMA((TOP_K,))],
    )
    return pl.pallas_call(
        functools.partial(_dispatch_kernel, n_tok_steps, t_all),
        grid_spec=grid_spec,
        out_shape=jax.ShapeDtypeStruct((n_sorted_rows * TILE_ROWS, LANES), F32),
        compiler_params=pltpu.CompilerParams(dimension_semantics=("arbitrary",)),
        name="dispatch",
    )(pos_flat, pad_rows, x1t)


def _combine_kernel(alpha, t_all, n_main_tiles, pos_ref, x1t_ref, w_ref, g_ref, b_ref, outs_hbm, *rest):
    if n_main_tiles is None:
        out_ref, gbuf, sems = rest
        tail_ref = None
    else:
        out_ref, tail_ref, gbuf, sems = rest
    j = pl.program_id(0)
    slot = j % 2
    k_rows = TOK_TILE * TILE_ROWS

    n = pl.num_programs(0)

    def wait(s):
        pltpu.make_async_copy(outs_hbm.at[pl.ds(0, TOP_K * k_rows)], gbuf.at[s], sems.at[s]).wait()

    def issue(step, s):
        def body(tt, c):
            t = step * TOK_TILE + tt
            for k in range(TOP_K):
                pltpu.make_async_copy(_token_tile(outs_hbm, pos_ref[k * t_all + t]),
                                      _token_tile(gbuf.at[s], k * TOK_TILE + tt), sems.at[s]).start()
            return c
        lax.fori_loop(0, TOK_TILE, body, 0, unroll=8)

    @pl.when(j == 0)
    def _():
        issue(0, 0)

    @pl.when(j + 1 < n)
    def _():
        issue(j + 1, 1 - slot)

    wait(slot)

    w = w_ref[...]
    y = (w[:, 0:1] * _load_token_tiles(gbuf.at[slot], TOK_TILE)
         + w[:, 1:2] * _load_token_tiles(gbuf.at[slot], TOK_TILE, k_rows))
    x1 = _load_token_tiles(x1t_ref, TOK_TILE)
    res = _layer_norm(alpha * x1 + y, g_ref[...], b_ref[...])
    if tail_ref is None:
        out_ref[...] = res
    else:
        @pl.when(j < n_main_tiles)
        def _():
            out_ref[...] = res

        @pl.when(j >= n_main_tiles)
        def _():
            tail_ref[...] = res


def _combine_call(pos_flat, x1t, w_col, g, b, outs, alpha, n_main_tiles=None):
    t_all = x1t.shape[0] // TILE_ROWS
    n_tiles = t_all // TOK_TILE
    if n_main_tiles is None:
        out_specs = pl.BlockSpec((TOK_TILE, D_MODEL), lambda i, p: (i, 0))
        out_shape = jax.ShapeDtypeStruct((t_all, D_MODEL), F32)
    else:
        assert n_tiles == n_main_tiles + 1
        out_specs = [pl.BlockSpec((TOK_TILE, D_MODEL), lambda i, p: (jnp.minimum(i, n_main_tiles - 1), 0)),
                     pl.BlockSpec((TOK_TILE, D_MODEL), lambda i, p: (0, 0))]
        out_shape = [jax.ShapeDtypeStruct((n_main_tiles * TOK_TILE, D_MODEL), F32),
                     jax.ShapeDtypeStruct((TOK_TILE, D_MODEL), F32)]
    grid_spec = pltpu.PrefetchScalarGridSpec(
        num_scalar_prefetch=1,
        grid=(t_all // TOK_TILE,),
        in_specs=[
            pl.BlockSpec((TOK_TILE * TILE_ROWS, LANES), lambda i, p: (i, 0)),
            pl.BlockSpec((TOK_TILE, TOP_K), lambda i, p: (i, 0)),
            pl.BlockSpec((1, D_MODEL), lambda i, p: (0, 0)),
            pl.BlockSpec((1, D_MODEL), lambda i, p: (0, 0)),
            pl.BlockSpec(memory_space=pl.ANY),
        ],
        out_specs=out_specs,
        scratch_shapes=[pltpu.VMEM((2, TOP_K * TOK_TILE * TILE_ROWS, LANES), F32), pltpu.SemaphoreType.DMA((2,))],
    )
    return pl.pallas_call(
        functools.partial(_combine_kernel, alpha, t_all, n_main_tiles),
        grid_spec=grid_spec,
        out_shape=out_shape,
        compiler_params=pltpu.CompilerParams(dimension_semantics=("arbitrary",), vmem_limit_bytes=VMEM_LIMIT),
        name="combine",
    )(pos_flat, x1t, w_col, g, b, outs)


SC_CHUNK = 96
SC_PAD_CHUNK = 32


def _sc_workers():
    info = plsc.get_sparse_core_info()
    return info.num_cores, info.num_cores * info.num_subcores


def _sc_split(per_w):
    n_full = per_w // SC_CHUNK
    tail = per_w - n_full * SC_CHUNK
    assert tail % 8 == 0
    return n_full, tail


def _sc_dispatch_call(pos_flat, pad_rows, x_tiles, tile_rows, n_sorted_rows):
    t_all = x_tiles.shape[0] // tile_rows
    dt = x_tiles.dtype
    nc, nw = _sc_workers()
    per_w = t_all // nw
    n_ch, tail = _sc_split(per_w)
    n_pad = pad_rows.shape[0]
    pad_per_w = n_pad // nw
    n_pch = pad_per_w // SC_PAD_CHUNK
    assert per_w * nw == t_all and n_pch * SC_PAD_CHUNK * nw == n_pad
    zeros = jnp.zeros((SC_PAD_CHUNK, tile_rows, LANES), dt)
    tail_scratch = [] if tail == 0 else [
        pltpu.VMEM((tail,), jnp.int32), pltpu.VMEM((tail,), jnp.int32), pltpu.VMEM((tail, tile_rows, LANES), dt)]

    @functools.partial(
        pl.kernel, mesh=plsc.VectorSubcoreMesh(core_axis_name="c", subcore_axis_name="s"),
        out_type=jax.ShapeDtypeStruct((n_sorted_rows, tile_rows, LANES), dt),
        scratch_types=[
            pltpu.VMEM((SC_CHUNK,), jnp.int32), pltpu.VMEM((SC_CHUNK,), jnp.int32),
            pltpu.VMEM((SC_CHUNK, tile_rows, LANES), dt),
            pltpu.VMEM((SC_PAD_CHUNK,), jnp.int32),
            pltpu.VMEM((SC_PAD_CHUNK, tile_rows, LANES), dt),
        ] + tail_scratch,
    )
    def k(pos_hbm, pad_hbm, x_hbm, z_hbm, xs_hbm, i0_v, i1_v, rows_v, ip_v, z_v, *tail_refs):
        wid = lax.axis_index("s") * nc + lax.axis_index("c")

        def move(b, n, idx0, idx1, rows):
            pltpu.sync_copy(pos_hbm.at[pl.ds(b, n)], idx0)
            pltpu.sync_copy(pos_hbm.at[pl.ds(t_all + b, n)], idx1)
            pltpu.sync_copy(x_hbm.at[pl.ds(b, n)], rows)
            pltpu.sync_copy(rows, xs_hbm.at[idx0])
            pltpu.sync_copy(rows, xs_hbm.at[idx1])

        @pl.loop(0, n_ch)
        def _(j):
            move(pl.multiple_of(wid * per_w + j * SC_CHUNK, 8), SC_CHUNK, i0_v, i1_v, rows_v)

        if tail:
            move(pl.multiple_of(wid * per_w + n_ch * SC_CHUNK, 8), tail, *tail_refs)

        pltpu.sync_copy(z_hbm, z_v)

        @pl.loop(0, n_pch)
        def _(j):
            b = pl.multiple_of(wid * pad_per_w + j * SC_PAD_CHUNK, 8)
            pltpu.sync_copy(pad_hbm.at[pl.ds(b, SC_PAD_CHUNK)], ip_v)
            pltpu.sync_copy(z_v, xs_hbm.at[ip_v])

    xs = k(pos_flat, pad_rows, x_tiles.reshape(t_all, tile_rows, LANES), zeros)
    return xs.reshape(n_sorted_rows * tile_rows, LANES)


def _sc_gather_call(pos_flat, outs, tile_rows):
    n_assign = pos_flat.shape[0]
    dt = outs.dtype
    nc, nw = _sc_workers()
    per_w = n_assign // nw
    n_ch, tail = _sc_split(per_w)
    assert per_w * nw == n_assign
    tail_scratch = [] if tail == 0 else [pltpu.VMEM((tail,), jnp.int32), pltpu.VMEM((tail, tile_rows, LANES), dt)]

    @functools.partial(
        pl.kernel, mesh=plsc.VectorSubcoreMesh(core_axis_name="c", subcore_axis_name="s"),
        out_type=jax.ShapeDtypeStruct((n_assign, tile_rows, LANES), dt),
        scratch_types=[pltpu.VMEM((SC_CHUNK,), jnp.int32),
                       pltpu.VMEM((SC_CHUNK, tile_rows, LANES), dt)] + tail_scratch,
    )
    def k(pos_hbm, o_hbm, y_hbm, i_v, rows_v, *tail_refs):
        wid = lax.axis_index("s") * nc + lax.axis_index("c")

        def move(b, n, idx, rows):
            pltpu.sync_copy(pos_hbm.at[pl.ds(b, n)], idx)
            pltpu.sync_copy(o_hbm.at[idx], rows)
            pltpu.sync_copy(rows, y_hbm.at[pl.ds(b, n)])

        @pl.loop(0, n_ch)
        def _(j):
            move(pl.multiple_of(wid * per_w + j * SC_CHUNK, 8), SC_CHUNK, i_v, rows_v)

        if tail:
            move(pl.multiple_of(wid * per_w + n_ch * SC_CHUNK, 8), tail, *tail_refs)

    y = k(pos_flat, outs.reshape(-1, tile_rows, LANES))
    return y.reshape(n_assign * tile_rows, LANES)


def _combine_dense_kernel(alpha, n_main_tiles, x1t_ref, y0_ref, y1_ref, w_ref, g_ref, b_ref, out_ref, *tail):
    j = pl.program_id(0)
    w = w_ref[...]
    y = (w[:, 0:1] * _load_packed_tiles(y0_ref, TOK_TILE, F32)
         + w[:, 1:2] * _load_packed_tiles(y1_ref, TOK_TILE, F32))
    res = _layer_norm(alpha * _load_token_tiles(x1t_ref, TOK_TILE) + y, g_ref[...], b_ref[...])
    if n_main_tiles is None:
        out_ref[...] = res
    else:
        @pl.when(j < n_main_tiles)
        def _():
            out_ref[...] = res

        @pl.when(j >= n_main_tiles)
        def _():
            tail[0][...] = res


def _combine_dense_call(x1t, y_pair, w_col, g, b, alpha, n_main_tiles=None):
    t_all = x1t.shape[0] // TILE_ROWS
    n_tiles = t_all // TOK_TILE
    if n_main_tiles is None:
        out_specs = pl.BlockSpec((TOK_TILE, D_MODEL), lambda i: (i, 0))
        out_shape = jax.ShapeDtypeStruct((t_all, D_MODEL), F32)
    else:
        assert n_tiles == n_main_tiles + 1
        out_specs = [pl.BlockSpec((TOK_TILE, D_MODEL), lambda i: (jnp.minimum(i, n_main_tiles - 1), 0)),
                     pl.BlockSpec((TOK_TILE, D_MODEL), lambda i: (0, 0))]
        out_shape = [jax.ShapeDtypeStruct((n_main_tiles * TOK_TILE, D_MODEL), F32),
                     jax.ShapeDtypeStruct((TOK_TILE, D_MODEL), F32)]
    tile = (TOK_TILE * TILE_ROWS, LANES)
    packed = (TOK_TILE * PACK_ROWS, LANES)
    return pl.pallas_call(
        functools.partial(_combine_dense_kernel, alpha, n_main_tiles),
        grid=(n_tiles,),
        in_specs=[
            pl.BlockSpec(tile, lambda i: (i, 0)),
            pl.BlockSpec(packed, lambda i: (i, 0)),
            pl.BlockSpec(packed, lambda i: (i + n_tiles, 0)),
            pl.BlockSpec((TOK_TILE, TOP_K), lambda i: (i, 0)),
            pl.BlockSpec((1, D_MODEL), lambda i: (0, 0)),
            pl.BlockSpec((1, D_MODEL), lambda i: (0, 0)),
        ],
        out_specs=out_specs,
        out_shape=out_shape,
        compiler_params=pltpu.CompilerParams(dimension_semantics=("arbitrary",), vmem_limit_bytes=VMEM_LIMIT),
        name="combine",
    )(x1t, y_pair, y_pair, w_col, g, b)


def _rope_tables(seq, dec_seq, past_len):
    half = HEAD_DIM // 2
    inv = ROPE_THETA ** (-jnp.arange(half, dtype=F32) / half)
    pos = jnp.concatenate([jnp.arange(seq), past_len + (jnp.arange(TOK_TILE) % dec_seq)])
    ang = pos.astype(F32)[:, None] * inv[None, :]
    cos = jnp.cos(ang)
    sin = jnp.sin(ang)
    zero = jnp.zeros_like(sin)
    reps = LANES // HEAD_DIM
    cos_t = jnp.tile(jnp.concatenate([cos, cos], -1), (1, reps))
    sa_t = jnp.tile(jnp.concatenate([-sin, zero], -1), (1, reps))
    sb_t = jnp.tile(jnp.concatenate([zero, sin], -1), (1, reps))
    return cos_t, sa_t, sb_t


def _inclusive_cumsum(v):
    n = v.shape[0]
    tri = jnp.arange(n)[None, :] <= jnp.arange(n)[:, None]
    return jnp.sum(jnp.where(tri, v[None, :], 0), axis=1)


def _dispatch_plan(ridx, counts, n_tiles):
    t_all = ridx.shape[1]
    tiles_e = (counts + EXPERT_TILE - 1) // EXPERT_TILE
    tile_end = _inclusive_cumsum(tiles_e)
    offs = (tile_end - tiles_e) * EXPERT_TILE
    n_used = tile_end[-1:]
    tile_expert = jnp.minimum(
        jnp.sum((jnp.arange(n_tiles)[:, None] >= tile_end[None, :]).astype(jnp.int32), axis=1), N_EXPERTS - 1)
    experts = jnp.arange(N_EXPERTS)
    pos = ridx[TOP_K:2 * TOP_K] + jnp.sum(
        jnp.where(ridx[:TOP_K, :, None] == experts[None, None, :], offs[None, None, :], 0), axis=-1)
    gap_start = jnp.concatenate([offs + counts, n_used * EXPERT_TILE])
    gap_len = jnp.concatenate([tiles_e * EXPERT_TILE - counts, (n_tiles - n_used) * EXPERT_TILE])
    gap_end_q = _inclusive_cumsum(gap_len)
    q = jnp.arange(n_tiles * EXPERT_TILE - TOP_K * t_all)
    gap = jnp.sum((q[:, None] >= gap_end_q[None, :]).astype(jnp.int32), axis=1)
    in_gap = gap[:, None] == jnp.arange(N_EXPERTS + 1)[None, :]
    pad_rows = q + jnp.sum(jnp.where(in_gap, (gap_start - gap_end_q + gap_len)[None, :], 0), axis=1)
    return (pos.reshape(-1).astype(jnp.int32), pad_rows.astype(jnp.int32),
            tile_expert.astype(jnp.int32), n_used.astype(jnp.int32))


def kernel(x_prompt, x_sample, cache_k, cache_v, w_in, sinks, gm_ln_g, gm_ln_b, gm_ws, gm_bs,
           out_norm_a, out_norm_b, w_o, ln1_g, ln1_b, w_router, router_bias,
           w_gate, w_up, w_down, ln2_g, ln2_b):
    batch, seq, _ = x_prompt.shape
    dec_batch, dec_seq, _ = x_sample.shape
    depth = w_in.shape[0]
    past_len = PAST_LEN
    assert dec_batch * dec_seq == TOK_TILE and dec_seq == CHUNK and seq % TOK_TILE == 0
    assert cache_k.shape[2] == WINDOW
    alpha = (2 * depth) ** 0.25
    tiles_per_seq = seq // TOK_TILE

    tables = _rope_tables(seq, dec_seq, past_len)
    wrt = w_router.T
    rb = router_bias.reshape(N_EXPERTS, 1)

    streams = [
        dict(x=jnp.concatenate([x_prompt.reshape(batch * seq, D_MODEL), x_sample.reshape(-1, D_MODEL)], axis=0),
             n_seq=batch, has_sample=True),
    ]
    kp, vp, ks, vs, gms = [], [], [], [], []
    for l in range(depth):
        layer_args = (
            sinks[l], w_in[l].astype(BF16),
            cache_k[l].reshape(dec_batch, WINDOW, KV_WIDTH), cache_v[l].reshape(dec_batch, WINDOW, KV_WIDTH),
            gm_ln_g[l].reshape(1, GM_WIDTH), gm_ln_b[l].reshape(1, GM_WIDTH),
            gm_ws[l], gm_bs[l].T,
            out_norm_a[l].reshape(1, ATT_WIDTH), out_norm_b[l].reshape(1, GM_WIDTH),
            w_o[l].astype(BF16),
            ln1_g[l].reshape(1, D_MODEL), ln1_b[l].reshape(1, D_MODEL), wrt, rb)
        k_tails, v_tails = [], []
        for st in streams:
            n_prompt = st["n_seq"] * seq
            n_prompt_tiles = n_prompt // TOK_TILE
            t_all = st["x"].shape[0]
            n_exp_tiles = (TOP_K * t_all) // EXPERT_TILE + N_EXPERTS
            res = _mixer_call(layer_args, st["x"], tables, n_prompt_tiles, tiles_per_seq, alpha)
            if st["has_sample"]:
                x1t, x1p, k_all, v_all, vn_s, ridx, rw, counts = res
            else:
                x1t, x1p, k_all, v_all, ridx, rw, counts = res

            pos, pad_rows, tile_expert, n_used = _dispatch_plan(ridx, counts[:, 0], n_exp_tiles)
            xs = _sc_dispatch_call(pos, pad_rows, x1p, PACK_ROWS, n_exp_tiles * EXPERT_TILE)
            outs = _expert_call(tile_expert, n_used, xs, w_gate, w_up, w_down, l)
            y_pair = _sc_gather_call(pos, outs, PACK_ROWS)
            split = n_prompt_tiles if (l == depth - 1 and st["has_sample"]) else None
            st["x"] = _combine_dense_call(x1t, y_pair, rw[:TOP_K].T, ln2_g[l].reshape(1, D_MODEL),
                                          ln2_b[l].reshape(1, D_MODEL), alpha, n_main_tiles=split)

            k_tails.append(k_all[:n_prompt].reshape(st["n_seq"], seq, KV_WIDTH)[:, -WINDOW:])
            v_tails.append(v_all[:n_prompt].reshape(st["n_seq"], seq, KV_WIDTH)[:, -WINDOW:])
            if st["has_sample"]:
                ks.append(k_all[n_prompt:].reshape(dec_batch, dec_seq, N_KV_HEADS, HEAD_DIM))
                vs.append(v_all[n_prompt:].reshape(dec_batch, dec_seq, N_KV_HEADS, HEAD_DIM))
                gms.append(vn_s.reshape(dec_batch, dec_seq, GM_WIDTH))
        kp.append(jnp.concatenate(k_tails, axis=0).reshape(batch, WINDOW, N_KV_HEADS, HEAD_DIM))
        vp.append(jnp.concatenate(v_tails, axis=0).reshape(batch, WINDOW, N_KV_HEADS, HEAD_DIM))

    y_prompt = jnp.concatenate([st["x"][0] if st["has_sample"] else st["x"] for st in streams],
                               axis=0).reshape(batch, seq, D_MODEL)
    y_sample = streams[-1]["x"][1].reshape(dec_batch, dec_seq, D_MODEL)
    return (y_prompt, y_sample, jnp.stack(kp), jnp.stack(vp), jnp.stack(ks), jnp.stack(vs), jnp.stack(gms))
```

```python
import functools

import jax
import jax.numpy as jnp
from jax import lax
from jax.experimental import pallas as pl
from jax.experimental.pallas import tpu as pltpu
from jax.experimental.pallas import tpu_sc as plsc

D_MODEL = 1024
CHUNK = 64
WINDOW = 128
HEAD_DIM = 64
N_Q_HEADS = 8
N_KV_HEADS = 2
Q_PER_KV = N_Q_HEADS // N_KV_HEADS
PAST_LEN = 2048
ATT_WIDTH = N_Q_HEADS * HEAD_DIM
KV_WIDTH = N_KV_HEADS * HEAD_DIM
ROPE_THETA = 10000.0
GM_GROUPS = 4
GM_CH = 128
GM_WIDTH = GM_GROUPS * GM_CH
GM_CHUNK = 128
D_MIX = ATT_WIDTH + GM_WIDTH
D_IN = ATT_WIDTH + 2 * KV_WIDTH + 2 * GM_WIDTH
N_EXPERTS = 16
N_EXPERT_GROUPS = 4
EXPERTS_PER_GROUP = N_EXPERTS // N_EXPERT_GROUPS
TOP_K = 2
D_EXPERT = 512
LN_EPS = 1e-5
NEG_INF = -1e30

LANES = 128
TILE_ROWS = D_MODEL // LANES
PACK_ROWS = TILE_ROWS // 2
HI_HALF = 0xFFFF0000
TOK_TILE = 512
EXPERT_TILE = 512
KEYS_PER_CHUNK = WINDOW + CHUNK
VMEM_LIMIT = 56 * 1024 * 1024

F32 = jnp.float32
BF16 = jnp.bfloat16


def _layer_norm(x, g, b):
    mu = jnp.mean(x, axis=-1, keepdims=True)
    d = x - mu
    var = jnp.mean(d * d, axis=-1, keepdims=True)
    return d * lax.rsqrt(var + LN_EPS) * g + b


def _rms_norm(x, g):
    return x * lax.rsqrt(jnp.mean(x * x, axis=-1, keepdims=True) + LN_EPS) * g


def _gelu(x):
    return 0.5 * x * (1.0 + lax.erf(x * (0.5 ** 0.5)))


def _store_token_tiles(ref, x):
    n = x.shape[0]
    for c in range(TILE_ROWS):
        ref[pl.ds(c, n, stride=TILE_ROWS), :] = x[:, c * LANES:(c + 1) * LANES]


def _load_token_tiles(ref, n, row0=0):
    return jnp.concatenate([ref[pl.ds(row0 + c, n, stride=TILE_ROWS), :] for c in range(TILE_ROWS)], axis=-1)


def _bf16_bits(x):
    return lax.bitcast_convert_type(x.astype(BF16).astype(F32), jnp.uint32)


def _store_packed_tiles(ref, x):
    n = x.shape[0]
    for s in range(PACK_ROWS):
        lo = _bf16_bits(x[:, (2 * s) * LANES:(2 * s + 1) * LANES]) >> 16
        hi = _bf16_bits(x[:, (2 * s + 1) * LANES:(2 * s + 2) * LANES]) & jnp.uint32(HI_HALF)
        ref[pl.ds(s, n, stride=PACK_ROWS), :] = hi | lo


def _load_packed_tiles(ref, n, dtype):
    cols = []
    for s in range(PACK_ROWS):
        w = ref[pl.ds(s, n, stride=PACK_ROWS), :]
        cols.append(lax.bitcast_convert_type(w << 16, F32).astype(dtype))
        cols.append(lax.bitcast_convert_type(w & jnp.uint32(HI_HALF), F32).astype(dtype))
    return jnp.concatenate(cols, axis=-1)


def _first_index_of_max(rows):
    m = rows[0]
    for r in rows[1:]:
        m = jnp.maximum(m, r)
    idx = jnp.full(m.shape, len(rows), jnp.int32)
    for e in reversed(range(len(rows))):
        idx = jnp.where(rows[e] == m, e, idx)
    return m, idx


def _mixer_kernel(n_prompt_tiles, tiles_per_seq, alpha, has_sample,
                  sinks_ref, x_ref, w_in_ref, cos_ref, sa_ref, sb_ref, ck_ref, cv_ref,
                  lng_ref, lnb_ref, ws_ref, bst_ref, na_ref, nb_ref, w_o_ref,
                  l1g_ref, l1b_ref, wrt_ref, rb_ref, x1t_ref, x1p_ref, kt_ref, vt_ref, *rest):
    if has_sample:
        (ks_ref, vs_ref, vn_ref, ridx_ref, rw_ref, counts_ref,
         kd_ref, vd_ref, qm_ref, att_ref, gm_ref, cnt_ref, w_in_s, w_o_s) = rest
    else:
        ridx_ref, rw_ref, counts_ref, kd_ref, vd_ref, qm_ref, att_ref, gm_ref, cnt_ref, w_in_s, w_o_s = rest
    i = pl.program_id(0)
    is_sample = i >= n_prompt_tiles
    seq_start = jnp.logical_and(jnp.logical_not(is_sample), (i % tiles_per_seq) == 0)
    seq_end = jnp.logical_and(jnp.logical_not(is_sample), (i % tiles_per_seq) == tiles_per_seq - 1)
    tm = TOK_TILE

    @pl.when(i == 0)
    def _():
        w_in_s[...] = w_in_ref[0].astype(BF16)
        w_o_s[...] = w_o_ref[0].astype(BF16)

    x = x_ref[...]
    h = jnp.dot(x.astype(BF16), w_in_s[...], preferred_element_type=F32)

    cos = cos_ref[...]
    sa = sa_ref[...]
    sb = sb_ref[...]

    def rope(blk):
        return blk * cos + pltpu.roll(blk, LANES - HEAD_DIM // 2, 1) * sa + pltpu.roll(blk, HEAD_DIM // 2, 1) * sb

    lane = lax.broadcasted_iota(jnp.int32, (1, LANES), 1)
    lo_half = lane < HEAD_DIM

    o_k = ATT_WIDTH
    o_v = o_k + KV_WIDTH
    o_u = o_v + KV_WIDTH
    o_g = o_u + GM_WIDTH

    k_rot = rope(h[:, o_k:o_v])
    v_new = h[:, o_v:o_u]

    @pl.when(seq_end)
    def _():
        kt_ref[...] = k_rot[tm - WINDOW:]
        vt_ref[...] = v_new[tm - WINDOW:]

    if has_sample:
        @pl.when(is_sample)
        def _():
            ks_ref[...] = k_rot
            vs_ref[...] = v_new

    def dup_heads(a):
        sw = pltpu.roll(a, HEAD_DIM, 1)
        return jnp.where(lo_half, a, sw).astype(BF16), jnp.where(lo_half, sw, a).astype(BF16)

    k_d = dup_heads(k_rot)
    v_d = dup_heads(v_new)

    @pl.when(jnp.logical_not(is_sample))
    def _():
        @pl.when(seq_start)
        def _():
            for g in range(N_KV_HEADS):
                kd_ref[g, 0:WINDOW, :] = jnp.zeros((WINDOW, LANES), BF16)
                vd_ref[g, 0:WINDOW, :] = jnp.zeros((WINDOW, LANES), BF16)

        @pl.when(jnp.logical_not(seq_start))
        def _():
            for g in range(N_KV_HEADS):
                kd_ref[g, 0:WINDOW, :] = kd_ref[g, tm:tm + WINDOW, :]
                vd_ref[g, 0:WINDOW, :] = vd_ref[g, tm:tm + WINDOW, :]

        for g in range(N_KV_HEADS):
            kd_ref[g, WINDOW:WINDOW + tm, :] = k_d[g]
            vd_ref[g, WINDOW:WINDOW + tm, :] = v_d[g]

    @pl.when(is_sample)
    def _():
        for b in range(tm // CHUNK):
            ck = dup_heads(ck_ref[b])
            cv = dup_heads(cv_ref[b])
            base = b * KEYS_PER_CHUNK
            for g in range(N_KV_HEADS):
                kd_ref[g, base:base + WINDOW, :] = ck[g]
                vd_ref[g, base:base + WINDOW, :] = cv[g]
                kd_ref[g, base + WINDOW:base + KEYS_PER_CHUNK, :] = k_d[g][b * CHUNK:(b + 1) * CHUNK]
                vd_ref[g, base + WINDOW:base + KEYS_PER_CHUNK, :] = v_d[g][b * CHUNK:(b + 1) * CHUNK]

    scale = HEAD_DIM ** -0.5
    for b in range(ATT_WIDTH // LANES):
        qb = rope(h[:, b * LANES:(b + 1) * LANES]) * scale
        qm_ref[2 * b] = jnp.where(lo_half, qb, 0.0).astype(BF16)
        qm_ref[2 * b + 1] = jnp.where(lo_half, 0.0, qb).astype(BF16)

    key_stride = jnp.where(is_sample, KEYS_PER_CHUNK, CHUNK)
    col = lax.broadcasted_iota(jnp.int32, (1, KEYS_PER_CHUNK), 1)
    row_blk = lax.broadcasted_iota(jnp.int32, (Q_PER_KV * CHUNK, 1), 0) // CHUNK
    sink_cols = [
        jnp.where(row_blk == 0, sinks_ref[Q_PER_KV * g],
                  jnp.where(row_blk == 1, sinks_ref[Q_PER_KV * g + 1],
                            jnp.where(row_blk == 2, sinks_ref[Q_PER_KV * g + 2], sinks_ref[Q_PER_KV * g + 3])))
        for g in range(N_KV_HEADS)]

    def chunk_body(c, carry):
        r0 = pl.multiple_of(c * CHUNK, CHUNK)
        k0 = pl.multiple_of(c * key_stride, CHUNK)
        first_valid = jnp.where(seq_start, (WINDOW // CHUNK - c) * CHUNK, 0)
        valid = col >= first_valid
        outs = []
        for g in range(N_KV_HEADS):
            q_st = jnp.concatenate([qm_ref[Q_PER_KV * g + r, pl.ds(r0, CHUNK), :] for r in range(Q_PER_KV)],
                                   axis=0)
            keys = kd_ref[g, pl.ds(k0, KEYS_PER_CHUNK), :]
            vals = vd_ref[g, pl.ds(k0, KEYS_PER_CHUNK), :]
            s = lax.dot_general(q_st, keys, (((1,), (1,)), ((), ())), preferred_element_type=F32)
            s = jnp.where(valid, s, NEG_INF)
            sink = sink_cols[g]
            m = jnp.maximum(jnp.max(s, axis=-1, keepdims=True), sink)
            p = jnp.exp(s - m)
            denom = jnp.sum(p, axis=-1, keepdims=True) + jnp.exp(sink - m)
            o = jnp.dot(p.astype(BF16), vals, preferred_element_type=F32) * (1.0 / denom)
            for bb in range(2):
                outs.append(jnp.where(lo_half, o[(2 * bb) * CHUNK:(2 * bb + 1) * CHUNK],
                                      o[(2 * bb + 1) * CHUNK:(2 * bb + 2) * CHUNK]))
        for b in range(ATT_WIDTH // LANES):
            att_ref[pl.ds(r0, CHUNK), b * LANES:(b + 1) * LANES] = outs[b]
        return carry

    lax.fori_loop(0, tm // CHUNK, chunk_body, 0, unroll=8)

    prow = lax.broadcasted_iota(jnp.int32, (GM_CHUNK, GM_CHUNK), 0)
    pcol = lax.broadcasted_iota(jnp.int32, (GM_CHUNK, GM_CHUNK), 1)
    half = GM_CHUNK // 2
    tril = pcol <= prow
    same_blk = (prow < half) == (pcol < half)
    prow1 = lax.broadcasted_iota(jnp.int32, (GM_CHUNK, 1), 0)
    bst = bst_ref[...]
    bst_s = jnp.where(prow1 < half, bst, pltpu.roll(bst, half, 0))
    bias = jnp.where(is_sample, bst_s, bst)
    for g in range(GM_GROUPS):
        wg_ = ws_ref[g]
        w_s = jnp.where(prow < half, wg_, pltpu.roll(pltpu.roll(wg_, half, 0), half, 1))
        w_eff = jnp.where(is_sample, jnp.where(same_blk, w_s, 0.0), wg_)
        w_eff = jnp.where(tril, w_eff, 0.0).astype(BF16)
        cols = slice(g * GM_CH, (g + 1) * GM_CH)
        ug = _gelu(h[:, o_u + g * GM_CH:o_u + (g + 1) * GM_CH])
        vn = _layer_norm(_gelu(h[:, o_g + g * GM_CH:o_g + (g + 1) * GM_CH]), lng_ref[:, cols], lnb_ref[:, cols])

        if has_sample:
            @pl.when(is_sample)
            def _(cols=cols, vn=vn):
                vn_ref[:, cols] = vn

        vn_b = vn.astype(BF16)
        for n in range(tm // GM_CHUNK):
            rows = slice(n * GM_CHUNK, (n + 1) * GM_CHUNK)
            s = jnp.dot(w_eff, vn_b[rows], preferred_element_type=F32) + bias[:, g:g + 1]
            gm_ref[rows, cols] = ug[rows] * s

    mixed = jnp.concatenate([_rms_norm(att_ref[...], na_ref[...]), _rms_norm(gm_ref[...], nb_ref[...])], axis=-1)
    y = alpha * x + jnp.dot(mixed.astype(BF16), w_o_s[...], preferred_element_type=F32)
    x1 = _layer_norm(y, l1g_ref[...], l1b_ref[...])
    _store_token_tiles(x1t_ref, x1)
    _store_packed_tiles(x1p_ref, x1)

    logits = lax.dot_general(wrt_ref[...], x1, (((1,), (1,)), ((), ())),
                             preferred_element_type=F32, precision=lax.Precision.HIGHEST)
    scores = jax.nn.sigmoid(logits)
    sel = scores + rb_ref[...]
    sel_rows = [sel[e:e + 1, :] for e in range(N_EXPERTS)]
    sc_rows = [scores[e:e + 1, :] for e in range(N_EXPERTS)]
    grp = []
    for g in range(N_EXPERT_GROUPS):
        r = sel_rows[g * EXPERTS_PER_GROUP:(g + 1) * EXPERTS_PER_GROUP]
        best_pair = None
        for a in range(EXPERTS_PER_GROUP):
            for b in range(a + 1, EXPERTS_PER_GROUP):
                pair = r[a] + r[b]
                best_pair = pair if best_pair is None else jnp.maximum(best_pair, pair)
        grp.append(best_pair)
    _, best = _first_index_of_max(grp)
    masked = [jnp.where(best == (e // EXPERTS_PER_GROUP), sel_rows[e], NEG_INF) for e in range(N_EXPERTS)]
    _, e0 = _first_index_of_max(masked)
    masked2 = [jnp.where(e0 == e, -jnp.inf, masked[e]) for e in range(N_EXPERTS)]
    _, e1 = _first_index_of_max(masked2)
    w0 = jnp.zeros_like(sc_rows[0])
    w1 = jnp.zeros_like(sc_rows[0])
    for e in range(N_EXPERTS):
        w0 = jnp.where(e0 == e, sc_rows[e], w0)
        w1 = jnp.where(e1 == e, sc_rows[e], w1)
    wsum = w0 + w1
    rw_ref[...] = jnp.concatenate([w0 / wsum, w1 / wsum, jnp.zeros((6, tm), F32)], axis=0)

    @pl.when(i == 0)
    def _():
        cnt_ref[...] = jnp.zeros_like(cnt_ref)

    eid = lax.broadcasted_iota(jnp.int32, (N_EXPERTS, tm), 0)
    oh0 = (eid == e0).astype(F32)
    oh1 = (eid == e1).astype(F32)
    oh = oh0 + oh1
    upper = (lax.broadcasted_iota(jnp.int32, (tm, tm), 0) <= lax.broadcasted_iota(jnp.int32, (tm, tm), 1))
    incl = jnp.dot(oh.astype(BF16), upper.astype(BF16), preferred_element_type=F32)
    before = cnt_ref[:, 0:1] + incl - oh
    r0 = jnp.sum(oh0 * before, axis=0, keepdims=True).astype(jnp.int32)
    r1 = jnp.sum(oh1 * before, axis=0, keepdims=True).astype(jnp.int32)
    cnt = cnt_ref[...] + incl[:, tm - 1:tm]
    cnt_ref[...] = cnt
    counts_ref[...] = cnt.astype(jnp.int32)
    ridx_ref[...] = jnp.concatenate([e0, e1, r0, r1, jnp.zeros((4, tm), jnp.int32)], axis=0)


def _mixer_call(layer_args, x_all, tables, n_prompt_tiles, tiles_per_seq, alpha, layer):
    (sinks, w_in, ck, cv, lng, lnb, ws, bst, na, nb, w_o, l1g, l1b, wrt, rb) = layer_args
    cos_t, sa_t, sb_t = tables
    t_all = x_all.shape[0]
    n_tiles = t_all // TOK_TILE
    n_seq_tiles = tiles_per_seq
    has_sample = n_tiles > n_prompt_tiles

    def const(shape):
        nd = len(shape)
        return pl.BlockSpec(shape, lambda i, _nd=nd: (0,) * _nd)

    def tab_map(i):
        return (jnp.where(i < n_prompt_tiles, i % n_seq_tiles, n_seq_tiles), 0)

    n_seq = n_prompt_tiles // n_seq_tiles

    def seq_map(i):
        return (jnp.minimum(i // n_seq_tiles, n_seq - 1), 0)

    row_blk = lambda w: pl.BlockSpec((TOK_TILE, w), lambda i: (i, 0))
    in_specs = [
        pl.BlockSpec(memory_space=pltpu.SMEM),
        row_blk(D_MODEL),
        pl.BlockSpec((1, D_MODEL, D_IN), lambda i: (layer, 0, 0)),
        pl.BlockSpec((TOK_TILE, LANES), tab_map),
        pl.BlockSpec((TOK_TILE, LANES), tab_map),
        pl.BlockSpec((TOK_TILE, LANES), tab_map),
        const(ck.shape), const(cv.shape),
        const((1, GM_WIDTH)), const((1, GM_WIDTH)),
        const((GM_GROUPS, GM_CHUNK, GM_CHUNK)), const((GM_CHUNK, GM_GROUPS)),
        const((1, ATT_WIDTH)), const((1, GM_WIDTH)),
        pl.BlockSpec((1, D_MIX, D_MODEL), lambda i: (layer, 0, 0)),
        const((1, D_MODEL)), const((1, D_MODEL)),
        const((N_EXPERTS, D_MODEL)), const((N_EXPERTS, 1)),
    ]
    out_shape = [
        jax.ShapeDtypeStruct((t_all * TILE_ROWS, LANES), F32),
        jax.ShapeDtypeStruct((t_all * PACK_ROWS, LANES), jnp.uint32),
        jax.ShapeDtypeStruct((n_seq * WINDOW, KV_WIDTH), F32),
        jax.ShapeDtypeStruct((n_seq * WINDOW, KV_WIDTH), F32),
        jax.ShapeDtypeStruct((TOK_TILE, KV_WIDTH), F32),
        jax.ShapeDtypeStruct((TOK_TILE, KV_WIDTH), F32),
        jax.ShapeDtypeStruct((TOK_TILE, GM_WIDTH), F32),
        jax.ShapeDtypeStruct((8, t_all), jnp.int32),
        jax.ShapeDtypeStruct((8, t_all), F32),
        jax.ShapeDtypeStruct((N_EXPERTS, LANES), jnp.int32),
    ]
    out_specs = [
        pl.BlockSpec((TOK_TILE * TILE_ROWS, LANES), lambda i: (i, 0)),
        pl.BlockSpec((TOK_TILE * PACK_ROWS, LANES), lambda i: (i, 0)),
        pl.BlockSpec((WINDOW, KV_WIDTH), seq_map), pl.BlockSpec((WINDOW, KV_WIDTH), seq_map),
        const((TOK_TILE, KV_WIDTH)), const((TOK_TILE, KV_WIDTH)),
        const((TOK_TILE, GM_WIDTH)),
        pl.BlockSpec((8, TOK_TILE), lambda i: (0, i)),
        pl.BlockSpec((8, TOK_TILE), lambda i: (0, i)),
        const((N_EXPERTS, LANES)),
    ]
    if not has_sample:
        del out_shape[4:7], out_specs[4:7]
    kd_rows = (TOK_TILE // CHUNK) * KEYS_PER_CHUNK
    scratch = [
        pltpu.VMEM((N_KV_HEADS, kd_rows, LANES), BF16),
        pltpu.VMEM((N_KV_HEADS, kd_rows, LANES), BF16),
        pltpu.VMEM((N_Q_HEADS, TOK_TILE, LANES), BF16),
        pltpu.VMEM((TOK_TILE, ATT_WIDTH), F32),
        pltpu.VMEM((TOK_TILE, GM_WIDTH), F32),
        pltpu.VMEM((N_EXPERTS, LANES), F32),
        pltpu.VMEM((D_MODEL, D_IN), BF16),
        pltpu.VMEM((D_MIX, D_MODEL), BF16),
    ]
    return pl.pallas_call(
        functools.partial(_mixer_kernel, n_prompt_tiles, tiles_per_seq, alpha, has_sample),
        grid=(n_tiles,),
        in_specs=in_specs, out_specs=out_specs, out_shape=out_shape,
        scratch_shapes=scratch,
        compiler_params=pltpu.CompilerParams(dimension_semantics=("arbitrary",), vmem_limit_bytes=VMEM_LIMIT),
        name="mixer",
    )(sinks, x_all, w_in, cos_t, sa_t, sb_t, ck, cv, lng, lnb, ws, bst, na, nb, w_o, l1g, l1b, wrt, rb)


def _expert_kernel(te_ref, nt_ref, xs_ref, wg_ref, wu_ref, wd_ref, out_ref, wg_s, wu_s, wd_s):
    i = pl.program_id(0)
    changed = jnp.logical_or(i == 0, te_ref[i] != te_ref[jnp.maximum(i - 1, 0)])

    @pl.when(changed)
    def _():
        wg_s[...] = wg_ref[0, 0].astype(BF16)
        wu_s[...] = wu_ref[0, 0].astype(BF16)
        wd_s[...] = wd_ref[0, 0].astype(BF16)

    @pl.when(i < nt_ref[0])
    def _():
        xs = _load_packed_tiles(xs_ref, EXPERT_TILE, BF16)
        g = jnp.dot(xs, wg_s[...], preferred_element_type=F32)
        u = jnp.dot(xs, wu_s[...], preferred_element_type=F32)
        hmid = (g * jax.nn.sigmoid(g)) * u
        _store_packed_tiles(out_ref, jnp.dot(hmid.astype(BF16), wd_s[...], preferred_element_type=F32))

    @pl.when(i >= nt_ref[0])
    def _():
        out_ref[...] = jnp.zeros_like(out_ref)


def _expert_call(tile_expert, n_used, xs, wg, wu, wd, layer):
    n_tiles = xs.shape[0] // (EXPERT_TILE * PACK_ROWS)
    tile_blk = pl.BlockSpec((EXPERT_TILE * PACK_ROWS, LANES), lambda i, te, nt: (i, 0))
    grid_spec = pltpu.PrefetchScalarGridSpec(
        num_scalar_prefetch=2,
        grid=(n_tiles,),
        in_specs=[
            tile_blk,
            pl.BlockSpec((1, 1, D_MODEL, D_EXPERT), lambda i, te, nt: (layer, te[i], 0, 0)),
            pl.BlockSpec((1, 1, D_MODEL, D_EXPERT), lambda i, te, nt: (layer, te[i], 0, 0)),
            pl.BlockSpec((1, 1, D_EXPERT, D_MODEL), lambda i, te, nt: (layer, te[i], 0, 0)),
        ],
        out_specs=tile_blk,
        scratch_shapes=[
            pltpu.VMEM((D_MODEL, D_EXPERT), BF16),
            pltpu.VMEM((D_MODEL, D_EXPERT), BF16),
            pltpu.VMEM((D_EXPERT, D_MODEL), BF16),
        ],
    )
    return pl.pallas_call(
        _expert_kernel,
        grid_spec=grid_spec,
        out_shape=jax.ShapeDtypeStruct(xs.shape, jnp.uint32),
        compiler_params=pltpu.CompilerParams(dimension_semantics=("arbitrary",), vmem_limit_bytes=VMEM_LIMIT),
        name="experts",
    )(tile_expert, n_used, xs, wg, wu, wd)


ROWS_PER_STEP = TOK_TILE * TOP_K


def _token_tile(ref, idx):
    return ref.at[pl.ds(pl.multiple_of(idx * TILE_ROWS, TILE_ROWS), TILE_ROWS)]


def _dispatch_kernel(n_tok_steps, t_all, pos_ref, pad_ref, x1t_ref, xs_hbm, zero_ref, sems):
    j = pl.program_id(0)

    @pl.when(j == 0)
    def _():
        zero_ref[...] = jnp.zeros_like(zero_ref)

    @pl.when(j < n_tok_steps)
    def _():
        def body(tt, c):
            t = j * TOK_TILE + tt
            for k in range(TOP_K):
                pltpu.make_async_copy(_token_tile(x1t_ref, tt), _token_tile(xs_hbm, pos_ref[k * t_all + t]),
                                      sems.at[k]).start()
            return c
        lax.fori_loop(0, TOK_TILE, body, 0, unroll=8)

    @pl.when(j >= n_tok_steps)
    def _():
        def body(q, c):
            for k in range(TOP_K):
                row = pad_ref[(j - n_tok_steps) * ROWS_PER_STEP + k * TOK_TILE + q]
                pltpu.make_async_copy(zero_ref, _token_tile(xs_hbm, row), sems.at[k]).start()
            return c
        lax.fori_loop(0, TOK_TILE, body, 0, unroll=8)

    for k in range(TOP_K):
        pltpu.make_async_copy(x1t_ref, xs_hbm.at[pl.ds(0, TOK_TILE * TILE_ROWS)], sems.at[k]).wait()


def _dispatch_call(pos_flat, pad_rows, x1t, n_sorted_rows):
    t_all = x1t.shape[0] // TILE_ROWS
    n_tok_steps = t_all // TOK_TILE
    n_steps = n_tok_steps + pad_rows.shape[0] // ROWS_PER_STEP
    grid_spec = pltpu.PrefetchScalarGridSpec(
        num_scalar_prefetch=2,
        grid=(n_steps,),
        in_specs=[pl.BlockSpec((TOK_TILE * TILE_ROWS, LANES),
                               lambda j, p, q: (jnp.minimum(j, n_tok_steps - 1), 0))],
        out_specs=pl.BlockSpec(memory_space=pl.ANY),
        scratch_shapes=[pltpu.VMEM((TILE_ROWS, LANES), F32), pltpu.SemaphoreType.DMA((TOP_K,))],
    )
    return pl.pallas_call(
        functools.partial(_dispatch_kernel, n_tok_steps, t_all),
        grid_spec=grid_spec,
        out_shape=jax.ShapeDtypeStruct((n_sorted_rows * TILE_ROWS, LANES), F32),
        compiler_params=pltpu.CompilerParams(dimension_semantics=("arbitrary",)),
        name="dispatch",
    )(pos_flat, pad_rows, x1t)


def _combine_kernel(alpha, t_all, n_main_tiles, pos_ref, x1t_ref, w_ref, g_ref, b_ref, outs_hbm, *rest):
    if n_main_tiles is None:
        out_ref, gbuf, sems = rest
        tail_ref = None
    else:
        out_ref, tail_ref, gbuf, sems = rest
    j = pl.program_id(0)
    slot = j % 2
    k_rows = TOK_TILE * TILE_ROWS

    n = pl.num_programs(0)

    def wait(s):
        pltpu.make_async_copy(outs_hbm.at[pl.ds(0, TOP_K * k_rows)], gbuf.at[s], sems.at[s]).wait()

    def issue(step, s):
        def body(tt, c):
            t = step * TOK_TILE + tt
            for k in range(TOP_K):
                pltpu.make_async_copy(_token_tile(outs_hbm, pos_ref[k * t_all + t]),
                                      _token_tile(gbuf.at[s], k * TOK_TILE + tt), sems.at[s]).start()
            return c
        lax.fori_loop(0, TOK_TILE, body, 0, unroll=8)

    @pl.when(j == 0)
    def _():
        issue(0, 0)

    @pl.when(j + 1 < n)
    def _():
        issue(j + 1, 1 - slot)

    wait(slot)

    w = w_ref[...]
    y = (w[:, 0:1] * _load_token_tiles(gbuf.at[slot], TOK_TILE)
         + w[:, 1:2] * _load_token_tiles(gbuf.at[slot], TOK_TILE, k_rows))
    x1 = _load_token_tiles(x1t_ref, TOK_TILE)
    res = _layer_norm(alpha * x1 + y, g_ref[...], b_ref[...])
    if tail_ref is None:
        out_ref[...] = res
    else:
        @pl.when(j < n_main_tiles)
        def _():
            out_ref[...] = res

        @pl.when(j >= n_main_tiles)
        def _():
            tail_ref[...] = res


def _combine_call(pos_flat, x1t, w_col, g, b, outs, alpha, n_main_tiles=None):
    t_all = x1t.shape[0] // TILE_ROWS
    n_tiles = t_all // TOK_TILE
    if n_main_tiles is None:
        out_specs = pl.BlockSpec((TOK_TILE, D_MODEL), lambda i, p: (i, 0))
        out_shape = jax.ShapeDtypeStruct((t_all, D_MODEL), F32)
    else:
        assert n_tiles == n_main_tiles + 1
        out_specs = [pl.BlockSpec((TOK_TILE, D_MODEL), lambda i, p: (jnp.minimum(i, n_main_tiles - 1), 0)),
                     pl.BlockSpec((TOK_TILE, D_MODEL), lambda i, p: (0, 0))]
        out_shape = [jax.ShapeDtypeStruct((n_main_tiles * TOK_TILE, D_MODEL), F32),
                     jax.ShapeDtypeStruct((TOK_TILE, D_MODEL), F32)]
    grid_spec = pltpu.PrefetchScalarGridSpec(
        num_scalar_prefetch=1,
        grid=(t_all // TOK_TILE,),
        in_specs=[
            pl.BlockSpec((TOK_TILE * TILE_ROWS, LANES), lambda i, p: (i, 0)),
            pl.BlockSpec((TOK_TILE, TOP_K), lambda i, p: (i, 0)),
            pl.BlockSpec((1, D_MODEL), lambda i, p: (0, 0)),
            pl.BlockSpec((1, D_MODEL), lambda i, p: (0, 0)),
            pl.BlockSpec(memory_space=pl.ANY),
        ],
        out_specs=out_specs,
        scratch_shapes=[pltpu.VMEM((2, TOP_K * TOK_TILE * TILE_ROWS, LANES), F32), pltpu.SemaphoreType.DMA((2,))],
    )
    return pl.pallas_call(
        functools.partial(_combine_kernel, alpha, t_all, n_main_tiles),
        grid_spec=grid_spec,
        out_shape=out_shape,
        compiler_params=pltpu.CompilerParams(dimension_semantics=("arbitrary",), vmem_limit_bytes=VMEM_LIMIT),
        name="combine",
    )(pos_flat, x1t, w_col, g, b, outs)


SC_CHUNK = 96
SC_PAD_CHUNK = 32


def _sc_workers():
    info = plsc.get_sparse_core_info()
    return info.num_cores, info.num_cores * info.num_subcores


def _sc_split(per_w):
    n_full = per_w // SC_CHUNK
    tail = per_w - n_full * SC_CHUNK
    assert tail % 8 == 0
    return n_full, tail


def _sc_dispatch_call(pos_flat, pad_rows, x_tiles, tile_rows, n_sorted_rows):
    t_all = x_tiles.shape[0] // tile_rows
    dt = x_tiles.dtype
    nc, nw = _sc_workers()
    per_w = t_all // nw
    n_ch, tail = _sc_split(per_w)
    n_pad = pad_rows.shape[0]
    pad_per_w = n_pad // nw
    n_pch = pad_per_w // SC_PAD_CHUNK
    assert per_w * nw == t_all and n_pch * SC_PAD_CHUNK * nw == n_pad
    zeros = jnp.zeros((SC_PAD_CHUNK, tile_rows, LANES), dt)
    tail_scratch = [] if tail == 0 else [
        pltpu.VMEM((tail,), jnp.int32), pltpu.VMEM((tail,), jnp.int32), pltpu.VMEM((tail, tile_rows, LANES), dt)]

    @functools.partial(
        pl.kernel, mesh=plsc.VectorSubcoreMesh(core_axis_name="c", subcore_axis_name="s"),
        out_type=jax.ShapeDtypeStruct((n_sorted_rows, tile_rows, LANES), dt),
        scratch_types=[
            pltpu.VMEM((SC_CHUNK,), jnp.int32), pltpu.VMEM((SC_CHUNK,), jnp.int32),
            pltpu.VMEM((SC_CHUNK, tile_rows, LANES), dt),
            pltpu.VMEM((SC_PAD_CHUNK,), jnp.int32),
            pltpu.VMEM((SC_PAD_CHUNK, tile_rows, LANES), dt),
        ] + tail_scratch,
    )
    def k(pos_hbm, pad_hbm, x_hbm, z_hbm, xs_hbm, i0_v, i1_v, rows_v, ip_v, z_v, *tail_refs):
        wid = lax.axis_index("s") * nc + lax.axis_index("c")

        def move(b, n, idx0, idx1, rows):
            pltpu.sync_copy(pos_hbm.at[pl.ds(b, n)], idx0)
            pltpu.sync_copy(pos_hbm.at[pl.ds(t_all + b, n)], idx1)
            pltpu.sync_copy(x_hbm.at[pl.ds(b, n)], rows)
            pltpu.sync_copy(rows, xs_hbm.at[idx0])
            pltpu.sync_copy(rows, xs_hbm.at[idx1])

        @pl.loop(0, n_ch)
        def _(j):
            move(pl.multiple_of(wid * per_w + j * SC_CHUNK, 8), SC_CHUNK, i0_v, i1_v, rows_v)

        if tail:
            move(pl.multiple_of(wid * per_w + n_ch * SC_CHUNK, 8), tail, *tail_refs)

        pltpu.sync_copy(z_hbm, z_v)

        @pl.loop(0, n_pch)
        def _(j):
            b = pl.multiple_of(wid * pad_per_w + j * SC_PAD_CHUNK, 8)
            pltpu.sync_copy(pad_hbm.at[pl.ds(b, SC_PAD_CHUNK)], ip_v)
            pltpu.sync_copy(z_v, xs_hbm.at[ip_v])

    xs = k(pos_flat, pad_rows, x_tiles.reshape(t_all, tile_rows, LANES), zeros)
    return xs.reshape(n_sorted_rows * tile_rows, LANES)


def _sc_gather_call(pos_flat, outs, tile_rows):
    n_assign = pos_flat.shape[0]
    dt = outs.dtype
    nc, nw = _sc_workers()
    per_w = n_assign // nw
    n_ch, tail = _sc_split(per_w)
    assert per_w * nw == n_assign
    tail_scratch = [] if tail == 0 else [pltpu.VMEM((tail,), jnp.int32), pltpu.VMEM((tail, tile_rows, LANES), dt)]

    @functools.partial(
        pl.kernel, mesh=plsc.VectorSubcoreMesh(core_axis_name="c", subcore_axis_name="s"),
        out_type=jax.ShapeDtypeStruct((n_assign, tile_rows, LANES), dt),
        scratch_types=[pltpu.VMEM((SC_CHUNK,), jnp.int32),
                       pltpu.VMEM((SC_CHUNK, tile_rows, LANES), dt)] + tail_scratch,
    )
    def k(pos_hbm, o_hbm, y_hbm, i_v, rows_v, *tail_refs):
        wid = lax.axis_index("s") * nc + lax.axis_index("c")

        def move(b, n, idx, rows):
            pltpu.sync_copy(pos_hbm.at[pl.ds(b, n)], idx)
            pltpu.sync_copy(o_hbm.at[idx], rows)
            pltpu.sync_copy(rows, y_hbm.at[pl.ds(b, n)])

        @pl.loop(0, n_ch)
        def _(j):
            move(pl.multiple_of(wid * per_w + j * SC_CHUNK, 8), SC_CHUNK, i_v, rows_v)

        if tail:
            move(pl.multiple_of(wid * per_w + n_ch * SC_CHUNK, 8), tail, *tail_refs)

    y = k(pos_flat, outs.reshape(-1, tile_rows, LANES))
    return y.reshape(n_assign * tile_rows, LANES)


def _combine_dense_kernel(alpha, n_main_tiles, x1t_ref, y0_ref, y1_ref, w_ref, g_ref, b_ref, out_ref, *tail):
    j = pl.program_id(0)
    w = w_ref[...]
    y = (w[:, 0:1] * _load_packed_tiles(y0_ref, TOK_TILE, F32)
         + w[:, 1:2] * _load_packed_tiles(y1_ref, TOK_TILE, F32))
    res = _layer_norm(alpha * _load_token_tiles(x1t_ref, TOK_TILE) + y, g_ref[...], b_ref[...])
    if n_main_tiles is None:
        out_ref[...] = res
    else:
        @pl.when(j < n_main_tiles)
        def _():
            out_ref[...] = res

        @pl.when(j >= n_main_tiles)
        def _():
            tail[0][...] = res


def _combine_dense_call(x1t, y_pair, w_col, g, b, alpha, n_main_tiles=None):
    t_all = x1t.shape[0] // TILE_ROWS
    n_tiles = t_all // TOK_TILE
    if n_main_tiles is None:
        out_specs = pl.BlockSpec((TOK_TILE, D_MODEL), lambda i: (i, 0))
        out_shape = jax.ShapeDtypeStruct((t_all, D_MODEL), F32)
    else:
        assert n_tiles == n_main_tiles + 1
        out_specs = [pl.BlockSpec((TOK_TILE, D_MODEL), lambda i: (jnp.minimum(i, n_main_tiles - 1), 0)),
                     pl.BlockSpec((TOK_TILE, D_MODEL), lambda i: (0, 0))]
        out_shape = [jax.ShapeDtypeStruct((n_main_tiles * TOK_TILE, D_MODEL), F32),
                     jax.ShapeDtypeStruct((TOK_TILE, D_MODEL), F32)]
    tile = (TOK_TILE * TILE_ROWS, LANES)
    packed = (TOK_TILE * PACK_ROWS, LANES)
    return pl.pallas_call(
        functools.partial(_combine_dense_kernel, alpha, n_main_tiles),
        grid=(n_tiles,),
        in_specs=[
            pl.BlockSpec(tile, lambda i: (i, 0)),
            pl.BlockSpec(packed, lambda i: (i, 0)),
            pl.BlockSpec(packed, lambda i: (i + n_tiles, 0)),
            pl.BlockSpec((TOK_TILE, TOP_K), lambda i: (i, 0)),
            pl.BlockSpec((1, D_MODEL), lambda i: (0, 0)),
            pl.BlockSpec((1, D_MODEL), lambda i: (0, 0)),
        ],
        out_specs=out_specs,
        out_shape=out_shape,
        compiler_params=pltpu.CompilerParams(dimension_semantics=("arbitrary",), vmem_limit_bytes=VMEM_LIMIT),
        name="combine",
    )(x1t, y_pair, y_pair, w_col, g, b)


def _rope_tables(seq, dec_seq, past_len):
    half = HEAD_DIM // 2
    inv = ROPE_THETA ** (-jnp.arange(half, dtype=F32) / half)
    pos = jnp.concatenate([jnp.arange(seq), past_len + (jnp.arange(TOK_TILE) % dec_seq)])
    ang = pos.astype(F32)[:, None] * inv[None, :]
    cos = jnp.cos(ang)
    sin = jnp.sin(ang)
    zero = jnp.zeros_like(sin)
    reps = LANES // HEAD_DIM
    cos_t = jnp.tile(jnp.concatenate([cos, cos], -1), (1, reps))
    sa_t = jnp.tile(jnp.concatenate([-sin, zero], -1), (1, reps))
    sb_t = jnp.tile(jnp.concatenate([zero, sin], -1), (1, reps))
    return cos_t, sa_t, sb_t


def _inclusive_cumsum(v):
    n = v.shape[0]
    tri = jnp.arange(n)[None, :] <= jnp.arange(n)[:, None]
    return jnp.sum(jnp.where(tri, v[None, :], 0), axis=1)


def _dispatch_plan(ridx, counts, n_tiles):
    t_all = ridx.shape[1]
    tiles_e = (counts + EXPERT_TILE - 1) // EXPERT_TILE
    tile_end = _inclusive_cumsum(tiles_e)
    offs = (tile_end - tiles_e) * EXPERT_TILE
    n_used = tile_end[-1:]
    tile_expert = jnp.minimum(
        jnp.sum((jnp.arange(n_tiles)[:, None] >= tile_end[None, :]).astype(jnp.int32), axis=1), N_EXPERTS - 1)
    experts = jnp.arange(N_EXPERTS)
    pos = ridx[TOP_K:2 * TOP_K] + jnp.sum(
        jnp.where(ridx[:TOP_K, :, None] == experts[None, None, :], offs[None, None, :], 0), axis=-1)
    gap_start = jnp.concatenate([offs + counts, n_used * EXPERT_TILE])
    gap_len = jnp.concatenate([tiles_e * EXPERT_TILE - counts, (n_tiles - n_used) * EXPERT_TILE])
    gap_end_q = _inclusive_cumsum(gap_len)
    q = jnp.arange(n_tiles * EXPERT_TILE - TOP_K * t_all)
    gap = jnp.sum((q[:, None] >= gap_end_q[None, :]).astype(jnp.int32), axis=1)
    in_gap = gap[:, None] == jnp.arange(N_EXPERTS + 1)[None, :]
    pad_rows = q + jnp.sum(jnp.where(in_gap, (gap_start - gap_end_q + gap_len)[None, :], 0), axis=1)
    return (pos.reshape(-1).astype(jnp.int32), pad_rows.astype(jnp.int32),
            tile_expert.astype(jnp.int32), n_used.astype(jnp.int32))


def kernel(x_prompt, x_sample, cache_k, cache_v, w_in, sinks, gm_ln_g, gm_ln_b, gm_ws, gm_bs,
           out_norm_a, out_norm_b, w_o, ln1_g, ln1_b, w_router, router_bias,
           w_gate, w_up, w_down, ln2_g, ln2_b):
    batch, seq, _ = x_prompt.shape
    dec_batch, dec_seq, _ = x_sample.shape
    depth = w_in.shape[0]
    past_len = PAST_LEN
    assert dec_batch * dec_seq == TOK_TILE and dec_seq == CHUNK and seq % TOK_TILE == 0
    assert cache_k.shape[2] == WINDOW
    alpha = (2 * depth) ** 0.25
    tiles_per_seq = seq // TOK_TILE

    tables = _rope_tables(seq, dec_seq, past_len)
    wrt = w_router.T
    rb = router_bias.reshape(N_EXPERTS, 1)

    streams = [
        dict(x=jnp.concatenate([x_prompt.reshape(batch * seq, D_MODEL), x_sample.reshape(-1, D_MODEL)], axis=0),
             n_seq=batch, has_sample=True),
    ]
    kp, vp, ks, vs, gms = [], [], [], [], []
    for l in range(depth):
        layer_args = (
            sinks[l], w_in,
            cache_k[l].reshape(dec_batch, WINDOW, KV_WIDTH), cache_v[l].reshape(dec_batch, WINDOW, KV_WIDTH),
            gm_ln_g[l].reshape(1, GM_WIDTH), gm_ln_b[l].reshape(1, GM_WIDTH),
            gm_ws[l], gm_bs[l].T,
            out_norm_a[l].reshape(1, ATT_WIDTH), out_norm_b[l].reshape(1, GM_WIDTH),
            w_o,
            ln1_g[l].reshape(1, D_MODEL), ln1_b[l].reshape(1, D_MODEL), wrt, rb)
        k_tails, v_tails = [], []
        for st in streams:
            n_prompt = st["n_seq"] * seq
            n_prompt_tiles = n_prompt // TOK_TILE
            t_all = st["x"].shape[0]
            n_exp_tiles = (TOP_K * t_all) // EXPERT_TILE + N_EXPERTS
            res = _mixer_call(layer_args, st["x"], tables, n_prompt_tiles, tiles_per_seq, alpha, l)
            if st["has_sample"]:
                x1t, x1p, k_tail, v_tail, k_s, v_s, vn_s, ridx, rw, counts = res
            else:
                x1t, x1p, k_tail, v_tail, ridx, rw, counts = res

            pos, pad_rows, tile_expert, n_used = _dispatch_plan(ridx, counts[:, 0], n_exp_tiles)
            xs = _sc_dispatch_call(pos, pad_rows, x1p, PACK_ROWS, n_exp_tiles * EXPERT_TILE)
            outs = _expert_call(tile_expert, n_used, xs, w_gate, w_up, w_down, l)
            y_pair = _sc_gather_call(pos, outs, PACK_ROWS)
            split = n_prompt_tiles if (l == depth - 1 and st["has_sample"]) else None
            st["x"] = _combine_dense_call(x1t, y_pair, rw[:TOP_K].T, ln2_g[l].reshape(1, D_MODEL),
                                          ln2_b[l].reshape(1, D_MODEL), alpha, n_main_tiles=split)

            k_tails.append(k_tail)
            v_tails.append(v_tail)
            if st["has_sample"]:
                ks.append(k_s.reshape(dec_batch, dec_seq, N_KV_HEADS, HEAD_DIM))
                vs.append(v_s.reshape(dec_batch, dec_seq, N_KV_HEADS, HEAD_DIM))
                gms.append(vn_s.reshape(dec_batch, dec_seq, GM_WIDTH))
        kp.append(jnp.concatenate(k_tails, axis=0).reshape(batch, WINDOW, N_KV_HEADS, HEAD_DIM))
        vp.append(jnp.concatenate(v_tails, axis=0).reshape(batch, WINDOW, N_KV_HEADS, HEAD_DIM))

    y_prompt = jnp.concatenate([st["x"][0] if st["has_sample"] else st["x"] for st in streams],
                               axis=0).reshape(batch, seq, D_MODEL)
    y_sample = streams[-1]["x"][1].reshape(dec_batch, dec_seq, D_MODEL)
    return (y_prompt, y_sample, jnp.stack(kp), jnp.stack(vp), jnp.stack(ks), jnp.stack(vs), jnp.stack(gms))
```

```python
import functools

import jax
import jax.numpy as jnp
from jax import lax
from jax.experimental import pallas as pl
from jax.experimental.pallas import tpu as pltpu
from jax.experimental.pallas import tpu_sc as plsc

D_MODEL = 1024
CHUNK = 64
WINDOW = 128
HEAD_DIM = 64
N_Q_HEADS = 8
N_KV_HEADS = 2
Q_PER_KV = N_Q_HEADS // N_KV_HEADS
PAST_LEN = 2048
ATT_WIDTH = N_Q_HEADS * HEAD_DIM
KV_WIDTH = N_KV_HEADS * HEAD_DIM
ROPE_THETA = 10000.0
GM_GROUPS = 4
GM_CH = 128
GM_WIDTH = GM_GROUPS * GM_CH
GM_CHUNK = 128
D_MIX = ATT_WIDTH + GM_WIDTH
D_IN = ATT_WIDTH + 2 * KV_WIDTH + 2 * GM_WIDTH
N_EXPERTS = 16
N_EXPERT_GROUPS = 4
EXPERTS_PER_GROUP = N_EXPERTS // N_EXPERT_GROUPS
TOP_K = 2
D_EXPERT = 512
LN_EPS = 1e-5
NEG_INF = -1e30

LANES = 128
TILE_ROWS = D_MODEL // LANES
PACK_ROWS = TILE_ROWS // 2
HI_HALF = 0xFFFF0000
TOK_TILE = 512
EXPERT_TILE = 512
KEYS_PER_CHUNK = WINDOW + CHUNK
VMEM_LIMIT = 60 * 1024 * 1024

F32 = jnp.float32
BF16 = jnp.bfloat16


def _layer_norm(x, g, b):
    mu = jnp.mean(x, axis=-1, keepdims=True)
    d = x - mu
    var = jnp.mean(d * d, axis=-1, keepdims=True)
    return d * lax.rsqrt(var + LN_EPS) * g + b


def _rms_norm(x, g):
    return x * lax.rsqrt(jnp.mean(x * x, axis=-1, keepdims=True) + LN_EPS) * g


def _gelu(x):
    return 0.5 * x * (1.0 + lax.erf(x * (0.5 ** 0.5)))


def _store_token_tiles(ref, x):
    n = x.shape[0]
    for c in range(TILE_ROWS):
        ref[pl.ds(c, n, stride=TILE_ROWS), :] = x[:, c * LANES:(c + 1) * LANES]


def _load_token_tiles(ref, n, row0=0):
    return jnp.concatenate([ref[pl.ds(row0 + c, n, stride=TILE_ROWS), :] for c in range(TILE_ROWS)], axis=-1)


def _bf16_bits(x):
    return lax.bitcast_convert_type(x.astype(BF16).astype(F32), jnp.uint32)


def _store_packed_tiles(ref, x):
    n = x.shape[0]
    for s in range(PACK_ROWS):
        lo = _bf16_bits(x[:, (2 * s) * LANES:(2 * s + 1) * LANES]) >> 16
        hi = _bf16_bits(x[:, (2 * s + 1) * LANES:(2 * s + 2) * LANES]) & jnp.uint32(HI_HALF)
        ref[pl.ds(s, n, stride=PACK_ROWS), :] = hi | lo


def _load_packed_tiles(ref, n, dtype):
    cols = []
    for s in range(PACK_ROWS):
        w = ref[pl.ds(s, n, stride=PACK_ROWS), :]
        cols.append(lax.bitcast_convert_type(w << 16, F32).astype(dtype))
        cols.append(lax.bitcast_convert_type(w & jnp.uint32(HI_HALF), F32).astype(dtype))
    return jnp.concatenate(cols, axis=-1)


def _first_index_of_max(rows):
    m = rows[0]
    for r in rows[1:]:
        m = jnp.maximum(m, r)
    idx = jnp.full(m.shape, len(rows), jnp.int32)
    for e in reversed(range(len(rows))):
        idx = jnp.where(rows[e] == m, e, idx)
    return m, idx


def _mixer_kernel(n_prompt_tiles, tiles_per_seq, alpha, has_sample, split_x,
                  sinks_ref, x_ref, xs_ref, w_in_ref, cos_ref, sa_ref, sb_ref, ck_ref, cv_ref,
                  lng_ref, lnb_ref, ws_ref, bst_ref, na_ref, nb_ref, w_o_ref,
                  l1g_ref, l1b_ref, wrt_ref, rb_ref, x1t_ref, x1p_ref, kt_ref, vt_ref, *rest):
    if has_sample:
        (ks_ref, vs_ref, vn_ref, ridx_ref, rw_ref, counts_ref,
         kd_ref, vd_ref, qm_ref, att_ref, gm_ref, cnt_ref, w_in_s, w_o_s) = rest
    else:
        ridx_ref, rw_ref, counts_ref, kd_ref, vd_ref, qm_ref, att_ref, gm_ref, cnt_ref, w_in_s, w_o_s = rest
    i = pl.program_id(0)
    is_sample = i >= n_prompt_tiles
    seq_start = jnp.logical_and(jnp.logical_not(is_sample), (i % tiles_per_seq) == 0)
    seq_end = jnp.logical_and(jnp.logical_not(is_sample), (i % tiles_per_seq) == tiles_per_seq - 1)
    tm = TOK_TILE

    @pl.when(i == 0)
    def _():
        w_in_s[...] = w_in_ref[0].astype(BF16)
        w_o_s[...] = w_o_ref[0].astype(BF16)

    def load_x():
        return jnp.where(is_sample, xs_ref[...], x_ref[...]) if split_x else x_ref[...]

    h = jnp.dot(load_x().astype(BF16), w_in_s[...], preferred_element_type=F32)

    cos = cos_ref[...]
    sa = sa_ref[...]
    sb = sb_ref[...]

    def rope(blk):
        return blk * cos + pltpu.roll(blk, LANES - HEAD_DIM // 2, 1) * sa + pltpu.roll(blk, HEAD_DIM // 2, 1) * sb

    lane = lax.broadcasted_iota(jnp.int32, (1, LANES), 1)
    lo_half = lane < HEAD_DIM

    o_k = ATT_WIDTH
    o_v = o_k + KV_WIDTH
    o_u = o_v + KV_WIDTH
    o_g = o_u + GM_WIDTH

    k_rot = rope(h[:, o_k:o_v])
    v_new = h[:, o_v:o_u]

    @pl.when(seq_end)
    def _():
        kt_ref[...] = k_rot[tm - WINDOW:]
        vt_ref[...] = v_new[tm - WINDOW:]

    if has_sample:
        @pl.when(is_sample)
        def _():
            ks_ref[...] = k_rot
            vs_ref[...] = v_new

    def dup_heads(a):
        sw = pltpu.roll(a, HEAD_DIM, 1)
        return jnp.where(lo_half, a, sw).astype(BF16), jnp.where(lo_half, sw, a).astype(BF16)

    k_d = dup_heads(k_rot)
    v_d = dup_heads(v_new)

    @pl.when(jnp.logical_not(is_sample))
    def _():
        @pl.when(seq_start)
        def _():
            for g in range(N_KV_HEADS):
                kd_ref[g, 0:WINDOW, :] = jnp.zeros((WINDOW, LANES), BF16)
                vd_ref[g, 0:WINDOW, :] = jnp.zeros((WINDOW, LANES), BF16)

        @pl.when(jnp.logical_not(seq_start))
        def _():
            for g in range(N_KV_HEADS):
                kd_ref[g, 0:WINDOW, :] = kd_ref[g, tm:tm + WINDOW, :]
                vd_ref[g, 0:WINDOW, :] = vd_ref[g, tm:tm + WINDOW, :]

        for g in range(N_KV_HEADS):
            kd_ref[g, WINDOW:WINDOW + tm, :] = k_d[g]
            vd_ref[g, WINDOW:WINDOW + tm, :] = v_d[g]

    @pl.when(is_sample)
    def _():
        for b in range(tm // CHUNK):
            ck = dup_heads(ck_ref[b])
            cv = dup_heads(cv_ref[b])
            base = b * KEYS_PER_CHUNK
            for g in range(N_KV_HEADS):
                kd_ref[g, base:base + WINDOW, :] = ck[g]
                vd_ref[g, base:base + WINDOW, :] = cv[g]
                kd_ref[g, base + WINDOW:base + KEYS_PER_CHUNK, :] = k_d[g][b * CHUNK:(b + 1) * CHUNK]
                vd_ref[g, base + WINDOW:base + KEYS_PER_CHUNK, :] = v_d[g][b * CHUNK:(b + 1) * CHUNK]

    scale = HEAD_DIM ** -0.5
    for b in range(ATT_WIDTH // LANES):
        qb = rope(h[:, b * LANES:(b + 1) * LANES]) * scale
        qm_ref[2 * b] = jnp.where(lo_half, qb, 0.0).astype(BF16)
        qm_ref[2 * b + 1] = jnp.where(lo_half, 0.0, qb).astype(BF16)

    key_stride = jnp.where(is_sample, KEYS_PER_CHUNK, CHUNK)
    col = lax.broadcasted_iota(jnp.int32, (1, KEYS_PER_CHUNK), 1)
    row_blk = lax.broadcasted_iota(jnp.int32, (Q_PER_KV * CHUNK, 1), 0) // CHUNK
    sink_cols = [
        jnp.where(row_blk == 0, sinks_ref[Q_PER_KV * g],
                  jnp.where(row_blk == 1, sinks_ref[Q_PER_KV * g + 1],
                            jnp.where(row_blk == 2, sinks_ref[Q_PER_KV * g + 2], sinks_ref[Q_PER_KV * g + 3])))
        for g in range(N_KV_HEADS)]

    def chunk_body(c, carry):
        r0 = pl.multiple_of(c * CHUNK, CHUNK)
        k0 = pl.multiple_of(c * key_stride, CHUNK)
        first_valid = jnp.where(seq_start, (WINDOW // CHUNK - c) * CHUNK, 0)
        valid = col >= first_valid
        outs = []
        for g in range(N_KV_HEADS):
            q_st = jnp.concatenate([qm_ref[Q_PER_KV * g + r, pl.ds(r0, CHUNK), :] for r in range(Q_PER_KV)],
                                   axis=0)
            keys = kd_ref[g, pl.ds(k0, KEYS_PER_CHUNK), :]
            vals = vd_ref[g, pl.ds(k0, KEYS_PER_CHUNK), :]
            s = lax.dot_general(q_st, keys, (((1,), (1,)), ((), ())), preferred_element_type=F32)
            s = jnp.where(valid, s, NEG_INF)
            sink = sink_cols[g]
            m = jnp.maximum(jnp.max(s, axis=-1, keepdims=True), sink)
            p = jnp.exp(s - m)
            denom = jnp.sum(p, axis=-1, keepdims=True) + jnp.exp(sink - m)
            o = jnp.dot(p.astype(BF16), vals, preferred_element_type=F32) * (1.0 / denom)
            for bb in range(2):
                outs.append(jnp.where(lo_half, o[(2 * bb) * CHUNK:(2 * bb + 1) * CHUNK],
                                      o[(2 * bb + 1) * CHUNK:(2 * bb + 2) * CHUNK]))
        for b in range(ATT_WIDTH // LANES):
            att_ref[pl.ds(r0, CHUNK), b * LANES:(b + 1) * LANES] = outs[b]
        return carry

    lax.fori_loop(0, tm // CHUNK, chunk_body, 0, unroll=8)

    prow = lax.broadcasted_iota(jnp.int32, (GM_CHUNK, GM_CHUNK), 0)
    pcol = lax.broadcasted_iota(jnp.int32, (GM_CHUNK, GM_CHUNK), 1)
    half = GM_CHUNK // 2
    tril = pcol <= prow
    same_blk = (prow < half) == (pcol < half)
    prow1 = lax.broadcasted_iota(jnp.int32, (GM_CHUNK, 1), 0)
    bst = bst_ref[...]
    bst_s = jnp.where(prow1 < half, bst, pltpu.roll(bst, half, 0))
    bias = jnp.where(is_sample, bst_s, bst)
    for g in range(GM_GROUPS):
        wg_ = ws_ref[g]
        w_s = jnp.where(prow < half, wg_, pltpu.roll(pltpu.roll(wg_, half, 0), half, 1))
        w_eff = jnp.where(is_sample, jnp.where(same_blk, w_s, 0.0), wg_)
        w_eff = jnp.where(tril, w_eff, 0.0).astype(BF16)
        cols = slice(g * GM_CH, (g + 1) * GM_CH)
        ug = _gelu(h[:, o_u + g * GM_CH:o_u + (g + 1) * GM_CH])
        vn = _layer_norm(_gelu(h[:, o_g + g * GM_CH:o_g + (g + 1) * GM_CH]), lng_ref[:, cols], lnb_ref[:, cols])

        if has_sample:
            @pl.when(is_sample)
            def _(cols=cols, vn=vn):
                vn_ref[:, cols] = vn

        vn_b = vn.astype(BF16)
        for n in range(tm // GM_CHUNK):
            rows = slice(n * GM_CHUNK, (n + 1) * GM_CHUNK)
            s = jnp.dot(w_eff, vn_b[rows], preferred_element_type=F32) + bias[:, g:g + 1]
            gm_ref[rows, cols] = ug[rows] * s

    mixed = jnp.concatenate([_rms_norm(att_ref[...], na_ref[...]), _rms_norm(gm_ref[...], nb_ref[...])], axis=-1)
    y = alpha * load_x() + jnp.dot(mixed.astype(BF16), w_o_s[...], preferred_element_type=F32)
    x1 = _layer_norm(y, l1g_ref[...], l1b_ref[...])
    _store_token_tiles(x1t_ref, x1)
    _store_packed_tiles(x1p_ref, x1)

    nt = (((1,), (1,)), ((), ()))
    wr = wrt_ref[...]
    wr_hi = wr.astype(BF16)
    wr_lo = (wr - wr_hi.astype(F32)).astype(BF16)
    x1_hi = x1.astype(BF16)
    x1_lo = (x1 - x1_hi.astype(F32)).astype(BF16)
    logits = (lax.dot_general(wr_hi, x1_hi, nt, preferred_element_type=F32)
              + (lax.dot_general(wr_hi, x1_lo, nt, preferred_element_type=F32)
                 + lax.dot_general(wr_lo, x1_hi, nt, preferred_element_type=F32)))
    scores = jax.nn.sigmoid(logits)
    sel = scores + rb_ref[...]
    sel_rows = [sel[e:e + 1, :] for e in range(N_EXPERTS)]
    sc_rows = [scores[e:e + 1, :] for e in range(N_EXPERTS)]
    grp = []
    for g in range(N_EXPERT_GROUPS):
        r = sel_rows[g * EXPERTS_PER_GROUP:(g + 1) * EXPERTS_PER_GROUP]
        best_pair = None
        for a in range(EXPERTS_PER_GROUP):
            for b in range(a + 1, EXPERTS_PER_GROUP):
                pair = r[a] + r[b]
                best_pair = pair if best_pair is None else jnp.maximum(best_pair, pair)
        grp.append(best_pair)
    _, best = _first_index_of_max(grp)
    masked = [jnp.where(best == (e // EXPERTS_PER_GROUP), sel_rows[e], NEG_INF) for e in range(N_EXPERTS)]
    _, e0 = _first_index_of_max(masked)
    masked2 = [jnp.where(e0 == e, -jnp.inf, masked[e]) for e in range(N_EXPERTS)]
    _, e1 = _first_index_of_max(masked2)
    w0 = jnp.zeros_like(sc_rows[0])
    w1 = jnp.zeros_like(sc_rows[0])
    for e in range(N_EXPERTS):
        w0 = jnp.where(e0 == e, sc_rows[e], w0)
        w1 = jnp.where(e1 == e, sc_rows[e], w1)
    wsum = w0 + w1
    rw_ref[...] = jnp.concatenate([w0 / wsum, w1 / wsum, jnp.zeros((6, tm), F32)], axis=0)

    @pl.when(i == 0)
    def _():
        cnt_ref[...] = jnp.zeros_like(cnt_ref)

    eid = lax.broadcasted_iota(jnp.int32, (N_EXPERTS, tm), 0)
    oh0 = (eid == e0).astype(F32)
    oh1 = (eid == e1).astype(F32)
    oh = oh0 + oh1
    upper = (lax.broadcasted_iota(jnp.int32, (tm, tm), 0) <= lax.broadcasted_iota(jnp.int32, (tm, tm), 1))
    incl = jnp.dot(oh.astype(BF16), upper.astype(BF16), preferred_element_type=F32)
    before = cnt_ref[:, 0:1] + incl - oh
    r0 = jnp.sum(oh0 * before, axis=0, keepdims=True).astype(jnp.int32)
    r1 = jnp.sum(oh1 * before, axis=0, keepdims=True).astype(jnp.int32)
    cnt = cnt_ref[...] + incl[:, tm - 1:tm]
    cnt_ref[...] = cnt
    counts_ref[...] = cnt.astype(jnp.int32)
    ridx_ref[...] = jnp.concatenate([e0, e1, r0, r1, jnp.zeros((4, tm), jnp.int32)], axis=0)


def _mixer_call(layer_args, x_all, tables, n_prompt_tiles, tiles_per_seq, alpha, layer):
    (sinks, w_in, ck, cv, lng, lnb, ws, bst, na, nb, w_o, l1g, l1b, wrt, rb) = layer_args
    cos_t, sa_t, sb_t = tables
    split_x = isinstance(x_all, tuple)
    if split_x:
        x_main, x_tail = x_all
        t_all = x_main.shape[0] + x_tail.shape[0]
        x_spec = pl.BlockSpec((TOK_TILE, D_MODEL), lambda i: (jnp.minimum(i, n_prompt_tiles - 1), 0))
    else:
        x_main = x_tail = x_all
        t_all = x_all.shape[0]
        x_spec = pl.BlockSpec((TOK_TILE, D_MODEL), lambda i: (i, 0))
    n_tiles = t_all // TOK_TILE
    n_seq_tiles = tiles_per_seq
    has_sample = n_tiles > n_prompt_tiles

    def const(shape):
        nd = len(shape)
        return pl.BlockSpec(shape, lambda i, _nd=nd: (0,) * _nd)

    def tab_map(i):
        return (jnp.where(i < n_prompt_tiles, i % n_seq_tiles, n_seq_tiles), 0)

    n_seq = n_prompt_tiles // n_seq_tiles

    def seq_map(i):
        return (jnp.minimum(i // n_seq_tiles, n_seq - 1), 0)

    row_blk = lambda w: pl.BlockSpec((TOK_TILE, w), lambda i: (i, 0))
    in_specs = [
        pl.BlockSpec(memory_space=pltpu.SMEM),
        x_spec,
        pl.BlockSpec((TOK_TILE, D_MODEL), lambda i: (0, 0)),
        pl.BlockSpec((1, D_MODEL, D_IN), lambda i: (layer, 0, 0)),
        pl.BlockSpec((TOK_TILE, LANES), tab_map),
        pl.BlockSpec((TOK_TILE, LANES), tab_map),
        pl.BlockSpec((TOK_TILE, LANES), tab_map),
        const(ck.shape), const(cv.shape),
        const((1, GM_WIDTH)), const((1, GM_WIDTH)),
        const((GM_GROUPS, GM_CHUNK, GM_CHUNK)), const((GM_CHUNK, GM_GROUPS)),
        const((1, ATT_WIDTH)), const((1, GM_WIDTH)),
        pl.BlockSpec((1, D_MIX, D_MODEL), lambda i: (layer, 0, 0)),
        const((1, D_MODEL)), const((1, D_MODEL)),
        const((N_EXPERTS, D_MODEL)), const((N_EXPERTS, 1)),
    ]
    out_shape = [
        jax.ShapeDtypeStruct((t_all * TILE_ROWS, LANES), F32),
        jax.ShapeDtypeStruct((t_all * PACK_ROWS, LANES), jnp.uint32),
        jax.ShapeDtypeStruct((n_seq * WINDOW, KV_WIDTH), F32),
        jax.ShapeDtypeStruct((n_seq * WINDOW, KV_WIDTH), F32),
        jax.ShapeDtypeStruct((TOK_TILE, KV_WIDTH), F32),
        jax.ShapeDtypeStruct((TOK_TILE, KV_WIDTH), F32),
        jax.ShapeDtypeStruct((TOK_TILE, GM_WIDTH), F32),
        jax.ShapeDtypeStruct((8, t_all), jnp.int32),
        jax.ShapeDtypeStruct((8, t_all), F32),
        jax.ShapeDtypeStruct((N_EXPERTS, LANES), jnp.int32),
    ]
    out_specs = [
        pl.BlockSpec((TOK_TILE * TILE_ROWS, LANES), lambda i: (i, 0)),
        pl.BlockSpec((TOK_TILE * PACK_ROWS, LANES), lambda i: (i, 0)),
        pl.BlockSpec((WINDOW, KV_WIDTH), seq_map), pl.BlockSpec((WINDOW, KV_WIDTH), seq_map),
        const((TOK_TILE, KV_WIDTH)), const((TOK_TILE, KV_WIDTH)),
        const((TOK_TILE, GM_WIDTH)),
        pl.BlockSpec((8, TOK_TILE), lambda i: (0, i)),
        pl.BlockSpec((8, TOK_TILE), lambda i: (0, i)),
        const((N_EXPERTS, LANES)),
    ]
    if not has_sample:
        del out_shape[4:7], out_specs[4:7]
    kd_rows = (TOK_TILE // CHUNK) * KEYS_PER_CHUNK
    scratch = [
        pltpu.VMEM((N_KV_HEADS, kd_rows, LANES), BF16),
        pltpu.VMEM((N_KV_HEADS, kd_rows, LANES), BF16),
        pltpu.VMEM((N_Q_HEADS, TOK_TILE, LANES), BF16),
        pltpu.VMEM((TOK_TILE, ATT_WIDTH), F32),
        pltpu.VMEM((TOK_TILE, GM_WIDTH), F32),
        pltpu.VMEM((N_EXPERTS, LANES), F32),
        pltpu.VMEM((D_MODEL, D_IN), BF16),
        pltpu.VMEM((D_MIX, D_MODEL), BF16),
    ]
    return pl.pallas_call(
        functools.partial(_mixer_kernel, n_prompt_tiles, tiles_per_seq, alpha, has_sample, split_x),
        grid=(n_tiles,),
        in_specs=in_specs, out_specs=out_specs, out_shape=out_shape,
        scratch_shapes=scratch,
        compiler_params=pltpu.CompilerParams(dimension_semantics=("arbitrary",), vmem_limit_bytes=VMEM_LIMIT),
        name="mixer",
    )(sinks, x_main, x_tail, w_in, cos_t, sa_t, sb_t, ck, cv, lng, lnb, ws, bst, na, nb, w_o, l1g, l1b, wrt, rb)


def _expert_kernel(te_ref, nt_ref, xs_ref, wg_ref, wu_ref, wd_ref, out_ref, wg_s, wu_s, wd_s):
    i = pl.program_id(0)
    changed = jnp.logical_or(i == 0, te_ref[i] != te_ref[jnp.maximum(i - 1, 0)])

    @pl.when(changed)
    def _():
        wg_s[...] = wg_ref[0, 0].astype(BF16)
        wu_s[...] = wu_ref[0, 0].astype(BF16)
        wd_s[...] = wd_ref[0, 0].astype(BF16)

    @pl.when(i < nt_ref[0])
    def _():
        xs = _load_packed_tiles(xs_ref, EXPERT_TILE, BF16)
        g = jnp.dot(xs, wg_s[...], preferred_element_type=F32)
        u = jnp.dot(xs, wu_s[...], preferred_element_type=F32)
        hmid = (g * jax.nn.sigmoid(g)) * u
        _store_packed_tiles(out_ref, jnp.dot(hmid.astype(BF16), wd_s[...], preferred_element_type=F32))

    @pl.when(i >= nt_ref[0])
    def _():
        out_ref[...] = jnp.zeros_like(out_ref)


def _expert_call(tile_expert, n_used, xs, wg, wu, wd, layer):
    n_tiles = xs.shape[0] // (EXPERT_TILE * PACK_ROWS)
    tile_blk = pl.BlockSpec((EXPERT_TILE * PACK_ROWS, LANES), lambda i, te, nt: (i, 0))
    grid_spec = pltpu.PrefetchScalarGridSpec(
        num_scalar_prefetch=2,
        grid=(n_tiles,),
        in_specs=[
            tile_blk,
            pl.BlockSpec((1, 1, D_MODEL, D_EXPERT), lambda i, te, nt: (layer, te[i], 0, 0)),
            pl.BlockSpec((1, 1, D_MODEL, D_EXPERT), lambda i, te, nt: (layer, te[i], 0, 0)),
            pl.BlockSpec((1, 1, D_EXPERT, D_MODEL), lambda i, te, nt: (layer, te[i], 0, 0)),
        ],
        out_specs=tile_blk,
        scratch_shapes=[
            pltpu.VMEM((D_MODEL, D_EXPERT), BF16),
            pltpu.VMEM((D_MODEL, D_EXPERT), BF16),
            pltpu.VMEM((D_EXPERT, D_MODEL), BF16),
        ],
    )
    return pl.pallas_call(
        _expert_kernel,
        grid_spec=grid_spec,
        out_shape=jax.ShapeDtypeStruct(xs.shape, jnp.uint32),
        compiler_params=pltpu.CompilerParams(dimension_semantics=("arbitrary",), vmem_limit_bytes=VMEM_LIMIT),
        name="experts",
    )(tile_expert, n_used, xs, wg, wu, wd)


ROWS_PER_STEP = TOK_TILE * TOP_K


def _token_tile(ref, idx):
    return ref.at[pl.ds(pl.multiple_of(idx * TILE_ROWS, TILE_ROWS), TILE_ROWS)]


def _dispatch_kernel(n_tok_steps, t_all, pos_ref, pad_ref, x1t_ref, xs_hbm, zero_ref, sems):
    j = pl.program_id(0)

    @pl.when(j == 0)
    def _():
        zero_ref[...] = jnp.zeros_like(zero_ref)

    @pl.when(j < n_tok_steps)
    def _():
        def body(tt, c):
            t = j * TOK_TILE + tt
            for k in range(TOP_K):
                pltpu.make_async_copy(_token_tile(x1t_ref, tt), _token_tile(xs_hbm, pos_ref[k * t_all + t]),
                                      sems.at[k]).start()
            return c
        lax.fori_loop(0, TOK_TILE, body, 0, unroll=8)

    @pl.when(j >= n_tok_steps)
    def _():
        def body(q, c):
            for k in range(TOP_K):
                row = pad_ref[(j - n_tok_steps) * ROWS_PER_STEP + k * TOK_TILE + q]
                pltpu.make_async_copy(zero_ref, _token_tile(xs_hbm, row), sems.at[k]).start()
            return c
        lax.fori_loop(0, TOK_TILE, body, 0, unroll=8)

    for k in range(TOP_K):
        pltpu.make_async_copy(x1t_ref, xs_hbm.at[pl.ds(0, TOK_TILE * TILE_ROWS)], sems.at[k]).wait()


def _dispatch_call(pos_flat, pad_rows, x1t, n_sorted_rows):
    t_all = x1t.shape[0] // TILE_ROWS
    n_tok_steps = t_all // TOK_TILE
    n_steps = n_tok_steps + pad_rows.shape[0] // ROWS_PER_STEP
    grid_spec = pltpu.PrefetchScalarGridSpec(
        num_scalar_prefetch=2,
        grid=(n_steps,),
        in_specs=[pl.BlockSpec((TOK_TILE * TILE_ROWS, LANES),
                               lambda j, p, q: (jnp.minimum(j, n_tok_steps - 1), 0))],
        out_specs=pl.BlockSpec(memory_space=pl.ANY),
        scratch_shapes=[pltpu.VMEM((TILE_ROWS, LANES), F32), pltpu.SemaphoreType.DMA((TOP_K,))],
    )
    return pl.pallas_call(
        functools.partial(_dispatch_kernel, n_tok_steps, t_all),
        grid_spec=grid_spec,
        out_shape=jax.ShapeDtypeStruct((n_sorted_rows * TILE_ROWS, LANES), F32),
        compiler_params=pltpu.CompilerParams(dimension_semantics=("arbitrary",)),
        name="dispatch",
    )(pos_flat, pad_rows, x1t)


def _combine_kernel(alpha, t_all, n_main_tiles, pos_ref, x1t_ref, w_ref, g_ref, b_ref, outs_hbm, *rest):
    if n_main_tiles is None:
        out_ref, gbuf, sems = rest
        tail_ref = None
    else:
        out_ref, tail_ref, gbuf, sems = rest
    j = pl.program_id(0)
    slot = j % 2
    k_rows = TOK_TILE * TILE_ROWS

    n = pl.num_programs(0)

    def wait(s):
        pltpu.make_async_copy(outs_hbm.at[pl.ds(0, TOP_K * k_rows)], gbuf.at[s], sems.at[s]).wait()

    def issue(step, s):
        def body(tt, c):
            t = step * TOK_TILE + tt
            for k in range(TOP_K):
                pltpu.make_async_copy(_token_tile(outs_hbm, pos_ref[k * t_all + t]),
                                      _token_tile(gbuf.at[s], k * TOK_TILE + tt), sems.at[s]).start()
            return c
        lax.fori_loop(0, TOK_TILE, body, 0, unroll=8)

    @pl.when(j == 0)
    def _():
        issue(0, 0)

    @pl.when(j + 1 < n)
    def _():
        issue(j + 1, 1 - slot)

    wait(slot)

    w = w_ref[...]
    y = (w[:, 0:1] * _load_token_tiles(gbuf.at[slot], TOK_TILE)
         + w[:, 1:2] * _load_token_tiles(gbuf.at[slot], TOK_TILE, k_rows))
    x1 = _load_token_tiles(x1t_ref, TOK_TILE)
    res = _layer_norm(alpha * x1 + y, g_ref[...], b_ref[...])
    if tail_ref is None:
        out_ref[...] = res
    else:
        @pl.when(j < n_main_tiles)
        def _():
            out_ref[...] = res

        @pl.when(j >= n_main_tiles)
        def _():
            tail_ref[...] = res


def _combine_call(pos_flat, x1t, w_col, g, b, outs, alpha, n_main_tiles=None):
    t_all = x1t.shape[0] // TILE_ROWS
    n_tiles = t_all // TOK_TILE
    if n_main_tiles is None:
        out_specs = pl.BlockSpec((TOK_TILE, D_MODEL), lambda i, p: (i, 0))
        out_shape = jax.ShapeDtypeStruct((t_all, D_MODEL), F32)
    else:
        assert n_tiles == n_main_tiles + 1
        out_specs = [pl.BlockSpec((TOK_TILE, D_MODEL), lambda i, p: (jnp.minimum(i, n_main_tiles - 1), 0)),
                     pl.BlockSpec((TOK_TILE, D_MODEL), lambda i, p: (0, 0))]
        out_shape = [jax.ShapeDtypeStruct((n_main_tiles * TOK_TILE, D_MODEL), F32),
                     jax.ShapeDtypeStruct((TOK_TILE, D_MODEL), F32)]
    grid_spec = pltpu.PrefetchScalarGridSpec(
        num_scalar_prefetch=1,
        grid=(t_all // TOK_TILE,),
        in_specs=[
            pl.BlockSpec((TOK_TILE * TILE_ROWS, LANES), lambda i, p: (i, 0)),
            pl.BlockSpec((TOK_TILE, TOP_K), lambda i, p: (i, 0)),
            pl.BlockSpec((1, D_MODEL), lambda i, p: (0, 0)),
            pl.BlockSpec((1, D_MODEL), lambda i, p: (0, 0)),
            pl.BlockSpec(memory_space=pl.ANY),
        ],
        out_specs=out_specs,
        scratch_shapes=[pltpu.VMEM((2, TOP_K * TOK_TILE * TILE_ROWS, LANES), F32), pltpu.SemaphoreType.DMA((2,))],
    )
    return pl.pallas_call(
        functools.partial(_combine_kernel, alpha, t_all, n_main_tiles),
        grid_spec=grid_spec,
        out_shape=out_shape,
        compiler_params=pltpu.CompilerParams(dimension_semantics=("arbitrary",), vmem_limit_bytes=VMEM_LIMIT),
        name="combine",
    )(pos_flat, x1t, w_col, g, b, outs)


SC_CHUNK = 96
SC_PAD_CHUNK = 32


def _sc_workers():
    info = plsc.get_sparse_core_info()
    return info.num_cores, info.num_cores * info.num_subcores


def _sc_split(per_w):
    n_full = per_w // SC_CHUNK
    tail = per_w - n_full * SC_CHUNK
    assert tail % 8 == 0
    return n_full, tail


def _sc_dispatch_call(pos_flat, pad_rows, x_tiles, tile_rows, n_sorted_rows):
    t_all = x_tiles.shape[0] // tile_rows
    dt = x_tiles.dtype
    nc, nw = _sc_workers()
    per_w = t_all // nw
    n_ch, tail = _sc_split(per_w)
    n_pad = pad_rows.shape[0]
    pad_per_w = n_pad // nw
    n_pch = pad_per_w // SC_PAD_CHUNK
    assert per_w * nw == t_all and n_pch * SC_PAD_CHUNK * nw == n_pad
    zeros = jnp.zeros((SC_PAD_CHUNK, tile_rows, LANES), dt)
    tail_scratch = [] if tail == 0 else [
        pltpu.VMEM((tail,), jnp.int32), pltpu.VMEM((tail,), jnp.int32), pltpu.VMEM((tail, tile_rows, LANES), dt)]

    @functools.partial(
        pl.kernel, mesh=plsc.VectorSubcoreMesh(core_axis_name="c", subcore_axis_name="s"),
        out_type=jax.ShapeDtypeStruct((n_sorted_rows, tile_rows, LANES), dt),
        scratch_types=[
            pltpu.VMEM((SC_CHUNK,), jnp.int32), pltpu.VMEM((SC_CHUNK,), jnp.int32),
            pltpu.VMEM((SC_CHUNK, tile_rows, LANES), dt),
            pltpu.VMEM((SC_PAD_CHUNK,), jnp.int32),
            pltpu.VMEM((SC_PAD_CHUNK, tile_rows, LANES), dt),
        ] + tail_scratch,
    )
    def k(pos_hbm, pad_hbm, x_hbm, z_hbm, xs_hbm, i0_v, i1_v, rows_v, ip_v, z_v, *tail_refs):
        wid = lax.axis_index("s") * nc + lax.axis_index("c")

        def move(b, n, idx0, idx1, rows):
            pltpu.sync_copy(pos_hbm.at[pl.ds(b, n)], idx0)
            pltpu.sync_copy(pos_hbm.at[pl.ds(t_all + b, n)], idx1)
            pltpu.sync_copy(x_hbm.at[pl.ds(b, n)], rows)
            pltpu.sync_copy(rows, xs_hbm.at[idx0])
            pltpu.sync_copy(rows, xs_hbm.at[idx1])

        @pl.loop(0, n_ch)
        def _(j):
            move(pl.multiple_of(wid * per_w + j * SC_CHUNK, 8), SC_CHUNK, i0_v, i1_v, rows_v)

        if tail:
            move(pl.multiple_of(wid * per_w + n_ch * SC_CHUNK, 8), tail, *tail_refs)

        pltpu.sync_copy(z_hbm, z_v)

        @pl.loop(0, n_pch)
        def _(j):
            b = pl.multiple_of(wid * pad_per_w + j * SC_PAD_CHUNK, 8)
            pltpu.sync_copy(pad_hbm.at[pl.ds(b, SC_PAD_CHUNK)], ip_v)
            pltpu.sync_copy(z_v, xs_hbm.at[ip_v])

    xs = k(pos_flat, pad_rows, x_tiles.reshape(t_all, tile_rows, LANES), zeros)
    return xs.reshape(n_sorted_rows * tile_rows, LANES)


def _sc_gather_call(pos_flat, outs, tile_rows):
    n_assign = pos_flat.shape[0]
    dt = outs.dtype
    nc, nw = _sc_workers()
    per_w = n_assign // nw
    n_ch, tail = _sc_split(per_w)
    assert per_w * nw == n_assign
    tail_scratch = [] if tail == 0 else [pltpu.VMEM((tail,), jnp.int32), pltpu.VMEM((tail, tile_rows, LANES), dt)]

    @functools.partial(
        pl.kernel, mesh=plsc.VectorSubcoreMesh(core_axis_name="c", subcore_axis_name="s"),
        out_type=jax.ShapeDtypeStruct((n_assign, tile_rows, LANES), dt),
        scratch_types=[pltpu.VMEM((SC_CHUNK,), jnp.int32),
                       pltpu.VMEM((SC_CHUNK, tile_rows, LANES), dt)] + tail_scratch,
    )
    def k(pos_hbm, o_hbm, y_hbm, i_v, rows_v, *tail_refs):
        wid = lax.axis_index("s") * nc + lax.axis_index("c")

        def move(b, n, idx, rows):
            pltpu.sync_copy(pos_hbm.at[pl.ds(b, n)], idx)
            pltpu.sync_copy(o_hbm.at[idx], rows)
            pltpu.sync_copy(rows, y_hbm.at[pl.ds(b, n)])

        @pl.loop(0, n_ch)
        def _(j):
            move(pl.multiple_of(wid * per_w + j * SC_CHUNK, 8), SC_CHUNK, i_v, rows_v)

        if tail:
            move(pl.multiple_of(wid * per_w + n_ch * SC_CHUNK, 8), tail, *tail_refs)

    y = k(pos_flat, outs.reshape(-1, tile_rows, LANES))
    return y.reshape(n_assign * tile_rows, LANES)


def _combine_dense_kernel(alpha, n_main_tiles, x1t_ref, y0_ref, y1_ref, w_ref, g_ref, b_ref, out_ref, *tail):
    j = pl.program_id(0)
    w = w_ref[...]
    y = (w[:, 0:1] * _load_packed_tiles(y0_ref, TOK_TILE, F32)
         + w[:, 1:2] * _load_packed_tiles(y1_ref, TOK_TILE, F32))
    res = _layer_norm(alpha * _load_token_tiles(x1t_ref, TOK_TILE) + y, g_ref[...], b_ref[...])
    if n_main_tiles is None:
        out_ref[...] = res
    else:
        @pl.when(j < n_main_tiles)
        def _():
            out_ref[...] = res

        @pl.when(j >= n_main_tiles)
        def _():
            tail[0][...] = res


def _combine_dense_call(x1t, y_pair, w_col, g, b, alpha, n_main_tiles=None):
    t_all = x1t.shape[0] // TILE_ROWS
    n_tiles = t_all // TOK_TILE
    if n_main_tiles is None:
        out_specs = pl.BlockSpec((TOK_TILE, D_MODEL), lambda i: (i, 0))
        out_shape = jax.ShapeDtypeStruct((t_all, D_MODEL), F32)
    else:
        assert n_tiles == n_main_tiles + 1
        out_specs = [pl.BlockSpec((TOK_TILE, D_MODEL), lambda i: (jnp.minimum(i, n_main_tiles - 1), 0)),
                     pl.BlockSpec((TOK_TILE, D_MODEL), lambda i: (0, 0))]
        out_shape = [jax.ShapeDtypeStruct((n_main_tiles * TOK_TILE, D_MODEL), F32),
                     jax.ShapeDtypeStruct((TOK_TILE, D_MODEL), F32)]
    tile = (TOK_TILE * TILE_ROWS, LANES)
    packed = (TOK_TILE * PACK_ROWS, LANES)
    return pl.pallas_call(
        functools.partial(_combine_dense_kernel, alpha, n_main_tiles),
        grid=(n_tiles,),
        in_specs=[
            pl.BlockSpec(tile, lambda i: (i, 0)),
            pl.BlockSpec(packed, lambda i: (i, 0)),
            pl.BlockSpec(packed, lambda i: (i + n_tiles, 0)),
            pl.BlockSpec((TOK_TILE, TOP_K), lambda i: (i, 0)),
            pl.BlockSpec((1, D_MODEL), lambda i: (0, 0)),
            pl.BlockSpec((1, D_MODEL), lambda i: (0, 0)),
        ],
        out_specs=out_specs,
        out_shape=out_shape,
        compiler_params=pltpu.CompilerParams(dimension_semantics=("arbitrary",), vmem_limit_bytes=VMEM_LIMIT),
        name="combine",
    )(x1t, y_pair, y_pair, w_col, g, b)


def _rope_tables(seq, dec_seq, past_len):
    half = HEAD_DIM // 2
    inv = ROPE_THETA ** (-jnp.arange(half, dtype=F32) / half)
    pos = jnp.concatenate([jnp.arange(seq), past_len + (jnp.arange(TOK_TILE) % dec_seq)])
    ang = pos.astype(F32)[:, None] * inv[None, :]
    cos = jnp.cos(ang)
    sin = jnp.sin(ang)
    zero = jnp.zeros_like(sin)
    reps = LANES // HEAD_DIM
    cos_t = jnp.tile(jnp.concatenate([cos, cos], -1), (1, reps))
    sa_t = jnp.tile(jnp.concatenate([-sin, zero], -1), (1, reps))
    sb_t = jnp.tile(jnp.concatenate([zero, sin], -1), (1, reps))
    return cos_t, sa_t, sb_t


def _inclusive_cumsum(v):
    n = v.shape[0]
    tri = jnp.arange(n)[None, :] <= jnp.arange(n)[:, None]
    return jnp.sum(jnp.where(tri, v[None, :], 0), axis=1)


def _dispatch_plan(ridx, counts, n_tiles):
    t_all = ridx.shape[1]
    tiles_e = (counts + EXPERT_TILE - 1) // EXPERT_TILE
    tile_end = _inclusive_cumsum(tiles_e)
    offs = (tile_end - tiles_e) * EXPERT_TILE
    n_used = tile_end[-1:]
    tile_expert = jnp.minimum(
        jnp.sum((jnp.arange(n_tiles)[:, None] >= tile_end[None, :]).astype(jnp.int32), axis=1), N_EXPERTS - 1)
    experts = jnp.arange(N_EXPERTS)
    pos = ridx[TOP_K:2 * TOP_K] + jnp.sum(
        jnp.where(ridx[:TOP_K, :, None] == experts[None, None, :], offs[None, None, :], 0), axis=-1)
    gap_start = jnp.concatenate([offs + counts, n_used * EXPERT_TILE])
    gap_len = jnp.concatenate([tiles_e * EXPERT_TILE - counts, (n_tiles - n_used) * EXPERT_TILE])
    gap_end_q = _inclusive_cumsum(gap_len)
    q = jnp.arange(n_tiles * EXPERT_TILE - TOP_K * t_all)
    gap = jnp.sum((q[:, None] >= gap_end_q[None, :]).astype(jnp.int32), axis=1)
    in_gap = gap[:, None] == jnp.arange(N_EXPERTS + 1)[None, :]
    pad_rows = q + jnp.sum(jnp.where(in_gap, (gap_start - gap_end_q + gap_len)[None, :], 0), axis=1)
    return (pos.reshape(-1).astype(jnp.int32), pad_rows.astype(jnp.int32),
            tile_expert.astype(jnp.int32), n_used.astype(jnp.int32))


def kernel(x_prompt, x_sample, cache_k, cache_v, w_in, sinks, gm_ln_g, gm_ln_b, gm_ws, gm_bs,
           out_norm_a, out_norm_b, w_o, ln1_g, ln1_b, w_router, router_bias,
           w_gate, w_up, w_down, ln2_g, ln2_b):
    batch, seq, _ = x_prompt.shape
    dec_batch, dec_seq, _ = x_sample.shape
    depth = w_in.shape[0]
    past_len = PAST_LEN
    assert dec_batch * dec_seq == TOK_TILE and dec_seq == CHUNK and seq % TOK_TILE == 0
    assert cache_k.shape[2] == WINDOW
    alpha = (2 * depth) ** 0.25
    tiles_per_seq = seq // TOK_TILE

    tables = _rope_tables(seq, dec_seq, past_len)
    wrt = w_router.T
    rb = router_bias.reshape(N_EXPERTS, 1)

    streams = [
        dict(x=(x_prompt.reshape(batch * seq, D_MODEL), x_sample.reshape(-1, D_MODEL)),
             n_seq=batch, has_sample=True),
    ]
    kp, vp, ks, vs, gms = [], [], [], [], []
    for l in range(depth):
        layer_args = (
            sinks[l], w_in,
            cache_k[l].reshape(dec_batch, WINDOW, KV_WIDTH), cache_v[l].reshape(dec_batch, WINDOW, KV_WIDTH),
            gm_ln_g[l].reshape(1, GM_WIDTH), gm_ln_b[l].reshape(1, GM_WIDTH),
            gm_ws[l], gm_bs[l].T,
            out_norm_a[l].reshape(1, ATT_WIDTH), out_norm_b[l].reshape(1, GM_WIDTH),
            w_o,
            ln1_g[l].reshape(1, D_MODEL), ln1_b[l].reshape(1, D_MODEL), wrt, rb)
        k_tails, v_tails = [], []
        for st in streams:
            n_prompt = st["n_seq"] * seq
            n_prompt_tiles = n_prompt // TOK_TILE
            t_all = n_prompt + (TOK_TILE if st["has_sample"] else 0)
            n_exp_tiles = (TOP_K * t_all) // EXPERT_TILE + N_EXPERTS
            res = _mixer_call(layer_args, st["x"], tables, n_prompt_tiles, tiles_per_seq, alpha, l)
            if st["has_sample"]:
                x1t, x1p, k_tail, v_tail, k_s, v_s, vn_s, ridx, rw, counts = res
            else:
                x1t, x1p, k_tail, v_tail, ridx, rw, counts = res

            pos, pad_rows, tile_expert, n_used = _dispatch_plan(ridx, counts[:, 0], n_exp_tiles)
            xs = _sc_dispatch_call(pos, pad_rows, x1p, PACK_ROWS, n_exp_tiles * EXPERT_TILE)
            outs = _expert_call(tile_expert, n_used, xs, w_gate, w_up, w_down, l)
            y_pair = _sc_gather_call(pos, outs, PACK_ROWS)
            split = n_prompt_tiles if (l == depth - 1 and st["has_sample"]) else None
            st["x"] = _combine_dense_call(x1t, y_pair, rw[:TOP_K].T, ln2_g[l].reshape(1, D_MODEL),
                                          ln2_b[l].reshape(1, D_MODEL), alpha, n_main_tiles=split)

            k_tails.append(k_tail)
            v_tails.append(v_tail)
            if st["has_sample"]:
                ks.append(k_s.reshape(dec_batch, dec_seq, N_KV_HEADS, HEAD_DIM))
                vs.append(v_s.reshape(dec_batch, dec_seq, N_KV_HEADS, HEAD_DIM))
                gms.append(vn_s.reshape(dec_batch, dec_seq, GM_WIDTH))
        kp.append(jnp.concatenate(k_tails, axis=0).reshape(batch, WINDOW, N_KV_HEADS, HEAD_DIM))
        vp.append(jnp.concatenate(v_tails, axis=0).reshape(batch, WINDOW, N_KV_HEADS, HEAD_DIM))

    y_prompt = jnp.concatenate([st["x"][0] if st["has_sample"] else st["x"] for st in streams],
                               axis=0).reshape(batch, seq, D_MODEL)
    y_sample = streams[-1]["x"][1].reshape(dec_batch, dec_seq, D_MODEL)
    return (y_prompt, y_sample, jnp.stack(kp), jnp.stack(vp), jnp.stack(ks), jnp.stack(vs), jnp.stack(gms))
```

```python
import functools

import jax
import jax.numpy as jnp
from jax import lax
from jax.experimental import pallas as pl
from jax.experimental.pallas import tpu as pltpu
from jax.experimental.pallas import tpu_sc as plsc

D_MODEL = 1024
CHUNK = 64
WINDOW = 128
HEAD_DIM = 64
N_Q_HEADS = 8
N_KV_HEADS = 2
Q_PER_KV = N_Q_HEADS // N_KV_HEADS
PAST_LEN = 2048
ATT_WIDTH = N_Q_HEADS * HEAD_DIM
KV_WIDTH = N_KV_HEADS * HEAD_DIM
ROPE_THETA = 10000.0
GM_GROUPS = 4
GM_CH = 128
GM_WIDTH = GM_GROUPS * GM_CH
GM_CHUNK = 128
D_MIX = ATT_WIDTH + GM_WIDTH
D_IN = ATT_WIDTH + 2 * KV_WIDTH + 2 * GM_WIDTH
N_EXPERTS = 16
N_EXPERT_GROUPS = 4
EXPERTS_PER_GROUP = N_EXPERTS // N_EXPERT_GROUPS
TOP_K = 2
D_EXPERT = 512
LN_EPS = 1e-5
NEG_INF = -1e30

LANES = 128
TILE_ROWS = D_MODEL // LANES
PACK_ROWS = TILE_ROWS // 2
HI_HALF = 0xFFFF0000
TOK_TILE = 512
EXPERT_TILE = 512
KEYS_PER_CHUNK = WINDOW + CHUNK
VMEM_LIMIT = 60 * 1024 * 1024

F32 = jnp.float32
BF16 = jnp.bfloat16


def _layer_norm(x, g, b):
    mu = jnp.mean(x, axis=-1, keepdims=True)
    d = x - mu
    var = jnp.mean(d * d, axis=-1, keepdims=True)
    return d * lax.rsqrt(var + LN_EPS) * g + b


def _rms_norm(x, g):
    return x * lax.rsqrt(jnp.mean(x * x, axis=-1, keepdims=True) + LN_EPS) * g


def _gelu(x):
    return 0.5 * x * (1.0 + lax.erf(x * (0.5 ** 0.5)))


def _store_token_tiles(ref, x):
    n = x.shape[0]
    for c in range(TILE_ROWS):
        ref[pl.ds(c, n, stride=TILE_ROWS), :] = x[:, c * LANES:(c + 1) * LANES]


def _load_token_tiles(ref, n, row0=0):
    return jnp.concatenate([ref[pl.ds(row0 + c, n, stride=TILE_ROWS), :] for c in range(TILE_ROWS)], axis=-1)


def _bf16_bits(x):
    return lax.bitcast_convert_type(x.astype(BF16).astype(F32), jnp.uint32)


def _store_packed_tiles(ref, x):
    n = x.shape[0]
    for s in range(PACK_ROWS):
        lo = _bf16_bits(x[:, (2 * s) * LANES:(2 * s + 1) * LANES]) >> 16
        hi = _bf16_bits(x[:, (2 * s + 1) * LANES:(2 * s + 2) * LANES]) & jnp.uint32(HI_HALF)
        ref[pl.ds(s, n, stride=PACK_ROWS), :] = hi | lo


def _load_packed_tiles(ref, n, dtype):
    cols = []
    for s in range(PACK_ROWS):
        w = ref[pl.ds(s, n, stride=PACK_ROWS), :]
        cols.append(lax.bitcast_convert_type(w << 16, F32).astype(dtype))
        cols.append(lax.bitcast_convert_type(w & jnp.uint32(HI_HALF), F32).astype(dtype))
    return jnp.concatenate(cols, axis=-1)


def _first_index_of_max(rows):
    m = rows[0]
    for r in rows[1:]:
        m = jnp.maximum(m, r)
    idx = jnp.full(m.shape, len(rows), jnp.int32)
    for e in reversed(range(len(rows))):
        idx = jnp.where(rows[e] == m, e, idx)
    return m, idx


def _mixer_kernel(n_prompt_tiles, tiles_per_seq, alpha, has_sample, split_x,
                  sinks_ref, x_ref, xs_ref, w_in_ref, cos_ref, sa_ref, sb_ref, ck_ref, cv_ref,
                  lng_ref, lnb_ref, ws_ref, bst_ref, na_ref, nb_ref, w_o_ref,
                  l1g_ref, l1b_ref, wrt_ref, rb_ref, tri_ref, x1t_ref, x1p_ref, kt_ref, vt_ref, *rest):
    if has_sample:
        (ks_ref, vs_ref, vn_ref, ridx_ref, rw_ref, counts_ref,
         kd_ref, vd_ref, qm_ref, att_ref, gm_ref, cnt_ref, w_in_s, w_o_s) = rest
    else:
        ridx_ref, rw_ref, counts_ref, kd_ref, vd_ref, qm_ref, att_ref, gm_ref, cnt_ref, w_in_s, w_o_s = rest
    i = pl.program_id(0)
    is_sample = i >= n_prompt_tiles
    seq_start = jnp.logical_and(jnp.logical_not(is_sample), (i % tiles_per_seq) == 0)
    seq_end = jnp.logical_and(jnp.logical_not(is_sample), (i % tiles_per_seq) == tiles_per_seq - 1)
    tm = TOK_TILE

    @pl.when(i == 0)
    def _():
        w_in_s[...] = w_in_ref[0].astype(BF16)
        w_o_s[...] = w_o_ref[0].astype(BF16)

    def load_x():
        return jnp.where(is_sample, xs_ref[...], x_ref[...]) if split_x else x_ref[...]

    h = jnp.dot(load_x().astype(BF16), w_in_s[...], preferred_element_type=F32)

    cos = cos_ref[...]
    sa = sa_ref[...]
    sb = sb_ref[...]

    def rope(blk):
        return blk * cos + pltpu.roll(blk, LANES - HEAD_DIM // 2, 1) * sa + pltpu.roll(blk, HEAD_DIM // 2, 1) * sb

    lane = lax.broadcasted_iota(jnp.int32, (1, LANES), 1)
    lo_half = lane < HEAD_DIM

    o_k = ATT_WIDTH
    o_v = o_k + KV_WIDTH
    o_u = o_v + KV_WIDTH
    o_g = o_u + GM_WIDTH

    k_rot = rope(h[:, o_k:o_v])
    v_new = h[:, o_v:o_u]

    @pl.when(seq_end)
    def _():
        kt_ref[...] = k_rot[tm - WINDOW:]
        vt_ref[...] = v_new[tm - WINDOW:]

    if has_sample:
        @pl.when(is_sample)
        def _():
            ks_ref[...] = k_rot
            vs_ref[...] = v_new

    def dup_heads(a):
        sw = pltpu.roll(a, HEAD_DIM, 1)
        return jnp.where(lo_half, a, sw).astype(BF16), jnp.where(lo_half, sw, a).astype(BF16)

    k_d = dup_heads(k_rot)
    v_d = dup_heads(v_new)

    @pl.when(jnp.logical_not(is_sample))
    def _():
        @pl.when(seq_start)
        def _():
            for g in range(N_KV_HEADS):
                kd_ref[g, 0:WINDOW, :] = jnp.zeros((WINDOW, LANES), BF16)
                vd_ref[g, 0:WINDOW, :] = jnp.zeros((WINDOW, LANES), BF16)

        @pl.when(jnp.logical_not(seq_start))
        def _():
            for g in range(N_KV_HEADS):
                kd_ref[g, 0:WINDOW, :] = kd_ref[g, tm:tm + WINDOW, :]
                vd_ref[g, 0:WINDOW, :] = vd_ref[g, tm:tm + WINDOW, :]

        for g in range(N_KV_HEADS):
            kd_ref[g, WINDOW:WINDOW + tm, :] = k_d[g]
            vd_ref[g, WINDOW:WINDOW + tm, :] = v_d[g]

    @pl.when(is_sample)
    def _():
        for b in range(tm // CHUNK):
            ck = dup_heads(ck_ref[b])
            cv = dup_heads(cv_ref[b])
            base = b * KEYS_PER_CHUNK
            for g in range(N_KV_HEADS):
                kd_ref[g, base:base + WINDOW, :] = ck[g]
                vd_ref[g, base:base + WINDOW, :] = cv[g]
                kd_ref[g, base + WINDOW:base + KEYS_PER_CHUNK, :] = k_d[g][b * CHUNK:(b + 1) * CHUNK]
                vd_ref[g, base + WINDOW:base + KEYS_PER_CHUNK, :] = v_d[g][b * CHUNK:(b + 1) * CHUNK]

    scale = HEAD_DIM ** -0.5
    for b in range(ATT_WIDTH // LANES):
        qb = (rope(h[:, b * LANES:(b + 1) * LANES]) * scale).astype(BF16)
        zero = jnp.zeros_like(qb)
        qm_ref[2 * b] = jnp.where(lo_half, qb, zero)
        qm_ref[2 * b + 1] = jnp.where(lo_half, zero, qb)

    key_stride = jnp.where(is_sample, KEYS_PER_CHUNK, CHUNK)
    col = lax.broadcasted_iota(jnp.int32, (1, KEYS_PER_CHUNK), 1)
    row_blk = lax.broadcasted_iota(jnp.int32, (Q_PER_KV * CHUNK, 1), 0) // CHUNK
    sink_cols = [
        jnp.where(row_blk == 0, sinks_ref[Q_PER_KV * g],
                  jnp.where(row_blk == 1, sinks_ref[Q_PER_KV * g + 1],
                            jnp.where(row_blk == 2, sinks_ref[Q_PER_KV * g + 2], sinks_ref[Q_PER_KV * g + 3])))
        for g in range(N_KV_HEADS)]

    for c in range(tm // CHUNK):
        r0 = c * CHUNK
        k0 = pl.multiple_of(c * key_stride, CHUNK)
        outs = []
        for g in range(N_KV_HEADS):
            q_st = jnp.concatenate([qm_ref[Q_PER_KV * g + r, pl.ds(r0, CHUNK), :] for r in range(Q_PER_KV)],
                                   axis=0)
            keys = kd_ref[g, pl.ds(k0, KEYS_PER_CHUNK), :]
            vals = vd_ref[g, pl.ds(k0, KEYS_PER_CHUNK), :]
            s = lax.dot_general(q_st, keys, (((1,), (1,)), ((), ())), preferred_element_type=F32)
            if c < WINDOW // CHUNK:
                first_valid = jnp.where(seq_start, (WINDOW // CHUNK - c) * CHUNK, 0)
                s = jnp.where(col >= first_valid, s, NEG_INF)
            sink = sink_cols[g]
            m = jnp.maximum(jnp.max(s, axis=-1, keepdims=True), sink)
            p = jnp.exp(s - m)
            denom = jnp.sum(p, axis=-1, keepdims=True) + jnp.exp(sink - m)
            o = jnp.dot(p.astype(BF16), vals, preferred_element_type=F32) * (1.0 / denom)
            for bb in range(2):
                outs.append(jnp.where(lo_half, o[(2 * bb) * CHUNK:(2 * bb + 1) * CHUNK],
                                      o[(2 * bb + 1) * CHUNK:(2 * bb + 2) * CHUNK]))
        for b in range(ATT_WIDTH // LANES):
            att_ref[pl.ds(r0, CHUNK), b * LANES:(b + 1) * LANES] = outs[b]

    prow = lax.broadcasted_iota(jnp.int32, (GM_CHUNK, GM_CHUNK), 0)
    pcol = lax.broadcasted_iota(jnp.int32, (GM_CHUNK, GM_CHUNK), 1)
    half = GM_CHUNK // 2
    tril = pcol <= prow
    same_blk = (prow < half) == (pcol < half)
    prow1 = lax.broadcasted_iota(jnp.int32, (GM_CHUNK, 1), 0)
    bst = bst_ref[...]
    bst_s = jnp.where(prow1 < half, bst, pltpu.roll(bst, half, 0))
    bias = jnp.where(is_sample, bst_s, bst)
    for g in range(GM_GROUPS):
        wg_ = ws_ref[g]
        w_s = jnp.where(prow < half, wg_, pltpu.roll(pltpu.roll(wg_, half, 0), half, 1))
        w_eff = jnp.where(is_sample, jnp.where(same_blk, w_s, 0.0), wg_)
        w_eff = jnp.where(tril, w_eff, 0.0).astype(BF16)
        cols = slice(g * GM_CH, (g + 1) * GM_CH)
        ug = _gelu(h[:, o_u + g * GM_CH:o_u + (g + 1) * GM_CH])
        vn = _layer_norm(_gelu(h[:, o_g + g * GM_CH:o_g + (g + 1) * GM_CH]), lng_ref[:, cols], lnb_ref[:, cols])

        if has_sample:
            @pl.when(is_sample)
            def _(cols=cols, vn=vn):
                vn_ref[:, cols] = vn

        vn_b = vn.astype(BF16)
        for n in range(tm // GM_CHUNK):
            rows = slice(n * GM_CHUNK, (n + 1) * GM_CHUNK)
            s = jnp.dot(w_eff, vn_b[rows], preferred_element_type=F32) + bias[:, g:g + 1]
            gm_ref[rows, cols] = ug[rows] * s

    mixed = jnp.concatenate([_rms_norm(att_ref[...], na_ref[...]), _rms_norm(gm_ref[...], nb_ref[...])], axis=-1)
    y = alpha * load_x() + jnp.dot(mixed.astype(BF16), w_o_s[...], preferred_element_type=F32)
    x1 = _layer_norm(y, l1g_ref[...], l1b_ref[...])
    _store_token_tiles(x1t_ref, x1)
    _store_packed_tiles(x1p_ref, x1)

    nt = (((1,), (1,)), ((), ()))
    wr = wrt_ref[...]
    wr_hi = wr.astype(BF16)
    wr_lo = (wr - wr_hi.astype(F32)).astype(BF16)
    x1_hi = x1.astype(BF16)
    x1_lo = (x1 - x1_hi.astype(F32)).astype(BF16)
    logits = (lax.dot_general(wr_hi, x1_hi, nt, preferred_element_type=F32)
              + (lax.dot_general(wr_hi, x1_lo, nt, preferred_element_type=F32)
                 + lax.dot_general(wr_lo, x1_hi, nt, preferred_element_type=F32)))
    scores = jax.nn.sigmoid(logits)
    sel = scores + rb_ref[...]
    sel_rows = [sel[e:e + 1, :] for e in range(N_EXPERTS)]
    sc_rows = [scores[e:e + 1, :] for e in range(N_EXPERTS)]
    grp = []
    for g in range(N_EXPERT_GROUPS):
        r = sel_rows[g * EXPERTS_PER_GROUP:(g + 1) * EXPERTS_PER_GROUP]
        best_pair = None
        for a in range(EXPERTS_PER_GROUP):
            for b in range(a + 1, EXPERTS_PER_GROUP):
                pair = r[a] + r[b]
                best_pair = pair if best_pair is None else jnp.maximum(best_pair, pair)
        grp.append(best_pair)
    _, best = _first_index_of_max(grp)
    masked = [jnp.where(best == (e // EXPERTS_PER_GROUP), sel_rows[e], NEG_INF) for e in range(N_EXPERTS)]
    _, e0 = _first_index_of_max(masked)
    masked2 = [jnp.where(e0 == e, -jnp.inf, masked[e]) for e in range(N_EXPERTS)]
    _, e1 = _first_index_of_max(masked2)
    w0 = jnp.zeros_like(sc_rows[0])
    w1 = jnp.zeros_like(sc_rows[0])
    for e in range(N_EXPERTS):
        w0 = jnp.where(e0 == e, sc_rows[e], w0)
        w1 = jnp.where(e1 == e, sc_rows[e], w1)
    wsum = w0 + w1
    rw_ref[...] = jnp.concatenate([w0 / wsum, w1 / wsum, jnp.zeros((6, tm), F32)], axis=0)

    @pl.when(i == 0)
    def _():
        cnt_ref[...] = jnp.zeros_like(cnt_ref)

    eid = lax.broadcasted_iota(jnp.int32, (N_EXPERTS, tm), 0)
    oh0 = (eid == e0).astype(F32)
    oh1 = (eid == e1).astype(F32)
    oh = oh0 + oh1
    incl = jnp.dot(oh.astype(BF16), tri_ref[...], preferred_element_type=F32)
    before = cnt_ref[:, 0:1] + incl - oh
    r0 = jnp.sum(oh0 * before, axis=0, keepdims=True).astype(jnp.int32)
    r1 = jnp.sum(oh1 * before, axis=0, keepdims=True).astype(jnp.int32)
    cnt = cnt_ref[...] + incl[:, tm - 1:tm]
    cnt_ref[...] = cnt
    counts_ref[...] = cnt.astype(jnp.int32)
    ridx_ref[...] = jnp.concatenate([e0, e1, r0, r1, jnp.zeros((4, tm), jnp.int32)], axis=0)


def _mixer_call(layer_args, x_all, tables, n_prompt_tiles, tiles_per_seq, alpha, layer):
    (sinks, w_in, ck, cv, lng, lnb, ws, bst, na, nb, w_o, l1g, l1b, wrt, rb) = layer_args
    cos_t, sa_t, sb_t = tables
    split_x = isinstance(x_all, tuple)
    if split_x:
        x_main, x_tail = x_all
        t_all = x_main.shape[0] + x_tail.shape[0]
        x_spec = pl.BlockSpec((TOK_TILE, D_MODEL), lambda i: (jnp.minimum(i, n_prompt_tiles - 1), 0))
    else:
        x_main = x_tail = x_all
        t_all = x_all.shape[0]
        x_spec = pl.BlockSpec((TOK_TILE, D_MODEL), lambda i: (i, 0))
    n_tiles = t_all // TOK_TILE
    n_seq_tiles = tiles_per_seq
    has_sample = n_tiles > n_prompt_tiles

    def const(shape):
        nd = len(shape)
        return pl.BlockSpec(shape, lambda i, _nd=nd: (0,) * _nd)

    def tab_map(i):
        return (jnp.where(i < n_prompt_tiles, i % n_seq_tiles, n_seq_tiles), 0)

    n_seq = n_prompt_tiles // n_seq_tiles

    def seq_map(i):
        return (jnp.minimum(i // n_seq_tiles, n_seq - 1), 0)

    row_blk = lambda w: pl.BlockSpec((TOK_TILE, w), lambda i: (i, 0))
    in_specs = [
        pl.BlockSpec(memory_space=pltpu.SMEM),
        x_spec,
        pl.BlockSpec((TOK_TILE, D_MODEL), lambda i: (0, 0)),
        pl.BlockSpec((1, D_MODEL, D_IN), lambda i: (layer, 0, 0)),
        pl.BlockSpec((TOK_TILE, LANES), tab_map),
        pl.BlockSpec((TOK_TILE, LANES), tab_map),
        pl.BlockSpec((TOK_TILE, LANES), tab_map),
        const(ck.shape), const(cv.shape),
        const((1, GM_WIDTH)), const((1, GM_WIDTH)),
        const((GM_GROUPS, GM_CHUNK, GM_CHUNK)), const((GM_CHUNK, GM_GROUPS)),
        const((1, ATT_WIDTH)), const((1, GM_WIDTH)),
        pl.BlockSpec((1, D_MIX, D_MODEL), lambda i: (layer, 0, 0)),
        const((1, D_MODEL)), const((1, D_MODEL)),
        const((N_EXPERTS, D_MODEL)), const((N_EXPERTS, 1)),
        const((TOK_TILE, TOK_TILE)),
    ]
    out_shape = [
        jax.ShapeDtypeStruct((t_all * TILE_ROWS, LANES), F32),
        jax.ShapeDtypeStruct((t_all * PACK_ROWS, LANES), jnp.uint32),
        jax.ShapeDtypeStruct((n_seq * WINDOW, KV_WIDTH), F32),
        jax.ShapeDtypeStruct((n_seq * WINDOW, KV_WIDTH), F32),
        jax.ShapeDtypeStruct((TOK_TILE, KV_WIDTH), F32),
        jax.ShapeDtypeStruct((TOK_TILE, KV_WIDTH), F32),
        jax.ShapeDtypeStruct((TOK_TILE, GM_WIDTH), F32),
        jax.ShapeDtypeStruct((8, t_all), jnp.int32),
        jax.ShapeDtypeStruct((8, t_all), F32),
        jax.ShapeDtypeStruct((N_EXPERTS, LANES), jnp.int32),
    ]
    out_specs = [
        pl.BlockSpec((TOK_TILE * TILE_ROWS, LANES), lambda i: (i, 0)),
        pl.BlockSpec((TOK_TILE * PACK_ROWS, LANES), lambda i: (i, 0)),
        pl.BlockSpec((WINDOW, KV_WIDTH), seq_map), pl.BlockSpec((WINDOW, KV_WIDTH), seq_map),
        const((TOK_TILE, KV_WIDTH)), const((TOK_TILE, KV_WIDTH)),
        const((TOK_TILE, GM_WIDTH)),
        pl.BlockSpec((8, TOK_TILE), lambda i: (0, i)),
        pl.BlockSpec((8, TOK_TILE), lambda i: (0, i)),
        const((N_EXPERTS, LANES)),
    ]
    if not has_sample:
        del out_shape[4:7], out_specs[4:7]
    kd_rows = (TOK_TILE // CHUNK) * KEYS_PER_CHUNK
    scratch = [
        pltpu.VMEM((N_KV_HEADS, kd_rows, LANES), BF16),
        pltpu.VMEM((N_KV_HEADS, kd_rows, LANES), BF16),
        pltpu.VMEM((N_Q_HEADS, TOK_TILE, LANES), BF16),
        pltpu.VMEM((TOK_TILE, ATT_WIDTH), F32),
        pltpu.VMEM((TOK_TILE, GM_WIDTH), F32),
        pltpu.VMEM((N_EXPERTS, LANES), F32),
        pltpu.VMEM((D_MODEL, D_IN), BF16),
        pltpu.VMEM((D_MIX, D_MODEL), BF16),
    ]
    return pl.pallas_call(
        functools.partial(_mixer_kernel, n_prompt_tiles, tiles_per_seq, alpha, has_sample, split_x),
        grid=(n_tiles,),
        in_specs=in_specs, out_specs=out_specs, out_shape=out_shape,
        scratch_shapes=scratch,
        compiler_params=pltpu.CompilerParams(dimension_semantics=("arbitrary",), vmem_limit_bytes=VMEM_LIMIT),
        name="mixer",
    )(sinks, x_main, x_tail, w_in, cos_t, sa_t, sb_t, ck, cv, lng, lnb, ws, bst, na, nb, w_o, l1g, l1b, wrt, rb,
      jnp.triu(jnp.ones((TOK_TILE, TOK_TILE), BF16)))


def _expert_kernel(te_ref, nt_ref, xs_ref, wg_ref, wu_ref, wd_ref, out_ref, wg_s, wu_s, wd_s):
    i = pl.program_id(0)
    changed = jnp.logical_or(i == 0, te_ref[i] != te_ref[jnp.maximum(i - 1, 0)])

    @pl.when(changed)
    def _():
        wg_s[...] = wg_ref[0, 0].astype(BF16)
        wu_s[...] = wu_ref[0, 0].astype(BF16)
        wd_s[...] = wd_ref[0, 0].astype(BF16)

    @pl.when(i < nt_ref[0])
    def _():
        xs = _load_packed_tiles(xs_ref, EXPERT_TILE, BF16)
        g = jnp.dot(xs, wg_s[...], preferred_element_type=F32)
        u = jnp.dot(xs, wu_s[...], preferred_element_type=F32)
        hmid = (g * jax.nn.sigmoid(g)) * u
        _store_packed_tiles(out_ref, jnp.dot(hmid.astype(BF16), wd_s[...], preferred_element_type=F32))


def _expert_call(tile_expert, n_used, xs, wg, wu, wd, layer):
    n_tiles = xs.shape[0] // (EXPERT_TILE * PACK_ROWS)
    tile_blk = pl.BlockSpec((EXPERT_TILE * PACK_ROWS, LANES), lambda i, te, nt: (jnp.minimum(i, nt[0] - 1), 0))
    w_blk = lambda shape: pl.BlockSpec((1, 1) + shape, lambda i, te, nt: (layer, te[i], 0, 0))
    grid_spec = pltpu.PrefetchScalarGridSpec(
        num_scalar_prefetch=2,
        grid=(n_tiles,),
        in_specs=[
            tile_blk,
            w_blk((D_MODEL, D_EXPERT)), w_blk((D_MODEL, D_EXPERT)), w_blk((D_EXPERT, D_MODEL)),
        ],
        out_specs=tile_blk,
        scratch_shapes=[
            pltpu.VMEM((D_MODEL, D_EXPERT), BF16),
            pltpu.VMEM((D_MODEL, D_EXPERT), BF16),
            pltpu.VMEM((D_EXPERT, D_MODEL), BF16),
        ],
    )
    return pl.pallas_call(
        _expert_kernel,
        grid_spec=grid_spec,
        out_shape=jax.ShapeDtypeStruct(xs.shape, jnp.uint32),
        compiler_params=pltpu.CompilerParams(dimension_semantics=("arbitrary",), vmem_limit_bytes=VMEM_LIMIT),
        name="experts",
    )(tile_expert, n_used, xs, wg, wu, wd)


ROWS_PER_STEP = TOK_TILE * TOP_K


def _token_tile(ref, idx):
    return ref.at[pl.ds(pl.multiple_of(idx * TILE_ROWS, TILE_ROWS), TILE_ROWS)]


def _dispatch_kernel(n_tok_steps, t_all, pos_ref, pad_ref, x1t_ref, xs_hbm, zero_ref, sems):
    j = pl.program_id(0)

    @pl.when(j == 0)
    def _():
        zero_ref[...] = jnp.zeros_like(zero_ref)

    @pl.when(j < n_tok_steps)
    def _():
        def body(tt, c):
            t = j * TOK_TILE + tt
            for k in range(TOP_K):
                pltpu.make_async_copy(_token_tile(x1t_ref, tt), _token_tile(xs_hbm, pos_ref[k * t_all + t]),
                                      sems.at[k]).start()
            return c
        lax.fori_loop(0, TOK_TILE, body, 0, unroll=8)

    @pl.when(j >= n_tok_steps)
    def _():
        def body(q, c):
            for k in range(TOP_K):
                row = pad_ref[(j - n_tok_steps) * ROWS_PER_STEP + k * TOK_TILE + q]
                pltpu.make_async_copy(zero_ref, _token_tile(xs_hbm, row), sems.at[k]).start()
            return c
        lax.fori_loop(0, TOK_TILE, body, 0, unroll=8)

    for k in range(TOP_K):
        pltpu.make_async_copy(x1t_ref, xs_hbm.at[pl.ds(0, TOK_TILE * TILE_ROWS)], sems.at[k]).wait()


def _dispatch_call(pos_flat, pad_rows, x1t, n_sorted_rows):
    t_all = x1t.shape[0] // TILE_ROWS
    n_tok_steps = t_all // TOK_TILE
    n_steps = n_tok_steps + pad_rows.shape[0] // ROWS_PER_STEP
    grid_spec = pltpu.PrefetchScalarGridSpec(
        num_scalar_prefetch=2,
        grid=(n_steps,),
        in_specs=[pl.BlockSpec((TOK_TILE * TILE_ROWS, LANES),
                               lambda j, p, q: (jnp.minimum(j, n_tok_steps - 1), 0))],
        out_specs=pl.BlockSpec(memory_space=pl.ANY),
        scratch_shapes=[pltpu.VMEM((TILE_ROWS, LANES), F32), pltpu.SemaphoreType.DMA((TOP_K,))],
    )
    return pl.pallas_call(
        functools.partial(_dispatch_kernel, n_tok_steps, t_all),
        grid_spec=grid_spec,
        out_shape=jax.ShapeDtypeStruct((n_sorted_rows * TILE_ROWS, LANES), F32),
        compiler_params=pltpu.CompilerParams(dimension_semantics=("arbitrary",)),
        name="dispatch",
    )(pos_flat, pad_rows, x1t)


def _combine_kernel(alpha, t_all, n_main_tiles, pos_ref, x1t_ref, w_ref, g_ref, b_ref, outs_hbm, *rest):
    if n_main_tiles is None:
        out_ref, gbuf, sems = rest
        tail_ref = None
    else:
        out_ref, tail_ref, gbuf, sems = rest
    j = pl.program_id(0)
    slot = j % 2
    k_rows = TOK_TILE * TILE_ROWS

    n = pl.num_programs(0)

    def wait(s):
        pltpu.make_async_copy(outs_hbm.at[pl.ds(0, TOP_K * k_rows)], gbuf.at[s], sems.at[s]).wait()

    def issue(step, s):
        def body(tt, c):
            t = step * TOK_TILE + tt
            for k in range(TOP_K):
                pltpu.make_async_copy(_token_tile(outs_hbm, pos_ref[k * t_all + t]),
                                      _token_tile(gbuf.at[s], k * TOK_TILE + tt), sems.at[s]).start()
            return c
        lax.fori_loop(0, TOK_TILE, body, 0, unroll=8)

    @pl.when(j == 0)
    def _():
        issue(0, 0)

    @pl.when(j + 1 < n)
    def _():
        issue(j + 1, 1 - slot)

    wait(slot)

    w = w_ref[...]
    y = (w[:, 0:1] * _load_token_tiles(gbuf.at[slot], TOK_TILE)
         + w[:, 1:2] * _load_token_tiles(gbuf.at[slot], TOK_TILE, k_rows))
    x1 = _load_token_tiles(x1t_ref, TOK_TILE)
    res = _layer_norm(alpha * x1 + y, g_ref[...], b_ref[...])
    if tail_ref is None:
        out_ref[...] = res
    else:
        @pl.when(j < n_main_tiles)
        def _():
            out_ref[...] = res

        @pl.when(j >= n_main_tiles)
        def _():
            tail_ref[...] = res


def _combine_call(pos_flat, x1t, w_col, g, b, outs, alpha, n_main_tiles=None):
    t_all = x1t.shape[0] // TILE_ROWS
    n_tiles = t_all // TOK_TILE
    if n_main_tiles is None:
        out_specs = pl.BlockSpec((TOK_TILE, D_MODEL), lambda i, p: (i, 0))
        out_shape = jax.ShapeDtypeStruct((t_all, D_MODEL), F32)
    else:
        assert n_tiles == n_main_tiles + 1
        out_specs = [pl.BlockSpec((TOK_TILE, D_MODEL), lambda i, p: (jnp.minimum(i, n_main_tiles - 1), 0)),
                     pl.BlockSpec((TOK_TILE, D_MODEL), lambda i, p: (0, 0))]
        out_shape = [jax.ShapeDtypeStruct((n_main_tiles * TOK_TILE, D_MODEL), F32),
                     jax.ShapeDtypeStruct((TOK_TILE, D_MODEL), F32)]
    grid_spec = pltpu.PrefetchScalarGridSpec(
        num_scalar_prefetch=1,
        grid=(t_all // TOK_TILE,),
        in_specs=[
            pl.BlockSpec((TOK_TILE * TILE_ROWS, LANES), lambda i, p: (i, 0)),
            pl.BlockSpec((TOK_TILE, TOP_K), lambda i, p: (i, 0)),
            pl.BlockSpec((1, D_MODEL), lambda i, p: (0, 0)),
            pl.BlockSpec((1, D_MODEL), lambda i, p: (0, 0)),
            pl.BlockSpec(memory_space=pl.ANY),
        ],
        out_specs=out_specs,
        scratch_shapes=[pltpu.VMEM((2, TOP_K * TOK_TILE * TILE_ROWS, LANES), F32), pltpu.SemaphoreType.DMA((2,))],
    )
    return pl.pallas_call(
        functools.partial(_combine_kernel, alpha, t_all, n_main_tiles),
        grid_spec=grid_spec,
        out_shape=out_shape,
        compiler_params=pltpu.CompilerParams(dimension_semantics=("arbitrary",), vmem_limit_bytes=VMEM_LIMIT),
        name="combine",
    )(pos_flat, x1t, w_col, g, b, outs)


SC_CHUNK = 96
SC_PAD_CHUNK = 32


def _sc_workers():
    info = plsc.get_sparse_core_info()
    return info.num_cores, info.num_cores * info.num_subcores


def _sc_split(per_w):
    n_full = per_w // SC_CHUNK
    tail = per_w - n_full * SC_CHUNK
    assert tail % 8 == 0
    return n_full, tail


def _sc_dispatch_call(pos_flat, pad_rows, x_tiles, tile_rows, n_sorted_rows):
    t_all = x_tiles.shape[0] // tile_rows
    dt = x_tiles.dtype
    nc, nw = _sc_workers()
    per_w = t_all // nw
    n_ch, tail = _sc_split(per_w)
    n_pad = pad_rows.shape[0]
    pad_per_w = n_pad // nw
    n_pch = pad_per_w // SC_PAD_CHUNK
    assert per_w * nw == t_all and n_pch * SC_PAD_CHUNK * nw == n_pad
    zeros = jnp.zeros((SC_PAD_CHUNK, tile_rows, LANES), dt)
    tail_scratch = [] if tail == 0 else [
        pltpu.VMEM((tail,), jnp.int32), pltpu.VMEM((tail,), jnp.int32), pltpu.VMEM((tail, tile_rows, LANES), dt)]

    @functools.partial(
        pl.kernel, mesh=plsc.VectorSubcoreMesh(core_axis_name="c", subcore_axis_name="s"),
        out_type=jax.ShapeDtypeStruct((n_sorted_rows, tile_rows, LANES), dt),
        scratch_types=[
            pltpu.VMEM((SC_CHUNK,), jnp.int32), pltpu.VMEM((SC_CHUNK,), jnp.int32),
            pltpu.VMEM((SC_CHUNK, tile_rows, LANES), dt),
            pltpu.VMEM((SC_PAD_CHUNK,), jnp.int32),
            pltpu.VMEM((SC_PAD_CHUNK, tile_rows, LANES), dt),
        ] + tail_scratch,
    )
    def k(pos_hbm, pad_hbm, x_hbm, z_hbm, xs_hbm, i0_v, i1_v, rows_v, ip_v, z_v, *tail_refs):
        wid = lax.axis_index("s") * nc + lax.axis_index("c")

        def move(b, n, idx0, idx1, rows):
            pltpu.sync_copy(pos_hbm.at[pl.ds(b, n)], idx0)
            pltpu.sync_copy(pos_hbm.at[pl.ds(t_all + b, n)], idx1)
            pltpu.sync_copy(x_hbm.at[pl.ds(b, n)], rows)
            pltpu.sync_copy(rows, xs_hbm.at[idx0])
            pltpu.sync_copy(rows, xs_hbm.at[idx1])

        @pl.loop(0, n_ch)
        def _(j):
            move(pl.multiple_of(wid * per_w + j * SC_CHUNK, 8), SC_CHUNK, i0_v, i1_v, rows_v)

        if tail:
            move(pl.multiple_of(wid * per_w + n_ch * SC_CHUNK, 8), tail, *tail_refs)

        pltpu.sync_copy(z_hbm, z_v)

        @pl.loop(0, n_pch)
        def _(j):
            b = pl.multiple_of(wid * pad_per_w + j * SC_PAD_CHUNK, 8)
            pltpu.sync_copy(pad_hbm.at[pl.ds(b, SC_PAD_CHUNK)], ip_v)
            pltpu.sync_copy(z_v, xs_hbm.at[ip_v])

    xs = k(pos_flat, pad_rows, x_tiles.reshape(t_all, tile_rows, LANES), zeros)
    return xs.reshape(n_sorted_rows * tile_rows, LANES)


def _sc_gather_call(pos_flat, outs, tile_rows):
    n_assign = pos_flat.shape[0]
    dt = outs.dtype
    nc, nw = _sc_workers()
    per_w = n_assign // nw
    n_ch, tail = _sc_split(per_w)
    assert per_w * nw == n_assign
    tail_scratch = [] if tail == 0 else [pltpu.VMEM((tail,), jnp.int32), pltpu.VMEM((tail, tile_rows, LANES), dt)]

    @functools.partial(
        pl.kernel, mesh=plsc.VectorSubcoreMesh(core_axis_name="c", subcore_axis_name="s"),
        out_type=jax.ShapeDtypeStruct((n_assign, tile_rows, LANES), dt),
        scratch_types=[pltpu.VMEM((SC_CHUNK,), jnp.int32),
                       pltpu.VMEM((SC_CHUNK, tile_rows, LANES), dt)] + tail_scratch,
    )
    def k(pos_hbm, o_hbm, y_hbm, i_v, rows_v, *tail_refs):
        wid = lax.axis_index("s") * nc + lax.axis_index("c")

        def move(b, n, idx, rows):
            pltpu.sync_copy(pos_hbm.at[pl.ds(b, n)], idx)
            pltpu.sync_copy(o_hbm.at[idx], rows)
            pltpu.sync_copy(rows, y_hbm.at[pl.ds(b, n)])

        @pl.loop(0, n_ch)
        def _(j):
            move(pl.multiple_of(wid * per_w + j * SC_CHUNK, 8), SC_CHUNK, i_v, rows_v)

        if tail:
            move(pl.multiple_of(wid * per_w + n_ch * SC_CHUNK, 8), tail, *tail_refs)

    y = k(pos_flat, outs.reshape(-1, tile_rows, LANES))
    return y.reshape(n_assign * tile_rows, LANES)


def _combine_dense_kernel(alpha, n_main_tiles, x1t_ref, y0_ref, y1_ref, w_ref, g_ref, b_ref, out_ref, *tail):
    j = pl.program_id(0)
    w = w_ref[...]
    y = (w[:, 0:1] * _load_packed_tiles(y0_ref, TOK_TILE, F32)
         + w[:, 1:2] * _load_packed_tiles(y1_ref, TOK_TILE, F32))
    res = _layer_norm(alpha * _load_token_tiles(x1t_ref, TOK_TILE) + y, g_ref[...], b_ref[...])
    if n_main_tiles is None:
        out_ref[...] = res
    else:
        @pl.when(j < n_main_tiles)
        def _():
            out_ref[...] = res

        @pl.when(j >= n_main_tiles)
        def _():
            tail[0][...] = res


def _combine_dense_call(x1t, y_pair, w_col, g, b, alpha, n_main_tiles=None):
    t_all = x1t.shape[0] // TILE_ROWS
    n_tiles = t_all // TOK_TILE
    if n_main_tiles is None:
        out_specs = pl.BlockSpec((TOK_TILE, D_MODEL), lambda i: (i, 0))
        out_shape = jax.ShapeDtypeStruct((t_all, D_MODEL), F32)
    else:
        assert n_tiles == n_main_tiles + 1
        out_specs = [pl.BlockSpec((TOK_TILE, D_MODEL), lambda i: (jnp.minimum(i, n_main_tiles - 1), 0)),
                     pl.BlockSpec((TOK_TILE, D_MODEL), lambda i: (0, 0))]
        out_shape = [jax.ShapeDtypeStruct((n_main_tiles * TOK_TILE, D_MODEL), F32),
                     jax.ShapeDtypeStruct((TOK_TILE, D_MODEL), F32)]
    tile = (TOK_TILE * TILE_ROWS, LANES)
    packed = (TOK_TILE * PACK_ROWS, LANES)
    return pl.pallas_call(
        functools.partial(_combine_dense_kernel, alpha, n_main_tiles),
        grid=(n_tiles,),
        in_specs=[
            pl.BlockSpec(tile, lambda i: (i, 0)),
            pl.BlockSpec(packed, lambda i: (i, 0)),
            pl.BlockSpec(packed, lambda i: (i + n_tiles, 0)),
            pl.BlockSpec((TOK_TILE, TOP_K), lambda i: (i, 0)),
            pl.BlockSpec((1, D_MODEL), lambda i: (0, 0)),
            pl.BlockSpec((1, D_MODEL), lambda i: (0, 0)),
        ],
        out_specs=out_specs,
        out_shape=out_shape,
        compiler_params=pltpu.CompilerParams(dimension_semantics=("arbitrary",), vmem_limit_bytes=VMEM_LIMIT),
        name="combine",
    )(x1t, y_pair, y_pair, w_col, g, b)


def _rope_tables(seq, dec_seq, past_len):
    half = HEAD_DIM // 2
    inv = ROPE_THETA ** (-jnp.arange(half, dtype=F32) / half)
    pos = jnp.concatenate([jnp.arange(seq), past_len + (jnp.arange(TOK_TILE) % dec_seq)])
    ang = pos.astype(F32)[:, None] * inv[None, :]
    cos = jnp.cos(ang)
    sin = jnp.sin(ang)
    zero = jnp.zeros_like(sin)
    reps = LANES // HEAD_DIM
    cos_t = jnp.tile(jnp.concatenate([cos, cos], -1), (1, reps))
    sa_t = jnp.tile(jnp.concatenate([-sin, zero], -1), (1, reps))
    sb_t = jnp.tile(jnp.concatenate([zero, sin], -1), (1, reps))
    return cos_t, sa_t, sb_t


def _inclusive_cumsum(v):
    n = v.shape[0]
    tri = jnp.arange(n)[None, :] <= jnp.arange(n)[:, None]
    return jnp.sum(jnp.where(tri, v[None, :], 0), axis=1)


def _dispatch_plan(ridx, counts, n_tiles):
    t_all = ridx.shape[1]
    tiles_e = (counts + EXPERT_TILE - 1) // EXPERT_TILE
    tile_end = _inclusive_cumsum(tiles_e)
    offs = (tile_end - tiles_e) * EXPERT_TILE
    n_used = tile_end[-1:]
    tile_expert = jnp.minimum(
        jnp.sum((jnp.arange(n_tiles)[:, None] >= tile_end[None, :]).astype(jnp.int32), axis=1), N_EXPERTS - 1)
    experts = jnp.arange(N_EXPERTS)
    pos = ridx[TOP_K:2 * TOP_K] + jnp.sum(
        jnp.where(ridx[:TOP_K, :, None] == experts[None, None, :], offs[None, None, :], 0), axis=-1)
    gap_start = jnp.concatenate([offs + counts, n_used * EXPERT_TILE])
    gap_len = jnp.concatenate([tiles_e * EXPERT_TILE - counts, (n_tiles - n_used) * EXPERT_TILE])
    gap_end_q = _inclusive_cumsum(gap_len)
    q = jnp.arange(n_tiles * EXPERT_TILE - TOP_K * t_all)
    gap = jnp.sum((q[:, None] >= gap_end_q[None, :]).astype(jnp.int32), axis=1)
    in_gap = gap[:, None] == jnp.arange(N_EXPERTS + 1)[None, :]
    pad_rows = q + jnp.sum(jnp.where(in_gap, (gap_start - gap_end_q + gap_len)[None, :], 0), axis=1)
    return (pos.reshape(-1).astype(jnp.int32), pad_rows.astype(jnp.int32),
            tile_expert.astype(jnp.int32), n_used.astype(jnp.int32))


def kernel(x_prompt, x_sample, cache_k, cache_v, w_in, sinks, gm_ln_g, gm_ln_b, gm_ws, gm_bs,
           out_norm_a, out_norm_b, w_o, ln1_g, ln1_b, w_router, router_bias,
           w_gate, w_up, w_down, ln2_g, ln2_b):
    batch, seq, _ = x_prompt.shape
    dec_batch, dec_seq, _ = x_sample.shape
    depth = w_in.shape[0]
    past_len = PAST_LEN
    assert dec_batch * dec_seq == TOK_TILE and dec_seq == CHUNK and seq % TOK_TILE == 0
    assert cache_k.shape[2] == WINDOW
    alpha = (2 * depth) ** 0.25
    tiles_per_seq = seq // TOK_TILE

    tables = _rope_tables(seq, dec_seq, past_len)
    wrt = w_router.T
    rb = router_bias.reshape(N_EXPERTS, 1)

    streams = [
        dict(x=(x_prompt.reshape(batch * seq, D_MODEL), x_sample.reshape(-1, D_MODEL)),
             n_seq=batch, has_sample=True),
    ]
    kp, vp, ks, vs, gms = [], [], [], [], []
    for l in range(depth):
        layer_args = (
            sinks[l], w_in,
            cache_k[l].reshape(dec_batch, WINDOW, KV_WIDTH), cache_v[l].reshape(dec_batch, WINDOW, KV_WIDTH),
            gm_ln_g[l].reshape(1, GM_WIDTH), gm_ln_b[l].reshape(1, GM_WIDTH),
            gm_ws[l], gm_bs[l].T,
            out_norm_a[l].reshape(1, ATT_WIDTH), out_norm_b[l].reshape(1, GM_WIDTH),
            w_o,
            ln1_g[l].reshape(1, D_MODEL), ln1_b[l].reshape(1, D_MODEL), wrt, rb)
        k_tails, v_tails = [], []
        for st in streams:
            n_prompt = st["n_seq"] * seq
            n_prompt_tiles = n_prompt // TOK_TILE
            t_all = n_prompt + (TOK_TILE if st["has_sample"] else 0)
            n_exp_tiles = (TOP_K * t_all) // EXPERT_TILE + N_EXPERTS
            res = _mixer_call(layer_args, st["x"], tables, n_prompt_tiles, tiles_per_seq, alpha, l)
            if st["has_sample"]:
                x1t, x1p, k_tail, v_tail, k_s, v_s, vn_s, ridx, rw, counts = res
            else:
                x1t, x1p, k_tail, v_tail, ridx, rw, counts = res

            pos, pad_rows, tile_expert, n_used = _dispatch_plan(ridx, counts[:, 0], n_exp_tiles)
            xs = _sc_dispatch_call(pos, pad_rows, x1p, PACK_ROWS, n_exp_tiles * EXPERT_TILE)
            outs = _expert_call(tile_expert, n_used, xs, w_gate, w_up, w_down, l)
            y_pair = _sc_gather_call(pos, outs, PACK_ROWS)
            split = n_prompt_tiles if (l == depth - 1 and st["has_sample"]) else None
            st["x"] = _combine_dense_call(x1t, y_pair, rw[:TOP_K].T, ln2_g[l].reshape(1, D_MODEL),
                                          ln2_b[l].reshape(1, D_MODEL), alpha, n_main_tiles=split)

            k_tails.append(k_tail)
            v_tails.append(v_tail)
            if st["has_sample"]:
                ks.append(k_s.reshape(dec_batch, dec_seq, N_KV_HEADS, HEAD_DIM))
                vs.append(v_s.reshape(dec_batch, dec_seq, N_KV_HEADS, HEAD_DIM))
                gms.append(vn_s.reshape(dec_batch, dec_seq, GM_WIDTH))
        kp.append(jnp.concatenate(k_tails, axis=0).reshape(batch, WINDOW, N_KV_HEADS, HEAD_DIM))
        vp.append(jnp.concatenate(v_tails, axis=0).reshape(batch, WINDOW, N_KV_HEADS, HEAD_DIM))

    y_prompt = jnp.concatenate([st["x"][0] if st["has_sample"] else st["x"] for st in streams],
                               axis=0).reshape(batch, seq, D_MODEL)
    y_sample = streams[-1]["x"][1].reshape(dec_batch, dec_seq, D_MODEL)
    return (y_prompt, y_sample, jnp.stack(kp), jnp.stack(vp), jnp.stack(ks), jnp.stack(vs), jnp.stack(gms))
```

```python
import functools

import jax
import jax.numpy as jnp
from jax import lax
from jax.experimental import pallas as pl
from jax.experimental.pallas import tpu as pltpu
from jax.experimental.pallas import tpu_sc as plsc

D_MODEL = 1024
CHUNK = 64
WINDOW = 128
HEAD_DIM = 64
N_Q_HEADS = 8
N_KV_HEADS = 2
Q_PER_KV = N_Q_HEADS // N_KV_HEADS
PAST_LEN = 2048
ATT_WIDTH = N_Q_HEADS * HEAD_DIM
KV_WIDTH = N_KV_HEADS * HEAD_DIM
ROPE_THETA = 10000.0
GM_GROUPS = 4
GM_CH = 128
GM_WIDTH = GM_GROUPS * GM_CH
GM_CHUNK = 128
D_MIX = ATT_WIDTH + GM_WIDTH
D_IN = ATT_WIDTH + 2 * KV_WIDTH + 2 * GM_WIDTH
N_EXPERTS = 16
N_EXPERT_GROUPS = 4
EXPERTS_PER_GROUP = N_EXPERTS // N_EXPERT_GROUPS
TOP_K = 2
D_EXPERT = 512
LN_EPS = 1e-5
NEG_INF = -1e30

LANES = 128
TILE_ROWS = D_MODEL // LANES
PACK_ROWS = TILE_ROWS // 2
HI_HALF = 0xFFFF0000
TOK_TILE = 512
EXPERT_TILE = 512
KEYS_PER_CHUNK = WINDOW + CHUNK
VMEM_LIMIT = 60 * 1024 * 1024

F32 = jnp.float32
BF16 = jnp.bfloat16


def _layer_norm(x, g, b):
    mu = jnp.mean(x, axis=-1, keepdims=True)
    d = x - mu
    var = jnp.mean(d * d, axis=-1, keepdims=True)
    return d * lax.rsqrt(var + LN_EPS) * g + b


def _rms_norm(x, g):
    return x * lax.rsqrt(jnp.mean(x * x, axis=-1, keepdims=True) + LN_EPS) * g


def _gelu(x):
    return 0.5 * x * (1.0 + lax.erf(x * (0.5 ** 0.5)))


def _store_token_tiles(ref, x):
    n = x.shape[0]
    for c in range(TILE_ROWS):
        ref[pl.ds(c, n, stride=TILE_ROWS), :] = x[:, c * LANES:(c + 1) * LANES]


def _load_token_tiles(ref, n, row0=0):
    return jnp.concatenate([ref[pl.ds(row0 + c, n, stride=TILE_ROWS), :] for c in range(TILE_ROWS)], axis=-1)


def _bf16_bits(x):
    return lax.bitcast_convert_type(x.astype(BF16).astype(F32), jnp.uint32)


def _store_packed_tiles(ref, x):
    n = x.shape[0]
    for s in range(PACK_ROWS):
        lo = _bf16_bits(x[:, (2 * s) * LANES:(2 * s + 1) * LANES]) >> 16
        hi = _bf16_bits(x[:, (2 * s + 1) * LANES:(2 * s + 2) * LANES]) & jnp.uint32(HI_HALF)
        ref[pl.ds(s, n, stride=PACK_ROWS), :] = hi | lo


def _load_packed_tiles(ref, n, dtype):
    cols = []
    for s in range(PACK_ROWS):
        w = ref[pl.ds(s, n, stride=PACK_ROWS), :]
        cols.append(lax.bitcast_convert_type(w << 16, F32).astype(dtype))
        cols.append(lax.bitcast_convert_type(w & jnp.uint32(HI_HALF), F32).astype(dtype))
    return jnp.concatenate(cols, axis=-1)


def _first_index_of_max(rows):
    m = rows[0]
    for r in rows[1:]:
        m = jnp.maximum(m, r)
    idx = jnp.full(m.shape, len(rows), jnp.int32)
    for e in reversed(range(len(rows))):
        idx = jnp.where(rows[e] == m, e, idx)
    return m, idx


def _mixer_kernel(n_prompt_tiles, tiles_per_seq, alpha, has_sample, first_layer, sinks_ref, *refs):
    n_x = 2 if first_layer else 6
    x_refs, refs = refs[:n_x], refs[n_x:]
    (w_in_ref, cos_ref, sa_ref, sb_ref, ck_ref, cv_ref, lng_ref, lnb_ref, ws_ref, bst_ref, na_ref, nb_ref, w_o_ref,
     l1g_ref, l1b_ref, wrt_ref, rb_ref, tri_ref, x1t_ref, x1p_ref, kt_ref, vt_ref) = refs[:22]
    rest = refs[22:]
    if has_sample:
        (ks_ref, vs_ref, vn_ref, ridx_ref, rw_ref, counts_ref,
         kd_ref, vd_ref, qm_ref, att_ref, gm_ref, cnt_ref, w_in_s, w_o_s, x_s) = rest
    else:
        (ridx_ref, rw_ref, counts_ref,
         kd_ref, vd_ref, qm_ref, att_ref, gm_ref, cnt_ref, w_in_s, w_o_s, x_s) = rest
    i = pl.program_id(0)
    is_sample = i >= n_prompt_tiles
    seq_start = jnp.logical_and(jnp.logical_not(is_sample), (i % tiles_per_seq) == 0)
    seq_end = jnp.logical_and(jnp.logical_not(is_sample), (i % tiles_per_seq) == tiles_per_seq - 1)
    tm = TOK_TILE

    @pl.when(i == 0)
    def _():
        w_in_s[...] = w_in_ref[0].astype(BF16)
        w_o_s[...] = w_o_ref[0].astype(BF16)

    if first_layer:
        xp_ref, xs_ref = x_refs
        x_s[...] = jnp.where(is_sample, xs_ref[...], xp_ref[...])
    else:
        x1_prev_ref, y0_ref, y1_ref, wc_ref, l2g_ref, l2b_ref = x_refs
        wc = wc_ref[...]
        y_moe = (wc[:, 0:1] * _load_packed_tiles(y0_ref, tm, F32)
                 + wc[:, 1:2] * _load_packed_tiles(y1_ref, tm, F32))
        x_s[...] = _layer_norm(alpha * _load_token_tiles(x1_prev_ref, tm) + y_moe, l2g_ref[...], l2b_ref[...])

    h = jnp.dot(x_s[...].astype(BF16), w_in_s[...], preferred_element_type=F32)

    cos = cos_ref[...]
    sa = sa_ref[...]
    sb = sb_ref[...]

    def rope(blk):
        return blk * cos + pltpu.roll(blk, LANES - HEAD_DIM // 2, 1) * sa + pltpu.roll(blk, HEAD_DIM // 2, 1) * sb

    lane = lax.broadcasted_iota(jnp.int32, (1, LANES), 1)
    lo_half = lane < HEAD_DIM

    o_k = ATT_WIDTH
    o_v = o_k + KV_WIDTH
    o_u = o_v + KV_WIDTH
    o_g = o_u + GM_WIDTH

    k_rot = rope(h[:, o_k:o_v])
    v_new = h[:, o_v:o_u]

    @pl.when(seq_end)
    def _():
        kt_ref[...] = k_rot[tm - WINDOW:]
        vt_ref[...] = v_new[tm - WINDOW:]

    if has_sample:
        @pl.when(is_sample)
        def _():
            ks_ref[...] = k_rot
            vs_ref[...] = v_new

    def dup_heads(a):
        sw = pltpu.roll(a, HEAD_DIM, 1)
        return jnp.where(lo_half, a, sw).astype(BF16), jnp.where(lo_half, sw, a).astype(BF16)

    k_d = dup_heads(k_rot)
    v_d = dup_heads(v_new)

    @pl.when(jnp.logical_not(is_sample))
    def _():
        @pl.when(seq_start)
        def _():
            for g in range(N_KV_HEADS):
                kd_ref[g, 0:WINDOW, :] = jnp.zeros((WINDOW, LANES), BF16)
                vd_ref[g, 0:WINDOW, :] = jnp.zeros((WINDOW, LANES), BF16)

        @pl.when(jnp.logical_not(seq_start))
        def _():
            for g in range(N_KV_HEADS):
                kd_ref[g, 0:WINDOW, :] = kd_ref[g, tm:tm + WINDOW, :]
                vd_ref[g, 0:WINDOW, :] = vd_ref[g, tm:tm + WINDOW, :]

        for g in range(N_KV_HEADS):
            kd_ref[g, WINDOW:WINDOW + tm, :] = k_d[g]
            vd_ref[g, WINDOW:WINDOW + tm, :] = v_d[g]

    @pl.when(is_sample)
    def _():
        for b in range(tm // CHUNK):
            ck = dup_heads(ck_ref[b])
            cv = dup_heads(cv_ref[b])
            base = b * KEYS_PER_CHUNK
            for g in range(N_KV_HEADS):
                kd_ref[g, base:base + WINDOW, :] = ck[g]
                vd_ref[g, base:base + WINDOW, :] = cv[g]
                kd_ref[g, base + WINDOW:base + KEYS_PER_CHUNK, :] = k_d[g][b * CHUNK:(b + 1) * CHUNK]
                vd_ref[g, base + WINDOW:base + KEYS_PER_CHUNK, :] = v_d[g][b * CHUNK:(b + 1) * CHUNK]

    scale = HEAD_DIM ** -0.5
    for b in range(ATT_WIDTH // LANES):
        qb = (rope(h[:, b * LANES:(b + 1) * LANES]) * scale).astype(BF16)
        zero = jnp.zeros_like(qb)
        qm_ref[2 * b] = jnp.where(lo_half, qb, zero)
        qm_ref[2 * b + 1] = jnp.where(lo_half, zero, qb)

    key_stride = jnp.where(is_sample, KEYS_PER_CHUNK, CHUNK)
    col = lax.broadcasted_iota(jnp.int32, (1, KEYS_PER_CHUNK), 1)
    row_blk = lax.broadcasted_iota(jnp.int32, (Q_PER_KV * CHUNK, 1), 0) // CHUNK
    sink_cols = [
        jnp.where(row_blk == 0, sinks_ref[Q_PER_KV * g],
                  jnp.where(row_blk == 1, sinks_ref[Q_PER_KV * g + 1],
                            jnp.where(row_blk == 2, sinks_ref[Q_PER_KV * g + 2], sinks_ref[Q_PER_KV * g + 3])))
        for g in range(N_KV_HEADS)]

    for c in range(tm // CHUNK):
        r0 = c * CHUNK
        k0 = pl.multiple_of(c * key_stride, CHUNK)
        outs = []
        for g in range(N_KV_HEADS):
            q_st = jnp.concatenate([qm_ref[Q_PER_KV * g + r, pl.ds(r0, CHUNK), :] for r in range(Q_PER_KV)],
                                   axis=0)
            keys = kd_ref[g, pl.ds(k0, KEYS_PER_CHUNK), :]
            vals = vd_ref[g, pl.ds(k0, KEYS_PER_CHUNK), :]
            s = lax.dot_general(q_st, keys, (((1,), (1,)), ((), ())), preferred_element_type=F32)
            if c < WINDOW // CHUNK:
                first_valid = jnp.where(seq_start, (WINDOW // CHUNK - c) * CHUNK, 0)
                s = jnp.where(col >= first_valid, s, NEG_INF)
            sink = sink_cols[g]
            m = jnp.maximum(jnp.max(s, axis=-1, keepdims=True), sink)
            p = jnp.exp(s - m)
            denom = jnp.sum(p, axis=-1, keepdims=True) + jnp.exp(sink - m)
            o = jnp.dot(p.astype(BF16), vals, preferred_element_type=F32) * (1.0 / denom)
            for bb in range(2):
                outs.append(jnp.where(lo_half, o[(2 * bb) * CHUNK:(2 * bb + 1) * CHUNK],
                                      o[(2 * bb + 1) * CHUNK:(2 * bb + 2) * CHUNK]))
        for b in range(ATT_WIDTH // LANES):
            att_ref[pl.ds(r0, CHUNK), b * LANES:(b + 1) * LANES] = outs[b]

    prow = lax.broadcasted_iota(jnp.int32, (GM_CHUNK, GM_CHUNK), 0)
    pcol = lax.broadcasted_iota(jnp.int32, (GM_CHUNK, GM_CHUNK), 1)
    half = GM_CHUNK // 2
    tril = pcol <= prow
    same_blk = (prow < half) == (pcol < half)
    prow1 = lax.broadcasted_iota(jnp.int32, (GM_CHUNK, 1), 0)
    bst = bst_ref[...]
    bst_s = jnp.where(prow1 < half, bst, pltpu.roll(bst, half, 0))
    bias = jnp.where(is_sample, bst_s, bst)
    for g in range(GM_GROUPS):
        wg_ = ws_ref[g]
        w_s = jnp.where(prow < half, wg_, pltpu.roll(pltpu.roll(wg_, half, 0), half, 1))
        w_eff = jnp.where(is_sample, jnp.where(same_blk, w_s, 0.0), wg_)
        w_eff = jnp.where(tril, w_eff, 0.0).astype(BF16)
        cols = slice(g * GM_CH, (g + 1) * GM_CH)
        ug = _gelu(h[:, o_u + g * GM_CH:o_u + (g + 1) * GM_CH])
        vn = _layer_norm(_gelu(h[:, o_g + g * GM_CH:o_g + (g + 1) * GM_CH]), lng_ref[:, cols], lnb_ref[:, cols])

        if has_sample:
            @pl.when(is_sample)
            def _(cols=cols, vn=vn):
                vn_ref[:, cols] = vn

        vn_b = vn.astype(BF16)
        for n in range(tm // GM_CHUNK):
            rows = slice(n * GM_CHUNK, (n + 1) * GM_CHUNK)
            s = jnp.dot(w_eff, vn_b[rows], preferred_element_type=F32) + bias[:, g:g + 1]
            gm_ref[rows, cols] = ug[rows] * s

    mixed = jnp.concatenate([_rms_norm(att_ref[...], na_ref[...]), _rms_norm(gm_ref[...], nb_ref[...])], axis=-1)
    y = alpha * x_s[...] + jnp.dot(mixed.astype(BF16), w_o_s[...], preferred_element_type=F32)
    x1 = _layer_norm(y, l1g_ref[...], l1b_ref[...])
    _store_token_tiles(x1t_ref, x1)
    _store_packed_tiles(x1p_ref, x1)

    nt = (((1,), (1,)), ((), ()))
    wr = wrt_ref[...]
    wr_hi = wr.astype(BF16)
    wr_lo = (wr - wr_hi.astype(F32)).astype(BF16)
    x1_hi = x1.astype(BF16)
    x1_lo = (x1 - x1_hi.astype(F32)).astype(BF16)
    logits = (lax.dot_general(wr_hi, x1_hi, nt, preferred_element_type=F32)
              + (lax.dot_general(wr_hi, x1_lo, nt, preferred_element_type=F32)
                 + lax.dot_general(wr_lo, x1_hi, nt, preferred_element_type=F32)))
    scores = jax.nn.sigmoid(logits)
    sel = scores + rb_ref[...]
    sel_rows = [sel[e:e + 1, :] for e in range(N_EXPERTS)]
    sc_rows = [scores[e:e + 1, :] for e in range(N_EXPERTS)]
    grp = []
    for g in range(N_EXPERT_GROUPS):
        r = sel_rows[g * EXPERTS_PER_GROUP:(g + 1) * EXPERTS_PER_GROUP]
        best_pair = None
        for a in range(EXPERTS_PER_GROUP):
            for b in range(a + 1, EXPERTS_PER_GROUP):
                pair = r[a] + r[b]
                best_pair = pair if best_pair is None else jnp.maximum(best_pair, pair)
        grp.append(best_pair)
    _, best = _first_index_of_max(grp)
    masked = [jnp.where(best == (e // EXPERTS_PER_GROUP), sel_rows[e], NEG_INF) for e in range(N_EXPERTS)]
    _, e0 = _first_index_of_max(masked)
    masked2 = [jnp.where(e0 == e, -jnp.inf, masked[e]) for e in range(N_EXPERTS)]
    _, e1 = _first_index_of_max(masked2)
    w0 = jnp.zeros_like(sc_rows[0])
    w1 = jnp.zeros_like(sc_rows[0])
    for e in range(N_EXPERTS):
        w0 = jnp.where(e0 == e, sc_rows[e], w0)
        w1 = jnp.where(e1 == e, sc_rows[e], w1)
    wsum = w0 + w1
    rw_ref[...] = jnp.concatenate([w0 / wsum, w1 / wsum, jnp.zeros((6, tm), F32)], axis=0)

    @pl.when(i == 0)
    def _():
        cnt_ref[...] = jnp.zeros_like(cnt_ref)

    eid = lax.broadcasted_iota(jnp.int32, (N_EXPERTS, tm), 0)
    oh0 = (eid == e0).astype(F32)
    oh1 = (eid == e1).astype(F32)
    oh = oh0 + oh1
    incl = jnp.dot(oh.astype(BF16), tri_ref[...], preferred_element_type=F32)
    before = cnt_ref[:, 0:1] + incl - oh
    r0 = jnp.sum(oh0 * before, axis=0, keepdims=True).astype(jnp.int32)
    r1 = jnp.sum(oh1 * before, axis=0, keepdims=True).astype(jnp.int32)
    cnt = cnt_ref[...] + incl[:, tm - 1:tm]
    cnt_ref[...] = cnt
    counts_ref[...] = cnt.astype(jnp.int32)
    ridx_ref[...] = jnp.concatenate([e0, e1, r0, r1, jnp.zeros((4, tm), jnp.int32)], axis=0)


def _mixer_call(layer_args, x_all, tables, n_prompt_tiles, tiles_per_seq, alpha, layer):
    (sinks, w_in, ck, cv, lng, lnb, ws, bst, na, nb, w_o, l1g, l1b, wrt, rb) = layer_args
    cos_t, sa_t, sb_t = tables
    first_layer = len(x_all) == 2
    if first_layer:
        x_main, x_tail = x_all
        t_all = x_main.shape[0] + x_tail.shape[0]
        n_tiles = t_all // TOK_TILE
        x_args = [x_main, x_tail]
        x_specs = [pl.BlockSpec((TOK_TILE, D_MODEL), lambda i: (jnp.minimum(i, n_prompt_tiles - 1), 0)),
                   pl.BlockSpec((TOK_TILE, D_MODEL), lambda i: (0, 0))]
    else:
        x1t_prev, y_pair, w_col, l2g, l2b = x_all
        t_all = x1t_prev.shape[0] // TILE_ROWS
        n_tiles = t_all // TOK_TILE
        x_args = [x1t_prev, y_pair, y_pair, w_col, l2g, l2b]
        x_specs = [pl.BlockSpec((TOK_TILE * TILE_ROWS, LANES), lambda i: (i, 0)),
                   pl.BlockSpec((TOK_TILE * PACK_ROWS, LANES), lambda i: (i, 0)),
                   pl.BlockSpec((TOK_TILE * PACK_ROWS, LANES), lambda i: (i + n_tiles, 0)),
                   pl.BlockSpec((TOK_TILE, TOP_K), lambda i: (i, 0)),
                   pl.BlockSpec((1, D_MODEL), lambda i: (0, 0)), pl.BlockSpec((1, D_MODEL), lambda i: (0, 0))]
    n_seq_tiles = tiles_per_seq
    has_sample = n_tiles > n_prompt_tiles

    def const(shape):
        nd = len(shape)
        return pl.BlockSpec(shape, lambda i, _nd=nd: (0,) * _nd)

    def tab_map(i):
        return (jnp.where(i < n_prompt_tiles, i % n_seq_tiles, n_seq_tiles), 0)

    n_seq = n_prompt_tiles // n_seq_tiles

    def seq_map(i):
        return (jnp.minimum(i // n_seq_tiles, n_seq - 1), 0)

    row_blk = lambda w: pl.BlockSpec((TOK_TILE, w), lambda i: (i, 0))
    in_specs = [
        pl.BlockSpec(memory_space=pltpu.SMEM),
        *x_specs,
        pl.BlockSpec((1, D_MODEL, D_IN), lambda i: (layer, 0, 0)),
        pl.BlockSpec((TOK_TILE, LANES), tab_map),
        pl.BlockSpec((TOK_TILE, LANES), tab_map),
        pl.BlockSpec((TOK_TILE, LANES), tab_map),
        const(ck.shape), const(cv.shape),
        const((1, GM_WIDTH)), const((1, GM_WIDTH)),
        const((GM_GROUPS, GM_CHUNK, GM_CHUNK)), const((GM_CHUNK, GM_GROUPS)),
        const((1, ATT_WIDTH)), const((1, GM_WIDTH)),
        pl.BlockSpec((1, D_MIX, D_MODEL), lambda i: (layer, 0, 0)),
        const((1, D_MODEL)), const((1, D_MODEL)),
        const((N_EXPERTS, D_MODEL)), const((N_EXPERTS, 1)),
        const((TOK_TILE, TOK_TILE)),
    ]
    out_shape = [
        jax.ShapeDtypeStruct((t_all * TILE_ROWS, LANES), F32),
        jax.ShapeDtypeStruct((t_all * PACK_ROWS, LANES), jnp.uint32),
        jax.ShapeDtypeStruct((n_seq * WINDOW, KV_WIDTH), F32),
        jax.ShapeDtypeStruct((n_seq * WINDOW, KV_WIDTH), F32),
        jax.ShapeDtypeStruct((TOK_TILE, KV_WIDTH), F32),
        jax.ShapeDtypeStruct((TOK_TILE, KV_WIDTH), F32),
        jax.ShapeDtypeStruct((TOK_TILE, GM_WIDTH), F32),
        jax.ShapeDtypeStruct((8, t_all), jnp.int32),
        jax.ShapeDtypeStruct((8, t_all), F32),
        jax.ShapeDtypeStruct((N_EXPERTS, LANES), jnp.int32),
    ]
    out_specs = [
        pl.BlockSpec((TOK_TILE * TILE_ROWS, LANES), lambda i: (i, 0)),
        pl.BlockSpec((TOK_TILE * PACK_ROWS, LANES), lambda i: (i, 0)),
        pl.BlockSpec((WINDOW, KV_WIDTH), seq_map), pl.BlockSpec((WINDOW, KV_WIDTH), seq_map),
        const((TOK_TILE, KV_WIDTH)), const((TOK_TILE, KV_WIDTH)),
        const((TOK_TILE, GM_WIDTH)),
        pl.BlockSpec((8, TOK_TILE), lambda i: (0, i)),
        pl.BlockSpec((8, TOK_TILE), lambda i: (0, i)),
        const((N_EXPERTS, LANES)),
    ]
    if not has_sample:
        del out_shape[4:7], out_specs[4:7]
    kd_rows = (TOK_TILE // CHUNK) * KEYS_PER_CHUNK
    scratch = [
        pltpu.VMEM((N_KV_HEADS, kd_rows, LANES), BF16),
        pltpu.VMEM((N_KV_HEADS, kd_rows, LANES), BF16),
        pltpu.VMEM((N_Q_HEADS, TOK_TILE, LANES), BF16),
        pltpu.VMEM((TOK_TILE, ATT_WIDTH), F32),
        pltpu.VMEM((TOK_TILE, GM_WIDTH), F32),
        pltpu.VMEM((N_EXPERTS, LANES), F32),
        pltpu.VMEM((D_MODEL, D_IN), BF16),
        pltpu.VMEM((D_MIX, D_MODEL), BF16),
        pltpu.VMEM((TOK_TILE, D_MODEL), F32),
    ]
    return pl.pallas_call(
        functools.partial(_mixer_kernel, n_prompt_tiles, tiles_per_seq, alpha, has_sample, first_layer),
        grid=(n_tiles,),
        in_specs=in_specs, out_specs=out_specs, out_shape=out_shape,
        scratch_shapes=scratch,
        compiler_params=pltpu.CompilerParams(dimension_semantics=("arbitrary",), vmem_limit_bytes=VMEM_LIMIT),
        name="mixer",
    )(sinks, *x_args, w_in, cos_t, sa_t, sb_t, ck, cv, lng, lnb, ws, bst, na, nb, w_o, l1g, l1b, wrt, rb,
      jnp.triu(jnp.ones((TOK_TILE, TOK_TILE), BF16)))


def _expert_kernel(te_ref, nt_ref, xs_ref, wg_ref, wu_ref, wd_ref, out_ref, wg_s, wu_s, wd_s):
    i = pl.program_id(0)
    changed = jnp.logical_or(i == 0, te_ref[i] != te_ref[jnp.maximum(i - 1, 0)])

    @pl.when(changed)
    def _():
        wg_s[...] = wg_ref[0, 0].astype(BF16)
        wu_s[...] = wu_ref[0, 0].astype(BF16)
        wd_s[...] = wd_ref[0, 0].astype(BF16)

    @pl.when(i < nt_ref[0])
    def _():
        xs = _load_packed_tiles(xs_ref, EXPERT_TILE, BF16)
        g = jnp.dot(xs, wg_s[...], preferred_element_type=F32)
        u = jnp.dot(xs, wu_s[...], preferred_element_type=F32)
        hmid = (g * jax.nn.sigmoid(g)) * u
        _store_packed_tiles(out_ref, jnp.dot(hmid.astype(BF16), wd_s[...], preferred_element_type=F32))


def _expert_call(tile_expert, n_used, xs, wg, wu, wd, layer):
    n_tiles = xs.shape[0] // (EXPERT_TILE * PACK_ROWS)
    tile_blk = pl.BlockSpec((EXPERT_TILE * PACK_ROWS, LANES), lambda i, te, nt: (jnp.minimum(i, nt[0] - 1), 0))
    w_blk = lambda shape: pl.BlockSpec((1, 1) + shape, lambda i, te, nt: (layer, te[i], 0, 0))
    grid_spec = pltpu.PrefetchScalarGridSpec(
        num_scalar_prefetch=2,
        grid=(n_tiles,),
        in_specs=[
            tile_blk,
            w_blk((D_MODEL, D_EXPERT)), w_blk((D_MODEL, D_EXPERT)), w_blk((D_EXPERT, D_MODEL)),
        ],
        out_specs=tile_blk,
        scratch_shapes=[
            pltpu.VMEM((D_MODEL, D_EXPERT), BF16),
            pltpu.VMEM((D_MODEL, D_EXPERT), BF16),
            pltpu.VMEM((D_EXPERT, D_MODEL), BF16),
        ],
    )
    return pl.pallas_call(
        _expert_kernel,
        grid_spec=grid_spec,
        out_shape=jax.ShapeDtypeStruct(xs.shape, jnp.uint32),
        compiler_params=pltpu.CompilerParams(dimension_semantics=("arbitrary",), vmem_limit_bytes=VMEM_LIMIT),
        name="experts",
    )(tile_expert, n_used, xs, wg, wu, wd)


SC_CHUNK = 96
SC_PAD_CHUNK = 32


def _sc_workers():
    info = plsc.get_sparse_core_info()
    return info.num_cores, info.num_cores * info.num_subcores


def _sc_split(per_w):
    n_full = per_w // SC_CHUNK
    tail = per_w - n_full * SC_CHUNK
    assert tail % 8 == 0
    return n_full, tail


def _sc_dispatch_call(pos_flat, pad_rows, x_tiles, tile_rows, n_sorted_rows):
    t_all = x_tiles.shape[0] // tile_rows
    dt = x_tiles.dtype
    nc, nw = _sc_workers()
    per_w = t_all // nw
    n_ch, tail = _sc_split(per_w)
    n_pad = pad_rows.shape[0]
    pad_per_w = n_pad // nw
    n_pch = pad_per_w // SC_PAD_CHUNK
    assert per_w * nw == t_all and n_pch * SC_PAD_CHUNK * nw == n_pad
    zeros = jnp.zeros((SC_PAD_CHUNK, tile_rows, LANES), dt)
    tail_scratch = [] if tail == 0 else [
        pltpu.VMEM((tail,), jnp.int32), pltpu.VMEM((tail,), jnp.int32), pltpu.VMEM((tail, tile_rows, LANES), dt)]

    @functools.partial(
        pl.kernel, mesh=plsc.VectorSubcoreMesh(core_axis_name="c", subcore_axis_name="s"),
        out_type=jax.ShapeDtypeStruct((n_sorted_rows, tile_rows, LANES), dt),
        scratch_types=[
            pltpu.VMEM((SC_CHUNK,), jnp.int32), pltpu.VMEM((SC_CHUNK,), jnp.int32),
            pltpu.VMEM((SC_CHUNK, tile_rows, LANES), dt),
            pltpu.VMEM((SC_PAD_CHUNK,), jnp.int32),
            pltpu.VMEM((SC_PAD_CHUNK, tile_rows, LANES), dt),
        ] + tail_scratch,
    )
    def k(pos_hbm, pad_hbm, x_hbm, z_hbm, xs_hbm, i0_v, i1_v, rows_v, ip_v, z_v, *tail_refs):
        wid = lax.axis_index("s") * nc + lax.axis_index("c")

        def move(b, n, idx0, idx1, rows):
            pltpu.sync_copy(pos_hbm.at[pl.ds(b, n)], idx0)
            pltpu.sync_copy(pos_hbm.at[pl.ds(t_all + b, n)], idx1)
            pltpu.sync_copy(x_hbm.at[pl.ds(b, n)], rows)
            pltpu.sync_copy(rows, xs_hbm.at[idx0])
            pltpu.sync_copy(rows, xs_hbm.at[idx1])

        @pl.loop(0, n_ch)
        def _(j):
            move(pl.multiple_of(wid * per_w + j * SC_CHUNK, 8), SC_CHUNK, i0_v, i1_v, rows_v)

        if tail:
            move(pl.multiple_of(wid * per_w + n_ch * SC_CHUNK, 8), tail, *tail_refs)

        pltpu.sync_copy(z_hbm, z_v)

        @pl.loop(0, n_pch)
        def _(j):
            b = pl.multiple_of(wid * pad_per_w + j * SC_PAD_CHUNK, 8)
            pltpu.sync_copy(pad_hbm.at[pl.ds(b, SC_PAD_CHUNK)], ip_v)
            pltpu.sync_copy(z_v, xs_hbm.at[ip_v])

    xs = k(pos_flat, pad_rows, x_tiles.reshape(t_all, tile_rows, LANES), zeros)
    return xs.reshape(n_sorted_rows * tile_rows, LANES)


def _sc_gather_call(pos_flat, outs, tile_rows):
    n_assign = pos_flat.shape[0]
    dt = outs.dtype
    nc, nw = _sc_workers()
    per_w = n_assign // nw
    n_ch, tail = _sc_split(per_w)
    assert per_w * nw == n_assign
    tail_scratch = [] if tail == 0 else [pltpu.VMEM((tail,), jnp.int32), pltpu.VMEM((tail, tile_rows, LANES), dt)]

    @functools.partial(
        pl.kernel, mesh=plsc.VectorSubcoreMesh(core_axis_name="c", subcore_axis_name="s"),
        out_type=jax.ShapeDtypeStruct((n_assign, tile_rows, LANES), dt),
        scratch_types=[pltpu.VMEM((SC_CHUNK,), jnp.int32),
                       pltpu.VMEM((SC_CHUNK, tile_rows, LANES), dt)] + tail_scratch,
    )
    def k(pos_hbm, o_hbm, y_hbm, i_v, rows_v, *tail_refs):
        wid = lax.axis_index("s") * nc + lax.axis_index("c")

        def move(b, n, idx, rows):
            pltpu.sync_copy(pos_hbm.at[pl.ds(b, n)], idx)
            pltpu.sync_copy(o_hbm.at[idx], rows)
            pltpu.sync_copy(rows, y_hbm.at[pl.ds(b, n)])

        @pl.loop(0, n_ch)
        def _(j):
            move(pl.multiple_of(wid * per_w + j * SC_CHUNK, 8), SC_CHUNK, i_v, rows_v)

        if tail:
            move(pl.multiple_of(wid * per_w + n_ch * SC_CHUNK, 8), tail, *tail_refs)

    y = k(pos_flat, outs.reshape(-1, tile_rows, LANES))
    return y.reshape(n_assign * tile_rows, LANES)


def _combine_dense_kernel(alpha, n_main_tiles, x1t_ref, y0_ref, y1_ref, w_ref, g_ref, b_ref, out_ref, *tail):
    j = pl.program_id(0)
    w = w_ref[...]
    y = (w[:, 0:1] * _load_packed_tiles(y0_ref, TOK_TILE, F32)
         + w[:, 1:2] * _load_packed_tiles(y1_ref, TOK_TILE, F32))
    res = _layer_norm(alpha * _load_token_tiles(x1t_ref, TOK_TILE) + y, g_ref[...], b_ref[...])
    if n_main_tiles is None:
        out_ref[...] = res
    else:
        @pl.when(j < n_main_tiles)
        def _():
            out_ref[...] = res

        @pl.when(j >= n_main_tiles)
        def _():
            tail[0][...] = res


def _combine_dense_call(x1t, y_pair, w_col, g, b, alpha, n_main_tiles=None):
    t_all = x1t.shape[0] // TILE_ROWS
    n_tiles = t_all // TOK_TILE
    if n_main_tiles is None:
        out_specs = pl.BlockSpec((TOK_TILE, D_MODEL), lambda i: (i, 0))
        out_shape = jax.ShapeDtypeStruct((t_all, D_MODEL), F32)
    else:
        assert n_tiles == n_main_tiles + 1
        out_specs = [pl.BlockSpec((TOK_TILE, D_MODEL), lambda i: (jnp.minimum(i, n_main_tiles - 1), 0)),
                     pl.BlockSpec((TOK_TILE, D_MODEL), lambda i: (0, 0))]
        out_shape = [jax.ShapeDtypeStruct((n_main_tiles * TOK_TILE, D_MODEL), F32),
                     jax.ShapeDtypeStruct((TOK_TILE, D_MODEL), F32)]
    tile = (TOK_TILE * TILE_ROWS, LANES)
    packed = (TOK_TILE * PACK_ROWS, LANES)
    return pl.pallas_call(
        functools.partial(_combine_dense_kernel, alpha, n_main_tiles),
        grid=(n_tiles,),
        in_specs=[
            pl.BlockSpec(tile, lambda i: (i, 0)),
            pl.BlockSpec(packed, lambda i: (i, 0)),
            pl.BlockSpec(packed, lambda i: (i + n_tiles, 0)),
            pl.BlockSpec((TOK_TILE, TOP_K), lambda i: (i, 0)),
            pl.BlockSpec((1, D_MODEL), lambda i: (0, 0)),
            pl.BlockSpec((1, D_MODEL), lambda i: (0, 0)),
        ],
        out_specs=out_specs,
        out_shape=out_shape,
        compiler_params=pltpu.CompilerParams(dimension_semantics=("arbitrary",), vmem_limit_bytes=VMEM_LIMIT),
        name="combine",
    )(x1t, y_pair, y_pair, w_col, g, b)


def _rope_tables(seq, dec_seq, past_len):
    half = HEAD_DIM // 2
    inv = ROPE_THETA ** (-jnp.arange(half, dtype=F32) / half)
    pos = jnp.concatenate([jnp.arange(seq), past_len + (jnp.arange(TOK_TILE) % dec_seq)])
    ang = pos.astype(F32)[:, None] * inv[None, :]
    cos = jnp.cos(ang)
    sin = jnp.sin(ang)
    zero = jnp.zeros_like(sin)
    reps = LANES // HEAD_DIM
    cos_t = jnp.tile(jnp.concatenate([cos, cos], -1), (1, reps))
    sa_t = jnp.tile(jnp.concatenate([-sin, zero], -1), (1, reps))
    sb_t = jnp.tile(jnp.concatenate([zero, sin], -1), (1, reps))
    return cos_t, sa_t, sb_t


def _inclusive_cumsum(v):
    n = v.shape[0]
    tri = jnp.arange(n)[None, :] <= jnp.arange(n)[:, None]
    return jnp.sum(jnp.where(tri, v[None, :], 0), axis=1)


def _dispatch_plan(ridx, counts, n_tiles):
    t_all = ridx.shape[1]
    tiles_e = (counts + EXPERT_TILE - 1) // EXPERT_TILE
    tile_end = _inclusive_cumsum(tiles_e)
    offs = (tile_end - tiles_e) * EXPERT_TILE
    n_used = tile_end[-1:]
    tile_expert = jnp.minimum(
        jnp.sum((jnp.arange(n_tiles)[:, None] >= tile_end[None, :]).astype(jnp.int32), axis=1), N_EXPERTS - 1)
    experts = jnp.arange(N_EXPERTS)
    pos = ridx[TOP_K:2 * TOP_K] + jnp.sum(
        jnp.where(ridx[:TOP_K, :, None] == experts[None, None, :], offs[None, None, :], 0), axis=-1)
    gap_start = jnp.concatenate([offs + counts, n_used * EXPERT_TILE])
    gap_len = jnp.concatenate([tiles_e * EXPERT_TILE - counts, (n_tiles - n_used) * EXPERT_TILE])
    gap_end_q = _inclusive_cumsum(gap_len)
    q = jnp.arange(n_tiles * EXPERT_TILE - TOP_K * t_all)
    gap = jnp.sum((q[:, None] >= gap_end_q[None, :]).astype(jnp.int32), axis=1)
    in_gap = gap[:, None] == jnp.arange(N_EXPERTS + 1)[None, :]
    pad_rows = q + jnp.sum(jnp.where(in_gap, (gap_start - gap_end_q + gap_len)[None, :], 0), axis=1)
    return (pos.reshape(-1).astype(jnp.int32), pad_rows.astype(jnp.int32),
            tile_expert.astype(jnp.int32), n_used.astype(jnp.int32))


def kernel(x_prompt, x_sample, cache_k, cache_v, w_in, sinks, gm_ln_g, gm_ln_b, gm_ws, gm_bs,
           out_norm_a, out_norm_b, w_o, ln1_g, ln1_b, w_router, router_bias,
           w_gate, w_up, w_down, ln2_g, ln2_b):
    batch, seq, _ = x_prompt.shape
    dec_batch, dec_seq, _ = x_sample.shape
    depth = w_in.shape[0]
    past_len = PAST_LEN
    assert dec_batch * dec_seq == TOK_TILE and dec_seq == CHUNK and seq % TOK_TILE == 0
    assert cache_k.shape[2] == WINDOW
    alpha = (2 * depth) ** 0.25
    tiles_per_seq = seq // TOK_TILE

    tables = _rope_tables(seq, dec_seq, past_len)
    wrt = w_router.T
    rb = router_bias.reshape(N_EXPERTS, 1)

    streams = [
        dict(x=(x_prompt.reshape(batch * seq, D_MODEL), x_sample.reshape(-1, D_MODEL)),
             n_seq=batch, has_sample=True),
    ]
    kp, vp, ks, vs, gms = [], [], [], [], []
    for l in range(depth):
        layer_args = (
            sinks[l], w_in,
            cache_k[l].reshape(dec_batch, WINDOW, KV_WIDTH), cache_v[l].reshape(dec_batch, WINDOW, KV_WIDTH),
            gm_ln_g[l].reshape(1, GM_WIDTH), gm_ln_b[l].reshape(1, GM_WIDTH),
            gm_ws[l], gm_bs[l].T,
            out_norm_a[l].reshape(1, ATT_WIDTH), out_norm_b[l].reshape(1, GM_WIDTH),
            w_o,
            ln1_g[l].reshape(1, D_MODEL), ln1_b[l].reshape(1, D_MODEL), wrt, rb)
        k_tails, v_tails = [], []
        for st in streams:
            n_prompt = st["n_seq"] * seq
            n_prompt_tiles = n_prompt // TOK_TILE
            t_all = n_prompt + (TOK_TILE if st["has_sample"] else 0)
            n_exp_tiles = (TOP_K * t_all) // EXPERT_TILE + N_EXPERTS
            res = _mixer_call(layer_args, st["x"], tables, n_prompt_tiles, tiles_per_seq, alpha, l)
            if st["has_sample"]:
                x1t, x1p, k_tail, v_tail, k_s, v_s, vn_s, ridx, rw, counts = res
            else:
                x1t, x1p, k_tail, v_tail, ridx, rw, counts = res

            pos, pad_rows, tile_expert, n_used = _dispatch_plan(ridx, counts[:, 0], n_exp_tiles)
            xs = _sc_dispatch_call(pos, pad_rows, x1p, PACK_ROWS, n_exp_tiles * EXPERT_TILE)
            outs = _expert_call(tile_expert, n_used, xs, w_gate, w_up, w_down, l)
            y_pair = _sc_gather_call(pos, outs, PACK_ROWS)
            combine_args = (x1t, y_pair, rw[:TOP_K].T, ln2_g[l].reshape(1, D_MODEL), ln2_b[l].reshape(1, D_MODEL))
            if l < depth - 1:
                st["x"] = combine_args
            else:
                st["x"] = _combine_dense_call(*combine_args, alpha,
                                              n_main_tiles=n_prompt_tiles if st["has_sample"] else None)

            k_tails.append(k_tail)
            v_tails.append(v_tail)
            if st["has_sample"]:
                ks.append(k_s.reshape(dec_batch, dec_seq, N_KV_HEADS, HEAD_DIM))
                vs.append(v_s.reshape(dec_batch, dec_seq, N_KV_HEADS, HEAD_DIM))
                gms.append(vn_s.reshape(dec_batch, dec_seq, GM_WIDTH))
        kp.append(jnp.concatenate(k_tails, axis=0).reshape(batch, WINDOW, N_KV_HEADS, HEAD_DIM))
        vp.append(jnp.concatenate(v_tails, axis=0).reshape(batch, WINDOW, N_KV_HEADS, HEAD_DIM))

    y_prompt = jnp.concatenate([st["x"][0] if st["has_sample"] else st["x"] for st in streams],
                               axis=0).reshape(batch, seq, D_MODEL)
    y_sample = streams[-1]["x"][1].reshape(dec_batch, dec_seq, D_MODEL)
    return (y_prompt, y_sample, jnp.stack(kp), jnp.stack(vp), jnp.stack(ks), jnp.stack(vs), jnp.stack(gms))
```

```python
import functools

import jax
import jax.numpy as jnp
from jax import lax
from jax.experimental import pallas as pl
from jax.experimental.pallas import tpu as pltpu
from jax.experimental.pallas import tpu_sc as plsc

D_MODEL = 1024
CHUNK = 64
WINDOW = 128
HEAD_DIM = 64
N_Q_HEADS = 8
N_KV_HEADS = 2
Q_PER_KV = N_Q_HEADS // N_KV_HEADS
PAST_LEN = 2048
ATT_WIDTH = N_Q_HEADS * HEAD_DIM
KV_WIDTH = N_KV_HEADS * HEAD_DIM
ROPE_THETA = 10000.0
GM_GROUPS = 4
GM_CH = 128
GM_WIDTH = GM_GROUPS * GM_CH
GM_CHUNK = 128
D_MIX = ATT_WIDTH + GM_WIDTH
D_IN = ATT_WIDTH + 2 * KV_WIDTH + 2 * GM_WIDTH
N_EXPERTS = 16
N_EXPERT_GROUPS = 4
EXPERTS_PER_GROUP = N_EXPERTS // N_EXPERT_GROUPS
TOP_K = 2
D_EXPERT = 512
LN_EPS = 1e-5
NEG_INF = -1e30

LANES = 128
PACK_ROWS = D_MODEL // (2 * LANES)
HI_HALF = 0xFFFF0000
TOK_TILE = 512
EXPERT_TILE = 512
KEYS_PER_CHUNK = WINDOW + CHUNK
VMEM_LIMIT = 60 * 1024 * 1024

F32 = jnp.float32
BF16 = jnp.bfloat16


def _layer_norm(x, g, b):
    mu = jnp.mean(x, axis=-1, keepdims=True)
    d = x - mu
    var = jnp.mean(d * d, axis=-1, keepdims=True)
    return d * lax.rsqrt(var + LN_EPS) * g + b


def _rms_norm(x, g):
    return x * lax.rsqrt(jnp.mean(x * x, axis=-1, keepdims=True) + LN_EPS) * g


def _gelu(x):
    return 0.5 * x * (1.0 + lax.erf(x * (0.5 ** 0.5)))


def _bf16_bits(x):
    return lax.bitcast_convert_type(x.astype(BF16).astype(F32), jnp.uint32)


def _store_packed_tiles(ref, x):
    n = x.shape[0]
    for s in range(PACK_ROWS):
        lo = _bf16_bits(x[:, (2 * s) * LANES:(2 * s + 1) * LANES]) >> 16
        hi = _bf16_bits(x[:, (2 * s + 1) * LANES:(2 * s + 2) * LANES]) & jnp.uint32(HI_HALF)
        ref[pl.ds(s, n, stride=PACK_ROWS), :] = hi | lo


def _load_packed_tiles(ref, n, dtype):
    cols = []
    for s in range(PACK_ROWS):
        w = ref[pl.ds(s, n, stride=PACK_ROWS), :]
        cols.append(lax.bitcast_convert_type(w << 16, F32).astype(dtype))
        cols.append(lax.bitcast_convert_type(w & jnp.uint32(HI_HALF), F32).astype(dtype))
    return jnp.concatenate(cols, axis=-1)


def _first_index_of_max(rows):
    m = rows[0]
    for r in rows[1:]:
        m = jnp.maximum(m, r)
    idx = jnp.full(m.shape, len(rows), jnp.int32)
    for e in reversed(range(len(rows))):
        idx = jnp.where(rows[e] == m, e, idx)
    return m, idx


def _mixer_kernel(n_prompt_tiles, tiles_per_seq, alpha, has_sample, first_layer, sinks_ref, *refs):
    n_x = 2 if first_layer else 6
    x_refs, refs = refs[:n_x], refs[n_x:]
    (w_in_ref, cos_ref, sa_ref, sb_ref, ck_ref, cv_ref, lng_ref, lnb_ref, ws_ref, bst_ref, na_ref, nb_ref, w_o_ref,
     l1g_ref, l1b_ref, wrt_ref, rb_ref, tri_ref, x1t_ref, x1p_ref, kt_ref, vt_ref) = refs[:22]
    rest = refs[22:]
    if has_sample:
        (ks_ref, vs_ref, vn_ref, ridx_ref, rw_ref, counts_ref,
         kd_ref, vd_ref, qm_ref, att_ref, gm_ref, cnt_ref, w_in_s, w_o_s, x_s) = rest
    else:
        (ridx_ref, rw_ref, counts_ref,
         kd_ref, vd_ref, qm_ref, att_ref, gm_ref, cnt_ref, w_in_s, w_o_s, x_s) = rest
    i = pl.program_id(0)
    is_sample = i >= n_prompt_tiles
    seq_start = jnp.logical_and(jnp.logical_not(is_sample), (i % tiles_per_seq) == 0)
    seq_end = jnp.logical_and(jnp.logical_not(is_sample), (i % tiles_per_seq) == tiles_per_seq - 1)
    tm = TOK_TILE

    @pl.when(i == 0)
    def _():
        w_in_s[...] = w_in_ref[0].astype(BF16)
        w_o_s[...] = w_o_ref[0].astype(BF16)

    if first_layer:
        xp_ref, xs_ref = x_refs
        x_s[...] = jnp.where(is_sample, xs_ref[...], xp_ref[...])
    else:
        x1_prev_ref, y0_ref, y1_ref, wc_ref, l2g_ref, l2b_ref = x_refs
        wc = wc_ref[...]
        y_moe = (wc[:, 0:1] * _load_packed_tiles(y0_ref, tm, F32)
                 + wc[:, 1:2] * _load_packed_tiles(y1_ref, tm, F32))
        x_s[...] = _layer_norm(alpha * x1_prev_ref[...] + y_moe, l2g_ref[...], l2b_ref[...])

    h = jnp.dot(x_s[...].astype(BF16), w_in_s[...], preferred_element_type=F32)

    cos = cos_ref[...]
    sa = sa_ref[...]
    sb = sb_ref[...]

    def rope(blk):
        return blk * cos + pltpu.roll(blk, LANES - HEAD_DIM // 2, 1) * sa + pltpu.roll(blk, HEAD_DIM // 2, 1) * sb

    lane = lax.broadcasted_iota(jnp.int32, (1, LANES), 1)
    lo_half = lane < HEAD_DIM

    o_k = ATT_WIDTH
    o_v = o_k + KV_WIDTH
    o_u = o_v + KV_WIDTH
    o_g = o_u + GM_WIDTH

    k_rot = rope(h[:, o_k:o_v])
    v_new = h[:, o_v:o_u]

    @pl.when(seq_end)
    def _():
        kt_ref[...] = k_rot[tm - WINDOW:]
        vt_ref[...] = v_new[tm - WINDOW:]

    if has_sample:
        @pl.when(is_sample)
        def _():
            ks_ref[...] = k_rot
            vs_ref[...] = v_new

    def dup_heads(a):
        sw = pltpu.roll(a, HEAD_DIM, 1)
        return jnp.where(lo_half, a, sw).astype(BF16), jnp.where(lo_half, sw, a).astype(BF16)

    k_d = dup_heads(k_rot)
    v_d = dup_heads(v_new)

    @pl.when(jnp.logical_not(is_sample))
    def _():
        @pl.when(seq_start)
        def _():
            for g in range(N_KV_HEADS):
                kd_ref[g, 0:WINDOW, :] = jnp.zeros((WINDOW, LANES), BF16)
                vd_ref[g, 0:WINDOW, :] = jnp.zeros((WINDOW, LANES), BF16)

        @pl.when(jnp.logical_not(seq_start))
        def _():
            for g in range(N_KV_HEADS):
                kd_ref[g, 0:WINDOW, :] = kd_ref[g, tm:tm + WINDOW, :]
                vd_ref[g, 0:WINDOW, :] = vd_ref[g, tm:tm + WINDOW, :]

        for g in range(N_KV_HEADS):
            kd_ref[g, WINDOW:WINDOW + tm, :] = k_d[g]
            vd_ref[g, WINDOW:WINDOW + tm, :] = v_d[g]

    @pl.when(is_sample)
    def _():
        for b in range(tm // CHUNK):
            ck = dup_heads(ck_ref[b])
            cv = dup_heads(cv_ref[b])
            base = b * KEYS_PER_CHUNK
            for g in range(N_KV_HEADS):
                kd_ref[g, base:base + WINDOW, :] = ck[g]
                vd_ref[g, base:base + WINDOW, :] = cv[g]
                kd_ref[g, base + WINDOW:base + KEYS_PER_CHUNK, :] = k_d[g][b * CHUNK:(b + 1) * CHUNK]
                vd_ref[g, base + WINDOW:base + KEYS_PER_CHUNK, :] = v_d[g][b * CHUNK:(b + 1) * CHUNK]

    scale = HEAD_DIM ** -0.5
    for b in range(ATT_WIDTH // LANES):
        qb = (rope(h[:, b * LANES:(b + 1) * LANES]) * scale).astype(BF16)
        zero = jnp.zeros_like(qb)
        qm_ref[2 * b] = jnp.where(lo_half, qb, zero)
        qm_ref[2 * b + 1] = jnp.where(lo_half, zero, qb)

    key_stride = jnp.where(is_sample, KEYS_PER_CHUNK, CHUNK)
    col = lax.broadcasted_iota(jnp.int32, (1, KEYS_PER_CHUNK), 1)
    row_blk = lax.broadcasted_iota(jnp.int32, (Q_PER_KV * CHUNK, 1), 0) // CHUNK
    sink_cols = [
        jnp.where(row_blk == 0, sinks_ref[Q_PER_KV * g],
                  jnp.where(row_blk == 1, sinks_ref[Q_PER_KV * g + 1],
                            jnp.where(row_blk == 2, sinks_ref[Q_PER_KV * g + 2], sinks_ref[Q_PER_KV * g + 3])))
        for g in range(N_KV_HEADS)]

    for c in range(tm // CHUNK):
        r0 = c * CHUNK
        k0 = pl.multiple_of(c * key_stride, CHUNK)
        outs = []
        for g in range(N_KV_HEADS):
            q_st = jnp.concatenate([qm_ref[Q_PER_KV * g + r, pl.ds(r0, CHUNK), :] for r in range(Q_PER_KV)],
                                   axis=0)
            keys = kd_ref[g, pl.ds(k0, KEYS_PER_CHUNK), :]
            vals = vd_ref[g, pl.ds(k0, KEYS_PER_CHUNK), :]
            s = lax.dot_general(q_st, keys, (((1,), (1,)), ((), ())), preferred_element_type=F32)
            if c < WINDOW // CHUNK:
                first_valid = jnp.where(seq_start, (WINDOW // CHUNK - c) * CHUNK, 0)
                s = jnp.where(col >= first_valid, s, NEG_INF)
            sink = sink_cols[g]
            m = jnp.maximum(jnp.max(s, axis=-1, keepdims=True), sink)
            p = jnp.exp(s - m)
            denom = jnp.sum(p, axis=-1, keepdims=True) + jnp.exp(sink - m)
            o = jnp.dot(p.astype(BF16), vals, preferred_element_type=F32) * (1.0 / denom)
            for bb in range(2):
                outs.append(jnp.where(lo_half, o[(2 * bb) * CHUNK:(2 * bb + 1) * CHUNK],
                                      o[(2 * bb + 1) * CHUNK:(2 * bb + 2) * CHUNK]))
        for b in range(ATT_WIDTH // LANES):
            att_ref[pl.ds(r0, CHUNK), b * LANES:(b + 1) * LANES] = outs[b]

    prow = lax.broadcasted_iota(jnp.int32, (GM_CHUNK, GM_CHUNK), 0)
    pcol = lax.broadcasted_iota(jnp.int32, (GM_CHUNK, GM_CHUNK), 1)
    half = GM_CHUNK // 2
    tril = pcol <= prow
    same_blk = (prow < half) == (pcol < half)
    prow1 = lax.broadcasted_iota(jnp.int32, (GM_CHUNK, 1), 0)
    bst = bst_ref[...]
    bst_s = jnp.where(prow1 < half, bst, pltpu.roll(bst, half, 0))
    bias = jnp.where(is_sample, bst_s, bst)
    for g in range(GM_GROUPS):
        wg_ = ws_ref[g]
        w_s = jnp.where(prow < half, wg_, pltpu.roll(pltpu.roll(wg_, half, 0), half, 1))
        w_eff = jnp.where(is_sample, jnp.where(same_blk, w_s, 0.0), wg_)
        w_eff = jnp.where(tril, w_eff, 0.0).astype(BF16)
        cols = slice(g * GM_CH, (g + 1) * GM_CH)
        ug = _gelu(h[:, o_u + g * GM_CH:o_u + (g + 1) * GM_CH])
        vn = _layer_norm(_gelu(h[:, o_g + g * GM_CH:o_g + (g + 1) * GM_CH]), lng_ref[:, cols], lnb_ref[:, cols])

        if has_sample:
            @pl.when(is_sample)
            def _(cols=cols, vn=vn):
                vn_ref[:, cols] = vn

        vn_b = vn.astype(BF16)
        for n in range(tm // GM_CHUNK):
            rows = slice(n * GM_CHUNK, (n + 1) * GM_CHUNK)
            s = jnp.dot(w_eff, vn_b[rows], preferred_element_type=F32) + bias[:, g:g + 1]
            gm_ref[rows, cols] = ug[rows] * s

    mixed = jnp.concatenate([_rms_norm(att_ref[...], na_ref[...]), _rms_norm(gm_ref[...], nb_ref[...])], axis=-1)
    y = alpha * x_s[...] + jnp.dot(mixed.astype(BF16), w_o_s[...], preferred_element_type=F32)
    x1 = _layer_norm(y, l1g_ref[...], l1b_ref[...])
    x1t_ref[...] = x1
    _store_packed_tiles(x1p_ref, x1)

    nt = (((1,), (1,)), ((), ()))
    wr = wrt_ref[...]
    wr_hi = wr.astype(BF16)
    wr_lo = (wr - wr_hi.astype(F32)).astype(BF16)
    x1_hi = x1.astype(BF16)
    x1_lo = (x1 - x1_hi.astype(F32)).astype(BF16)
    logits = (lax.dot_general(wr_hi, x1_hi, nt, preferred_element_type=F32)
              + (lax.dot_general(wr_hi, x1_lo, nt, preferred_element_type=F32)
                 + lax.dot_general(wr_lo, x1_hi, nt, preferred_element_type=F32)))
    scores = jax.nn.sigmoid(logits)
    sel = scores + rb_ref[...]
    sel_rows = [sel[e:e + 1, :] for e in range(N_EXPERTS)]
    sc_rows = [scores[e:e + 1, :] for e in range(N_EXPERTS)]
    grp = []
    for g in range(N_EXPERT_GROUPS):
        r = sel_rows[g * EXPERTS_PER_GROUP:(g + 1) * EXPERTS_PER_GROUP]
        best_pair = None
        for a in range(EXPERTS_PER_GROUP):
            for b in range(a + 1, EXPERTS_PER_GROUP):
                pair = r[a] + r[b]
                best_pair = pair if best_pair is None else jnp.maximum(best_pair, pair)
        grp.append(best_pair)
    _, best = _first_index_of_max(grp)
    masked = [jnp.where(best == (e // EXPERTS_PER_GROUP), sel_rows[e], NEG_INF) for e in range(N_EXPERTS)]
    _, e0 = _first_index_of_max(masked)
    masked2 = [jnp.where(e0 == e, -jnp.inf, masked[e]) for e in range(N_EXPERTS)]
    _, e1 = _first_index_of_max(masked2)
    w0 = jnp.zeros_like(sc_rows[0])
    w1 = jnp.zeros_like(sc_rows[0])
    for e in range(N_EXPERTS):
        w0 = jnp.where(e0 == e, sc_rows[e], w0)
        w1 = jnp.where(e1 == e, sc_rows[e], w1)
    wsum = w0 + w1
    rw_ref[...] = jnp.concatenate([w0 / wsum, w1 / wsum, jnp.zeros((6, tm), F32)], axis=0)

    @pl.when(i == 0)
    def _():
        cnt_ref[...] = jnp.zeros_like(cnt_ref)

    eid = lax.broadcasted_iota(jnp.int32, (N_EXPERTS, tm), 0)
    oh0 = (eid == e0).astype(F32)
    oh1 = (eid == e1).astype(F32)
    oh = oh0 + oh1
    incl = jnp.dot(oh.astype(BF16), tri_ref[...], preferred_element_type=F32)
    before = cnt_ref[:, 0:1] + incl - oh
    r0 = jnp.sum(oh0 * before, axis=0, keepdims=True).astype(jnp.int32)
    r1 = jnp.sum(oh1 * before, axis=0, keepdims=True).astype(jnp.int32)
    cnt = cnt_ref[...] + incl[:, tm - 1:tm]
    cnt_ref[...] = cnt
    counts_ref[...] = cnt.astype(jnp.int32)
    ridx_ref[...] = jnp.concatenate([e0, e1, r0, r1, jnp.zeros((4, tm), jnp.int32)], axis=0)


def _mixer_call(layer_args, x_all, tables, n_prompt_tiles, tiles_per_seq, alpha, layer):
    (sinks, w_in, ck, cv, lng, lnb, ws, bst, na, nb, w_o, l1g, l1b, wrt, rb) = layer_args
    cos_t, sa_t, sb_t = tables
    first_layer = len(x_all) == 2
    if first_layer:
        x_main, x_tail = x_all
        t_all = x_main.shape[0] + x_tail.shape[0]
        n_tiles = t_all // TOK_TILE
        x_args = [x_main, x_tail]
        x_specs = [pl.BlockSpec((TOK_TILE, D_MODEL), lambda i: (jnp.minimum(i, n_prompt_tiles - 1), 0)),
                   pl.BlockSpec((TOK_TILE, D_MODEL), lambda i: (0, 0))]
    else:
        x1t_prev, y_pair, w_col, l2g, l2b = x_all
        t_all = x1t_prev.shape[0]
        n_tiles = t_all // TOK_TILE
        x_args = [x1t_prev, y_pair, y_pair, w_col, l2g, l2b]
        x_specs = [pl.BlockSpec((TOK_TILE, D_MODEL), lambda i: (i, 0)),
                   pl.BlockSpec((TOK_TILE * PACK_ROWS, LANES), lambda i: (i, 0)),
                   pl.BlockSpec((TOK_TILE * PACK_ROWS, LANES), lambda i: (i + n_tiles, 0)),
                   pl.BlockSpec((TOK_TILE, TOP_K), lambda i: (i, 0)),
                   pl.BlockSpec((1, D_MODEL), lambda i: (0, 0)), pl.BlockSpec((1, D_MODEL), lambda i: (0, 0))]
    n_seq_tiles = tiles_per_seq
    has_sample = n_tiles > n_prompt_tiles

    def const(shape):
        nd = len(shape)
        return pl.BlockSpec(shape, lambda i, _nd=nd: (0,) * _nd)

    def tab_map(i):
        return (jnp.where(i < n_prompt_tiles, i % n_seq_tiles, n_seq_tiles), 0)

    n_seq = n_prompt_tiles // n_seq_tiles

    def seq_map(i):
        return (jnp.minimum(i // n_seq_tiles, n_seq - 1), 0)

    row_blk = lambda w: pl.BlockSpec((TOK_TILE, w), lambda i: (i, 0))
    in_specs = [
        pl.BlockSpec(memory_space=pltpu.SMEM),
        *x_specs,
        pl.BlockSpec((1, D_MODEL, D_IN), lambda i: (layer, 0, 0)),
        pl.BlockSpec((TOK_TILE, LANES), tab_map),
        pl.BlockSpec((TOK_TILE, LANES), tab_map),
        pl.BlockSpec((TOK_TILE, LANES), tab_map),
        const(ck.shape), const(cv.shape),
        const((1, GM_WIDTH)), const((1, GM_WIDTH)),
        const((GM_GROUPS, GM_CHUNK, GM_CHUNK)), const((GM_CHUNK, GM_GROUPS)),
        const((1, ATT_WIDTH)), const((1, GM_WIDTH)),
        pl.BlockSpec((1, D_MIX, D_MODEL), lambda i: (layer, 0, 0)),
        const((1, D_MODEL)), const((1, D_MODEL)),
        const((N_EXPERTS, D_MODEL)), const((N_EXPERTS, 1)),
        const((TOK_TILE, TOK_TILE)),
    ]
    out_shape = [
        jax.ShapeDtypeStruct((t_all, D_MODEL), F32),
        jax.ShapeDtypeStruct((t_all * PACK_ROWS, LANES), jnp.uint32),
        jax.ShapeDtypeStruct((n_seq * WINDOW, KV_WIDTH), F32),
        jax.ShapeDtypeStruct((n_seq * WINDOW, KV_WIDTH), F32),
        jax.ShapeDtypeStruct((TOK_TILE, KV_WIDTH), F32),
        jax.ShapeDtypeStruct((TOK_TILE, KV_WIDTH), F32),
        jax.ShapeDtypeStruct((TOK_TILE, GM_WIDTH), F32),
        jax.ShapeDtypeStruct((8, t_all), jnp.int32),
        jax.ShapeDtypeStruct((8, t_all), F32),
        jax.ShapeDtypeStruct((N_EXPERTS, LANES), jnp.int32),
    ]
    out_specs = [
        pl.BlockSpec((TOK_TILE, D_MODEL), lambda i: (i, 0)),
        pl.BlockSpec((TOK_TILE * PACK_ROWS, LANES), lambda i: (i, 0)),
        pl.BlockSpec((WINDOW, KV_WIDTH), seq_map), pl.BlockSpec((WINDOW, KV_WIDTH), seq_map),
        const((TOK_TILE, KV_WIDTH)), const((TOK_TILE, KV_WIDTH)),
        const((TOK_TILE, GM_WIDTH)),
        pl.BlockSpec((8, TOK_TILE), lambda i: (0, i)),
        pl.BlockSpec((8, TOK_TILE), lambda i: (0, i)),
        const((N_EXPERTS, LANES)),
    ]
    if not has_sample:
        del out_shape[4:7], out_specs[4:7]
    kd_rows = (TOK_TILE // CHUNK) * KEYS_PER_CHUNK
    scratch = [
        pltpu.VMEM((N_KV_HEADS, kd_rows, LANES), BF16),
        pltpu.VMEM((N_KV_HEADS, kd_rows, LANES), BF16),
        pltpu.VMEM((N_Q_HEADS, TOK_TILE, LANES), BF16),
        pltpu.VMEM((TOK_TILE, ATT_WIDTH), F32),
        pltpu.VMEM((TOK_TILE, GM_WIDTH), F32),
        pltpu.VMEM((N_EXPERTS, LANES), F32),
        pltpu.VMEM((D_MODEL, D_IN), BF16),
        pltpu.VMEM((D_MIX, D_MODEL), BF16),
        pltpu.VMEM((TOK_TILE, D_MODEL), F32),
    ]
    return pl.pallas_call(
        functools.partial(_mixer_kernel, n_prompt_tiles, tiles_per_seq, alpha, has_sample, first_layer),
        grid=(n_tiles,),
        in_specs=in_specs, out_specs=out_specs, out_shape=out_shape,
        scratch_shapes=scratch,
        compiler_params=pltpu.CompilerParams(dimension_semantics=("arbitrary",), vmem_limit_bytes=VMEM_LIMIT),
        name="mixer",
    )(sinks, *x_args, w_in, cos_t, sa_t, sb_t, ck, cv, lng, lnb, ws, bst, na, nb, w_o, l1g, l1b, wrt, rb,
      jnp.triu(jnp.ones((TOK_TILE, TOK_TILE), BF16)))


def _expert_kernel(te_ref, nt_ref, xs_ref, wg_ref, wu_ref, wd_ref, out_ref, wg_s, wu_s, wd_s):
    i = pl.program_id(0)
    changed = jnp.logical_or(i == 0, te_ref[i] != te_ref[jnp.maximum(i - 1, 0)])

    @pl.when(changed)
    def _():
        wg_s[...] = wg_ref[0, 0].astype(BF16)
        wu_s[...] = wu_ref[0, 0].astype(BF16)
        wd_s[...] = wd_ref[0, 0].astype(BF16)

    @pl.when(i < nt_ref[0])
    def _():
        xs = _load_packed_tiles(xs_ref, EXPERT_TILE, BF16)
        g = jnp.dot(xs, wg_s[...], preferred_element_type=F32)
        u = jnp.dot(xs, wu_s[...], preferred_element_type=F32)
        hmid = (g * jax.nn.sigmoid(g)) * u
        _store_packed_tiles(out_ref, jnp.dot(hmid.astype(BF16), wd_s[...], preferred_element_type=F32))


def _expert_call(tile_expert, n_used, xs, wg, wu, wd, layer):
    n_tiles = xs.shape[0] // (EXPERT_TILE * PACK_ROWS)
    tile_blk = pl.BlockSpec((EXPERT_TILE * PACK_ROWS, LANES), lambda i, te, nt: (jnp.minimum(i, nt[0] - 1), 0))
    w_blk = lambda shape: pl.BlockSpec((1, 1) + shape, lambda i, te, nt: (layer, te[i], 0, 0))
    grid_spec = pltpu.PrefetchScalarGridSpec(
        num_scalar_prefetch=2,
        grid=(n_tiles,),
        in_specs=[
            tile_blk,
            w_blk((D_MODEL, D_EXPERT)), w_blk((D_MODEL, D_EXPERT)), w_blk((D_EXPERT, D_MODEL)),
        ],
        out_specs=tile_blk,
        scratch_shapes=[
            pltpu.VMEM((D_MODEL, D_EXPERT), BF16),
            pltpu.VMEM((D_MODEL, D_EXPERT), BF16),
            pltpu.VMEM((D_EXPERT, D_MODEL), BF16),
        ],
    )
    return pl.pallas_call(
        _expert_kernel,
        grid_spec=grid_spec,
        out_shape=jax.ShapeDtypeStruct(xs.shape, jnp.uint32),
        compiler_params=pltpu.CompilerParams(dimension_semantics=("arbitrary",), vmem_limit_bytes=VMEM_LIMIT),
        name="experts",
    )(tile_expert, n_used, xs, wg, wu, wd)


SC_CHUNK = 128
SC_PAD_CHUNK = 64


def _sc_workers():
    info = plsc.get_sparse_core_info()
    return info.num_cores, info.num_cores * info.num_subcores


def _sc_split(per_w):
    n_full = per_w // SC_CHUNK
    tail = per_w - n_full * SC_CHUNK
    assert tail % 8 == 0
    return n_full, tail


def _sc_dispatch_call(pos_flat, pad_rows, x_tiles, tile_rows, n_sorted_rows):
    t_all = x_tiles.shape[0] // tile_rows
    dt = x_tiles.dtype
    nc, nw = _sc_workers()
    per_w = t_all // nw
    n_ch, tail = _sc_split(per_w)
    n_pad = pad_rows.shape[0]
    pad_per_w = n_pad // nw
    n_pch = pad_per_w // SC_PAD_CHUNK
    assert per_w * nw == t_all and n_pch * SC_PAD_CHUNK * nw == n_pad
    zeros = jnp.zeros((SC_PAD_CHUNK, tile_rows, LANES), dt)
    tail_scratch = [] if tail == 0 else [
        pltpu.VMEM((tail,), jnp.int32), pltpu.VMEM((tail,), jnp.int32), pltpu.VMEM((tail, tile_rows, LANES), dt)]

    @functools.partial(
        pl.kernel, mesh=plsc.VectorSubcoreMesh(core_axis_name="c", subcore_axis_name="s"),
        out_type=jax.ShapeDtypeStruct((n_sorted_rows, tile_rows, LANES), dt),
        scratch_types=[
            pltpu.VMEM((SC_CHUNK,), jnp.int32), pltpu.VMEM((SC_CHUNK,), jnp.int32),
            pltpu.VMEM((SC_CHUNK, tile_rows, LANES), dt),
            pltpu.VMEM((SC_PAD_CHUNK,), jnp.int32),
            pltpu.VMEM((SC_PAD_CHUNK, tile_rows, LANES), dt),
        ] + tail_scratch,
    )
    def k(pos_hbm, pad_hbm, x_hbm, z_hbm, xs_hbm, i0_v, i1_v, rows_v, ip_v, z_v, *tail_refs):
        wid = lax.axis_index("s") * nc + lax.axis_index("c")

        def move(b, n, idx0, idx1, rows):
            pltpu.sync_copy(pos_hbm.at[pl.ds(b, n)], idx0)
            pltpu.sync_copy(pos_hbm.at[pl.ds(t_all + b, n)], idx1)
            pltpu.sync_copy(x_hbm.at[pl.ds(b, n)], rows)
            pltpu.sync_copy(rows, xs_hbm.at[idx0])
            pltpu.sync_copy(rows, xs_hbm.at[idx1])

        @pl.loop(0, n_ch)
        def _(j):
            move(pl.multiple_of(wid * per_w + j * SC_CHUNK, 8), SC_CHUNK, i0_v, i1_v, rows_v)

        if tail:
            move(pl.multiple_of(wid * per_w + n_ch * SC_CHUNK, 8), tail, *tail_refs)

        pltpu.sync_copy(z_hbm, z_v)

        @pl.loop(0, n_pch)
        def _(j):
            b = pl.multiple_of(wid * pad_per_w + j * SC_PAD_CHUNK, 8)
            pltpu.sync_copy(pad_hbm.at[pl.ds(b, SC_PAD_CHUNK)], ip_v)
            pltpu.sync_copy(z_v, xs_hbm.at[ip_v])

    xs = k(pos_flat, pad_rows, x_tiles.reshape(t_all, tile_rows, LANES), zeros)
    return xs.reshape(n_sorted_rows * tile_rows, LANES)


def _sc_gather_call(pos_flat, outs, tile_rows):
    n_assign = pos_flat.shape[0]
    dt = outs.dtype
    nc, nw = _sc_workers()
    per_w = n_assign // nw
    n_ch, tail = _sc_split(per_w)
    assert per_w * nw == n_assign
    tail_scratch = [] if tail == 0 else [pltpu.VMEM((tail,), jnp.int32), pltpu.VMEM((tail, tile_rows, LANES), dt)]

    @functools.partial(
        pl.kernel, mesh=plsc.VectorSubcoreMesh(core_axis_name="c", subcore_axis_name="s"),
        out_type=jax.ShapeDtypeStruct((n_assign, tile_rows, LANES), dt),
        scratch_types=[pltpu.VMEM((SC_CHUNK,), jnp.int32),
                       pltpu.VMEM((SC_CHUNK, tile_rows, LANES), dt)] + tail_scratch,
    )
    def k(pos_hbm, o_hbm, y_hbm, i_v, rows_v, *tail_refs):
        wid = lax.axis_index("s") * nc + lax.axis_index("c")

        def move(b, n, idx, rows):
            pltpu.sync_copy(pos_hbm.at[pl.ds(b, n)], idx)
            pltpu.sync_copy(o_hbm.at[idx], rows)
            pltpu.sync_copy(rows, y_hbm.at[pl.ds(b, n)])

        @pl.loop(0, n_ch)
        def _(j):
            move(pl.multiple_of(wid * per_w + j * SC_CHUNK, 8), SC_CHUNK, i_v, rows_v)

        if tail:
            move(pl.multiple_of(wid * per_w + n_ch * SC_CHUNK, 8), tail, *tail_refs)

    y = k(pos_flat, outs.reshape(-1, tile_rows, LANES))
    return y.reshape(n_assign * tile_rows, LANES)


def _combine_dense_kernel(alpha, n_main_tiles, x1t_ref, y0_ref, y1_ref, w_ref, g_ref, b_ref, out_ref, *tail):
    j = pl.program_id(0)
    w = w_ref[...]
    y = (w[:, 0:1] * _load_packed_tiles(y0_ref, TOK_TILE, F32)
         + w[:, 1:2] * _load_packed_tiles(y1_ref, TOK_TILE, F32))
    res = _layer_norm(alpha * x1t_ref[...] + y, g_ref[...], b_ref[...])
    if n_main_tiles is None:
        out_ref[...] = res
    else:
        @pl.when(j < n_main_tiles)
        def _():
            out_ref[...] = res

        @pl.when(j >= n_main_tiles)
        def _():
            tail[0][...] = res


def _combine_dense_call(x1t, y_pair, w_col, g, b, alpha, n_main_tiles=None):
    t_all = x1t.shape[0]
    n_tiles = t_all // TOK_TILE
    if n_main_tiles is None:
        out_specs = pl.BlockSpec((TOK_TILE, D_MODEL), lambda i: (i, 0))
        out_shape = jax.ShapeDtypeStruct((t_all, D_MODEL), F32)
    else:
        assert n_tiles == n_main_tiles + 1
        out_specs = [pl.BlockSpec((TOK_TILE, D_MODEL), lambda i: (jnp.minimum(i, n_main_tiles - 1), 0)),
                     pl.BlockSpec((TOK_TILE, D_MODEL), lambda i: (0, 0))]
        out_shape = [jax.ShapeDtypeStruct((n_main_tiles * TOK_TILE, D_MODEL), F32),
                     jax.ShapeDtypeStruct((TOK_TILE, D_MODEL), F32)]
    tile = (TOK_TILE, D_MODEL)
    packed = (TOK_TILE * PACK_ROWS, LANES)
    return pl.pallas_call(
        functools.partial(_combine_dense_kernel, alpha, n_main_tiles),
        grid=(n_tiles,),
        in_specs=[
            pl.BlockSpec(tile, lambda i: (i, 0)),
            pl.BlockSpec(packed, lambda i: (i, 0)),
            pl.BlockSpec(packed, lambda i: (i + n_tiles, 0)),
            pl.BlockSpec((TOK_TILE, TOP_K), lambda i: (i, 0)),
            pl.BlockSpec((1, D_MODEL), lambda i: (0, 0)),
            pl.BlockSpec((1, D_MODEL), lambda i: (0, 0)),
        ],
        out_specs=out_specs,
        out_shape=out_shape,
        compiler_params=pltpu.CompilerParams(dimension_semantics=("arbitrary",), vmem_limit_bytes=VMEM_LIMIT),
        name="combine",
    )(x1t, y_pair, y_pair, w_col, g, b)


def _rope_tables(seq, dec_seq, past_len):
    half = HEAD_DIM // 2
    inv = ROPE_THETA ** (-jnp.arange(half, dtype=F32) / half)
    pos = jnp.concatenate([jnp.arange(seq), past_len + (jnp.arange(TOK_TILE) % dec_seq)])
    ang = pos.astype(F32)[:, None] * inv[None, :]
    cos = jnp.cos(ang)
    sin = jnp.sin(ang)
    zero = jnp.zeros_like(sin)
    reps = LANES // HEAD_DIM
    cos_t = jnp.tile(jnp.concatenate([cos, cos], -1), (1, reps))
    sa_t = jnp.tile(jnp.concatenate([-sin, zero], -1), (1, reps))
    sb_t = jnp.tile(jnp.concatenate([zero, sin], -1), (1, reps))
    return cos_t, sa_t, sb_t


def _inclusive_cumsum(v):
    n = v.shape[0]
    tri = jnp.arange(n)[None, :] <= jnp.arange(n)[:, None]
    return jnp.sum(jnp.where(tri, v[None, :], 0), axis=1)


def _dispatch_plan(ridx, counts, n_tiles):
    t_all = ridx.shape[1]
    tiles_e = (counts + EXPERT_TILE - 1) // EXPERT_TILE
    tile_end = _inclusive_cumsum(tiles_e)
    offs = (tile_end - tiles_e) * EXPERT_TILE
    n_used = tile_end[-1:]
    tile_expert = jnp.minimum(
        jnp.sum((jnp.arange(n_tiles)[:, None] >= tile_end[None, :]).astype(jnp.int32), axis=1), N_EXPERTS - 1)
    experts = jnp.arange(N_EXPERTS)
    pos = ridx[TOP_K:2 * TOP_K] + jnp.sum(
        jnp.where(ridx[:TOP_K, :, None] == experts[None, None, :], offs[None, None, :], 0), axis=-1)
    gap_start = jnp.concatenate([offs + counts, n_used * EXPERT_TILE])
    gap_len = jnp.concatenate([tiles_e * EXPERT_TILE - counts, (n_tiles - n_used) * EXPERT_TILE])
    gap_end_q = _inclusive_cumsum(gap_len)
    q = jnp.arange(n_tiles * EXPERT_TILE - TOP_K * t_all)
    gap = jnp.sum((q[:, None] >= gap_end_q[None, :]).astype(jnp.int32), axis=1)
    in_gap = gap[:, None] == jnp.arange(N_EXPERTS + 1)[None, :]
    pad_rows = q + jnp.sum(jnp.where(in_gap, (gap_start - gap_end_q + gap_len)[None, :], 0), axis=1)
    return (pos.reshape(-1).astype(jnp.int32), pad_rows.astype(jnp.int32),
            tile_expert.astype(jnp.int32), n_used.astype(jnp.int32))


def kernel(x_prompt, x_sample, cache_k, cache_v, w_in, sinks, gm_ln_g, gm_ln_b, gm_ws, gm_bs,
           out_norm_a, out_norm_b, w_o, ln1_g, ln1_b, w_router, router_bias,
           w_gate, w_up, w_down, ln2_g, ln2_b):
    batch, seq, _ = x_prompt.shape
    dec_batch, dec_seq, _ = x_sample.shape
    depth = w_in.shape[0]
    past_len = PAST_LEN
    assert dec_batch * dec_seq == TOK_TILE and dec_seq == CHUNK and seq % TOK_TILE == 0
    assert cache_k.shape[2] == WINDOW
    alpha = (2 * depth) ** 0.25
    tiles_per_seq = seq // TOK_TILE

    tables = _rope_tables(seq, dec_seq, past_len)
    wrt = w_router.T
    rb = router_bias.reshape(N_EXPERTS, 1)

    streams = [
        dict(x=(x_prompt.reshape(batch * seq, D_MODEL), x_sample.reshape(-1, D_MODEL)),
             n_seq=batch, has_sample=True),
    ]
    kp, vp, ks, vs, gms = [], [], [], [], []
    for l in range(depth):
        layer_args = (
            sinks[l], w_in,
            cache_k[l].reshape(dec_batch, WINDOW, KV_WIDTH), cache_v[l].reshape(dec_batch, WINDOW, KV_WIDTH),
            gm_ln_g[l].reshape(1, GM_WIDTH), gm_ln_b[l].reshape(1, GM_WIDTH),
            gm_ws[l], gm_bs[l].T,
            out_norm_a[l].reshape(1, ATT_WIDTH), out_norm_b[l].reshape(1, GM_WIDTH),
            w_o,
            ln1_g[l].reshape(1, D_MODEL), ln1_b[l].reshape(1, D_MODEL), wrt, rb)
        k_tails, v_tails = [], []
        for st in streams:
            n_prompt = st["n_seq"] * seq
            n_prompt_tiles = n_prompt // TOK_TILE
            t_all = n_prompt + (TOK_TILE if st["has_sample"] else 0)
            n_exp_tiles = (TOP_K * t_all) // EXPERT_TILE + N_EXPERTS
            res = _mixer_call(layer_args, st["x"], tables, n_prompt_tiles, tiles_per_seq, alpha, l)
            if st["has_sample"]:
                x1t, x1p, k_tail, v_tail, k_s, v_s, vn_s, ridx, rw, counts = res
            else:
                x1t, x1p, k_tail, v_tail, ridx, rw, counts = res

            pos, pad_rows, tile_expert, n_used = _dispatch_plan(ridx, counts[:, 0], n_exp_tiles)
            xs = _sc_dispatch_call(pos, pad_rows, x1p, PACK_ROWS, n_exp_tiles * EXPERT_TILE)
            outs = _expert_call(tile_expert, n_used, xs, w_gate, w_up, w_down, l)
            y_pair = _sc_gather_call(pos, outs, PACK_ROWS)
            combine_args = (x1t, y_pair, rw[:TOP_K].T, ln2_g[l].reshape(1, D_MODEL), ln2_b[l].reshape(1, D_MODEL))
            if l < depth - 1:
                st["x"] = combine_args
            else:
                st["x"] = _combine_dense_call(*combine_args, alpha,
                                              n_main_tiles=n_prompt_tiles if st["has_sample"] else None)

            k_tails.append(k_tail)
            v_tails.append(v_tail)
            if st["has_sample"]:
                ks.append(k_s.reshape(dec_batch, dec_seq, N_KV_HEADS, HEAD_DIM))
                vs.append(v_s.reshape(dec_batch, dec_seq, N_KV_HEADS, HEAD_DIM))
                gms.append(vn_s.reshape(dec_batch, dec_seq, GM_WIDTH))
        kp.append(jnp.concatenate(k_tails, axis=0).reshape(batch, WINDOW, N_KV_HEADS, HEAD_DIM))
        vp.append(jnp.concatenate(v_tails, axis=0).reshape(batch, WINDOW, N_KV_HEADS, HEAD_DIM))

    y_prompt = jnp.concatenate([st["x"][0] if st["has_sample"] else st["x"] for st in streams],
                               axis=0).reshape(batch, seq, D_MODEL)
    y_sample = streams[-1]["x"][1].reshape(dec_batch, dec_seq, D_MODEL)
    return (y_prompt, y_sample, jnp.stack(kp), jnp.stack(vp), jnp.stack(ks), jnp.stack(vs), jnp.stack(gms))
```

```python
import functools

import jax
import jax.numpy as jnp
from jax import lax
from jax.experimental import pallas as pl
from jax.experimental.pallas import tpu as pltpu
from jax.experimental.pallas import tpu_sc as plsc

D_MODEL = 1024
CHUNK = 64
WINDOW = 128
HEAD_DIM = 64
N_Q_HEADS = 8
N_KV_HEADS = 2
Q_PER_KV = N_Q_HEADS // N_KV_HEADS
PAST_LEN = 2048
ATT_WIDTH = N_Q_HEADS * HEAD_DIM
KV_WIDTH = N_KV_HEADS * HEAD_DIM
ROPE_THETA = 10000.0
GM_GROUPS = 4
GM_CH = 128
GM_WIDTH = GM_GROUPS * GM_CH
GM_CHUNK = 128
D_MIX = ATT_WIDTH + GM_WIDTH
D_IN = ATT_WIDTH + 2 * KV_WIDTH + 2 * GM_WIDTH
N_EXPERTS = 16
N_EXPERT_GROUPS = 4
EXPERTS_PER_GROUP = N_EXPERTS // N_EXPERT_GROUPS
TOP_K = 2
D_EXPERT = 512
LN_EPS = 1e-5
NEG_INF = -1e30

LANES = 128
PACK_ROWS = D_MODEL // (2 * LANES)
HI_HALF = 0xFFFF0000
TOK_TILE = 512
EXPERT_TILE = 512
KEYS_PER_CHUNK = WINDOW + CHUNK
VMEM_LIMIT = 60 * 1024 * 1024

F32 = jnp.float32
BF16 = jnp.bfloat16


def _layer_norm(x, g, b):
    mu = jnp.mean(x, axis=-1, keepdims=True)
    d = x - mu
    var = jnp.mean(d * d, axis=-1, keepdims=True)
    return d * lax.rsqrt(var + LN_EPS) * g + b


def _rms_norm(x, g):
    return x * lax.rsqrt(jnp.mean(x * x, axis=-1, keepdims=True) + LN_EPS) * g


def _gelu(x):
    return 0.5 * x * (1.0 + lax.erf(x * (0.5 ** 0.5)))


def _bf16_bits(x):
    return lax.bitcast_convert_type(x.astype(BF16).astype(F32), jnp.uint32)


def _store_packed_tiles(ref, x):
    n = x.shape[0]
    for s in range(PACK_ROWS):
        lo = _bf16_bits(x[:, (2 * s) * LANES:(2 * s + 1) * LANES]) >> 16
        hi = _bf16_bits(x[:, (2 * s + 1) * LANES:(2 * s + 2) * LANES]) & jnp.uint32(HI_HALF)
        ref[pl.ds(s, n, stride=PACK_ROWS), :] = hi | lo


def _load_packed_tiles(ref, n, dtype):
    cols = []
    for s in range(PACK_ROWS):
        w = ref[pl.ds(s, n, stride=PACK_ROWS), :]
        cols.append(lax.bitcast_convert_type(w << 16, F32).astype(dtype))
        cols.append(lax.bitcast_convert_type(w & jnp.uint32(HI_HALF), F32).astype(dtype))
    return jnp.concatenate(cols, axis=-1)


def _first_index_of_max(rows):
    m = rows[0]
    for r in rows[1:]:
        m = jnp.maximum(m, r)
    idx = jnp.full(m.shape, len(rows), jnp.int32)
    for e in reversed(range(len(rows))):
        idx = jnp.where(rows[e] == m, e, idx)
    return m, idx


def _mixer_kernel(n_prompt_tiles, tiles_per_seq, alpha, has_sample, first_layer, sinks_ref, *refs):
    n_x = 2 if first_layer else 6
    x_refs, refs = refs[:n_x], refs[n_x:]
    (w_in_ref, cos_ref, sa_ref, sb_ref, ck_ref, cv_ref, lng_ref, lnb_ref, ws_ref, bst_ref, na_ref, nb_ref, w_o_ref,
     l1g_ref, l1b_ref, wrt_ref, rb_ref, tri_ref, x1t_ref, x1p_ref, kt_ref, vt_ref) = refs[:22]
    rest = refs[22:]
    if has_sample:
        (ks_ref, vs_ref, vn_ref, ridx_ref, rw_ref, counts_ref,
         kd_ref, vd_ref, qm_ref, att_ref, gm_ref, cnt_ref, w_in_s, w_o_s, x_s) = rest
    else:
        (ridx_ref, rw_ref, counts_ref,
         kd_ref, vd_ref, qm_ref, att_ref, gm_ref, cnt_ref, w_in_s, w_o_s, x_s) = rest
    i = pl.program_id(0)
    is_sample = i >= n_prompt_tiles
    seq_start = jnp.logical_and(jnp.logical_not(is_sample), (i % tiles_per_seq) == 0)
    seq_end = jnp.logical_and(jnp.logical_not(is_sample), (i % tiles_per_seq) == tiles_per_seq - 1)
    tm = TOK_TILE

    @pl.when(i == 0)
    def _():
        w_in_s[...] = w_in_ref[0].astype(BF16)
        w_o_s[...] = w_o_ref[0].astype(BF16)

    if first_layer:
        xp_ref, xs_ref = x_refs
        x_s[...] = jnp.where(is_sample, xs_ref[...], xp_ref[...])
    else:
        x1_prev_ref, y0_ref, y1_ref, wc_ref, l2g_ref, l2b_ref = x_refs
        wc = wc_ref[...].T
        y_moe = (wc[:, 0:1] * _load_packed_tiles(y0_ref, tm, F32)
                 + wc[:, 1:2] * _load_packed_tiles(y1_ref, tm, F32))
        x_s[...] = _layer_norm(alpha * x1_prev_ref[...] + y_moe, l2g_ref[...], l2b_ref[...])

    h = jnp.dot(x_s[...].astype(BF16), w_in_s[...], preferred_element_type=F32)

    cos = cos_ref[...]
    sa = sa_ref[...]
    sb = sb_ref[...]

    def rope(blk):
        return blk * cos + pltpu.roll(blk, LANES - HEAD_DIM // 2, 1) * sa + pltpu.roll(blk, HEAD_DIM // 2, 1) * sb

    lane = lax.broadcasted_iota(jnp.int32, (1, LANES), 1)
    lo_half = lane < HEAD_DIM

    o_k = ATT_WIDTH
    o_v = o_k + KV_WIDTH
    o_u = o_v + KV_WIDTH
    o_g = o_u + GM_WIDTH

    k_rot = rope(h[:, o_k:o_v])
    v_new = h[:, o_v:o_u]

    @pl.when(seq_end)
    def _():
        kt_ref[...] = k_rot[tm - WINDOW:]
        vt_ref[...] = v_new[tm - WINDOW:]

    if has_sample:
        @pl.when(is_sample)
        def _():
            ks_ref[...] = k_rot
            vs_ref[...] = v_new

    def dup_heads(a):
        sw = pltpu.roll(a, HEAD_DIM, 1)
        return jnp.where(lo_half, a, sw).astype(BF16), jnp.where(lo_half, sw, a).astype(BF16)

    k_d = dup_heads(k_rot)
    v_d = dup_heads(v_new)

    @pl.when(jnp.logical_not(is_sample))
    def _():
        @pl.when(seq_start)
        def _():
            for g in range(N_KV_HEADS):
                kd_ref[g, 0:WINDOW, :] = jnp.zeros((WINDOW, LANES), BF16)
                vd_ref[g, 0:WINDOW, :] = jnp.zeros((WINDOW, LANES), BF16)

        @pl.when(jnp.logical_not(seq_start))
        def _():
            for g in range(N_KV_HEADS):
                kd_ref[g, 0:WINDOW, :] = kd_ref[g, tm:tm + WINDOW, :]
                vd_ref[g, 0:WINDOW, :] = vd_ref[g, tm:tm + WINDOW, :]

        for g in range(N_KV_HEADS):
            kd_ref[g, WINDOW:WINDOW + tm, :] = k_d[g]
            vd_ref[g, WINDOW:WINDOW + tm, :] = v_d[g]

    @pl.when(is_sample)
    def _():
        for b in range(tm // CHUNK):
            ck = dup_heads(ck_ref[b])
            cv = dup_heads(cv_ref[b])
            base = b * KEYS_PER_CHUNK
            for g in range(N_KV_HEADS):
                kd_ref[g, base:base + WINDOW, :] = ck[g]
                vd_ref[g, base:base + WINDOW, :] = cv[g]
                kd_ref[g, base + WINDOW:base + KEYS_PER_CHUNK, :] = k_d[g][b * CHUNK:(b + 1) * CHUNK]
                vd_ref[g, base + WINDOW:base + KEYS_PER_CHUNK, :] = v_d[g][b * CHUNK:(b + 1) * CHUNK]

    scale = HEAD_DIM ** -0.5
    for b in range(ATT_WIDTH // LANES):
        qb = (rope(h[:, b * LANES:(b + 1) * LANES]) * scale).astype(BF16)
        zero = jnp.zeros_like(qb)
        qm_ref[2 * b] = jnp.where(lo_half, qb, zero)
        qm_ref[2 * b + 1] = jnp.where(lo_half, zero, qb)

    key_stride = jnp.where(is_sample, KEYS_PER_CHUNK, CHUNK)
    col = lax.broadcasted_iota(jnp.int32, (1, KEYS_PER_CHUNK), 1)
    row_blk = lax.broadcasted_iota(jnp.int32, (Q_PER_KV * CHUNK, 1), 0) // CHUNK
    sink_cols = [
        jnp.where(row_blk == 0, sinks_ref[Q_PER_KV * g],
                  jnp.where(row_blk == 1, sinks_ref[Q_PER_KV * g + 1],
                            jnp.where(row_blk == 2, sinks_ref[Q_PER_KV * g + 2], sinks_ref[Q_PER_KV * g + 3])))
        for g in range(N_KV_HEADS)]

    for c in range(tm // CHUNK):
        r0 = c * CHUNK
        k0 = pl.multiple_of(c * key_stride, CHUNK)
        outs = []
        for g in range(N_KV_HEADS):
            q_st = jnp.concatenate([qm_ref[Q_PER_KV * g + r, pl.ds(r0, CHUNK), :] for r in range(Q_PER_KV)],
                                   axis=0)
            keys = kd_ref[g, pl.ds(k0, KEYS_PER_CHUNK), :]
            vals = vd_ref[g, pl.ds(k0, KEYS_PER_CHUNK), :]
            s = lax.dot_general(q_st, keys, (((1,), (1,)), ((), ())), preferred_element_type=F32)
            if c < WINDOW // CHUNK:
                first_valid = jnp.where(seq_start, (WINDOW // CHUNK - c) * CHUNK, 0)
                s = jnp.where(col >= first_valid, s, NEG_INF)
            sink = sink_cols[g]
            m = jnp.maximum(jnp.max(s, axis=-1, keepdims=True), sink)
            p = jnp.exp(s - m)
            denom = jnp.sum(p, axis=-1, keepdims=True) + jnp.exp(sink - m)
            o = jnp.dot(p.astype(BF16), vals, preferred_element_type=F32) * (1.0 / denom)
            for bb in range(2):
                outs.append(jnp.where(lo_half, o[(2 * bb) * CHUNK:(2 * bb + 1) * CHUNK],
                                      o[(2 * bb + 1) * CHUNK:(2 * bb + 2) * CHUNK]))
        for b in range(ATT_WIDTH // LANES):
            att_ref[pl.ds(r0, CHUNK), b * LANES:(b + 1) * LANES] = outs[b]

    prow = lax.broadcasted_iota(jnp.int32, (GM_CHUNK, GM_CHUNK), 0)
    pcol = lax.broadcasted_iota(jnp.int32, (GM_CHUNK, GM_CHUNK), 1)
    half = GM_CHUNK // 2
    tril = pcol <= prow
    same_blk = (prow < half) == (pcol < half)
    prow1 = lax.broadcasted_iota(jnp.int32, (GM_CHUNK, 1), 0)
    bst = bst_ref[...]
    bst_s = jnp.where(prow1 < half, bst, pltpu.roll(bst, half, 0))
    bias = jnp.where(is_sample, bst_s, bst)
    for g in range(GM_GROUPS):
        wg_ = ws_ref[g]
        w_s = jnp.where(prow < half, wg_, pltpu.roll(pltpu.roll(wg_, half, 0), half, 1))
        w_eff = jnp.where(is_sample, jnp.where(same_blk, w_s, 0.0), wg_)
        w_eff = jnp.where(tril, w_eff, 0.0).astype(BF16)
        cols = slice(g * GM_CH, (g + 1) * GM_CH)
        ug = _gelu(h[:, o_u + g * GM_CH:o_u + (g + 1) * GM_CH])
        vn = _layer_norm(_gelu(h[:, o_g + g * GM_CH:o_g + (g + 1) * GM_CH]), lng_ref[:, cols], lnb_ref[:, cols])

        if has_sample:
            @pl.when(is_sample)
            def _(cols=cols, vn=vn):
                vn_ref[:, cols] = vn

        vn_b = vn.astype(BF16)
        for n in range(tm // GM_CHUNK):
            rows = slice(n * GM_CHUNK, (n + 1) * GM_CHUNK)
            s = jnp.dot(w_eff, vn_b[rows], preferred_element_type=F32) + bias[:, g:g + 1]
            gm_ref[rows, cols] = ug[rows] * s

    mixed = jnp.concatenate([_rms_norm(att_ref[...], na_ref[...]), _rms_norm(gm_ref[...], nb_ref[...])], axis=-1)
    y = alpha * x_s[...] + jnp.dot(mixed.astype(BF16), w_o_s[...], preferred_element_type=F32)
    x1 = _layer_norm(y, l1g_ref[...], l1b_ref[...])
    x1t_ref[...] = x1
    _store_packed_tiles(x1p_ref, x1)

    nt = (((1,), (1,)), ((), ()))
    wr = wrt_ref[...]
    wr_hi = wr.astype(BF16)
    wr_lo = (wr - wr_hi.astype(F32)).astype(BF16)
    x1_hi = x1.astype(BF16)
    x1_lo = (x1 - x1_hi.astype(F32)).astype(BF16)
    logits = (lax.dot_general(wr_hi, x1_hi, nt, preferred_element_type=F32)
              + (lax.dot_general(wr_hi, x1_lo, nt, preferred_element_type=F32)
                 + lax.dot_general(wr_lo, x1_hi, nt, preferred_element_type=F32)))
    scores = jax.nn.sigmoid(logits)
    sel = scores + rb_ref[...]
    sel_rows = [sel[e:e + 1, :] for e in range(N_EXPERTS)]
    sc_rows = [scores[e:e + 1, :] for e in range(N_EXPERTS)]
    grp = []
    for g in range(N_EXPERT_GROUPS):
        r = sel_rows[g * EXPERTS_PER_GROUP:(g + 1) * EXPERTS_PER_GROUP]
        best_pair = None
        for a in range(EXPERTS_PER_GROUP):
            for b in range(a + 1, EXPERTS_PER_GROUP):
                pair = r[a] + r[b]
                best_pair = pair if best_pair is None else jnp.maximum(best_pair, pair)
        grp.append(best_pair)
    _, best = _first_index_of_max(grp)
    masked = [jnp.where(best == (e // EXPERTS_PER_GROUP), sel_rows[e], NEG_INF) for e in range(N_EXPERTS)]
    _, e0 = _first_index_of_max(masked)
    masked2 = [jnp.where(e0 == e, -jnp.inf, masked[e]) for e in range(N_EXPERTS)]
    _, e1 = _first_index_of_max(masked2)
    w0 = jnp.zeros_like(sc_rows[0])
    w1 = jnp.zeros_like(sc_rows[0])
    for e in range(N_EXPERTS):
        w0 = jnp.where(e0 == e, sc_rows[e], w0)
        w1 = jnp.where(e1 == e, sc_rows[e], w1)
    wsum = w0 + w1
    rw_ref[...] = jnp.concatenate([w0 / wsum, w1 / wsum, jnp.zeros((6, tm), F32)], axis=0)

    @pl.when(i == 0)
    def _():
        cnt_ref[...] = jnp.zeros_like(cnt_ref)

    eid = lax.broadcasted_iota(jnp.int32, (N_EXPERTS, tm), 0)
    oh0 = (eid == e0).astype(F32)
    oh1 = (eid == e1).astype(F32)
    oh = oh0 + oh1
    incl = jnp.dot(oh.astype(BF16), tri_ref[...], preferred_element_type=F32)
    before = cnt_ref[:, 0:1] + incl - oh
    r0 = jnp.sum(oh0 * before, axis=0, keepdims=True).astype(jnp.int32)
    r1 = jnp.sum(oh1 * before, axis=0, keepdims=True).astype(jnp.int32)
    cnt = cnt_ref[...] + incl[:, tm - 1:tm]
    cnt_ref[...] = cnt
    counts_ref[...] = cnt.astype(jnp.int32)
    ridx_ref[...] = jnp.concatenate([e0, e1, r0, r1, jnp.zeros((4, tm), jnp.int32)], axis=0)


def _mixer_call(layer_args, x_all, tables, n_prompt_tiles, tiles_per_seq, alpha, layer):
    (sinks, w_in, ck, cv, lng, lnb, ws, bst, na, nb, w_o, l1g, l1b, wrt, rb) = layer_args
    cos_t, sa_t, sb_t = tables
    first_layer = len(x_all) == 2
    if first_layer:
        x_main, x_tail = x_all
        t_all = x_main.shape[0] + x_tail.shape[0]
        n_tiles = t_all // TOK_TILE
        x_args = [x_main, x_tail]
        x_specs = [pl.BlockSpec((TOK_TILE, D_MODEL), lambda i: (jnp.minimum(i, n_prompt_tiles - 1), 0)),
                   pl.BlockSpec((TOK_TILE, D_MODEL), lambda i: (0, 0))]
    else:
        x1t_prev, y_pair, w_col, l2g, l2b = x_all
        t_all = x1t_prev.shape[0]
        n_tiles = t_all // TOK_TILE
        x_args = [x1t_prev, y_pair, y_pair, w_col, l2g, l2b]
        x_specs = [pl.BlockSpec((TOK_TILE, D_MODEL), lambda i: (i, 0)),
                   pl.BlockSpec((TOK_TILE * PACK_ROWS, LANES), lambda i: (i, 0)),
                   pl.BlockSpec((TOK_TILE * PACK_ROWS, LANES), lambda i: (i + n_tiles, 0)),
                   pl.BlockSpec((8, TOK_TILE), lambda i: (0, i)),
                   pl.BlockSpec((1, D_MODEL), lambda i: (0, 0)), pl.BlockSpec((1, D_MODEL), lambda i: (0, 0))]
    n_seq_tiles = tiles_per_seq
    has_sample = n_tiles > n_prompt_tiles

    def const(shape):
        nd = len(shape)
        return pl.BlockSpec(shape, lambda i, _nd=nd: (0,) * _nd)

    def tab_map(i):
        return (jnp.where(i < n_prompt_tiles, i % n_seq_tiles, n_seq_tiles), 0)

    n_seq = n_prompt_tiles // n_seq_tiles

    def seq_map(i):
        return (jnp.minimum(i // n_seq_tiles, n_seq - 1), 0)

    row_blk = lambda w: pl.BlockSpec((TOK_TILE, w), lambda i: (i, 0))
    in_specs = [
        pl.BlockSpec(memory_space=pltpu.SMEM),
        *x_specs,
        pl.BlockSpec((1, D_MODEL, D_IN), lambda i: (layer, 0, 0)),
        pl.BlockSpec((TOK_TILE, LANES), tab_map),
        pl.BlockSpec((TOK_TILE, LANES), tab_map),
        pl.BlockSpec((TOK_TILE, LANES), tab_map),
        pl.BlockSpec((TOK_TILE // CHUNK, WINDOW, KV_WIDTH), lambda i: (layer, 0, 0)),
        pl.BlockSpec((TOK_TILE // CHUNK, WINDOW, KV_WIDTH), lambda i: (layer, 0, 0)),
        const((1, GM_WIDTH)), const((1, GM_WIDTH)),
        const((GM_GROUPS, GM_CHUNK, GM_CHUNK)), const((GM_CHUNK, GM_GROUPS)),
        const((1, ATT_WIDTH)), const((1, GM_WIDTH)),
        pl.BlockSpec((1, D_MIX, D_MODEL), lambda i: (layer, 0, 0)),
        const((1, D_MODEL)), const((1, D_MODEL)),
        const((N_EXPERTS, D_MODEL)), const((N_EXPERTS, 1)),
        const((TOK_TILE, TOK_TILE)),
    ]
    out_shape = [
        jax.ShapeDtypeStruct((t_all, D_MODEL), F32),
        jax.ShapeDtypeStruct((t_all * PACK_ROWS, LANES), jnp.uint32),
        jax.ShapeDtypeStruct((n_seq * WINDOW, KV_WIDTH), F32),
        jax.ShapeDtypeStruct((n_seq * WINDOW, KV_WIDTH), F32),
        jax.ShapeDtypeStruct((TOK_TILE, KV_WIDTH), F32),
        jax.ShapeDtypeStruct((TOK_TILE, KV_WIDTH), F32),
        jax.ShapeDtypeStruct((TOK_TILE, GM_WIDTH), F32),
        jax.ShapeDtypeStruct((8, t_all), jnp.int32),
        jax.ShapeDtypeStruct((8, t_all), F32),
        jax.ShapeDtypeStruct((N_EXPERTS, LANES), jnp.int32),
    ]
    out_specs = [
        pl.BlockSpec((TOK_TILE, D_MODEL), lambda i: (i, 0)),
        pl.BlockSpec((TOK_TILE * PACK_ROWS, LANES), lambda i: (i, 0)),
        pl.BlockSpec((WINDOW, KV_WIDTH), seq_map), pl.BlockSpec((WINDOW, KV_WIDTH), seq_map),
        const((TOK_TILE, KV_WIDTH)), const((TOK_TILE, KV_WIDTH)),
        const((TOK_TILE, GM_WIDTH)),
        pl.BlockSpec((8, TOK_TILE), lambda i: (0, i)),
        pl.BlockSpec((8, TOK_TILE), lambda i: (0, i)),
        const((N_EXPERTS, LANES)),
    ]
    if not has_sample:
        del out_shape[4:7], out_specs[4:7]
    kd_rows = (TOK_TILE // CHUNK) * KEYS_PER_CHUNK
    scratch = [
        pltpu.VMEM((N_KV_HEADS, kd_rows, LANES), BF16),
        pltpu.VMEM((N_KV_HEADS, kd_rows, LANES), BF16),
        pltpu.VMEM((N_Q_HEADS, TOK_TILE, LANES), BF16),
        pltpu.VMEM((TOK_TILE, ATT_WIDTH), F32),
        pltpu.VMEM((TOK_TILE, GM_WIDTH), F32),
        pltpu.VMEM((N_EXPERTS, LANES), F32),
        pltpu.VMEM((D_MODEL, D_IN), BF16),
        pltpu.VMEM((D_MIX, D_MODEL), BF16),
        pltpu.VMEM((TOK_TILE, D_MODEL), F32),
    ]
    return pl.pallas_call(
        functools.partial(_mixer_kernel, n_prompt_tiles, tiles_per_seq, alpha, has_sample, first_layer),
        grid=(n_tiles,),
        in_specs=in_specs, out_specs=out_specs, out_shape=out_shape,
        scratch_shapes=scratch,
        compiler_params=pltpu.CompilerParams(dimension_semantics=("arbitrary",), vmem_limit_bytes=VMEM_LIMIT),
        name="mixer",
    )(sinks, *x_args, w_in, cos_t, sa_t, sb_t, ck, cv, lng, lnb, ws, bst, na, nb, w_o, l1g, l1b, wrt, rb,
      jnp.triu(jnp.ones((TOK_TILE, TOK_TILE), BF16)))


def _expert_kernel(te_ref, nt_ref, xs_ref, wg_ref, wu_ref, wd_ref, out_ref, wg_s, wu_s, wd_s):
    i = pl.program_id(0)
    changed = jnp.logical_or(i == 0, te_ref[i] != te_ref[jnp.maximum(i - 1, 0)])

    @pl.when(changed)
    def _():
        wg_s[...] = wg_ref[0, 0].astype(BF16)
        wu_s[...] = wu_ref[0, 0].astype(BF16)
        wd_s[...] = wd_ref[0, 0].astype(BF16)

    @pl.when(i < nt_ref[0])
    def _():
        xs = _load_packed_tiles(xs_ref, EXPERT_TILE, BF16)
        g = jnp.dot(xs, wg_s[...], preferred_element_type=F32)
        u = jnp.dot(xs, wu_s[...], preferred_element_type=F32)
        hmid = (g * jax.nn.sigmoid(g)) * u
        _store_packed_tiles(out_ref, jnp.dot(hmid.astype(BF16), wd_s[...], preferred_element_type=F32))


def _expert_call(tile_expert, n_used, xs, wg, wu, wd, layer):
    n_tiles = xs.shape[0] // (EXPERT_TILE * PACK_ROWS)
    tile_blk = pl.BlockSpec((EXPERT_TILE * PACK_ROWS, LANES), lambda i, te, nt: (jnp.minimum(i, nt[0] - 1), 0))
    w_blk = lambda shape: pl.BlockSpec((1, 1) + shape, lambda i, te, nt: (layer, te[i], 0, 0))
    grid_spec = pltpu.PrefetchScalarGridSpec(
        num_scalar_prefetch=2,
        grid=(n_tiles,),
        in_specs=[
            tile_blk,
            w_blk((D_MODEL, D_EXPERT)), w_blk((D_MODEL, D_EXPERT)), w_blk((D_EXPERT, D_MODEL)),
        ],
        out_specs=tile_blk,
        scratch_shapes=[
            pltpu.VMEM((D_MODEL, D_EXPERT), BF16),
            pltpu.VMEM((D_MODEL, D_EXPERT), BF16),
            pltpu.VMEM((D_EXPERT, D_MODEL), BF16),
        ],
    )
    return pl.pallas_call(
        _expert_kernel,
        grid_spec=grid_spec,
        out_shape=jax.ShapeDtypeStruct(xs.shape, jnp.uint32),
        compiler_params=pltpu.CompilerParams(dimension_semantics=("arbitrary",), vmem_limit_bytes=VMEM_LIMIT),
        name="experts",
    )(tile_expert, n_used, xs, wg, wu, wd)


SC_CHUNK = 128
SC_PAD_CHUNK = 64


def _sc_workers():
    info = plsc.get_sparse_core_info()
    return info.num_cores, info.num_cores * info.num_subcores


def _sc_split(per_w):
    n_full = per_w // SC_CHUNK
    tail = per_w - n_full * SC_CHUNK
    assert tail % 8 == 0
    return n_full, tail


def _sc_dispatch_call(pos_flat, pad_rows, x_tiles, tile_rows, n_sorted_rows):
    t_all = x_tiles.shape[0] // tile_rows
    dt = x_tiles.dtype
    nc, nw = _sc_workers()
    per_w = t_all // nw
    n_ch, tail = _sc_split(per_w)
    n_pad = pad_rows.shape[0]
    pad_per_w = n_pad // nw
    n_pch = pad_per_w // SC_PAD_CHUNK
    assert per_w * nw == t_all and n_pch * SC_PAD_CHUNK * nw == n_pad
    zeros = jnp.zeros((SC_PAD_CHUNK, tile_rows, LANES), dt)
    tail_scratch = [] if tail == 0 else [
        pltpu.VMEM((tail,), jnp.int32), pltpu.VMEM((tail,), jnp.int32), pltpu.VMEM((tail, tile_rows, LANES), dt)]

    @functools.partial(
        pl.kernel, mesh=plsc.VectorSubcoreMesh(core_axis_name="c", subcore_axis_name="s"),
        out_type=jax.ShapeDtypeStruct((n_sorted_rows, tile_rows, LANES), dt),
        scratch_types=[
            pltpu.VMEM((SC_CHUNK,), jnp.int32), pltpu.VMEM((SC_CHUNK,), jnp.int32),
            pltpu.VMEM((SC_CHUNK, tile_rows, LANES), dt),
            pltpu.VMEM((SC_PAD_CHUNK,), jnp.int32),
            pltpu.VMEM((SC_PAD_CHUNK, tile_rows, LANES), dt),
        ] + tail_scratch,
    )
    def k(pos_hbm, pad_hbm, x_hbm, z_hbm, xs_hbm, i0_v, i1_v, rows_v, ip_v, z_v, *tail_refs):
        wid = lax.axis_index("s") * nc + lax.axis_index("c")

        def move(b, n, idx0, idx1, rows):
            pltpu.sync_copy(pos_hbm.at[pl.ds(b, n)], idx0)
            pltpu.sync_copy(pos_hbm.at[pl.ds(t_all + b, n)], idx1)
            pltpu.sync_copy(x_hbm.at[pl.ds(b, n)], rows)
            pltpu.sync_copy(rows, xs_hbm.at[idx0])
            pltpu.sync_copy(rows, xs_hbm.at[idx1])

        @pl.loop(0, n_ch)
        def _(j):
            move(pl.multiple_of(wid * per_w + j * SC_CHUNK, 8), SC_CHUNK, i0_v, i1_v, rows_v)

        if tail:
            move(pl.multiple_of(wid * per_w + n_ch * SC_CHUNK, 8), tail, *tail_refs)

        pltpu.sync_copy(z_hbm, z_v)

        @pl.loop(0, n_pch)
        def _(j):
            b = pl.multiple_of(wid * pad_per_w + j * SC_PAD_CHUNK, 8)
            pltpu.sync_copy(pad_hbm.at[pl.ds(b, SC_PAD_CHUNK)], ip_v)
            pltpu.sync_copy(z_v, xs_hbm.at[ip_v])

    xs = k(pos_flat, pad_rows, x_tiles.reshape(t_all, tile_rows, LANES), zeros)
    return xs.reshape(n_sorted_rows * tile_rows, LANES)


def _sc_gather_call(pos_flat, outs, tile_rows):
    n_assign = pos_flat.shape[0]
    dt = outs.dtype
    nc, nw = _sc_workers()
    per_w = n_assign // nw
    n_ch, tail = _sc_split(per_w)
    assert per_w * nw == n_assign
    tail_scratch = [] if tail == 0 else [pltpu.VMEM((tail,), jnp.int32), pltpu.VMEM((tail, tile_rows, LANES), dt)]

    @functools.partial(
        pl.kernel, mesh=plsc.VectorSubcoreMesh(core_axis_name="c", subcore_axis_name="s"),
        out_type=jax.ShapeDtypeStruct((n_assign, tile_rows, LANES), dt),
        scratch_types=[pltpu.VMEM((SC_CHUNK,), jnp.int32),
                       pltpu.VMEM((SC_CHUNK, tile_rows, LANES), dt)] + tail_scratch,
    )
    def k(pos_hbm, o_hbm, y_hbm, i_v, rows_v, *tail_refs):
        wid = lax.axis_index("s") * nc + lax.axis_index("c")

        def move(b, n, idx, rows):
            pltpu.sync_copy(pos_hbm.at[pl.ds(b, n)], idx)
            pltpu.sync_copy(o_hbm.at[idx], rows)
            pltpu.sync_copy(rows, y_hbm.at[pl.ds(b, n)])

        @pl.loop(0, n_ch)
        def _(j):
            move(pl.multiple_of(wid * per_w + j * SC_CHUNK, 8), SC_CHUNK, i_v, rows_v)

        if tail:
            move(pl.multiple_of(wid * per_w + n_ch * SC_CHUNK, 8), tail, *tail_refs)

    y = k(pos_flat, outs.reshape(-1, tile_rows, LANES))
    return y.reshape(n_assign * tile_rows, LANES)


def _combine_dense_kernel(alpha, n_main_tiles, x1t_ref, y0_ref, y1_ref, w_ref, g_ref, b_ref, out_ref, *tail):
    j = pl.program_id(0)
    w = w_ref[...].T
    y = (w[:, 0:1] * _load_packed_tiles(y0_ref, TOK_TILE, F32)
         + w[:, 1:2] * _load_packed_tiles(y1_ref, TOK_TILE, F32))
    res = _layer_norm(alpha * x1t_ref[...] + y, g_ref[...], b_ref[...])
    if n_main_tiles is None:
        out_ref[...] = res
    else:
        @pl.when(j < n_main_tiles)
        def _():
            out_ref[...] = res

        @pl.when(j >= n_main_tiles)
        def _():
            tail[0][...] = res


def _combine_dense_call(x1t, y_pair, w_col, g, b, alpha, n_main_tiles=None):
    t_all = x1t.shape[0]
    n_tiles = t_all // TOK_TILE
    if n_main_tiles is None:
        out_specs = pl.BlockSpec((TOK_TILE, D_MODEL), lambda i: (i, 0))
        out_shape = jax.ShapeDtypeStruct((t_all, D_MODEL), F32)
    else:
        assert n_tiles == n_main_tiles + 1
        out_specs = [pl.BlockSpec((TOK_TILE, D_MODEL), lambda i: (jnp.minimum(i, n_main_tiles - 1), 0)),
                     pl.BlockSpec((TOK_TILE, D_MODEL), lambda i: (0, 0))]
        out_shape = [jax.ShapeDtypeStruct((n_main_tiles * TOK_TILE, D_MODEL), F32),
                     jax.ShapeDtypeStruct((TOK_TILE, D_MODEL), F32)]
    tile = (TOK_TILE, D_MODEL)
    packed = (TOK_TILE * PACK_ROWS, LANES)
    return pl.pallas_call(
        functools.partial(_combine_dense_kernel, alpha, n_main_tiles),
        grid=(n_tiles,),
        in_specs=[
            pl.BlockSpec(tile, lambda i: (i, 0)),
            pl.BlockSpec(packed, lambda i: (i, 0)),
            pl.BlockSpec(packed, lambda i: (i + n_tiles, 0)),
            pl.BlockSpec((8, TOK_TILE), lambda i: (0, i)),
            pl.BlockSpec((1, D_MODEL), lambda i: (0, 0)),
            pl.BlockSpec((1, D_MODEL), lambda i: (0, 0)),
        ],
        out_specs=out_specs,
        out_shape=out_shape,
        compiler_params=pltpu.CompilerParams(dimension_semantics=("arbitrary",), vmem_limit_bytes=VMEM_LIMIT),
        name="combine",
    )(x1t, y_pair, y_pair, w_col, g, b)


def _rope_tables(seq, dec_seq, past_len):
    half = HEAD_DIM // 2
    inv = ROPE_THETA ** (-jnp.arange(half, dtype=F32) / half)
    pos = jnp.concatenate([jnp.arange(seq), past_len + (jnp.arange(TOK_TILE) % dec_seq)])
    ang = pos.astype(F32)[:, None] * inv[None, :]
    cos = jnp.cos(ang)
    sin = jnp.sin(ang)
    zero = jnp.zeros_like(sin)
    reps = LANES // HEAD_DIM
    cos_t = jnp.tile(jnp.concatenate([cos, cos], -1), (1, reps))
    sa_t = jnp.tile(jnp.concatenate([-sin, zero], -1), (1, reps))
    sb_t = jnp.tile(jnp.concatenate([zero, sin], -1), (1, reps))
    return cos_t, sa_t, sb_t


def _inclusive_cumsum(v):
    n = v.shape[0]
    tri = jnp.arange(n)[None, :] <= jnp.arange(n)[:, None]
    return jnp.sum(jnp.where(tri, v[None, :], 0), axis=1)


def _dispatch_plan(ridx, counts, n_tiles):
    t_all = ridx.shape[1]
    tiles_e = (counts + EXPERT_TILE - 1) // EXPERT_TILE
    tile_end = _inclusive_cumsum(tiles_e)
    offs = (tile_end - tiles_e) * EXPERT_TILE
    n_used = tile_end[-1:]
    tile_expert = jnp.minimum(
        jnp.sum((jnp.arange(n_tiles)[:, None] >= tile_end[None, :]).astype(jnp.int32), axis=1), N_EXPERTS - 1)
    experts = jnp.arange(N_EXPERTS)
    pos = ridx[TOP_K:2 * TOP_K] + jnp.sum(
        jnp.where(ridx[:TOP_K, :, None] == experts[None, None, :], offs[None, None, :], 0), axis=-1)
    gap_start = jnp.concatenate([offs + counts, n_used * EXPERT_TILE])
    gap_len = jnp.concatenate([tiles_e * EXPERT_TILE - counts, (n_tiles - n_used) * EXPERT_TILE])
    gap_end_q = _inclusive_cumsum(gap_len)
    q = jnp.arange(n_tiles * EXPERT_TILE - TOP_K * t_all)
    gap = jnp.sum((q[:, None] >= gap_end_q[None, :]).astype(jnp.int32), axis=1)
    in_gap = gap[:, None] == jnp.arange(N_EXPERTS + 1)[None, :]
    pad_rows = q + jnp.sum(jnp.where(in_gap, (gap_start - gap_end_q + gap_len)[None, :], 0), axis=1)
    return (pos.reshape(-1).astype(jnp.int32), pad_rows.astype(jnp.int32),
            tile_expert.astype(jnp.int32), n_used.astype(jnp.int32))


def kernel(x_prompt, x_sample, cache_k, cache_v, w_in, sinks, gm_ln_g, gm_ln_b, gm_ws, gm_bs,
           out_norm_a, out_norm_b, w_o, ln1_g, ln1_b, w_router, router_bias,
           w_gate, w_up, w_down, ln2_g, ln2_b):
    batch, seq, _ = x_prompt.shape
    dec_batch, dec_seq, _ = x_sample.shape
    depth = w_in.shape[0]
    past_len = PAST_LEN
    assert dec_batch * dec_seq == TOK_TILE and dec_seq == CHUNK and seq % TOK_TILE == 0
    assert cache_k.shape[2] == WINDOW
    alpha = (2 * depth) ** 0.25
    tiles_per_seq = seq // TOK_TILE

    tables = _rope_tables(seq, dec_seq, past_len)
    wrt = w_router.T
    rb = router_bias.reshape(N_EXPERTS, 1)
    cache_k2 = cache_k.reshape(depth * dec_batch, WINDOW, KV_WIDTH)
    cache_v2 = cache_v.reshape(depth * dec_batch, WINDOW, KV_WIDTH)

    streams = [
        dict(x=(x_prompt.reshape(batch * seq, D_MODEL), x_sample.reshape(-1, D_MODEL)),
             n_seq=batch, has_sample=True),
    ]
    kp, vp, ks, vs, gms = [], [], [], [], []
    for l in range(depth):
        layer_args = (
            sinks[l], w_in,
            cache_k2, cache_v2,
            gm_ln_g[l].reshape(1, GM_WIDTH), gm_ln_b[l].reshape(1, GM_WIDTH),
            gm_ws[l], gm_bs[l].T,
            out_norm_a[l].reshape(1, ATT_WIDTH), out_norm_b[l].reshape(1, GM_WIDTH),
            w_o,
            ln1_g[l].reshape(1, D_MODEL), ln1_b[l].reshape(1, D_MODEL), wrt, rb)
        k_tails, v_tails = [], []
        for st in streams:
            n_prompt = st["n_seq"] * seq
            n_prompt_tiles = n_prompt // TOK_TILE
            t_all = n_prompt + (TOK_TILE if st["has_sample"] else 0)
            n_exp_tiles = (TOP_K * t_all) // EXPERT_TILE + N_EXPERTS
            res = _mixer_call(layer_args, st["x"], tables, n_prompt_tiles, tiles_per_seq, alpha, l)
            if st["has_sample"]:
                x1t, x1p, k_tail, v_tail, k_s, v_s, vn_s, ridx, rw, counts = res
            else:
                x1t, x1p, k_tail, v_tail, ridx, rw, counts = res

            pos, pad_rows, tile_expert, n_used = _dispatch_plan(ridx, counts[:, 0], n_exp_tiles)
            xs = _sc_dispatch_call(pos, pad_rows, x1p, PACK_ROWS, n_exp_tiles * EXPERT_TILE)
            outs = _expert_call(tile_expert, n_used, xs, w_gate, w_up, w_down, l)
            y_pair = _sc_gather_call(pos, outs, PACK_ROWS)
            combine_args = (x1t, y_pair, rw, ln2_g[l].reshape(1, D_MODEL), ln2_b[l].reshape(1, D_MODEL))
            if l < depth - 1:
                st["x"] = combine_args
            else:
                st["x"] = _combine_dense_call(*combine_args, alpha,
                                              n_main_tiles=n_prompt_tiles if st["has_sample"] else None)

            k_tails.append(k_tail)
            v_tails.append(v_tail)
            if st["has_sample"]:
                ks.append(k_s)
                vs.append(v_s)
                gms.append(vn_s)
        kp.append(jnp.concatenate(k_tails, axis=0))
        vp.append(jnp.concatenate(v_tails, axis=0))

    y_prompt = jnp.concatenate([st["x"][0] if st["has_sample"] else st["x"] for st in streams],
                               axis=0).reshape(batch, seq, D_MODEL)
    y_sample = streams[-1]["x"][1].reshape(dec_batch, dec_seq, D_MODEL)
    heads = (N_KV_HEADS, HEAD_DIM)
    return (y_prompt, y_sample,
            jnp.stack(kp).reshape(depth, batch, WINDOW, *heads), jnp.stack(vp).reshape(depth, batch, WINDOW, *heads),
            jnp.stack(ks).reshape(depth, dec_batch, dec_seq, *heads),
            jnp.stack(vs).reshape(depth, dec_batch, dec_seq, *heads),
            jnp.stack(gms).reshape(depth, dec_batch, dec_seq, GM_WIDTH))
```

```python
import functools

import jax
import jax.numpy as jnp
from jax import lax
from jax.experimental import pallas as pl
from jax.experimental.pallas import tpu as pltpu
from jax.experimental.pallas import tpu_sc as plsc

D_MODEL = 1024
CHUNK = 64
WINDOW = 128
HEAD_DIM = 64
N_Q_HEADS = 8
N_KV_HEADS = 2
Q_PER_KV = N_Q_HEADS // N_KV_HEADS
PAST_LEN = 2048
ATT_WIDTH = N_Q_HEADS * HEAD_DIM
KV_WIDTH = N_KV_HEADS * HEAD_DIM
ROPE_THETA = 10000.0
GM_GROUPS = 4
GM_CH = 128
GM_WIDTH = GM_GROUPS * GM_CH
GM_CHUNK = 128
D_MIX = ATT_WIDTH + GM_WIDTH
D_IN = ATT_WIDTH + 2 * KV_WIDTH + 2 * GM_WIDTH
N_EXPERTS = 16
N_EXPERT_GROUPS = 4
EXPERTS_PER_GROUP = N_EXPERTS // N_EXPERT_GROUPS
TOP_K = 2
D_EXPERT = 512
LN_EPS = 1e-5
NEG_INF = -1e30

LANES = 128
PACK_ROWS = D_MODEL // (2 * LANES)
HI_HALF = 0xFFFF0000
TOK_TILE = 512
EXPERT_TILE = 512
KEYS_PER_CHUNK = WINDOW + CHUNK
VMEM_LIMIT = 60 * 1024 * 1024

F32 = jnp.float32
BF16 = jnp.bfloat16


def _layer_norm(x, g, b):
    mu = jnp.mean(x, axis=-1, keepdims=True)
    d = x - mu
    var = jnp.mean(d * d, axis=-1, keepdims=True)
    return d * lax.rsqrt(var + LN_EPS) * g + b


def _rms_norm(x, g):
    return x * lax.rsqrt(jnp.mean(x * x, axis=-1, keepdims=True) + LN_EPS) * g


def _gelu(x):
    return 0.5 * x * (1.0 + lax.erf(x * (0.5 ** 0.5)))


def _bf16_bits(x):
    return lax.bitcast_convert_type(x.astype(BF16).astype(F32), jnp.uint32)


def _store_packed_tiles(ref, x):
    n = x.shape[0]
    for s in range(PACK_ROWS):
        lo = _bf16_bits(x[:, (2 * s) * LANES:(2 * s + 1) * LANES]) >> 16
        hi = _bf16_bits(x[:, (2 * s + 1) * LANES:(2 * s + 2) * LANES]) & jnp.uint32(HI_HALF)
        ref[pl.ds(s, n, stride=PACK_ROWS), :] = hi | lo


def _load_packed_tiles(ref, n, dtype):
    cols = []
    for s in range(PACK_ROWS):
        w = ref[pl.ds(s, n, stride=PACK_ROWS), :]
        cols.append(lax.bitcast_convert_type(w << 16, F32).astype(dtype))
        cols.append(lax.bitcast_convert_type(w & jnp.uint32(HI_HALF), F32).astype(dtype))
    return jnp.concatenate(cols, axis=-1)


def _first_index_of_max(rows):
    m = rows[0]
    for r in rows[1:]:
        m = jnp.maximum(m, r)
    idx = jnp.full(m.shape, len(rows), jnp.int32)
    for e in reversed(range(len(rows))):
        idx = jnp.where(rows[e] == m, e, idx)
    return m, idx


def _mixer_kernel(n_prompt_tiles, tiles_per_seq, alpha, has_sample, first_layer, sinks_ref, *refs):
    n_x = 2 if first_layer else 6
    x_refs, refs = refs[:n_x], refs[n_x:]
    (w_in_ref, cos_ref, sa_ref, sb_ref, ck_ref, cv_ref, lng_ref, lnb_ref, ws_ref, bst_ref, na_ref, nb_ref, w_o_ref,
     l1g_ref, l1b_ref, wrt_ref, rb_ref, tri_ref, x1t_ref, x1p_ref, kt_ref, vt_ref) = refs[:22]
    rest = refs[22:]
    if has_sample:
        (ks_ref, vs_ref, vn_ref, ridx_ref, rw_ref, counts_ref,
         kd_ref, vd_ref, qm_ref, att_ref, gm_ref, cnt_ref, w_in_s, w_o_s, x_s) = rest
    else:
        (ridx_ref, rw_ref, counts_ref,
         kd_ref, vd_ref, qm_ref, att_ref, gm_ref, cnt_ref, w_in_s, w_o_s, x_s) = rest
    i = pl.program_id(0)
    is_sample = i >= n_prompt_tiles
    seq_start = jnp.logical_and(jnp.logical_not(is_sample), (i % tiles_per_seq) == 0)
    seq_end = jnp.logical_and(jnp.logical_not(is_sample), (i % tiles_per_seq) == tiles_per_seq - 1)
    tm = TOK_TILE

    @pl.when(i == 0)
    def _():
        w_in_s[...] = w_in_ref[0].astype(BF16)
        w_o_s[...] = w_o_ref[0].astype(BF16)

    if first_layer:
        xp_ref, xs_ref = x_refs
        x_s[...] = jnp.where(is_sample, xs_ref[...], xp_ref[...])
    else:
        x1_prev_ref, y0_ref, y1_ref, wc_ref, l2g_ref, l2b_ref = x_refs
        wc = wc_ref[...].T
        y_moe = (wc[:, 0:1] * _load_packed_tiles(y0_ref, tm, F32)
                 + wc[:, 1:2] * _load_packed_tiles(y1_ref, tm, F32))
        x_s[...] = _layer_norm(alpha * x1_prev_ref[...] + y_moe, l2g_ref[...], l2b_ref[...])

    h = jnp.dot(x_s[...].astype(BF16), w_in_s[...], preferred_element_type=F32)

    cos = cos_ref[...]
    sa = sa_ref[...]
    sb = sb_ref[...]

    def rope(blk):
        return blk * cos + pltpu.roll(blk, LANES - HEAD_DIM // 2, 1) * sa + pltpu.roll(blk, HEAD_DIM // 2, 1) * sb

    lane = lax.broadcasted_iota(jnp.int32, (1, LANES), 1)
    lo_half = lane < HEAD_DIM

    o_k = ATT_WIDTH
    o_v = o_k + KV_WIDTH
    o_u = o_v + KV_WIDTH
    o_g = o_u + GM_WIDTH

    k_rot = rope(h[:, o_k:o_v])
    v_new = h[:, o_v:o_u]

    @pl.when(seq_end)
    def _():
        kt_ref[...] = k_rot[tm - WINDOW:]
        vt_ref[...] = v_new[tm - WINDOW:]

    if has_sample:
        @pl.when(is_sample)
        def _():
            ks_ref[...] = k_rot
            vs_ref[...] = v_new

    def dup_heads(a):
        sw = pltpu.roll(a, HEAD_DIM, 1)
        return jnp.where(lo_half, a, sw).astype(BF16), jnp.where(lo_half, sw, a).astype(BF16)

    k_d = dup_heads(k_rot)
    v_d = dup_heads(v_new)

    @pl.when(jnp.logical_not(is_sample))
    def _():
        @pl.when(seq_start)
        def _():
            for g in range(N_KV_HEADS):
                kd_ref[g, 0:WINDOW, :] = jnp.zeros((WINDOW, LANES), BF16)
                vd_ref[g, 0:WINDOW, :] = jnp.zeros((WINDOW, LANES), BF16)

        @pl.when(jnp.logical_not(seq_start))
        def _():
            for g in range(N_KV_HEADS):
                kd_ref[g, 0:WINDOW, :] = kd_ref[g, tm:tm + WINDOW, :]
                vd_ref[g, 0:WINDOW, :] = vd_ref[g, tm:tm + WINDOW, :]

        for g in range(N_KV_HEADS):
            kd_ref[g, WINDOW:WINDOW + tm, :] = k_d[g]
            vd_ref[g, WINDOW:WINDOW + tm, :] = v_d[g]

    @pl.when(is_sample)
    def _():
        for b in range(tm // CHUNK):
            ck = dup_heads(ck_ref[b])
            cv = dup_heads(cv_ref[b])
            base = b * KEYS_PER_CHUNK
            for g in range(N_KV_HEADS):
                kd_ref[g, base:base + WINDOW, :] = ck[g]
                vd_ref[g, base:base + WINDOW, :] = cv[g]
                kd_ref[g, base + WINDOW:base + KEYS_PER_CHUNK, :] = k_d[g][b * CHUNK:(b + 1) * CHUNK]
                vd_ref[g, base + WINDOW:base + KEYS_PER_CHUNK, :] = v_d[g][b * CHUNK:(b + 1) * CHUNK]

    scale = HEAD_DIM ** -0.5
    for b in range(ATT_WIDTH // LANES):
        qb = (rope(h[:, b * LANES:(b + 1) * LANES]) * scale).astype(BF16)
        zero = jnp.zeros_like(qb)
        qm_ref[2 * b] = jnp.where(lo_half, qb, zero)
        qm_ref[2 * b + 1] = jnp.where(lo_half, zero, qb)

    key_stride = jnp.where(is_sample, KEYS_PER_CHUNK, CHUNK)
    col = lax.broadcasted_iota(jnp.int32, (1, KEYS_PER_CHUNK), 1)
    row_blk = lax.broadcasted_iota(jnp.int32, (Q_PER_KV * CHUNK, 1), 0) // CHUNK
    sink_cols = [
        jnp.where(row_blk == 0, sinks_ref[Q_PER_KV * g],
                  jnp.where(row_blk == 1, sinks_ref[Q_PER_KV * g + 1],
                            jnp.where(row_blk == 2, sinks_ref[Q_PER_KV * g + 2], sinks_ref[Q_PER_KV * g + 3])))
        for g in range(N_KV_HEADS)]

    for c in range(tm // CHUNK):
        r0 = c * CHUNK
        k0 = pl.multiple_of(c * key_stride, CHUNK)
        outs = []
        for g in range(N_KV_HEADS):
            q_st = jnp.concatenate([qm_ref[Q_PER_KV * g + r, pl.ds(r0, CHUNK), :] for r in range(Q_PER_KV)],
                                   axis=0)
            keys = kd_ref[g, pl.ds(k0, KEYS_PER_CHUNK), :]
            vals = vd_ref[g, pl.ds(k0, KEYS_PER_CHUNK), :]
            s = lax.dot_general(q_st, keys, (((1,), (1,)), ((), ())), preferred_element_type=F32)
            if c < WINDOW // CHUNK:
                first_valid = jnp.where(seq_start, (WINDOW // CHUNK - c) * CHUNK, 0)
                s = jnp.where(col >= first_valid, s, NEG_INF)
            sink = sink_cols[g]
            m = jnp.maximum(jnp.max(s, axis=-1, keepdims=True), sink)
            p = jnp.exp(s - m)
            denom = jnp.sum(p, axis=-1, keepdims=True) + jnp.exp(sink - m)
            o = jnp.dot(p.astype(BF16), vals, preferred_element_type=F32) * (1.0 / denom)
            for bb in range(2):
                outs.append(jnp.where(lo_half, o[(2 * bb) * CHUNK:(2 * bb + 1) * CHUNK],
                                      o[(2 * bb + 1) * CHUNK:(2 * bb + 2) * CHUNK]))
        for b in range(ATT_WIDTH // LANES):
            att_ref[pl.ds(r0, CHUNK), b * LANES:(b + 1) * LANES] = outs[b]

    prow = lax.broadcasted_iota(jnp.int32, (GM_CHUNK, GM_CHUNK), 0)
    pcol = lax.broadcasted_iota(jnp.int32, (GM_CHUNK, GM_CHUNK), 1)
    half = GM_CHUNK // 2
    tril = pcol <= prow
    same_blk = (prow < half) == (pcol < half)
    prow1 = lax.broadcasted_iota(jnp.int32, (GM_CHUNK, 1), 0)
    bst = bst_ref[...]
    bst_s = jnp.where(prow1 < half, bst, pltpu.roll(bst, half, 0))
    bias = jnp.where(is_sample, bst_s, bst)
    for g in range(GM_GROUPS):
        wg_ = ws_ref[g]
        w_s = jnp.where(prow < half, wg_, pltpu.roll(pltpu.roll(wg_, half, 0), half, 1))
        w_eff = jnp.where(is_sample, jnp.where(same_blk, w_s, 0.0), wg_)
        w_eff = jnp.where(tril, w_eff, 0.0).astype(BF16)
        cols = slice(g * GM_CH, (g + 1) * GM_CH)
        ug = _gelu(h[:, o_u + g * GM_CH:o_u + (g + 1) * GM_CH])
        vn = _layer_norm(_gelu(h[:, o_g + g * GM_CH:o_g + (g + 1) * GM_CH]), lng_ref[:, cols], lnb_ref[:, cols])

        if has_sample:
            @pl.when(is_sample)
            def _(cols=cols, vn=vn):
                vn_ref[:, cols] = vn

        vn_b = vn.astype(BF16)
        for n in range(tm // GM_CHUNK):
            rows = slice(n * GM_CHUNK, (n + 1) * GM_CHUNK)
            s = jnp.dot(w_eff, vn_b[rows], preferred_element_type=F32) + bias[:, g:g + 1]
            gm_ref[rows, cols] = ug[rows] * s

    mixed = jnp.concatenate([_rms_norm(att_ref[...], na_ref[...]), _rms_norm(gm_ref[...], nb_ref[...])], axis=-1)
    y = alpha * x_s[...] + jnp.dot(mixed.astype(BF16), w_o_s[...], preferred_element_type=F32)
    x1 = _layer_norm(y, l1g_ref[...], l1b_ref[...])
    x1t_ref[...] = x1
    _store_packed_tiles(x1p_ref, x1)

    nt = (((1,), (1,)), ((), ()))
    wr = wrt_ref[...]
    wr_hi = wr.astype(BF16)
    wr_lo = (wr - wr_hi.astype(F32)).astype(BF16)
    x1_hi = x1.astype(BF16)
    x1_lo = (x1 - x1_hi.astype(F32)).astype(BF16)
    logits = (lax.dot_general(wr_hi, x1_hi, nt, preferred_element_type=F32)
              + (lax.dot_general(wr_hi, x1_lo, nt, preferred_element_type=F32)
                 + lax.dot_general(wr_lo, x1_hi, nt, preferred_element_type=F32)))
    scores = jax.nn.sigmoid(logits)
    sel = scores + rb_ref[...]
    sel_rows = [sel[e:e + 1, :] for e in range(N_EXPERTS)]
    sc_rows = [scores[e:e + 1, :] for e in range(N_EXPERTS)]
    grp = []
    for g in range(N_EXPERT_GROUPS):
        r = sel_rows[g * EXPERTS_PER_GROUP:(g + 1) * EXPERTS_PER_GROUP]
        best_pair = None
        for a in range(EXPERTS_PER_GROUP):
            for b in range(a + 1, EXPERTS_PER_GROUP):
                pair = r[a] + r[b]
                best_pair = pair if best_pair is None else jnp.maximum(best_pair, pair)
        grp.append(best_pair)
    _, best = _first_index_of_max(grp)
    masked = [jnp.where(best == (e // EXPERTS_PER_GROUP), sel_rows[e], NEG_INF) for e in range(N_EXPERTS)]
    _, e0 = _first_index_of_max(masked)
    masked2 = [jnp.where(e0 == e, -jnp.inf, masked[e]) for e in range(N_EXPERTS)]
    _, e1 = _first_index_of_max(masked2)
    w0 = jnp.zeros_like(sc_rows[0])
    w1 = jnp.zeros_like(sc_rows[0])
    for e in range(N_EXPERTS):
        w0 = jnp.where(e0 == e, sc_rows[e], w0)
        w1 = jnp.where(e1 == e, sc_rows[e], w1)
    wsum = w0 + w1
    rw_ref[...] = jnp.concatenate([w0 / wsum, w1 / wsum, jnp.zeros((6, tm), F32)], axis=0)

    @pl.when(i == 0)
    def _():
        cnt_ref[...] = jnp.zeros_like(cnt_ref)

    eid = lax.broadcasted_iota(jnp.int32, (N_EXPERTS, tm), 0)
    oh0 = (eid == e0).astype(F32)
    oh1 = (eid == e1).astype(F32)
    oh = oh0 + oh1
    incl = jnp.dot(oh.astype(BF16), tri_ref[...], preferred_element_type=F32)
    before = cnt_ref[:, 0:1] + incl - oh
    r0 = jnp.sum(oh0 * before, axis=0, keepdims=True).astype(jnp.int32)
    r1 = jnp.sum(oh1 * before, axis=0, keepdims=True).astype(jnp.int32)
    cnt = cnt_ref[...] + incl[:, tm - 1:tm]
    cnt_ref[...] = cnt
    counts_ref[...] = cnt.astype(jnp.int32)
    ridx_ref[...] = jnp.concatenate([e0, e1, r0, r1, jnp.zeros((4, tm), jnp.int32)], axis=0)


def _mixer_call(layer_args, x_all, tables, n_prompt_tiles, tiles_per_seq, alpha, layer):
    (sinks, w_in, ck, cv, lng, lnb, ws, bst, na, nb, w_o, l1g, l1b, wrt, rb) = layer_args
    cos_t, sa_t, sb_t = tables
    first_layer = len(x_all) == 2
    if first_layer:
        x_main, x_tail = x_all
        t_all = x_main.shape[0] + x_tail.shape[0]
        n_tiles = t_all // TOK_TILE
        x_args = [x_main, x_tail]
        x_specs = [pl.BlockSpec((TOK_TILE, D_MODEL), lambda i: (jnp.minimum(i, n_prompt_tiles - 1), 0)),
                   pl.BlockSpec((TOK_TILE, D_MODEL), lambda i: (0, 0))]
    else:
        x1t_prev, y_pair, w_col, l2g, l2b = x_all
        t_all = x1t_prev.shape[0]
        n_tiles = t_all // TOK_TILE
        x_args = [x1t_prev, y_pair, y_pair, w_col, l2g, l2b]
        x_specs = [pl.BlockSpec((TOK_TILE, D_MODEL), lambda i: (i, 0)),
                   pl.BlockSpec((TOK_TILE * PACK_ROWS, LANES), lambda i: (i, 0)),
                   pl.BlockSpec((TOK_TILE * PACK_ROWS, LANES), lambda i: (i + n_tiles, 0)),
                   pl.BlockSpec((8, TOK_TILE), lambda i: (0, i)),
                   pl.BlockSpec((1, D_MODEL), lambda i: (0, 0)), pl.BlockSpec((1, D_MODEL), lambda i: (0, 0))]
    n_seq_tiles = tiles_per_seq
    has_sample = n_tiles > n_prompt_tiles

    def const(shape):
        nd = len(shape)
        return pl.BlockSpec(shape, lambda i, _nd=nd: (0,) * _nd)

    def tab_map(i):
        return (jnp.where(i < n_prompt_tiles, i % n_seq_tiles, n_seq_tiles), 0)

    n_seq = n_prompt_tiles // n_seq_tiles

    def seq_map(i):
        return (jnp.minimum(i // n_seq_tiles, n_seq - 1), 0)

    row_blk = lambda w: pl.BlockSpec((TOK_TILE, w), lambda i: (i, 0))
    in_specs = [
        pl.BlockSpec(memory_space=pltpu.SMEM),
        *x_specs,
        pl.BlockSpec((1, D_MODEL, D_IN), lambda i: (layer, 0, 0)),
        pl.BlockSpec((TOK_TILE, LANES), tab_map),
        pl.BlockSpec((TOK_TILE, LANES), tab_map),
        pl.BlockSpec((TOK_TILE, LANES), tab_map),
        pl.BlockSpec((TOK_TILE // CHUNK, WINDOW, KV_WIDTH), lambda i: (layer, 0, 0)),
        pl.BlockSpec((TOK_TILE // CHUNK, WINDOW, KV_WIDTH), lambda i: (layer, 0, 0)),
        const((1, GM_WIDTH)), const((1, GM_WIDTH)),
        const((GM_GROUPS, GM_CHUNK, GM_CHUNK)), const((GM_CHUNK, GM_GROUPS)),
        const((1, ATT_WIDTH)), const((1, GM_WIDTH)),
        pl.BlockSpec((1, D_MIX, D_MODEL), lambda i: (layer, 0, 0)),
        const((1, D_MODEL)), const((1, D_MODEL)),
        const((N_EXPERTS, D_MODEL)), const((N_EXPERTS, 1)),
        const((TOK_TILE, TOK_TILE)),
    ]
    out_shape = [
        jax.ShapeDtypeStruct((t_all, D_MODEL), F32),
        jax.ShapeDtypeStruct((t_all * PACK_ROWS, LANES), jnp.uint32),
        jax.ShapeDtypeStruct((n_seq * WINDOW, KV_WIDTH), F32),
        jax.ShapeDtypeStruct((n_seq * WINDOW, KV_WIDTH), F32),
        jax.ShapeDtypeStruct((TOK_TILE, KV_WIDTH), F32),
        jax.ShapeDtypeStruct((TOK_TILE, KV_WIDTH), F32),
        jax.ShapeDtypeStruct((TOK_TILE, GM_WIDTH), F32),
        jax.ShapeDtypeStruct((8, t_all), jnp.int32),
        jax.ShapeDtypeStruct((8, t_all), F32),
        jax.ShapeDtypeStruct((N_EXPERTS, LANES), jnp.int32),
    ]
    out_specs = [
        pl.BlockSpec((TOK_TILE, D_MODEL), lambda i: (i, 0)),
        pl.BlockSpec((TOK_TILE * PACK_ROWS, LANES), lambda i: (i, 0)),
        pl.BlockSpec((WINDOW, KV_WIDTH), seq_map), pl.BlockSpec((WINDOW, KV_WIDTH), seq_map),
        const((TOK_TILE, KV_WIDTH)), const((TOK_TILE, KV_WIDTH)),
        const((TOK_TILE, GM_WIDTH)),
        pl.BlockSpec((8, TOK_TILE), lambda i: (0, i)),
        pl.BlockSpec((8, TOK_TILE), lambda i: (0, i)),
        const((N_EXPERTS, LANES)),
    ]
    if not has_sample:
        del out_shape[4:7], out_specs[4:7]
    kd_rows = (TOK_TILE // CHUNK) * KEYS_PER_CHUNK
    scratch = [
        pltpu.VMEM((N_KV_HEADS, kd_rows, LANES), BF16),
        pltpu.VMEM((N_KV_HEADS, kd_rows, LANES), BF16),
        pltpu.VMEM((N_Q_HEADS, TOK_TILE, LANES), BF16),
        pltpu.VMEM((TOK_TILE, ATT_WIDTH), F32),
        pltpu.VMEM((TOK_TILE, GM_WIDTH), F32),
        pltpu.VMEM((N_EXPERTS, LANES), F32),
        pltpu.VMEM((D_MODEL, D_IN), BF16),
        pltpu.VMEM((D_MIX, D_MODEL), BF16),
        pltpu.VMEM((TOK_TILE, D_MODEL), F32),
    ]
    return pl.pallas_call(
        functools.partial(_mixer_kernel, n_prompt_tiles, tiles_per_seq, alpha, has_sample, first_layer),
        grid=(n_tiles,),
        in_specs=in_specs, out_specs=out_specs, out_shape=out_shape,
        scratch_shapes=scratch,
        compiler_params=pltpu.CompilerParams(dimension_semantics=("arbitrary",), vmem_limit_bytes=VMEM_LIMIT),
        name="mixer",
    )(sinks, *x_args, w_in, cos_t, sa_t, sb_t, ck, cv, lng, lnb, ws, bst, na, nb, w_o, l1g, l1b, wrt, rb,
      jnp.triu(jnp.ones((TOK_TILE, TOK_TILE), BF16)))


def _expert_kernel(layer, te_ref, nt_ref, plan_ref, xs_ref, wg_hbm, wu_hbm, wd_hbm, out_ref,
                   wg_s, wu_s, wd_s, wg_f, wu_f, wd_f, sems):
    i = pl.program_id(0)
    plan = plan_ref[i]
    first = (plan & 1) == 1
    slot = (plan >> 1) & 1
    nxt = plan >> 2

    def weight_copies(e, s):
        return (pltpu.make_async_copy(wg_hbm.at[layer, e], wg_f.at[s], sems.at[s, 0]),
                pltpu.make_async_copy(wu_hbm.at[layer, e], wu_f.at[s], sems.at[s, 1]),
                pltpu.make_async_copy(wd_hbm.at[layer, e], wd_f.at[s], sems.at[s, 2]))

    @pl.when(i == 0)
    def _():
        for c in weight_copies(te_ref[0], slot):
            c.start()

    @pl.when(first)
    def _():
        for c in weight_copies(te_ref[i], slot):
            c.wait()

        @pl.when(nxt < N_EXPERTS)
        def _():
            for c in weight_copies(nxt, 1 - slot):
                c.start()

        wg_s[...] = wg_f[slot].astype(BF16)
        wu_s[...] = wu_f[slot].astype(BF16)
        wd_s[...] = wd_f[slot].astype(BF16)

    @pl.when(i < nt_ref[0])
    def _():
        xs = _load_packed_tiles(xs_ref, EXPERT_TILE, BF16)
        g = jnp.dot(xs, wg_s[...], preferred_element_type=F32)
        u = jnp.dot(xs, wu_s[...], preferred_element_type=F32)
        hmid = (g * jax.nn.sigmoid(g)) * u
        _store_packed_tiles(out_ref, jnp.dot(hmid.astype(BF16), wd_s[...], preferred_element_type=F32))


def _expert_call(tile_expert, n_used, tile_plan, xs, wg, wu, wd, layer):
    n_tiles = xs.shape[0] // (EXPERT_TILE * PACK_ROWS)
    tile_blk = pl.BlockSpec((EXPERT_TILE * PACK_ROWS, LANES),
                            lambda i, te, nt, plan: (jnp.minimum(i, nt[0] - 1), 0))
    grid_spec = pltpu.PrefetchScalarGridSpec(
        num_scalar_prefetch=3,
        grid=(n_tiles,),
        in_specs=[tile_blk] + [pl.BlockSpec(memory_space=pl.ANY)] * 3,
        out_specs=tile_blk,
        scratch_shapes=[
            pltpu.VMEM((D_MODEL, D_EXPERT), BF16),
            pltpu.VMEM((D_MODEL, D_EXPERT), BF16),
            pltpu.VMEM((D_EXPERT, D_MODEL), BF16),
            pltpu.VMEM((2, D_MODEL, D_EXPERT), F32),
            pltpu.VMEM((2, D_MODEL, D_EXPERT), F32),
            pltpu.VMEM((2, D_EXPERT, D_MODEL), F32),
            pltpu.SemaphoreType.DMA((2, 3)),
        ],
    )
    return pl.pallas_call(
        functools.partial(_expert_kernel, layer),
        grid_spec=grid_spec,
        out_shape=jax.ShapeDtypeStruct(xs.shape, jnp.uint32),
        compiler_params=pltpu.CompilerParams(dimension_semantics=("arbitrary",), vmem_limit_bytes=VMEM_LIMIT),
        name="experts",
    )(tile_expert, n_used, tile_plan, xs, wg, wu, wd)


SC_CHUNK = 128
SC_PAD_CHUNK = 64


def _sc_workers():
    info = plsc.get_sparse_core_info()
    return info.num_cores, info.num_cores * info.num_subcores


def _sc_split(per_w):
    n_full = per_w // SC_CHUNK
    tail = per_w - n_full * SC_CHUNK
    assert tail % 8 == 0
    return n_full, tail


def _sc_dispatch_call(pos_flat, pad_rows, x_tiles, tile_rows, n_sorted_rows):
    t_all = x_tiles.shape[0] // tile_rows
    dt = x_tiles.dtype
    nc, nw = _sc_workers()
    per_w = t_all // nw
    n_ch, tail = _sc_split(per_w)
    n_pad = pad_rows.shape[0]
    pad_per_w = n_pad // nw
    n_pch = pad_per_w // SC_PAD_CHUNK
    assert per_w * nw == t_all and n_pch * SC_PAD_CHUNK * nw == n_pad
    zeros = jnp.zeros((SC_PAD_CHUNK, tile_rows, LANES), dt)
    tail_scratch = [] if tail == 0 else [
        pltpu.VMEM((tail,), jnp.int32), pltpu.VMEM((tail,), jnp.int32), pltpu.VMEM((tail, tile_rows, LANES), dt)]

    @functools.partial(
        pl.kernel, mesh=plsc.VectorSubcoreMesh(core_axis_name="c", subcore_axis_name="s"),
        out_type=jax.ShapeDtypeStruct((n_sorted_rows, tile_rows, LANES), dt),
        scratch_types=[
            pltpu.VMEM((SC_CHUNK,), jnp.int32), pltpu.VMEM((SC_CHUNK,), jnp.int32),
            pltpu.VMEM((SC_CHUNK, tile_rows, LANES), dt),
            pltpu.VMEM((SC_PAD_CHUNK,), jnp.int32),
            pltpu.VMEM((SC_PAD_CHUNK, tile_rows, LANES), dt),
        ] + tail_scratch,
    )
    def k(pos_hbm, pad_hbm, x_hbm, z_hbm, xs_hbm, i0_v, i1_v, rows_v, ip_v, z_v, *tail_refs):
        wid = lax.axis_index("s") * nc + lax.axis_index("c")

        def move(b, n, idx0, idx1, rows):
            pltpu.sync_copy(pos_hbm.at[pl.ds(b, n)], idx0)
            pltpu.sync_copy(pos_hbm.at[pl.ds(t_all + b, n)], idx1)
            pltpu.sync_copy(x_hbm.at[pl.ds(b, n)], rows)
            pltpu.sync_copy(rows, xs_hbm.at[idx0])
            pltpu.sync_copy(rows, xs_hbm.at[idx1])

        @pl.loop(0, n_ch)
        def _(j):
            move(pl.multiple_of(wid * per_w + j * SC_CHUNK, 8), SC_CHUNK, i0_v, i1_v, rows_v)

        if tail:
            move(pl.multiple_of(wid * per_w + n_ch * SC_CHUNK, 8), tail, *tail_refs)

        pltpu.sync_copy(z_hbm, z_v)

        @pl.loop(0, n_pch)
        def _(j):
            b = pl.multiple_of(wid * pad_per_w + j * SC_PAD_CHUNK, 8)
            pltpu.sync_copy(pad_hbm.at[pl.ds(b, SC_PAD_CHUNK)], ip_v)
            pltpu.sync_copy(z_v, xs_hbm.at[ip_v])

    xs = k(pos_flat, pad_rows, x_tiles.reshape(t_all, tile_rows, LANES), zeros)
    return xs.reshape(n_sorted_rows * tile_rows, LANES)


def _sc_gather_call(pos_flat, outs, tile_rows):
    n_assign = pos_flat.shape[0]
    dt = outs.dtype
    nc, nw = _sc_workers()
    per_w = n_assign // nw
    n_ch, tail = _sc_split(per_w)
    assert per_w * nw == n_assign
    tail_scratch = [] if tail == 0 else [pltpu.VMEM((tail,), jnp.int32), pltpu.VMEM((tail, tile_rows, LANES), dt)]

    @functools.partial(
        pl.kernel, mesh=plsc.VectorSubcoreMesh(core_axis_name="c", subcore_axis_name="s"),
        out_type=jax.ShapeDtypeStruct((n_assign, tile_rows, LANES), dt),
        scratch_types=[pltpu.VMEM((SC_CHUNK,), jnp.int32),
                       pltpu.VMEM((SC_CHUNK, tile_rows, LANES), dt)] + tail_scratch,
    )
    def k(pos_hbm, o_hbm, y_hbm, i_v, rows_v, *tail_refs):
        wid = lax.axis_index("s") * nc + lax.axis_index("c")

        def move(b, n, idx, rows):
            pltpu.sync_copy(pos_hbm.at[pl.ds(b, n)], idx)
            pltpu.sync_copy(o_hbm.at[idx], rows)
            pltpu.sync_copy(rows, y_hbm.at[pl.ds(b, n)])

        @pl.loop(0, n_ch)
        def _(j):
            move(pl.multiple_of(wid * per_w + j * SC_CHUNK, 8), SC_CHUNK, i_v, rows_v)

        if tail:
            move(pl.multiple_of(wid * per_w + n_ch * SC_CHUNK, 8), tail, *tail_refs)

    y = k(pos_flat, outs.reshape(-1, tile_rows, LANES))
    return y.reshape(n_assign * tile_rows, LANES)


def _combine_dense_kernel(alpha, n_main_tiles, x1t_ref, y0_ref, y1_ref, w_ref, g_ref, b_ref, out_ref, *tail):
    j = pl.program_id(0)
    w = w_ref[...].T
    y = (w[:, 0:1] * _load_packed_tiles(y0_ref, TOK_TILE, F32)
         + w[:, 1:2] * _load_packed_tiles(y1_ref, TOK_TILE, F32))
    res = _layer_norm(alpha * x1t_ref[...] + y, g_ref[...], b_ref[...])
    if n_main_tiles is None:
        out_ref[...] = res
    else:
        @pl.when(j < n_main_tiles)
        def _():
            out_ref[...] = res

        @pl.when(j >= n_main_tiles)
        def _():
            tail[0][...] = res


def _combine_dense_call(x1t, y_pair, w_col, g, b, alpha, n_main_tiles=None):
    t_all = x1t.shape[0]
    n_tiles = t_all // TOK_TILE
    if n_main_tiles is None:
        out_specs = pl.BlockSpec((TOK_TILE, D_MODEL), lambda i: (i, 0))
        out_shape = jax.ShapeDtypeStruct((t_all, D_MODEL), F32)
    else:
        assert n_tiles == n_main_tiles + 1
        out_specs = [pl.BlockSpec((TOK_TILE, D_MODEL), lambda i: (jnp.minimum(i, n_main_tiles - 1), 0)),
                     pl.BlockSpec((TOK_TILE, D_MODEL), lambda i: (0, 0))]
        out_shape = [jax.ShapeDtypeStruct((n_main_tiles * TOK_TILE, D_MODEL), F32),
                     jax.ShapeDtypeStruct((TOK_TILE, D_MODEL), F32)]
    tile = (TOK_TILE, D_MODEL)
    packed = (TOK_TILE * PACK_ROWS, LANES)
    return pl.pallas_call(
        functools.partial(_combine_dense_kernel, alpha, n_main_tiles),
        grid=(n_tiles,),
        in_specs=[
            pl.BlockSpec(tile, lambda i: (i, 0)),
            pl.BlockSpec(packed, lambda i: (i, 0)),
            pl.BlockSpec(packed, lambda i: (i + n_tiles, 0)),
            pl.BlockSpec((8, TOK_TILE), lambda i: (0, i)),
            pl.BlockSpec((1, D_MODEL), lambda i: (0, 0)),
            pl.BlockSpec((1, D_MODEL), lambda i: (0, 0)),
        ],
        out_specs=out_specs,
        out_shape=out_shape,
        compiler_params=pltpu.CompilerParams(dimension_semantics=("arbitrary",), vmem_limit_bytes=VMEM_LIMIT),
        name="combine",
    )(x1t, y_pair, y_pair, w_col, g, b)


def _rope_tables(seq, dec_seq, past_len):
    half = HEAD_DIM // 2
    inv = ROPE_THETA ** (-jnp.arange(half, dtype=F32) / half)
    pos = jnp.concatenate([jnp.arange(seq), past_len + (jnp.arange(TOK_TILE) % dec_seq)])
    ang = pos.astype(F32)[:, None] * inv[None, :]
    cos = jnp.cos(ang)
    sin = jnp.sin(ang)
    zero = jnp.zeros_like(sin)
    reps = LANES // HEAD_DIM
    cos_t = jnp.tile(jnp.concatenate([cos, cos], -1), (1, reps))
    sa_t = jnp.tile(jnp.concatenate([-sin, zero], -1), (1, reps))
    sb_t = jnp.tile(jnp.concatenate([zero, sin], -1), (1, reps))
    return cos_t, sa_t, sb_t


def _inclusive_cumsum(v):
    n = v.shape[0]
    tri = jnp.arange(n)[None, :] <= jnp.arange(n)[:, None]
    return jnp.sum(jnp.where(tri, v[None, :], 0), axis=1)


def _dispatch_plan(ridx, counts, n_tiles):
    t_all = ridx.shape[1]
    tiles_e = (counts + EXPERT_TILE - 1) // EXPERT_TILE
    tile_end = _inclusive_cumsum(tiles_e)
    offs = (tile_end - tiles_e) * EXPERT_TILE
    n_used = tile_end[-1:]
    tile_expert = jnp.minimum(
        jnp.sum((jnp.arange(n_tiles)[:, None] >= tile_end[None, :]).astype(jnp.int32), axis=1), N_EXPERTS - 1)
    experts = jnp.arange(N_EXPERTS)
    used = tiles_e > 0
    slot_e = (_inclusive_cumsum(used.astype(jnp.int32)) - 1) % 2
    next_e = jnp.min(jnp.where((experts[None, :] > experts[:, None]) & used[None, :], experts[None, :], N_EXPERTS),
                     axis=1)
    tiles = jnp.arange(n_tiles)
    of_tile = tile_expert[:, None] == experts[None, :]
    pick = lambda v: jnp.sum(jnp.where(of_tile, v[None, :], 0), axis=1)
    first = (tiles == pick(tile_end - tiles_e)) & (tiles < n_used)
    tile_plan = first.astype(jnp.int32) + 2 * pick(slot_e) + 4 * pick(next_e)
    pos = ridx[TOP_K:2 * TOP_K] + jnp.sum(
        jnp.where(ridx[:TOP_K, :, None] == experts[None, None, :], offs[None, None, :], 0), axis=-1)
    gap_start = jnp.concatenate([offs + counts, n_used * EXPERT_TILE])
    gap_len = jnp.concatenate([tiles_e * EXPERT_TILE - counts, (n_tiles - n_used) * EXPERT_TILE])
    gap_end_q = _inclusive_cumsum(gap_len)
    q = jnp.arange(n_tiles * EXPERT_TILE - TOP_K * t_all)
    gap = jnp.sum((q[:, None] >= gap_end_q[None, :]).astype(jnp.int32), axis=1)
    in_gap = gap[:, None] == jnp.arange(N_EXPERTS + 1)[None, :]
    pad_rows = q + jnp.sum(jnp.where(in_gap, (gap_start - gap_end_q + gap_len)[None, :], 0), axis=1)
    return (pos.reshape(-1).astype(jnp.int32), pad_rows.astype(jnp.int32),
            tile_expert.astype(jnp.int32), n_used.astype(jnp.int32), tile_plan.astype(jnp.int32))


def kernel(x_prompt, x_sample, cache_k, cache_v, w_in, sinks, gm_ln_g, gm_ln_b, gm_ws, gm_bs,
           out_norm_a, out_norm_b, w_o, ln1_g, ln1_b, w_router, router_bias,
           w_gate, w_up, w_down, ln2_g, ln2_b):
    batch, seq, _ = x_prompt.shape
    dec_batch, dec_seq, _ = x_sample.shape
    depth = w_in.shape[0]
    past_len = PAST_LEN
    assert dec_batch * dec_seq == TOK_TILE and dec_seq == CHUNK and seq % TOK_TILE == 0
    assert cache_k.shape[2] == WINDOW
    alpha = (2 * depth) ** 0.25
    tiles_per_seq = seq // TOK_TILE

    tables = _rope_tables(seq, dec_seq, past_len)
    wrt = w_router.T
    rb = router_bias.reshape(N_EXPERTS, 1)
    cache_k2 = cache_k.reshape(depth * dec_batch, WINDOW, KV_WIDTH)
    cache_v2 = cache_v.reshape(depth * dec_batch, WINDOW, KV_WIDTH)

    streams = [
        dict(x=(x_prompt.reshape(batch * seq, D_MODEL), x_sample.reshape(-1, D_MODEL)),
             n_seq=batch, has_sample=True),
    ]
    kp, vp, ks, vs, gms = [], [], [], [], []
    for l in range(depth):
        layer_args = (
            sinks[l], w_in,
            cache_k2, cache_v2,
            gm_ln_g[l].reshape(1, GM_WIDTH), gm_ln_b[l].reshape(1, GM_WIDTH),
            gm_ws[l], gm_bs[l].T,
            out_norm_a[l].reshape(1, ATT_WIDTH), out_norm_b[l].reshape(1, GM_WIDTH),
            w_o,
            ln1_g[l].reshape(1, D_MODEL), ln1_b[l].reshape(1, D_MODEL), wrt, rb)
        k_tails, v_tails = [], []
        for st in streams:
            n_prompt = st["n_seq"] * seq
            n_prompt_tiles = n_prompt // TOK_TILE
            t_all = n_prompt + (TOK_TILE if st["has_sample"] else 0)
            n_exp_tiles = (TOP_K * t_all) // EXPERT_TILE + N_EXPERTS
            res = _mixer_call(layer_args, st["x"], tables, n_prompt_tiles, tiles_per_seq, alpha, l)
            if st["has_sample"]:
                x1t, x1p, k_tail, v_tail, k_s, v_s, vn_s, ridx, rw, counts = res
            else:
                x1t, x1p, k_tail, v_tail, ridx, rw, counts = res

            pos, pad_rows, tile_expert, n_used, tile_plan = _dispatch_plan(ridx, counts[:, 0], n_exp_tiles)
            xs = _sc_dispatch_call(pos, pad_rows, x1p, PACK_ROWS, n_exp_tiles * EXPERT_TILE)
            outs = _expert_call(tile_expert, n_used, tile_plan, xs, w_gate, w_up, w_down, l)
            y_pair = _sc_gather_call(pos, outs, PACK_ROWS)
            combine_args = (x1t, y_pair, rw, ln2_g[l].reshape(1, D_MODEL), ln2_b[l].reshape(1, D_MODEL))
            if l < depth - 1:
                st["x"] = combine_args
            else:
                st["x"] = _combine_dense_call(*combine_args, alpha,
                                              n_main_tiles=n_prompt_tiles if st["has_sample"] else None)

            k_tails.append(k_tail)
            v_tails.append(v_tail)
            if st["has_sample"]:
                ks.append(k_s)
                vs.append(v_s)
                gms.append(vn_s)
        kp.append(jnp.concatenate(k_tails, axis=0))
        vp.append(jnp.concatenate(v_tails, axis=0))

    y_prompt = jnp.concatenate([st["x"][0] if st["has_sample"] else st["x"] for st in streams],
                               axis=0).reshape(batch, seq, D_MODEL)
    y_sample = streams[-1]["x"][1].reshape(dec_batch, dec_seq, D_MODEL)
    heads = (N_KV_HEADS, HEAD_DIM)
    return (y_prompt, y_sample,
            jnp.stack(kp).reshape(depth, batch, WINDOW, *heads), jnp.stack(vp).reshape(depth, batch, WINDOW, *heads),
            jnp.stack(ks).reshape(depth, dec_batch, dec_seq, *heads),
            jnp.stack(vs).reshape(depth, dec_batch, dec_seq, *heads),
            jnp.stack(gms).reshape(depth, dec_batch, dec_seq, GM_WIDTH))
```

```python
import functools

import jax
import jax.numpy as jnp
from jax import lax
from jax.experimental import pallas as pl
from jax.experimental.pallas import tpu as pltpu
from jax.experimental.pallas import tpu_sc as plsc

D_MODEL = 1024
CHUNK = 64
WINDOW = 128
HEAD_DIM = 64
N_Q_HEADS = 8
N_KV_HEADS = 2
Q_PER_KV = N_Q_HEADS // N_KV_HEADS
PAST_LEN = 2048
ATT_WIDTH = N_Q_HEADS * HEAD_DIM
KV_WIDTH = N_KV_HEADS * HEAD_DIM
ROPE_THETA = 10000.0
GM_GROUPS = 4
GM_CH = 128
GM_WIDTH = GM_GROUPS * GM_CH
GM_CHUNK = 128
D_MIX = ATT_WIDTH + GM_WIDTH
D_IN = ATT_WIDTH + 2 * KV_WIDTH + 2 * GM_WIDTH
N_EXPERTS = 16
N_EXPERT_GROUPS = 4
EXPERTS_PER_GROUP = N_EXPERTS // N_EXPERT_GROUPS
TOP_K = 2
D_EXPERT = 512
LN_EPS = 1e-5
NEG_INF = -1e30

LANES = 128
PACK_ROWS = D_MODEL // (2 * LANES)
HI_HALF = 0xFFFF0000
TOK_TILE = 512
EXPERT_TILE = 512
KEYS_PER_CHUNK = WINDOW + CHUNK
VMEM_LIMIT = 60 * 1024 * 1024

F32 = jnp.float32
BF16 = jnp.bfloat16


def _layer_norm(x, g, b):
    mu = jnp.mean(x, axis=-1, keepdims=True)
    d = x - mu
    var = jnp.mean(d * d, axis=-1, keepdims=True)
    return d * lax.rsqrt(var + LN_EPS) * g + b


def _rms_norm(x, g):
    return x * lax.rsqrt(jnp.mean(x * x, axis=-1, keepdims=True) + LN_EPS) * g


def _gelu(x):
    return 0.5 * x * (1.0 + lax.erf(x * (0.5 ** 0.5)))


def _bf16_bits(x):
    return lax.bitcast_convert_type(x.astype(BF16).astype(F32), jnp.uint32)


def _store_packed_tiles(ref, x):
    n = x.shape[0]
    for s in range(PACK_ROWS):
        lo = _bf16_bits(x[:, (2 * s) * LANES:(2 * s + 1) * LANES]) >> 16
        hi = _bf16_bits(x[:, (2 * s + 1) * LANES:(2 * s + 2) * LANES]) & jnp.uint32(HI_HALF)
        ref[pl.ds(s, n, stride=PACK_ROWS), :] = hi | lo


def _load_packed_tiles(ref, n, dtype):
    cols = []
    for s in range(PACK_ROWS):
        w = ref[pl.ds(s, n, stride=PACK_ROWS), :]
        cols.append(lax.bitcast_convert_type(w << 16, F32).astype(dtype))
        cols.append(lax.bitcast_convert_type(w & jnp.uint32(HI_HALF), F32).astype(dtype))
    return jnp.concatenate(cols, axis=-1)


def _first_index_of_max(rows):
    m = rows[0]
    for r in rows[1:]:
        m = jnp.maximum(m, r)
    idx = jnp.full(m.shape, len(rows), jnp.int32)
    for e in reversed(range(len(rows))):
        idx = jnp.where(rows[e] == m, e, idx)
    return m, idx


def _mixer_kernel(n_prompt_tiles, tiles_per_seq, alpha, has_sample, first_layer, sinks_ref, *refs):
    n_x = 2 if first_layer else 6
    x_refs, refs = refs[:n_x], refs[n_x:]
    (w_in_ref, cos_ref, sa_ref, sb_ref, ck_ref, cv_ref, lng_ref, lnb_ref, ws_ref, bst_ref, na_ref, nb_ref, w_o_ref,
     l1g_ref, l1b_ref, wrt_ref, rb_ref, tri_ref, x1t_ref, x1p_ref, kt_ref, vt_ref) = refs[:22]
    rest = refs[22:]
    if has_sample:
        (ks_ref, vs_ref, vn_ref, ridx_ref, rw_ref, counts_ref,
         kd_ref, vd_ref, qm_ref, att_ref, gm_ref, cnt_ref, w_in_s, w_o_s, x_s) = rest
    else:
        (ridx_ref, rw_ref, counts_ref,
         kd_ref, vd_ref, qm_ref, att_ref, gm_ref, cnt_ref, w_in_s, w_o_s, x_s) = rest
    i = pl.program_id(0)
    is_sample = i >= n_prompt_tiles
    seq_start = jnp.logical_and(jnp.logical_not(is_sample), (i % tiles_per_seq) == 0)
    seq_end = jnp.logical_and(jnp.logical_not(is_sample), (i % tiles_per_seq) == tiles_per_seq - 1)
    tm = TOK_TILE

    @pl.when(i == 0)
    def _():
        w_in_s[...] = w_in_ref[0].astype(BF16)
        w_o_s[...] = w_o_ref[0].astype(BF16)

    if first_layer:
        xp_ref, xs_ref = x_refs
        x_s[...] = jnp.where(is_sample, xs_ref[...], xp_ref[...])
    else:
        x1_prev_ref, y0_ref, y1_ref, wc_ref, l2g_ref, l2b_ref = x_refs
        wc = wc_ref[...].T
        y_moe = (wc[:, 0:1] * _load_packed_tiles(y0_ref, tm, F32)
                 + wc[:, 1:2] * _load_packed_tiles(y1_ref, tm, F32))
        x_s[...] = _layer_norm(alpha * x1_prev_ref[...] + y_moe, l2g_ref[...], l2b_ref[...])

    h = jnp.dot(x_s[...].astype(BF16), w_in_s[...], preferred_element_type=F32)

    cos = cos_ref[...]
    sa = sa_ref[...]
    sb = sb_ref[...]

    def rope(blk):
        return blk * cos + pltpu.roll(blk, LANES - HEAD_DIM // 2, 1) * sa + pltpu.roll(blk, HEAD_DIM // 2, 1) * sb

    lane = lax.broadcasted_iota(jnp.int32, (1, LANES), 1)
    lo_half = lane < HEAD_DIM

    o_k = ATT_WIDTH
    o_v = o_k + KV_WIDTH
    o_u = o_v + KV_WIDTH
    o_g = o_u + GM_WIDTH

    k_rot = rope(h[:, o_k:o_v])
    v_new = h[:, o_v:o_u]

    @pl.when(seq_end)
    def _():
        kt_ref[...] = k_rot[tm - WINDOW:]
        vt_ref[...] = v_new[tm - WINDOW:]

    if has_sample:
        @pl.when(is_sample)
        def _():
            ks_ref[...] = k_rot
            vs_ref[...] = v_new

    def dup_heads(a):
        sw = pltpu.roll(a, HEAD_DIM, 1)
        return jnp.where(lo_half, a, sw).astype(BF16), jnp.where(lo_half, sw, a).astype(BF16)

    k_d = dup_heads(k_rot)
    v_d = dup_heads(v_new)

    @pl.when(jnp.logical_not(is_sample))
    def _():
        @pl.when(seq_start)
        def _():
            for g in range(N_KV_HEADS):
                kd_ref[g, 0:WINDOW, :] = jnp.zeros((WINDOW, LANES), BF16)
                vd_ref[g, 0:WINDOW, :] = jnp.zeros((WINDOW, LANES), BF16)

        @pl.when(jnp.logical_not(seq_start))
        def _():
            for g in range(N_KV_HEADS):
                kd_ref[g, 0:WINDOW, :] = kd_ref[g, tm:tm + WINDOW, :]
                vd_ref[g, 0:WINDOW, :] = vd_ref[g, tm:tm + WINDOW, :]

        for g in range(N_KV_HEADS):
            kd_ref[g, WINDOW:WINDOW + tm, :] = k_d[g]
            vd_ref[g, WINDOW:WINDOW + tm, :] = v_d[g]

    @pl.when(is_sample)
    def _():
        for b in range(tm // CHUNK):
            ck = dup_heads(ck_ref[b])
            cv = dup_heads(cv_ref[b])
            base = b * KEYS_PER_CHUNK
            for g in range(N_KV_HEADS):
                kd_ref[g, base:base + WINDOW, :] = ck[g]
                vd_ref[g, base:base + WINDOW, :] = cv[g]
                kd_ref[g, base + WINDOW:base + KEYS_PER_CHUNK, :] = k_d[g][b * CHUNK:(b + 1) * CHUNK]
                vd_ref[g, base + WINDOW:base + KEYS_PER_CHUNK, :] = v_d[g][b * CHUNK:(b + 1) * CHUNK]

    scale = HEAD_DIM ** -0.5
    for b in range(ATT_WIDTH // LANES):
        qb = (rope(h[:, b * LANES:(b + 1) * LANES]) * scale).astype(BF16)
        zero = jnp.zeros_like(qb)
        qm_ref[2 * b] = jnp.where(lo_half, qb, zero)
        qm_ref[2 * b + 1] = jnp.where(lo_half, zero, qb)

    key_stride = jnp.where(is_sample, KEYS_PER_CHUNK, CHUNK)
    col = lax.broadcasted_iota(jnp.int32, (1, KEYS_PER_CHUNK), 1)
    row_blk = lax.broadcasted_iota(jnp.int32, (Q_PER_KV * CHUNK, 1), 0) // CHUNK
    sink_cols = [
        jnp.where(row_blk == 0, sinks_ref[Q_PER_KV * g],
                  jnp.where(row_blk == 1, sinks_ref[Q_PER_KV * g + 1],
                            jnp.where(row_blk == 2, sinks_ref[Q_PER_KV * g + 2], sinks_ref[Q_PER_KV * g + 3])))
        for g in range(N_KV_HEADS)]

    for c in range(tm // CHUNK):
        r0 = c * CHUNK
        k0 = pl.multiple_of(c * key_stride, CHUNK)
        outs = []
        for g in range(N_KV_HEADS):
            q_st = jnp.concatenate([qm_ref[Q_PER_KV * g + r, pl.ds(r0, CHUNK), :] for r in range(Q_PER_KV)],
                                   axis=0)
            keys = kd_ref[g, pl.ds(k0, KEYS_PER_CHUNK), :]
            vals = vd_ref[g, pl.ds(k0, KEYS_PER_CHUNK), :]
            s = lax.dot_general(q_st, keys, (((1,), (1,)), ((), ())), preferred_element_type=F32)
            if c < WINDOW // CHUNK:
                first_valid = jnp.where(seq_start, (WINDOW // CHUNK - c) * CHUNK, 0)
                s = jnp.where(col >= first_valid, s, NEG_INF)
            sink = sink_cols[g]
            m = jnp.maximum(jnp.max(s, axis=-1, keepdims=True), sink)
            p = jnp.exp(s - m)
            denom = jnp.sum(p, axis=-1, keepdims=True) + jnp.exp(sink - m)
            o = jnp.dot(p.astype(BF16), vals, preferred_element_type=F32) * (1.0 / denom)
            for bb in range(2):
                outs.append(jnp.where(lo_half, o[(2 * bb) * CHUNK:(2 * bb + 1) * CHUNK],
                                      o[(2 * bb + 1) * CHUNK:(2 * bb + 2) * CHUNK]))
        for b in range(ATT_WIDTH // LANES):
            att_ref[pl.ds(r0, CHUNK), b * LANES:(b + 1) * LANES] = outs[b]

    prow = lax.broadcasted_iota(jnp.int32, (GM_CHUNK, GM_CHUNK), 0)
    pcol = lax.broadcasted_iota(jnp.int32, (GM_CHUNK, GM_CHUNK), 1)
    half = GM_CHUNK // 2
    tril = pcol <= prow
    same_blk = (prow < half) == (pcol < half)
    prow1 = lax.broadcasted_iota(jnp.int32, (GM_CHUNK, 1), 0)
    bst = bst_ref[...]
    bst_s = jnp.where(prow1 < half, bst, pltpu.roll(bst, half, 0))
    bias = jnp.where(is_sample, bst_s, bst)
    for g in range(GM_GROUPS):
        wg_ = ws_ref[g]
        w_s = jnp.where(prow < half, wg_, pltpu.roll(pltpu.roll(wg_, half, 0), half, 1))
        w_eff = jnp.where(is_sample, jnp.where(same_blk, w_s, 0.0), wg_)
        w_eff = jnp.where(tril, w_eff, 0.0).astype(BF16)
        cols = slice(g * GM_CH, (g + 1) * GM_CH)
        ug = _gelu(h[:, o_u + g * GM_CH:o_u + (g + 1) * GM_CH])
        vn = _layer_norm(_gelu(h[:, o_g + g * GM_CH:o_g + (g + 1) * GM_CH]), lng_ref[:, cols], lnb_ref[:, cols])

        if has_sample:
            @pl.when(is_sample)
            def _(cols=cols, vn=vn):
                vn_ref[:, cols] = vn

        vn_b = vn.astype(BF16)
        for n in range(tm // GM_CHUNK):
            rows = slice(n * GM_CHUNK, (n + 1) * GM_CHUNK)
            s = jnp.dot(w_eff, vn_b[rows], preferred_element_type=F32) + bias[:, g:g + 1]
            gm_ref[rows, cols] = ug[rows] * s

    mixed = jnp.concatenate([_rms_norm(att_ref[...], na_ref[...]), _rms_norm(gm_ref[...], nb_ref[...])], axis=-1)
    y = alpha * x_s[...] + jnp.dot(mixed.astype(BF16), w_o_s[...], preferred_element_type=F32)
    x1 = _layer_norm(y, l1g_ref[...], l1b_ref[...])
    x1t_ref[...] = x1
    _store_packed_tiles(x1p_ref, x1)

    nt = (((1,), (1,)), ((), ()))
    wr = wrt_ref[...]
    wr_hi = wr.astype(BF16)
    wr_lo = (wr - wr_hi.astype(F32)).astype(BF16)
    x1_hi = x1.astype(BF16)
    x1_lo = (x1 - x1_hi.astype(F32)).astype(BF16)
    logits = (lax.dot_general(wr_hi, x1_hi, nt, preferred_element_type=F32)
              + (lax.dot_general(wr_hi, x1_lo, nt, preferred_element_type=F32)
                 + lax.dot_general(wr_lo, x1_hi, nt, preferred_element_type=F32)))
    scores = jax.nn.sigmoid(logits)
    sel = scores + rb_ref[...]
    sel_rows = [sel[e:e + 1, :] for e in range(N_EXPERTS)]
    sc_rows = [scores[e:e + 1, :] for e in range(N_EXPERTS)]
    grp = []
    for g in range(N_EXPERT_GROUPS):
        r = sel_rows[g * EXPERTS_PER_GROUP:(g + 1) * EXPERTS_PER_GROUP]
        best_pair = None
        for a in range(EXPERTS_PER_GROUP):
            for b in range(a + 1, EXPERTS_PER_GROUP):
                pair = r[a] + r[b]
                best_pair = pair if best_pair is None else jnp.maximum(best_pair, pair)
        grp.append(best_pair)
    _, best = _first_index_of_max(grp)
    masked = [jnp.where(best == (e // EXPERTS_PER_GROUP), sel_rows[e], NEG_INF) for e in range(N_EXPERTS)]
    _, e0 = _first_index_of_max(masked)
    masked2 = [jnp.where(e0 == e, -jnp.inf, masked[e]) for e in range(N_EXPERTS)]
    _, e1 = _first_index_of_max(masked2)
    w0 = jnp.zeros_like(sc_rows[0])
    w1 = jnp.zeros_like(sc_rows[0])
    for e in range(N_EXPERTS):
        w0 = jnp.where(e0 == e, sc_rows[e], w0)
        w1 = jnp.where(e1 == e, sc_rows[e], w1)
    wsum = w0 + w1
    rw_ref[...] = jnp.concatenate([w0 / wsum, w1 / wsum, jnp.zeros((6, tm), F32)], axis=0)

    @pl.when(i == 0)
    def _():
        cnt_ref[...] = jnp.zeros_like(cnt_ref)

    eid = lax.broadcasted_iota(jnp.int32, (N_EXPERTS, tm), 0)
    oh0 = (eid == e0).astype(F32)
    oh1 = (eid == e1).astype(F32)
    oh = oh0 + oh1
    incl = jnp.dot(oh.astype(BF16), tri_ref[...], preferred_element_type=F32)
    before = cnt_ref[:, 0:1] + incl - oh
    r0 = jnp.sum(oh0 * before, axis=0, keepdims=True).astype(jnp.int32)
    r1 = jnp.sum(oh1 * before, axis=0, keepdims=True).astype(jnp.int32)
    cnt = cnt_ref[...] + incl[:, tm - 1:tm]
    cnt_ref[...] = cnt
    counts_ref[...] = cnt.astype(jnp.int32)
    ridx_ref[...] = jnp.concatenate([e0, e1, r0, r1, jnp.zeros((4, tm), jnp.int32)], axis=0)


def _mixer_call(layer_args, x_all, tables, n_prompt_tiles, tiles_per_seq, alpha, layer):
    (sinks, w_in, ck, cv, lng, lnb, ws, bst, na, nb, w_o, l1g, l1b, wrt, rb) = layer_args
    cos_t, sa_t, sb_t = tables

    def of_layer(shape, which=layer):
        nd = len(shape)
        return pl.BlockSpec((None,) + shape, lambda i, _nd=nd: (which,) + (0,) * _nd)

    first_layer = len(x_all) == 2
    if first_layer:
        x_main, x_tail = x_all
        t_all = x_main.shape[0] + x_tail.shape[0]
        n_tiles = t_all // TOK_TILE
        x_args = [x_main, x_tail]
        x_specs = [pl.BlockSpec((TOK_TILE, D_MODEL), lambda i: (jnp.minimum(i, n_prompt_tiles - 1), 0)),
                   pl.BlockSpec((TOK_TILE, D_MODEL), lambda i: (0, 0))]
    else:
        x1t_prev, y_pair, w_col, l2g, l2b = x_all
        t_all = x1t_prev.shape[0]
        n_tiles = t_all // TOK_TILE
        x_args = [x1t_prev, y_pair, y_pair, w_col, l2g, l2b]
        x_specs = [pl.BlockSpec((TOK_TILE, D_MODEL), lambda i: (i, 0)),
                   pl.BlockSpec((TOK_TILE * PACK_ROWS, LANES), lambda i: (i, 0)),
                   pl.BlockSpec((TOK_TILE * PACK_ROWS, LANES), lambda i: (i + n_tiles, 0)),
                   pl.BlockSpec((8, TOK_TILE), lambda i: (0, i)),
                   of_layer((1, D_MODEL), layer - 1), of_layer((1, D_MODEL), layer - 1)]
    n_seq_tiles = tiles_per_seq
    has_sample = n_tiles > n_prompt_tiles

    def const(shape):
        nd = len(shape)
        return pl.BlockSpec(shape, lambda i, _nd=nd: (0,) * _nd)

    def tab_map(i):
        return (jnp.where(i < n_prompt_tiles, i % n_seq_tiles, n_seq_tiles), 0)

    n_seq = n_prompt_tiles // n_seq_tiles

    def seq_map(i):
        return (jnp.minimum(i // n_seq_tiles, n_seq - 1), 0)

    row_blk = lambda w: pl.BlockSpec((TOK_TILE, w), lambda i: (i, 0))
    in_specs = [
        pl.BlockSpec(memory_space=pltpu.SMEM),
        *x_specs,
        pl.BlockSpec((1, D_MODEL, D_IN), lambda i: (layer, 0, 0)),
        pl.BlockSpec((TOK_TILE, LANES), tab_map),
        pl.BlockSpec((TOK_TILE, LANES), tab_map),
        pl.BlockSpec((TOK_TILE, LANES), tab_map),
        pl.BlockSpec((TOK_TILE // CHUNK, WINDOW, KV_WIDTH), lambda i: (layer, 0, 0)),
        pl.BlockSpec((TOK_TILE // CHUNK, WINDOW, KV_WIDTH), lambda i: (layer, 0, 0)),
        of_layer((1, GM_WIDTH)), of_layer((1, GM_WIDTH)),
        of_layer((GM_GROUPS, GM_CHUNK, GM_CHUNK)), of_layer((GM_CHUNK, GM_GROUPS)),
        of_layer((1, ATT_WIDTH)), of_layer((1, GM_WIDTH)),
        pl.BlockSpec((1, D_MIX, D_MODEL), lambda i: (layer, 0, 0)),
        of_layer((1, D_MODEL)), of_layer((1, D_MODEL)),
        const((N_EXPERTS, D_MODEL)), const((N_EXPERTS, 1)),
        const((TOK_TILE, TOK_TILE)),
    ]
    out_shape = [
        jax.ShapeDtypeStruct((t_all, D_MODEL), F32),
        jax.ShapeDtypeStruct((t_all * PACK_ROWS, LANES), jnp.uint32),
        jax.ShapeDtypeStruct((n_seq * WINDOW, KV_WIDTH), F32),
        jax.ShapeDtypeStruct((n_seq * WINDOW, KV_WIDTH), F32),
        jax.ShapeDtypeStruct((TOK_TILE, KV_WIDTH), F32),
        jax.ShapeDtypeStruct((TOK_TILE, KV_WIDTH), F32),
        jax.ShapeDtypeStruct((TOK_TILE, GM_WIDTH), F32),
        jax.ShapeDtypeStruct((8, t_all), jnp.int32),
        jax.ShapeDtypeStruct((8, t_all), F32),
        jax.ShapeDtypeStruct((N_EXPERTS, LANES), jnp.int32),
    ]
    out_specs = [
        pl.BlockSpec((TOK_TILE, D_MODEL), lambda i: (i, 0)),
        pl.BlockSpec((TOK_TILE * PACK_ROWS, LANES), lambda i: (i, 0)),
        pl.BlockSpec((WINDOW, KV_WIDTH), seq_map), pl.BlockSpec((WINDOW, KV_WIDTH), seq_map),
        const((TOK_TILE, KV_WIDTH)), const((TOK_TILE, KV_WIDTH)),
        const((TOK_TILE, GM_WIDTH)),
        pl.BlockSpec((8, TOK_TILE), lambda i: (0, i)),
        pl.BlockSpec((8, TOK_TILE), lambda i: (0, i)),
        const((N_EXPERTS, LANES)),
    ]
    if not has_sample:
        del out_shape[4:7], out_specs[4:7]
    kd_rows = (TOK_TILE // CHUNK) * KEYS_PER_CHUNK
    scratch = [
        pltpu.VMEM((N_KV_HEADS, kd_rows, LANES), BF16),
        pltpu.VMEM((N_KV_HEADS, kd_rows, LANES), BF16),
        pltpu.VMEM((N_Q_HEADS, TOK_TILE, LANES), BF16),
        pltpu.VMEM((TOK_TILE, ATT_WIDTH), F32),
        pltpu.VMEM((TOK_TILE, GM_WIDTH), F32),
        pltpu.VMEM((N_EXPERTS, LANES), F32),
        pltpu.VMEM((D_MODEL, D_IN), BF16),
        pltpu.VMEM((D_MIX, D_MODEL), BF16),
        pltpu.VMEM((TOK_TILE, D_MODEL), F32),
    ]
    return pl.pallas_call(
        functools.partial(_mixer_kernel, n_prompt_tiles, tiles_per_seq, alpha, has_sample, first_layer),
        grid=(n_tiles,),
        in_specs=in_specs, out_specs=out_specs, out_shape=out_shape,
        scratch_shapes=scratch,
        compiler_params=pltpu.CompilerParams(dimension_semantics=("arbitrary",), vmem_limit_bytes=VMEM_LIMIT),
        name="mixer",
    )(sinks, *x_args, w_in, cos_t, sa_t, sb_t, ck, cv, lng, lnb, ws, bst, na, nb, w_o, l1g, l1b, wrt, rb,
      jnp.triu(jnp.ones((TOK_TILE, TOK_TILE), BF16)))


def _expert_kernel(layer, te_ref, nt_ref, plan_ref, xs_ref, wg_hbm, wu_hbm, wd_hbm, out_ref,
                   wg_s, wu_s, wd_s, wg_f, wu_f, wd_f, sems):
    i = pl.program_id(0)
    plan = plan_ref[i]
    first = (plan & 1) == 1
    slot = (plan >> 1) & 1
    nxt = plan >> 2

    def weight_copies(e, s):
        return (pltpu.make_async_copy(wg_hbm.at[layer, e], wg_f.at[s], sems.at[s, 0]),
                pltpu.make_async_copy(wu_hbm.at[layer, e], wu_f.at[s], sems.at[s, 1]),
                pltpu.make_async_copy(wd_hbm.at[layer, e], wd_f.at[s], sems.at[s, 2]))

    @pl.when(i == 0)
    def _():
        for c in weight_copies(te_ref[0], slot):
            c.start()

    @pl.when(first)
    def _():
        for c in weight_copies(te_ref[i], slot):
            c.wait()

        @pl.when(nxt < N_EXPERTS)
        def _():
            for c in weight_copies(nxt, 1 - slot):
                c.start()

        wg_s[...] = wg_f[slot].astype(BF16)
        wu_s[...] = wu_f[slot].astype(BF16)
        wd_s[...] = wd_f[slot].astype(BF16)

    @pl.when(i < nt_ref[0])
    def _():
        xs = _load_packed_tiles(xs_ref, EXPERT_TILE, BF16)
        g = jnp.dot(xs, wg_s[...], preferred_element_type=F32)
        u = jnp.dot(xs, wu_s[...], preferred_element_type=F32)
        hmid = (g * jax.nn.sigmoid(g)) * u
        _store_packed_tiles(out_ref, jnp.dot(hmid.astype(BF16), wd_s[...], preferred_element_type=F32))


def _expert_call(tile_expert, n_used, tile_plan, xs, wg, wu, wd, layer):
    n_tiles = xs.shape[0] // (EXPERT_TILE * PACK_ROWS)
    tile_blk = pl.BlockSpec((EXPERT_TILE * PACK_ROWS, LANES),
                            lambda i, te, nt, plan: (jnp.minimum(i, nt[0] - 1), 0))
    grid_spec = pltpu.PrefetchScalarGridSpec(
        num_scalar_prefetch=3,
        grid=(n_tiles,),
        in_specs=[tile_blk] + [pl.BlockSpec(memory_space=pl.ANY)] * 3,
        out_specs=tile_blk,
        scratch_shapes=[
            pltpu.VMEM((D_MODEL, D_EXPERT), BF16),
            pltpu.VMEM((D_MODEL, D_EXPERT), BF16),
            pltpu.VMEM((D_EXPERT, D_MODEL), BF16),
            pltpu.VMEM((2, D_MODEL, D_EXPERT), F32),
            pltpu.VMEM((2, D_MODEL, D_EXPERT), F32),
            pltpu.VMEM((2, D_EXPERT, D_MODEL), F32),
            pltpu.SemaphoreType.DMA((2, 3)),
        ],
    )
    return pl.pallas_call(
        functools.partial(_expert_kernel, layer),
        grid_spec=grid_spec,
        out_shape=jax.ShapeDtypeStruct(xs.shape, jnp.uint32),
        compiler_params=pltpu.CompilerParams(dimension_semantics=("arbitrary",), vmem_limit_bytes=VMEM_LIMIT),
        name="experts",
    )(tile_expert, n_used, tile_plan, xs, wg, wu, wd)


SC_CHUNK = 128
SC_PAD_CHUNK = 64


def _sc_workers():
    info = plsc.get_sparse_core_info()
    return info.num_cores, info.num_cores * info.num_subcores


def _sc_split(per_w):
    n_full = per_w // SC_CHUNK
    tail = per_w - n_full * SC_CHUNK
    assert tail % 8 == 0
    return n_full, tail


def _sc_dispatch_call(pos_flat, pad_rows, x_tiles, tile_rows, n_sorted_rows):
    t_all = x_tiles.shape[0] // tile_rows
    dt = x_tiles.dtype
    nc, nw = _sc_workers()
    per_w = t_all // nw
    n_ch, tail = _sc_split(per_w)
    n_pad = pad_rows.shape[0]
    pad_per_w = n_pad // nw
    n_pch = pad_per_w // SC_PAD_CHUNK
    assert per_w * nw == t_all and n_pch * SC_PAD_CHUNK * nw == n_pad
    zeros = jnp.zeros((SC_PAD_CHUNK, tile_rows, LANES), dt)
    tail_scratch = [] if tail == 0 else [
        pltpu.VMEM((tail,), jnp.int32), pltpu.VMEM((tail,), jnp.int32), pltpu.VMEM((tail, tile_rows, LANES), dt)]

    @functools.partial(
        pl.kernel, mesh=plsc.VectorSubcoreMesh(core_axis_name="c", subcore_axis_name="s"),
        out_type=jax.ShapeDtypeStruct((n_sorted_rows, tile_rows, LANES), dt),
        scratch_types=[
            pltpu.VMEM((SC_CHUNK,), jnp.int32), pltpu.VMEM((SC_CHUNK,), jnp.int32),
            pltpu.VMEM((SC_CHUNK, tile_rows, LANES), dt),
            pltpu.VMEM((SC_PAD_CHUNK,), jnp.int32),
            pltpu.VMEM((SC_PAD_CHUNK, tile_rows, LANES), dt),
        ] + tail_scratch,
    )
    def k(pos_hbm, pad_hbm, x_hbm, z_hbm, xs_hbm, i0_v, i1_v, rows_v, ip_v, z_v, *tail_refs):
        wid = lax.axis_index("s") * nc + lax.axis_index("c")

        def move(b, n, idx0, idx1, rows):
            pltpu.sync_copy(pos_hbm.at[pl.ds(b, n)], idx0)
            pltpu.sync_copy(pos_hbm.at[pl.ds(t_all + b, n)], idx1)
            pltpu.sync_copy(x_hbm.at[pl.ds(b, n)], rows)
            pltpu.sync_copy(rows, xs_hbm.at[idx0])
            pltpu.sync_copy(rows, xs_hbm.at[idx1])

        @pl.loop(0, n_ch)
        def _(j):
            move(pl.multiple_of(wid * per_w + j * SC_CHUNK, 8), SC_CHUNK, i0_v, i1_v, rows_v)

        if tail:
            move(pl.multiple_of(wid * per_w + n_ch * SC_CHUNK, 8), tail, *tail_refs)

        pltpu.sync_copy(z_hbm, z_v)

        @pl.loop(0, n_pch)
        def _(j):
            b = pl.multiple_of(wid * pad_per_w + j * SC_PAD_CHUNK, 8)
            pltpu.sync_copy(pad_hbm.at[pl.ds(b, SC_PAD_CHUNK)], ip_v)
            pltpu.sync_copy(z_v, xs_hbm.at[ip_v])

    xs = k(pos_flat, pad_rows, x_tiles.reshape(t_all, tile_rows, LANES), zeros)
    return xs.reshape(n_sorted_rows * tile_rows, LANES)


def _sc_gather_call(pos_flat, outs, tile_rows):
    n_assign = pos_flat.shape[0]
    dt = outs.dtype
    nc, nw = _sc_workers()
    per_w = n_assign // nw
    n_ch, tail = _sc_split(per_w)
    assert per_w * nw == n_assign
    tail_scratch = [] if tail == 0 else [pltpu.VMEM((tail,), jnp.int32), pltpu.VMEM((tail, tile_rows, LANES), dt)]

    @functools.partial(
        pl.kernel, mesh=plsc.VectorSubcoreMesh(core_axis_name="c", subcore_axis_name="s"),
        out_type=jax.ShapeDtypeStruct((n_assign, tile_rows, LANES), dt),
        scratch_types=[pltpu.VMEM((SC_CHUNK,), jnp.int32),
                       pltpu.VMEM((SC_CHUNK, tile_rows, LANES), dt)] + tail_scratch,
    )
    def k(pos_hbm, o_hbm, y_hbm, i_v, rows_v, *tail_refs):
        wid = lax.axis_index("s") * nc + lax.axis_index("c")

        def move(b, n, idx, rows):
            pltpu.sync_copy(pos_hbm.at[pl.ds(b, n)], idx)
            pltpu.sync_copy(o_hbm.at[idx], rows)
            pltpu.sync_copy(rows, y_hbm.at[pl.ds(b, n)])

        @pl.loop(0, n_ch)
        def _(j):
            move(pl.multiple_of(wid * per_w + j * SC_CHUNK, 8), SC_CHUNK, i_v, rows_v)

        if tail:
            move(pl.multiple_of(wid * per_w + n_ch * SC_CHUNK, 8), tail, *tail_refs)

    y = k(pos_flat, outs.reshape(-1, tile_rows, LANES))
    return y.reshape(n_assign * tile_rows, LANES)


def _combine_dense_kernel(alpha, n_main_tiles, x1t_ref, y0_ref, y1_ref, w_ref, g_ref, b_ref, out_ref, *tail):
    j = pl.program_id(0)
    w = w_ref[...].T
    y = (w[:, 0:1] * _load_packed_tiles(y0_ref, TOK_TILE, F32)
         + w[:, 1:2] * _load_packed_tiles(y1_ref, TOK_TILE, F32))
    res = _layer_norm(alpha * x1t_ref[...] + y, g_ref[...], b_ref[...])
    if n_main_tiles is None:
        out_ref[...] = res
    else:
        @pl.when(j < n_main_tiles)
        def _():
            out_ref[...] = res

        @pl.when(j >= n_main_tiles)
        def _():
            tail[0][...] = res


def _combine_dense_call(x1t, y_pair, w_col, g, b, alpha, n_main_tiles=None):
    t_all = x1t.shape[0]
    n_tiles = t_all // TOK_TILE
    if n_main_tiles is None:
        out_specs = pl.BlockSpec((TOK_TILE, D_MODEL), lambda i: (i, 0))
        out_shape = jax.ShapeDtypeStruct((t_all, D_MODEL), F32)
    else:
        assert n_tiles == n_main_tiles + 1
        out_specs = [pl.BlockSpec((TOK_TILE, D_MODEL), lambda i: (jnp.minimum(i, n_main_tiles - 1), 0)),
                     pl.BlockSpec((TOK_TILE, D_MODEL), lambda i: (0, 0))]
        out_shape = [jax.ShapeDtypeStruct((n_main_tiles * TOK_TILE, D_MODEL), F32),
                     jax.ShapeDtypeStruct((TOK_TILE, D_MODEL), F32)]
    tile = (TOK_TILE, D_MODEL)
    packed = (TOK_TILE * PACK_ROWS, LANES)
    return pl.pallas_call(
        functools.partial(_combine_dense_kernel, alpha, n_main_tiles),
        grid=(n_tiles,),
        in_specs=[
            pl.BlockSpec(tile, lambda i: (i, 0)),
            pl.BlockSpec(packed, lambda i: (i, 0)),
            pl.BlockSpec(packed, lambda i: (i + n_tiles, 0)),
            pl.BlockSpec((8, TOK_TILE), lambda i: (0, i)),
            pl.BlockSpec((1, D_MODEL), lambda i: (0, 0)),
            pl.BlockSpec((1, D_MODEL), lambda i: (0, 0)),
        ],
        out_specs=out_specs,
        out_shape=out_shape,
        compiler_params=pltpu.CompilerParams(dimension_semantics=("arbitrary",), vmem_limit_bytes=VMEM_LIMIT),
        name="combine",
    )(x1t, y_pair, y_pair, w_col, g, b)


def _rope_tables(seq, dec_seq, past_len):
    half = HEAD_DIM // 2
    inv = ROPE_THETA ** (-jnp.arange(half, dtype=F32) / half)
    pos = jnp.concatenate([jnp.arange(seq), past_len + (jnp.arange(TOK_TILE) % dec_seq)])
    ang = pos.astype(F32)[:, None] * inv[None, :]
    cos = jnp.cos(ang)
    sin = jnp.sin(ang)
    zero = jnp.zeros_like(sin)
    reps = LANES // HEAD_DIM
    cos_t = jnp.tile(jnp.concatenate([cos, cos], -1), (1, reps))
    sa_t = jnp.tile(jnp.concatenate([-sin, zero], -1), (1, reps))
    sb_t = jnp.tile(jnp.concatenate([zero, sin], -1), (1, reps))
    return cos_t, sa_t, sb_t


def _inclusive_cumsum(v):
    n = v.shape[0]
    tri = jnp.arange(n)[None, :] <= jnp.arange(n)[:, None]
    return jnp.sum(jnp.where(tri, v[None, :], 0), axis=1)


def _dispatch_plan(ridx, counts, n_tiles):
    t_all = ridx.shape[1]
    tiles_e = (counts + EXPERT_TILE - 1) // EXPERT_TILE
    tile_end = _inclusive_cumsum(tiles_e)
    offs = (tile_end - tiles_e) * EXPERT_TILE
    n_used = tile_end[-1:]
    tile_expert = jnp.minimum(
        jnp.sum((jnp.arange(n_tiles)[:, None] >= tile_end[None, :]).astype(jnp.int32), axis=1), N_EXPERTS - 1)
    experts = jnp.arange(N_EXPERTS)
    used = tiles_e > 0
    slot_e = (_inclusive_cumsum(used.astype(jnp.int32)) - 1) % 2
    next_e = jnp.min(jnp.where((experts[None, :] > experts[:, None]) & used[None, :], experts[None, :], N_EXPERTS),
                     axis=1)
    tiles = jnp.arange(n_tiles)
    of_tile = tile_expert[:, None] == experts[None, :]
    pick = lambda v: jnp.sum(jnp.where(of_tile, v[None, :], 0), axis=1)
    first = (tiles == pick(tile_end - tiles_e)) & (tiles < n_used)
    tile_plan = first.astype(jnp.int32) + 2 * pick(slot_e) + 4 * pick(next_e)
    pos = ridx[TOP_K:2 * TOP_K] + jnp.sum(
        jnp.where(ridx[:TOP_K, :, None] == experts[None, None, :], offs[None, None, :], 0), axis=-1)
    gap_start = jnp.concatenate([offs + counts, n_used * EXPERT_TILE])
    gap_len = jnp.concatenate([tiles_e * EXPERT_TILE - counts, (n_tiles - n_used) * EXPERT_TILE])
    gap_end_q = _inclusive_cumsum(gap_len)
    q = jnp.arange(n_tiles * EXPERT_TILE - TOP_K * t_all)
    gap = jnp.sum((q[:, None] >= gap_end_q[None, :]).astype(jnp.int32), axis=1)
    in_gap = gap[:, None] == jnp.arange(N_EXPERTS + 1)[None, :]
    pad_rows = q + jnp.sum(jnp.where(in_gap, (gap_start - gap_end_q + gap_len)[None, :], 0), axis=1)
    return (pos.reshape(-1).astype(jnp.int32), pad_rows.astype(jnp.int32),
            tile_expert.astype(jnp.int32), n_used.astype(jnp.int32), tile_plan.astype(jnp.int32))


def kernel(x_prompt, x_sample, cache_k, cache_v, w_in, sinks, gm_ln_g, gm_ln_b, gm_ws, gm_bs,
           out_norm_a, out_norm_b, w_o, ln1_g, ln1_b, w_router, router_bias,
           w_gate, w_up, w_down, ln2_g, ln2_b):
    batch, seq, _ = x_prompt.shape
    dec_batch, dec_seq, _ = x_sample.shape
    depth = w_in.shape[0]
    past_len = PAST_LEN
    assert dec_batch * dec_seq == TOK_TILE and dec_seq == CHUNK and seq % TOK_TILE == 0
    assert cache_k.shape[2] == WINDOW
    alpha = (2 * depth) ** 0.25
    tiles_per_seq = seq // TOK_TILE

    tables = _rope_tables(seq, dec_seq, past_len)
    wrt = w_router.T
    rb = router_bias.reshape(N_EXPERTS, 1)
    cache_k2 = cache_k.reshape(depth * dec_batch, WINDOW, KV_WIDTH)
    cache_v2 = cache_v.reshape(depth * dec_batch, WINDOW, KV_WIDTH)
    as_rows = lambda p: p.reshape(depth, 1, -1)
    lng3, lnb3, na3, nb3 = as_rows(gm_ln_g), as_rows(gm_ln_b), as_rows(out_norm_a), as_rows(out_norm_b)
    l1g3, l1b3, l2g3, l2b3 = as_rows(ln1_g), as_rows(ln1_b), as_rows(ln2_g), as_rows(ln2_b)
    bst3 = jnp.swapaxes(gm_bs, 1, 2)

    streams = [
        dict(x=(x_prompt.reshape(batch * seq, D_MODEL), x_sample.reshape(-1, D_MODEL)),
             n_seq=batch, has_sample=True),
    ]
    kp, vp, ks, vs, gms = [], [], [], [], []
    for l in range(depth):
        layer_args = (sinks[l], w_in, cache_k2, cache_v2, lng3, lnb3, gm_ws, bst3, na3, nb3, w_o, l1g3, l1b3, wrt, rb)
        k_tails, v_tails = [], []
        for st in streams:
            n_prompt = st["n_seq"] * seq
            n_prompt_tiles = n_prompt // TOK_TILE
            t_all = n_prompt + (TOK_TILE if st["has_sample"] else 0)
            n_exp_tiles = (TOP_K * t_all) // EXPERT_TILE + N_EXPERTS
            res = _mixer_call(layer_args, st["x"], tables, n_prompt_tiles, tiles_per_seq, alpha, l)
            if st["has_sample"]:
                x1t, x1p, k_tail, v_tail, k_s, v_s, vn_s, ridx, rw, counts = res
            else:
                x1t, x1p, k_tail, v_tail, ridx, rw, counts = res

            pos, pad_rows, tile_expert, n_used, tile_plan = _dispatch_plan(ridx, counts[:, 0], n_exp_tiles)
            xs = _sc_dispatch_call(pos, pad_rows, x1p, PACK_ROWS, n_exp_tiles * EXPERT_TILE)
            outs = _expert_call(tile_expert, n_used, tile_plan, xs, w_gate, w_up, w_down, l)
            y_pair = _sc_gather_call(pos, outs, PACK_ROWS)
            if l < depth - 1:
                st["x"] = (x1t, y_pair, rw, l2g3, l2b3)
            else:
                st["x"] = _combine_dense_call(x1t, y_pair, rw, l2g3[l], l2b3[l], alpha,
                                              n_main_tiles=n_prompt_tiles if st["has_sample"] else None)

            k_tails.append(k_tail)
            v_tails.append(v_tail)
            if st["has_sample"]:
                ks.append(k_s)
                vs.append(v_s)
                gms.append(vn_s)
        kp.append(jnp.concatenate(k_tails, axis=0))
        vp.append(jnp.concatenate(v_tails, axis=0))

    y_prompt = jnp.concatenate([st["x"][0] if st["has_sample"] else st["x"] for st in streams],
                               axis=0).reshape(batch, seq, D_MODEL)
    y_sample = streams[-1]["x"][1].reshape(dec_batch, dec_seq, D_MODEL)
    heads = (N_KV_HEADS, HEAD_DIM)
    return (y_prompt, y_sample,
            jnp.stack(kp).reshape(depth, batch, WINDOW, *heads), jnp.stack(vp).reshape(depth, batch, WINDOW, *heads),
            jnp.stack(ks).reshape(depth, dec_batch, dec_seq, *heads),
            jnp.stack(vs).reshape(depth, dec_batch, dec_seq, *heads),
            jnp.stack(gms).reshape(depth, dec_batch, dec_seq, GM_WIDTH))
```

```python
import functools

import jax
import jax.numpy as jnp
from jax import lax
from jax.experimental import pallas as pl
from jax.experimental.pallas import tpu as pltpu
from jax.experimental.pallas import tpu_sc as plsc

D_MODEL = 1024
CHUNK = 64
WINDOW = 128
HEAD_DIM = 64
N_Q_HEADS = 8
N_KV_HEADS = 2
Q_PER_KV = N_Q_HEADS // N_KV_HEADS
PAST_LEN = 2048
ATT_WIDTH = N_Q_HEADS * HEAD_DIM
KV_WIDTH = N_KV_HEADS * HEAD_DIM
ROPE_THETA = 10000.0
GM_GROUPS = 4
GM_CH = 128
GM_WIDTH = GM_GROUPS * GM_CH
GM_CHUNK = 128
D_MIX = ATT_WIDTH + GM_WIDTH
D_IN = ATT_WIDTH + 2 * KV_WIDTH + 2 * GM_WIDTH
N_EXPERTS = 16
N_EXPERT_GROUPS = 4
EXPERTS_PER_GROUP = N_EXPERTS // N_EXPERT_GROUPS
TOP_K = 2
D_EXPERT = 512
LN_EPS = 1e-5
NEG_INF = -1e30

LANES = 128
PACK_ROWS = D_MODEL // (2 * LANES)
HI_HALF = 0xFFFF0000
TOK_TILE = 512
EXPERT_TILE = 512
KEYS_PER_CHUNK = WINDOW + CHUNK
VMEM_LIMIT = 60 * 1024 * 1024

F32 = jnp.float32
BF16 = jnp.bfloat16


def _layer_norm(x, g, b):
    mu = jnp.mean(x, axis=-1, keepdims=True)
    d = x - mu
    var = jnp.mean(d * d, axis=-1, keepdims=True)
    return d * lax.rsqrt(var + LN_EPS) * g + b


def _rms_norm(x, g):
    return x * lax.rsqrt(jnp.mean(x * x, axis=-1, keepdims=True) + LN_EPS) * g


def _gelu(x):
    return 0.5 * x * (1.0 + lax.erf(x * (0.5 ** 0.5)))


def _bf16_bits(x):
    return lax.bitcast_convert_type(x.astype(BF16).astype(F32), jnp.uint32)


def _store_packed_tiles(ref, x):
    n = x.shape[0]
    for s in range(PACK_ROWS):
        lo = _bf16_bits(x[:, (2 * s) * LANES:(2 * s + 1) * LANES]) >> 16
        hi = _bf16_bits(x[:, (2 * s + 1) * LANES:(2 * s + 2) * LANES]) & jnp.uint32(HI_HALF)
        ref[pl.ds(s, n, stride=PACK_ROWS), :] = hi | lo


def _load_packed_tiles(ref, n, dtype):
    cols = []
    for s in range(PACK_ROWS):
        w = ref[pl.ds(s, n, stride=PACK_ROWS), :]
        cols.append(lax.bitcast_convert_type(w << 16, F32).astype(dtype))
        cols.append(lax.bitcast_convert_type(w & jnp.uint32(HI_HALF), F32).astype(dtype))
    return jnp.concatenate(cols, axis=-1)


def _first_index_of_max(rows):
    m = rows[0]
    for r in rows[1:]:
        m = jnp.maximum(m, r)
    idx = jnp.full(m.shape, len(rows), jnp.int32)
    for e in reversed(range(len(rows))):
        idx = jnp.where(rows[e] == m, e, idx)
    return m, idx


def _mixer_kernel(n_prompt_tiles, tiles_per_seq, alpha, has_sample, first_layer, sinks_ref, *refs):
    n_x = 2 if first_layer else 6
    x_refs, refs = refs[:n_x], refs[n_x:]
    (w_in_ref, cos_ref, sa_ref, sb_ref, ck_ref, cv_ref, lng_ref, lnb_ref, ws_ref, bst_ref, na_ref, nb_ref, w_o_ref,
     l1g_ref, l1b_ref, wrt_ref, rb_ref, tri_ref, x1t_ref, x1p_ref, kt_ref, vt_ref) = refs[:22]
    rest = refs[22:]
    if has_sample:
        (ks_ref, vs_ref, vn_ref, ridx_ref, rw_ref, counts_ref,
         kd_ref, vd_ref, qm_ref, att_ref, gm_ref, cnt_ref, w_in_s, w_o_s, x_s) = rest
    else:
        (ridx_ref, rw_ref, counts_ref,
         kd_ref, vd_ref, qm_ref, att_ref, gm_ref, cnt_ref, w_in_s, w_o_s, x_s) = rest
    i = pl.program_id(0)
    is_sample = i >= n_prompt_tiles
    seq_start = jnp.logical_and(jnp.logical_not(is_sample), (i % tiles_per_seq) == 0)
    seq_end = jnp.logical_and(jnp.logical_not(is_sample), (i % tiles_per_seq) == tiles_per_seq - 1)
    tm = TOK_TILE

    @pl.when(i == 0)
    def _():
        w_in_s[...] = w_in_ref[0].astype(BF16)
        w_o_s[...] = w_o_ref[0].astype(BF16)

    if first_layer:
        xp_ref, xs_ref = x_refs
        x_s[...] = jnp.where(is_sample, xs_ref[...], xp_ref[...])
    else:
        x1_prev_ref, y0_ref, y1_ref, wc_ref, l2g_ref, l2b_ref = x_refs
        wc = wc_ref[...].T
        y_moe = (wc[:, 0:1] * _load_packed_tiles(y0_ref, tm, F32)
                 + wc[:, 1:2] * _load_packed_tiles(y1_ref, tm, F32))
        x_s[...] = _layer_norm(alpha * x1_prev_ref[...] + y_moe, l2g_ref[...], l2b_ref[...])

    h = jnp.dot(x_s[...].astype(BF16), w_in_s[...], preferred_element_type=F32)

    cos = cos_ref[...]
    sa = sa_ref[...]
    sb = sb_ref[...]

    def rope(blk):
        return blk * cos + pltpu.roll(blk, LANES - HEAD_DIM // 2, 1) * sa + pltpu.roll(blk, HEAD_DIM // 2, 1) * sb

    lane = lax.broadcasted_iota(jnp.int32, (1, LANES), 1)
    lo_half = lane < HEAD_DIM

    o_k = ATT_WIDTH
    o_v = o_k + KV_WIDTH
    o_u = o_v + KV_WIDTH
    o_g = o_u + GM_WIDTH

    k_rot = rope(h[:, o_k:o_v])
    v_new = h[:, o_v:o_u]

    @pl.when(seq_end)
    def _():
        kt_ref[...] = k_rot[tm - WINDOW:]
        vt_ref[...] = v_new[tm - WINDOW:]

    if has_sample:
        @pl.when(is_sample)
        def _():
            ks_ref[...] = k_rot
            vs_ref[...] = v_new

    def dup_heads(a):
        sw = pltpu.roll(a, HEAD_DIM, 1)
        return jnp.where(lo_half, a, sw).astype(BF16), jnp.where(lo_half, sw, a).astype(BF16)

    k_d = dup_heads(k_rot)
    v_d = dup_heads(v_new)

    @pl.when(jnp.logical_not(is_sample))
    def _():
        @pl.when(seq_start)
        def _():
            for g in range(N_KV_HEADS):
                kd_ref[g, 0:WINDOW, :] = jnp.zeros((WINDOW, LANES), BF16)
                vd_ref[g, 0:WINDOW, :] = jnp.zeros((WINDOW, LANES), BF16)

        @pl.when(jnp.logical_not(seq_start))
        def _():
            for g in range(N_KV_HEADS):
                kd_ref[g, 0:WINDOW, :] = kd_ref[g, tm:tm + WINDOW, :]
                vd_ref[g, 0:WINDOW, :] = vd_ref[g, tm:tm + WINDOW, :]

        for g in range(N_KV_HEADS):
            kd_ref[g, WINDOW:WINDOW + tm, :] = k_d[g]
            vd_ref[g, WINDOW:WINDOW + tm, :] = v_d[g]

    @pl.when(is_sample)
    def _():
        for b in range(tm // CHUNK):
            ck = dup_heads(ck_ref[b])
            cv = dup_heads(cv_ref[b])
            base = b * KEYS_PER_CHUNK
            for g in range(N_KV_HEADS):
                kd_ref[g, base:base + WINDOW, :] = ck[g]
                vd_ref[g, base:base + WINDOW, :] = cv[g]
                kd_ref[g, base + WINDOW:base + KEYS_PER_CHUNK, :] = k_d[g][b * CHUNK:(b + 1) * CHUNK]
                vd_ref[g, base + WINDOW:base + KEYS_PER_CHUNK, :] = v_d[g][b * CHUNK:(b + 1) * CHUNK]

    scale = HEAD_DIM ** -0.5
    for b in range(ATT_WIDTH // LANES):
        qb = (rope(h[:, b * LANES:(b + 1) * LANES]) * scale).astype(BF16)
        zero = jnp.zeros_like(qb)
        qm_ref[2 * b] = jnp.where(lo_half, qb, zero)
        qm_ref[2 * b + 1] = jnp.where(lo_half, zero, qb)

    key_stride = jnp.where(is_sample, KEYS_PER_CHUNK, CHUNK)
    col = lax.broadcasted_iota(jnp.int32, (1, KEYS_PER_CHUNK), 1)
    row_blk = lax.broadcasted_iota(jnp.int32, (Q_PER_KV * CHUNK, 1), 0) // CHUNK
    sink_cols = [
        jnp.where(row_blk == 0, sinks_ref[Q_PER_KV * g],
                  jnp.where(row_blk == 1, sinks_ref[Q_PER_KV * g + 1],
                            jnp.where(row_blk == 2, sinks_ref[Q_PER_KV * g + 2], sinks_ref[Q_PER_KV * g + 3])))
        for g in range(N_KV_HEADS)]

    for c in range(tm // CHUNK):
        r0 = c * CHUNK
        k0 = pl.multiple_of(c * key_stride, CHUNK)
        outs = []
        for g in range(N_KV_HEADS):
            q_st = jnp.concatenate([qm_ref[Q_PER_KV * g + r, pl.ds(r0, CHUNK), :] for r in range(Q_PER_KV)],
                                   axis=0)
            keys = kd_ref[g, pl.ds(k0, KEYS_PER_CHUNK), :]
            vals = vd_ref[g, pl.ds(k0, KEYS_PER_CHUNK), :]
            s = lax.dot_general(q_st, keys, (((1,), (1,)), ((), ())), preferred_element_type=F32)
            if c < WINDOW // CHUNK:
                first_valid = jnp.where(seq_start, (WINDOW // CHUNK - c) * CHUNK, 0)
                s = jnp.where(col >= first_valid, s, NEG_INF)
            sink = sink_cols[g]
            m = jnp.maximum(jnp.max(s, axis=-1, keepdims=True), sink)
            p = jnp.exp(s - m)
            denom = jnp.sum(p, axis=-1, keepdims=True) + jnp.exp(sink - m)
            o = jnp.dot(p.astype(BF16), vals, preferred_element_type=F32) * (1.0 / denom)
            for bb in range(2):
                outs.append(jnp.where(lo_half, o[(2 * bb) * CHUNK:(2 * bb + 1) * CHUNK],
                                      o[(2 * bb + 1) * CHUNK:(2 * bb + 2) * CHUNK]))
        for b in range(ATT_WIDTH // LANES):
            att_ref[pl.ds(r0, CHUNK), b * LANES:(b + 1) * LANES] = outs[b]

    prow = lax.broadcasted_iota(jnp.int32, (GM_CHUNK, GM_CHUNK), 0)
    pcol = lax.broadcasted_iota(jnp.int32, (GM_CHUNK, GM_CHUNK), 1)
    half = GM_CHUNK // 2
    tril = pcol <= prow
    same_blk = (prow < half) == (pcol < half)
    prow1 = lax.broadcasted_iota(jnp.int32, (GM_CHUNK, 1), 0)
    bst = bst_ref[...]
    bst_s = jnp.where(prow1 < half, bst, pltpu.roll(bst, half, 0))
    bias = jnp.where(is_sample, bst_s, bst)
    for g in range(GM_GROUPS):
        wg_ = ws_ref[g]
        w_s = jnp.where(prow < half, wg_, pltpu.roll(pltpu.roll(wg_, half, 0), half, 1))
        w_eff = jnp.where(is_sample, jnp.where(same_blk, w_s, 0.0), wg_)
        w_eff = jnp.where(tril, w_eff, 0.0).astype(BF16)
        cols = slice(g * GM_CH, (g + 1) * GM_CH)
        ug = _gelu(h[:, o_u + g * GM_CH:o_u + (g + 1) * GM_CH])
        vn = _layer_norm(_gelu(h[:, o_g + g * GM_CH:o_g + (g + 1) * GM_CH]), lng_ref[:, cols], lnb_ref[:, cols])

        if has_sample:
            @pl.when(is_sample)
            def _(cols=cols, vn=vn):
                vn_ref[:, cols] = vn

        vn_b = vn.astype(BF16)
        for n in range(tm // GM_CHUNK):
            rows = slice(n * GM_CHUNK, (n + 1) * GM_CHUNK)
            s = jnp.dot(w_eff, vn_b[rows], preferred_element_type=F32) + bias[:, g:g + 1]
            gm_ref[rows, cols] = ug[rows] * s

    mixed = jnp.concatenate([_rms_norm(att_ref[...], na_ref[...]), _rms_norm(gm_ref[...], nb_ref[...])], axis=-1)
    y = alpha * x_s[...] + jnp.dot(mixed.astype(BF16), w_o_s[...], preferred_element_type=F32)
    x1 = _layer_norm(y, l1g_ref[...], l1b_ref[...])
    x1t_ref[...] = x1
    _store_packed_tiles(x1p_ref, x1)

    nt = (((1,), (1,)), ((), ()))
    wr = wrt_ref[...]
    wr_hi = wr.astype(BF16)
    wr_lo = (wr - wr_hi.astype(F32)).astype(BF16)
    x1_hi = x1.astype(BF16)
    x1_lo = (x1 - x1_hi.astype(F32)).astype(BF16)
    logits = (lax.dot_general(wr_hi, x1_hi, nt, preferred_element_type=F32)
              + (lax.dot_general(wr_hi, x1_lo, nt, preferred_element_type=F32)
                 + lax.dot_general(wr_lo, x1_hi, nt, preferred_element_type=F32)))
    scores = jax.nn.sigmoid(logits)
    sel = scores + rb_ref[...]
    sel_rows = [sel[e:e + 1, :] for e in range(N_EXPERTS)]
    sc_rows = [scores[e:e + 1, :] for e in range(N_EXPERTS)]
    grp = []
    for g in range(N_EXPERT_GROUPS):
        r = sel_rows[g * EXPERTS_PER_GROUP:(g + 1) * EXPERTS_PER_GROUP]
        best_pair = None
        for a in range(EXPERTS_PER_GROUP):
            for b in range(a + 1, EXPERTS_PER_GROUP):
                pair = r[a] + r[b]
                best_pair = pair if best_pair is None else jnp.maximum(best_pair, pair)
        grp.append(best_pair)
    _, best = _first_index_of_max(grp)
    masked = [jnp.where(best == (e // EXPERTS_PER_GROUP), sel_rows[e], NEG_INF) for e in range(N_EXPERTS)]
    _, e0 = _first_index_of_max(masked)
    masked2 = [jnp.where(e0 == e, -jnp.inf, masked[e]) for e in range(N_EXPERTS)]
    _, e1 = _first_index_of_max(masked2)
    w0 = jnp.zeros_like(sc_rows[0])
    w1 = jnp.zeros_like(sc_rows[0])
    for e in range(N_EXPERTS):
        w0 = jnp.where(e0 == e, sc_rows[e], w0)
        w1 = jnp.where(e1 == e, sc_rows[e], w1)
    wsum = w0 + w1
    rw_ref[...] = jnp.concatenate([w0 / wsum, w1 / wsum, jnp.zeros((6, tm), F32)], axis=0)

    @pl.when(i == 0)
    def _():
        cnt_ref[...] = jnp.zeros_like(cnt_ref)

    eid = lax.broadcasted_iota(jnp.int32, (N_EXPERTS, tm), 0)
    oh0 = (eid == e0).astype(F32)
    oh1 = (eid == e1).astype(F32)
    oh = oh0 + oh1
    incl = jnp.dot(oh.astype(BF16), tri_ref[...], preferred_element_type=F32)
    before = cnt_ref[:, 0:1] + incl - oh
    r0 = jnp.sum(oh0 * before, axis=0, keepdims=True).astype(jnp.int32)
    r1 = jnp.sum(oh1 * before, axis=0, keepdims=True).astype(jnp.int32)
    cnt = cnt_ref[...] + incl[:, tm - 1:tm]
    cnt_ref[...] = cnt
    counts_ref[...] = cnt.astype(jnp.int32)
    ridx_ref[...] = jnp.concatenate([e0, e1, r0, r1, jnp.zeros((4, tm), jnp.int32)], axis=0)


def _mixer_call(layer_args, x_all, tables, n_prompt_tiles, tiles_per_seq, alpha, layer):
    (sinks, w_in, ck, cv, lng, lnb, ws, bst, na, nb, w_o, l1g, l1b, wrt, rb) = layer_args
    cos_t, sa_t, sb_t = tables

    def of_layer(shape, which=layer):
        nd = len(shape)
        return pl.BlockSpec((None,) + shape, lambda i, _nd=nd: (which,) + (0,) * _nd)

    first_layer = len(x_all) == 2
    if first_layer:
        x_main, x_tail = x_all
        t_all = x_main.shape[0] + x_tail.shape[0]
        n_tiles = t_all // TOK_TILE
        x_args = [x_main, x_tail]
        x_specs = [pl.BlockSpec((TOK_TILE, D_MODEL), lambda i: (jnp.minimum(i, n_prompt_tiles - 1), 0)),
                   pl.BlockSpec((TOK_TILE, D_MODEL), lambda i: (0, 0))]
    else:
        x1t_prev, y_pair, w_col, l2g, l2b = x_all
        t_all = x1t_prev.shape[0]
        n_tiles = t_all // TOK_TILE
        x_args = [x1t_prev, y_pair, y_pair, w_col, l2g, l2b]
        x_specs = [pl.BlockSpec((TOK_TILE, D_MODEL), lambda i: (i, 0)),
                   pl.BlockSpec((TOK_TILE * PACK_ROWS, LANES), lambda i: (i, 0)),
                   pl.BlockSpec((TOK_TILE * PACK_ROWS, LANES), lambda i: (i + n_tiles, 0)),
                   pl.BlockSpec((8, TOK_TILE), lambda i: (0, i)),
                   of_layer((1, D_MODEL), layer - 1), of_layer((1, D_MODEL), layer - 1)]
    n_seq_tiles = tiles_per_seq
    has_sample = n_tiles > n_prompt_tiles

    def const(shape):
        nd = len(shape)
        return pl.BlockSpec(shape, lambda i, _nd=nd: (0,) * _nd)

    def tab_map(i):
        return (jnp.where(i < n_prompt_tiles, i % n_seq_tiles, n_seq_tiles), 0)

    n_seq = n_prompt_tiles // n_seq_tiles

    def seq_map(i):
        return (jnp.minimum(i // n_seq_tiles, n_seq - 1), 0)

    row_blk = lambda w: pl.BlockSpec((TOK_TILE, w), lambda i: (i, 0))
    in_specs = [
        pl.BlockSpec(memory_space=pltpu.SMEM),
        *x_specs,
        pl.BlockSpec((1, D_MODEL, D_IN), lambda i: (layer, 0, 0)),
        pl.BlockSpec((TOK_TILE, LANES), tab_map),
        pl.BlockSpec((TOK_TILE, LANES), tab_map),
        pl.BlockSpec((TOK_TILE, LANES), tab_map),
        pl.BlockSpec((TOK_TILE // CHUNK, WINDOW, KV_WIDTH), lambda i: (layer, 0, 0)),
        pl.BlockSpec((TOK_TILE // CHUNK, WINDOW, KV_WIDTH), lambda i: (layer, 0, 0)),
        of_layer((1, GM_WIDTH)), of_layer((1, GM_WIDTH)),
        of_layer((GM_GROUPS, GM_CHUNK, GM_CHUNK)), of_layer((GM_CHUNK, GM_GROUPS)),
        of_layer((1, ATT_WIDTH)), of_layer((1, GM_WIDTH)),
        pl.BlockSpec((1, D_MIX, D_MODEL), lambda i: (layer, 0, 0)),
        of_layer((1, D_MODEL)), of_layer((1, D_MODEL)),
        const((N_EXPERTS, D_MODEL)), const((N_EXPERTS, 1)),
        const((TOK_TILE, TOK_TILE)),
    ]
    out_shape = [
        jax.ShapeDtypeStruct((t_all, D_MODEL), F32),
        jax.ShapeDtypeStruct((t_all * PACK_ROWS, LANES), jnp.uint32),
        jax.ShapeDtypeStruct((n_seq * WINDOW, KV_WIDTH), F32),
        jax.ShapeDtypeStruct((n_seq * WINDOW, KV_WIDTH), F32),
        jax.ShapeDtypeStruct((TOK_TILE, KV_WIDTH), F32),
        jax.ShapeDtypeStruct((TOK_TILE, KV_WIDTH), F32),
        jax.ShapeDtypeStruct((TOK_TILE, GM_WIDTH), F32),
        jax.ShapeDtypeStruct((8, t_all), jnp.int32),
        jax.ShapeDtypeStruct((8, t_all), F32),
        jax.ShapeDtypeStruct((N_EXPERTS, LANES), jnp.int32),
    ]
    out_specs = [
        pl.BlockSpec((TOK_TILE, D_MODEL), lambda i: (i, 0)),
        pl.BlockSpec((TOK_TILE * PACK_ROWS, LANES), lambda i: (i, 0)),
        pl.BlockSpec((WINDOW, KV_WIDTH), seq_map), pl.BlockSpec((WINDOW, KV_WIDTH), seq_map),
        const((TOK_TILE, KV_WIDTH)), const((TOK_TILE, KV_WIDTH)),
        const((TOK_TILE, GM_WIDTH)),
        pl.BlockSpec((8, TOK_TILE), lambda i: (0, i)),
        pl.BlockSpec((8, TOK_TILE), lambda i: (0, i)),
        const((N_EXPERTS, LANES)),
    ]
    if not has_sample:
        del out_shape[4:7], out_specs[4:7]
    kd_rows = (TOK_TILE // CHUNK) * KEYS_PER_CHUNK
    scratch = [
        pltpu.VMEM((N_KV_HEADS, kd_rows, LANES), BF16),
        pltpu.VMEM((N_KV_HEADS, kd_rows, LANES), BF16),
        pltpu.VMEM((N_Q_HEADS, TOK_TILE, LANES), BF16),
        pltpu.VMEM((TOK_TILE, ATT_WIDTH), F32),
        pltpu.VMEM((TOK_TILE, GM_WIDTH), F32),
        pltpu.VMEM((N_EXPERTS, LANES), F32),
        pltpu.VMEM((D_MODEL, D_IN), BF16),
        pltpu.VMEM((D_MIX, D_MODEL), BF16),
        pltpu.VMEM((TOK_TILE, D_MODEL), F32),
    ]
    return pl.pallas_call(
        functools.partial(_mixer_kernel, n_prompt_tiles, tiles_per_seq, alpha, has_sample, first_layer),
        grid=(n_tiles,),
        in_specs=in_specs, out_specs=out_specs, out_shape=out_shape,
        scratch_shapes=scratch,
        compiler_params=pltpu.CompilerParams(dimension_semantics=("arbitrary",), vmem_limit_bytes=VMEM_LIMIT),
        name="mixer",
    )(sinks, *x_args, w_in, cos_t, sa_t, sb_t, ck, cv, lng, lnb, ws, bst, na, nb, w_o, l1g, l1b, wrt, rb,
      jnp.triu(jnp.ones((TOK_TILE, TOK_TILE), BF16)))


def _expert_kernel(layer, te_ref, nt_ref, plan_ref, xs_ref, wg_hbm, wu_hbm, wd_hbm, out_ref,
                   wg_s, wu_s, wd_s, wg_f, wu_f, wd_f, sems):
    i = pl.program_id(0)
    plan = plan_ref[i]
    first = (plan & 1) == 1
    slot = (plan >> 1) & 1
    nxt = plan >> 2

    def weight_copies(e, s):
        return (pltpu.make_async_copy(wg_hbm.at[layer, e], wg_f.at[s], sems.at[s, 0]),
                pltpu.make_async_copy(wu_hbm.at[layer, e], wu_f.at[s], sems.at[s, 1]),
                pltpu.make_async_copy(wd_hbm.at[layer, e], wd_f.at[s], sems.at[s, 2]))

    @pl.when(i == 0)
    def _():
        for c in weight_copies(te_ref[0], slot):
            c.start()

    @pl.when(first)
    def _():
        for c in weight_copies(te_ref[i], slot):
            c.wait()

        @pl.when(nxt < N_EXPERTS)
        def _():
            for c in weight_copies(nxt, 1 - slot):
                c.start()

        wg_s[...] = wg_f[slot].astype(BF16)
        wu_s[...] = wu_f[slot].astype(BF16)
        wd_s[...] = wd_f[slot].astype(BF16)

    @pl.when(i < nt_ref[0])
    def _():
        xs = _load_packed_tiles(xs_ref, EXPERT_TILE, BF16)
        g = jnp.dot(xs, wg_s[...], preferred_element_type=F32)
        u = jnp.dot(xs, wu_s[...], preferred_element_type=F32)
        hmid = (g * jax.nn.sigmoid(g)) * u
        _store_packed_tiles(out_ref, jnp.dot(hmid.astype(BF16), wd_s[...], preferred_element_type=F32))


def _expert_call(tile_expert, n_used, tile_plan, xs, wg, wu, wd, layer):
    n_tiles = xs.shape[0] // (EXPERT_TILE * PACK_ROWS)
    tile_blk = pl.BlockSpec((EXPERT_TILE * PACK_ROWS, LANES),
                            lambda i, te, nt, plan: (jnp.minimum(i, nt[0] - 1), 0))
    grid_spec = pltpu.PrefetchScalarGridSpec(
        num_scalar_prefetch=3,
        grid=(n_tiles,),
        in_specs=[tile_blk] + [pl.BlockSpec(memory_space=pl.ANY)] * 3,
        out_specs=tile_blk,
        scratch_shapes=[
            pltpu.VMEM((D_MODEL, D_EXPERT), BF16),
            pltpu.VMEM((D_MODEL, D_EXPERT), BF16),
            pltpu.VMEM((D_EXPERT, D_MODEL), BF16),
            pltpu.VMEM((2, D_MODEL, D_EXPERT), F32),
            pltpu.VMEM((2, D_MODEL, D_EXPERT), F32),
            pltpu.VMEM((2, D_EXPERT, D_MODEL), F32),
            pltpu.SemaphoreType.DMA((2, 3)),
        ],
    )
    return pl.pallas_call(
        functools.partial(_expert_kernel, layer),
        grid_spec=grid_spec,
        out_shape=jax.ShapeDtypeStruct(xs.shape, jnp.uint32),
        compiler_params=pltpu.CompilerParams(dimension_semantics=("arbitrary",), vmem_limit_bytes=VMEM_LIMIT),
        name="experts",
    )(tile_expert, n_used, tile_plan, xs, wg, wu, wd)


SC_CHUNK = 128
SC_PAD_CHUNK = 64


def _sc_workers():
    info = plsc.get_sparse_core_info()
    return info.num_cores, info.num_cores * info.num_subcores


def _sc_split(per_w):
    n_full = per_w // SC_CHUNK
    tail = per_w - n_full * SC_CHUNK
    assert tail % 8 == 0
    return n_full, tail


def _sc_dispatch_call(pos_flat, pad_rows, x_tiles, tile_rows, n_sorted_rows):
    t_all = x_tiles.shape[0] // tile_rows
    dt = x_tiles.dtype
    nc, nw = _sc_workers()
    per_w = t_all // nw
    n_ch, tail = _sc_split(per_w)
    n_pad = pad_rows.shape[0]
    pad_per_w = n_pad // nw
    n_pch = pad_per_w // SC_PAD_CHUNK
    assert per_w * nw == t_all and n_pch * SC_PAD_CHUNK * nw == n_pad
    zeros = jnp.zeros((SC_PAD_CHUNK, tile_rows, LANES), dt)
    tail_scratch = [] if tail == 0 else [
        pltpu.VMEM((tail,), jnp.int32), pltpu.VMEM((tail,), jnp.int32), pltpu.VMEM((tail, tile_rows, LANES), dt)]

    @functools.partial(
        pl.kernel, mesh=plsc.VectorSubcoreMesh(core_axis_name="c", subcore_axis_name="s"),
        out_type=jax.ShapeDtypeStruct((n_sorted_rows, tile_rows, LANES), dt),
        scratch_types=[
            pltpu.VMEM((SC_CHUNK,), jnp.int32), pltpu.VMEM((SC_CHUNK,), jnp.int32),
            pltpu.VMEM((SC_CHUNK, tile_rows, LANES), dt),
            pltpu.VMEM((SC_PAD_CHUNK,), jnp.int32),
            pltpu.VMEM((SC_PAD_CHUNK, tile_rows, LANES), dt),
            pltpu.SemaphoreType.DMA, pltpu.SemaphoreType.DMA, pltpu.SemaphoreType.DMA,
        ] + tail_scratch,
    )
    def k(pos_hbm, pad_hbm, x_hbm, z_hbm, xs_hbm, i0_v, i1_v, rows_v, ip_v, z_v, sem0, sem1, sem2, *tail_refs):
        wid = lax.axis_index("s") * nc + lax.axis_index("c")

        def move(b, n, idx0, idx1, rows):
            loads = (pltpu.async_copy(pos_hbm.at[pl.ds(b, n)], idx0, sem0),
                     pltpu.async_copy(pos_hbm.at[pl.ds(t_all + b, n)], idx1, sem1),
                     pltpu.async_copy(x_hbm.at[pl.ds(b, n)], rows, sem2))
            for c in loads:
                c.wait()
            stores = (pltpu.async_copy(rows, xs_hbm.at[idx0], sem0), pltpu.async_copy(rows, xs_hbm.at[idx1], sem1))
            for c in stores:
                c.wait()

        @pl.loop(0, n_ch)
        def _(j):
            move(pl.multiple_of(wid * per_w + j * SC_CHUNK, 8), SC_CHUNK, i0_v, i1_v, rows_v)

        if tail:
            move(pl.multiple_of(wid * per_w + n_ch * SC_CHUNK, 8), tail, *tail_refs)

        pltpu.sync_copy(z_hbm, z_v)

        @pl.loop(0, n_pch)
        def _(j):
            b = pl.multiple_of(wid * pad_per_w + j * SC_PAD_CHUNK, 8)
            pltpu.sync_copy(pad_hbm.at[pl.ds(b, SC_PAD_CHUNK)], ip_v)
            pltpu.sync_copy(z_v, xs_hbm.at[ip_v])

    xs = k(pos_flat, pad_rows, x_tiles.reshape(t_all, tile_rows, LANES), zeros)
    return xs.reshape(n_sorted_rows * tile_rows, LANES)


def _sc_gather_call(pos_flat, outs, tile_rows):
    n_assign = pos_flat.shape[0]
    dt = outs.dtype
    nc, nw = _sc_workers()
    per_w = n_assign // nw
    n_ch, tail = _sc_split(per_w)
    assert per_w * nw == n_assign
    tail_scratch = [] if tail == 0 else [pltpu.VMEM((tail,), jnp.int32), pltpu.VMEM((tail, tile_rows, LANES), dt)]

    @functools.partial(
        pl.kernel, mesh=plsc.VectorSubcoreMesh(core_axis_name="c", subcore_axis_name="s"),
        out_type=jax.ShapeDtypeStruct((n_assign, tile_rows, LANES), dt),
        scratch_types=[pltpu.VMEM((SC_CHUNK,), jnp.int32),
                       pltpu.VMEM((SC_CHUNK, tile_rows, LANES), dt)] + tail_scratch,
    )
    def k(pos_hbm, o_hbm, y_hbm, i_v, rows_v, *tail_refs):
        wid = lax.axis_index("s") * nc + lax.axis_index("c")

        def move(b, n, idx, rows):
            pltpu.sync_copy(pos_hbm.at[pl.ds(b, n)], idx)
            pltpu.sync_copy(o_hbm.at[idx], rows)
            pltpu.sync_copy(rows, y_hbm.at[pl.ds(b, n)])

        @pl.loop(0, n_ch)
        def _(j):
            move(pl.multiple_of(wid * per_w + j * SC_CHUNK, 8), SC_CHUNK, i_v, rows_v)

        if tail:
            move(pl.multiple_of(wid * per_w + n_ch * SC_CHUNK, 8), tail, *tail_refs)

    y = k(pos_flat, outs.reshape(-1, tile_rows, LANES))
    return y.reshape(n_assign * tile_rows, LANES)


def _combine_dense_kernel(alpha, n_main_tiles, x1t_ref, y0_ref, y1_ref, w_ref, g_ref, b_ref, out_ref, *tail):
    j = pl.program_id(0)
    w = w_ref[...].T
    y = (w[:, 0:1] * _load_packed_tiles(y0_ref, TOK_TILE, F32)
         + w[:, 1:2] * _load_packed_tiles(y1_ref, TOK_TILE, F32))
    res = _layer_norm(alpha * x1t_ref[...] + y, g_ref[...], b_ref[...])
    if n_main_tiles is None:
        out_ref[...] = res
    else:
        @pl.when(j < n_main_tiles)
        def _():
            out_ref[...] = res

        @pl.when(j >= n_main_tiles)
        def _():
            tail[0][...] = res


def _combine_dense_call(x1t, y_pair, w_col, g, b, alpha, n_main_tiles=None):
    t_all = x1t.shape[0]
    n_tiles = t_all // TOK_TILE
    if n_main_tiles is None:
        out_specs = pl.BlockSpec((TOK_TILE, D_MODEL), lambda i: (i, 0))
        out_shape = jax.ShapeDtypeStruct((t_all, D_MODEL), F32)
    else:
        assert n_tiles == n_main_tiles + 1
        out_specs = [pl.BlockSpec((TOK_TILE, D_MODEL), lambda i: (jnp.minimum(i, n_main_tiles - 1), 0)),
                     pl.BlockSpec((TOK_TILE, D_MODEL), lambda i: (0, 0))]
        out_shape = [jax.ShapeDtypeStruct((n_main_tiles * TOK_TILE, D_MODEL), F32),
                     jax.ShapeDtypeStruct((TOK_TILE, D_MODEL), F32)]
    tile = (TOK_TILE, D_MODEL)
    packed = (TOK_TILE * PACK_ROWS, LANES)
    return pl.pallas_call(
        functools.partial(_combine_dense_kernel, alpha, n_main_tiles),
        grid=(n_tiles,),
        in_specs=[
            pl.BlockSpec(tile, lambda i: (i, 0)),
            pl.BlockSpec(packed, lambda i: (i, 0)),
            pl.BlockSpec(packed, lambda i: (i + n_tiles, 0)),
            pl.BlockSpec((8, TOK_TILE), lambda i: (0, i)),
            pl.BlockSpec((1, D_MODEL), lambda i: (0, 0)),
            pl.BlockSpec((1, D_MODEL), lambda i: (0, 0)),
        ],
        out_specs=out_specs,
        out_shape=out_shape,
        compiler_params=pltpu.CompilerParams(dimension_semantics=("arbitrary",), vmem_limit_bytes=VMEM_LIMIT),
        name="combine",
    )(x1t, y_pair, y_pair, w_col, g, b)


def _rope_tables(seq, dec_seq, past_len):
    half = HEAD_DIM // 2
    inv = ROPE_THETA ** (-jnp.arange(half, dtype=F32) / half)
    pos = jnp.concatenate([jnp.arange(seq), past_len + (jnp.arange(TOK_TILE) % dec_seq)])
    ang = pos.astype(F32)[:, None] * inv[None, :]
    cos = jnp.cos(ang)
    sin = jnp.sin(ang)
    zero = jnp.zeros_like(sin)
    reps = LANES // HEAD_DIM
    cos_t = jnp.tile(jnp.concatenate([cos, cos], -1), (1, reps))
    sa_t = jnp.tile(jnp.concatenate([-sin, zero], -1), (1, reps))
    sb_t = jnp.tile(jnp.concatenate([zero, sin], -1), (1, reps))
    return cos_t, sa_t, sb_t


def _inclusive_cumsum(v):
    n = v.shape[0]
    tri = jnp.arange(n)[None, :] <= jnp.arange(n)[:, None]
    return jnp.sum(jnp.where(tri, v[None, :], 0), axis=1)


def _dispatch_plan(ridx, counts, n_tiles):
    t_all = ridx.shape[1]
    tiles_e = (counts + EXPERT_TILE - 1) // EXPERT_TILE
    tile_end = _inclusive_cumsum(tiles_e)
    offs = (tile_end - tiles_e) * EXPERT_TILE
    n_used = tile_end[-1:]
    tile_expert = jnp.minimum(
        jnp.sum((jnp.arange(n_tiles)[:, None] >= tile_end[None, :]).astype(jnp.int32), axis=1), N_EXPERTS - 1)
    experts = jnp.arange(N_EXPERTS)
    used = tiles_e > 0
    slot_e = (_inclusive_cumsum(used.astype(jnp.int32)) - 1) % 2
    next_e = jnp.min(jnp.where((experts[None, :] > experts[:, None]) & used[None, :], experts[None, :], N_EXPERTS),
                     axis=1)
    tiles = jnp.arange(n_tiles)
    of_tile = tile_expert[:, None] == experts[None, :]
    pick = lambda v: jnp.sum(jnp.where(of_tile, v[None, :], 0), axis=1)
    first = (tiles == pick(tile_end - tiles_e)) & (tiles < n_used)
    tile_plan = first.astype(jnp.int32) + 2 * pick(slot_e) + 4 * pick(next_e)
    pos = ridx[TOP_K:2 * TOP_K] + jnp.sum(
        jnp.where(ridx[:TOP_K, :, None] == experts[None, None, :], offs[None, None, :], 0), axis=-1)
    gap_start = jnp.concatenate([offs + counts, n_used * EXPERT_TILE])
    gap_len = jnp.concatenate([tiles_e * EXPERT_TILE - counts, (n_tiles - n_used) * EXPERT_TILE])
    gap_end_q = _inclusive_cumsum(gap_len)
    q = jnp.arange(n_tiles * EXPERT_TILE - TOP_K * t_all)
    gap = jnp.sum((q[:, None] >= gap_end_q[None, :]).astype(jnp.int32), axis=1)
    in_gap = gap[:, None] == jnp.arange(N_EXPERTS + 1)[None, :]
    pad_rows = q + jnp.sum(jnp.where(in_gap, (gap_start - gap_end_q + gap_len)[None, :], 0), axis=1)
    return (pos.reshape(-1).astype(jnp.int32), pad_rows.astype(jnp.int32),
            tile_expert.astype(jnp.int32), n_used.astype(jnp.int32), tile_plan.astype(jnp.int32))


def kernel(x_prompt, x_sample, cache_k, cache_v, w_in, sinks, gm_ln_g, gm_ln_b, gm_ws, gm_bs,
           out_norm_a, out_norm_b, w_o, ln1_g, ln1_b, w_router, router_bias,
           w_gate, w_up, w_down, ln2_g, ln2_b):
    batch, seq, _ = x_prompt.shape
    dec_batch, dec_seq, _ = x_sample.shape
    depth = w_in.shape[0]
    past_len = PAST_LEN
    assert dec_batch * dec_seq == TOK_TILE and dec_seq == CHUNK and seq % TOK_TILE == 0
    assert cache_k.shape[2] == WINDOW
    alpha = (2 * depth) ** 0.25
    tiles_per_seq = seq // TOK_TILE

    tables = _rope_tables(seq, dec_seq, past_len)
    wrt = w_router.T
    rb = router_bias.reshape(N_EXPERTS, 1)
    cache_k2 = cache_k.reshape(depth * dec_batch, WINDOW, KV_WIDTH)
    cache_v2 = cache_v.reshape(depth * dec_batch, WINDOW, KV_WIDTH)
    as_rows = lambda p: p.reshape(depth, 1, -1)
    lng3, lnb3, na3, nb3 = as_rows(gm_ln_g), as_rows(gm_ln_b), as_rows(out_norm_a), as_rows(out_norm_b)
    l1g3, l1b3, l2g3, l2b3 = as_rows(ln1_g), as_rows(ln1_b), as_rows(ln2_g), as_rows(ln2_b)
    bst3 = jnp.swapaxes(gm_bs, 1, 2)

    streams = [
        dict(x=(x_prompt.reshape(batch * seq, D_MODEL), x_sample.reshape(-1, D_MODEL)),
             n_seq=batch, has_sample=True),
    ]
    kp, vp, ks, vs, gms = [], [], [], [], []
    for l in range(depth):
        layer_args = (sinks[l], w_in, cache_k2, cache_v2, lng3, lnb3, gm_ws, bst3, na3, nb3, w_o, l1g3, l1b3, wrt, rb)
        k_tails, v_tails = [], []
        for st in streams:
            n_prompt = st["n_seq"] * seq
            n_prompt_tiles = n_prompt // TOK_TILE
            t_all = n_prompt + (TOK_TILE if st["has_sample"] else 0)
            n_exp_tiles = (TOP_K * t_all) // EXPERT_TILE + N_EXPERTS
            res = _mixer_call(layer_args, st["x"], tables, n_prompt_tiles, tiles_per_seq, alpha, l)
            if st["has_sample"]:
                x1t, x1p, k_tail, v_tail, k_s, v_s, vn_s, ridx, rw, counts = res
            else:
                x1t, x1p, k_tail, v_tail, ridx, rw, counts = res

            pos, pad_rows, tile_expert, n_used, tile_plan = _dispatch_plan(ridx, counts[:, 0], n_exp_tiles)
            xs = _sc_dispatch_call(pos, pad_rows, x1p, PACK_ROWS, n_exp_tiles * EXPERT_TILE)
            outs = _expert_call(tile_expert, n_used, tile_plan, xs, w_gate, w_up, w_down, l)
            y_pair = _sc_gather_call(pos, outs, PACK_ROWS)
            if l < depth - 1:
                st["x"] = (x1t, y_pair, rw, l2g3, l2b3)
            else:
                st["x"] = _combine_dense_call(x1t, y_pair, rw, l2g3[l], l2b3[l], alpha,
                                              n_main_tiles=n_prompt_tiles if st["has_sample"] else None)

            k_tails.append(k_tail)
            v_tails.append(v_tail)
            if st["has_sample"]:
                ks.append(k_s)
                vs.append(v_s)
                gms.append(vn_s)
        kp.append(jnp.concatenate(k_tails, axis=0))
        vp.append(jnp.concatenate(v_tails, axis=0))

    y_prompt = jnp.concatenate([st["x"][0] if st["has_sample"] else st["x"] for st in streams],
                               axis=0).reshape(batch, seq, D_MODEL)
    y_sample = streams[-1]["x"][1].reshape(dec_batch, dec_seq, D_MODEL)
    heads = (N_KV_HEADS, HEAD_DIM)
    return (y_prompt, y_sample,
            jnp.stack(kp).reshape(depth, batch, WINDOW, *heads), jnp.stack(vp).reshape(depth, batch, WINDOW, *heads),
            jnp.stack(ks).reshape(depth, dec_batch, dec_seq, *heads),
            jnp.stack(vs).reshape(depth, dec_batch, dec_seq, *heads),
            jnp.stack(gms).reshape(depth, dec_batch, dec_seq, GM_WIDTH))
```

```python
import functools

import jax
import jax.numpy as jnp
from jax import lax
from jax.experimental import pallas as pl
from jax.experimental.pallas import tpu as pltpu
from jax.experimental.pallas import tpu_sc as plsc

D_MODEL = 1024
CHUNK = 64
WINDOW = 128
HEAD_DIM = 64
N_Q_HEADS = 8
N_KV_HEADS = 2
Q_PER_KV = N_Q_HEADS // N_KV_HEADS
PAST_LEN = 2048
ATT_WIDTH = N_Q_HEADS * HEAD_DIM
KV_WIDTH = N_KV_HEADS * HEAD_DIM
ROPE_THETA = 10000.0
GM_GROUPS = 4
GM_CH = 128
GM_WIDTH = GM_GROUPS * GM_CH
GM_CHUNK = 128
D_MIX = ATT_WIDTH + GM_WIDTH
D_IN = ATT_WIDTH + 2 * KV_WIDTH + 2 * GM_WIDTH
N_EXPERTS = 16
N_EXPERT_GROUPS = 4
EXPERTS_PER_GROUP = N_EXPERTS // N_EXPERT_GROUPS
TOP_K = 2
D_EXPERT = 512
LN_EPS = 1e-5
NEG_INF = -1e30

LANES = 128
PACK_ROWS = D_MODEL // (2 * LANES)
HI_HALF = 0xFFFF0000
TOK_TILE = 512
EXPERT_TILE = 512
KEYS_PER_CHUNK = WINDOW + CHUNK
VMEM_LIMIT = 60 * 1024 * 1024

F32 = jnp.float32
BF16 = jnp.bfloat16


def _layer_norm(x, g, b):
    mu = jnp.mean(x, axis=-1, keepdims=True)
    d = x - mu
    var = jnp.mean(d * d, axis=-1, keepdims=True)
    return d * lax.rsqrt(var + LN_EPS) * g + b


def _rms_norm(x, g):
    return x * lax.rsqrt(jnp.mean(x * x, axis=-1, keepdims=True) + LN_EPS) * g


def _gelu(x):
    return 0.5 * x * (1.0 + lax.erf(x * (0.5 ** 0.5)))


def _bf16_bits(x):
    return lax.bitcast_convert_type(x.astype(BF16).astype(F32), jnp.uint32)


def _store_packed_tiles(ref, x):
    n = x.shape[0]
    for s in range(PACK_ROWS):
        lo = _bf16_bits(x[:, (2 * s) * LANES:(2 * s + 1) * LANES]) >> 16
        hi = _bf16_bits(x[:, (2 * s + 1) * LANES:(2 * s + 2) * LANES]) & jnp.uint32(HI_HALF)
        ref[pl.ds(s, n, stride=PACK_ROWS), :] = hi | lo


def _load_packed_tiles(ref, n, dtype):
    cols = []
    for s in range(PACK_ROWS):
        w = ref[pl.ds(s, n, stride=PACK_ROWS), :]
        cols.append(lax.bitcast_convert_type(w << 16, F32).astype(dtype))
        cols.append(lax.bitcast_convert_type(w & jnp.uint32(HI_HALF), F32).astype(dtype))
    return jnp.concatenate(cols, axis=-1)


def _first_index_of_max(rows):
    m = rows[0]
    for r in rows[1:]:
        m = jnp.maximum(m, r)
    idx = jnp.full(m.shape, len(rows), jnp.int32)
    for e in reversed(range(len(rows))):
        idx = jnp.where(rows[e] == m, e, idx)
    return m, idx


def _mixer_kernel(n_prompt_tiles, tiles_per_seq, alpha, has_sample, first_layer, sinks_ref, *refs):
    n_x = 2 if first_layer else 6
    x_refs, refs = refs[:n_x], refs[n_x:]
    (w_in_ref, cos_ref, sa_ref, sb_ref, ck_ref, cv_ref, lng_ref, lnb_ref, ws_ref, bst_ref, na_ref, nb_ref, w_o_ref,
     l1g_ref, l1b_ref, wrt_ref, rb_ref, tri_ref, x1t_ref, x1p_ref, kt_ref, vt_ref) = refs[:22]
    rest = refs[22:]
    if has_sample:
        (ks_ref, vs_ref, vn_ref, ridx_ref, rw_ref, counts_ref,
         kd_ref, vd_ref, qm_ref, att_ref, gm_ref, cnt_ref, w_in_s, w_o_s, x_s) = rest
    else:
        (ridx_ref, rw_ref, counts_ref,
         kd_ref, vd_ref, qm_ref, att_ref, gm_ref, cnt_ref, w_in_s, w_o_s, x_s) = rest
    i = pl.program_id(0)
    is_sample = i >= n_prompt_tiles
    seq_start = jnp.logical_and(jnp.logical_not(is_sample), (i % tiles_per_seq) == 0)
    seq_end = jnp.logical_and(jnp.logical_not(is_sample), (i % tiles_per_seq) == tiles_per_seq - 1)
    tm = TOK_TILE

    @pl.when(i == 0)
    def _():
        w_in_s[...] = w_in_ref[0].astype(BF16)
        w_o_s[...] = w_o_ref[0].astype(BF16)

    if first_layer:
        xp_ref, xs_ref = x_refs
        x_s[...] = jnp.where(is_sample, xs_ref[...], xp_ref[...])
    else:
        x1_prev_ref, y0_ref, y1_ref, wc_ref, l2g_ref, l2b_ref = x_refs
        wc = wc_ref[...].T
        y_moe = (wc[:, 0:1] * _load_packed_tiles(y0_ref, tm, F32)
                 + wc[:, 1:2] * _load_packed_tiles(y1_ref, tm, F32))
        x_s[...] = _layer_norm(alpha * x1_prev_ref[...] + y_moe, l2g_ref[...], l2b_ref[...])

    h = jnp.dot(x_s[...].astype(BF16), w_in_s[...], preferred_element_type=F32)

    cos = cos_ref[...]
    sa = sa_ref[...]
    sb = sb_ref[...]

    def rope(blk):
        return blk * cos + pltpu.roll(blk, LANES - HEAD_DIM // 2, 1) * sa + pltpu.roll(blk, HEAD_DIM // 2, 1) * sb

    lane = lax.broadcasted_iota(jnp.int32, (1, LANES), 1)
    lo_half = lane < HEAD_DIM

    o_k = ATT_WIDTH
    o_v = o_k + KV_WIDTH
    o_u = o_v + KV_WIDTH
    o_g = o_u + GM_WIDTH

    k_rot = rope(h[:, o_k:o_v])
    v_new = h[:, o_v:o_u]

    @pl.when(seq_end)
    def _():
        kt_ref[...] = k_rot[tm - WINDOW:]
        vt_ref[...] = v_new[tm - WINDOW:]

    if has_sample:
        @pl.when(is_sample)
        def _():
            ks_ref[...] = k_rot
            vs_ref[...] = v_new

    def dup_heads(a):
        sw = pltpu.roll(a, HEAD_DIM, 1)
        return jnp.where(lo_half, a, sw).astype(BF16), jnp.where(lo_half, sw, a).astype(BF16)

    k_d = dup_heads(k_rot)
    v_d = dup_heads(v_new)

    @pl.when(jnp.logical_not(is_sample))
    def _():
        @pl.when(seq_start)
        def _():
            for g in range(N_KV_HEADS):
                kd_ref[g, 0:WINDOW, :] = jnp.zeros((WINDOW, LANES), BF16)
                vd_ref[g, 0:WINDOW, :] = jnp.zeros((WINDOW, LANES), BF16)

        @pl.when(jnp.logical_not(seq_start))
        def _():
            for g in range(N_KV_HEADS):
                kd_ref[g, 0:WINDOW, :] = kd_ref[g, tm:tm + WINDOW, :]
                vd_ref[g, 0:WINDOW, :] = vd_ref[g, tm:tm + WINDOW, :]

        for g in range(N_KV_HEADS):
            kd_ref[g, WINDOW:WINDOW + tm, :] = k_d[g]
            vd_ref[g, WINDOW:WINDOW + tm, :] = v_d[g]

    @pl.when(is_sample)
    def _():
        for b in range(tm // CHUNK):
            ck = dup_heads(ck_ref[b])
            cv = dup_heads(cv_ref[b])
            base = b * KEYS_PER_CHUNK
            for g in range(N_KV_HEADS):
                kd_ref[g, base:base + WINDOW, :] = ck[g]
                vd_ref[g, base:base + WINDOW, :] = cv[g]
                kd_ref[g, base + WINDOW:base + KEYS_PER_CHUNK, :] = k_d[g][b * CHUNK:(b + 1) * CHUNK]
                vd_ref[g, base + WINDOW:base + KEYS_PER_CHUNK, :] = v_d[g][b * CHUNK:(b + 1) * CHUNK]

    scale = HEAD_DIM ** -0.5
    for b in range(ATT_WIDTH // LANES):
        qb = (rope(h[:, b * LANES:(b + 1) * LANES]) * scale).astype(BF16)
        zero = jnp.zeros_like(qb)
        qm_ref[2 * b] = jnp.where(lo_half, qb, zero)
        qm_ref[2 * b + 1] = jnp.where(lo_half, zero, qb)

    key_stride = jnp.where(is_sample, KEYS_PER_CHUNK, CHUNK)
    col = lax.broadcasted_iota(jnp.int32, (1, KEYS_PER_CHUNK), 1)
    row_blk = lax.broadcasted_iota(jnp.int32, (Q_PER_KV * CHUNK, 1), 0) // CHUNK
    sink_cols = [
        jnp.where(row_blk == 0, sinks_ref[Q_PER_KV * g],
                  jnp.where(row_blk == 1, sinks_ref[Q_PER_KV * g + 1],
                            jnp.where(row_blk == 2, sinks_ref[Q_PER_KV * g + 2], sinks_ref[Q_PER_KV * g + 3])))
        for g in range(N_KV_HEADS)]

    for c in range(tm // CHUNK):
        r0 = c * CHUNK
        k0 = pl.multiple_of(c * key_stride, CHUNK)
        outs = []
        for g in range(N_KV_HEADS):
            q_st = jnp.concatenate([qm_ref[Q_PER_KV * g + r, pl.ds(r0, CHUNK), :] for r in range(Q_PER_KV)],
                                   axis=0)
            keys = kd_ref[g, pl.ds(k0, KEYS_PER_CHUNK), :]
            vals = vd_ref[g, pl.ds(k0, KEYS_PER_CHUNK), :]
            s = lax.dot_general(q_st, keys, (((1,), (1,)), ((), ())), preferred_element_type=F32)
            if c < WINDOW // CHUNK:
                first_valid = jnp.where(seq_start, (WINDOW // CHUNK - c) * CHUNK, 0)
                s = jnp.where(col >= first_valid, s, NEG_INF)
            sink = sink_cols[g]
            m = jnp.maximum(jnp.max(s, axis=-1, keepdims=True), sink)
            p = jnp.exp(s - m)
            denom = jnp.sum(p, axis=-1, keepdims=True) + jnp.exp(sink - m)
            o = jnp.dot(p.astype(BF16), vals, preferred_element_type=F32) * (1.0 / denom)
            for bb in range(2):
                outs.append(jnp.where(lo_half, o[(2 * bb) * CHUNK:(2 * bb + 1) * CHUNK],
                                      o[(2 * bb + 1) * CHUNK:(2 * bb + 2) * CHUNK]))
        for b in range(ATT_WIDTH // LANES):
            att_ref[pl.ds(r0, CHUNK), b * LANES:(b + 1) * LANES] = outs[b]

    prow = lax.broadcasted_iota(jnp.int32, (GM_CHUNK, GM_CHUNK), 0)
    pcol = lax.broadcasted_iota(jnp.int32, (GM_CHUNK, GM_CHUNK), 1)
    half = GM_CHUNK // 2
    tril = pcol <= prow
    same_blk = (prow < half) == (pcol < half)
    prow1 = lax.broadcasted_iota(jnp.int32, (GM_CHUNK, 1), 0)
    bst = bst_ref[...]
    bst_s = jnp.where(prow1 < half, bst, pltpu.roll(bst, half, 0))
    bias = jnp.where(is_sample, bst_s, bst)
    for g in range(GM_GROUPS):
        wg_ = ws_ref[g]
        w_s = jnp.where(prow < half, wg_, pltpu.roll(pltpu.roll(wg_, half, 0), half, 1))
        w_eff = jnp.where(is_sample, jnp.where(same_blk, w_s, 0.0), wg_)
        w_eff = jnp.where(tril, w_eff, 0.0).astype(BF16)
        cols = slice(g * GM_CH, (g + 1) * GM_CH)
        ug = _gelu(h[:, o_u + g * GM_CH:o_u + (g + 1) * GM_CH])
        vn = _layer_norm(_gelu(h[:, o_g + g * GM_CH:o_g + (g + 1) * GM_CH]), lng_ref[:, cols], lnb_ref[:, cols])

        if has_sample:
            @pl.when(is_sample)
            def _(cols=cols, vn=vn):
                vn_ref[:, cols] = vn

        vn_b = vn.astype(BF16)
        for n in range(tm // GM_CHUNK):
            rows = slice(n * GM_CHUNK, (n + 1) * GM_CHUNK)
            s = jnp.dot(w_eff, vn_b[rows], preferred_element_type=F32) + bias[:, g:g + 1]
            gm_ref[rows, cols] = ug[rows] * s

    mixed = jnp.concatenate([_rms_norm(att_ref[...], na_ref[...]), _rms_norm(gm_ref[...], nb_ref[...])], axis=-1)
    y = alpha * x_s[...] + jnp.dot(mixed.astype(BF16), w_o_s[...], preferred_element_type=F32)
    x1 = _layer_norm(y, l1g_ref[...], l1b_ref[...])
    x1t_ref[...] = x1
    _store_packed_tiles(x1p_ref, x1)

    nt = (((1,), (1,)), ((), ()))
    wr = wrt_ref[...]
    wr_hi = wr.astype(BF16)
    wr_lo = (wr - wr_hi.astype(F32)).astype(BF16)
    x1_hi = x1.astype(BF16)
    x1_lo = (x1 - x1_hi.astype(F32)).astype(BF16)
    logits = (lax.dot_general(wr_hi, x1_hi, nt, preferred_element_type=F32)
              + (lax.dot_general(wr_hi, x1_lo, nt, preferred_element_type=F32)
                 + lax.dot_general(wr_lo, x1_hi, nt, preferred_element_type=F32)))
    scores = jax.nn.sigmoid(logits)
    sel = scores + rb_ref[...]
    sel_rows = [sel[e:e + 1, :] for e in range(N_EXPERTS)]
    sc_rows = [scores[e:e + 1, :] for e in range(N_EXPERTS)]
    grp = []
    for g in range(N_EXPERT_GROUPS):
        r = sel_rows[g * EXPERTS_PER_GROUP:(g + 1) * EXPERTS_PER_GROUP]
        best_pair = None
        for a in range(EXPERTS_PER_GROUP):
            for b in range(a + 1, EXPERTS_PER_GROUP):
                pair = r[a] + r[b]
                best_pair = pair if best_pair is None else jnp.maximum(best_pair, pair)
        grp.append(best_pair)
    _, best = _first_index_of_max(grp)
    masked = [jnp.where(best == (e // EXPERTS_PER_GROUP), sel_rows[e], NEG_INF) for e in range(N_EXPERTS)]
    _, e0 = _first_index_of_max(masked)
    masked2 = [jnp.where(e0 == e, -jnp.inf, masked[e]) for e in range(N_EXPERTS)]
    _, e1 = _first_index_of_max(masked2)
    w0 = jnp.zeros_like(sc_rows[0])
    w1 = jnp.zeros_like(sc_rows[0])
    for e in range(N_EXPERTS):
        w0 = jnp.where(e0 == e, sc_rows[e], w0)
        w1 = jnp.where(e1 == e, sc_rows[e], w1)
    wsum = w0 + w1
    rw_ref[...] = jnp.concatenate([w0 / wsum, w1 / wsum, jnp.zeros((6, tm), F32)], axis=0)

    @pl.when(i == 0)
    def _():
        cnt_ref[...] = jnp.zeros_like(cnt_ref)

    eid = lax.broadcasted_iota(jnp.int32, (N_EXPERTS, tm), 0)
    oh0 = (eid == e0).astype(F32)
    oh1 = (eid == e1).astype(F32)
    oh = oh0 + oh1
    incl = jnp.dot(oh.astype(BF16), tri_ref[...], preferred_element_type=F32)
    before = cnt_ref[:, 0:1] + incl - oh
    r0 = jnp.sum(oh0 * before, axis=0, keepdims=True).astype(jnp.int32)
    r1 = jnp.sum(oh1 * before, axis=0, keepdims=True).astype(jnp.int32)
    cnt = cnt_ref[...] + incl[:, tm - 1:tm]
    cnt_ref[...] = cnt
    counts_ref[...] = cnt.astype(jnp.int32)
    ridx_ref[...] = jnp.concatenate([e0, e1, r0, r1, jnp.zeros((4, tm), jnp.int32)], axis=0)


def _mixer_call(layer_args, x_all, tables, n_prompt_tiles, tiles_per_seq, alpha, layer):
    (sinks, w_in, ck, cv, lng, lnb, ws, bst, na, nb, w_o, l1g, l1b, wrt, rb) = layer_args
    cos_t, sa_t, sb_t = tables

    def of_layer(shape, which=layer):
        nd = len(shape)
        return pl.BlockSpec((None,) + shape, lambda i, _nd=nd: (which,) + (0,) * _nd)

    first_layer = len(x_all) == 2
    if first_layer:
        x_main, x_tail = x_all
        t_all = x_main.shape[0] + x_tail.shape[0]
        n_tiles = t_all // TOK_TILE
        x_args = [x_main, x_tail]
        x_specs = [pl.BlockSpec((TOK_TILE, D_MODEL), lambda i: (jnp.minimum(i, n_prompt_tiles - 1), 0)),
                   pl.BlockSpec((TOK_TILE, D_MODEL), lambda i: (0, 0))]
    else:
        x1t_prev, y_pair, w_col, l2g, l2b = x_all
        t_all = x1t_prev.shape[0]
        n_tiles = t_all // TOK_TILE
        x_args = [x1t_prev, y_pair, y_pair, w_col, l2g, l2b]
        x_specs = [pl.BlockSpec((TOK_TILE, D_MODEL), lambda i: (i, 0)),
                   pl.BlockSpec((TOK_TILE * PACK_ROWS, LANES), lambda i: (i, 0)),
                   pl.BlockSpec((TOK_TILE * PACK_ROWS, LANES), lambda i: (i + n_tiles, 0)),
                   pl.BlockSpec((8, TOK_TILE), lambda i: (0, i)),
                   of_layer((1, D_MODEL), layer - 1), of_layer((1, D_MODEL), layer - 1)]
    n_seq_tiles = tiles_per_seq
    has_sample = n_tiles > n_prompt_tiles

    def const(shape):
        nd = len(shape)
        return pl.BlockSpec(shape, lambda i, _nd=nd: (0,) * _nd)

    def tab_map(i):
        return (jnp.where(i < n_prompt_tiles, i % n_seq_tiles, n_seq_tiles), 0)

    n_seq = n_prompt_tiles // n_seq_tiles

    def seq_map(i):
        return (jnp.minimum(i // n_seq_tiles, n_seq - 1), 0)

    row_blk = lambda w: pl.BlockSpec((TOK_TILE, w), lambda i: (i, 0))
    in_specs = [
        pl.BlockSpec(memory_space=pltpu.SMEM),
        *x_specs,
        pl.BlockSpec((1, D_MODEL, D_IN), lambda i: (layer, 0, 0)),
        pl.BlockSpec((TOK_TILE, LANES), tab_map),
        pl.BlockSpec((TOK_TILE, LANES), tab_map),
        pl.BlockSpec((TOK_TILE, LANES), tab_map),
        pl.BlockSpec((TOK_TILE // CHUNK, WINDOW, KV_WIDTH), lambda i: (layer, 0, 0)),
        pl.BlockSpec((TOK_TILE // CHUNK, WINDOW, KV_WIDTH), lambda i: (layer, 0, 0)),
        of_layer((1, GM_WIDTH)), of_layer((1, GM_WIDTH)),
        of_layer((GM_GROUPS, GM_CHUNK, GM_CHUNK)), of_layer((GM_CHUNK, GM_GROUPS)),
        of_layer((1, ATT_WIDTH)), of_layer((1, GM_WIDTH)),
        pl.BlockSpec((1, D_MIX, D_MODEL), lambda i: (layer, 0, 0)),
        of_layer((1, D_MODEL)), of_layer((1, D_MODEL)),
        const((N_EXPERTS, D_MODEL)), const((N_EXPERTS, 1)),
        const((TOK_TILE, TOK_TILE)),
    ]
    out_shape = [
        jax.ShapeDtypeStruct((t_all, D_MODEL), F32),
        jax.ShapeDtypeStruct((t_all * PACK_ROWS, LANES), jnp.uint32),
        jax.ShapeDtypeStruct((n_seq * WINDOW, KV_WIDTH), F32),
        jax.ShapeDtypeStruct((n_seq * WINDOW, KV_WIDTH), F32),
        jax.ShapeDtypeStruct((TOK_TILE, KV_WIDTH), F32),
        jax.ShapeDtypeStruct((TOK_TILE, KV_WIDTH), F32),
        jax.ShapeDtypeStruct((TOK_TILE, GM_WIDTH), F32),
        jax.ShapeDtypeStruct((8, t_all), jnp.int32),
        jax.ShapeDtypeStruct((8, t_all), F32),
        jax.ShapeDtypeStruct((N_EXPERTS, LANES), jnp.int32),
    ]
    out_specs = [
        pl.BlockSpec((TOK_TILE, D_MODEL), lambda i: (i, 0)),
        pl.BlockSpec((TOK_TILE * PACK_ROWS, LANES), lambda i: (i, 0)),
        pl.BlockSpec((WINDOW, KV_WIDTH), seq_map), pl.BlockSpec((WINDOW, KV_WIDTH), seq_map),
        const((TOK_TILE, KV_WIDTH)), const((TOK_TILE, KV_WIDTH)),
        const((TOK_TILE, GM_WIDTH)),
        pl.BlockSpec((8, TOK_TILE), lambda i: (0, i)),
        pl.BlockSpec((8, TOK_TILE), lambda i: (0, i)),
        const((N_EXPERTS, LANES)),
    ]
    if not has_sample:
        del out_shape[4:7], out_specs[4:7]
    kd_rows = (TOK_TILE // CHUNK) * KEYS_PER_CHUNK
    scratch = [
        pltpu.VMEM((N_KV_HEADS, kd_rows, LANES), BF16),
        pltpu.VMEM((N_KV_HEADS, kd_rows, LANES), BF16),
        pltpu.VMEM((N_Q_HEADS, TOK_TILE, LANES), BF16),
        pltpu.VMEM((TOK_TILE, ATT_WIDTH), F32),
        pltpu.VMEM((TOK_TILE, GM_WIDTH), F32),
        pltpu.VMEM((N_EXPERTS, LANES), F32),
        pltpu.VMEM((D_MODEL, D_IN), BF16),
        pltpu.VMEM((D_MIX, D_MODEL), BF16),
        pltpu.VMEM((TOK_TILE, D_MODEL), F32),
    ]
    return pl.pallas_call(
        functools.partial(_mixer_kernel, n_prompt_tiles, tiles_per_seq, alpha, has_sample, first_layer),
        grid=(n_tiles,),
        in_specs=in_specs, out_specs=out_specs, out_shape=out_shape,
        scratch_shapes=scratch,
        compiler_params=pltpu.CompilerParams(dimension_semantics=("arbitrary",), vmem_limit_bytes=VMEM_LIMIT),
        name="mixer",
    )(sinks, *x_args, w_in, cos_t, sa_t, sb_t, ck, cv, lng, lnb, ws, bst, na, nb, w_o, l1g, l1b, wrt, rb,
      jnp.triu(jnp.ones((TOK_TILE, TOK_TILE), BF16)))


def _expert_kernel(layer, te_ref, nt_ref, plan_ref, xs_ref, wg_hbm, wu_hbm, wd_hbm, out_ref,
                   wg_s, wu_s, wd_s, wg_f, wu_f, wd_f, sems):
    i = pl.program_id(0)
    plan = plan_ref[i]
    first = (plan & 1) == 1
    slot = (plan >> 1) & 1
    nxt = plan >> 2

    def weight_copies(e, s):
        return (pltpu.make_async_copy(wg_hbm.at[layer, e], wg_f.at[s], sems.at[s, 0]),
                pltpu.make_async_copy(wu_hbm.at[layer, e], wu_f.at[s], sems.at[s, 1]),
                pltpu.make_async_copy(wd_hbm.at[layer, e], wd_f.at[s], sems.at[s, 2]))

    @pl.when(i == 0)
    def _():
        for c in weight_copies(te_ref[0], slot):
            c.start()

    @pl.when(first)
    def _():
        for c in weight_copies(te_ref[i], slot):
            c.wait()

        @pl.when(nxt < N_EXPERTS)
        def _():
            for c in weight_copies(nxt, 1 - slot):
                c.start()

        wg_s[...] = wg_f[slot].astype(BF16)
        wu_s[...] = wu_f[slot].astype(BF16)
        wd_s[...] = wd_f[slot].astype(BF16)

    @pl.when(i < nt_ref[0])
    def _():
        xs = _load_packed_tiles(xs_ref, EXPERT_TILE, BF16)
        g = jnp.dot(xs, wg_s[...], preferred_element_type=F32)
        u = jnp.dot(xs, wu_s[...], preferred_element_type=F32)
        hmid = (g * jax.nn.sigmoid(g)) * u
        _store_packed_tiles(out_ref, jnp.dot(hmid.astype(BF16), wd_s[...], preferred_element_type=F32))


def _expert_call(tile_expert, n_used, tile_plan, xs, wg, wu, wd, layer):
    n_tiles = xs.shape[0] // (EXPERT_TILE * PACK_ROWS)
    tile_blk = pl.BlockSpec((EXPERT_TILE * PACK_ROWS, LANES),
                            lambda i, te, nt, plan: (jnp.minimum(i, nt[0] - 1), 0))
    grid_spec = pltpu.PrefetchScalarGridSpec(
        num_scalar_prefetch=3,
        grid=(n_tiles,),
        in_specs=[tile_blk] + [pl.BlockSpec(memory_space=pl.ANY)] * 3,
        out_specs=tile_blk,
        scratch_shapes=[
            pltpu.VMEM((D_MODEL, D_EXPERT), BF16),
            pltpu.VMEM((D_MODEL, D_EXPERT), BF16),
            pltpu.VMEM((D_EXPERT, D_MODEL), BF16),
            pltpu.VMEM((2, D_MODEL, D_EXPERT), F32),
            pltpu.VMEM((2, D_MODEL, D_EXPERT), F32),
            pltpu.VMEM((2, D_EXPERT, D_MODEL), F32),
            pltpu.SemaphoreType.DMA((2, 3)),
        ],
    )
    return pl.pallas_call(
        functools.partial(_expert_kernel, layer),
        grid_spec=grid_spec,
        out_shape=jax.ShapeDtypeStruct(xs.shape, jnp.uint32),
        compiler_params=pltpu.CompilerParams(dimension_semantics=("arbitrary",), vmem_limit_bytes=VMEM_LIMIT),
        name="experts",
    )(tile_expert, n_used, tile_plan, xs, wg, wu, wd)


SC_CHUNK = 128
SC_PAD_CHUNK = 64
SC_GATHER_CHUNK = 64


def _sc_workers():
    info = plsc.get_sparse_core_info()
    return info.num_cores, info.num_cores * info.num_subcores


def _sc_split(per_w):
    n_full = per_w // SC_CHUNK
    tail = per_w - n_full * SC_CHUNK
    assert tail % 8 == 0
    return n_full, tail


def _sc_dispatch_call(pos_flat, pad_rows, x_tiles, tile_rows, n_sorted_rows):
    t_all = x_tiles.shape[0] // tile_rows
    dt = x_tiles.dtype
    nc, nw = _sc_workers()
    per_w = t_all // nw
    n_ch, tail = _sc_split(per_w)
    n_pad = pad_rows.shape[0]
    pad_per_w = n_pad // nw
    n_pch = pad_per_w // SC_PAD_CHUNK
    assert per_w * nw == t_all and n_pch * SC_PAD_CHUNK * nw == n_pad
    zeros = jnp.zeros((SC_PAD_CHUNK, tile_rows, LANES), dt)
    tail_scratch = [] if tail == 0 else [
        pltpu.VMEM((tail,), jnp.int32), pltpu.VMEM((tail,), jnp.int32), pltpu.VMEM((tail, tile_rows, LANES), dt)]

    @functools.partial(
        pl.kernel, mesh=plsc.VectorSubcoreMesh(core_axis_name="c", subcore_axis_name="s"),
        out_type=jax.ShapeDtypeStruct((n_sorted_rows, tile_rows, LANES), dt),
        scratch_types=[
            pltpu.VMEM((SC_CHUNK,), jnp.int32), pltpu.VMEM((SC_CHUNK,), jnp.int32),
            pltpu.VMEM((SC_CHUNK, tile_rows, LANES), dt),
            pltpu.VMEM((SC_PAD_CHUNK,), jnp.int32),
            pltpu.VMEM((SC_PAD_CHUNK, tile_rows, LANES), dt),
            pltpu.SemaphoreType.DMA, pltpu.SemaphoreType.DMA, pltpu.SemaphoreType.DMA,
        ] + tail_scratch,
    )
    def k(pos_hbm, pad_hbm, x_hbm, z_hbm, xs_hbm, i0_v, i1_v, rows_v, ip_v, z_v, sem0, sem1, sem2, *tail_refs):
        wid = lax.axis_index("s") * nc + lax.axis_index("c")

        def move(b, n, idx0, idx1, rows):
            loads = (pltpu.async_copy(pos_hbm.at[pl.ds(b, n)], idx0, sem0),
                     pltpu.async_copy(pos_hbm.at[pl.ds(t_all + b, n)], idx1, sem1),
                     pltpu.async_copy(x_hbm.at[pl.ds(b, n)], rows, sem2))
            for c in loads:
                c.wait()
            stores = (pltpu.async_copy(rows, xs_hbm.at[idx0], sem0), pltpu.async_copy(rows, xs_hbm.at[idx1], sem1))
            for c in stores:
                c.wait()

        @pl.loop(0, n_ch)
        def _(j):
            move(pl.multiple_of(wid * per_w + j * SC_CHUNK, 8), SC_CHUNK, i0_v, i1_v, rows_v)

        if tail:
            move(pl.multiple_of(wid * per_w + n_ch * SC_CHUNK, 8), tail, *tail_refs)

        pltpu.sync_copy(z_hbm, z_v)

        @pl.loop(0, n_pch)
        def _(j):
            b = pl.multiple_of(wid * pad_per_w + j * SC_PAD_CHUNK, 8)
            pltpu.sync_copy(pad_hbm.at[pl.ds(b, SC_PAD_CHUNK)], ip_v)
            pltpu.sync_copy(z_v, xs_hbm.at[ip_v])

    xs = k(pos_flat, pad_rows, x_tiles.reshape(t_all, tile_rows, LANES), zeros)
    return xs.reshape(n_sorted_rows * tile_rows, LANES)


def _sc_gather_call(pos_flat, outs, tile_rows):
    n_assign = pos_flat.shape[0]
    dt = outs.dtype
    nc, nw = _sc_workers()
    per_w = n_assign // nw
    chunk = SC_GATHER_CHUNK
    n_ch = per_w // chunk
    tail = per_w - n_ch * chunk
    assert per_w * nw == n_assign and n_ch % 2 == 0 and tail % 8 == 0
    tail_scratch = [] if tail == 0 else [pltpu.VMEM((tail,), jnp.int32), pltpu.VMEM((tail, tile_rows, LANES), dt)]

    @functools.partial(
        pl.kernel, mesh=plsc.VectorSubcoreMesh(core_axis_name="c", subcore_axis_name="s"),
        out_type=jax.ShapeDtypeStruct((n_assign, tile_rows, LANES), dt),
        scratch_types=[pltpu.VMEM((chunk,), jnp.int32), pltpu.VMEM((chunk,), jnp.int32),
                       pltpu.VMEM((chunk, tile_rows, LANES), dt), pltpu.VMEM((chunk, tile_rows, LANES), dt),
                       pltpu.SemaphoreType.DMA, pltpu.SemaphoreType.DMA] + tail_scratch,
    )
    def k(pos_hbm, o_hbm, y_hbm, idx_a, idx_b, rows_a, rows_b, sem_a, sem_b, *tail_refs):
        wid = lax.axis_index("s") * nc + lax.axis_index("c")

        def row0(j):
            return pl.multiple_of(wid * per_w + j * chunk, 8)

        def start(j, idx, rows, sem):
            pltpu.sync_copy(pos_hbm.at[pl.ds(row0(j), chunk)], idx)
            pltpu.async_copy(o_hbm.at[idx], rows, sem)

        def finish(j, idx, rows, sem):
            pltpu.make_async_copy(o_hbm.at[idx], rows, sem).wait()
            pltpu.sync_copy(rows, y_hbm.at[pl.ds(row0(j), chunk)])

        start(0, idx_a, rows_a, sem_a)

        @pl.loop(0, n_ch // 2)
        def _(p):
            j = 2 * p
            start(j + 1, idx_b, rows_b, sem_b)
            finish(j, idx_a, rows_a, sem_a)

            @pl.when(j + 2 < n_ch)
            def _():
                start(j + 2, idx_a, rows_a, sem_a)

            finish(j + 1, idx_b, rows_b, sem_b)

        if tail:
            idx_t, rows_t = tail_refs
            b = pl.multiple_of(wid * per_w + n_ch * chunk, 8)
            pltpu.sync_copy(pos_hbm.at[pl.ds(b, tail)], idx_t)
            pltpu.sync_copy(o_hbm.at[idx_t], rows_t)
            pltpu.sync_copy(rows_t, y_hbm.at[pl.ds(b, tail)])

    y = k(pos_flat, outs.reshape(-1, tile_rows, LANES))
    return y.reshape(n_assign * tile_rows, LANES)


def _combine_dense_kernel(alpha, n_main_tiles, x1t_ref, y0_ref, y1_ref, w_ref, g_ref, b_ref, out_ref, *tail):
    j = pl.program_id(0)
    w = w_ref[...].T
    y = (w[:, 0:1] * _load_packed_tiles(y0_ref, TOK_TILE, F32)
         + w[:, 1:2] * _load_packed_tiles(y1_ref, TOK_TILE, F32))
    res = _layer_norm(alpha * x1t_ref[...] + y, g_ref[...], b_ref[...])
    if n_main_tiles is None:
        out_ref[...] = res
    else:
        @pl.when(j < n_main_tiles)
        def _():
            out_ref[...] = res

        @pl.when(j >= n_main_tiles)
        def _():
            tail[0][...] = res


def _combine_dense_call(x1t, y_pair, w_col, g, b, alpha, n_main_tiles=None):
    t_all = x1t.shape[0]
    n_tiles = t_all // TOK_TILE
    if n_main_tiles is None:
        out_specs = pl.BlockSpec((TOK_TILE, D_MODEL), lambda i: (i, 0))
        out_shape = jax.ShapeDtypeStruct((t_all, D_MODEL), F32)
    else:
        assert n_tiles == n_main_tiles + 1
        out_specs = [pl.BlockSpec((TOK_TILE, D_MODEL), lambda i: (jnp.minimum(i, n_main_tiles - 1), 0)),
                     pl.BlockSpec((TOK_TILE, D_MODEL), lambda i: (0, 0))]
        out_shape = [jax.ShapeDtypeStruct((n_main_tiles * TOK_TILE, D_MODEL), F32),
                     jax.ShapeDtypeStruct((TOK_TILE, D_MODEL), F32)]
    tile = (TOK_TILE, D_MODEL)
    packed = (TOK_TILE * PACK_ROWS, LANES)
    return pl.pallas_call(
        functools.partial(_combine_dense_kernel, alpha, n_main_tiles),
        grid=(n_tiles,),
        in_specs=[
            pl.BlockSpec(tile, lambda i: (i, 0)),
            pl.BlockSpec(packed, lambda i: (i, 0)),
            pl.BlockSpec(packed, lambda i: (i + n_tiles, 0)),
            pl.BlockSpec((8, TOK_TILE), lambda i: (0, i)),
            pl.BlockSpec((1, D_MODEL), lambda i: (0, 0)),
            pl.BlockSpec((1, D_MODEL), lambda i: (0, 0)),
        ],
        out_specs=out_specs,
        out_shape=out_shape,
        compiler_params=pltpu.CompilerParams(dimension_semantics=("arbitrary",), vmem_limit_bytes=VMEM_LIMIT),
        name="combine",
    )(x1t, y_pair, y_pair, w_col, g, b)


def _rope_tables(seq, dec_seq, past_len):
    half = HEAD_DIM // 2
    inv = ROPE_THETA ** (-jnp.arange(half, dtype=F32) / half)
    pos = jnp.concatenate([jnp.arange(seq), past_len + (jnp.arange(TOK_TILE) % dec_seq)])
    ang = pos.astype(F32)[:, None] * inv[None, :]
    cos = jnp.cos(ang)
    sin = jnp.sin(ang)
    zero = jnp.zeros_like(sin)
    reps = LANES // HEAD_DIM
    cos_t = jnp.tile(jnp.concatenate([cos, cos], -1), (1, reps))
    sa_t = jnp.tile(jnp.concatenate([-sin, zero], -1), (1, reps))
    sb_t = jnp.tile(jnp.concatenate([zero, sin], -1), (1, reps))
    return cos_t, sa_t, sb_t


def _inclusive_cumsum(v):
    n = v.shape[0]
    tri = jnp.arange(n)[None, :] <= jnp.arange(n)[:, None]
    return jnp.sum(jnp.where(tri, v[None, :], 0), axis=1)


def _dispatch_plan(ridx, counts, n_tiles):
    t_all = ridx.shape[1]
    tiles_e = (counts + EXPERT_TILE - 1) // EXPERT_TILE
    tile_end = _inclusive_cumsum(tiles_e)
    offs = (tile_end - tiles_e) * EXPERT_TILE
    n_used = tile_end[-1:]
    tile_expert = jnp.minimum(
        jnp.sum((jnp.arange(n_tiles)[:, None] >= tile_end[None, :]).astype(jnp.int32), axis=1), N_EXPERTS - 1)
    experts = jnp.arange(N_EXPERTS)
    used = tiles_e > 0
    slot_e = (_inclusive_cumsum(used.astype(jnp.int32)) - 1) % 2
    next_e = jnp.min(jnp.where((experts[None, :] > experts[:, None]) & used[None, :], experts[None, :], N_EXPERTS),
                     axis=1)
    tiles = jnp.arange(n_tiles)
    of_tile = tile_expert[:, None] == experts[None, :]
    pick = lambda v: jnp.sum(jnp.where(of_tile, v[None, :], 0), axis=1)
    first = (tiles == pick(tile_end - tiles_e)) & (tiles < n_used)
    tile_plan = first.astype(jnp.int32) + 2 * pick(slot_e) + 4 * pick(next_e)
    pos = ridx[TOP_K:2 * TOP_K] + jnp.sum(
        jnp.where(ridx[:TOP_K, :, None] == experts[None, None, :], offs[None, None, :], 0), axis=-1)
    gap_start = jnp.concatenate([offs + counts, n_used * EXPERT_TILE])
    gap_len = jnp.concatenate([tiles_e * EXPERT_TILE - counts, (n_tiles - n_used) * EXPERT_TILE])
    gap_end_q = _inclusive_cumsum(gap_len)
    q = jnp.arange(n_tiles * EXPERT_TILE - TOP_K * t_all)
    gap = jnp.sum((q[:, None] >= gap_end_q[None, :]).astype(jnp.int32), axis=1)
    in_gap = gap[:, None] == jnp.arange(N_EXPERTS + 1)[None, :]
    pad_rows = q + jnp.sum(jnp.where(in_gap, (gap_start - gap_end_q + gap_len)[None, :], 0), axis=1)
    return (pos.reshape(-1).astype(jnp.int32), pad_rows.astype(jnp.int32),
            tile_expert.astype(jnp.int32), n_used.astype(jnp.int32), tile_plan.astype(jnp.int32))


def kernel(x_prompt, x_sample, cache_k, cache_v, w_in, sinks, gm_ln_g, gm_ln_b, gm_ws, gm_bs,
           out_norm_a, out_norm_b, w_o, ln1_g, ln1_b, w_router, router_bias,
           w_gate, w_up, w_down, ln2_g, ln2_b):
    batch, seq, _ = x_prompt.shape
    dec_batch, dec_seq, _ = x_sample.shape
    depth = w_in.shape[0]
    past_len = PAST_LEN
    assert dec_batch * dec_seq == TOK_TILE and dec_seq == CHUNK and seq % TOK_TILE == 0
    assert cache_k.shape[2] == WINDOW
    alpha = (2 * depth) ** 0.25
    tiles_per_seq = seq // TOK_TILE

    tables = _rope_tables(seq, dec_seq, past_len)
    wrt = w_router.T
    rb = router_bias.reshape(N_EXPERTS, 1)
    cache_k2 = cache_k.reshape(depth * dec_batch, WINDOW, KV_WIDTH)
    cache_v2 = cache_v.reshape(depth * dec_batch, WINDOW, KV_WIDTH)
    as_rows = lambda p: p.reshape(depth, 1, -1)
    lng3, lnb3, na3, nb3 = as_rows(gm_ln_g), as_rows(gm_ln_b), as_rows(out_norm_a), as_rows(out_norm_b)
    l1g3, l1b3, l2g3, l2b3 = as_rows(ln1_g), as_rows(ln1_b), as_rows(ln2_g), as_rows(ln2_b)
    bst3 = jnp.swapaxes(gm_bs, 1, 2)

    streams = [
        dict(x=(x_prompt.reshape(batch * seq, D_MODEL), x_sample.reshape(-1, D_MODEL)),
             n_seq=batch, has_sample=True),
    ]
    kp, vp, ks, vs, gms = [], [], [], [], []
    for l in range(depth):
        layer_args = (sinks[l], w_in, cache_k2, cache_v2, lng3, lnb3, gm_ws, bst3, na3, nb3, w_o, l1g3, l1b3, wrt, rb)
        k_tails, v_tails = [], []
        for st in streams:
            n_prompt = st["n_seq"] * seq
            n_prompt_tiles = n_prompt // TOK_TILE
            t_all = n_prompt + (TOK_TILE if st["has_sample"] else 0)
            n_exp_tiles = (TOP_K * t_all) // EXPERT_TILE + N_EXPERTS
            res = _mixer_call(layer_args, st["x"], tables, n_prompt_tiles, tiles_per_seq, alpha, l)
            if st["has_sample"]:
                x1t, x1p, k_tail, v_tail, k_s, v_s, vn_s, ridx, rw, counts = res
            else:
                x1t, x1p, k_tail, v_tail, ridx, rw, counts = res

            pos, pad_rows, tile_expert, n_used, tile_plan = _dispatch_plan(ridx, counts[:, 0], n_exp_tiles)
            xs = _sc_dispatch_call(pos, pad_rows, x1p, PACK_ROWS, n_exp_tiles * EXPERT_TILE)
            outs = _expert_call(tile_expert, n_used, tile_plan, xs, w_gate, w_up, w_down, l)
            y_pair = _sc_gather_call(pos, outs, PACK_ROWS)
            if l < depth - 1:
                st["x"] = (x1t, y_pair, rw, l2g3, l2b3)
            else:
                st["x"] = _combine_dense_call(x1t, y_pair, rw, l2g3[l], l2b3[l], alpha,
                                              n_main_tiles=n_prompt_tiles if st["has_sample"] else None)

            k_tails.append(k_tail)
            v_tails.append(v_tail)
            if st["has_sample"]:
                ks.append(k_s)
                vs.append(v_s)
                gms.append(vn_s)
        kp.append(jnp.concatenate(k_tails, axis=0))
        vp.append(jnp.concatenate(v_tails, axis=0))

    y_prompt = jnp.concatenate([st["x"][0] if st["has_sample"] else st["x"] for st in streams],
                               axis=0).reshape(batch, seq, D_MODEL)
    y_sample = streams[-1]["x"][1].reshape(dec_batch, dec_seq, D_MODEL)
    heads = (N_KV_HEADS, HEAD_DIM)
    return (y_prompt, y_sample,
            jnp.stack(kp).reshape(depth, batch, WINDOW, *heads), jnp.stack(vp).reshape(depth, batch, WINDOW, *heads),
            jnp.stack(ks).reshape(depth, dec_batch, dec_seq, *heads),
            jnp.stack(vs).reshape(depth, dec_batch, dec_seq, *heads),
            jnp.stack(gms).reshape(depth, dec_batch, dec_seq, GM_WIDTH))
```

```python
import functools

import jax
import jax.numpy as jnp
from jax import lax
from jax.experimental import pallas as pl
from jax.experimental.pallas import tpu as pltpu
from jax.experimental.pallas import tpu_sc as plsc

D_MODEL = 1024
CHUNK = 64
WINDOW = 128
HEAD_DIM = 64
N_Q_HEADS = 8
N_KV_HEADS = 2
Q_PER_KV = N_Q_HEADS // N_KV_HEADS
PAST_LEN = 2048
ATT_WIDTH = N_Q_HEADS * HEAD_DIM
KV_WIDTH = N_KV_HEADS * HEAD_DIM
ROPE_THETA = 10000.0
GM_GROUPS = 4
GM_CH = 128
GM_WIDTH = GM_GROUPS * GM_CH
GM_CHUNK = 128
D_MIX = ATT_WIDTH + GM_WIDTH
D_IN = ATT_WIDTH + 2 * KV_WIDTH + 2 * GM_WIDTH
N_EXPERTS = 16
N_EXPERT_GROUPS = 4
EXPERTS_PER_GROUP = N_EXPERTS // N_EXPERT_GROUPS
TOP_K = 2
D_EXPERT = 512
LN_EPS = 1e-5
NEG_INF = -1e30

LANES = 128
PACK_ROWS = D_MODEL // (2 * LANES)
HI_HALF = 0xFFFF0000
TOK_TILE = 512
EXPERT_TILE = 512
KEYS_PER_CHUNK = WINDOW + CHUNK
VMEM_LIMIT = 60 * 1024 * 1024

F32 = jnp.float32
BF16 = jnp.bfloat16


def _layer_norm(x, g, b):
    mu = jnp.mean(x, axis=-1, keepdims=True)
    d = x - mu
    var = jnp.mean(d * d, axis=-1, keepdims=True)
    return d * lax.rsqrt(var + LN_EPS) * g + b


def _rms_norm(x, g):
    return x * lax.rsqrt(jnp.mean(x * x, axis=-1, keepdims=True) + LN_EPS) * g


def _gelu(x):
    return 0.5 * x * (1.0 + lax.erf(x * (0.5 ** 0.5)))


def _bf16_bits(x):
    return lax.bitcast_convert_type(x.astype(BF16).astype(F32), jnp.uint32)


def _store_packed_tiles(ref, x):
    n = x.shape[0]
    for s in range(PACK_ROWS):
        lo = _bf16_bits(x[:, (2 * s) * LANES:(2 * s + 1) * LANES]) >> 16
        hi = _bf16_bits(x[:, (2 * s + 1) * LANES:(2 * s + 2) * LANES]) & jnp.uint32(HI_HALF)
        ref[pl.ds(s, n, stride=PACK_ROWS), :] = hi | lo


def _load_packed_tiles(ref, n, dtype):
    cols = []
    for s in range(PACK_ROWS):
        w = ref[pl.ds(s, n, stride=PACK_ROWS), :]
        cols.append(lax.bitcast_convert_type(w << 16, F32).astype(dtype))
        cols.append(lax.bitcast_convert_type(w & jnp.uint32(HI_HALF), F32).astype(dtype))
    return jnp.concatenate(cols, axis=-1)


def _first_index_of_max(rows):
    m = rows[0]
    for r in rows[1:]:
        m = jnp.maximum(m, r)
    idx = jnp.full(m.shape, len(rows), jnp.int32)
    for e in reversed(range(len(rows))):
        idx = jnp.where(rows[e] == m, e, idx)
    return m, idx


def _mixer_kernel(n_prompt_tiles, tiles_per_seq, alpha, has_sample, first_layer, sinks_ref, *refs):
    n_x = 2 if first_layer else 6
    x_refs, refs = refs[:n_x], refs[n_x:]
    (w_in_ref, cos_ref, sa_ref, sb_ref, ck_ref, cv_ref, lng_ref, lnb_ref, ws_ref, bst_ref, na_ref, nb_ref, w_o_ref,
     l1g_ref, l1b_ref, wrt_ref, rb_ref, tri_ref, x1t_ref, x1p_ref, kt_ref, vt_ref) = refs[:22]
    rest = refs[22:]
    if has_sample:
        (ks_ref, vs_ref, vn_ref, ridx_ref, rw_ref, counts_ref,
         kd_ref, vd_ref, qm_ref, att_ref, gm_ref, cnt_ref, w_in_s, w_o_s, x_s) = rest
    else:
        (ridx_ref, rw_ref, counts_ref,
         kd_ref, vd_ref, qm_ref, att_ref, gm_ref, cnt_ref, w_in_s, w_o_s, x_s) = rest
    i = pl.program_id(0)
    is_sample = i >= n_prompt_tiles
    seq_start = jnp.logical_and(jnp.logical_not(is_sample), (i % tiles_per_seq) == 0)
    seq_end = jnp.logical_and(jnp.logical_not(is_sample), (i % tiles_per_seq) == tiles_per_seq - 1)
    tm = TOK_TILE

    @pl.when(i == 0)
    def _():
        w_in_s[...] = w_in_ref[0].astype(BF16)
        w_o_s[...] = w_o_ref[0].astype(BF16)

    if first_layer:
        xp_ref, xs_ref = x_refs
        x_s[...] = jnp.where(is_sample, xs_ref[...], xp_ref[...])
    else:
        x1_prev_ref, y0_ref, y1_ref, wc_ref, l2g_ref, l2b_ref = x_refs
        wc = wc_ref[...].T
        y_moe = (wc[:, 0:1] * _load_packed_tiles(y0_ref, tm, F32)
                 + wc[:, 1:2] * _load_packed_tiles(y1_ref, tm, F32))
        x_s[...] = _layer_norm(alpha * x1_prev_ref[...] + y_moe, l2g_ref[...], l2b_ref[...])

    h = jnp.dot(x_s[...].astype(BF16), w_in_s[...], preferred_element_type=F32)

    cos = cos_ref[...]
    sa = sa_ref[...]
    sb = sb_ref[...]

    def rope(blk):
        return blk * cos + pltpu.roll(blk, LANES - HEAD_DIM // 2, 1) * sa + pltpu.roll(blk, HEAD_DIM // 2, 1) * sb

    lane = lax.broadcasted_iota(jnp.int32, (1, LANES), 1)
    lo_half = lane < HEAD_DIM

    o_k = ATT_WIDTH
    o_v = o_k + KV_WIDTH
    o_u = o_v + KV_WIDTH
    o_g = o_u + GM_WIDTH

    k_rot = rope(h[:, o_k:o_v])
    v_new = h[:, o_v:o_u]

    @pl.when(seq_end)
    def _():
        kt_ref[...] = k_rot[tm - WINDOW:]
        vt_ref[...] = v_new[tm - WINDOW:]

    if has_sample:
        @pl.when(is_sample)
        def _():
            ks_ref[...] = k_rot
            vs_ref[...] = v_new

    def dup_heads(a):
        sw = pltpu.roll(a, HEAD_DIM, 1)
        return jnp.where(lo_half, a, sw).astype(BF16), jnp.where(lo_half, sw, a).astype(BF16)

    k_d = dup_heads(k_rot)
    v_d = dup_heads(v_new)

    @pl.when(jnp.logical_not(is_sample))
    def _():
        @pl.when(seq_start)
        def _():
            for g in range(N_KV_HEADS):
                kd_ref[g, 0:WINDOW, :] = jnp.zeros((WINDOW, LANES), BF16)
                vd_ref[g, 0:WINDOW, :] = jnp.zeros((WINDOW, LANES), BF16)

        @pl.when(jnp.logical_not(seq_start))
        def _():
            for g in range(N_KV_HEADS):
                kd_ref[g, 0:WINDOW, :] = kd_ref[g, tm:tm + WINDOW, :]
                vd_ref[g, 0:WINDOW, :] = vd_ref[g, tm:tm + WINDOW, :]

        for g in range(N_KV_HEADS):
            kd_ref[g, WINDOW:WINDOW + tm, :] = k_d[g]
            vd_ref[g, WINDOW:WINDOW + tm, :] = v_d[g]

    @pl.when(is_sample)
    def _():
        for b in range(tm // CHUNK):
            ck = dup_heads(ck_ref[b])
            cv = dup_heads(cv_ref[b])
            base = b * KEYS_PER_CHUNK
            for g in range(N_KV_HEADS):
                kd_ref[g, base:base + WINDOW, :] = ck[g]
                vd_ref[g, base:base + WINDOW, :] = cv[g]
                kd_ref[g, base + WINDOW:base + KEYS_PER_CHUNK, :] = k_d[g][b * CHUNK:(b + 1) * CHUNK]
                vd_ref[g, base + WINDOW:base + KEYS_PER_CHUNK, :] = v_d[g][b * CHUNK:(b + 1) * CHUNK]

    scale = HEAD_DIM ** -0.5
    for b in range(ATT_WIDTH // LANES):
        qb = (rope(h[:, b * LANES:(b + 1) * LANES]) * scale).astype(BF16)
        zero = jnp.zeros_like(qb)
        qm_ref[2 * b] = jnp.where(lo_half, qb, zero)
        qm_ref[2 * b + 1] = jnp.where(lo_half, zero, qb)

    key_stride = jnp.where(is_sample, KEYS_PER_CHUNK, CHUNK)
    col = lax.broadcasted_iota(jnp.int32, (1, KEYS_PER_CHUNK), 1)
    row_blk = lax.broadcasted_iota(jnp.int32, (Q_PER_KV * CHUNK, 1), 0) // CHUNK
    sink_cols = [
        jnp.where(row_blk == 0, sinks_ref[Q_PER_KV * g],
                  jnp.where(row_blk == 1, sinks_ref[Q_PER_KV * g + 1],
                            jnp.where(row_blk == 2, sinks_ref[Q_PER_KV * g + 2], sinks_ref[Q_PER_KV * g + 3])))
        for g in range(N_KV_HEADS)]

    for c in range(tm // CHUNK):
        r0 = c * CHUNK
        k0 = pl.multiple_of(c * key_stride, CHUNK)
        outs = []
        for g in range(N_KV_HEADS):
            q_st = jnp.concatenate([qm_ref[Q_PER_KV * g + r, pl.ds(r0, CHUNK), :] for r in range(Q_PER_KV)],
                                   axis=0)
            keys = kd_ref[g, pl.ds(k0, KEYS_PER_CHUNK), :]
            vals = vd_ref[g, pl.ds(k0, KEYS_PER_CHUNK), :]
            s = lax.dot_general(q_st, keys, (((1,), (1,)), ((), ())), preferred_element_type=F32)
            if c < WINDOW // CHUNK:
                first_valid = jnp.where(seq_start, (WINDOW // CHUNK - c) * CHUNK, 0)
                s = jnp.where(col >= first_valid, s, NEG_INF)
            sink = sink_cols[g]
            m = jnp.maximum(jnp.max(s, axis=-1, keepdims=True), sink)
            p = jnp.exp(s - m)
            denom = jnp.sum(p, axis=-1, keepdims=True) + jnp.exp(sink - m)
            o = jnp.dot(p.astype(BF16), vals, preferred_element_type=F32) * (1.0 / denom)
            for bb in range(2):
                outs.append(jnp.where(lo_half, o[(2 * bb) * CHUNK:(2 * bb + 1) * CHUNK],
                                      o[(2 * bb + 1) * CHUNK:(2 * bb + 2) * CHUNK]))
        for b in range(ATT_WIDTH // LANES):
            att_ref[pl.ds(r0, CHUNK), b * LANES:(b + 1) * LANES] = outs[b]

    prow = lax.broadcasted_iota(jnp.int32, (GM_CHUNK, GM_CHUNK), 0)
    pcol = lax.broadcasted_iota(jnp.int32, (GM_CHUNK, GM_CHUNK), 1)
    half = GM_CHUNK // 2
    tril = pcol <= prow
    same_blk = (prow < half) == (pcol < half)
    prow1 = lax.broadcasted_iota(jnp.int32, (GM_CHUNK, 1), 0)
    bst = bst_ref[...]
    bst_s = jnp.where(prow1 < half, bst, pltpu.roll(bst, half, 0))
    bias = jnp.where(is_sample, bst_s, bst)
    for g in range(GM_GROUPS):
        wg_ = ws_ref[g]
        w_s = jnp.where(prow < half, wg_, pltpu.roll(pltpu.roll(wg_, half, 0), half, 1))
        w_eff = jnp.where(is_sample, jnp.where(same_blk, w_s, 0.0), wg_)
        w_eff = jnp.where(tril, w_eff, 0.0).astype(BF16)
        cols = slice(g * GM_CH, (g + 1) * GM_CH)
        ug = _gelu(h[:, o_u + g * GM_CH:o_u + (g + 1) * GM_CH])
        vn = _layer_norm(_gelu(h[:, o_g + g * GM_CH:o_g + (g + 1) * GM_CH]), lng_ref[:, cols], lnb_ref[:, cols])

        if has_sample:
            @pl.when(is_sample)
            def _(cols=cols, vn=vn):
                vn_ref[:, cols] = vn

        vn_b = vn.astype(BF16)
        for n in range(tm // GM_CHUNK):
            rows = slice(n * GM_CHUNK, (n + 1) * GM_CHUNK)
            s = jnp.dot(w_eff, vn_b[rows], preferred_element_type=F32) + bias[:, g:g + 1]
            gm_ref[rows, cols] = ug[rows] * s

    mixed = jnp.concatenate([_rms_norm(att_ref[...], na_ref[...]), _rms_norm(gm_ref[...], nb_ref[...])], axis=-1)
    y = alpha * x_s[...] + jnp.dot(mixed.astype(BF16), w_o_s[...], preferred_element_type=F32)
    x1 = _layer_norm(y, l1g_ref[...], l1b_ref[...])
    x1t_ref[...] = x1
    _store_packed_tiles(x1p_ref, x1)

    nt = (((1,), (1,)), ((), ()))
    wr = wrt_ref[...]
    wr_hi = wr.astype(BF16)
    wr_lo = (wr - wr_hi.astype(F32)).astype(BF16)
    x1_hi = x1.astype(BF16)
    x1_lo = (x1 - x1_hi.astype(F32)).astype(BF16)
    logits = (lax.dot_general(wr_hi, x1_hi, nt, preferred_element_type=F32)
              + (lax.dot_general(wr_hi, x1_lo, nt, preferred_element_type=F32)
                 + lax.dot_general(wr_lo, x1_hi, nt, preferred_element_type=F32)))
    scores = jax.nn.sigmoid(logits)
    sel = scores + rb_ref[...]
    sel_rows = [sel[e:e + 1, :] for e in range(N_EXPERTS)]
    sc_rows = [scores[e:e + 1, :] for e in range(N_EXPERTS)]
    grp = []
    for g in range(N_EXPERT_GROUPS):
        r = sel_rows[g * EXPERTS_PER_GROUP:(g + 1) * EXPERTS_PER_GROUP]
        best_pair = None
        for a in range(EXPERTS_PER_GROUP):
            for b in range(a + 1, EXPERTS_PER_GROUP):
                pair = r[a] + r[b]
                best_pair = pair if best_pair is None else jnp.maximum(best_pair, pair)
        grp.append(best_pair)
    _, best = _first_index_of_max(grp)
    masked = [jnp.where(best == (e // EXPERTS_PER_GROUP), sel_rows[e], NEG_INF) for e in range(N_EXPERTS)]
    _, e0 = _first_index_of_max(masked)
    masked2 = [jnp.where(e0 == e, -jnp.inf, masked[e]) for e in range(N_EXPERTS)]
    _, e1 = _first_index_of_max(masked2)
    w0 = jnp.zeros_like(sc_rows[0])
    w1 = jnp.zeros_like(sc_rows[0])
    for e in range(N_EXPERTS):
        w0 = jnp.where(e0 == e, sc_rows[e], w0)
        w1 = jnp.where(e1 == e, sc_rows[e], w1)
    wsum = w0 + w1
    rw_ref[...] = jnp.concatenate([w0 / wsum, w1 / wsum, jnp.zeros((6, tm), F32)], axis=0)

    @pl.when(i == 0)
    def _():
        cnt_ref[...] = jnp.zeros_like(cnt_ref)

    eid = lax.broadcasted_iota(jnp.int32, (N_EXPERTS, tm), 0)
    oh0 = (eid == e0).astype(F32)
    oh1 = (eid == e1).astype(F32)
    oh = oh0 + oh1
    incl = jnp.dot(oh.astype(BF16), tri_ref[...], preferred_element_type=F32)
    before = cnt_ref[:, 0:1] + incl - oh
    r0 = jnp.sum(oh0 * before, axis=0, keepdims=True).astype(jnp.int32)
    r1 = jnp.sum(oh1 * before, axis=0, keepdims=True).astype(jnp.int32)
    cnt = cnt_ref[...] + incl[:, tm - 1:tm]
    cnt_ref[...] = cnt
    counts_ref[...] = cnt.astype(jnp.int32)
    ridx_ref[...] = jnp.concatenate([e0, e1, r0, r1, jnp.zeros((4, tm), jnp.int32)], axis=0)


def _mixer_call(layer_args, x_all, tables, n_prompt_tiles, tiles_per_seq, alpha, layer):
    (sinks, w_in, ck, cv, lng, lnb, ws, bst, na, nb, w_o, l1g, l1b, wrt, rb) = layer_args
    cos_t, sa_t, sb_t = tables

    def of_layer(shape, which=layer):
        nd = len(shape)
        return pl.BlockSpec((None,) + shape, lambda i, _nd=nd: (which,) + (0,) * _nd)

    first_layer = len(x_all) == 2
    if first_layer:
        x_main, x_tail = x_all
        t_all = x_main.shape[0] + x_tail.shape[0]
        n_tiles = t_all // TOK_TILE
        x_args = [x_main, x_tail]
        x_specs = [pl.BlockSpec((TOK_TILE, D_MODEL), lambda i: (jnp.minimum(i, n_prompt_tiles - 1), 0)),
                   pl.BlockSpec((TOK_TILE, D_MODEL), lambda i: (0, 0))]
    else:
        x1t_prev, y_pair, w_col, l2g, l2b = x_all
        t_all = x1t_prev.shape[0]
        n_tiles = t_all // TOK_TILE
        x_args = [x1t_prev, y_pair, y_pair, w_col, l2g, l2b]
        x_specs = [pl.BlockSpec((TOK_TILE, D_MODEL), lambda i: (i, 0)),
                   pl.BlockSpec((TOK_TILE * PACK_ROWS, LANES), lambda i: (i, 0)),
                   pl.BlockSpec((TOK_TILE * PACK_ROWS, LANES), lambda i: (i + n_tiles, 0)),
                   pl.BlockSpec((8, TOK_TILE), lambda i: (0, i)),
                   of_layer((1, D_MODEL), layer - 1), of_layer((1, D_MODEL), layer - 1)]
    n_seq_tiles = tiles_per_seq
    has_sample = n_tiles > n_prompt_tiles

    def const(shape):
        nd = len(shape)
        return pl.BlockSpec(shape, lambda i, _nd=nd: (0,) * _nd)

    def tab_map(i):
        return (jnp.where(i < n_prompt_tiles, i % n_seq_tiles, n_seq_tiles), 0)

    n_seq = n_prompt_tiles // n_seq_tiles

    def seq_map(i):
        return (jnp.minimum(i // n_seq_tiles, n_seq - 1), 0)

    row_blk = lambda w: pl.BlockSpec((TOK_TILE, w), lambda i: (i, 0))
    in_specs = [
        pl.BlockSpec(memory_space=pltpu.SMEM),
        *x_specs,
        pl.BlockSpec((1, D_MODEL, D_IN), lambda i: (layer, 0, 0)),
        pl.BlockSpec((TOK_TILE, LANES), tab_map),
        pl.BlockSpec((TOK_TILE, LANES), tab_map),
        pl.BlockSpec((TOK_TILE, LANES), tab_map),
        pl.BlockSpec((TOK_TILE // CHUNK, WINDOW, KV_WIDTH), lambda i: (layer, 0, 0)),
        pl.BlockSpec((TOK_TILE // CHUNK, WINDOW, KV_WIDTH), lambda i: (layer, 0, 0)),
        of_layer((1, GM_WIDTH)), of_layer((1, GM_WIDTH)),
        of_layer((GM_GROUPS, GM_CHUNK, GM_CHUNK)), of_layer((GM_CHUNK, GM_GROUPS)),
        of_layer((1, ATT_WIDTH)), of_layer((1, GM_WIDTH)),
        pl.BlockSpec((1, D_MIX, D_MODEL), lambda i: (layer, 0, 0)),
        of_layer((1, D_MODEL)), of_layer((1, D_MODEL)),
        const((N_EXPERTS, D_MODEL)), const((N_EXPERTS, 1)),
        const((TOK_TILE, TOK_TILE)),
    ]
    out_shape = [
        jax.ShapeDtypeStruct((t_all, D_MODEL), F32),
        jax.ShapeDtypeStruct((t_all * PACK_ROWS, LANES), jnp.uint32),
        jax.ShapeDtypeStruct((n_seq * WINDOW, KV_WIDTH), F32),
        jax.ShapeDtypeStruct((n_seq * WINDOW, KV_WIDTH), F32),
        jax.ShapeDtypeStruct((TOK_TILE, KV_WIDTH), F32),
        jax.ShapeDtypeStruct((TOK_TILE, KV_WIDTH), F32),
        jax.ShapeDtypeStruct((TOK_TILE, GM_WIDTH), F32),
        jax.ShapeDtypeStruct((8, t_all), jnp.int32),
        jax.ShapeDtypeStruct((8, t_all), F32),
        jax.ShapeDtypeStruct((N_EXPERTS, LANES), jnp.int32),
    ]
    out_specs = [
        pl.BlockSpec((TOK_TILE, D_MODEL), lambda i: (i, 0)),
        pl.BlockSpec((TOK_TILE * PACK_ROWS, LANES), lambda i: (i, 0)),
        pl.BlockSpec((WINDOW, KV_WIDTH), seq_map), pl.BlockSpec((WINDOW, KV_WIDTH), seq_map),
        const((TOK_TILE, KV_WIDTH)), const((TOK_TILE, KV_WIDTH)),
        const((TOK_TILE, GM_WIDTH)),
        pl.BlockSpec((8, TOK_TILE), lambda i: (0, i)),
        pl.BlockSpec((8, TOK_TILE), lambda i: (0, i)),
        const((N_EXPERTS, LANES)),
    ]
    if not has_sample:
        del out_shape[4:7], out_specs[4:7]
    kd_rows = (TOK_TILE // CHUNK) * KEYS_PER_CHUNK
    scratch = [
        pltpu.VMEM((N_KV_HEADS, kd_rows, LANES), BF16),
        pltpu.VMEM((N_KV_HEADS, kd_rows, LANES), BF16),
        pltpu.VMEM((N_Q_HEADS, TOK_TILE, LANES), BF16),
        pltpu.VMEM((TOK_TILE, ATT_WIDTH), F32),
        pltpu.VMEM((TOK_TILE, GM_WIDTH), F32),
        pltpu.VMEM((N_EXPERTS, LANES), F32),
        pltpu.VMEM((D_MODEL, D_IN), BF16),
        pltpu.VMEM((D_MIX, D_MODEL), BF16),
        pltpu.VMEM((TOK_TILE, D_MODEL), F32),
    ]
    return pl.pallas_call(
        functools.partial(_mixer_kernel, n_prompt_tiles, tiles_per_seq, alpha, has_sample, first_layer),
        grid=(n_tiles,),
        in_specs=in_specs, out_specs=out_specs, out_shape=out_shape,
        scratch_shapes=scratch,
        compiler_params=pltpu.CompilerParams(dimension_semantics=("arbitrary",), vmem_limit_bytes=VMEM_LIMIT),
        name="mixer",
    )(sinks, *x_args, w_in, cos_t, sa_t, sb_t, ck, cv, lng, lnb, ws, bst, na, nb, w_o, l1g, l1b, wrt, rb,
      jnp.triu(jnp.ones((TOK_TILE, TOK_TILE), BF16)))


def _expert_kernel(layer, te_ref, nt_ref, plan_ref, xs_ref, wg_hbm, wu_hbm, wd_hbm, out_ref,
                   wg_s, wu_s, wd_s, wg_f, wu_f, wd_f, sems):
    i = pl.program_id(0)
    plan = plan_ref[i]
    first = (plan & 1) == 1
    slot = (plan >> 1) & 1
    nxt = plan >> 2

    def weight_copies(e, s):
        return (pltpu.make_async_copy(wg_hbm.at[layer, e], wg_f.at[s], sems.at[s, 0]),
                pltpu.make_async_copy(wu_hbm.at[layer, e], wu_f.at[s], sems.at[s, 1]),
                pltpu.make_async_copy(wd_hbm.at[layer, e], wd_f.at[s], sems.at[s, 2]))

    @pl.when(i == 0)
    def _():
        for c in weight_copies(te_ref[0], slot):
            c.start()

    @pl.when(first)
    def _():
        for c in weight_copies(te_ref[i], slot):
            c.wait()

        @pl.when(nxt < N_EXPERTS)
        def _():
            for c in weight_copies(nxt, 1 - slot):
                c.start()

        wg_s[...] = wg_f[slot].astype(BF16)
        wu_s[...] = wu_f[slot].astype(BF16)
        wd_s[...] = wd_f[slot].astype(BF16)

    @pl.when(i < nt_ref[0])
    def _():
        xs = _load_packed_tiles(xs_ref, EXPERT_TILE, BF16)
        g = jnp.dot(xs, wg_s[...], preferred_element_type=F32)
        u = jnp.dot(xs, wu_s[...], preferred_element_type=F32)
        hmid = (g * jax.nn.sigmoid(g)) * u
        _store_packed_tiles(out_ref, jnp.dot(hmid.astype(BF16), wd_s[...], preferred_element_type=F32))


def _expert_call(tile_expert, n_used, tile_plan, xs, wg, wu, wd, layer):
    n_tiles = xs.shape[0] // (EXPERT_TILE * PACK_ROWS)
    tile_blk = pl.BlockSpec((EXPERT_TILE * PACK_ROWS, LANES),
                            lambda i, te, nt, plan: (jnp.minimum(i, nt[0] - 1), 0))
    grid_spec = pltpu.PrefetchScalarGridSpec(
        num_scalar_prefetch=3,
        grid=(n_tiles,),
        in_specs=[tile_blk] + [pl.BlockSpec(memory_space=pl.ANY)] * 3,
        out_specs=tile_blk,
        scratch_shapes=[
            pltpu.VMEM((D_MODEL, D_EXPERT), BF16),
            pltpu.VMEM((D_MODEL, D_EXPERT), BF16),
            pltpu.VMEM((D_EXPERT, D_MODEL), BF16),
            pltpu.VMEM((2, D_MODEL, D_EXPERT), F32),
            pltpu.VMEM((2, D_MODEL, D_EXPERT), F32),
            pltpu.VMEM((2, D_EXPERT, D_MODEL), F32),
            pltpu.SemaphoreType.DMA((2, 3)),
        ],
    )
    return pl.pallas_call(
        functools.partial(_expert_kernel, layer),
        grid_spec=grid_spec,
        out_shape=jax.ShapeDtypeStruct(xs.shape, jnp.uint32),
        compiler_params=pltpu.CompilerParams(dimension_semantics=("arbitrary",), vmem_limit_bytes=VMEM_LIMIT),
        name="experts",
    )(tile_expert, n_used, tile_plan, xs, wg, wu, wd)


SC_PIPE_CHUNK = 64
SC_PAD_CHUNK = 64


def _sc_workers():
    info = plsc.get_sparse_core_info()
    return info.num_cores, info.num_cores * info.num_subcores


def _sc_dispatch_call(pos_flat, pad_rows, x_tiles, tile_rows, n_sorted_rows):
    t_all = x_tiles.shape[0] // tile_rows
    dt = x_tiles.dtype
    nc, nw = _sc_workers()
    per_w = t_all // nw
    chunk = SC_PIPE_CHUNK
    n_ch = per_w // chunk
    tail = per_w - n_ch * chunk
    n_pad = pad_rows.shape[0]
    pad_per_w = n_pad // nw
    n_pch = pad_per_w // SC_PAD_CHUNK
    assert per_w * nw == t_all and n_pch * SC_PAD_CHUNK * nw == n_pad and n_ch % 2 == 0 and tail % 8 == 0
    zeros = jnp.zeros((SC_PAD_CHUNK, tile_rows, LANES), dt)
    tail_scratch = [] if tail == 0 else [
        pltpu.VMEM((tail,), jnp.int32), pltpu.VMEM((tail,), jnp.int32), pltpu.VMEM((tail, tile_rows, LANES), dt)]
    buffer_set = [pltpu.VMEM((chunk,), jnp.int32), pltpu.VMEM((chunk,), jnp.int32),
                  pltpu.VMEM((chunk, tile_rows, LANES), dt), pltpu.SemaphoreType.DMA, pltpu.SemaphoreType.DMA]

    @functools.partial(
        pl.kernel, mesh=plsc.VectorSubcoreMesh(core_axis_name="c", subcore_axis_name="s"),
        out_type=jax.ShapeDtypeStruct((n_sorted_rows, tile_rows, LANES), dt),
        scratch_types=buffer_set + buffer_set + [
            pltpu.VMEM((SC_PAD_CHUNK,), jnp.int32),
            pltpu.VMEM((SC_PAD_CHUNK, tile_rows, LANES), dt),
        ] + tail_scratch,
    )
    def k(pos_hbm, pad_hbm, x_hbm, z_hbm, xs_hbm, *refs):
        set_a, set_b, (ip_v, z_v), tail_refs = refs[0:5], refs[5:10], refs[10:12], refs[12:]
        wid = lax.axis_index("s") * nc + lax.axis_index("c")

        def loads(j, idx0, idx1, rows, sem_i, sem_r):
            b = pl.multiple_of(wid * per_w + j * chunk, 8)
            return (pltpu.make_async_copy(pos_hbm.at[pl.ds(b, chunk)], idx0, sem_i),
                    pltpu.make_async_copy(pos_hbm.at[pl.ds(t_all + b, chunk)], idx1, sem_i),
                    pltpu.make_async_copy(x_hbm.at[pl.ds(b, chunk)], rows, sem_r))

        def start(j, bufs):
            for c in loads(j, *bufs):
                c.start()

        def scatter(j, bufs):
            idx0, idx1, rows, sem_i, sem_r = bufs
            for c in loads(j, *bufs):
                c.wait()
            stores = (pltpu.async_copy(rows, xs_hbm.at[idx0], sem_i), pltpu.async_copy(rows, xs_hbm.at[idx1], sem_r))
            for c in stores:
                c.wait()

        start(0, set_a)

        @pl.loop(0, n_ch // 2)
        def _(p):
            j = 2 * p
            start(j + 1, set_b)
            scatter(j, set_a)

            @pl.when(j + 2 < n_ch)
            def _():
                start(j + 2, set_a)

            scatter(j + 1, set_b)

        if tail:
            idx0, idx1, rows = tail_refs
            b = pl.multiple_of(wid * per_w + n_ch * chunk, 8)
            pltpu.sync_copy(pos_hbm.at[pl.ds(b, tail)], idx0)
            pltpu.sync_copy(pos_hbm.at[pl.ds(t_all + b, tail)], idx1)
            pltpu.sync_copy(x_hbm.at[pl.ds(b, tail)], rows)
            pltpu.sync_copy(rows, xs_hbm.at[idx0])
            pltpu.sync_copy(rows, xs_hbm.at[idx1])

        pltpu.sync_copy(z_hbm, z_v)

        @pl.loop(0, n_pch)
        def _(j):
            b = pl.multiple_of(wid * pad_per_w + j * SC_PAD_CHUNK, 8)
            pltpu.sync_copy(pad_hbm.at[pl.ds(b, SC_PAD_CHUNK)], ip_v)
            pltpu.sync_copy(z_v, xs_hbm.at[ip_v])

    xs = k(pos_flat, pad_rows, x_tiles.reshape(t_all, tile_rows, LANES), zeros)
    return xs.reshape(n_sorted_rows * tile_rows, LANES)


def _sc_gather_call(pos_flat, outs, tile_rows):
    n_assign = pos_flat.shape[0]
    dt = outs.dtype
    nc, nw = _sc_workers()
    per_w = n_assign // nw
    chunk = SC_PIPE_CHUNK
    n_ch = per_w // chunk
    tail = per_w - n_ch * chunk
    assert per_w * nw == n_assign and n_ch % 2 == 0 and tail % 8 == 0
    tail_scratch = [] if tail == 0 else [pltpu.VMEM((tail,), jnp.int32), pltpu.VMEM((tail, tile_rows, LANES), dt)]

    @functools.partial(
        pl.kernel, mesh=plsc.VectorSubcoreMesh(core_axis_name="c", subcore_axis_name="s"),
        out_type=jax.ShapeDtypeStruct((n_assign, tile_rows, LANES), dt),
        scratch_types=[pltpu.VMEM((chunk,), jnp.int32), pltpu.VMEM((chunk,), jnp.int32),
                       pltpu.VMEM((chunk, tile_rows, LANES), dt), pltpu.VMEM((chunk, tile_rows, LANES), dt),
                       pltpu.SemaphoreType.DMA, pltpu.SemaphoreType.DMA] + tail_scratch,
    )
    def k(pos_hbm, o_hbm, y_hbm, idx_a, idx_b, rows_a, rows_b, sem_a, sem_b, *tail_refs):
        wid = lax.axis_index("s") * nc + lax.axis_index("c")

        def row0(j):
            return pl.multiple_of(wid * per_w + j * chunk, 8)

        def start(j, idx, rows, sem):
            pltpu.sync_copy(pos_hbm.at[pl.ds(row0(j), chunk)], idx)
            pltpu.async_copy(o_hbm.at[idx], rows, sem)

        def finish(j, idx, rows, sem):
            pltpu.make_async_copy(o_hbm.at[idx], rows, sem).wait()
            pltpu.sync_copy(rows, y_hbm.at[pl.ds(row0(j), chunk)])

        start(0, idx_a, rows_a, sem_a)

        @pl.loop(0, n_ch // 2)
        def _(p):
            j = 2 * p
            start(j + 1, idx_b, rows_b, sem_b)
            finish(j, idx_a, rows_a, sem_a)

            @pl.when(j + 2 < n_ch)
            def _():
                start(j + 2, idx_a, rows_a, sem_a)

            finish(j + 1, idx_b, rows_b, sem_b)

        if tail:
            idx_t, rows_t = tail_refs
            b = pl.multiple_of(wid * per_w + n_ch * chunk, 8)
            pltpu.sync_copy(pos_hbm.at[pl.ds(b, tail)], idx_t)
            pltpu.sync_copy(o_hbm.at[idx_t], rows_t)
            pltpu.sync_copy(rows_t, y_hbm.at[pl.ds(b, tail)])

    y = k(pos_flat, outs.reshape(-1, tile_rows, LANES))
    return y.reshape(n_assign * tile_rows, LANES)


def _combine_dense_kernel(alpha, n_main_tiles, x1t_ref, y0_ref, y1_ref, w_ref, g_ref, b_ref, out_ref, *tail):
    j = pl.program_id(0)
    w = w_ref[...].T
    y = (w[:, 0:1] * _load_packed_tiles(y0_ref, TOK_TILE, F32)
         + w[:, 1:2] * _load_packed_tiles(y1_ref, TOK_TILE, F32))
    res = _layer_norm(alpha * x1t_ref[...] + y, g_ref[...], b_ref[...])
    if n_main_tiles is None:
        out_ref[...] = res
    else:
        @pl.when(j < n_main_tiles)
        def _():
            out_ref[...] = res

        @pl.when(j >= n_main_tiles)
        def _():
            tail[0][...] = res


def _combine_dense_call(x1t, y_pair, w_col, g, b, alpha, n_main_tiles=None):
    t_all = x1t.shape[0]
    n_tiles = t_all // TOK_TILE
    if n_main_tiles is None:
        out_specs = pl.BlockSpec((TOK_TILE, D_MODEL), lambda i: (i, 0))
        out_shape = jax.ShapeDtypeStruct((t_all, D_MODEL), F32)
    else:
        assert n_tiles == n_main_tiles + 1
        out_specs = [pl.BlockSpec((TOK_TILE, D_MODEL), lambda i: (jnp.minimum(i, n_main_tiles - 1), 0)),
                     pl.BlockSpec((TOK_TILE, D_MODEL), lambda i: (0, 0))]
        out_shape = [jax.ShapeDtypeStruct((n_main_tiles * TOK_TILE, D_MODEL), F32),
                     jax.ShapeDtypeStruct((TOK_TILE, D_MODEL), F32)]
    tile = (TOK_TILE, D_MODEL)
    packed = (TOK_TILE * PACK_ROWS, LANES)
    return pl.pallas_call(
        functools.partial(_combine_dense_kernel, alpha, n_main_tiles),
        grid=(n_tiles,),
        in_specs=[
            pl.BlockSpec(tile, lambda i: (i, 0)),
            pl.BlockSpec(packed, lambda i: (i, 0)),
            pl.BlockSpec(packed, lambda i: (i + n_tiles, 0)),
            pl.BlockSpec((8, TOK_TILE), lambda i: (0, i)),
            pl.BlockSpec((1, D_MODEL), lambda i: (0, 0)),
            pl.BlockSpec((1, D_MODEL), lambda i: (0, 0)),
        ],
        out_specs=out_specs,
        out_shape=out_shape,
        compiler_params=pltpu.CompilerParams(dimension_semantics=("arbitrary",), vmem_limit_bytes=VMEM_LIMIT),
        name="combine",
    )(x1t, y_pair, y_pair, w_col, g, b)


def _rope_tables(seq, dec_seq, past_len):
    half = HEAD_DIM // 2
    inv = ROPE_THETA ** (-jnp.arange(half, dtype=F32) / half)
    pos = jnp.concatenate([jnp.arange(seq), past_len + (jnp.arange(TOK_TILE) % dec_seq)])
    ang = pos.astype(F32)[:, None] * inv[None, :]
    cos = jnp.cos(ang)
    sin = jnp.sin(ang)
    zero = jnp.zeros_like(sin)
    reps = LANES // HEAD_DIM
    cos_t = jnp.tile(jnp.concatenate([cos, cos], -1), (1, reps))
    sa_t = jnp.tile(jnp.concatenate([-sin, zero], -1), (1, reps))
    sb_t = jnp.tile(jnp.concatenate([zero, sin], -1), (1, reps))
    return cos_t, sa_t, sb_t


def _inclusive_cumsum(v):
    n = v.shape[0]
    tri = jnp.arange(n)[None, :] <= jnp.arange(n)[:, None]
    return jnp.sum(jnp.where(tri, v[None, :], 0), axis=1)


def _dispatch_plan(ridx, counts, n_tiles):
    t_all = ridx.shape[1]
    tiles_e = (counts + EXPERT_TILE - 1) // EXPERT_TILE
    tile_end = _inclusive_cumsum(tiles_e)
    offs = (tile_end - tiles_e) * EXPERT_TILE
    n_used = tile_end[-1:]
    tile_expert = jnp.minimum(
        jnp.sum((jnp.arange(n_tiles)[:, None] >= tile_end[None, :]).astype(jnp.int32), axis=1), N_EXPERTS - 1)
    experts = jnp.arange(N_EXPERTS)
    used = tiles_e > 0
    slot_e = (_inclusive_cumsum(used.astype(jnp.int32)) - 1) % 2
    next_e = jnp.min(jnp.where((experts[None, :] > experts[:, None]) & used[None, :], experts[None, :], N_EXPERTS),
                     axis=1)
    tiles = jnp.arange(n_tiles)
    of_tile = tile_expert[:, None] == experts[None, :]
    pick = lambda v: jnp.sum(jnp.where(of_tile, v[None, :], 0), axis=1)
    first = (tiles == pick(tile_end - tiles_e)) & (tiles < n_used)
    tile_plan = first.astype(jnp.int32) + 2 * pick(slot_e) + 4 * pick(next_e)
    pos = ridx[TOP_K:2 * TOP_K] + jnp.sum(
        jnp.where(ridx[:TOP_K, :, None] == experts[None, None, :], offs[None, None, :], 0), axis=-1)
    gap_start = jnp.concatenate([offs + counts, n_used * EXPERT_TILE])
    gap_len = jnp.concatenate([tiles_e * EXPERT_TILE - counts, (n_tiles - n_used) * EXPERT_TILE])
    gap_end_q = _inclusive_cumsum(gap_len)
    q = jnp.arange(n_tiles * EXPERT_TILE - TOP_K * t_all)
    gap = jnp.sum((q[:, None] >= gap_end_q[None, :]).astype(jnp.int32), axis=1)
    in_gap = gap[:, None] == jnp.arange(N_EXPERTS + 1)[None, :]
    pad_rows = q + jnp.sum(jnp.where(in_gap, (gap_start - gap_end_q + gap_len)[None, :], 0), axis=1)
    return (pos.reshape(-1).astype(jnp.int32), pad_rows.astype(jnp.int32),
            tile_expert.astype(jnp.int32), n_used.astype(jnp.int32), tile_plan.astype(jnp.int32))


def kernel(x_prompt, x_sample, cache_k, cache_v, w_in, sinks, gm_ln_g, gm_ln_b, gm_ws, gm_bs,
           out_norm_a, out_norm_b, w_o, ln1_g, ln1_b, w_router, router_bias,
           w_gate, w_up, w_down, ln2_g, ln2_b):
    batch, seq, _ = x_prompt.shape
    dec_batch, dec_seq, _ = x_sample.shape
    depth = w_in.shape[0]
    past_len = PAST_LEN
    assert dec_batch * dec_seq == TOK_TILE and dec_seq == CHUNK and seq % TOK_TILE == 0
    assert cache_k.shape[2] == WINDOW
    alpha = (2 * depth) ** 0.25
    tiles_per_seq = seq // TOK_TILE

    tables = _rope_tables(seq, dec_seq, past_len)
    wrt = w_router.T
    rb = router_bias.reshape(N_EXPERTS, 1)
    cache_k2 = cache_k.reshape(depth * dec_batch, WINDOW, KV_WIDTH)
    cache_v2 = cache_v.reshape(depth * dec_batch, WINDOW, KV_WIDTH)
    as_rows = lambda p: p.reshape(depth, 1, -1)
    lng3, lnb3, na3, nb3 = as_rows(gm_ln_g), as_rows(gm_ln_b), as_rows(out_norm_a), as_rows(out_norm_b)
    l1g3, l1b3, l2g3, l2b3 = as_rows(ln1_g), as_rows(ln1_b), as_rows(ln2_g), as_rows(ln2_b)
    bst3 = jnp.swapaxes(gm_bs, 1, 2)

    streams = [
        dict(x=(x_prompt.reshape(batch * seq, D_MODEL), x_sample.reshape(-1, D_MODEL)),
             n_seq=batch, has_sample=True),
    ]
    kp, vp, ks, vs, gms = [], [], [], [], []
    for l in range(depth):
        layer_args = (sinks[l], w_in, cache_k2, cache_v2, lng3, lnb3, gm_ws, bst3, na3, nb3, w_o, l1g3, l1b3, wrt, rb)
        k_tails, v_tails = [], []
        for st in streams:
            n_prompt = st["n_seq"] * seq
            n_prompt_tiles = n_prompt // TOK_TILE
            t_all = n_prompt + (TOK_TILE if st["has_sample"] else 0)
            n_exp_tiles = (TOP_K * t_all) // EXPERT_TILE + N_EXPERTS
            res = _mixer_call(layer_args, st["x"], tables, n_prompt_tiles, tiles_per_seq, alpha, l)
            if st["has_sample"]:
                x1t, x1p, k_tail, v_tail, k_s, v_s, vn_s, ridx, rw, counts = res
            else:
                x1t, x1p, k_tail, v_tail, ridx, rw, counts = res

            pos, pad_rows, tile_expert, n_used, tile_plan = _dispatch_plan(ridx, counts[:, 0], n_exp_tiles)
            xs = _sc_dispatch_call(pos, pad_rows, x1p, PACK_ROWS, n_exp_tiles * EXPERT_TILE)
            outs = _expert_call(tile_expert, n_used, tile_plan, xs, w_gate, w_up, w_down, l)
            y_pair = _sc_gather_call(pos, outs, PACK_ROWS)
            if l < depth - 1:
                st["x"] = (x1t, y_pair, rw, l2g3, l2b3)
            else:
                st["x"] = _combine_dense_call(x1t, y_pair, rw, l2g3[l], l2b3[l], alpha,
                                              n_main_tiles=n_prompt_tiles if st["has_sample"] else None)

            k_tails.append(k_tail)
            v_tails.append(v_tail)
            if st["has_sample"]:
                ks.append(k_s)
                vs.append(v_s)
                gms.append(vn_s)
        kp.append(jnp.concatenate(k_tails, axis=0))
        vp.append(jnp.concatenate(v_tails, axis=0))

    y_prompt = jnp.concatenate([st["x"][0] if st["has_sample"] else st["x"] for st in streams],
                               axis=0).reshape(batch, seq, D_MODEL)
    y_sample = streams[-1]["x"][1].reshape(dec_batch, dec_seq, D_MODEL)
    heads = (N_KV_HEADS, HEAD_DIM)
    return (y_prompt, y_sample,
            jnp.stack(kp).reshape(depth, batch, WINDOW, *heads), jnp.stack(vp).reshape(depth, batch, WINDOW, *heads),
            jnp.stack(ks).reshape(depth, dec_batch, dec_seq, *heads),
            jnp.stack(vs).reshape(depth, dec_batch, dec_seq, *heads),
            jnp.stack(gms).reshape(depth, dec_batch, dec_seq, GM_WIDTH))
```

```python
import functools

import jax
import jax.numpy as jnp
from jax import lax
from jax.experimental import pallas as pl
from jax.experimental.pallas import tpu as pltpu
from jax.experimental.pallas import tpu_sc as plsc

D_MODEL = 1024
CHUNK = 64
WINDOW = 128
HEAD_DIM = 64
N_Q_HEADS = 8
N_KV_HEADS = 2
Q_PER_KV = N_Q_HEADS // N_KV_HEADS
PAST_LEN = 2048
ATT_WIDTH = N_Q_HEADS * HEAD_DIM
KV_WIDTH = N_KV_HEADS * HEAD_DIM
ROPE_THETA = 10000.0
GM_GROUPS = 4
GM_CH = 128
GM_WIDTH = GM_GROUPS * GM_CH
GM_CHUNK = 128
D_MIX = ATT_WIDTH + GM_WIDTH
D_IN = ATT_WIDTH + 2 * KV_WIDTH + 2 * GM_WIDTH
N_EXPERTS = 16
N_EXPERT_GROUPS = 4
EXPERTS_PER_GROUP = N_EXPERTS // N_EXPERT_GROUPS
TOP_K = 2
D_EXPERT = 512
LN_EPS = 1e-5
NEG_INF = -1e30

LANES = 128
PACK_ROWS = D_MODEL // (2 * LANES)
HI_HALF = 0xFFFF0000
TOK_TILE = 512
EXPERT_TILE = 512
KEYS_PER_CHUNK = WINDOW + CHUNK
VMEM_LIMIT = 60 * 1024 * 1024

F32 = jnp.float32
BF16 = jnp.bfloat16


def _layer_norm(x, g, b):
    mu = jnp.mean(x, axis=-1, keepdims=True)
    d = x - mu
    var = jnp.mean(d * d, axis=-1, keepdims=True)
    return d * lax.rsqrt(var + LN_EPS) * g + b


def _rms_norm(x, g):
    return x * lax.rsqrt(jnp.mean(x * x, axis=-1, keepdims=True) + LN_EPS) * g


def _gelu(x):
    return 0.5 * x * (1.0 + lax.erf(x * (0.5 ** 0.5)))


def _bf16_bits(x):
    return lax.bitcast_convert_type(x.astype(BF16).astype(F32), jnp.uint32)


def _store_packed_tiles(ref, x):
    n = x.shape[0]
    for s in range(PACK_ROWS):
        lo = _bf16_bits(x[:, (2 * s) * LANES:(2 * s + 1) * LANES]) >> 16
        hi = _bf16_bits(x[:, (2 * s + 1) * LANES:(2 * s + 2) * LANES]) & jnp.uint32(HI_HALF)
        ref[pl.ds(s, n, stride=PACK_ROWS), :] = hi | lo


def _load_packed_tiles(ref, n, dtype):
    cols = []
    for s in range(PACK_ROWS):
        w = ref[pl.ds(s, n, stride=PACK_ROWS), :]
        cols.append(lax.bitcast_convert_type(w << 16, F32).astype(dtype))
        cols.append(lax.bitcast_convert_type(w & jnp.uint32(HI_HALF), F32).astype(dtype))
    return jnp.concatenate(cols, axis=-1)


def _first_index_of_max(rows):
    m = rows[0]
    for r in rows[1:]:
        m = jnp.maximum(m, r)
    idx = jnp.full(m.shape, len(rows), jnp.int32)
    for e in reversed(range(len(rows))):
        idx = jnp.where(rows[e] == m, e, idx)
    return m, idx


def _mixer_kernel(n_prompt_tiles, tiles_per_seq, alpha, has_sample, first_layer, sinks_ref, *refs):
    n_x = 2 if first_layer else 6
    x_refs, refs = refs[:n_x], refs[n_x:]
    (w_in_ref, cos_ref, sa_ref, sb_ref, ck_ref, cv_ref, lng_ref, lnb_ref, ws_ref, bst_ref, na_ref, nb_ref, w_o_ref,
     l1g_ref, l1b_ref, wrt_ref, rb_ref, tri_ref, x1t_ref, x1p_ref, kt_ref, vt_ref) = refs[:22]
    rest = refs[22:]
    if has_sample:
        (ks_ref, vs_ref, vn_ref, ridx_ref, rw_ref, counts_ref,
         kd_ref, vd_ref, qm_ref, att_ref, gm_ref, cnt_ref, w_in_s, w_o_s, x_s) = rest
    else:
        (ridx_ref, rw_ref, counts_ref,
         kd_ref, vd_ref, qm_ref, att_ref, gm_ref, cnt_ref, w_in_s, w_o_s, x_s) = rest
    i = pl.program_id(0)
    is_sample = i >= n_prompt_tiles
    seq_start = jnp.logical_and(jnp.logical_not(is_sample), (i % tiles_per_seq) == 0)
    seq_end = jnp.logical_and(jnp.logical_not(is_sample), (i % tiles_per_seq) == tiles_per_seq - 1)
    tm = TOK_TILE

    @pl.when(i == 0)
    def _():
        w_in_s[...] = w_in_ref[0].astype(BF16)
        w_o_s[...] = w_o_ref[0].astype(BF16)

    if first_layer:
        xp_ref, xs_ref = x_refs
        x_s[...] = jnp.where(is_sample, xs_ref[...], xp_ref[...])
    else:
        x1_prev_ref, y0_ref, y1_ref, wc_ref, l2g_ref, l2b_ref = x_refs
        wc = wc_ref[...].T
        y_moe = (wc[:, 0:1] * _load_packed_tiles(y0_ref, tm, F32)
                 + wc[:, 1:2] * _load_packed_tiles(y1_ref, tm, F32))
        x_s[...] = _layer_norm(alpha * x1_prev_ref[...] + y_moe, l2g_ref[...], l2b_ref[...])

    h = jnp.dot(x_s[...].astype(BF16), w_in_s[...], preferred_element_type=F32)

    cos = cos_ref[...]
    sa = sa_ref[...]
    sb = sb_ref[...]

    def rope(blk):
        return blk * cos + pltpu.roll(blk, LANES - HEAD_DIM // 2, 1) * sa + pltpu.roll(blk, HEAD_DIM // 2, 1) * sb

    lane = lax.broadcasted_iota(jnp.int32, (1, LANES), 1)
    lo_half = lane < HEAD_DIM

    o_k = ATT_WIDTH
    o_v = o_k + KV_WIDTH
    o_u = o_v + KV_WIDTH
    o_g = o_u + GM_WIDTH

    k_rot = rope(h[:, o_k:o_v])
    v_new = h[:, o_v:o_u]

    @pl.when(seq_end)
    def _():
        kt_ref[...] = k_rot[tm - WINDOW:]
        vt_ref[...] = v_new[tm - WINDOW:]

    if has_sample:
        @pl.when(is_sample)
        def _():
            ks_ref[...] = k_rot
            vs_ref[...] = v_new

    def dup_heads(a):
        sw = pltpu.roll(a, HEAD_DIM, 1)
        return jnp.where(lo_half, a, sw).astype(BF16), jnp.where(lo_half, sw, a).astype(BF16)

    k_d = dup_heads(k_rot)
    v_d = dup_heads(v_new)

    @pl.when(jnp.logical_not(is_sample))
    def _():
        @pl.when(seq_start)
        def _():
            for g in range(N_KV_HEADS):
                kd_ref[g, 0:WINDOW, :] = jnp.zeros((WINDOW, LANES), BF16)
                vd_ref[g, 0:WINDOW, :] = jnp.zeros((WINDOW, LANES), BF16)

        @pl.when(jnp.logical_not(seq_start))
        def _():
            for g in range(N_KV_HEADS):
                kd_ref[g, 0:WINDOW, :] = kd_ref[g, tm:tm + WINDOW, :]
                vd_ref[g, 0:WINDOW, :] = vd_ref[g, tm:tm + WINDOW, :]

        for g in range(N_KV_HEADS):
            kd_ref[g, WINDOW:WINDOW + tm, :] = k_d[g]
            vd_ref[g, WINDOW:WINDOW + tm, :] = v_d[g]

    @pl.when(is_sample)
    def _():
        for b in range(tm // CHUNK):
            ck = dup_heads(ck_ref[b])
            cv = dup_heads(cv_ref[b])
            base = b * KEYS_PER_CHUNK
            for g in range(N_KV_HEADS):
                kd_ref[g, base:base + WINDOW, :] = ck[g]
                vd_ref[g, base:base + WINDOW, :] = cv[g]
                kd_ref[g, base + WINDOW:base + KEYS_PER_CHUNK, :] = k_d[g][b * CHUNK:(b + 1) * CHUNK]
                vd_ref[g, base + WINDOW:base + KEYS_PER_CHUNK, :] = v_d[g][b * CHUNK:(b + 1) * CHUNK]

    scale = HEAD_DIM ** -0.5
    for b in range(ATT_WIDTH // LANES):
        qb = (rope(h[:, b * LANES:(b + 1) * LANES]) * scale).astype(BF16)
        zero = jnp.zeros_like(qb)
        qm_ref[2 * b] = jnp.where(lo_half, qb, zero)
        qm_ref[2 * b + 1] = jnp.where(lo_half, zero, qb)

    key_stride = jnp.where(is_sample, KEYS_PER_CHUNK, CHUNK)
    col = lax.broadcasted_iota(jnp.int32, (1, KEYS_PER_CHUNK), 1)
    row_blk = lax.broadcasted_iota(jnp.int32, (Q_PER_KV * CHUNK, 1), 0) // CHUNK
    sink_cols = [
        jnp.where(row_blk == 0, sinks_ref[Q_PER_KV * g],
                  jnp.where(row_blk == 1, sinks_ref[Q_PER_KV * g + 1],
                            jnp.where(row_blk == 2, sinks_ref[Q_PER_KV * g + 2], sinks_ref[Q_PER_KV * g + 3])))
        for g in range(N_KV_HEADS)]

    for c in range(tm // CHUNK):
        r0 = c * CHUNK
        k0 = pl.multiple_of(c * key_stride, CHUNK)
        outs = []
        for g in range(N_KV_HEADS):
            q_st = jnp.concatenate([qm_ref[Q_PER_KV * g + r, pl.ds(r0, CHUNK), :] for r in range(Q_PER_KV)],
                                   axis=0)
            keys = kd_ref[g, pl.ds(k0, KEYS_PER_CHUNK), :]
            vals = vd_ref[g, pl.ds(k0, KEYS_PER_CHUNK), :]
            s = lax.dot_general(q_st, keys, (((1,), (1,)), ((), ())), preferred_element_type=F32)
            if c < WINDOW // CHUNK:
                first_valid = jnp.where(seq_start, (WINDOW // CHUNK - c) * CHUNK, 0)
                s = jnp.where(col >= first_valid, s, NEG_INF)
            sink = sink_cols[g]
            m = jnp.maximum(jnp.max(s, axis=-1, keepdims=True), sink)
            p = jnp.exp(s - m)
            denom = jnp.sum(p, axis=-1, keepdims=True) + jnp.exp(sink - m)
            o = jnp.dot(p.astype(BF16), vals, preferred_element_type=F32) * (1.0 / denom)
            for bb in range(2):
                outs.append(jnp.where(lo_half, o[(2 * bb) * CHUNK:(2 * bb + 1) * CHUNK],
                                      o[(2 * bb + 1) * CHUNK:(2 * bb + 2) * CHUNK]))
        for b in range(ATT_WIDTH // LANES):
            att_ref[pl.ds(r0, CHUNK), b * LANES:(b + 1) * LANES] = outs[b]

    prow = lax.broadcasted_iota(jnp.int32, (GM_CHUNK, GM_CHUNK), 0)
    pcol = lax.broadcasted_iota(jnp.int32, (GM_CHUNK, GM_CHUNK), 1)
    half = GM_CHUNK // 2
    tril = pcol <= prow
    same_blk = (prow < half) == (pcol < half)
    prow1 = lax.broadcasted_iota(jnp.int32, (GM_CHUNK, 1), 0)
    bst = bst_ref[...]
    bst_s = jnp.where(prow1 < half, bst, pltpu.roll(bst, half, 0))
    bias = jnp.where(is_sample, bst_s, bst)
    for g in range(GM_GROUPS):
        wg_ = ws_ref[g]
        w_s = jnp.where(prow < half, wg_, pltpu.roll(pltpu.roll(wg_, half, 0), half, 1))
        w_eff = jnp.where(is_sample, jnp.where(same_blk, w_s, 0.0), wg_)
        w_eff = jnp.where(tril, w_eff, 0.0).astype(BF16)
        cols = slice(g * GM_CH, (g + 1) * GM_CH)
        ug = _gelu(h[:, o_u + g * GM_CH:o_u + (g + 1) * GM_CH])
        vn = _layer_norm(_gelu(h[:, o_g + g * GM_CH:o_g + (g + 1) * GM_CH]), lng_ref[:, cols], lnb_ref[:, cols])

        if has_sample:
            @pl.when(is_sample)
            def _(cols=cols, vn=vn):
                vn_ref[:, cols] = vn

        vn_b = vn.astype(BF16)
        for n in range(tm // GM_CHUNK):
            rows = slice(n * GM_CHUNK, (n + 1) * GM_CHUNK)
            s = jnp.dot(w_eff, vn_b[rows], preferred_element_type=F32) + bias[:, g:g + 1]
            gm_ref[rows, cols] = ug[rows] * s

    mixed = jnp.concatenate([_rms_norm(att_ref[...], na_ref[...]), _rms_norm(gm_ref[...], nb_ref[...])], axis=-1)
    y = alpha * x_s[...] + jnp.dot(mixed.astype(BF16), w_o_s[...], preferred_element_type=F32)
    x1 = _layer_norm(y, l1g_ref[...], l1b_ref[...])
    x1t_ref[...] = x1
    _store_packed_tiles(x1p_ref, x1)

    nt = (((1,), (1,)), ((), ()))
    wr = wrt_ref[...]
    wr_hi = wr.astype(BF16)
    wr_lo = (wr - wr_hi.astype(F32)).astype(BF16)
    x1_hi = x1.astype(BF16)
    x1_lo = (x1 - x1_hi.astype(F32)).astype(BF16)
    logits = (lax.dot_general(wr_hi, x1_hi, nt, preferred_element_type=F32)
              + (lax.dot_general(wr_hi, x1_lo, nt, preferred_element_type=F32)
                 + lax.dot_general(wr_lo, x1_hi, nt, preferred_element_type=F32)))
    scores = jax.nn.sigmoid(logits)
    sel = scores + rb_ref[...]
    sel_rows = [sel[e:e + 1, :] for e in range(N_EXPERTS)]
    sc_rows = [scores[e:e + 1, :] for e in range(N_EXPERTS)]
    grp = []
    for g in range(N_EXPERT_GROUPS):
        r = sel_rows[g * EXPERTS_PER_GROUP:(g + 1) * EXPERTS_PER_GROUP]
        best_pair = None
        for a in range(EXPERTS_PER_GROUP):
            for b in range(a + 1, EXPERTS_PER_GROUP):
                pair = r[a] + r[b]
                best_pair = pair if best_pair is None else jnp.maximum(best_pair, pair)
        grp.append(best_pair)
    _, best = _first_index_of_max(grp)
    masked = [jnp.where(best == (e // EXPERTS_PER_GROUP), sel_rows[e], NEG_INF) for e in range(N_EXPERTS)]
    _, e0 = _first_index_of_max(masked)
    masked2 = [jnp.where(e0 == e, -jnp.inf, masked[e]) for e in range(N_EXPERTS)]
    _, e1 = _first_index_of_max(masked2)
    w0 = jnp.zeros_like(sc_rows[0])
    w1 = jnp.zeros_like(sc_rows[0])
    for e in range(N_EXPERTS):
        w0 = jnp.where(e0 == e, sc_rows[e], w0)
        w1 = jnp.where(e1 == e, sc_rows[e], w1)
    wsum = w0 + w1
    rw_ref[...] = jnp.concatenate([w0 / wsum, w1 / wsum, jnp.zeros((6, tm), F32)], axis=0)

    @pl.when(i == 0)
    def _():
        cnt_ref[...] = jnp.zeros_like(cnt_ref)

    eid = lax.broadcasted_iota(jnp.int32, (N_EXPERTS, tm), 0)
    oh0 = (eid == e0).astype(F32)
    oh1 = (eid == e1).astype(F32)
    oh = oh0 + oh1
    incl = jnp.dot(oh.astype(BF16), tri_ref[...], preferred_element_type=F32)
    before = cnt_ref[:, 0:1] + incl - oh
    r0 = jnp.sum(oh0 * before, axis=0, keepdims=True).astype(jnp.int32)
    r1 = jnp.sum(oh1 * before, axis=0, keepdims=True).astype(jnp.int32)
    cnt = cnt_ref[...] + incl[:, tm - 1:tm]
    cnt_ref[...] = cnt
    counts_ref[...] = cnt.astype(jnp.int32)
    ridx_ref[...] = jnp.concatenate([e0, e1, r0, r1, jnp.zeros((4, tm), jnp.int32)], axis=0)


def _mixer_call(layer_args, x_all, tables, n_prompt_tiles, tiles_per_seq, alpha, layer):
    (sinks, w_in, ck, cv, lng, lnb, ws, bst, na, nb, w_o, l1g, l1b, wrt, rb) = layer_args
    cos_t, sa_t, sb_t = tables

    def of_layer(shape, which=layer):
        nd = len(shape)
        return pl.BlockSpec((None,) + shape, lambda i, _nd=nd: (which,) + (0,) * _nd)

    first_layer = len(x_all) == 2
    if first_layer:
        x_main, x_tail = x_all
        t_all = x_main.shape[0] + x_tail.shape[0]
        n_tiles = t_all // TOK_TILE
        x_args = [x_main, x_tail]
        x_specs = [pl.BlockSpec((TOK_TILE, D_MODEL), lambda i: (jnp.minimum(i, n_prompt_tiles - 1), 0)),
                   pl.BlockSpec((TOK_TILE, D_MODEL), lambda i: (0, 0))]
    else:
        x1t_prev, y_pair, w_col, l2g, l2b = x_all
        t_all = x1t_prev.shape[0]
        n_tiles = t_all // TOK_TILE
        x_args = [x1t_prev, y_pair, y_pair, w_col, l2g, l2b]
        x_specs = [pl.BlockSpec((TOK_TILE, D_MODEL), lambda i: (i, 0)),
                   pl.BlockSpec((TOK_TILE * PACK_ROWS, LANES), lambda i: (i, 0)),
                   pl.BlockSpec((TOK_TILE * PACK_ROWS, LANES), lambda i: (i + n_tiles, 0)),
                   pl.BlockSpec((8, TOK_TILE), lambda i: (0, i)),
                   of_layer((1, D_MODEL), layer - 1), of_layer((1, D_MODEL), layer - 1)]
    n_seq_tiles = tiles_per_seq
    has_sample = n_tiles > n_prompt_tiles

    def const(shape):
        nd = len(shape)
        return pl.BlockSpec(shape, lambda i, _nd=nd: (0,) * _nd)

    def tab_map(i):
        return (jnp.where(i < n_prompt_tiles, i % n_seq_tiles, n_seq_tiles), 0)

    n_seq = n_prompt_tiles // n_seq_tiles

    def seq_map(i):
        return (jnp.minimum(i // n_seq_tiles, n_seq - 1), 0)

    row_blk = lambda w: pl.BlockSpec((TOK_TILE, w), lambda i: (i, 0))
    in_specs = [
        pl.BlockSpec(memory_space=pltpu.SMEM),
        *x_specs,
        pl.BlockSpec((1, D_MODEL, D_IN), lambda i: (layer, 0, 0)),
        pl.BlockSpec((TOK_TILE, LANES), tab_map),
        pl.BlockSpec((TOK_TILE, LANES), tab_map),
        pl.BlockSpec((TOK_TILE, LANES), tab_map),
        pl.BlockSpec((TOK_TILE // CHUNK, WINDOW, KV_WIDTH), lambda i: (layer, 0, 0)),
        pl.BlockSpec((TOK_TILE // CHUNK, WINDOW, KV_WIDTH), lambda i: (layer, 0, 0)),
        of_layer((1, GM_WIDTH)), of_layer((1, GM_WIDTH)),
        of_layer((GM_GROUPS, GM_CHUNK, GM_CHUNK)), of_layer((GM_CHUNK, GM_GROUPS)),
        of_layer((1, ATT_WIDTH)), of_layer((1, GM_WIDTH)),
        pl.BlockSpec((1, D_MIX, D_MODEL), lambda i: (layer, 0, 0)),
        of_layer((1, D_MODEL)), of_layer((1, D_MODEL)),
        const((N_EXPERTS, D_MODEL)), const((N_EXPERTS, 1)),
        const((TOK_TILE, TOK_TILE)),
    ]
    out_shape = [
        jax.ShapeDtypeStruct((t_all, D_MODEL), F32),
        jax.ShapeDtypeStruct((t_all * PACK_ROWS, LANES), jnp.uint32),
        jax.ShapeDtypeStruct((n_seq * WINDOW, KV_WIDTH), F32),
        jax.ShapeDtypeStruct((n_seq * WINDOW, KV_WIDTH), F32),
        jax.ShapeDtypeStruct((TOK_TILE, KV_WIDTH), F32),
        jax.ShapeDtypeStruct((TOK_TILE, KV_WIDTH), F32),
        jax.ShapeDtypeStruct((TOK_TILE, GM_WIDTH), F32),
        jax.ShapeDtypeStruct((8, t_all), jnp.int32),
        jax.ShapeDtypeStruct((8, t_all), F32),
        jax.ShapeDtypeStruct((N_EXPERTS, LANES), jnp.int32),
    ]
    out_specs = [
        pl.BlockSpec((TOK_TILE, D_MODEL), lambda i: (i, 0)),
        pl.BlockSpec((TOK_TILE * PACK_ROWS, LANES), lambda i: (i, 0)),
        pl.BlockSpec((WINDOW, KV_WIDTH), seq_map), pl.BlockSpec((WINDOW, KV_WIDTH), seq_map),
        const((TOK_TILE, KV_WIDTH)), const((TOK_TILE, KV_WIDTH)),
        const((TOK_TILE, GM_WIDTH)),
        pl.BlockSpec((8, TOK_TILE), lambda i: (0, i)),
        pl.BlockSpec((8, TOK_TILE), lambda i: (0, i)),
        const((N_EXPERTS, LANES)),
    ]
    if not has_sample:
        del out_shape[4:7], out_specs[4:7]
    kd_rows = (TOK_TILE // CHUNK) * KEYS_PER_CHUNK
    scratch = [
        pltpu.VMEM((N_KV_HEADS, kd_rows, LANES), BF16),
        pltpu.VMEM((N_KV_HEADS, kd_rows, LANES), BF16),
        pltpu.VMEM((N_Q_HEADS, TOK_TILE, LANES), BF16),
        pltpu.VMEM((TOK_TILE, ATT_WIDTH), F32),
        pltpu.VMEM((TOK_TILE, GM_WIDTH), F32),
        pltpu.VMEM((N_EXPERTS, LANES), F32),
        pltpu.VMEM((D_MODEL, D_IN), BF16),
        pltpu.VMEM((D_MIX, D_MODEL), BF16),
        pltpu.VMEM((TOK_TILE, D_MODEL), F32),
    ]
    return pl.pallas_call(
        functools.partial(_mixer_kernel, n_prompt_tiles, tiles_per_seq, alpha, has_sample, first_layer),
        grid=(n_tiles,),
        in_specs=in_specs, out_specs=out_specs, out_shape=out_shape,
        scratch_shapes=scratch,
        compiler_params=pltpu.CompilerParams(dimension_semantics=("arbitrary",), vmem_limit_bytes=VMEM_LIMIT),
        name="mixer",
    )(sinks, *x_args, w_in, cos_t, sa_t, sb_t, ck, cv, lng, lnb, ws, bst, na, nb, w_o, l1g, l1b, wrt, rb,
      jnp.triu(jnp.ones((TOK_TILE, TOK_TILE), BF16)))


def _expert_kernel(layer, te_ref, nt_ref, plan_ref, xs_ref, wg_hbm, wu_hbm, wd_hbm, out_ref,
                   wg_s, wu_s, wd_s, wg_f, wu_f, wd_f, sems):
    i = pl.program_id(0)
    plan = plan_ref[i]
    first = (plan & 1) == 1
    slot = (plan >> 1) & 1
    nxt = plan >> 2

    def weight_copies(e, s):
        return (pltpu.make_async_copy(wg_hbm.at[layer, e], wg_f.at[s], sems.at[s, 0]),
                pltpu.make_async_copy(wu_hbm.at[layer, e], wu_f.at[s], sems.at[s, 1]),
                pltpu.make_async_copy(wd_hbm.at[layer, e], wd_f.at[s], sems.at[s, 2]))

    @pl.when(i == 0)
    def _():
        for c in weight_copies(te_ref[0], slot):
            c.start()

    @pl.when(first)
    def _():
        for c in weight_copies(te_ref[i], slot):
            c.wait()

        @pl.when(nxt < N_EXPERTS)
        def _():
            for c in weight_copies(nxt, 1 - slot):
                c.start()

        wg_s[...] = wg_f[slot].astype(BF16)
        wu_s[...] = wu_f[slot].astype(BF16)
        wd_s[...] = wd_f[slot].astype(BF16)

    @pl.when(i < nt_ref[0])
    def _():
        xs = _load_packed_tiles(xs_ref, EXPERT_TILE, BF16)
        g = jnp.dot(xs, wg_s[...], preferred_element_type=F32)
        u = jnp.dot(xs, wu_s[...], preferred_element_type=F32)
        hmid = (g * jax.nn.sigmoid(g)) * u
        _store_packed_tiles(out_ref, jnp.dot(hmid.astype(BF16), wd_s[...], preferred_element_type=F32))

    @pl.when(i >= nt_ref[0])
    def _():
        out_ref[...] = jnp.zeros_like(out_ref)


def _expert_call(tile_expert, n_used, tile_plan, xs, wg, wu, wd, layer):
    n_tiles = xs.shape[0] // (EXPERT_TILE * PACK_ROWS)
    tile_shape = (EXPERT_TILE * PACK_ROWS, LANES)
    grid_spec = pltpu.PrefetchScalarGridSpec(
        num_scalar_prefetch=3,
        grid=(n_tiles,),
        in_specs=[pl.BlockSpec(tile_shape, lambda i, te, nt, plan: (jnp.minimum(i, nt[0] - 1), 0))]
        + [pl.BlockSpec(memory_space=pl.ANY)] * 3,
        out_specs=pl.BlockSpec(tile_shape, lambda i, te, nt, plan: (i, 0)),
        scratch_shapes=[
            pltpu.VMEM((D_MODEL, D_EXPERT), BF16),
            pltpu.VMEM((D_MODEL, D_EXPERT), BF16),
            pltpu.VMEM((D_EXPERT, D_MODEL), BF16),
            pltpu.VMEM((2, D_MODEL, D_EXPERT), F32),
            pltpu.VMEM((2, D_MODEL, D_EXPERT), F32),
            pltpu.VMEM((2, D_EXPERT, D_MODEL), F32),
            pltpu.SemaphoreType.DMA((2, 3)),
        ],
    )
    return pl.pallas_call(
        functools.partial(_expert_kernel, layer),
        grid_spec=grid_spec,
        out_shape=jax.ShapeDtypeStruct(xs.shape, jnp.uint32),
        compiler_params=pltpu.CompilerParams(dimension_semantics=("arbitrary",), vmem_limit_bytes=VMEM_LIMIT),
        name="experts",
    )(tile_expert, n_used, tile_plan, xs, wg, wu, wd)


SC_CHUNK = 128
SC_PAD_CHUNK = 64
SC_GATHER_CHUNK = 64


def _sc_workers():
    info = plsc.get_sparse_core_info()
    return info.num_cores, info.num_cores * info.num_subcores


def _sc_split(per_w):
    n_full = per_w // SC_CHUNK
    tail = per_w - n_full * SC_CHUNK
    assert tail % 8 == 0
    return n_full, tail


def _sc_dispatch_call(pos_flat, pad_rows, x_tiles, tile_rows, n_sorted_rows):
    t_all = x_tiles.shape[0] // tile_rows
    dt = x_tiles.dtype
    nc, nw = _sc_workers()
    per_w = t_all // nw
    n_ch, tail = _sc_split(per_w)
    n_pad = pad_rows.shape[0]
    pad_per_w = n_pad // nw
    n_pch = pad_per_w // SC_PAD_CHUNK
    assert per_w * nw == t_all and n_pch * SC_PAD_CHUNK * nw == n_pad
    zeros = jnp.zeros((SC_PAD_CHUNK, tile_rows, LANES), dt)
    tail_scratch = [] if tail == 0 else [
        pltpu.VMEM((tail,), jnp.int32), pltpu.VMEM((tail,), jnp.int32), pltpu.VMEM((tail, tile_rows, LANES), dt)]

    @functools.partial(
        pl.kernel, mesh=plsc.VectorSubcoreMesh(core_axis_name="c", subcore_axis_name="s"),
        out_type=jax.ShapeDtypeStruct((n_sorted_rows, tile_rows, LANES), dt),
        scratch_types=[
            pltpu.VMEM((SC_CHUNK,), jnp.int32), pltpu.VMEM((SC_CHUNK,), jnp.int32),
            pltpu.VMEM((SC_CHUNK, tile_rows, LANES), dt),
            pltpu.VMEM((SC_PAD_CHUNK,), jnp.int32),
            pltpu.VMEM((SC_PAD_CHUNK, tile_rows, LANES), dt),
            pltpu.SemaphoreType.DMA, pltpu.SemaphoreType.DMA, pltpu.SemaphoreType.DMA,
        ] + tail_scratch,
    )
    def k(pos_hbm, pad_hbm, x_hbm, z_hbm, xs_hbm, i0_v, i1_v, rows_v, ip_v, z_v, sem0, sem1, sem2, *tail_refs):
        wid = lax.axis_index("s") * nc + lax.axis_index("c")

        def move(b, n, idx0, idx1, rows):
            loads = (pltpu.async_copy(pos_hbm.at[pl.ds(b, n)], idx0, sem0),
                     pltpu.async_copy(pos_hbm.at[pl.ds(t_all + b, n)], idx1, sem1),
                     pltpu.async_copy(x_hbm.at[pl.ds(b, n)], rows, sem2))
            for c in loads:
                c.wait()
            stores = (pltpu.async_copy(rows, xs_hbm.at[idx0], sem0), pltpu.async_copy(rows, xs_hbm.at[idx1], sem1))
            for c in stores:
                c.wait()

        @pl.loop(0, n_ch)
        def _(j):
            move(pl.multiple_of(wid * per_w + j * SC_CHUNK, 8), SC_CHUNK, i0_v, i1_v, rows_v)

        if tail:
            move(pl.multiple_of(wid * per_w + n_ch * SC_CHUNK, 8), tail, *tail_refs)

        pltpu.sync_copy(z_hbm, z_v)

        @pl.loop(0, n_pch)
        def _(j):
            b = pl.multiple_of(wid * pad_per_w + j * SC_PAD_CHUNK, 8)
            pltpu.sync_copy(pad_hbm.at[pl.ds(b, SC_PAD_CHUNK)], ip_v)
            pltpu.sync_copy(z_v, xs_hbm.at[ip_v])

    xs = k(pos_flat, pad_rows, x_tiles.reshape(t_all, tile_rows, LANES), zeros)
    return xs.reshape(n_sorted_rows * tile_rows, LANES)


def _sc_gather_call(pos_flat, outs, tile_rows):
    n_assign = pos_flat.shape[0]
    dt = outs.dtype
    nc, nw = _sc_workers()
    per_w = n_assign // nw
    chunk = SC_GATHER_CHUNK
    n_ch = per_w // chunk
    tail = per_w - n_ch * chunk
    assert per_w * nw == n_assign and n_ch % 2 == 0 and tail % 8 == 0
    tail_scratch = [] if tail == 0 else [pltpu.VMEM((tail,), jnp.int32), pltpu.VMEM((tail, tile_rows, LANES), dt)]

    @functools.partial(
        pl.kernel, mesh=plsc.VectorSubcoreMesh(core_axis_name="c", subcore_axis_name="s"),
        out_type=jax.ShapeDtypeStruct((n_assign, tile_rows, LANES), dt),
        scratch_types=[pltpu.VMEM((chunk,), jnp.int32), pltpu.VMEM((chunk,), jnp.int32),
                       pltpu.VMEM((chunk, tile_rows, LANES), dt), pltpu.VMEM((chunk, tile_rows, LANES), dt),
                       pltpu.SemaphoreType.DMA, pltpu.SemaphoreType.DMA] + tail_scratch,
    )
    def k(pos_hbm, o_hbm, y_hbm, idx_a, idx_b, rows_a, rows_b, sem_a, sem_b, *tail_refs):
        wid = lax.axis_index("s") * nc + lax.axis_index("c")

        def row0(j):
            return pl.multiple_of(wid * per_w + j * chunk, 8)

        def start(j, idx, rows, sem):
            pltpu.sync_copy(pos_hbm.at[pl.ds(row0(j), chunk)], idx)
            pltpu.async_copy(o_hbm.at[idx], rows, sem)

        def finish(j, idx, rows, sem):
            pltpu.make_async_copy(o_hbm.at[idx], rows, sem).wait()
            pltpu.sync_copy(rows, y_hbm.at[pl.ds(row0(j), chunk)])

        start(0, idx_a, rows_a, sem_a)

        @pl.loop(0, n_ch // 2)
        def _(p):
            j = 2 * p
            start(j + 1, idx_b, rows_b, sem_b)
            finish(j, idx_a, rows_a, sem_a)

            @pl.when(j + 2 < n_ch)
            def _():
                start(j + 2, idx_a, rows_a, sem_a)

            finish(j + 1, idx_b, rows_b, sem_b)

        if tail:
            idx_t, rows_t = tail_refs
            b = pl.multiple_of(wid * per_w + n_ch * chunk, 8)
            pltpu.sync_copy(pos_hbm.at[pl.ds(b, tail)], idx_t)
            pltpu.sync_copy(o_hbm.at[idx_t], rows_t)
            pltpu.sync_copy(rows_t, y_hbm.at[pl.ds(b, tail)])

    y = k(pos_flat, outs.reshape(-1, tile_rows, LANES))
    return y.reshape(n_assign * tile_rows, LANES)


def _combine_dense_kernel(alpha, n_main_tiles, x1t_ref, y0_ref, y1_ref, w_ref, g_ref, b_ref, out_ref, *tail):
    j = pl.program_id(0)
    w = w_ref[...].T
    y = (w[:, 0:1] * _load_packed_tiles(y0_ref, TOK_TILE, F32)
         + w[:, 1:2] * _load_packed_tiles(y1_ref, TOK_TILE, F32))
    res = _layer_norm(alpha * x1t_ref[...] + y, g_ref[...], b_ref[...])
    if n_main_tiles is None:
        out_ref[...] = res
    else:
        @pl.when(j < n_main_tiles)
        def _():
            out_ref[...] = res

        @pl.when(j >= n_main_tiles)
        def _():
            tail[0][...] = res


def _combine_dense_call(x1t, y_pair, w_col, g, b, alpha, n_main_tiles=None):
    t_all = x1t.shape[0]
    n_tiles = t_all // TOK_TILE
    if n_main_tiles is None:
        out_specs = pl.BlockSpec((TOK_TILE, D_MODEL), lambda i: (i, 0))
        out_shape = jax.ShapeDtypeStruct((t_all, D_MODEL), F32)
    else:
        assert n_tiles == n_main_tiles + 1
        out_specs = [pl.BlockSpec((TOK_TILE, D_MODEL), lambda i: (jnp.minimum(i, n_main_tiles - 1), 0)),
                     pl.BlockSpec((TOK_TILE, D_MODEL), lambda i: (0, 0))]
        out_shape = [jax.ShapeDtypeStruct((n_main_tiles * TOK_TILE, D_MODEL), F32),
                     jax.ShapeDtypeStruct((TOK_TILE, D_MODEL), F32)]
    tile = (TOK_TILE, D_MODEL)
    packed = (TOK_TILE * PACK_ROWS, LANES)
    return pl.pallas_call(
        functools.partial(_combine_dense_kernel, alpha, n_main_tiles),
        grid=(n_tiles,),
        in_specs=[
            pl.BlockSpec(tile, lambda i: (i, 0)),
            pl.BlockSpec(packed, lambda i: (i, 0)),
            pl.BlockSpec(packed, lambda i: (i + n_tiles, 0)),
            pl.BlockSpec((8, TOK_TILE), lambda i: (0, i)),
            pl.BlockSpec((1, D_MODEL), lambda i: (0, 0)),
            pl.BlockSpec((1, D_MODEL), lambda i: (0, 0)),
        ],
        out_specs=out_specs,
        out_shape=out_shape,
        compiler_params=pltpu.CompilerParams(dimension_semantics=("arbitrary",), vmem_limit_bytes=VMEM_LIMIT),
        name="combine",
    )(x1t, y_pair, y_pair, w_col, g, b)


def _rope_tables(seq, dec_seq, past_len):
    half = HEAD_DIM // 2
    inv = ROPE_THETA ** (-jnp.arange(half, dtype=F32) / half)
    pos = jnp.concatenate([jnp.arange(seq), past_len + (jnp.arange(TOK_TILE) % dec_seq)])
    ang = pos.astype(F32)[:, None] * inv[None, :]
    cos = jnp.cos(ang)
    sin = jnp.sin(ang)
    zero = jnp.zeros_like(sin)
    reps = LANES // HEAD_DIM
    cos_t = jnp.tile(jnp.concatenate([cos, cos], -1), (1, reps))
    sa_t = jnp.tile(jnp.concatenate([-sin, zero], -1), (1, reps))
    sb_t = jnp.tile(jnp.concatenate([zero, sin], -1), (1, reps))
    return cos_t, sa_t, sb_t


def _inclusive_cumsum(v):
    n = v.shape[0]
    tri = jnp.arange(n)[None, :] <= jnp.arange(n)[:, None]
    return jnp.sum(jnp.where(tri, v[None, :], 0), axis=1)


def _dispatch_plan(ridx, counts, n_tiles):
    t_all = ridx.shape[1]
    tiles_e = (counts + EXPERT_TILE - 1) // EXPERT_TILE
    tile_end = _inclusive_cumsum(tiles_e)
    offs = (tile_end - tiles_e) * EXPERT_TILE
    n_used = tile_end[-1:]
    tile_expert = jnp.minimum(
        jnp.sum((jnp.arange(n_tiles)[:, None] >= tile_end[None, :]).astype(jnp.int32), axis=1), N_EXPERTS - 1)
    experts = jnp.arange(N_EXPERTS)
    used = tiles_e > 0
    slot_e = (_inclusive_cumsum(used.astype(jnp.int32)) - 1) % 2
    next_e = jnp.min(jnp.where((experts[None, :] > experts[:, None]) & used[None, :], experts[None, :], N_EXPERTS),
                     axis=1)
    tiles = jnp.arange(n_tiles)
    of_tile = tile_expert[:, None] == experts[None, :]
    pick = lambda v: jnp.sum(jnp.where(of_tile, v[None, :], 0), axis=1)
    first = (tiles == pick(tile_end - tiles_e)) & (tiles < n_used)
    tile_plan = first.astype(jnp.int32) + 2 * pick(slot_e) + 4 * pick(next_e)
    pos = ridx[TOP_K:2 * TOP_K] + jnp.sum(
        jnp.where(ridx[:TOP_K, :, None] == experts[None, None, :], offs[None, None, :], 0), axis=-1)
    gap_start = jnp.concatenate([offs + counts, n_used * EXPERT_TILE])
    gap_len = jnp.concatenate([tiles_e * EXPERT_TILE - counts, (n_tiles - n_used) * EXPERT_TILE])
    gap_end_q = _inclusive_cumsum(gap_len)
    q = jnp.arange(n_tiles * EXPERT_TILE - TOP_K * t_all)
    gap = jnp.sum((q[:, None] >= gap_end_q[None, :]).astype(jnp.int32), axis=1)
    in_gap = gap[:, None] == jnp.arange(N_EXPERTS + 1)[None, :]
    pad_rows = q + jnp.sum(jnp.where(in_gap, (gap_start - gap_end_q + gap_len)[None, :], 0), axis=1)
    return (pos.reshape(-1).astype(jnp.int32), pad_rows.astype(jnp.int32),
            tile_expert.astype(jnp.int32), n_used.astype(jnp.int32), tile_plan.astype(jnp.int32))


def kernel(x_prompt, x_sample, cache_k, cache_v, w_in, sinks, gm_ln_g, gm_ln_b, gm_ws, gm_bs,
           out_norm_a, out_norm_b, w_o, ln1_g, ln1_b, w_router, router_bias,
           w_gate, w_up, w_down, ln2_g, ln2_b):
    batch, seq, _ = x_prompt.shape
    dec_batch, dec_seq, _ = x_sample.shape
    depth = w_in.shape[0]
    past_len = PAST_LEN
    assert dec_batch * dec_seq == TOK_TILE and dec_seq == CHUNK and seq % TOK_TILE == 0
    assert cache_k.shape[2] == WINDOW
    alpha = (2 * depth) ** 0.25
    tiles_per_seq = seq // TOK_TILE

    tables = _rope_tables(seq, dec_seq, past_len)
    wrt = w_router.T
    rb = router_bias.reshape(N_EXPERTS, 1)
    cache_k2 = cache_k.reshape(depth * dec_batch, WINDOW, KV_WIDTH)
    cache_v2 = cache_v.reshape(depth * dec_batch, WINDOW, KV_WIDTH)
    as_rows = lambda p: p.reshape(depth, 1, -1)
    lng3, lnb3, na3, nb3 = as_rows(gm_ln_g), as_rows(gm_ln_b), as_rows(out_norm_a), as_rows(out_norm_b)
    l1g3, l1b3, l2g3, l2b3 = as_rows(ln1_g), as_rows(ln1_b), as_rows(ln2_g), as_rows(ln2_b)
    bst3 = jnp.swapaxes(gm_bs, 1, 2)

    streams = [
        dict(x=(x_prompt.reshape(batch * seq, D_MODEL), x_sample.reshape(-1, D_MODEL)),
             n_seq=batch, has_sample=True),
    ]
    kp, vp, ks, vs, gms = [], [], [], [], []
    for l in range(depth):
        layer_args = (sinks[l], w_in, cache_k2, cache_v2, lng3, lnb3, gm_ws, bst3, na3, nb3, w_o, l1g3, l1b3, wrt, rb)
        k_tails, v_tails = [], []
        for st in streams:
            n_prompt = st["n_seq"] * seq
            n_prompt_tiles = n_prompt // TOK_TILE
            t_all = n_prompt + (TOK_TILE if st["has_sample"] else 0)
            n_exp_tiles = (TOP_K * t_all) // EXPERT_TILE + N_EXPERTS
            res = _mixer_call(layer_args, st["x"], tables, n_prompt_tiles, tiles_per_seq, alpha, l)
            if st["has_sample"]:
                x1t, x1p, k_tail, v_tail, k_s, v_s, vn_s, ridx, rw, counts = res
            else:
                x1t, x1p, k_tail, v_tail, ridx, rw, counts = res

            pos, pad_rows, tile_expert, n_used, tile_plan = _dispatch_plan(ridx, counts[:, 0], n_exp_tiles)
            xs = _sc_dispatch_call(pos, pad_rows, x1p, PACK_ROWS, n_exp_tiles * EXPERT_TILE)
            outs = _expert_call(tile_expert, n_used, tile_plan, xs, w_gate, w_up, w_down, l)
            y_pair = _sc_gather_call(pos, outs, PACK_ROWS)
            if l < depth - 1:
                st["x"] = (x1t, y_pair, rw, l2g3, l2b3)
            else:
                st["x"] = _combine_dense_call(x1t, y_pair, rw, l2g3[l], l2b3[l], alpha,
                                              n_main_tiles=n_prompt_tiles if st["has_sample"] else None)

            k_tails.append(k_tail)
            v_tails.append(v_tail)
            if st["has_sample"]:
                ks.append(k_s)
                vs.append(v_s)
                gms.append(vn_s)
        kp.append(jnp.concatenate(k_tails, axis=0))
        vp.append(jnp.concatenate(v_tails, axis=0))

    y_prompt = jnp.concatenate([st["x"][0] if st["has_sample"] else st["x"] for st in streams],
                               axis=0).reshape(batch, seq, D_MODEL)
    y_sample = streams[-1]["x"][1].reshape(dec_batch, dec_seq, D_MODEL)
    heads = (N_KV_HEADS, HEAD_DIM)
    return (y_prompt, y_sample,
            jnp.stack(kp).reshape(depth, batch, WINDOW, *heads), jnp.stack(vp).reshape(depth, batch, WINDOW, *heads),
            jnp.stack(ks).reshape(depth, dec_batch, dec_seq, *heads),
            jnp.stack(vs).reshape(depth, dec_batch, dec_seq, *heads),
            jnp.stack(gms).reshape(depth, dec_batch, dec_seq, GM_WIDTH))
```

```python
import functools

import jax
import jax.numpy as jnp
from jax import lax
from jax.experimental import pallas as pl
from jax.experimental.pallas import tpu as pltpu
from jax.experimental.pallas import tpu_sc as plsc

D_MODEL = 1024
CHUNK = 64
WINDOW = 128
HEAD_DIM = 64
N_Q_HEADS = 8
N_KV_HEADS = 2
Q_PER_KV = N_Q_HEADS // N_KV_HEADS
PAST_LEN = 2048
ATT_WIDTH = N_Q_HEADS * HEAD_DIM
KV_WIDTH = N_KV_HEADS * HEAD_DIM
ROPE_THETA = 10000.0
GM_GROUPS = 4
GM_CH = 128
GM_WIDTH = GM_GROUPS * GM_CH
GM_CHUNK = 128
D_MIX = ATT_WIDTH + GM_WIDTH
D_IN = ATT_WIDTH + 2 * KV_WIDTH + 2 * GM_WIDTH
N_EXPERTS = 16
N_EXPERT_GROUPS = 4
EXPERTS_PER_GROUP = N_EXPERTS // N_EXPERT_GROUPS
TOP_K = 2
D_EXPERT = 512
LN_EPS = 1e-5
NEG_INF = -1e30

LANES = 128
PACK_ROWS = D_MODEL // (2 * LANES)
HI_HALF = 0xFFFF0000
TOK_TILE = 512
EXPERT_TILE = 512
KEYS_PER_CHUNK = WINDOW + CHUNK
VMEM_LIMIT = 60 * 1024 * 1024

F32 = jnp.float32
BF16 = jnp.bfloat16


def _layer_norm(x, g, b):
    mu = jnp.mean(x, axis=-1, keepdims=True)
    d = x - mu
    var = jnp.mean(d * d, axis=-1, keepdims=True)
    return d * lax.rsqrt(var + LN_EPS) * g + b


def _rms_norm(x, g):
    return x * lax.rsqrt(jnp.mean(x * x, axis=-1, keepdims=True) + LN_EPS) * g


def _gelu(x):
    return 0.5 * x * (1.0 + lax.erf(x * (0.5 ** 0.5)))


def _bf16_bits(x):
    return lax.bitcast_convert_type(x.astype(BF16).astype(F32), jnp.uint32)


def _store_packed_tiles(ref, x):
    n = x.shape[0]
    for s in range(PACK_ROWS):
        lo = _bf16_bits(x[:, (2 * s) * LANES:(2 * s + 1) * LANES]) >> 16
        hi = _bf16_bits(x[:, (2 * s + 1) * LANES:(2 * s + 2) * LANES]) & jnp.uint32(HI_HALF)
        ref[pl.ds(s, n, stride=PACK_ROWS), :] = hi | lo


def _load_packed_tiles(ref, n, dtype):
    cols = []
    for s in range(PACK_ROWS):
        w = ref[pl.ds(s, n, stride=PACK_ROWS), :]
        cols.append(lax.bitcast_convert_type(w << 16, F32).astype(dtype))
        cols.append(lax.bitcast_convert_type(w & jnp.uint32(HI_HALF), F32).astype(dtype))
    return jnp.concatenate(cols, axis=-1)


def _first_index_of_max(rows):
    m = rows[0]
    for r in rows[1:]:
        m = jnp.maximum(m, r)
    idx = jnp.full(m.shape, len(rows), jnp.int32)
    for e in reversed(range(len(rows))):
        idx = jnp.where(rows[e] == m, e, idx)
    return m, idx


def _mixer_kernel(n_prompt_tiles, tiles_per_seq, alpha, has_sample, first_layer, sinks_ref, *refs):
    n_x = 2 if first_layer else 6
    x_refs, refs = refs[:n_x], refs[n_x:]
    (w_in_ref, cos_ref, sa_ref, sb_ref, ck_ref, cv_ref, lng_ref, lnb_ref, ws_ref, bst_ref, na_ref, nb_ref, w_o_ref,
     l1g_ref, l1b_ref, wrt_ref, rb_ref, tri_ref, x1t_ref, x1p_ref, kt_ref, vt_ref) = refs[:22]
    rest = refs[22:]
    if has_sample:
        (ks_ref, vs_ref, vn_ref, ridx_ref, rw_ref, counts_ref,
         kd_ref, vd_ref, qm_ref, att_ref, gm_ref, cnt_ref, w_in_s, w_o_s, x_s) = rest
    else:
        (ridx_ref, rw_ref, counts_ref,
         kd_ref, vd_ref, qm_ref, att_ref, gm_ref, cnt_ref, w_in_s, w_o_s, x_s) = rest
    i = pl.program_id(0)
    is_sample = i >= n_prompt_tiles
    seq_start = jnp.logical_and(jnp.logical_not(is_sample), (i % tiles_per_seq) == 0)
    seq_end = jnp.logical_and(jnp.logical_not(is_sample), (i % tiles_per_seq) == tiles_per_seq - 1)
    tm = TOK_TILE

    @pl.when(i == 0)
    def _():
        w_in_s[...] = w_in_ref[0].astype(BF16)
        w_o_s[...] = w_o_ref[0].astype(BF16)

    if first_layer:
        xp_ref, xs_ref = x_refs
        x_s[...] = jnp.where(is_sample, xs_ref[...], xp_ref[...])
    else:
        x1_prev_ref, y0_ref, y1_ref, wc_ref, l2g_ref, l2b_ref = x_refs
        wc = wc_ref[...].T
        y_moe = (wc[:, 0:1] * _load_packed_tiles(y0_ref, tm, F32)
                 + wc[:, 1:2] * _load_packed_tiles(y1_ref, tm, F32))
        x_s[...] = _layer_norm(alpha * x1_prev_ref[...] + y_moe, l2g_ref[...], l2b_ref[...])

    h = jnp.dot(x_s[...].astype(BF16), w_in_s[...], preferred_element_type=F32)

    cos = cos_ref[...]
    sa = sa_ref[...]
    sb = sb_ref[...]

    def rope(blk):
        return blk * cos + pltpu.roll(blk, LANES - HEAD_DIM // 2, 1) * sa + pltpu.roll(blk, HEAD_DIM // 2, 1) * sb

    lane = lax.broadcasted_iota(jnp.int32, (1, LANES), 1)
    lo_half = lane < HEAD_DIM

    o_k = ATT_WIDTH
    o_v = o_k + KV_WIDTH
    o_u = o_v + KV_WIDTH
    o_g = o_u + GM_WIDTH

    k_rot = rope(h[:, o_k:o_v])
    v_new = h[:, o_v:o_u]

    @pl.when(seq_end)
    def _():
        kt_ref[...] = k_rot[tm - WINDOW:]
        vt_ref[...] = v_new[tm - WINDOW:]

    if has_sample:
        @pl.when(is_sample)
        def _():
            ks_ref[...] = k_rot
            vs_ref[...] = v_new

    def dup_heads(a):
        sw = pltpu.roll(a, HEAD_DIM, 1)
        return jnp.where(lo_half, a, sw).astype(BF16), jnp.where(lo_half, sw, a).astype(BF16)

    k_d = dup_heads(k_rot)
    v_d = dup_heads(v_new)

    @pl.when(jnp.logical_not(is_sample))
    def _():
        @pl.when(seq_start)
        def _():
            for g in range(N_KV_HEADS):
                kd_ref[g, 0:WINDOW, :] = jnp.zeros((WINDOW, LANES), BF16)
                vd_ref[g, 0:WINDOW, :] = jnp.zeros((WINDOW, LANES), BF16)

        @pl.when(jnp.logical_not(seq_start))
        def _():
            for g in range(N_KV_HEADS):
                kd_ref[g, 0:WINDOW, :] = kd_ref[g, tm:tm + WINDOW, :]
                vd_ref[g, 0:WINDOW, :] = vd_ref[g, tm:tm + WINDOW, :]

        for g in range(N_KV_HEADS):
            kd_ref[g, WINDOW:WINDOW + tm, :] = k_d[g]
            vd_ref[g, WINDOW:WINDOW + tm, :] = v_d[g]

    @pl.when(is_sample)
    def _():
        for b in range(tm // CHUNK):
            ck = dup_heads(ck_ref[b])
            cv = dup_heads(cv_ref[b])
            base = b * KEYS_PER_CHUNK
            for g in range(N_KV_HEADS):
                kd_ref[g, base:base + WINDOW, :] = ck[g]
                vd_ref[g, base:base + WINDOW, :] = cv[g]
                kd_ref[g, base + WINDOW:base + KEYS_PER_CHUNK, :] = k_d[g][b * CHUNK:(b + 1) * CHUNK]
                vd_ref[g, base + WINDOW:base + KEYS_PER_CHUNK, :] = v_d[g][b * CHUNK:(b + 1) * CHUNK]

    scale = HEAD_DIM ** -0.5
    for b in range(ATT_WIDTH // LANES):
        qb = (rope(h[:, b * LANES:(b + 1) * LANES]) * scale).astype(BF16)
        zero = jnp.zeros_like(qb)
        qm_ref[2 * b] = jnp.where(lo_half, qb, zero)
        qm_ref[2 * b + 1] = jnp.where(lo_half, zero, qb)

    key_stride = jnp.where(is_sample, KEYS_PER_CHUNK, CHUNK)
    col = lax.broadcasted_iota(jnp.int32, (1, KEYS_PER_CHUNK), 1)
    row_blk = lax.broadcasted_iota(jnp.int32, (Q_PER_KV * CHUNK, 1), 0) // CHUNK
    sink_cols = [
        jnp.where(row_blk == 0, sinks_ref[Q_PER_KV * g],
                  jnp.where(row_blk == 1, sinks_ref[Q_PER_KV * g + 1],
                            jnp.where(row_blk == 2, sinks_ref[Q_PER_KV * g + 2], sinks_ref[Q_PER_KV * g + 3])))
        for g in range(N_KV_HEADS)]

    for c in range(tm // CHUNK):
        r0 = c * CHUNK
        k0 = pl.multiple_of(c * key_stride, CHUNK)
        outs = []
        for g in range(N_KV_HEADS):
            q_st = jnp.concatenate([qm_ref[Q_PER_KV * g + r, pl.ds(r0, CHUNK), :] for r in range(Q_PER_KV)],
                                   axis=0)
            keys = kd_ref[g, pl.ds(k0, KEYS_PER_CHUNK), :]
            vals = vd_ref[g, pl.ds(k0, KEYS_PER_CHUNK), :]
            s = lax.dot_general(q_st, keys, (((1,), (1,)), ((), ())), preferred_element_type=F32)
            if c < WINDOW // CHUNK:
                first_valid = jnp.where(seq_start, (WINDOW // CHUNK - c) * CHUNK, 0)
                s = jnp.where(col >= first_valid, s, NEG_INF)
            sink = sink_cols[g]
            m = jnp.maximum(jnp.max(s, axis=-1, keepdims=True), sink)
            p = jnp.exp(s - m)
            denom = jnp.sum(p, axis=-1, keepdims=True) + jnp.exp(sink - m)
            o = jnp.dot(p.astype(BF16), vals, preferred_element_type=F32) * (1.0 / denom)
            for bb in range(2):
                outs.append(jnp.where(lo_half, o[(2 * bb) * CHUNK:(2 * bb + 1) * CHUNK],
                                      o[(2 * bb + 1) * CHUNK:(2 * bb + 2) * CHUNK]))
        for b in range(ATT_WIDTH // LANES):
            att_ref[pl.ds(r0, CHUNK), b * LANES:(b + 1) * LANES] = outs[b]

    prow = lax.broadcasted_iota(jnp.int32, (GM_CHUNK, GM_CHUNK), 0)
    pcol = lax.broadcasted_iota(jnp.int32, (GM_CHUNK, GM_CHUNK), 1)
    half = GM_CHUNK // 2
    tril = pcol <= prow
    same_blk = (prow < half) == (pcol < half)
    prow1 = lax.broadcasted_iota(jnp.int32, (GM_CHUNK, 1), 0)
    bst = bst_ref[...]
    bst_s = jnp.where(prow1 < half, bst, pltpu.roll(bst, half, 0))
    bias = jnp.where(is_sample, bst_s, bst)
    for g in range(GM_GROUPS):
        wg_ = ws_ref[g]
        w_s = jnp.where(prow < half, wg_, pltpu.roll(pltpu.roll(wg_, half, 0), half, 1))
        w_eff = jnp.where(is_sample, jnp.where(same_blk, w_s, 0.0), wg_)
        w_eff = jnp.where(tril, w_eff, 0.0).astype(BF16)
        cols = slice(g * GM_CH, (g + 1) * GM_CH)
        ug = _gelu(h[:, o_u + g * GM_CH:o_u + (g + 1) * GM_CH])
        vn = _layer_norm(_gelu(h[:, o_g + g * GM_CH:o_g + (g + 1) * GM_CH]), lng_ref[:, cols], lnb_ref[:, cols])

        if has_sample:
            @pl.when(is_sample)
            def _(cols=cols, vn=vn):
                vn_ref[:, cols] = vn

        vn_b = vn.astype(BF16)
        for n in range(tm // GM_CHUNK):
            rows = slice(n * GM_CHUNK, (n + 1) * GM_CHUNK)
            s = jnp.dot(w_eff, vn_b[rows], preferred_element_type=F32) + bias[:, g:g + 1]
            gm_ref[rows, cols] = ug[rows] * s

    mixed = jnp.concatenate([_rms_norm(att_ref[...], na_ref[...]), _rms_norm(gm_ref[...], nb_ref[...])], axis=-1)
    y = alpha * x_s[...] + jnp.dot(mixed.astype(BF16), w_o_s[...], preferred_element_type=F32)
    x1 = _layer_norm(y, l1g_ref[...], l1b_ref[...])
    x1t_ref[...] = x1
    _store_packed_tiles(x1p_ref, x1)

    nt = (((1,), (1,)), ((), ()))
    wr = wrt_ref[...]
    wr_hi = wr.astype(BF16)
    wr_lo = (wr - wr_hi.astype(F32)).astype(BF16)
    x1_hi = x1.astype(BF16)
    x1_lo = (x1 - x1_hi.astype(F32)).astype(BF16)
    logits = (lax.dot_general(wr_hi, x1_hi, nt, preferred_element_type=F32)
              + (lax.dot_general(wr_hi, x1_lo, nt, preferred_element_type=F32)
                 + lax.dot_general(wr_lo, x1_hi, nt, preferred_element_type=F32)))
    scores = jax.nn.sigmoid(logits)
    sel = scores + rb_ref[...]
    sel_rows = [sel[e:e + 1, :] for e in range(N_EXPERTS)]
    sc_rows = [scores[e:e + 1, :] for e in range(N_EXPERTS)]
    grp = []
    for g in range(N_EXPERT_GROUPS):
        r = sel_rows[g * EXPERTS_PER_GROUP:(g + 1) * EXPERTS_PER_GROUP]
        best_pair = None
        for a in range(EXPERTS_PER_GROUP):
            for b in range(a + 1, EXPERTS_PER_GROUP):
                pair = r[a] + r[b]
                best_pair = pair if best_pair is None else jnp.maximum(best_pair, pair)
        grp.append(best_pair)
    _, best = _first_index_of_max(grp)
    masked = [jnp.where(best == (e // EXPERTS_PER_GROUP), sel_rows[e], NEG_INF) for e in range(N_EXPERTS)]
    _, e0 = _first_index_of_max(masked)
    masked2 = [jnp.where(e0 == e, -jnp.inf, masked[e]) for e in range(N_EXPERTS)]
    _, e1 = _first_index_of_max(masked2)
    w0 = jnp.zeros_like(sc_rows[0])
    w1 = jnp.zeros_like(sc_rows[0])
    for e in range(N_EXPERTS):
        w0 = jnp.where(e0 == e, sc_rows[e], w0)
        w1 = jnp.where(e1 == e, sc_rows[e], w1)
    wsum = w0 + w1
    rw_ref[...] = jnp.concatenate([w0 / wsum, w1 / wsum, jnp.zeros((6, tm), F32)], axis=0)

    @pl.when(i == 0)
    def _():
        cnt_ref[...] = jnp.zeros_like(cnt_ref)

    eid = lax.broadcasted_iota(jnp.int32, (N_EXPERTS, tm), 0)
    oh0 = (eid == e0).astype(F32)
    oh1 = (eid == e1).astype(F32)
    oh = oh0 + oh1
    incl = jnp.dot(oh.astype(BF16), tri_ref[...], preferred_element_type=F32)
    before = cnt_ref[:, 0:1] + incl - oh
    r0 = jnp.sum(oh0 * before, axis=0, keepdims=True).astype(jnp.int32)
    r1 = jnp.sum(oh1 * before, axis=0, keepdims=True).astype(jnp.int32)
    cnt = cnt_ref[...] + incl[:, tm - 1:tm]
    cnt_ref[...] = cnt
    counts_ref[...] = cnt.astype(jnp.int32)
    ridx_ref[...] = jnp.concatenate([e0, e1, r0, r1, jnp.zeros((4, tm), jnp.int32)], axis=0)


def _mixer_call(layer_args, x_all, tables, n_prompt_tiles, tiles_per_seq, alpha, layer):
    (sinks, w_in, ck, cv, lng, lnb, ws, bst, na, nb, w_o, l1g, l1b, wrt, rb) = layer_args
    cos_t, sa_t, sb_t = tables

    def of_layer(shape, which=layer):
        nd = len(shape)
        return pl.BlockSpec((None,) + shape, lambda i, _nd=nd: (which,) + (0,) * _nd)

    first_layer = len(x_all) == 2
    if first_layer:
        x_main, x_tail = x_all
        t_all = x_main.shape[0] + x_tail.shape[0]
        n_tiles = t_all // TOK_TILE
        x_args = [x_main, x_tail]
        x_specs = [pl.BlockSpec((TOK_TILE, D_MODEL), lambda i: (jnp.minimum(i, n_prompt_tiles - 1), 0)),
                   pl.BlockSpec((TOK_TILE, D_MODEL), lambda i: (0, 0))]
    else:
        x1t_prev, y_pair, w_col, l2g, l2b = x_all
        t_all = x1t_prev.shape[0]
        n_tiles = t_all // TOK_TILE
        x_args = [x1t_prev, y_pair, y_pair, w_col, l2g, l2b]
        x_specs = [pl.BlockSpec((TOK_TILE, D_MODEL), lambda i: (i, 0)),
                   pl.BlockSpec((TOK_TILE * PACK_ROWS, LANES), lambda i: (i, 0)),
                   pl.BlockSpec((TOK_TILE * PACK_ROWS, LANES), lambda i: (i + n_tiles, 0)),
                   pl.BlockSpec((8, TOK_TILE), lambda i: (0, i)),
                   of_layer((1, D_MODEL), layer - 1), of_layer((1, D_MODEL), layer - 1)]
    n_seq_tiles = tiles_per_seq
    has_sample = n_tiles > n_prompt_tiles

    def const(shape):
        nd = len(shape)
        return pl.BlockSpec(shape, lambda i, _nd=nd: (0,) * _nd)

    def tab_map(i):
        return (jnp.where(i < n_prompt_tiles, i % n_seq_tiles, n_seq_tiles), 0)

    n_seq = n_prompt_tiles // n_seq_tiles

    def seq_map(i):
        return (jnp.minimum(i // n_seq_tiles, n_seq - 1), 0)

    row_blk = lambda w: pl.BlockSpec((TOK_TILE, w), lambda i: (i, 0))
    in_specs = [
        pl.BlockSpec(memory_space=pltpu.SMEM),
        *x_specs,
        pl.BlockSpec((1, D_MODEL, D_IN), lambda i: (layer, 0, 0)),
        pl.BlockSpec((TOK_TILE, LANES), tab_map),
        pl.BlockSpec((TOK_TILE, LANES), tab_map),
        pl.BlockSpec((TOK_TILE, LANES), tab_map),
        pl.BlockSpec((TOK_TILE // CHUNK, WINDOW, KV_WIDTH), lambda i: (layer, 0, 0)),
        pl.BlockSpec((TOK_TILE // CHUNK, WINDOW, KV_WIDTH), lambda i: (layer, 0, 0)),
        of_layer((1, GM_WIDTH)), of_layer((1, GM_WIDTH)),
        of_layer((GM_GROUPS, GM_CHUNK, GM_CHUNK)), of_layer((GM_CHUNK, GM_GROUPS)),
        of_layer((1, ATT_WIDTH)), of_layer((1, GM_WIDTH)),
        pl.BlockSpec((1, D_MIX, D_MODEL), lambda i: (layer, 0, 0)),
        of_layer((1, D_MODEL)), of_layer((1, D_MODEL)),
        const((N_EXPERTS, D_MODEL)), const((N_EXPERTS, 1)),
        const((TOK_TILE, TOK_TILE)),
    ]
    out_shape = [
        jax.ShapeDtypeStruct((t_all, D_MODEL), F32),
        jax.ShapeDtypeStruct((t_all * PACK_ROWS, LANES), jnp.uint32),
        jax.ShapeDtypeStruct((n_seq * WINDOW, KV_WIDTH), F32),
        jax.ShapeDtypeStruct((n_seq * WINDOW, KV_WIDTH), F32),
        jax.ShapeDtypeStruct((TOK_TILE, KV_WIDTH), F32),
        jax.ShapeDtypeStruct((TOK_TILE, KV_WIDTH), F32),
        jax.ShapeDtypeStruct((TOK_TILE, GM_WIDTH), F32),
        jax.ShapeDtypeStruct((8, t_all), jnp.int32),
        jax.ShapeDtypeStruct((8, t_all), F32),
        jax.ShapeDtypeStruct((N_EXPERTS, LANES), jnp.int32),
    ]
    out_specs = [
        pl.BlockSpec((TOK_TILE, D_MODEL), lambda i: (i, 0)),
        pl.BlockSpec((TOK_TILE * PACK_ROWS, LANES), lambda i: (i, 0)),
        pl.BlockSpec((WINDOW, KV_WIDTH), seq_map), pl.BlockSpec((WINDOW, KV_WIDTH), seq_map),
        const((TOK_TILE, KV_WIDTH)), const((TOK_TILE, KV_WIDTH)),
        const((TOK_TILE, GM_WIDTH)),
        pl.BlockSpec((8, TOK_TILE), lambda i: (0, i)),
        pl.BlockSpec((8, TOK_TILE), lambda i: (0, i)),
        const((N_EXPERTS, LANES)),
    ]
    if not has_sample:
        del out_shape[4:7], out_specs[4:7]
    kd_rows = (TOK_TILE // CHUNK) * KEYS_PER_CHUNK
    scratch = [
        pltpu.VMEM((N_KV_HEADS, kd_rows, LANES), BF16),
        pltpu.VMEM((N_KV_HEADS, kd_rows, LANES), BF16),
        pltpu.VMEM((N_Q_HEADS, TOK_TILE, LANES), BF16),
        pltpu.VMEM((TOK_TILE, ATT_WIDTH), F32),
        pltpu.VMEM((TOK_TILE, GM_WIDTH), F32),
        pltpu.VMEM((N_EXPERTS, LANES), F32),
        pltpu.VMEM((D_MODEL, D_IN), BF16),
        pltpu.VMEM((D_MIX, D_MODEL), BF16),
        pltpu.VMEM((TOK_TILE, D_MODEL), F32),
    ]
    return pl.pallas_call(
        functools.partial(_mixer_kernel, n_prompt_tiles, tiles_per_seq, alpha, has_sample, first_layer),
        grid=(n_tiles,),
        in_specs=in_specs, out_specs=out_specs, out_shape=out_shape,
        scratch_shapes=scratch,
        compiler_params=pltpu.CompilerParams(dimension_semantics=("arbitrary",), vmem_limit_bytes=VMEM_LIMIT),
        name="mixer",
    )(sinks, *x_args, w_in, cos_t, sa_t, sb_t, ck, cv, lng, lnb, ws, bst, na, nb, w_o, l1g, l1b, wrt, rb,
      jnp.triu(jnp.ones((TOK_TILE, TOK_TILE), BF16)))


def _expert_kernel(layer, te_ref, nt_ref, plan_ref, xs_ref, wg_hbm, wu_hbm, wd_hbm, out_ref,
                   wgu_s, wd_s, wg_f, wu_f, wd_f, sems):
    i = pl.program_id(0)
    plan = plan_ref[i]
    first = (plan & 1) == 1
    slot = (plan >> 1) & 1
    nxt = plan >> 2

    def weight_copies(e, s):
        return (pltpu.make_async_copy(wg_hbm.at[layer, e], wg_f.at[s], sems.at[s, 0]),
                pltpu.make_async_copy(wu_hbm.at[layer, e], wu_f.at[s], sems.at[s, 1]),
                pltpu.make_async_copy(wd_hbm.at[layer, e], wd_f.at[s], sems.at[s, 2]))

    @pl.when(i == 0)
    def _():
        for c in weight_copies(te_ref[0], slot):
            c.start()

    @pl.when(first)
    def _():
        for c in weight_copies(te_ref[i], slot):
            c.wait()

        @pl.when(nxt < N_EXPERTS)
        def _():
            for c in weight_copies(nxt, 1 - slot):
                c.start()

        wgu_s[:, 0:D_EXPERT] = wg_f[slot].astype(BF16)
        wgu_s[:, D_EXPERT:2 * D_EXPERT] = wu_f[slot].astype(BF16)
        wd_s[...] = wd_f[slot].astype(BF16)

    @pl.when(i < nt_ref[0])
    def _():
        xs = _load_packed_tiles(xs_ref, EXPERT_TILE, BF16)
        gu = jnp.dot(xs, wgu_s[...], preferred_element_type=F32)
        g = gu[:, 0:D_EXPERT]
        hmid = (g * jax.nn.sigmoid(g)) * gu[:, D_EXPERT:2 * D_EXPERT]
        _store_packed_tiles(out_ref, jnp.dot(hmid.astype(BF16), wd_s[...], preferred_element_type=F32))

    @pl.when(i >= nt_ref[0])
    def _():
        out_ref[...] = jnp.zeros_like(out_ref)


def _expert_call(tile_expert, n_used, tile_plan, xs, wg, wu, wd, layer):
    n_tiles = xs.shape[0] // (EXPERT_TILE * PACK_ROWS)
    tile_shape = (EXPERT_TILE * PACK_ROWS, LANES)
    grid_spec = pltpu.PrefetchScalarGridSpec(
        num_scalar_prefetch=3,
        grid=(n_tiles,),
        in_specs=[pl.BlockSpec(tile_shape, lambda i, te, nt, plan: (jnp.minimum(i, nt[0] - 1), 0))]
        + [pl.BlockSpec(memory_space=pl.ANY)] * 3,
        out_specs=pl.BlockSpec(tile_shape, lambda i, te, nt, plan: (i, 0)),
        scratch_shapes=[
            pltpu.VMEM((D_MODEL, 2 * D_EXPERT), BF16),
            pltpu.VMEM((D_EXPERT, D_MODEL), BF16),
            pltpu.VMEM((2, D_MODEL, D_EXPERT), F32),
            pltpu.VMEM((2, D_MODEL, D_EXPERT), F32),
            pltpu.VMEM((2, D_EXPERT, D_MODEL), F32),
            pltpu.SemaphoreType.DMA((2, 3)),
        ],
    )
    return pl.pallas_call(
        functools.partial(_expert_kernel, layer),
        grid_spec=grid_spec,
        out_shape=jax.ShapeDtypeStruct(xs.shape, jnp.uint32),
        compiler_params=pltpu.CompilerParams(dimension_semantics=("arbitrary",), vmem_limit_bytes=VMEM_LIMIT),
        name="experts",
    )(tile_expert, n_used, tile_plan, xs, wg, wu, wd)


SC_CHUNK = 128
SC_PAD_CHUNK = 64
SC_GATHER_CHUNK = 64


def _sc_workers():
    info = plsc.get_sparse_core_info()
    return info.num_cores, info.num_cores * info.num_subcores


def _sc_split(per_w):
    n_full = per_w // SC_CHUNK
    tail = per_w - n_full * SC_CHUNK
    assert tail % 8 == 0
    return n_full, tail


def _sc_dispatch_call(pos_flat, pad_rows, x_tiles, tile_rows, n_sorted_rows):
    t_all = x_tiles.shape[0] // tile_rows
    dt = x_tiles.dtype
    nc, nw = _sc_workers()
    per_w = t_all // nw
    n_ch, tail = _sc_split(per_w)
    n_pad = pad_rows.shape[0]
    pad_per_w = n_pad // nw
    n_pch = pad_per_w // SC_PAD_CHUNK
    assert per_w * nw == t_all and n_pch * SC_PAD_CHUNK * nw == n_pad
    zeros = jnp.zeros((SC_PAD_CHUNK, tile_rows, LANES), dt)
    tail_scratch = [] if tail == 0 else [
        pltpu.VMEM((tail,), jnp.int32), pltpu.VMEM((tail,), jnp.int32), pltpu.VMEM((tail, tile_rows, LANES), dt)]

    @functools.partial(
        pl.kernel, mesh=plsc.VectorSubcoreMesh(core_axis_name="c", subcore_axis_name="s"),
        out_type=jax.ShapeDtypeStruct((n_sorted_rows, tile_rows, LANES), dt),
        scratch_types=[
            pltpu.VMEM((SC_CHUNK,), jnp.int32), pltpu.VMEM((SC_CHUNK,), jnp.int32),
            pltpu.VMEM((SC_CHUNK, tile_rows, LANES), dt),
            pltpu.VMEM((SC_PAD_CHUNK,), jnp.int32),
            pltpu.VMEM((SC_PAD_CHUNK, tile_rows, LANES), dt),
            pltpu.SemaphoreType.DMA, pltpu.SemaphoreType.DMA, pltpu.SemaphoreType.DMA,
        ] + tail_scratch,
    )
    def k(pos_hbm, pad_hbm, x_hbm, z_hbm, xs_hbm, i0_v, i1_v, rows_v, ip_v, z_v, sem0, sem1, sem2, *tail_refs):
        wid = lax.axis_index("s") * nc + lax.axis_index("c")

        def move(b, n, idx0, idx1, rows):
            loads = (pltpu.async_copy(pos_hbm.at[pl.ds(b, n)], idx0, sem0),
                     pltpu.async_copy(pos_hbm.at[pl.ds(t_all + b, n)], idx1, sem1),
                     pltpu.async_copy(x_hbm.at[pl.ds(b, n)], rows, sem2))
            for c in loads:
                c.wait()
            stores = (pltpu.async_copy(rows, xs_hbm.at[idx0], sem0), pltpu.async_copy(rows, xs_hbm.at[idx1], sem1))
            for c in stores:
                c.wait()

        @pl.loop(0, n_ch)
        def _(j):
            move(pl.multiple_of(wid * per_w + j * SC_CHUNK, 8), SC_CHUNK, i0_v, i1_v, rows_v)

        if tail:
            move(pl.multiple_of(wid * per_w + n_ch * SC_CHUNK, 8), tail, *tail_refs)

        pltpu.sync_copy(z_hbm, z_v)

        @pl.loop(0, n_pch)
        def _(j):
            b = pl.multiple_of(wid * pad_per_w + j * SC_PAD_CHUNK, 8)
            pltpu.sync_copy(pad_hbm.at[pl.ds(b, SC_PAD_CHUNK)], ip_v)
            pltpu.sync_copy(z_v, xs_hbm.at[ip_v])

    xs = k(pos_flat, pad_rows, x_tiles.reshape(t_all, tile_rows, LANES), zeros)
    return xs.reshape(n_sorted_rows * tile_rows, LANES)


def _sc_gather_call(pos_flat, outs, tile_rows):
    n_assign = pos_flat.shape[0]
    dt = outs.dtype
    nc, nw = _sc_workers()
    per_w = n_assign // nw
    chunk = SC_GATHER_CHUNK
    n_ch = per_w // chunk
    tail = per_w - n_ch * chunk
    assert per_w * nw == n_assign and n_ch % 2 == 0 and tail % 8 == 0
    tail_scratch = [] if tail == 0 else [pltpu.VMEM((tail,), jnp.int32), pltpu.VMEM((tail, tile_rows, LANES), dt)]

    @functools.partial(
        pl.kernel, mesh=plsc.VectorSubcoreMesh(core_axis_name="c", subcore_axis_name="s"),
        out_type=jax.ShapeDtypeStruct((n_assign, tile_rows, LANES), dt),
        scratch_types=[pltpu.VMEM((chunk,), jnp.int32), pltpu.VMEM((chunk,), jnp.int32),
                       pltpu.VMEM((chunk, tile_rows, LANES), dt), pltpu.VMEM((chunk, tile_rows, LANES), dt),
                       pltpu.SemaphoreType.DMA, pltpu.SemaphoreType.DMA] + tail_scratch,
    )
    def k(pos_hbm, o_hbm, y_hbm, idx_a, idx_b, rows_a, rows_b, sem_a, sem_b, *tail_refs):
        wid = lax.axis_index("s") * nc + lax.axis_index("c")

        def row0(j):
            return pl.multiple_of(wid * per_w + j * chunk, 8)

        def start(j, idx, rows, sem):
            pltpu.sync_copy(pos_hbm.at[pl.ds(row0(j), chunk)], idx)
            pltpu.async_copy(o_hbm.at[idx], rows, sem)

        def finish(j, idx, rows, sem):
            pltpu.make_async_copy(o_hbm.at[idx], rows, sem).wait()
            pltpu.sync_copy(rows, y_hbm.at[pl.ds(row0(j), chunk)])

        start(0, idx_a, rows_a, sem_a)

        @pl.loop(0, n_ch // 2)
        def _(p):
            j = 2 * p
            start(j + 1, idx_b, rows_b, sem_b)
            finish(j, idx_a, rows_a, sem_a)

            @pl.when(j + 2 < n_ch)
            def _():
                start(j + 2, idx_a, rows_a, sem_a)

            finish(j + 1, idx_b, rows_b, sem_b)

        if tail:
            idx_t, rows_t = tail_refs
            b = pl.multiple_of(wid * per_w + n_ch * chunk, 8)
            pltpu.sync_copy(pos_hbm.at[pl.ds(b, tail)], idx_t)
            pltpu.sync_copy(o_hbm.at[idx_t], rows_t)
            pltpu.sync_copy(rows_t, y_hbm.at[pl.ds(b, tail)])

    y = k(pos_flat, outs.reshape(-1, tile_rows, LANES))
    return y.reshape(n_assign * tile_rows, LANES)


def _combine_dense_kernel(alpha, n_main_tiles, x1t_ref, y0_ref, y1_ref, w_ref, g_ref, b_ref, out_ref, *tail):
    j = pl.program_id(0)
    w = w_ref[...].T
    y = (w[:, 0:1] * _load_packed_tiles(y0_ref, TOK_TILE, F32)
         + w[:, 1:2] * _load_packed_tiles(y1_ref, TOK_TILE, F32))
    res = _layer_norm(alpha * x1t_ref[...] + y, g_ref[...], b_ref[...])
    if n_main_tiles is None:
        out_ref[...] = res
    else:
        @pl.when(j < n_main_tiles)
        def _():
            out_ref[...] = res

        @pl.when(j >= n_main_tiles)
        def _():
            tail[0][...] = res


def _combine_dense_call(x1t, y_pair, w_col, g, b, alpha, n_main_tiles=None):
    t_all = x1t.shape[0]
    n_tiles = t_all // TOK_TILE
    if n_main_tiles is None:
        out_specs = pl.BlockSpec((TOK_TILE, D_MODEL), lambda i: (i, 0))
        out_shape = jax.ShapeDtypeStruct((t_all, D_MODEL), F32)
    else:
        assert n_tiles == n_main_tiles + 1
        out_specs = [pl.BlockSpec((TOK_TILE, D_MODEL), lambda i: (jnp.minimum(i, n_main_tiles - 1), 0)),
                     pl.BlockSpec((TOK_TILE, D_MODEL), lambda i: (0, 0))]
        out_shape = [jax.ShapeDtypeStruct((n_main_tiles * TOK_TILE, D_MODEL), F32),
                     jax.ShapeDtypeStruct((TOK_TILE, D_MODEL), F32)]
    tile = (TOK_TILE, D_MODEL)
    packed = (TOK_TILE * PACK_ROWS, LANES)
    return pl.pallas_call(
        functools.partial(_combine_dense_kernel, alpha, n_main_tiles),
        grid=(n_tiles,),
        in_specs=[
            pl.BlockSpec(tile, lambda i: (i, 0)),
            pl.BlockSpec(packed, lambda i: (i, 0)),
            pl.BlockSpec(packed, lambda i: (i + n_tiles, 0)),
            pl.BlockSpec((8, TOK_TILE), lambda i: (0, i)),
            pl.BlockSpec((1, D_MODEL), lambda i: (0, 0)),
            pl.BlockSpec((1, D_MODEL), lambda i: (0, 0)),
        ],
        out_specs=out_specs,
        out_shape=out_shape,
        compiler_params=pltpu.CompilerParams(dimension_semantics=("arbitrary",), vmem_limit_bytes=VMEM_LIMIT),
        name="combine",
    )(x1t, y_pair, y_pair, w_col, g, b)


def _rope_tables(seq, dec_seq, past_len):
    half = HEAD_DIM // 2
    inv = ROPE_THETA ** (-jnp.arange(half, dtype=F32) / half)
    pos = jnp.concatenate([jnp.arange(seq), past_len + (jnp.arange(TOK_TILE) % dec_seq)])
    ang = pos.astype(F32)[:, None] * inv[None, :]
    cos = jnp.cos(ang)
    sin = jnp.sin(ang)
    zero = jnp.zeros_like(sin)
    reps = LANES // HEAD_DIM
    cos_t = jnp.tile(jnp.concatenate([cos, cos], -1), (1, reps))
    sa_t = jnp.tile(jnp.concatenate([-sin, zero], -1), (1, reps))
    sb_t = jnp.tile(jnp.concatenate([zero, sin], -1), (1, reps))
    return cos_t, sa_t, sb_t


def _inclusive_cumsum(v):
    n = v.shape[0]
    tri = jnp.arange(n)[None, :] <= jnp.arange(n)[:, None]
    return jnp.sum(jnp.where(tri, v[None, :], 0), axis=1)


def _dispatch_plan(ridx, counts, n_tiles):
    t_all = ridx.shape[1]
    tiles_e = (counts + EXPERT_TILE - 1) // EXPERT_TILE
    tile_end = _inclusive_cumsum(tiles_e)
    offs = (tile_end - tiles_e) * EXPERT_TILE
    n_used = tile_end[-1:]
    tile_expert = jnp.minimum(
        jnp.sum((jnp.arange(n_tiles)[:, None] >= tile_end[None, :]).astype(jnp.int32), axis=1), N_EXPERTS - 1)
    experts = jnp.arange(N_EXPERTS)
    used = tiles_e > 0
    slot_e = (_inclusive_cumsum(used.astype(jnp.int32)) - 1) % 2
    next_e = jnp.min(jnp.where((experts[None, :] > experts[:, None]) & used[None, :], experts[None, :], N_EXPERTS),
                     axis=1)
    tiles = jnp.arange(n_tiles)
    of_tile = tile_expert[:, None] == experts[None, :]
    pick = lambda v: jnp.sum(jnp.where(of_tile, v[None, :], 0), axis=1)
    first = (tiles == pick(tile_end - tiles_e)) & (tiles < n_used)
    tile_plan = first.astype(jnp.int32) + 2 * pick(slot_e) + 4 * pick(next_e)
    pos = ridx[TOP_K:2 * TOP_K] + jnp.sum(
        jnp.where(ridx[:TOP_K, :, None] == experts[None, None, :], offs[None, None, :], 0), axis=-1)
    gap_start = jnp.concatenate([offs + counts, n_used * EXPERT_TILE])
    gap_len = jnp.concatenate([tiles_e * EXPERT_TILE - counts, (n_tiles - n_used) * EXPERT_TILE])
    gap_end_q = _inclusive_cumsum(gap_len)
    q = jnp.arange(n_tiles * EXPERT_TILE - TOP_K * t_all)
    gap = jnp.sum((q[:, None] >= gap_end_q[None, :]).astype(jnp.int32), axis=1)
    in_gap = gap[:, None] == jnp.arange(N_EXPERTS + 1)[None, :]
    pad_rows = q + jnp.sum(jnp.where(in_gap, (gap_start - gap_end_q + gap_len)[None, :], 0), axis=1)
    return (pos.reshape(-1).astype(jnp.int32), pad_rows.astype(jnp.int32),
            tile_expert.astype(jnp.int32), n_used.astype(jnp.int32), tile_plan.astype(jnp.int32))


def kernel(x_prompt, x_sample, cache_k, cache_v, w_in, sinks, gm_ln_g, gm_ln_b, gm_ws, gm_bs,
           out_norm_a, out_norm_b, w_o, ln1_g, ln1_b, w_router, router_bias,
           w_gate, w_up, w_down, ln2_g, ln2_b):
    batch, seq, _ = x_prompt.shape
    dec_batch, dec_seq, _ = x_sample.shape
    depth = w_in.shape[0]
    past_len = PAST_LEN
    assert dec_batch * dec_seq == TOK_TILE and dec_seq == CHUNK and seq % TOK_TILE == 0
    assert cache_k.shape[2] == WINDOW
    alpha = (2 * depth) ** 0.25
    tiles_per_seq = seq // TOK_TILE

    tables = _rope_tables(seq, dec_seq, past_len)
    wrt = w_router.T
    rb = router_bias.reshape(N_EXPERTS, 1)
    cache_k2 = cache_k.reshape(depth * dec_batch, WINDOW, KV_WIDTH)
    cache_v2 = cache_v.reshape(depth * dec_batch, WINDOW, KV_WIDTH)
    as_rows = lambda p: p.reshape(depth, 1, -1)
    lng3, lnb3, na3, nb3 = as_rows(gm_ln_g), as_rows(gm_ln_b), as_rows(out_norm_a), as_rows(out_norm_b)
    l1g3, l1b3, l2g3, l2b3 = as_rows(ln1_g), as_rows(ln1_b), as_rows(ln2_g), as_rows(ln2_b)
    bst3 = jnp.swapaxes(gm_bs, 1, 2)

    streams = [
        dict(x=(x_prompt.reshape(batch * seq, D_MODEL), x_sample.reshape(-1, D_MODEL)),
             n_seq=batch, has_sample=True),
    ]
    kp, vp, ks, vs, gms = [], [], [], [], []
    for l in range(depth):
        layer_args = (sinks[l], w_in, cache_k2, cache_v2, lng3, lnb3, gm_ws, bst3, na3, nb3, w_o, l1g3, l1b3, wrt, rb)
        k_tails, v_tails = [], []
        for st in streams:
            n_prompt = st["n_seq"] * seq
            n_prompt_tiles = n_prompt // TOK_TILE
            t_all = n_prompt + (TOK_TILE if st["has_sample"] else 0)
            n_exp_tiles = (TOP_K * t_all) // EXPERT_TILE + N_EXPERTS
            res = _mixer_call(layer_args, st["x"], tables, n_prompt_tiles, tiles_per_seq, alpha, l)
            if st["has_sample"]:
                x1t, x1p, k_tail, v_tail, k_s, v_s, vn_s, ridx, rw, counts = res
            else:
                x1t, x1p, k_tail, v_tail, ridx, rw, counts = res

            pos, pad_rows, tile_expert, n_used, tile_plan = _dispatch_plan(ridx, counts[:, 0], n_exp_tiles)
            xs = _sc_dispatch_call(pos, pad_rows, x1p, PACK_ROWS, n_exp_tiles * EXPERT_TILE)
            outs = _expert_call(tile_expert, n_used, tile_plan, xs, w_gate, w_up, w_down, l)
            y_pair = _sc_gather_call(pos, outs, PACK_ROWS)
            if l < depth - 1:
                st["x"] = (x1t, y_pair, rw, l2g3, l2b3)
            else:
                st["x"] = _combine_dense_call(x1t, y_pair, rw, l2g3[l], l2b3[l], alpha,
                                              n_main_tiles=n_prompt_tiles if st["has_sample"] else None)

            k_tails.append(k_tail)
            v_tails.append(v_tail)
            if st["has_sample"]:
                ks.append(k_s)
                vs.append(v_s)
                gms.append(vn_s)
        kp.append(jnp.concatenate(k_tails, axis=0))
        vp.append(jnp.concatenate(v_tails, axis=0))

    y_prompt = jnp.concatenate([st["x"][0] if st["has_sample"] else st["x"] for st in streams],
                               axis=0).reshape(batch, seq, D_MODEL)
    y_sample = streams[-1]["x"][1].reshape(dec_batch, dec_seq, D_MODEL)
    heads = (N_KV_HEADS, HEAD_DIM)
    return (y_prompt, y_sample,
            jnp.stack(kp).reshape(depth, batch, WINDOW, *heads), jnp.stack(vp).reshape(depth, batch, WINDOW, *heads),
            jnp.stack(ks).reshape(depth, dec_batch, dec_seq, *heads),
            jnp.stack(vs).reshape(depth, dec_batch, dec_seq, *heads),
            jnp.stack(gms).reshape(depth, dec_batch, dec_seq, GM_WIDTH))
```

```python
import functools

import jax
import jax.numpy as jnp
from jax import lax
from jax.experimental import pallas as pl
from jax.experimental.pallas import tpu as pltpu
from jax.experimental.pallas import tpu_sc as plsc

D_MODEL = 1024
CHUNK = 64
WINDOW = 128
HEAD_DIM = 64
N_Q_HEADS = 8
N_KV_HEADS = 2
Q_PER_KV = N_Q_HEADS // N_KV_HEADS
PAST_LEN = 2048
ATT_WIDTH = N_Q_HEADS * HEAD_DIM
KV_WIDTH = N_KV_HEADS * HEAD_DIM
ROPE_THETA = 10000.0
GM_GROUPS = 4
GM_CH = 128
GM_WIDTH = GM_GROUPS * GM_CH
GM_CHUNK = 128
D_MIX = ATT_WIDTH + GM_WIDTH
D_IN = ATT_WIDTH + 2 * KV_WIDTH + 2 * GM_WIDTH
N_EXPERTS = 16
N_EXPERT_GROUPS = 4
EXPERTS_PER_GROUP = N_EXPERTS // N_EXPERT_GROUPS
TOP_K = 2
D_EXPERT = 512
LN_EPS = 1e-5
NEG_INF = -1e30

LANES = 128
PACK_ROWS = D_MODEL // (2 * LANES)
HI_HALF = 0xFFFF0000
TOK_TILE = 512
EXPERT_TILE = 512
KEYS_PER_CHUNK = WINDOW + CHUNK
VMEM_LIMIT = 60 * 1024 * 1024

F32 = jnp.float32
BF16 = jnp.bfloat16


def _layer_norm(x, g, b):
    mu = jnp.mean(x, axis=-1, keepdims=True)
    d = x - mu
    var = jnp.mean(d * d, axis=-1, keepdims=True)
    return d * lax.rsqrt(var + LN_EPS) * g + b


def _rms_norm(x, g):
    return x * lax.rsqrt(jnp.mean(x * x, axis=-1, keepdims=True) + LN_EPS) * g


def _gelu(x):
    return 0.5 * x * (1.0 + lax.erf(x * (0.5 ** 0.5)))


def _bf16_bits(x):
    return lax.bitcast_convert_type(x.astype(BF16).astype(F32), jnp.uint32)


def _store_packed_tiles(ref, x):
    n = x.shape[0]
    for s in range(PACK_ROWS):
        lo = _bf16_bits(x[:, (2 * s) * LANES:(2 * s + 1) * LANES]) >> 16
        hi = _bf16_bits(x[:, (2 * s + 1) * LANES:(2 * s + 2) * LANES]) & jnp.uint32(HI_HALF)
        ref[pl.ds(s, n, stride=PACK_ROWS), :] = hi | lo


def _load_packed_tiles(ref, n, dtype):
    cols = []
    for s in range(PACK_ROWS):
        w = ref[pl.ds(s, n, stride=PACK_ROWS), :]
        cols.append(lax.bitcast_convert_type(w << 16, F32).astype(dtype))
        cols.append(lax.bitcast_convert_type(w & jnp.uint32(HI_HALF), F32).astype(dtype))
    return jnp.concatenate(cols, axis=-1)


def _first_index_of_max(rows):
    m = rows[0]
    for r in rows[1:]:
        m = jnp.maximum(m, r)
    idx = jnp.full(m.shape, len(rows), jnp.int32)
    for e in reversed(range(len(rows))):
        idx = jnp.where(rows[e] == m, e, idx)
    return m, idx


def _mixer_kernel(n_prompt_tiles, tiles_per_seq, alpha, has_sample, first_layer, sinks_ref, *refs):
    n_x = 2 if first_layer else 6
    x_refs, refs = refs[:n_x], refs[n_x:]
    (w_in_ref, cos_ref, sa_ref, sb_ref, ck_ref, cv_ref, lng_ref, lnb_ref, ws_ref, bst_ref, na_ref, nb_ref, w_o_ref,
     l1g_ref, l1b_ref, wrt_ref, rb_ref, tri_ref, x1t_ref, x1p_ref, kt_ref, vt_ref) = refs[:22]
    rest = refs[22:]
    if has_sample:
        (ks_ref, vs_ref, vn_ref, ridx_ref, rw_ref, counts_ref,
         kd_ref, vd_ref, qm_ref, att_ref, gm_ref, cnt_ref, w_in_s, w_o_s, x_s) = rest
    else:
        (ridx_ref, rw_ref, counts_ref,
         kd_ref, vd_ref, qm_ref, att_ref, gm_ref, cnt_ref, w_in_s, w_o_s, x_s) = rest
    i = pl.program_id(0)
    is_sample = i >= n_prompt_tiles
    seq_start = jnp.logical_and(jnp.logical_not(is_sample), (i % tiles_per_seq) == 0)
    seq_end = jnp.logical_and(jnp.logical_not(is_sample), (i % tiles_per_seq) == tiles_per_seq - 1)
    tm = TOK_TILE

    @pl.when(i == 0)
    def _():
        w_in_s[...] = w_in_ref[0].astype(BF16)
        w_o_s[...] = w_o_ref[0].astype(BF16)

    if first_layer:
        xp_ref, xs_ref = x_refs
        x = jnp.where(is_sample, xs_ref[...], xp_ref[...])
    else:
        x1_prev_ref, y0_ref, y1_ref, wc_ref, l2g_ref, l2b_ref = x_refs
        wc = wc_ref[...].T
        y_moe = (wc[:, 0:1] * _load_packed_tiles(y0_ref, tm, F32)
                 + wc[:, 1:2] * _load_packed_tiles(y1_ref, tm, F32))
        x = _layer_norm(alpha * x1_prev_ref[...] + y_moe, l2g_ref[...], l2b_ref[...])

    x_s[...] = alpha * x
    h = jnp.dot(x.astype(BF16), w_in_s[...], preferred_element_type=F32)

    cos = cos_ref[...]
    sa = sa_ref[...]
    sb = sb_ref[...]

    def rope(blk):
        return blk * cos + pltpu.roll(blk, LANES - HEAD_DIM // 2, 1) * sa + pltpu.roll(blk, HEAD_DIM // 2, 1) * sb

    lane = lax.broadcasted_iota(jnp.int32, (1, LANES), 1)
    lo_half = lane < HEAD_DIM

    o_k = ATT_WIDTH
    o_v = o_k + KV_WIDTH
    o_u = o_v + KV_WIDTH
    o_g = o_u + GM_WIDTH

    k_rot = rope(h[:, o_k:o_v])
    v_new = h[:, o_v:o_u]

    @pl.when(seq_end)
    def _():
        kt_ref[...] = k_rot[tm - WINDOW:]
        vt_ref[...] = v_new[tm - WINDOW:]

    if has_sample:
        @pl.when(is_sample)
        def _():
            ks_ref[...] = k_rot
            vs_ref[...] = v_new

    def dup_heads(a):
        sw = pltpu.roll(a, HEAD_DIM, 1)
        return jnp.where(lo_half, a, sw).astype(BF16), jnp.where(lo_half, sw, a).astype(BF16)

    k_d = dup_heads(k_rot)
    v_d = dup_heads(v_new)

    @pl.when(jnp.logical_not(is_sample))
    def _():
        @pl.when(seq_start)
        def _():
            for g in range(N_KV_HEADS):
                kd_ref[g, 0:WINDOW, :] = jnp.zeros((WINDOW, LANES), BF16)
                vd_ref[g, 0:WINDOW, :] = jnp.zeros((WINDOW, LANES), BF16)

        @pl.when(jnp.logical_not(seq_start))
        def _():
            for g in range(N_KV_HEADS):
                kd_ref[g, 0:WINDOW, :] = kd_ref[g, tm:tm + WINDOW, :]
                vd_ref[g, 0:WINDOW, :] = vd_ref[g, tm:tm + WINDOW, :]

        for g in range(N_KV_HEADS):
            kd_ref[g, WINDOW:WINDOW + tm, :] = k_d[g]
            vd_ref[g, WINDOW:WINDOW + tm, :] = v_d[g]

    @pl.when(is_sample)
    def _():
        for b in range(tm // CHUNK):
            ck = dup_heads(ck_ref[b])
            cv = dup_heads(cv_ref[b])
            base = b * KEYS_PER_CHUNK
            for g in range(N_KV_HEADS):
                kd_ref[g, base:base + WINDOW, :] = ck[g]
                vd_ref[g, base:base + WINDOW, :] = cv[g]
                kd_ref[g, base + WINDOW:base + KEYS_PER_CHUNK, :] = k_d[g][b * CHUNK:(b + 1) * CHUNK]
                vd_ref[g, base + WINDOW:base + KEYS_PER_CHUNK, :] = v_d[g][b * CHUNK:(b + 1) * CHUNK]

    scale = HEAD_DIM ** -0.5
    for b in range(ATT_WIDTH // LANES):
        qb = (rope(h[:, b * LANES:(b + 1) * LANES]) * scale).astype(BF16)
        zero = jnp.zeros_like(qb)
        qm_ref[2 * b] = jnp.where(lo_half, qb, zero)
        qm_ref[2 * b + 1] = jnp.where(lo_half, zero, qb)

    key_stride = jnp.where(is_sample, KEYS_PER_CHUNK, CHUNK)
    col = lax.broadcasted_iota(jnp.int32, (1, KEYS_PER_CHUNK), 1)
    row_blk = lax.broadcasted_iota(jnp.int32, (Q_PER_KV * CHUNK, 1), 0) // CHUNK
    sink_cols = [
        jnp.where(row_blk == 0, sinks_ref[Q_PER_KV * g],
                  jnp.where(row_blk == 1, sinks_ref[Q_PER_KV * g + 1],
                            jnp.where(row_blk == 2, sinks_ref[Q_PER_KV * g + 2], sinks_ref[Q_PER_KV * g + 3])))
        for g in range(N_KV_HEADS)]

    for c in range(tm // CHUNK):
        r0 = c * CHUNK
        k0 = pl.multiple_of(c * key_stride, CHUNK)
        outs = []
        for g in range(N_KV_HEADS):
            q_st = jnp.concatenate([qm_ref[Q_PER_KV * g + r, pl.ds(r0, CHUNK), :] for r in range(Q_PER_KV)],
                                   axis=0)
            keys = kd_ref[g, pl.ds(k0, KEYS_PER_CHUNK), :]
            vals = vd_ref[g, pl.ds(k0, KEYS_PER_CHUNK), :]
            s = lax.dot_general(q_st, keys, (((1,), (1,)), ((), ())), preferred_element_type=F32)
            if c < WINDOW // CHUNK:
                first_valid = jnp.where(seq_start, (WINDOW // CHUNK - c) * CHUNK, 0)
                s = jnp.where(col >= first_valid, s, NEG_INF)
            sink = sink_cols[g]
            m = jnp.maximum(jnp.max(s, axis=-1, keepdims=True), sink)
            p = jnp.exp(s - m)
            denom = jnp.sum(p, axis=-1, keepdims=True) + jnp.exp(sink - m)
            o = jnp.dot(p.astype(BF16), vals, preferred_element_type=F32) * (1.0 / denom)
            for bb in range(2):
                outs.append(jnp.where(lo_half, o[(2 * bb) * CHUNK:(2 * bb + 1) * CHUNK],
                                      o[(2 * bb + 1) * CHUNK:(2 * bb + 2) * CHUNK]))
        for b in range(ATT_WIDTH // LANES):
            att_ref[pl.ds(r0, CHUNK), b * LANES:(b + 1) * LANES] = outs[b]

    prow = lax.broadcasted_iota(jnp.int32, (GM_CHUNK, GM_CHUNK), 0)
    pcol = lax.broadcasted_iota(jnp.int32, (GM_CHUNK, GM_CHUNK), 1)
    half = GM_CHUNK // 2
    tril = pcol <= prow
    same_blk = (prow < half) == (pcol < half)
    prow1 = lax.broadcasted_iota(jnp.int32, (GM_CHUNK, 1), 0)
    bst = bst_ref[...]
    bst_s = jnp.where(prow1 < half, bst, pltpu.roll(bst, half, 0))
    bias = jnp.where(is_sample, bst_s, bst)
    for g in range(GM_GROUPS):
        wg_ = ws_ref[g]
        w_s = jnp.where(prow < half, wg_, pltpu.roll(pltpu.roll(wg_, half, 0), half, 1))
        w_eff = jnp.where(is_sample, jnp.where(same_blk, w_s, 0.0), wg_)
        w_eff = jnp.where(tril, w_eff, 0.0).astype(BF16)
        cols = slice(g * GM_CH, (g + 1) * GM_CH)
        ug = _gelu(h[:, o_u + g * GM_CH:o_u + (g + 1) * GM_CH])
        vn = _layer_norm(_gelu(h[:, o_g + g * GM_CH:o_g + (g + 1) * GM_CH]), lng_ref[:, cols], lnb_ref[:, cols])

        if has_sample:
            @pl.when(is_sample)
            def _(cols=cols, vn=vn):
                vn_ref[:, cols] = vn

        vn_b = vn.astype(BF16)
        for n in range(tm // GM_CHUNK):
            rows = slice(n * GM_CHUNK, (n + 1) * GM_CHUNK)
            s = jnp.dot(w_eff, vn_b[rows], preferred_element_type=F32) + bias[:, g:g + 1]
            gm_ref[rows, cols] = ug[rows] * s

    mixed = jnp.concatenate([_rms_norm(att_ref[...], na_ref[...]), _rms_norm(gm_ref[...], nb_ref[...])], axis=-1)
    y = x_s[...] + jnp.dot(mixed.astype(BF16), w_o_s[...], preferred_element_type=F32)
    x1 = _layer_norm(y, l1g_ref[...], l1b_ref[...])
    x1t_ref[...] = x1
    _store_packed_tiles(x1p_ref, x1)

    nt = (((1,), (1,)), ((), ()))
    wr = wrt_ref[...]
    wr_hi = wr.astype(BF16)
    wr_lo = (wr - wr_hi.astype(F32)).astype(BF16)
    x1_hi = x1.astype(BF16)
    x1_lo = (x1 - x1_hi.astype(F32)).astype(BF16)
    logits = (lax.dot_general(wr_hi, x1_hi, nt, preferred_element_type=F32)
              + (lax.dot_general(wr_hi, x1_lo, nt, preferred_element_type=F32)
                 + lax.dot_general(wr_lo, x1_hi, nt, preferred_element_type=F32)))
    scores = jax.nn.sigmoid(logits)
    sel = scores + rb_ref[...]
    sel_rows = [sel[e:e + 1, :] for e in range(N_EXPERTS)]
    sc_rows = [scores[e:e + 1, :] for e in range(N_EXPERTS)]
    grp = []
    for g in range(N_EXPERT_GROUPS):
        r = sel_rows[g * EXPERTS_PER_GROUP:(g + 1) * EXPERTS_PER_GROUP]
        best_pair = None
        for a in range(EXPERTS_PER_GROUP):
            for b in range(a + 1, EXPERTS_PER_GROUP):
                pair = r[a] + r[b]
                best_pair = pair if best_pair is None else jnp.maximum(best_pair, pair)
        grp.append(best_pair)
    _, best = _first_index_of_max(grp)
    masked = [jnp.where(best == (e // EXPERTS_PER_GROUP), sel_rows[e], NEG_INF) for e in range(N_EXPERTS)]
    _, e0 = _first_index_of_max(masked)
    masked2 = [jnp.where(e0 == e, -jnp.inf, masked[e]) for e in range(N_EXPERTS)]
    _, e1 = _first_index_of_max(masked2)
    w0 = jnp.zeros_like(sc_rows[0])
    w1 = jnp.zeros_like(sc_rows[0])
    for e in range(N_EXPERTS):
        w0 = jnp.where(e0 == e, sc_rows[e], w0)
        w1 = jnp.where(e1 == e, sc_rows[e], w1)
    wsum = w0 + w1
    rw_ref[...] = jnp.concatenate([w0 / wsum, w1 / wsum, jnp.zeros((6, tm), F32)], axis=0)

    @pl.when(i == 0)
    def _():
        cnt_ref[...] = jnp.zeros_like(cnt_ref)

    eid = lax.broadcasted_iota(jnp.int32, (N_EXPERTS, tm), 0)
    oh0 = (eid == e0).astype(F32)
    oh1 = (eid == e1).astype(F32)
    oh = oh0 + oh1
    incl = jnp.dot(oh.astype(BF16), tri_ref[...], preferred_element_type=F32)
    before = cnt_ref[:, 0:1] + incl - oh
    r0 = jnp.sum(oh0 * before, axis=0, keepdims=True).astype(jnp.int32)
    r1 = jnp.sum(oh1 * before, axis=0, keepdims=True).astype(jnp.int32)
    cnt = cnt_ref[...] + incl[:, tm - 1:tm]
    cnt_ref[...] = cnt
    counts_ref[...] = cnt.astype(jnp.int32)
    ridx_ref[...] = jnp.concatenate([e0, e1, r0, r1, jnp.zeros((4, tm), jnp.int32)], axis=0)


def _mixer_call(layer_args, x_all, tables, n_prompt_tiles, tiles_per_seq, alpha, layer):
    (sinks, w_in, ck, cv, lng, lnb, ws, bst, na, nb, w_o, l1g, l1b, wrt, rb) = layer_args
    cos_t, sa_t, sb_t = tables

    def of_layer(shape, which=layer):
        nd = len(shape)
        return pl.BlockSpec((None,) + shape, lambda i, _nd=nd: (which,) + (0,) * _nd)

    first_layer = len(x_all) == 2
    if first_layer:
        x_main, x_tail = x_all
        t_all = x_main.shape[0] + x_tail.shape[0]
        n_tiles = t_all // TOK_TILE
        x_args = [x_main, x_tail]
        x_specs = [pl.BlockSpec((TOK_TILE, D_MODEL), lambda i: (jnp.minimum(i, n_prompt_tiles - 1), 0)),
                   pl.BlockSpec((TOK_TILE, D_MODEL), lambda i: (0, 0))]
    else:
        x1t_prev, y_pair, w_col, l2g, l2b = x_all
        t_all = x1t_prev.shape[0]
        n_tiles = t_all // TOK_TILE
        x_args = [x1t_prev, y_pair, y_pair, w_col, l2g, l2b]
        x_specs = [pl.BlockSpec((TOK_TILE, D_MODEL), lambda i: (i, 0)),
                   pl.BlockSpec((TOK_TILE * PACK_ROWS, LANES), lambda i: (i, 0)),
                   pl.BlockSpec((TOK_TILE * PACK_ROWS, LANES), lambda i: (i + n_tiles, 0)),
                   pl.BlockSpec((8, TOK_TILE), lambda i: (0, i)),
                   of_layer((1, D_MODEL), layer - 1), of_layer((1, D_MODEL), layer - 1)]
    n_seq_tiles = tiles_per_seq
    has_sample = n_tiles > n_prompt_tiles

    def const(shape):
        nd = len(shape)
        return pl.BlockSpec(shape, lambda i, _nd=nd: (0,) * _nd)

    def tab_map(i):
        return (jnp.where(i < n_prompt_tiles, i % n_seq_tiles, n_seq_tiles), 0)

    n_seq = n_prompt_tiles // n_seq_tiles

    def seq_map(i):
        return (jnp.minimum(i // n_seq_tiles, n_seq - 1), 0)

    row_blk = lambda w: pl.BlockSpec((TOK_TILE, w), lambda i: (i, 0))
    in_specs = [
        pl.BlockSpec(memory_space=pltpu.SMEM),
        *x_specs,
        pl.BlockSpec((1, D_MODEL, D_IN), lambda i: (layer, 0, 0)),
        pl.BlockSpec((TOK_TILE, LANES), tab_map),
        pl.BlockSpec((TOK_TILE, LANES), tab_map),
        pl.BlockSpec((TOK_TILE, LANES), tab_map),
        pl.BlockSpec((TOK_TILE // CHUNK, WINDOW, KV_WIDTH), lambda i: (layer, 0, 0)),
        pl.BlockSpec((TOK_TILE // CHUNK, WINDOW, KV_WIDTH), lambda i: (layer, 0, 0)),
        of_layer((1, GM_WIDTH)), of_layer((1, GM_WIDTH)),
        of_layer((GM_GROUPS, GM_CHUNK, GM_CHUNK)), of_layer((GM_CHUNK, GM_GROUPS)),
        of_layer((1, ATT_WIDTH)), of_layer((1, GM_WIDTH)),
        pl.BlockSpec((1, D_MIX, D_MODEL), lambda i: (layer, 0, 0)),
        of_layer((1, D_MODEL)), of_layer((1, D_MODEL)),
        const((N_EXPERTS, D_MODEL)), const((N_EXPERTS, 1)),
        const((TOK_TILE, TOK_TILE)),
    ]
    out_shape = [
        jax.ShapeDtypeStruct((t_all, D_MODEL), F32),
        jax.ShapeDtypeStruct((t_all * PACK_ROWS, LANES), jnp.uint32),
        jax.ShapeDtypeStruct((n_seq * WINDOW, KV_WIDTH), F32),
        jax.ShapeDtypeStruct((n_seq * WINDOW, KV_WIDTH), F32),
        jax.ShapeDtypeStruct((TOK_TILE, KV_WIDTH), F32),
        jax.ShapeDtypeStruct((TOK_TILE, KV_WIDTH), F32),
        jax.ShapeDtypeStruct((TOK_TILE, GM_WIDTH), F32),
        jax.ShapeDtypeStruct((8, t_all), jnp.int32),
        jax.ShapeDtypeStruct((8, t_all), F32),
        jax.ShapeDtypeStruct((N_EXPERTS, LANES), jnp.int32),
    ]
    out_specs = [
        pl.BlockSpec((TOK_TILE, D_MODEL), lambda i: (i, 0)),
        pl.BlockSpec((TOK_TILE * PACK_ROWS, LANES), lambda i: (i, 0)),
        pl.BlockSpec((WINDOW, KV_WIDTH), seq_map), pl.BlockSpec((WINDOW, KV_WIDTH), seq_map),
        const((TOK_TILE, KV_WIDTH)), const((TOK_TILE, KV_WIDTH)),
        const((TOK_TILE, GM_WIDTH)),
        pl.BlockSpec((8, TOK_TILE), lambda i: (0, i)),
        pl.BlockSpec((8, TOK_TILE), lambda i: (0, i)),
        const((N_EXPERTS, LANES)),
    ]
    if not has_sample:
        del out_shape[4:7], out_specs[4:7]
    kd_rows = (TOK_TILE // CHUNK) * KEYS_PER_CHUNK
    scratch = [
        pltpu.VMEM((N_KV_HEADS, kd_rows, LANES), BF16),
        pltpu.VMEM((N_KV_HEADS, kd_rows, LANES), BF16),
        pltpu.VMEM((N_Q_HEADS, TOK_TILE, LANES), BF16),
        pltpu.VMEM((TOK_TILE, ATT_WIDTH), F32),
        pltpu.VMEM((TOK_TILE, GM_WIDTH), F32),
        pltpu.VMEM((N_EXPERTS, LANES), F32),
        pltpu.VMEM((D_MODEL, D_IN), BF16),
        pltpu.VMEM((D_MIX, D_MODEL), BF16),
        pltpu.VMEM((TOK_TILE, D_MODEL), F32),
    ]
    return pl.pallas_call(
        functools.partial(_mixer_kernel, n_prompt_tiles, tiles_per_seq, alpha, has_sample, first_layer),
        grid=(n_tiles,),
        in_specs=in_specs, out_specs=out_specs, out_shape=out_shape,
        scratch_shapes=scratch,
        compiler_params=pltpu.CompilerParams(dimension_semantics=("arbitrary",), vmem_limit_bytes=VMEM_LIMIT),
        name="mixer",
    )(sinks, *x_args, w_in, cos_t, sa_t, sb_t, ck, cv, lng, lnb, ws, bst, na, nb, w_o, l1g, l1b, wrt, rb,
      jnp.triu(jnp.ones((TOK_TILE, TOK_TILE), BF16)))


def _expert_kernel(layer, te_ref, nt_ref, plan_ref, xs_ref, wg_hbm, wu_hbm, wd_hbm, out_ref,
                   wg_s, wu_s, wd_s, wg_f, wu_f, wd_f, sems):
    i = pl.program_id(0)
    plan = plan_ref[i]
    first = (plan & 1) == 1
    slot = (plan >> 1) & 1
    nxt = plan >> 2

    def weight_copies(e, s):
        return (pltpu.make_async_copy(wg_hbm.at[layer, e], wg_f.at[s], sems.at[s, 0]),
                pltpu.make_async_copy(wu_hbm.at[layer, e], wu_f.at[s], sems.at[s, 1]),
                pltpu.make_async_copy(wd_hbm.at[layer, e], wd_f.at[s], sems.at[s, 2]))

    @pl.when(i == 0)
    def _():
        for c in weight_copies(te_ref[0], slot):
            c.start()

    @pl.when(first)
    def _():
        for c in weight_copies(te_ref[i], slot):
            c.wait()

        @pl.when(nxt < N_EXPERTS)
        def _():
            for c in weight_copies(nxt, 1 - slot):
                c.start()

        wg_s[...] = wg_f[slot].astype(BF16)
        wu_s[...] = wu_f[slot].astype(BF16)
        wd_s[...] = wd_f[slot].astype(BF16)

    @pl.when(i < nt_ref[0])
    def _():
        xs = _load_packed_tiles(xs_ref, EXPERT_TILE, BF16)
        g = jnp.dot(xs, wg_s[...], preferred_element_type=F32)
        u = jnp.dot(xs, wu_s[...], preferred_element_type=F32)
        hmid = (g * jax.nn.sigmoid(g)) * u
        _store_packed_tiles(out_ref, jnp.dot(hmid.astype(BF16), wd_s[...], preferred_element_type=F32))

    @pl.when(i >= nt_ref[0])
    def _():
        out_ref[...] = jnp.zeros_like(out_ref)


def _expert_call(tile_expert, n_used, tile_plan, xs, wg, wu, wd, layer):
    n_tiles = xs.shape[0] // (EXPERT_TILE * PACK_ROWS)
    tile_shape = (EXPERT_TILE * PACK_ROWS, LANES)
    grid_spec = pltpu.PrefetchScalarGridSpec(
        num_scalar_prefetch=3,
        grid=(n_tiles,),
        in_specs=[pl.BlockSpec(tile_shape, lambda i, te, nt, plan: (jnp.minimum(i, nt[0] - 1), 0))]
        + [pl.BlockSpec(memory_space=pl.ANY)] * 3,
        out_specs=pl.BlockSpec(tile_shape, lambda i, te, nt, plan: (i, 0)),
        scratch_shapes=[
            pltpu.VMEM((D_MODEL, D_EXPERT), BF16),
            pltpu.VMEM((D_MODEL, D_EXPERT), BF16),
            pltpu.VMEM((D_EXPERT, D_MODEL), BF16),
            pltpu.VMEM((2, D_MODEL, D_EXPERT), F32),
            pltpu.VMEM((2, D_MODEL, D_EXPERT), F32),
            pltpu.VMEM((2, D_EXPERT, D_MODEL), F32),
            pltpu.SemaphoreType.DMA((2, 3)),
        ],
    )
    return pl.pallas_call(
        functools.partial(_expert_kernel, layer),
        grid_spec=grid_spec,
        out_shape=jax.ShapeDtypeStruct(xs.shape, jnp.uint32),
        compiler_params=pltpu.CompilerParams(dimension_semantics=("arbitrary",), vmem_limit_bytes=VMEM_LIMIT),
        name="experts",
    )(tile_expert, n_used, tile_plan, xs, wg, wu, wd)


SC_CHUNK = 128
SC_PAD_CHUNK = 64
SC_GATHER_CHUNK = 64


def _sc_workers():
    info = plsc.get_sparse_core_info()
    return info.num_cores, info.num_cores * info.num_subcores


def _sc_split(per_w):
    n_full = per_w // SC_CHUNK
    tail = per_w - n_full * SC_CHUNK
    assert tail % 8 == 0
    return n_full, tail


def _sc_dispatch_call(pos_flat, pad_rows, x_tiles, tile_rows, n_sorted_rows):
    t_all = x_tiles.shape[0] // tile_rows
    dt = x_tiles.dtype
    nc, nw = _sc_workers()
    per_w = t_all // nw
    n_ch, tail = _sc_split(per_w)
    n_pad = pad_rows.shape[0]
    pad_per_w = n_pad // nw
    n_pch = pad_per_w // SC_PAD_CHUNK
    assert per_w * nw == t_all and n_pch * SC_PAD_CHUNK * nw == n_pad
    zeros = jnp.zeros((SC_PAD_CHUNK, tile_rows, LANES), dt)
    tail_scratch = [] if tail == 0 else [
        pltpu.VMEM((tail,), jnp.int32), pltpu.VMEM((tail,), jnp.int32), pltpu.VMEM((tail, tile_rows, LANES), dt)]

    @functools.partial(
        pl.kernel, mesh=plsc.VectorSubcoreMesh(core_axis_name="c", subcore_axis_name="s"),
        out_type=jax.ShapeDtypeStruct((n_sorted_rows, tile_rows, LANES), dt),
        scratch_types=[
            pltpu.VMEM((SC_CHUNK,), jnp.int32), pltpu.VMEM((SC_CHUNK,), jnp.int32),
            pltpu.VMEM((SC_CHUNK, tile_rows, LANES), dt),
            pltpu.VMEM((SC_PAD_CHUNK,), jnp.int32),
            pltpu.VMEM((SC_PAD_CHUNK, tile_rows, LANES), dt),
            pltpu.SemaphoreType.DMA, pltpu.SemaphoreType.DMA, pltpu.SemaphoreType.DMA,
        ] + tail_scratch,
    )
    def k(pos_hbm, pad_hbm, x_hbm, z_hbm, xs_hbm, i0_v, i1_v, rows_v, ip_v, z_v, sem0, sem1, sem2, *tail_refs):
        wid = lax.axis_index("s") * nc + lax.axis_index("c")

        def move(b, n, idx0, idx1, rows):
            loads = (pltpu.async_copy(pos_hbm.at[pl.ds(b, n)], idx0, sem0),
                     pltpu.async_copy(pos_hbm.at[pl.ds(t_all + b, n)], idx1, sem1),
                     pltpu.async_copy(x_hbm.at[pl.ds(b, n)], rows, sem2))
            for c in loads:
                c.wait()
            stores = (pltpu.async_copy(rows, xs_hbm.at[idx0], sem0), pltpu.async_copy(rows, xs_hbm.at[idx1], sem1))
            for c in stores:
                c.wait()

        @pl.loop(0, n_ch)
        def _(j):
            move(pl.multiple_of(wid * per_w + j * SC_CHUNK, 8), SC_CHUNK, i0_v, i1_v, rows_v)

        if tail:
            move(pl.multiple_of(wid * per_w + n_ch * SC_CHUNK, 8), tail, *tail_refs)

        pltpu.sync_copy(z_hbm, z_v)

        @pl.loop(0, n_pch)
        def _(j):
            b = pl.multiple_of(wid * pad_per_w + j * SC_PAD_CHUNK, 8)
            pltpu.sync_copy(pad_hbm.at[pl.ds(b, SC_PAD_CHUNK)], ip_v)
            pltpu.sync_copy(z_v, xs_hbm.at[ip_v])

    xs = k(pos_flat, pad_rows, x_tiles.reshape(t_all, tile_rows, LANES), zeros)
    return xs.reshape(n_sorted_rows * tile_rows, LANES)


def _sc_gather_call(pos_flat, outs, tile_rows):
    n_assign = pos_flat.shape[0]
    dt = outs.dtype
    nc, nw = _sc_workers()
    per_w = n_assign // nw
    chunk = SC_GATHER_CHUNK
    n_ch = per_w // chunk
    tail = per_w - n_ch * chunk
    assert per_w * nw == n_assign and n_ch % 2 == 0 and tail % 8 == 0
    tail_scratch = [] if tail == 0 else [pltpu.VMEM((tail,), jnp.int32), pltpu.VMEM((tail, tile_rows, LANES), dt)]

    @functools.partial(
        pl.kernel, mesh=plsc.VectorSubcoreMesh(core_axis_name="c", subcore_axis_name="s"),
        out_type=jax.ShapeDtypeStruct((n_assign, tile_rows, LANES), dt),
        scratch_types=[pltpu.VMEM((chunk,), jnp.int32), pltpu.VMEM((chunk,), jnp.int32),
                       pltpu.VMEM((chunk, tile_rows, LANES), dt), pltpu.VMEM((chunk, tile_rows, LANES), dt),
                       pltpu.SemaphoreType.DMA, pltpu.SemaphoreType.DMA] + tail_scratch,
    )
    def k(pos_hbm, o_hbm, y_hbm, idx_a, idx_b, rows_a, rows_b, sem_a, sem_b, *tail_refs):
        wid = lax.axis_index("s") * nc + lax.axis_index("c")

        def row0(j):
            return pl.multiple_of(wid * per_w + j * chunk, 8)

        def start(j, idx, rows, sem):
            pltpu.sync_copy(pos_hbm.at[pl.ds(row0(j), chunk)], idx)
            pltpu.async_copy(o_hbm.at[idx], rows, sem)

        def finish(j, idx, rows, sem):
            pltpu.make_async_copy(o_hbm.at[idx], rows, sem).wait()
            pltpu.sync_copy(rows, y_hbm.at[pl.ds(row0(j), chunk)])

        start(0, idx_a, rows_a, sem_a)

        @pl.loop(0, n_ch // 2)
        def _(p):
            j = 2 * p
            start(j + 1, idx_b, rows_b, sem_b)
            finish(j, idx_a, rows_a, sem_a)

            @pl.when(j + 2 < n_ch)
            def _():
                start(j + 2, idx_a, rows_a, sem_a)

            finish(j + 1, idx_b, rows_b, sem_b)

        if tail:
            idx_t, rows_t = tail_refs
            b = pl.multiple_of(wid * per_w + n_ch * chunk, 8)
            pltpu.sync_copy(pos_hbm.at[pl.ds(b, tail)], idx_t)
            pltpu.sync_copy(o_hbm.at[idx_t], rows_t)
            pltpu.sync_copy(rows_t, y_hbm.at[pl.ds(b, tail)])

    y = k(pos_flat, outs.reshape(-1, tile_rows, LANES))
    return y.reshape(n_assign * tile_rows, LANES)


def _combine_dense_kernel(alpha, n_main_tiles, x1t_ref, y0_ref, y1_ref, w_ref, g_ref, b_ref, out_ref, *tail):
    j = pl.program_id(0)
    w = w_ref[...].T
    y = (w[:, 0:1] * _load_packed_tiles(y0_ref, TOK_TILE, F32)
         + w[:, 1:2] * _load_packed_tiles(y1_ref, TOK_TILE, F32))
    res = _layer_norm(alpha * x1t_ref[...] + y, g_ref[...], b_ref[...])
    if n_main_tiles is None:
        out_ref[...] = res
    else:
        @pl.when(j < n_main_tiles)
        def _():
            out_ref[...] = res

        @pl.when(j >= n_main_tiles)
        def _():
            tail[0][...] = res


def _combine_dense_call(x1t, y_pair, w_col, g, b, alpha, n_main_tiles=None):
    t_all = x1t.shape[0]
    n_tiles = t_all // TOK_TILE
    if n_main_tiles is None:
        out_specs = pl.BlockSpec((TOK_TILE, D_MODEL), lambda i: (i, 0))
        out_shape = jax.ShapeDtypeStruct((t_all, D_MODEL), F32)
    else:
        assert n_tiles == n_main_tiles + 1
        out_specs = [pl.BlockSpec((TOK_TILE, D_MODEL), lambda i: (jnp.minimum(i, n_main_tiles - 1), 0)),
                     pl.BlockSpec((TOK_TILE, D_MODEL), lambda i: (0, 0))]
        out_shape = [jax.ShapeDtypeStruct((n_main_tiles * TOK_TILE, D_MODEL), F32),
                     jax.ShapeDtypeStruct((TOK_TILE, D_MODEL), F32)]
    tile = (TOK_TILE, D_MODEL)
    packed = (TOK_TILE * PACK_ROWS, LANES)
    return pl.pallas_call(
        functools.partial(_combine_dense_kernel, alpha, n_main_tiles),
        grid=(n_tiles,),
        in_specs=[
            pl.BlockSpec(tile, lambda i: (i, 0)),
            pl.BlockSpec(packed, lambda i: (i, 0)),
            pl.BlockSpec(packed, lambda i: (i + n_tiles, 0)),
            pl.BlockSpec((8, TOK_TILE), lambda i: (0, i)),
            pl.BlockSpec((1, D_MODEL), lambda i: (0, 0)),
            pl.BlockSpec((1, D_MODEL), lambda i: (0, 0)),
        ],
        out_specs=out_specs,
        out_shape=out_shape,
        compiler_params=pltpu.CompilerParams(dimension_semantics=("arbitrary",), vmem_limit_bytes=VMEM_LIMIT),
        name="combine",
    )(x1t, y_pair, y_pair, w_col, g, b)


def _rope_tables(seq, dec_seq, past_len):
    half = HEAD_DIM // 2
    inv = ROPE_THETA ** (-jnp.arange(half, dtype=F32) / half)
    pos = jnp.concatenate([jnp.arange(seq), past_len + (jnp.arange(TOK_TILE) % dec_seq)])
    ang = pos.astype(F32)[:, None] * inv[None, :]
    cos = jnp.cos(ang)
    sin = jnp.sin(ang)
    zero = jnp.zeros_like(sin)
    reps = LANES // HEAD_DIM
    cos_t = jnp.tile(jnp.concatenate([cos, cos], -1), (1, reps))
    sa_t = jnp.tile(jnp.concatenate([-sin, zero], -1), (1, reps))
    sb_t = jnp.tile(jnp.concatenate([zero, sin], -1), (1, reps))
    return cos_t, sa_t, sb_t


def _inclusive_cumsum(v):
    n = v.shape[0]
    tri = jnp.arange(n)[None, :] <= jnp.arange(n)[:, None]
    return jnp.sum(jnp.where(tri, v[None, :], 0), axis=1)


def _dispatch_plan(ridx, counts, n_tiles):
    t_all = ridx.shape[1]
    tiles_e = (counts + EXPERT_TILE - 1) // EXPERT_TILE
    tile_end = _inclusive_cumsum(tiles_e)
    offs = (tile_end - tiles_e) * EXPERT_TILE
    n_used = tile_end[-1:]
    tile_expert = jnp.minimum(
        jnp.sum((jnp.arange(n_tiles)[:, None] >= tile_end[None, :]).astype(jnp.int32), axis=1), N_EXPERTS - 1)
    experts = jnp.arange(N_EXPERTS)
    used = tiles_e > 0
    slot_e = (_inclusive_cumsum(used.astype(jnp.int32)) - 1) % 2
    next_e = jnp.min(jnp.where((experts[None, :] > experts[:, None]) & used[None, :], experts[None, :], N_EXPERTS),
                     axis=1)
    tiles = jnp.arange(n_tiles)
    of_tile = tile_expert[:, None] == experts[None, :]
    pick = lambda v: jnp.sum(jnp.where(of_tile, v[None, :], 0), axis=1)
    first = (tiles == pick(tile_end - tiles_e)) & (tiles < n_used)
    tile_plan = first.astype(jnp.int32) + 2 * pick(slot_e) + 4 * pick(next_e)
    pos = ridx[TOP_K:2 * TOP_K] + jnp.sum(
        jnp.where(ridx[:TOP_K, :, None] == experts[None, None, :], offs[None, None, :], 0), axis=-1)
    gap_start = jnp.concatenate([offs + counts, n_used * EXPERT_TILE])
    gap_len = jnp.concatenate([tiles_e * EXPERT_TILE - counts, (n_tiles - n_used) * EXPERT_TILE])
    gap_end_q = _inclusive_cumsum(gap_len)
    q = jnp.arange(n_tiles * EXPERT_TILE - TOP_K * t_all)
    gap = jnp.sum((q[:, None] >= gap_end_q[None, :]).astype(jnp.int32), axis=1)
    in_gap = gap[:, None] == jnp.arange(N_EXPERTS + 1)[None, :]
    pad_rows = q + jnp.sum(jnp.where(in_gap, (gap_start - gap_end_q + gap_len)[None, :], 0), axis=1)
    return (pos.reshape(-1).astype(jnp.int32), pad_rows.astype(jnp.int32),
            tile_expert.astype(jnp.int32), n_used.astype(jnp.int32), tile_plan.astype(jnp.int32))


def kernel(x_prompt, x_sample, cache_k, cache_v, w_in, sinks, gm_ln_g, gm_ln_b, gm_ws, gm_bs,
           out_norm_a, out_norm_b, w_o, ln1_g, ln1_b, w_router, router_bias,
           w_gate, w_up, w_down, ln2_g, ln2_b):
    batch, seq, _ = x_prompt.shape
    dec_batch, dec_seq, _ = x_sample.shape
    depth = w_in.shape[0]
    past_len = PAST_LEN
    assert dec_batch * dec_seq == TOK_TILE and dec_seq == CHUNK and seq % TOK_TILE == 0
    assert cache_k.shape[2] == WINDOW
    alpha = (2 * depth) ** 0.25
    tiles_per_seq = seq // TOK_TILE

    tables = _rope_tables(seq, dec_seq, past_len)
    wrt = w_router.T
    rb = router_bias.reshape(N_EXPERTS, 1)
    cache_k2 = cache_k.reshape(depth * dec_batch, WINDOW, KV_WIDTH)
    cache_v2 = cache_v.reshape(depth * dec_batch, WINDOW, KV_WIDTH)
    as_rows = lambda p: p.reshape(depth, 1, -1)
    lng3, lnb3, na3, nb3 = as_rows(gm_ln_g), as_rows(gm_ln_b), as_rows(out_norm_a), as_rows(out_norm_b)
    l1g3, l1b3, l2g3, l2b3 = as_rows(ln1_g), as_rows(ln1_b), as_rows(ln2_g), as_rows(ln2_b)
    bst3 = jnp.swapaxes(gm_bs, 1, 2)

    streams = [
        dict(x=(x_prompt.reshape(batch * seq, D_MODEL), x_sample.reshape(-1, D_MODEL)),
             n_seq=batch, has_sample=True),
    ]
    kp, vp, ks, vs, gms = [], [], [], [], []
    for l in range(depth):
        layer_args = (sinks[l], w_in, cache_k2, cache_v2, lng3, lnb3, gm_ws, bst3, na3, nb3, w_o, l1g3, l1b3, wrt, rb)
        k_tails, v_tails = [], []
        for st in streams:
            n_prompt = st["n_seq"] * seq
            n_prompt_tiles = n_prompt // TOK_TILE
            t_all = n_prompt + (TOK_TILE if st["has_sample"] else 0)
            n_exp_tiles = (TOP_K * t_all) // EXPERT_TILE + N_EXPERTS
            res = _mixer_call(layer_args, st["x"], tables, n_prompt_tiles, tiles_per_seq, alpha, l)
            if st["has_sample"]:
                x1t, x1p, k_tail, v_tail, k_s, v_s, vn_s, ridx, rw, counts = res
            else:
                x1t, x1p, k_tail, v_tail, ridx, rw, counts = res

            pos, pad_rows, tile_expert, n_used, tile_plan = _dispatch_plan(ridx, counts[:, 0], n_exp_tiles)
            xs = _sc_dispatch_call(pos, pad_rows, x1p, PACK_ROWS, n_exp_tiles * EXPERT_TILE)
            outs = _expert_call(tile_expert, n_used, tile_plan, xs, w_gate, w_up, w_down, l)
            y_pair = _sc_gather_call(pos, outs, PACK_ROWS)
            if l < depth - 1:
                st["x"] = (x1t, y_pair, rw, l2g3, l2b3)
            else:
                st["x"] = _combine_dense_call(x1t, y_pair, rw, l2g3[l], l2b3[l], alpha,
                                              n_main_tiles=n_prompt_tiles if st["has_sample"] else None)

            k_tails.append(k_tail)
            v_tails.append(v_tail)
            if st["has_sample"]:
                ks.append(k_s)
                vs.append(v_s)
                gms.append(vn_s)
        kp.append(jnp.concatenate(k_tails, axis=0))
        vp.append(jnp.concatenate(v_tails, axis=0))

    y_prompt = jnp.concatenate([st["x"][0] if st["has_sample"] else st["x"] for st in streams],
                               axis=0).reshape(batch, seq, D_MODEL)
    y_sample = streams[-1]["x"][1].reshape(dec_batch, dec_seq, D_MODEL)
    heads = (N_KV_HEADS, HEAD_DIM)
    return (y_prompt, y_sample,
            jnp.stack(kp).reshape(depth, batch, WINDOW, *heads), jnp.stack(vp).reshape(depth, batch, WINDOW, *heads),
            jnp.stack(ks).reshape(depth, dec_batch, dec_seq, *heads),
            jnp.stack(vs).reshape(depth, dec_batch, dec_seq, *heads),
            jnp.stack(gms).reshape(depth, dec_batch, dec_seq, GM_WIDTH))
```
